```python
import jax, jax.numpy as jnp
from jax import lax
import numpy as np

D_MODEL = 4096
BATCH = 1
SEQ = 8192
DEPTH = 1

N_META = 16
ATT_HEADS = 16
ATT_KV_HEADS = 4
HEAD_DIM = 128
ATT_GROUP = ATT_HEADS // ATT_KV_HEADS
ATT_WIDTH = ATT_HEADS * HEAD_DIM
KV_WIDTH = ATT_KV_HEADS * HEAD_DIM
WINDOW = 128
BLOCK = 128
ROPE_DIMS = HEAD_DIM // 4
ROPE_THETA = 500000.0
RWKV_HEAD = 64
RWKV_WIDTH = D_MODEL // 2
RWKV_HEADS = RWKV_WIDTH // RWKV_HEAD
DECAY_LORA = 96
ICLR_LORA = 96
GATE_LORA = 256
SHIFT_WIDTH = 3 * RWKV_WIDTH + 2 * DECAY_LORA + 2 * ICLR_LORA + GATE_LORA
IN_WIDTH = 2 * D_MODEL + ATT_WIDTH + 2 * KV_WIDTH + SHIFT_WIDTH
D_FF = -(-8 * D_MODEL // (3 * 256)) * 256
RMS_EPS = 1e-6
GN_EPS = 64e-5

kernel_name = 'hybrid_bidir_swa_rwkv7_gated'


def rms_norm(x, w):
    xf = x.astype(jnp.float32)
    y = xf * lax.rsqrt(jnp.mean(xf * xf, axis=-1, keepdims=True) + RMS_EPS)
    return (y * w.astype(jnp.float32)).astype(x.dtype)


def rope_partial(t, pos):
    inv = ROPE_THETA ** (-jnp.arange(0, ROPE_DIMS, 2, dtype=jnp.float32) / ROPE_DIMS)
    ang = pos.astype(jnp.float32)[:, None] * inv[None, :]
    cos = jnp.cos(ang)[None, :, None, :]
    sin = jnp.sin(ang)[None, :, None, :]
    tf = t.astype(jnp.float32)
    x1 = tf[..., :ROPE_DIMS // 2]
    x2 = tf[..., ROPE_DIMS // 2:ROPE_DIMS]
    rot = jnp.concatenate([x1 * cos - x2 * sin, x2 * cos + x1 * sin, tf[..., ROPE_DIMS:]], axis=-1)
    return rot.astype(t.dtype)


def windowed_gqa(q, k, v, sink):
    B, L = q.shape[0], q.shape[1]
    S = L - N_META
    nb = S // BLOCK
    f32 = jnp.float32
    scale = HEAD_DIM ** -0.5
    neg = jnp.array(-jnp.inf, f32)
    pos = jnp.arange(L, dtype=jnp.int32)
    sink_f = sink.astype(f32).reshape(ATT_KV_HEADS, ATT_GROUP)
    qm, qr = q[:, :N_META], q[:, N_META:]
    km, kr = k[:, :N_META], k[:, N_META:]
    vm, vr = v[:, :N_META], v[:, N_META:]

    qb = qr.reshape(B, nb, BLOCK, ATT_KV_HEADS, ATT_GROUP, HEAD_DIM)
    padw = ((0, 0), (BLOCK, BLOCK), (0, 0), (0, 0))
    kp = jnp.pad(kr, padw).reshape(B, nb + 2, BLOCK, ATT_KV_HEADS, HEAD_DIM)
    vp = jnp.pad(vr, padw).reshape(B, nb + 2, BLOCK, ATT_KV_HEADS, HEAD_DIM)
    kband = jnp.concatenate([kp[:, :-2], kp[:, 1:-1], kp[:, 2:]], axis=2)
    vband = jnp.concatenate([vp[:, :-2], vp[:, 1:-1], vp[:, 2:]], axis=2)
    kpos = jnp.pad(pos[N_META:], (BLOCK, BLOCK), constant_values=-(2 ** 30)).reshape(nb + 2, BLOCK)
    kpos_band = jnp.concatenate([kpos[:-2], kpos[1:-1], kpos[2:]], axis=1)
    qpos = pos[N_META:].reshape(nb, BLOCK)
    band_ok = jnp.abs(qpos[:, :, None] - kpos_band[:, None, :]) <= WINDOW
    s_band = jnp.einsum('bnqhgd,bnkhd->bnhgqk', qb, kband).astype(f32) * scale
    s_band = jnp.where(band_ok[None, :, None, None], s_band, neg)
    s_meta = jnp.einsum('bnqhgd,bmhd->bnhgqm', qb, km).astype(f32) * scale
    s_sink = jnp.broadcast_to(sink_f[None, None, :, :, None, None], s_meta.shape[:-1] + (1,))
    p = jax.nn.softmax(jnp.concatenate([s_band, s_meta, s_sink], axis=-1), axis=-1).astype(v.dtype)
    o_real = (jnp.einsum('bnhgqk,bnkhd->bnqhgd', p[..., :3 * BLOCK], vband)
              + jnp.einsum('bnhgqm,bmhd->bnqhgd', p[..., 3 * BLOCK:3 * BLOCK + N_META], vm))
    o_real = o_real.reshape(B, S, ATT_WIDTH)

    qmb = qm.reshape(B, N_META, ATT_KV_HEADS, ATT_GROUP, HEAD_DIM)
    kmk = jnp.concatenate([km, kr[:, :BLOCK]], axis=1)
    vmk = jnp.concatenate([vm, vr[:, :BLOCK]], axis=1)
    ok_m = jnp.abs(pos[:N_META, None] - pos[None, :N_META + BLOCK]) <= WINDOW
    s_m = jnp.einsum('bqhgd,bkhd->bhgqk', qmb, kmk).astype(f32) * scale
    s_m = jnp.where(ok_m[None, None, None], s_m, neg)
    s_ms = jnp.broadcast_to(sink_f[None, :, :, None, None], s_m.shape[:-1] + (1,))
    pm = jax.nn.softmax(jnp.concatenate([s_m, s_ms], axis=-1), axis=-1).astype(v.dtype)
    o_meta = jnp.einsum('bhgqk,bkhd->bqhgd', pm[..., :N_META + BLOCK], vmk).reshape(B, N_META, ATT_WIDTH)
    return jnp.concatenate([o_meta, o_real], axis=1)


def token_shift_centred(u, mu_prev, mu_next):
    prev = jnp.pad(u, ((0, 0), (1, 0), (0, 0)))[:, :-1]
    nxt = jnp.pad(u, ((0, 0), (0, 1), (0, 0)))[:, 1:]
    return u + mu_prev * (prev - u) + mu_next * (nxt - u)


def rwkv7_scan(r, w, k, v, kk, b, reverse):
    B, L, H, N = r.shape

    def step(S, inp):
        r_t, w_t, k_t, v_t, kk_t, b_t = inp
        sa = jnp.einsum('bhij,bhj->bhi', S, kk_t)
        S = S * w_t[:, :, None, :] - sa[..., :, None] * b_t[:, :, None, :] + v_t[..., :, None] * k_t[:, :, None, :]
        return S, jnp.einsum('bhij,bhj->bhi', S, r_t)

    xs = tuple(jnp.moveaxis(t, 1, 0) for t in (r, w, k, v, kk, b))
    S0 = jnp.zeros((B, H, N, N), jnp.float32)
    _, y = lax.scan(step, S0, xs, reverse=reverse)
    return jnp.moveaxis(y, 0, 1)


def rwkv7_bidir(u, mu_prev, mu_next, w0, w2, a0, a2, g2, k_k, k_a, r_k, ln_w, ln_b):
    B, L, _ = u.shape
    C, H, N = RWKV_WIDTH, RWKV_HEADS, RWKV_HEAD
    f32 = jnp.float32
    out_dtype = u.dtype
    u = token_shift_centred(u, mu_prev, mu_next).astype(f32)
    idx = [C, 2 * C, 3 * C, 3 * C + 2 * DECAY_LORA, 3 * C + 2 * DECAY_LORA + 2 * ICLR_LORA]
    r, k, v, wd, ad, gd = jnp.split(u, idx, axis=-1)
    wd = wd.reshape(B, L, 2, DECAY_LORA)
    ad = ad.reshape(B, L, 2, ICLR_LORA)
    w_log = -jax.nn.softplus(-(w0.astype(f32) + jnp.einsum('bldr,drc->bldc', jnp.tanh(wd), w2.astype(f32)))) - 0.5
    decay = jnp.exp(-jnp.exp(w_log))
    a = jax.nn.sigmoid(a0.astype(f32) + jnp.einsum('bldr,drc->bldc', ad, a2.astype(f32)))
    g = jax.nn.sigmoid(gd) @ g2.astype(f32)

    def heads(t):
        return t.reshape(t.shape[:-1] + (H, N))

    kk = heads(k * k_k.astype(f32))
    kk = kk / jnp.maximum(jnp.sqrt(jnp.sum(kk * kk, axis=-1, keepdims=True)), 1e-12)
    k_dir = heads(k[:, :, None, :] * (1.0 + (a - 1.0) * k_a.astype(f32)))
    b_dir = kk[:, :, None] * heads(a)
    decay_h = heads(decay)
    r_h, v_h = heads(r), heads(v)
    y_f = rwkv7_scan(r_h, decay_h[:, :, 0], k_dir[:, :, 0], v_h, kk, b_dir[:, :, 0], reverse=False)
    y_b = rwkv7_scan(r_h, decay_h[:, :, 1], k_dir[:, :, 1], v_h, kk, b_dir[:, :, 1], reverse=True)
    y = y_f + y_b
    mean = jnp.mean(y, axis=-1, keepdims=True)
    var = jnp.mean(jnp.square(y - mean), axis=-1, keepdims=True)
    y = (y - mean) * lax.rsqrt(var + GN_EPS) * heads(ln_w.astype(f32)) + heads(ln_b.astype(f32))
    bonus = jnp.sum(r_h[:, :, None] * k_dir * heads(r_k.astype(f32)), axis=(2, -1))[..., None] * v_h
    out = (y + bonus).reshape(B, L, C) * g
    return out.astype(out_dtype)


def setup_inputs(seed: int = 0) -> dict:
    key = jax.random.key(seed)
    ks = jax.random.split(key, 26)
    f32 = jnp.float32
    C = RWKV_WIDTH

    def nrm(k, shape, scale):
        return jax.random.normal(k, shape, f32) * scale

    decay_base = jnp.linspace(-6.5, -1.5, C, dtype=f32)
    return {
        'x': nrm(ks[0], (BATCH, SEQ, D_MODEL), 1.0),
        'meta_tokens': nrm(ks[1], (N_META, D_MODEL), 1.0),
        'norm_mix_w': 1.0 + nrm(ks[2], (DEPTH, D_MODEL), 0.05),
        'w_in': nrm(ks[3], (DEPTH, D_MODEL, IN_WIDTH), D_MODEL ** -0.5),
        'b_gate': nrm(ks[4], (DEPTH, 2, D_MODEL), 0.1),
        'mu_prev': jax.random.uniform(ks[5], (DEPTH, SHIFT_WIDTH), f32, 0.1, 0.5),
        'mu_next': jax.random.uniform(ks[6], (DEPTH, SHIFT_WIDTH), f32, 0.1, 0.5),
        'dec_w0': decay_base + nrm(ks[7], (DEPTH, 2, C), 0.1),
        'dec_w2': nrm(ks[8], (DEPTH, 2, DECAY_LORA, C), 0.1 * DECAY_LORA ** -0.5),
        'iclr_a0': nrm(ks[9], (DEPTH, 2, C), 0.3),
        'iclr_a2': nrm(ks[10], (DEPTH, 2, ICLR_LORA, C), ICLR_LORA ** -0.5),
        'gate_w2': nrm(ks[11], (DEPTH, GATE_LORA, C), GATE_LORA ** -0.5),
        'k_k': 0.85 + nrm(ks[12], (DEPTH, C), 0.05),
        'k_a': 1.0 + nrm(ks[13], (DEPTH, C), 0.05),
        'r_k': nrm(ks[14], (DEPTH, C), 0.1),
        'ln_x_w': 1.0 + nrm(ks[15], (DEPTH, C), 0.05),
        'ln_x_b': nrm(ks[16], (DEPTH, C), 0.01),
        'attn_sink': nrm(ks[17], (DEPTH, ATT_HEADS), 1.0),
        'w_proj_attn': nrm(ks[18], (DEPTH, ATT_WIDTH, D_MODEL), ATT_WIDTH ** -0.5),
        'w_proj_rwkv': nrm(ks[19], (DEPTH, RWKV_WIDTH, D_MODEL), RWKV_WIDTH ** -0.5),
        'w_out': nrm(ks[20], (DEPTH, D_MODEL, D_MODEL), D_MODEL ** -0.5),
        'norm_ffn_w': 1.0 + nrm(ks[21], (DEPTH, D_MODEL), 0.05),
        'w_ffn_gate': nrm(ks[22], (DEPTH, D_MODEL, D_FF), D_MODEL ** -0.5),
        'w_ffn_up': nrm(ks[23], (DEPTH, D_MODEL, D_FF), D_MODEL ** -0.5),
        'w_ffn_down': nrm(ks[24], (DEPTH, D_FF, D_MODEL), D_FF ** -0.5),
        'norm_final_w': 1.0 + nrm(ks[25], (D_MODEL,), 0.05),
    }


def reference(x, meta_tokens, norm_mix_w, w_in, b_gate, mu_prev, mu_next, dec_w0, dec_w2,
              iclr_a0, iclr_a2, gate_w2, k_k, k_a, r_k, ln_x_w, ln_x_b, attn_sink,
              w_proj_attn, w_proj_rwkv, w_out, norm_ffn_w, w_ffn_gate, w_ffn_up, w_ffn_down,
              norm_final_w):
    B = x.shape[0]
    meta = jnp.broadcast_to(meta_tokens.astype(x.dtype)[None], (B, N_META, D_MODEL))
    h_res = jnp.concatenate([meta, x], axis=1)
    L = h_res.shape[1]
    pos = jnp.arange(L, dtype=jnp.int32)
    split_idx = [2 * D_MODEL, 2 * D_MODEL + ATT_WIDTH, 2 * D_MODEL + ATT_WIDTH + KV_WIDTH,
                 2 * D_MODEL + ATT_WIDTH + 2 * KV_WIDTH]
    for layer in range(DEPTH):
        h = rms_norm(h_res, norm_mix_w[layer])
        z = h @ w_in[layer]
        gate_logits, q, k, v, u = jnp.split(z, split_idx, axis=-1)
        gates = jax.nn.sigmoid(gate_logits.reshape(B, L, 2, D_MODEL) + b_gate[layer])
        q = rope_partial(q.reshape(B, L, ATT_HEADS, HEAD_DIM), pos)
        k = rope_partial(k.reshape(B, L, ATT_KV_HEADS, HEAD_DIM), pos)
        v = v.reshape(B, L, ATT_KV_HEADS, HEAD_DIM)
        o_att = windowed_gqa(q, k, v, attn_sink[layer])
        o_rwkv = rwkv7_bidir(u, mu_prev[layer], mu_next[layer], dec_w0[layer], dec_w2[layer],
                             iclr_a0[layer], iclr_a2[layer], gate_w2[layer], k_k[layer],
                             k_a[layer], r_k[layer], ln_x_w[layer], ln_x_b[layer])
        mixed = (gates[:, :, 0] * (o_att @ w_proj_attn[layer])
                 + gates[:, :, 1] * (o_rwkv @ w_proj_rwkv[layer]))
        h_res = h_res + mixed @ w_out[layer]
        h = rms_norm(h_res, norm_ffn_w[layer])
        ffn = (jax.nn.silu(h @ w_ffn_gate[layer]) * (h @ w_ffn_up[layer])) @ w_ffn_down[layer]
        h_res = h_res + ffn
    y = rms_norm(h_res, norm_final_w)
    return y[:, N_META:]
```

```python
import functools

import jax
import jax.numpy as jnp
from jax import lax
from jax.experimental import pallas as pl
from jax.experimental.pallas import tpu as pltpu

F32 = jnp.float32
BF16 = jnp.bfloat16

N_META = 16
HEAD_DIM = 128
WINDOW = 128
BLOCK = 128
ROPE_DIMS = HEAD_DIM // 4
ROPE_THETA = 500000.0
RWKV_HEAD = 64
RMS_EPS = 1e-6
GN_EPS = 64e-5
LANES = 128
PAD_ROWS = BLOCK
CHUNK = 64
LORA_PAD = 512
VMEM_PHYSICAL = 64 * 1024 * 1024

NN = (((1,), (0,)), ((), ()))
NT = (((1,), (1,)), ((), ()))
TN = (((0,), (0,)), ((), ()))


def _vmem_limit(*block_bytes, temps=0):
    need = 2 * sum(block_bytes) + temps + (4 << 20)
    return int(min(max(need, 16 << 20), VMEM_PHYSICAL - (6 << 20)))


def _params(sem, limit):
    return pltpu.CompilerParams(dimension_semantics=sem, vmem_limit_bytes=limit)


def _pick(n, cands):
    for c in cands:
        if n % c == 0:
            return c
    raise ValueError(f"no tile in {cands} divides {n}")


def _rms(x, w):
    return x * lax.rsqrt(jnp.mean(x * x, axis=-1, keepdims=True) + RMS_EPS) * w


def _norm_ext_kernel(x_ref, meta_ref, w_ref, o_ref, *, nb):
    i = pl.program_id(0)

    @pl.when(i < nb)
    def _():
        o_ref[...] = _rms(x_ref[...], w_ref[...]).astype(o_ref.dtype)

    @pl.when(i == nb)
    def _():
        o_ref[...] = jnp.zeros(o_ref.shape, o_ref.dtype)
        o_ref[PAD_ROWS - N_META:, :] = _rms(meta_ref[...], w_ref[...]).astype(o_ref.dtype)


def _norm_ext(x, meta, w):
    S, D = x.shape
    nb = S // BLOCK
    return pl.pallas_call(
        functools.partial(_norm_ext_kernel, nb=nb),
        grid=(nb + 1,),
        in_specs=[pl.BlockSpec((BLOCK, D), lambda i: (jnp.minimum(i, nb - 1), 0)),
                  pl.BlockSpec((N_META, D), lambda i: (0, 0)),
                  pl.BlockSpec((1, D), lambda i: (0, 0))],
        out_specs=pl.BlockSpec((BLOCK, D), lambda i: (i, 0)),
        out_shape=jax.ShapeDtypeStruct((S + PAD_ROWS, D), BF16),
        compiler_params=_params(("arbitrary",), _vmem_limit(BLOCK * D * 4, BLOCK * D * 2)),
        name="norm_ext",
    )(x, meta, w.reshape(1, D))


def _norm_kernel(x_ref, w_ref, o_ref):
    o_ref[...] = _rms(x_ref[...], w_ref[...]).astype(o_ref.dtype)


def _norm(x, w, out_dtype, tr=256):
    S, D = x.shape
    return pl.pallas_call(
        _norm_kernel,
        grid=(S // tr,),
        in_specs=[pl.BlockSpec((tr, D), lambda i: (i, 0)), pl.BlockSpec((1, D), lambda i: (0, 0))],
        out_specs=pl.BlockSpec((tr, D), lambda i: (i, 0)),
        out_shape=jax.ShapeDtypeStruct((S, D), out_dtype),
        compiler_params=_params(("arbitrary",), _vmem_limit(tr * D * 4, tr * D * 4)),
        name="norm",
    )(x, w.reshape(1, D))


def _dot(a, b, dims=NN):
    return lax.dot_general(a, b, dims, preferred_element_type=F32)


def _mm_kernel(a_ref, b_ref, o_ref):
    o_ref[...] = _dot(a_ref[...], b_ref[...]).astype(o_ref.dtype)


def _mm(a, b, tm, tn, out_dtype):
    M, K = a.shape
    N = b.shape[1]
    ob = jnp.dtype(out_dtype).itemsize
    return pl.pallas_call(
        _mm_kernel,
        grid=(M // tm, N // tn),
        in_specs=[pl.BlockSpec((tm, K), lambda i, j: (i, 0)), pl.BlockSpec((K, tn), lambda i, j: (0, j))],
        out_specs=pl.BlockSpec((tm, tn), lambda i, j: (i, j)),
        out_shape=jax.ShapeDtypeStruct((M, N), out_dtype),
        compiler_params=_params(("arbitrary", "arbitrary"),
                                _vmem_limit(tm * K * 2, K * tn * 2, tm * tn * ob, temps=tm * tn * 4)),
        name="mm_in",
    )(a, b)


def _mm_mix_kernel(a1_ref, b1_ref, a2_ref, b2_ref, z0_ref, z1_ref, bg_ref, o_ref):
    g0 = jax.nn.sigmoid(z0_ref[...] + bg_ref[0:1, :])
    g1 = jax.nn.sigmoid(z1_ref[...] + bg_ref[1:2, :])
    o_ref[...] = (g0 * _dot(a1_ref[...], b1_ref[...]) + g1 * _dot(a2_ref[...], b2_ref[...])).astype(o_ref.dtype)


def _mm_mix(o_att, w_pa, o_rwkv, w_pr, z, b_gate, S, D, tm, tn):
    K1, K2 = o_att.shape[1], o_rwkv.shape[1]
    nj = D // tn
    return pl.pallas_call(
        _mm_mix_kernel,
        grid=(S // tm, nj),
        in_specs=[pl.BlockSpec((tm, K1), lambda i, j: (i, 0)), pl.BlockSpec((K1, tn), lambda i, j: (0, j)),
                  pl.BlockSpec((tm, K2), lambda i, j: (i, 0)), pl.BlockSpec((K2, tn), lambda i, j: (0, j)),
                  pl.BlockSpec((tm, tn), lambda i, j: (i, j)), pl.BlockSpec((tm, tn), lambda i, j: (i, nj + j)),
                  pl.BlockSpec((2, tn), lambda i, j: (0, j))],
        out_specs=pl.BlockSpec((tm, tn), lambda i, j: (i, j)),
        out_shape=jax.ShapeDtypeStruct((S, D), BF16),
        compiler_params=_params(("arbitrary", "arbitrary"),
                                _vmem_limit(tm * K1 * 2, K1 * tn * 2, tm * K2 * 2, K2 * tn * 2,
                                            2 * tm * tn * 4, tm * tn * 2, temps=3 * tm * tn * 4)),
        name="mm_mix",
    )(o_att, w_pa, o_rwkv, w_pr, z, z, b_gate)


def _mm_res_kernel(a_ref, b_ref, r_ref, o_ref, *, nk):
    k = pl.program_id(2)
    part = _dot(a_ref[...], b_ref[...])
    if nk == 1:
        o_ref[...] = r_ref[...] + part
    else:
        @pl.when(k == 0)
        def _():
            o_ref[...] = r_ref[...] + part

        @pl.when(k > 0)
        def _():
            o_ref[...] += part


def _mm_res(a, b, res, tm, tn, tk):
    M, K = a.shape
    N = b.shape[1]
    nk = K // tk
    return pl.pallas_call(
        functools.partial(_mm_res_kernel, nk=nk),
        grid=(M // tm, N // tn, nk),
        in_specs=[pl.BlockSpec((tm, tk), lambda i, j, k: (i, k)), pl.BlockSpec((tk, tn), lambda i, j, k: (k, j)),
                  pl.BlockSpec((tm, tn), lambda i, j, k: (i, j))],
        out_specs=pl.BlockSpec((tm, tn), lambda i, j, k: (i, j)),
        out_shape=jax.ShapeDtypeStruct((M, N), F32),
        compiler_params=_params(("arbitrary", "arbitrary", "arbitrary"),
                                _vmem_limit(tm * tk * 2, tk * tn * 2, 2 * tm * tn * 4, temps=tm * tn * 4)),
        name="mm_res",
    )(a, b, res)


def _mm_glu_kernel(a_ref, bg_ref, bu_ref, o_ref):
    a = a_ref[...]
    gate = _dot(a, bg_ref[...])
    up = _dot(a, bu_ref[...])
    o_ref[...] = (gate * jax.nn.sigmoid(gate) * up).astype(o_ref.dtype)


def _mm_glu(a, bg, bu, tm, tn):
    M, K = a.shape
    N = bg.shape[1]
    return pl.pallas_call(
        _mm_glu_kernel,
        grid=(M // tm, N // tn),
        in_specs=[pl.BlockSpec((tm, K), lambda i, j: (i, 0)), pl.BlockSpec((K, tn), lambda i, j: (0, j)),
                  pl.BlockSpec((K, tn), lambda i, j: (0, j))],
        out_specs=pl.BlockSpec((tm, tn), lambda i, j: (i, j)),
        out_shape=jax.ShapeDtypeStruct((M, N), BF16),
        compiler_params=_params(("arbitrary", "arbitrary"),
                                _vmem_limit(tm * K * 2, 2 * K * tn * 2, tm * tn * 2, temps=3 * tm * tn * 4)),
        name="mm_glu",
    )(a, bg, bu)


def _rope_kernel(z_ref, cos_ref, sin_ref, o_ref, *, n_rope_blocks, heads_per_block, k_heads_last):
    j = pl.program_id(1)
    cos = cos_ref[...]
    sin = sin_ref[...]
    lane = lax.broadcasted_iota(jnp.int32, cos.shape, 1)
    half = ROPE_DIMS // 2
    for h in range(heads_per_block):
        t = z_ref[:, h * HEAD_DIM:(h + 1) * HEAD_DIM]
        up = jnp.concatenate([t[:, half:], t[:, :half]], axis=1)
        down = jnp.concatenate([t[:, HEAD_DIM - half:], t[:, :HEAD_DIM - half]], axis=1)
        partner = jnp.where(lane < half, up, down)
        rot = t * cos + partner * sin
        if h >= k_heads_last:
            rot = jnp.where(j == n_rope_blocks - 1, t, rot)
        o_ref[:, h * HEAD_DIM:(h + 1) * HEAD_DIM] = rot.astype(o_ref.dtype)


def _rope_cast(z, cos_t, sin_t, off_q, width, kv_width):
    Lp = z.shape[0]
    tc = 2 * kv_width
    assert off_q % tc == 0 and width % tc == 0
    nblk = width // tc
    return pl.pallas_call(
        functools.partial(_rope_kernel, n_rope_blocks=nblk, heads_per_block=tc // HEAD_DIM,
                          k_heads_last=kv_width // HEAD_DIM),
        grid=(Lp // BLOCK, nblk),
        in_specs=[pl.BlockSpec((BLOCK, tc), lambda i, j: (i, off_q // tc + j)),
                  pl.BlockSpec((BLOCK, HEAD_DIM), lambda i, j: (i, 0)),
                  pl.BlockSpec((BLOCK, HEAD_DIM), lambda i, j: (i, 0))],
        out_specs=pl.BlockSpec((BLOCK, tc), lambda i, j: (i, j)),
        out_shape=jax.ShapeDtypeStruct((Lp, width), BF16),
        compiler_params=_params(("arbitrary", "arbitrary"), _vmem_limit(BLOCK * tc * 4, BLOCK * tc * 2)),
        name="rope_cast",
    )(z, cos_t, sin_t)


NEG_BIG = -1e30


def _attn_kernel(q_ref, kp_ref, kc_ref, kn_ref, km_ref, vp_ref, vc_ref, vn_ref, vm_ref, sink_ref, o_ref,
                 *, nb, group):
    i = pl.program_id(1)
    scale = HEAD_DIM ** -0.5
    zpad = jnp.zeros((BLOCK - N_META, HEAD_DIM), BF16)
    k_all = jnp.concatenate([kp_ref[...], kc_ref[...], kn_ref[...], km_ref[...], zpad], axis=0)
    v_all = jnp.concatenate([vp_ref[...], vc_ref[...], vn_ref[...], vm_ref[...], zpad], axis=0)
    q_all = jnp.concatenate([q_ref[:, g * HEAD_DIM:(g + 1) * HEAD_DIM] for g in range(group)], axis=0)
    s = _dot(q_all, k_all, NT) * scale
    rows = lax.broadcasted_iota(jnp.int32, s.shape, 0) % BLOCK
    cols = lax.broadcasted_iota(jnp.int32, s.shape, 1)
    rel = cols - BLOCK - rows
    kblk = i - 1 + cols // BLOCK
    band_ok = (cols < 3 * BLOCK) & (kblk >= 0) & (kblk < nb) & (jnp.abs(rel) <= WINDOW)
    meta_ok = (cols >= 3 * BLOCK) & (cols < 3 * BLOCK + N_META)
    s = jnp.where(band_ok | meta_ok, s, NEG_BIG)
    sink = sink_ref[:, 0:1]
    m = jnp.maximum(jnp.max(s, axis=-1, keepdims=True), sink)
    p = jnp.exp(s - m)
    denom = jnp.sum(p, axis=-1, keepdims=True) + jnp.exp(sink - m)
    o = _dot(p.astype(BF16), v_all) / denom
    for g in range(group):
        o_ref[:, g * HEAD_DIM:(g + 1) * HEAD_DIM] = o[g * BLOCK:(g + 1) * BLOCK, :].astype(o_ref.dtype)


def _attention(qkv, sink, S, n_heads, n_kv):
    group = n_heads // n_kv
    nb = S // BLOCK
    qw = group * HEAD_DIM
    kcol = n_heads
    vcol = n_heads + n_kv
    meta_blk = (S + PAD_ROWS - N_META) // N_META
    sink_rows = jnp.broadcast_to(sink.astype(F32).reshape(n_kv, group, 1, 1),
                                 (n_kv, group, BLOCK, LANES)).reshape(n_kv, group * BLOCK, LANES)

    def kv_specs(col0):
        return [pl.BlockSpec((BLOCK, HEAD_DIM), lambda h, i: (jnp.maximum(i - 1, 0), col0 + h)),
                pl.BlockSpec((BLOCK, HEAD_DIM), lambda h, i: (i, col0 + h)),
                pl.BlockSpec((BLOCK, HEAD_DIM), lambda h, i: (jnp.minimum(i + 1, nb - 1), col0 + h)),
                pl.BlockSpec((N_META, HEAD_DIM), lambda h, i: (meta_blk, col0 + h))]

    return pl.pallas_call(
        functools.partial(_attn_kernel, nb=nb, group=group),
        grid=(n_kv, nb),
        in_specs=[pl.BlockSpec((BLOCK, qw), lambda h, i: (i, h))] + kv_specs(kcol) + kv_specs(vcol)
                 + [pl.BlockSpec((None, group * BLOCK, LANES), lambda h, i: (h, 0, 0))],
        out_specs=pl.BlockSpec((BLOCK, qw), lambda h, i: (i, h)),
        out_shape=jax.ShapeDtypeStruct((S, n_heads * HEAD_DIM), BF16),
        compiler_params=_params(("arbitrary", "arbitrary"),
                                _vmem_limit(BLOCK * qw * 2 * 2, 8 * BLOCK * HEAD_DIM * 2,
                                            group * BLOCK * LANES * 4, temps=6 * group * BLOCK * 512 * 4)),
        name="attention",
    )(qkv, qkv, qkv, qkv, qkv, qkv, qkv, qkv, qkv, sink_rows)


def _split3(x):
    hi = x.astype(BF16)
    r1 = x - hi.astype(F32)
    mid = r1.astype(BF16)
    lo = (r1 - mid.astype(F32)).astype(BF16)
    return hi, mid, lo


def _split2(x):
    hi = x.astype(BF16)
    return hi, (x - hi.astype(F32)).astype(BF16)


def _dot_exact_rhs(x, m_bf16, dims=NN, left=False):
    parts = _split3(x)
    if left:
        return sum(_dot(m_bf16, p, dims) for p in parts)
    return sum(_dot(p, m_bf16, dims) for p in parts)


def _dotp(a, b, passes, dims=NN):
    if passes == 1:
        return _dot(a.astype(BF16), b.astype(BF16), dims)
    a_hi, a_lo = _split2(a)
    b_hi, b_lo = _split2(b)
    return _dot(a_hi, b_hi, dims) + (_dot(a_hi, b_lo, dims) + _dot(a_lo, b_hi, dims))


def _head_sum(x, ones_bd):
    outs = []
    for s in range(x.shape[1] // LANES):
        outs.append(_dot_exact_rhs(x[:, s * LANES:(s + 1) * LANES], ones_bd))
    return outs[0] if len(outs) == 1 else jnp.concatenate(outs, axis=1)


def _prep_kernel(r_ref, rp_ref, rn_ref, k_ref, kp_ref, kn_ref, v_ref, vp_ref, vn_ref,
                 lo_ref, lop_ref, lon_ref, gd_ref, gdp_ref, gdn_ref,
                 pc_ref, pl_ref, wl_ref, g2_ref, ones_ref,
                 ro_ref, vo_ref, kko_ref, lwf_ref, lwb_ref, kdf_ref, kdb_ref, bdf_ref, bdb_ref, go_ref,
                 buf_ref, *, n_tiles, tr, lora_tanh_cols):
    i = pl.program_id(0)

    def shifted(main_ref, prev_ref, next_ref, mup, mun):
        w = main_ref.shape[1]
        buf_ref[0:8, 0:w] = prev_ref[...]
        buf_ref[8:8 + tr, 0:w] = main_ref[...]
        buf_ref[8 + tr:16 + tr, 0:w] = next_ref[...]
        x = main_ref[...]
        prev = buf_ref[7:7 + tr, 0:w]
        nxt = buf_ref[9:9 + tr, 0:w]
        return x + mup * (prev - x) + mun * (nxt - x)

    r = shifted(r_ref, rp_ref, rn_ref, pc_ref[0:1, :], pc_ref[1:2, :])
    k = shifted(k_ref, kp_ref, kn_ref, pc_ref[2:3, :], pc_ref[3:4, :])
    v = shifted(v_ref, vp_ref, vn_ref, pc_ref[4:5, :], pc_ref[5:6, :])
    lo = shifted(lo_ref, lop_ref, lon_ref, pl_ref[0:1, 0:LORA_PAD], pl_ref[1:2, 0:LORA_PAD])
    gd = shifted(gd_ref, gdp_ref, gdn_ref, pl_ref[0:1, LORA_PAD:], pl_ref[1:2, LORA_PAD:])

    row = lax.broadcasted_iota(jnp.int32, (tr, 1), 0)
    valid = jnp.where((i < n_tiles - 1) | (row >= tr - N_META), 1.0, 0.0).astype(F32)

    lane = lax.broadcasted_iota(jnp.int32, lo.shape, 1)
    lo_act = jnp.where(lane < lora_tanh_cols, jnp.tanh(lo), lo).astype(BF16)
    dec_f = _dot(lo_act, wl_ref[0])
    dec_b = _dot(lo_act, wl_ref[1])
    apre_f = _dot(lo_act, wl_ref[2])
    apre_b = _dot(lo_act, wl_ref[3])
    g = _dot(jax.nn.sigmoid(gd).astype(BF16), g2_ref[...])

    def log_decay(dec, w0):
        w_log = -jax.nn.softplus(-(w0 + dec)) - 0.5
        return -jnp.exp(w_log)

    a_f = jax.nn.sigmoid(pc_ref[8:9, :] + apre_f)
    a_b = jax.nn.sigmoid(pc_ref[9:10, :] + apre_b)
    kk = k * pc_ref[10:11, :]
    ss = _head_sum(kk * kk, ones_ref[...])
    kk = kk / jnp.maximum(jnp.sqrt(ss), 1e-12) * valid
    k_a = pc_ref[11:12, :]
    kv = k * valid

    ro_ref[...] = r * valid
    vo_ref[...] = v * valid
    kko_ref[...] = kk
    lwf_ref[...] = log_decay(dec_f, pc_ref[6:7, :])
    lwb_ref[...] = log_decay(dec_b, pc_ref[7:8, :])
    kdf_ref[...] = kv * (1.0 + (a_f - 1.0) * k_a)
    kdb_ref[...] = kv * (1.0 + (a_b - 1.0) * k_a)
    bdf_ref[...] = kk * a_f
    bdb_ref[...] = kk * a_b
    go_ref[...] = g


def _rwkv_prep(z, off_r, C, pc, plo, wl, g2, ones_bd, lora_tanh_cols):
    Lp = z.shape[0]
    tr = BLOCK
    ct = _pick(C, (1024, 512))
    n_tiles = Lp // tr
    n8 = Lp // 8
    off_lo = off_r + 3 * C
    off_gd = off_lo + LORA_PAD
    assert off_r % ct == 0 and off_lo % LORA_PAD == 0

    def seg_specs(off, w, with_c):
        cb = off // w

        def col(c):
            return cb + c if with_c else cb
        return [pl.BlockSpec((tr, w), lambda i, c: (i, col(c))),
                pl.BlockSpec((8, w), lambda i, c: ((i * (tr // 8) + n8 - 1) % n8, col(c))),
                pl.BlockSpec((8, w), lambda i, c: (((i + 1) * (tr // 8)) % n8, col(c)))]

    in_specs = (seg_specs(off_r, ct, True) + seg_specs(off_r + C, ct, True) + seg_specs(off_r + 2 * C, ct, True)
                + seg_specs(off_lo, LORA_PAD, False) + seg_specs(off_gd, LORA_PAD, False)
                + [pl.BlockSpec((16, ct), lambda i, c: (0, c)),
                   pl.BlockSpec((8, 2 * LORA_PAD), lambda i, c: (0, 0)),
                   pl.BlockSpec((4, LORA_PAD, ct), lambda i, c: (0, 0, c)),
                   pl.BlockSpec((LORA_PAD, ct), lambda i, c: (0, c)),
                   pl.BlockSpec((LANES, LANES), lambda i, c: (0, 0))])
    out_spec = pl.BlockSpec((tr, ct), lambda i, c: (i, c))
    out_sds = jax.ShapeDtypeStruct((Lp, C), F32)
    return pl.pallas_call(
        functools.partial(_prep_kernel, n_tiles=n_tiles, tr=tr, lora_tanh_cols=lora_tanh_cols),
        grid=(n_tiles, C // ct),
        in_specs=in_specs,
        out_specs=[out_spec] * 10,
        out_shape=[out_sds] * 10,
        scratch_shapes=[pltpu.VMEM((tr + 16, max(ct, LORA_PAD)), F32)],
        compiler_params=_params(("arbitrary", "arbitrary"),
                                _vmem_limit(3 * tr * ct * 4, 2 * tr * LORA_PAD * 4, 4 * LORA_PAD * ct * 2,
                                            LORA_PAD * ct * 2, 10 * tr * ct * 4, temps=24 * tr * ct * 4)),
        name="rwkv_prep",
    )(*([z] * 15), pc, plo, wl, g2, ones_bd)


P_SCORE = 1
P_INV = 3
P_APPLY = 1
P_STATE = 3


def _scan_dir(r, v, al, lw, k, be, h_ref, reverse):
    C = CHUNK
    f32 = F32
    ti = lax.broadcasted_iota(jnp.int32, (C, C), 0)
    si = lax.broadcasted_iota(jnp.int32, (C, C), 1)
    cum_mat = jnp.where((si >= ti) if reverse else (si <= ti), 1.0, 0.0).astype(BF16)
    cl = _dot_exact_rhs(lw, cum_mat, NN, left=True)
    total = cl[0:1, :] if reverse else cl[C - 1:C, :]
    w_incl = jnp.exp(cl)
    w_excl = jnp.exp(cl - lw)
    w_inv = jnp.exp(-cl)
    w_rest = jnp.exp(total - cl)

    lane = lax.broadcasted_iota(jnp.int32, (C, LANES), 1)
    head0 = lane < RWKV_HEAD

    def stack2(x):
        return jnp.concatenate([jnp.where(head0, x, 0.0), jnp.where(head0, 0.0, x)], axis=0)

    a_s = stack2(al * w_excl)
    r_s = stack2(r * w_incl)
    k_s = stack2(k * w_inv)
    b_s = stack2(be * w_inv)
    kh_s = stack2(k * w_rest)
    bh_s = stack2(be * w_rest)
    v_s = stack2(v)

    rr = lax.broadcasted_iota(jnp.int32, (2 * C, 2 * C), 0)
    cc = lax.broadcasted_iota(jnp.int32, (2 * C, 2 * C), 1)
    same = (rr // C) == (cc // C)
    tt, ss = rr % C, cc % C
    strict = same & ((ss > tt) if reverse else (ss < tt))
    incl = same & ((ss >= tt) if reverse else (ss <= tt))
    eye = rr == cc

    sc = _dotp(jnp.concatenate([a_s, r_s], axis=0), jnp.concatenate([k_s, b_s], axis=0), P_SCORE, NT)
    a_ak = jnp.where(strict, sc[0:2 * C, 0:2 * C], 0.0)
    n_ab = jnp.where(strict, sc[0:2 * C, 2 * C:], 0.0)
    a_rk = jnp.where(incl, sc[2 * C:, 0:2 * C], 0.0)
    a_rb = jnp.where(incl, sc[2 * C:, 2 * C:], 0.0)

    t_inv = jnp.where(eye, 1.0, 0.0).astype(f32) - n_ab
    sq = n_ab
    n_round = 1
    while (1 << n_round) < C:
        sq = _dotp(sq, sq, P_INV)
        t_inv = t_inv + _dotp(t_inv, sq, P_INV)
        n_round += 1

    av = _dotp(a_ak, v_s, P_APPLY)
    tp = _dotp(t_inv, jnp.concatenate([av, a_s], axis=1), P_INV)
    p0_s, at_s = tp[:, 0:LANES], tp[:, LANES:]
    rb = _dotp(a_rb, jnp.concatenate([p0_s, at_s], axis=1), P_APPLY)
    y0_s = _dotp(a_rk, v_s, P_APPLY) - rb[:, 0:LANES]
    rh_s = r_s - rb[:, LANES:]
    bt = _dotp(bh_s, jnp.concatenate([at_s, p0_s], axis=1), P_STATE, TN)
    m_mat = jnp.where(eye, jnp.broadcast_to(jnp.exp(total), (LANES, LANES)), 0.0) - bt[:, 0:LANES]
    g_mat = _dotp(kh_s, v_s, P_STATE, TN) - bt[:, LANES:]

    h = h_ref[...]
    y_s = _dotp(rh_s, h, P_STATE) + y0_s
    h_ref[...] = _dotp(m_mat, h, P_STATE) + g_mat
    return y_s[0:C, :] + y_s[C:, :]


def _scan_kernel(rf_ref, vf_ref, af_ref, lwf_ref, kf_ref, bf_ref,
                 rb_ref, vb_ref, ab_ref, lwb_ref, kb_ref, bb_ref,
                 yf_ref, yb_ref, hf_ref, hb_ref):
    @pl.when(pl.program_id(1) == 0)
    def _():
        hf_ref[...] = jnp.zeros(hf_ref.shape, F32)
        hb_ref[...] = jnp.zeros(hb_ref.shape, F32)

    yf_ref[...] = _scan_dir(rf_ref[...], vf_ref[...], af_ref[...], lwf_ref[...], kf_ref[...], bf_ref[...],
                            hf_ref, reverse=False)
    yb_ref[...] = _scan_dir(rb_ref[...], vb_ref[...], ab_ref[...], lwb_ref[...], kb_ref[...], bb_ref[...],
                            hb_ref, reverse=True)


def _rwkv_scan(r, v, kk, lw_f, lw_b, kd_f, kd_b, bd_f, bd_b, S):
    Lp, C = r.shape
    n_real = S // CHUNK
    n_chunks = Lp // CHUNK
    steps = n_real + 1

    def fwd(p, s):
        return ((s + n_real + 1) % n_chunks, p)

    def bwd(p, s):
        return (n_real - s, p)

    spec_f = pl.BlockSpec((CHUNK, LANES), fwd)
    spec_b = pl.BlockSpec((CHUNK, LANES), bwd)
    out_sds = jax.ShapeDtypeStruct((Lp, C), F32)
    return pl.pallas_call(
        _scan_kernel,
        grid=(C // LANES, steps),
        in_specs=[spec_f] * 6 + [spec_b] * 6,
        out_specs=[spec_f, spec_b],
        out_shape=[out_sds, out_sds],
        scratch_shapes=[pltpu.VMEM((LANES, LANES), F32), pltpu.VMEM((LANES, LANES), F32)],
        compiler_params=_params(("arbitrary", "arbitrary"), 32 << 20),
        name="rwkv_scan",
    )(r, v, kk, lw_f, kd_f, bd_f, r, v, kk, lw_b, kd_b, bd_b)


def _post_kernel(yf_ref, yb_ref, r_ref, kdf_ref, kdb_ref, v_ref, g_ref, pq_ref, ones_ref, o_ref):
    ones_bd = ones_ref[...]
    inv_n = 1.0 / RWKV_HEAD
    y = yf_ref[...] + yb_ref[...]
    mean = _head_sum(y, ones_bd) * inv_n
    yc = y - mean
    var = _head_sum(yc * yc, ones_bd) * inv_n
    yn = yc * lax.rsqrt(var + GN_EPS) * pq_ref[0:1, :] + pq_ref[1:2, :]
    bonus = _head_sum(r_ref[...] * (kdf_ref[...] + kdb_ref[...]) * pq_ref[2:3, :], ones_bd) * v_ref[...]
    o_ref[...] = ((yn + bonus) * g_ref[...]).astype(o_ref.dtype)


def _rwkv_post(y_f, y_b, r, kd_f, kd_b, v, g, pq, ones_bd, S):
    C = r.shape[1]
    tr = 256
    ct = _pick(C, (1024, 512))
    spec = pl.BlockSpec((tr, ct), lambda i, c: (i, c))
    return pl.pallas_call(
        _post_kernel,
        grid=(S // tr, C // ct),
        in_specs=[spec] * 7 + [pl.BlockSpec((8, ct), lambda i, c: (0, c)),
                               pl.BlockSpec((LANES, LANES), lambda i, c: (0, 0))],
        out_specs=spec,
        out_shape=jax.ShapeDtypeStruct((S, C), BF16),
        compiler_params=_params(("arbitrary", "arbitrary"),
                                _vmem_limit(7 * tr * ct * 4, tr * ct * 2, temps=16 * tr * ct * 4)),
        name="rwkv_post",
    )(y_f, y_b, r, kd_f, kd_b, v, g, pq, ones_bd)


def _pad_cols(w, width):
    return jnp.pad(w, ((0, 0), (0, width - w.shape[1])))


def kernel(x, meta_tokens, norm_mix_w, w_in, b_gate, mu_prev, mu_next, dec_w0, dec_w2, iclr_a0, iclr_a2,
           gate_w2, k_k, k_a, r_k, ln_x_w, ln_x_b, attn_sink, w_proj_attn, w_proj_rwkv, w_out, norm_ffn_w,
           w_ffn_gate, w_ffn_up, w_ffn_down, norm_final_w):
    B, S, D = x.shape
    assert B == 1 and norm_mix_w.shape[0] == 1, "one sequence, one layer"
    C = k_k.shape[-1]
    AW = w_proj_attn.shape[1]
    n_heads = attn_sink.shape[-1]
    DL, IL, GL = dec_w2.shape[2], iclr_a2.shape[2], gate_w2.shape[1]
    shift_w = mu_prev.shape[-1]
    KVW = (w_in.shape[-1] - 2 * D - AW - shift_w) // 2
    n_kv = KVW // HEAD_DIM
    F = w_ffn_gate.shape[-1]
    lora_w = 2 * DL + 2 * IL
    assert shift_w == 3 * C + lora_w + GL and lora_w <= LORA_PAD and GL <= LORA_PAD
    assert S % BLOCK == 0 and C % LANES == 0 and n_heads * HEAD_DIM == AW
    Lp = S + PAD_ROWS

    off_q = 2 * D
    off_r = off_q + AW + 2 * KVW
    off_lo = off_r + 3 * C
    n_in = off_lo + 2 * LORA_PAD
    tn_in = _pick(n_in, (1024, 512))

    def shift_layout(a):
        return jnp.concatenate([a[..., :3 * C + lora_w], jnp.zeros(a.shape[:-1] + (LORA_PAD - lora_w,), a.dtype),
                                a[..., 3 * C + lora_w:], jnp.zeros(a.shape[:-1] + (LORA_PAD - GL,), a.dtype)], axis=-1)

    w_in0 = w_in[0]
    w_in_b = jnp.concatenate([w_in0[:, :off_r], shift_layout(w_in0[:, off_r:])], axis=1).astype(BF16)
    mu_p, mu_n = shift_layout(mu_prev[0]), shift_layout(mu_next[0])
    zc = jnp.zeros((C,), F32)
    pc = jnp.stack([mu_p[:C], mu_n[:C], mu_p[C:2 * C], mu_n[C:2 * C], mu_p[2 * C:3 * C], mu_n[2 * C:3 * C],
                    dec_w0[0, 0], dec_w0[0, 1], iclr_a0[0, 0], iclr_a0[0, 1], k_k[0], k_a[0], zc, zc, zc, zc])
    plo = jnp.zeros((8, 2 * LORA_PAD), F32).at[0].set(mu_p[3 * C:]).at[1].set(mu_n[3 * C:])
    wl = jnp.zeros((4, LORA_PAD, C), F32)
    wl = wl.at[0, 0:DL].set(dec_w2[0, 0]).at[1, DL:2 * DL].set(dec_w2[0, 1])
    wl = wl.at[2, 2 * DL:2 * DL + IL].set(iclr_a2[0, 0]).at[3, 2 * DL + IL:lora_w].set(iclr_a2[0, 1])
    wl = wl.astype(BF16)
    g2 = jnp.zeros((LORA_PAD, C), F32).at[0:GL].set(gate_w2[0]).astype(BF16)
    pq = jnp.zeros((8, C), F32).at[0].set(ln_x_w[0]).at[1].set(ln_x_b[0]).at[2].set(r_k[0])
    lane = jnp.arange(LANES)
    ones_bd = (lane[:, None] // RWKV_HEAD == lane[None, :] // RWKV_HEAD).astype(BF16)

    f_pad = -(-F // 1024) * 1024
    w_g = _pad_cols(w_ffn_gate[0], f_pad).astype(BF16)
    w_u = _pad_cols(w_ffn_up[0], f_pad).astype(BF16)
    w_d = jnp.pad(w_ffn_down[0], ((0, f_pad - F), (0, 0))).astype(BF16)
    w_pa = w_proj_attn[0].astype(BF16)
    w_pr = w_proj_rwkv[0].astype(BF16)
    w_o = w_out[0].astype(BF16)

    pos = jnp.concatenate([jnp.arange(N_META, N_META + S), jnp.zeros((PAD_ROWS - N_META,), jnp.int32),
                           jnp.arange(N_META)]).astype(F32)
    inv = ROPE_THETA ** (-jnp.arange(0, ROPE_DIMS, 2, dtype=F32) / ROPE_DIMS)
    ang = pos[:, None] * inv[None, :]
    ones_rest = jnp.ones((Lp, HEAD_DIM - ROPE_DIMS), F32)
    cos_t = jnp.concatenate([jnp.cos(ang), jnp.cos(ang), ones_rest], axis=1)
    sin_t = jnp.concatenate([-jnp.sin(ang), jnp.sin(ang), 0.0 * ones_rest], axis=1)

    x2 = x[0]
    h_ext = _norm_ext(x2, meta_tokens.astype(x.dtype), norm_mix_w[0])
    z = _mm(h_ext, w_in_b, _pick(Lp, (640, 128)), tn_in, F32)
    qkv = _rope_cast(z, cos_t, sin_t, off_q, AW + 2 * KVW, KVW)
    o_att = _attention(qkv, attn_sink[0], S, n_heads, n_kv)
    r_s, v_s, kk, lw_f, lw_b, kd_f, kd_b, bd_f, bd_b, g = _rwkv_prep(
        z, off_r, C, pc, plo, wl, g2, ones_bd, 2 * DL)
    y_f, y_b = _rwkv_scan(r_s, v_s, kk, lw_f, lw_b, kd_f, kd_b, bd_f, bd_b, S)
    o_rwkv = _rwkv_post(y_f, y_b, r_s, kd_f, kd_b, v_s, g, pq, ones_bd, S)
    mixed = _mm_mix(o_att, w_pa, o_rwkv, w_pr, z, b_gate[0], S, D, 512, _pick(D, (1024, 512)))
    h1 = _mm_res(mixed, w_o, x2, 512, _pick(D, (1024, 512)), D)
    hn = _norm(h1, norm_ffn_w[0], BF16)
    act = _mm_glu(hn, w_g, w_u, 512, 512)
    h2 = _mm_res(act, w_d, h1, _pick(S, (1024, 512)), 512, f_pad // 2)
    y = _norm(h2, norm_final_w, x.dtype)
    return y[None]
```

```python
import functools

import jax
import jax.numpy as jnp
from jax import lax
from jax.experimental import pallas as pl
from jax.experimental.pallas import tpu as pltpu

F32 = jnp.float32
BF16 = jnp.bfloat16

N_META = 16
HEAD_DIM = 128
WINDOW = 128
BLOCK = 128
ROPE_DIMS = HEAD_DIM // 4
ROPE_THETA = 500000.0
RWKV_HEAD = 64
RMS_EPS = 1e-6
GN_EPS = 64e-5
LANES = 128
PAD_ROWS = BLOCK
CHUNK = 64
LORA_PAD = 512
VMEM_PHYSICAL = 64 * 1024 * 1024

NN = (((1,), (0,)), ((), ()))
NT = (((1,), (1,)), ((), ()))
TN = (((0,), (0,)), ((), ()))


def _vmem_limit(*block_bytes, temps=0):
    need = 2 * sum(block_bytes) + temps + (4 << 20)
    return int(min(max(need, 16 << 20), VMEM_PHYSICAL - (6 << 20)))


def _params(sem, limit):
    return pltpu.CompilerParams(dimension_semantics=sem, vmem_limit_bytes=limit)


def _pick(n, cands):
    for c in cands:
        if n % c == 0:
            return c
    raise ValueError(f"no tile in {cands} divides {n}")


def _rms(x, w):
    return x * lax.rsqrt(jnp.mean(x * x, axis=-1, keepdims=True) + RMS_EPS) * w


def _norm_ext_kernel(x_ref, meta_ref, w_ref, o_ref, *, nb):
    i = pl.program_id(0)

    @pl.when(i < nb)
    def _():
        o_ref[...] = _rms(x_ref[...], w_ref[...]).astype(o_ref.dtype)

    @pl.when(i == nb)
    def _():
        o_ref[...] = jnp.zeros(o_ref.shape, o_ref.dtype)
        o_ref[PAD_ROWS - N_META:, :] = _rms(meta_ref[...], w_ref[...]).astype(o_ref.dtype)


def _norm_ext(x, meta, w):
    S, D = x.shape
    nb = S // BLOCK
    return pl.pallas_call(
        functools.partial(_norm_ext_kernel, nb=nb),
        grid=(nb + 1,),
        in_specs=[pl.BlockSpec((BLOCK, D), lambda i: (jnp.minimum(i, nb - 1), 0)),
                  pl.BlockSpec((N_META, D), lambda i: (0, 0)),
                  pl.BlockSpec((1, D), lambda i: (0, 0))],
        out_specs=pl.BlockSpec((BLOCK, D), lambda i: (i, 0)),
        out_shape=jax.ShapeDtypeStruct((S + PAD_ROWS, D), BF16),
        compiler_params=_params(("arbitrary",), _vmem_limit(BLOCK * D * 4, BLOCK * D * 2)),
        name="norm_ext",
    )(x, meta, w.reshape(1, D))


def _norm_kernel(x_ref, w_ref, o_ref):
    o_ref[...] = _rms(x_ref[...], w_ref[...]).astype(o_ref.dtype)


def _norm(x, w, out_dtype, tr=256):
    S, D = x.shape
    return pl.pallas_call(
        _norm_kernel,
        grid=(S // tr,),
        in_specs=[pl.BlockSpec((tr, D), lambda i: (i, 0)), pl.BlockSpec((1, D), lambda i: (0, 0))],
        out_specs=pl.BlockSpec((tr, D), lambda i: (i, 0)),
        out_shape=jax.ShapeDtypeStruct((S, D), out_dtype),
        compiler_params=_params(("arbitrary",), _vmem_limit(tr * D * 4, tr * D * 4)),
        name="norm",
    )(x, w.reshape(1, D))


def _dot(a, b, dims=NN):
    return lax.dot_general(a, b, dims, preferred_element_type=F32)


def _mm_kernel(a_ref, b_ref, o_ref):
    o_ref[...] = _dot(a_ref[...], b_ref[...]).astype(o_ref.dtype)


def _mm(a, b, tm, tn, out_dtype):
    M, K = a.shape
    N = b.shape[1]
    ob = jnp.dtype(out_dtype).itemsize
    return pl.pallas_call(
        _mm_kernel,
        grid=(M // tm, N // tn),
        in_specs=[pl.BlockSpec((tm, K), lambda i, j: (i, 0)), pl.BlockSpec((K, tn), lambda i, j: (0, j))],
        out_specs=pl.BlockSpec((tm, tn), lambda i, j: (i, j)),
        out_shape=jax.ShapeDtypeStruct((M, N), out_dtype),
        compiler_params=_params(("arbitrary", "arbitrary"),
                                _vmem_limit(tm * K * 2, K * tn * 2, tm * tn * ob, temps=tm * tn * 4)),
        name="mm_in",
    )(a, b)


def _mm_mix_kernel(a1_ref, b1_ref, a2_ref, b2_ref, z0_ref, z1_ref, bg_ref, o_ref):
    g0 = jax.nn.sigmoid(z0_ref[...] + bg_ref[0:1, :])
    g1 = jax.nn.sigmoid(z1_ref[...] + bg_ref[1:2, :])
    o_ref[...] = (g0 * _dot(a1_ref[...], b1_ref[...]) + g1 * _dot(a2_ref[...], b2_ref[...])).astype(o_ref.dtype)


def _mm_mix(o_att, w_pa, o_rwkv, w_pr, z, b_gate, S, D, tm, tn):
    K1, K2 = o_att.shape[1], o_rwkv.shape[1]
    nj = D // tn
    return pl.pallas_call(
        _mm_mix_kernel,
        grid=(S // tm, nj),
        in_specs=[pl.BlockSpec((tm, K1), lambda i, j: (i, 0)), pl.BlockSpec((K1, tn), lambda i, j: (0, j)),
                  pl.BlockSpec((tm, K2), lambda i, j: (i, 0)), pl.BlockSpec((K2, tn), lambda i, j: (0, j)),
                  pl.BlockSpec((tm, tn), lambda i, j: (i, j)), pl.BlockSpec((tm, tn), lambda i, j: (i, nj + j)),
                  pl.BlockSpec((2, tn), lambda i, j: (0, j))],
        out_specs=pl.BlockSpec((tm, tn), lambda i, j: (i, j)),
        out_shape=jax.ShapeDtypeStruct((S, D), BF16),
        compiler_params=_params(("arbitrary", "arbitrary"),
                                _vmem_limit(tm * K1 * 2, K1 * tn * 2, tm * K2 * 2, K2 * tn * 2,
                                            2 * tm * tn * 4, tm * tn * 2, temps=3 * tm * tn * 4)),
        name="mm_mix",
    )(o_att, w_pa, o_rwkv, w_pr, z, z, b_gate)


def _mm_res_kernel(a_ref, b_ref, r_ref, o_ref, *, nk):
    k = pl.program_id(2)
    part = _dot(a_ref[...], b_ref[...])
    if nk == 1:
        o_ref[...] = r_ref[...] + part
    else:
        @pl.when(k == 0)
        def _():
            o_ref[...] = r_ref[...] + part

        @pl.when(k > 0)
        def _():
            o_ref[...] += part


def _mm_res(a, b, res, tm, tn, tk):
    M, K = a.shape
    N = b.shape[1]
    nk = K // tk
    return pl.pallas_call(
        functools.partial(_mm_res_kernel, nk=nk),
        grid=(M // tm, N // tn, nk),
        in_specs=[pl.BlockSpec((tm, tk), lambda i, j, k: (i, k)), pl.BlockSpec((tk, tn), lambda i, j, k: (k, j)),
                  pl.BlockSpec((tm, tn), lambda i, j, k: (i, j))],
        out_specs=pl.BlockSpec((tm, tn), lambda i, j, k: (i, j)),
        out_shape=jax.ShapeDtypeStruct((M, N), F32),
        compiler_params=_params(("arbitrary", "arbitrary", "arbitrary"),
                                _vmem_limit(tm * tk * 2, tk * tn * 2, 2 * tm * tn * 4, temps=tm * tn * 4)),
        name="mm_res",
    )(a, b, res)


def _mm_glu_kernel(a_ref, bg_ref, bu_ref, o_ref):
    a = a_ref[...]
    gate = _dot(a, bg_ref[...])
    up = _dot(a, bu_ref[...])
    o_ref[...] = (gate * jax.nn.sigmoid(gate) * up).astype(o_ref.dtype)


def _mm_glu(a, bg, bu, tm, tn):
    M, K = a.shape
    N = bg.shape[1]
    return pl.pallas_call(
        _mm_glu_kernel,
        grid=(M // tm, N // tn),
        in_specs=[pl.BlockSpec((tm, K), lambda i, j: (i, 0)), pl.BlockSpec((K, tn), lambda i, j: (0, j)),
                  pl.BlockSpec((K, tn), lambda i, j: (0, j))],
        out_specs=pl.BlockSpec((tm, tn), lambda i, j: (i, j)),
        out_shape=jax.ShapeDtypeStruct((M, N), BF16),
        compiler_params=_params(("arbitrary", "arbitrary"),
                                _vmem_limit(tm * K * 2, 2 * K * tn * 2, tm * tn * 2, temps=3 * tm * tn * 4)),
        name="mm_glu",
    )(a, bg, bu)


def _rope_kernel(z_ref, cos_ref, sin_ref, o_ref, *, n_rope_blocks, heads_per_block, k_heads_last):
    j = pl.program_id(1)
    cos = cos_ref[...]
    sin = sin_ref[...]
    lane = lax.broadcasted_iota(jnp.int32, cos.shape, 1)
    half = ROPE_DIMS // 2
    for h in range(heads_per_block):
        t = z_ref[:, h * HEAD_DIM:(h + 1) * HEAD_DIM]
        up = jnp.concatenate([t[:, half:], t[:, :half]], axis=1)
        down = jnp.concatenate([t[:, HEAD_DIM - half:], t[:, :HEAD_DIM - half]], axis=1)
        partner = jnp.where(lane < half, up, down)
        rot = t * cos + partner * sin
        if h >= k_heads_last:
            rot = jnp.where(j == n_rope_blocks - 1, t, rot)
        o_ref[:, h * HEAD_DIM:(h + 1) * HEAD_DIM] = rot.astype(o_ref.dtype)


def _rope_cast(z, cos_t, sin_t, off_q, width, kv_width):
    Lp = z.shape[0]
    tc = 2 * kv_width
    assert off_q % tc == 0 and width % tc == 0
    nblk = width // tc
    return pl.pallas_call(
        functools.partial(_rope_kernel, n_rope_blocks=nblk, heads_per_block=tc // HEAD_DIM,
                          k_heads_last=kv_width // HEAD_DIM),
        grid=(Lp // BLOCK, nblk),
        in_specs=[pl.BlockSpec((BLOCK, tc), lambda i, j: (i, off_q // tc + j)),
                  pl.BlockSpec((BLOCK, HEAD_DIM), lambda i, j: (i, 0)),
                  pl.BlockSpec((BLOCK, HEAD_DIM), lambda i, j: (i, 0))],
        out_specs=pl.BlockSpec((BLOCK, tc), lambda i, j: (i, j)),
        out_shape=jax.ShapeDtypeStruct((Lp, width), BF16),
        compiler_params=_params(("arbitrary", "arbitrary"), _vmem_limit(BLOCK * tc * 4, BLOCK * tc * 2)),
        name="rope_cast",
    )(z, cos_t, sin_t)


NEG_BIG = -1e30


def _attn_kernel(q_ref, kp_ref, kc_ref, kn_ref, km_ref, vp_ref, vc_ref, vn_ref, vm_ref, sink_ref, o_ref,
                 *, nb, group):
    i = pl.program_id(1)
    scale = HEAD_DIM ** -0.5
    zpad = jnp.zeros((BLOCK - N_META, HEAD_DIM), BF16)
    k_all = jnp.concatenate([kp_ref[...], kc_ref[...], kn_ref[...], km_ref[...], zpad], axis=0)
    v_all = jnp.concatenate([vp_ref[...], vc_ref[...], vn_ref[...], vm_ref[...], zpad], axis=0)
    q_all = jnp.concatenate([q_ref[:, g * HEAD_DIM:(g + 1) * HEAD_DIM] for g in range(group)], axis=0)
    s = _dot(q_all, k_all, NT) * scale
    rows = lax.broadcasted_iota(jnp.int32, s.shape, 0) % BLOCK
    cols = lax.broadcasted_iota(jnp.int32, s.shape, 1)
    rel = cols - BLOCK - rows
    kblk = i - 1 + cols // BLOCK
    band_ok = (cols < 3 * BLOCK) & (kblk >= 0) & (kblk < nb) & (jnp.abs(rel) <= WINDOW)
    meta_ok = (cols >= 3 * BLOCK) & (cols < 3 * BLOCK + N_META)
    s = jnp.where(band_ok | meta_ok, s, NEG_BIG)
    sink = sink_ref[:, 0:1]
    m = jnp.maximum(jnp.max(s, axis=-1, keepdims=True), sink)
    p = jnp.exp(s - m)
    denom = jnp.sum(p, axis=-1, keepdims=True) + jnp.exp(sink - m)
    o = _dot(p.astype(BF16), v_all) / denom
    for g in range(group):
        o_ref[:, g * HEAD_DIM:(g + 1) * HEAD_DIM] = o[g * BLOCK:(g + 1) * BLOCK, :].astype(o_ref.dtype)


def _attention(qkv, sink, S, n_heads, n_kv):
    group = n_heads // n_kv
    nb = S // BLOCK
    qw = group * HEAD_DIM
    kcol = n_heads
    vcol = n_heads + n_kv
    meta_blk = (S + PAD_ROWS - N_META) // N_META
    sink_rows = jnp.broadcast_to(sink.astype(F32).reshape(n_kv, group, 1, 1),
                                 (n_kv, group, BLOCK, LANES)).reshape(n_kv, group * BLOCK, LANES)

    def kv_specs(col0):
        return [pl.BlockSpec((BLOCK, HEAD_DIM), lambda h, i: (jnp.maximum(i - 1, 0), col0 + h)),
                pl.BlockSpec((BLOCK, HEAD_DIM), lambda h, i: (i, col0 + h)),
                pl.BlockSpec((BLOCK, HEAD_DIM), lambda h, i: (jnp.minimum(i + 1, nb - 1), col0 + h)),
                pl.BlockSpec((N_META, HEAD_DIM), lambda h, i: (meta_blk, col0 + h))]

    return pl.pallas_call(
        functools.partial(_attn_kernel, nb=nb, group=group),
        grid=(n_kv, nb),
        in_specs=[pl.BlockSpec((BLOCK, qw), lambda h, i: (i, h))] + kv_specs(kcol) + kv_specs(vcol)
                 + [pl.BlockSpec((None, group * BLOCK, LANES), lambda h, i: (h, 0, 0))],
        out_specs=pl.BlockSpec((BLOCK, qw), lambda h, i: (i, h)),
        out_shape=jax.ShapeDtypeStruct((S, n_heads * HEAD_DIM), BF16),
        compiler_params=_params(("arbitrary", "arbitrary"),
                                _vmem_limit(BLOCK * qw * 2 * 2, 8 * BLOCK * HEAD_DIM * 2,
                                            group * BLOCK * LANES * 4, temps=6 * group * BLOCK * 512 * 4)),
        name="attention",
    )(qkv, qkv, qkv, qkv, qkv, qkv, qkv, qkv, qkv, sink_rows)


def _split3(x):
    hi = x.astype(BF16)
    r1 = x - hi.astype(F32)
    mid = r1.astype(BF16)
    lo = (r1 - mid.astype(F32)).astype(BF16)
    return hi, mid, lo


def _split2(x):
    hi = x.astype(BF16)
    return hi, (x - hi.astype(F32)).astype(BF16)


def _dot_exact_rhs(x, m_bf16, dims=NN, left=False):
    parts = _split3(x)
    if left:
        return sum(_dot(m_bf16, p, dims) for p in parts)
    return sum(_dot(p, m_bf16, dims) for p in parts)


def _dotp(a, b, passes, dims=NN):
    if passes == 1:
        return _dot(a.astype(BF16), b.astype(BF16), dims)
    a_hi, a_lo = _split2(a)
    b_hi, b_lo = _split2(b)
    return _dot(a_hi, b_hi, dims) + (_dot(a_hi, b_lo, dims) + _dot(a_lo, b_hi, dims))


def _head_sum(x, ones_bd):
    outs = []
    for s in range(x.shape[1] // LANES):
        outs.append(_dot_exact_rhs(x[:, s * LANES:(s + 1) * LANES], ones_bd))
    return outs[0] if len(outs) == 1 else jnp.concatenate(outs, axis=1)


def _prep_kernel(r_ref, rp_ref, rn_ref, k_ref, kp_ref, kn_ref, v_ref, vp_ref, vn_ref,
                 lo_ref, lop_ref, lon_ref, gd_ref, gdp_ref, gdn_ref,
                 pc_ref, pl_ref, wl_ref, g2_ref, ones_ref,
                 ro_ref, vo_ref, kko_ref, lwf_ref, lwb_ref, kdf_ref, kdb_ref, bdf_ref, bdb_ref, go_ref,
                 buf_ref, *, n_tiles, tr, lora_tanh_cols):
    i = pl.program_id(0)

    def shifted(main_ref, prev_ref, next_ref, mup, mun):
        w = main_ref.shape[1]
        buf_ref[0:8, 0:w] = prev_ref[...]
        buf_ref[8:8 + tr, 0:w] = main_ref[...]
        buf_ref[8 + tr:16 + tr, 0:w] = next_ref[...]
        x = main_ref[...]
        prev = buf_ref[7:7 + tr, 0:w]
        nxt = buf_ref[9:9 + tr, 0:w]
        return x + mup * (prev - x) + mun * (nxt - x)

    r = shifted(r_ref, rp_ref, rn_ref, pc_ref[0:1, :], pc_ref[1:2, :])
    k = shifted(k_ref, kp_ref, kn_ref, pc_ref[2:3, :], pc_ref[3:4, :])
    v = shifted(v_ref, vp_ref, vn_ref, pc_ref[4:5, :], pc_ref[5:6, :])
    lo = shifted(lo_ref, lop_ref, lon_ref, pl_ref[0:1, 0:LORA_PAD], pl_ref[1:2, 0:LORA_PAD])
    gd = shifted(gd_ref, gdp_ref, gdn_ref, pl_ref[0:1, LORA_PAD:], pl_ref[1:2, LORA_PAD:])

    row = lax.broadcasted_iota(jnp.int32, (tr, 1), 0)
    valid = jnp.where((i < n_tiles - 1) | (row >= tr - N_META), 1.0, 0.0).astype(F32)

    lane = lax.broadcasted_iota(jnp.int32, lo.shape, 1)
    lo_act = jnp.where(lane < lora_tanh_cols, jnp.tanh(lo), lo).astype(BF16)
    dec_f = _dot(lo_act, wl_ref[0])
    dec_b = _dot(lo_act, wl_ref[1])
    apre_f = _dot(lo_act, wl_ref[2])
    apre_b = _dot(lo_act, wl_ref[3])
    g = _dot(jax.nn.sigmoid(gd).astype(BF16), g2_ref[...])

    def log_decay(dec, w0):
        w_log = -jax.nn.softplus(-(w0 + dec)) - 0.5
        return -jnp.exp(w_log)

    a_f = jax.nn.sigmoid(pc_ref[8:9, :] + apre_f)
    a_b = jax.nn.sigmoid(pc_ref[9:10, :] + apre_b)
    kk = k * pc_ref[10:11, :]
    ss = _head_sum(kk * kk, ones_ref[...])
    kk = kk / jnp.maximum(jnp.sqrt(ss), 1e-12) * valid
    k_a = pc_ref[11:12, :]
    kv = k * valid

    ro_ref[...] = r * valid
    vo_ref[...] = v * valid
    kko_ref[...] = kk
    lwf_ref[...] = log_decay(dec_f, pc_ref[6:7, :])
    lwb_ref[...] = log_decay(dec_b, pc_ref[7:8, :])
    kdf_ref[...] = kv * (1.0 + (a_f - 1.0) * k_a)
    kdb_ref[...] = kv * (1.0 + (a_b - 1.0) * k_a)
    bdf_ref[...] = kk * a_f
    bdb_ref[...] = kk * a_b
    go_ref[...] = g


def _rwkv_prep(z, off_r, C, pc, plo, wl, g2, ones_bd, lora_tanh_cols):
    Lp = z.shape[0]
    tr = BLOCK
    ct = _pick(C, (1024, 512))
    n_tiles = Lp // tr
    n8 = Lp // 8
    off_lo = off_r + 3 * C
    off_gd = off_lo + LORA_PAD
    assert off_r % ct == 0 and off_lo % LORA_PAD == 0

    def seg_specs(off, w, with_c):
        cb = off // w

        def col(c):
            return cb + c if with_c else cb
        return [pl.BlockSpec((tr, w), lambda i, c: (i, col(c))),
                pl.BlockSpec((8, w), lambda i, c: ((i * (tr // 8) + n8 - 1) % n8, col(c))),
                pl.BlockSpec((8, w), lambda i, c: (((i + 1) * (tr // 8)) % n8, col(c)))]

    in_specs = (seg_specs(off_r, ct, True) + seg_specs(off_r + C, ct, True) + seg_specs(off_r + 2 * C, ct, True)
                + seg_specs(off_lo, LORA_PAD, False) + seg_specs(off_gd, LORA_PAD, False)
                + [pl.BlockSpec((16, ct), lambda i, c: (0, c)),
                   pl.BlockSpec((8, 2 * LORA_PAD), lambda i, c: (0, 0)),
                   pl.BlockSpec((4, LORA_PAD, ct), lambda i, c: (0, 0, c)),
                   pl.BlockSpec((LORA_PAD, ct), lambda i, c: (0, c)),
                   pl.BlockSpec((LANES, LANES), lambda i, c: (0, 0))])
    out_spec = pl.BlockSpec((tr, ct), lambda i, c: (i, c))
    out_sds = jax.ShapeDtypeStruct((Lp, C), F32)
    return pl.pallas_call(
        functools.partial(_prep_kernel, n_tiles=n_tiles, tr=tr, lora_tanh_cols=lora_tanh_cols),
        grid=(n_tiles, C // ct),
        in_specs=in_specs,
        out_specs=[out_spec] * 10,
        out_shape=[out_sds] * 10,
        scratch_shapes=[pltpu.VMEM((tr + 16, max(ct, LORA_PAD)), F32)],
        compiler_params=_params(("arbitrary", "arbitrary"),
                                _vmem_limit(3 * tr * ct * 4, 2 * tr * LORA_PAD * 4, 4 * LORA_PAD * ct * 2,
                                            LORA_PAD * ct * 2, 10 * tr * ct * 4, temps=24 * tr * ct * 4)),
        name="rwkv_prep",
    )(*([z] * 15), pc, plo, wl, g2, ones_bd)


SCAN_PAIRS = 4


def _scan_consts():
    C = CHUNK
    ti = lax.broadcasted_iota(jnp.int32, (C, C), 0)
    si = lax.broadcasted_iota(jnp.int32, (C, C), 1)
    row = lax.broadcasted_iota(jnp.int32, (C, 2 * C), 0)
    col = lax.broadcasted_iota(jnp.int32, (C, 2 * C), 1) % C
    rr = lax.broadcasted_iota(jnp.int32, (LANES, LANES), 0)
    cc = lax.broadcasted_iota(jnp.int32, (LANES, LANES), 1)
    return dict(
        cum_f=jnp.where(si <= ti, 1.0, 0.0).astype(BF16), cum_b=jnp.where(si >= ti, 1.0, 0.0).astype(BF16),
        head0=lax.broadcasted_iota(jnp.int32, (C, LANES), 1) < RWKV_HEAD,
        strict_f=col < row, strict_b=col > row, incl_f=col <= row, incl_b=col >= row,
        eye_sbs=jnp.where(col == row, 1.0, 0.0).astype(F32),
        same_head=(rr // RWKV_HEAD) == (cc // RWKV_HEAD), eye=rr == cc)


def _scan_chains(chains, cst):
    C = CHUNK
    head0, same, eye = cst["head0"], cst["same_head"], cst["eye"]
    n = range(len(chains))
    rev = [c[7] for c in chains]
    r, v, al, lw, k, be, h = ([c[i] for c in chains] for i in range(7))
    strict = [cst["strict_b" if x else "strict_f"] for x in rev]
    incl = [cst["incl_b" if x else "incl_f"] for x in rev]

    def bd(x):
        xb = x.astype(BF16)
        zero = jnp.zeros_like(xb)
        return jnp.concatenate([jnp.where(head0, xb, zero), jnp.where(head0, zero, xb)], axis=0)

    def mm(a, b_bf16, dims=NN):
        return _dot(a.astype(BF16), b_bf16, dims)

    cl = [_dot_exact_rhs(lw[i], cst["cum_b" if rev[i] else "cum_f"], NN, left=True) for i in n]
    total = [cl[i][0:1, :] if rev[i] else cl[i][C - 1:C, :] for i in n]
    a_t = [al[i] * jnp.exp(cl[i] - lw[i]) for i in n]
    r_t = [r[i] * jnp.exp(cl[i]) for i in n]
    w_inv = [jnp.exp(-cl[i]) for i in n]
    w_rest = [jnp.exp(total[i] - cl[i]) for i in n]
    bd_v = [bd(v[i]) for i in n]
    sc = [mm(jnp.concatenate([a_t[i], r_t[i]], axis=0),
             jnp.concatenate([bd(k[i] * w_inv[i]), bd(be[i] * w_inv[i])], axis=0), NT) for i in n]
    a_ak = [jnp.where(strict[i], sc[i][0:C, 0:2 * C], 0.0) for i in n]
    n_ab = [jnp.where(strict[i], sc[i][0:C, 2 * C:], 0.0) for i in n]
    a_rk = [jnp.where(incl[i], sc[i][C:, 0:2 * C], 0.0) for i in n]
    a_rb = [jnp.where(incl[i], sc[i][C:, 2 * C:], 0.0) for i in n]

    t_inv = [cst["eye_sbs"] - n_ab[i] for i in n]
    sq = n_ab
    n_round = 1
    while (1 << n_round) < C:
        sq = [mm(sq[i], bd(sq[i])) for i in n]
        t_inv = [t_inv[i] + mm(t_inv[i], bd(sq[i])) for i in n]
        n_round += 1

    av = [mm(a_ak[i], bd_v[i]) for i in n]
    tp = [mm(t_inv[i], jnp.concatenate([bd(av[i]), bd(a_t[i])], axis=1)) for i in n]
    p0 = [tp[i][:, 0:LANES] for i in n]
    at = [tp[i][:, LANES:] for i in n]
    y0 = [mm(jnp.concatenate([a_rk[i], a_rb[i]], axis=1), jnp.concatenate([bd_v[i], bd(-p0[i])], axis=0)) for i in n]
    rh = [r_t[i] - mm(a_rb[i], bd(at[i])) for i in n]
    bt = [mm(be[i] * w_rest[i], jnp.concatenate([at[i], p0[i]], axis=1).astype(BF16), TN) for i in n]
    kv = [mm(k[i] * w_rest[i], v[i].astype(BF16), TN) for i in n]
    m_mat = [jnp.where(same, jnp.where(eye, jnp.broadcast_to(jnp.exp(total[i]), (LANES, LANES)), 0.0)
                       - bt[i][:, 0:LANES], 0.0) for i in n]
    g_mat = [jnp.where(same, kv[i] - bt[i][:, LANES:], 0.0) for i in n]
    h_b = [h[i].astype(BF16) for i in n]
    y = [mm(rh[i], h_b[i]) + y0[i] for i in n]
    h_new = [mm(m_mat[i], h_b[i]) + g_mat[i] for i in n]
    return list(zip(y, h_new))


def _scan_kernel(rf_ref, vf_ref, af_ref, lwf_ref, kf_ref, bf_ref,
                 rb_ref, vb_ref, ab_ref, lwb_ref, kb_ref, bb_ref,
                 yf_ref, yb_ref, hf_ref, hb_ref, *, pairs):
    @pl.when(pl.program_id(1) == 0)
    def _():
        hf_ref[...] = jnp.zeros(hf_ref.shape, F32)
        hb_ref[...] = jnp.zeros(hb_ref.shape, F32)

    cst = _scan_consts()
    chains = []
    for g in range(pairs):
        sl = slice(g * LANES, (g + 1) * LANES)
        chains.append((rf_ref[:, sl], vf_ref[:, sl], af_ref[:, sl], lwf_ref[:, sl], kf_ref[:, sl], bf_ref[:, sl],
                       hf_ref[g], False))
        chains.append((rb_ref[:, sl], vb_ref[:, sl], ab_ref[:, sl], lwb_ref[:, sl], kb_ref[:, sl], bb_ref[:, sl],
                       hb_ref[g], True))
    res = _scan_chains(chains, cst)
    for g in range(pairs):
        sl = slice(g * LANES, (g + 1) * LANES)
        yf_ref[:, sl], hf_ref[g] = res[2 * g]
        yb_ref[:, sl], hb_ref[g] = res[2 * g + 1]


def _rwkv_scan(r, v, kk, lw_f, lw_b, kd_f, kd_b, bd_f, bd_b, S):
    Lp, C = r.shape
    n_real = S // CHUNK
    n_chunks = Lp // CHUNK
    steps = n_real + 1
    pairs = _pick(C // LANES, (SCAN_PAIRS, 2, 1))
    w = pairs * LANES

    def fwd(p, s):
        return ((s + n_real + 1) % n_chunks, p)

    def bwd(p, s):
        return (n_real - s, p)

    spec_f = pl.BlockSpec((CHUNK, w), fwd)
    spec_b = pl.BlockSpec((CHUNK, w), bwd)
    out_sds = jax.ShapeDtypeStruct((Lp, C), F32)
    return pl.pallas_call(
        functools.partial(_scan_kernel, pairs=pairs),
        grid=(C // w, steps),
        in_specs=[spec_f] * 6 + [spec_b] * 6,
        out_specs=[spec_f, spec_b],
        out_shape=[out_sds, out_sds],
        scratch_shapes=[pltpu.VMEM((pairs, LANES, LANES), F32), pltpu.VMEM((pairs, LANES, LANES), F32)],
        compiler_params=_params(("arbitrary", "arbitrary"), 32 << 20),
        name="rwkv_scan",
    )(r, v, kk, lw_f, kd_f, bd_f, r, v, kk, lw_b, kd_b, bd_b)


def _post_kernel(yf_ref, yb_ref, r_ref, kdf_ref, kdb_ref, v_ref, g_ref, pq_ref, ones_ref, o_ref):
    ones_bd = ones_ref[...]
    inv_n = 1.0 / RWKV_HEAD
    y = yf_ref[...] + yb_ref[...]
    mean = _head_sum(y, ones_bd) * inv_n
    yc = y - mean
    var = _head_sum(yc * yc, ones_bd) * inv_n
    yn = yc * lax.rsqrt(var + GN_EPS) * pq_ref[0:1, :] + pq_ref[1:2, :]
    bonus = _head_sum(r_ref[...] * (kdf_ref[...] + kdb_ref[...]) * pq_ref[2:3, :], ones_bd) * v_ref[...]
    o_ref[...] = ((yn + bonus) * g_ref[...]).astype(o_ref.dtype)


def _rwkv_post(y_f, y_b, r, kd_f, kd_b, v, g, pq, ones_bd, S):
    C = r.shape[1]
    tr = 256
    ct = _pick(C, (1024, 512))
    spec = pl.BlockSpec((tr, ct), lambda i, c: (i, c))
    return pl.pallas_call(
        _post_kernel,
        grid=(S // tr, C // ct),
        in_specs=[spec] * 7 + [pl.BlockSpec((8, ct), lambda i, c: (0, c)),
                               pl.BlockSpec((LANES, LANES), lambda i, c: (0, 0))],
        out_specs=spec,
        out_shape=jax.ShapeDtypeStruct((S, C), BF16),
        compiler_params=_params(("arbitrary", "arbitrary"),
                                _vmem_limit(7 * tr * ct * 4, tr * ct * 2, temps=16 * tr * ct * 4)),
        name="rwkv_post",
    )(y_f, y_b, r, kd_f, kd_b, v, g, pq, ones_bd)


def _pad_cols(w, width):
    return jnp.pad(w, ((0, 0), (0, width - w.shape[1])))


def kernel(x, meta_tokens, norm_mix_w, w_in, b_gate, mu_prev, mu_next, dec_w0, dec_w2, iclr_a0, iclr_a2,
           gate_w2, k_k, k_a, r_k, ln_x_w, ln_x_b, attn_sink, w_proj_attn, w_proj_rwkv, w_out, norm_ffn_w,
           w_ffn_gate, w_ffn_up, w_ffn_down, norm_final_w):
    B, S, D = x.shape
    assert B == 1 and norm_mix_w.shape[0] == 1, "one sequence, one layer"
    C = k_k.shape[-1]
    AW = w_proj_attn.shape[1]
    n_heads = attn_sink.shape[-1]
    DL, IL, GL = dec_w2.shape[2], iclr_a2.shape[2], gate_w2.shape[1]
    shift_w = mu_prev.shape[-1]
    KVW = (w_in.shape[-1] - 2 * D - AW - shift_w) // 2
    n_kv = KVW // HEAD_DIM
    F = w_ffn_gate.shape[-1]
    lora_w = 2 * DL + 2 * IL
    assert shift_w == 3 * C + lora_w + GL and lora_w <= LORA_PAD and GL <= LORA_PAD
    assert S % BLOCK == 0 and C % LANES == 0 and n_heads * HEAD_DIM == AW
    Lp = S + PAD_ROWS

    off_q = 2 * D
    off_r = off_q + AW + 2 * KVW
    off_lo = off_r + 3 * C
    n_in = off_lo + 2 * LORA_PAD
    tn_in = _pick(n_in, (1024, 512))

    def shift_layout(a):
        return jnp.concatenate([a[..., :3 * C + lora_w], jnp.zeros(a.shape[:-1] + (LORA_PAD - lora_w,), a.dtype),
                                a[..., 3 * C + lora_w:], jnp.zeros(a.shape[:-1] + (LORA_PAD - GL,), a.dtype)], axis=-1)

    w_in0 = w_in[0]
    w_in_b = jnp.concatenate([w_in0[:, :off_r], shift_layout(w_in0[:, off_r:])], axis=1).astype(BF16)
    mu_p, mu_n = shift_layout(mu_prev[0]), shift_layout(mu_next[0])
    zc = jnp.zeros((C,), F32)
    pc = jnp.stack([mu_p[:C], mu_n[:C], mu_p[C:2 * C], mu_n[C:2 * C], mu_p[2 * C:3 * C], mu_n[2 * C:3 * C],
                    dec_w0[0, 0], dec_w0[0, 1], iclr_a0[0, 0], iclr_a0[0, 1], k_k[0], k_a[0], zc, zc, zc, zc])
    plo = jnp.zeros((8, 2 * LORA_PAD), F32).at[0].set(mu_p[3 * C:]).at[1].set(mu_n[3 * C:])
    wl = jnp.zeros((4, LORA_PAD, C), F32)
    wl = wl.at[0, 0:DL].set(dec_w2[0, 0]).at[1, DL:2 * DL].set(dec_w2[0, 1])
    wl = wl.at[2, 2 * DL:2 * DL + IL].set(iclr_a2[0, 0]).at[3, 2 * DL + IL:lora_w].set(iclr_a2[0, 1])
    wl = wl.astype(BF16)
    g2 = jnp.zeros((LORA_PAD, C), F32).at[0:GL].set(gate_w2[0]).astype(BF16)
    pq = jnp.zeros((8, C), F32).at[0].set(ln_x_w[0]).at[1].set(ln_x_b[0]).at[2].set(r_k[0])
    lane = jnp.arange(LANES)
    ones_bd = (lane[:, None] // RWKV_HEAD == lane[None, :] // RWKV_HEAD).astype(BF16)

    f_pad = -(-F // 1024) * 1024
    w_g = _pad_cols(w_ffn_gate[0], f_pad).astype(BF16)
    w_u = _pad_cols(w_ffn_up[0], f_pad).astype(BF16)
    w_d = jnp.pad(w_ffn_down[0], ((0, f_pad - F), (0, 0))).astype(BF16)
    w_pa = w_proj_attn[0].astype(BF16)
    w_pr = w_proj_rwkv[0].astype(BF16)
    w_o = w_out[0].astype(BF16)

    pos = jnp.concatenate([jnp.arange(N_META, N_META + S), jnp.zeros((PAD_ROWS - N_META,), jnp.int32),
                           jnp.arange(N_META)]).astype(F32)
    inv = ROPE_THETA ** (-jnp.arange(0, ROPE_DIMS, 2, dtype=F32) / ROPE_DIMS)
    ang = pos[:, None] * inv[None, :]
    ones_rest = jnp.ones((Lp, HEAD_DIM - ROPE_DIMS), F32)
    cos_t = jnp.concatenate([jnp.cos(ang), jnp.cos(ang), ones_rest], axis=1)
    sin_t = jnp.concatenate([-jnp.sin(ang), jnp.sin(ang), 0.0 * ones_rest], axis=1)

    x2 = x[0]
    h_ext = _norm_ext(x2, meta_tokens.astype(x.dtype), norm_mix_w[0])
    z = _mm(h_ext, w_in_b, _pick(Lp, (640, 128)), tn_in, F32)
    qkv = _rope_cast(z, cos_t, sin_t, off_q, AW + 2 * KVW, KVW)
    o_att = _attention(qkv, attn_sink[0], S, n_heads, n_kv)
    r_s, v_s, kk, lw_f, lw_b, kd_f, kd_b, bd_f, bd_b, g = _rwkv_prep(
        z, off_r, C, pc, plo, wl, g2, ones_bd, 2 * DL)
    y_f, y_b = _rwkv_scan(r_s, v_s, kk, lw_f, lw_b, kd_f, kd_b, bd_f, bd_b, S)
    o_rwkv = _rwkv_post(y_f, y_b, r_s, kd_f, kd_b, v_s, g, pq, ones_bd, S)
    mixed = _mm_mix(o_att, w_pa, o_rwkv, w_pr, z, b_gate[0], S, D, 512, _pick(D, (1024, 512)))
    h1 = _mm_res(mixed, w_o, x2, 512, _pick(D, (1024, 512)), D)
    hn = _norm(h1, norm_ffn_w[0], BF16)
    act = _mm_glu(hn, w_g, w_u, 512, 512)
    h2 = _mm_res(act, w_d, h1, _pick(S, (1024, 512)), 512, f_pad // 2)
    y = _norm(h2, norm_final_w, x.dtype)
    return y[None]
```

```python
import functools

import jax
import jax.numpy as jnp
from jax import lax
from jax.experimental import pallas as pl
from jax.experimental.pallas import tpu as pltpu

F32 = jnp.float32
BF16 = jnp.bfloat16

N_META = 16
HEAD_DIM = 128
WINDOW = 128
BLOCK = 128
ROPE_DIMS = HEAD_DIM // 4
ROPE_THETA = 500000.0
RWKV_HEAD = 64
RMS_EPS = 1e-6
GN_EPS = 64e-5
LANES = 128
PAD_ROWS = BLOCK
CHUNK = 64
LORA_PAD = 512
VMEM_PHYSICAL = 64 * 1024 * 1024

NN = (((1,), (0,)), ((), ()))
NT = (((1,), (1,)), ((), ()))
TN = (((0,), (0,)), ((), ()))


def _vmem_limit(*block_bytes, temps=0):
    need = 2 * sum(block_bytes) + temps + (4 << 20)
    return int(min(max(need, 16 << 20), VMEM_PHYSICAL - (6 << 20)))


def _params(sem, limit):
    return pltpu.CompilerParams(dimension_semantics=sem, vmem_limit_bytes=limit)


def _pick(n, cands):
    for c in cands:
        if n % c == 0:
            return c
    raise ValueError(f"no tile in {cands} divides {n}")


def _rms(x, w):
    return x * lax.rsqrt(jnp.mean(x * x, axis=-1, keepdims=True) + RMS_EPS) * w


def _norm_ext_kernel(x_ref, meta_ref, w_ref, o_ref, *, nb):
    i = pl.program_id(0)

    @pl.when(i < nb)
    def _():
        o_ref[...] = _rms(x_ref[...], w_ref[...]).astype(o_ref.dtype)

    @pl.when(i == nb)
    def _():
        o_ref[...] = jnp.zeros(o_ref.shape, o_ref.dtype)
        o_ref[PAD_ROWS - N_META:, :] = _rms(meta_ref[...], w_ref[...]).astype(o_ref.dtype)


def _norm_ext(x, meta, w):
    S, D = x.shape
    nb = S // BLOCK
    return pl.pallas_call(
        functools.partial(_norm_ext_kernel, nb=nb),
        grid=(nb + 1,),
        in_specs=[pl.BlockSpec((BLOCK, D), lambda i: (jnp.minimum(i, nb - 1), 0)),
                  pl.BlockSpec((N_META, D), lambda i: (0, 0)),
                  pl.BlockSpec((1, D), lambda i: (0, 0))],
        out_specs=pl.BlockSpec((BLOCK, D), lambda i: (i, 0)),
        out_shape=jax.ShapeDtypeStruct((S + PAD_ROWS, D), BF16),
        compiler_params=_params(("arbitrary",), _vmem_limit(BLOCK * D * 4, BLOCK * D * 2)),
        name="norm_ext",
    )(x, meta, w.reshape(1, D))


def _norm_kernel(x_ref, w_ref, o_ref):
    o_ref[...] = _rms(x_ref[...], w_ref[...]).astype(o_ref.dtype)


def _norm(x, w, out_dtype, tr=256):
    S, D = x.shape
    return pl.pallas_call(
        _norm_kernel,
        grid=(S // tr,),
        in_specs=[pl.BlockSpec((tr, D), lambda i: (i, 0)), pl.BlockSpec((1, D), lambda i: (0, 0))],
        out_specs=pl.BlockSpec((tr, D), lambda i: (i, 0)),
        out_shape=jax.ShapeDtypeStruct((S, D), out_dtype),
        compiler_params=_params(("arbitrary",), _vmem_limit(tr * D * 4, tr * D * 4)),
        name="norm",
    )(x, w.reshape(1, D))


def _dot(a, b, dims=NN):
    return lax.dot_general(a, b, dims, preferred_element_type=F32)


def _cast_resident(src_ref, dst_ref, col0=None, n_valid=None):
    K = src_ref.shape[0]
    rows = _pick(K, (256, 128))

    def body(c, carry):
        sl = pl.ds(pl.multiple_of(c * rows, rows), rows)
        blk = src_ref[sl, :]
        if n_valid is not None:
            col = col0 + lax.broadcasted_iota(jnp.int32, blk.shape, 1)
            blk = jnp.where(col < n_valid, blk, 0.0)
        dst_ref[sl, :] = blk.astype(dst_ref.dtype)
        return carry

    lax.fori_loop(0, K // rows, body, 0)


def _mm_in_kernel(a_ref, b_ref, o_ref, bw_ref, *, n_valid):
    j, i = pl.program_id(0), pl.program_id(1)

    @pl.when(i == 0)
    def _():
        _cast_resident(b_ref, bw_ref, j * b_ref.shape[1], n_valid)

    o_ref[...] = _dot(a_ref[...], bw_ref[...])


def _mm_in(a, w, tm, tn):
    M, K = a.shape
    N = w.shape[1]
    nj = pl.cdiv(N, tn)
    return pl.pallas_call(
        functools.partial(_mm_in_kernel, n_valid=N),
        grid=(nj, M // tm),
        in_specs=[pl.BlockSpec((tm, K), lambda j, i: (i, 0)), pl.BlockSpec((K, tn), lambda j, i: (0, j))],
        out_specs=pl.BlockSpec((tm, tn), lambda j, i: (i, j)),
        out_shape=jax.ShapeDtypeStruct((M, nj * tn), F32),
        scratch_shapes=[pltpu.VMEM((K, tn), BF16)],
        compiler_params=_params(("arbitrary", "arbitrary"),
                                _vmem_limit(tm * K * 2, K * tn * 4, tm * tn * 4, temps=K * tn * 2 + tm * tn * 4)),
        name="mm_in",
    )(a, w)


def _mm_mix_kernel(a1_ref, b1_ref, a2_ref, b2_ref, z0_ref, z1_ref, bg_ref, o_ref, w1_ref, w2_ref):
    @pl.when(pl.program_id(1) == 0)
    def _():
        _cast_resident(b1_ref, w1_ref)
        _cast_resident(b2_ref, w2_ref)

    g0 = jax.nn.sigmoid(z0_ref[...] + bg_ref[0:1, :])
    g1 = jax.nn.sigmoid(z1_ref[...] + bg_ref[1:2, :])
    o_ref[...] = (g0 * _dot(a1_ref[...], w1_ref[...]) + g1 * _dot(a2_ref[...], w2_ref[...])).astype(o_ref.dtype)


def _mm_mix(o_att, w_pa, o_rwkv, w_pr, z, b_gate, S, D, tm, tn):
    K1, K2 = o_att.shape[1], o_rwkv.shape[1]
    nj = D // tn
    return pl.pallas_call(
        _mm_mix_kernel,
        grid=(nj, S // tm),
        in_specs=[pl.BlockSpec((tm, K1), lambda j, i: (i, 0)), pl.BlockSpec((K1, tn), lambda j, i: (0, j)),
                  pl.BlockSpec((tm, K2), lambda j, i: (i, 0)), pl.BlockSpec((K2, tn), lambda j, i: (0, j)),
                  pl.BlockSpec((tm, tn), lambda j, i: (i, j)), pl.BlockSpec((tm, tn), lambda j, i: (i, nj + j)),
                  pl.BlockSpec((2, tn), lambda j, i: (0, j))],
        out_specs=pl.BlockSpec((tm, tn), lambda j, i: (i, j)),
        out_shape=jax.ShapeDtypeStruct((S, D), BF16),
        scratch_shapes=[pltpu.VMEM((K1, tn), BF16), pltpu.VMEM((K2, tn), BF16)],
        compiler_params=_params(("arbitrary", "arbitrary"),
                                _vmem_limit(tm * K1 * 2, K1 * tn * 4, tm * K2 * 2, K2 * tn * 4,
                                            2 * tm * tn * 4, tm * tn * 2,
                                            temps=(K1 + K2) * tn * 2 + 3 * tm * tn * 4)),
        name="mm_mix",
    )(o_att, w_pa, o_rwkv, w_pr, z, z, b_gate)


def _mm_resw_kernel(a_ref, b_ref, r_ref, o_ref, bw_ref):
    @pl.when(pl.program_id(1) == 0)
    def _():
        _cast_resident(b_ref, bw_ref)

    o_ref[...] = r_ref[...] + _dot(a_ref[...], bw_ref[...])


def _mm_resw(a, w, res, tm, tn):
    M, K = a.shape
    N = w.shape[1]
    return pl.pallas_call(
        _mm_resw_kernel,
        grid=(N // tn, M // tm),
        in_specs=[pl.BlockSpec((tm, K), lambda j, i: (i, 0)), pl.BlockSpec((K, tn), lambda j, i: (0, j)),
                  pl.BlockSpec((tm, tn), lambda j, i: (i, j))],
        out_specs=pl.BlockSpec((tm, tn), lambda j, i: (i, j)),
        out_shape=jax.ShapeDtypeStruct((M, N), F32),
        scratch_shapes=[pltpu.VMEM((K, tn), BF16)],
        compiler_params=_params(("arbitrary", "arbitrary"),
                                _vmem_limit(tm * K * 2, K * tn * 4, 2 * tm * tn * 4, temps=K * tn * 2 + tm * tn * 4)),
        name="mm_out",
    )(a, w, res)


def _mm_res_kernel(a_ref, b_ref, r_ref, o_ref, *, nk):
    k = pl.program_id(2)
    part = _dot(a_ref[...], b_ref[...])
    if nk == 1:
        o_ref[...] = r_ref[...] + part
    else:
        @pl.when(k == 0)
        def _():
            o_ref[...] = r_ref[...] + part

        @pl.when(k > 0)
        def _():
            o_ref[...] += part


def _mm_res(a, b, res, tm, tn, tk):
    M, K = a.shape
    N = b.shape[1]
    nk = K // tk
    return pl.pallas_call(
        functools.partial(_mm_res_kernel, nk=nk),
        grid=(M // tm, N // tn, nk),
        in_specs=[pl.BlockSpec((tm, tk), lambda i, j, k: (i, k)), pl.BlockSpec((tk, tn), lambda i, j, k: (k, j)),
                  pl.BlockSpec((tm, tn), lambda i, j, k: (i, j))],
        out_specs=pl.BlockSpec((tm, tn), lambda i, j, k: (i, j)),
        out_shape=jax.ShapeDtypeStruct((M, N), F32),
        compiler_params=_params(("arbitrary", "arbitrary", "arbitrary"),
                                _vmem_limit(tm * tk * 2, tk * tn * 2, 2 * tm * tn * 4, temps=tm * tn * 4)),
        name="mm_down",
    )(a, b, res)


def _mm_glu_kernel(a_ref, bg_ref, bu_ref, o_ref, wg_ref, wu_ref):
    @pl.when(pl.program_id(1) == 0)
    def _():
        _cast_resident(bg_ref, wg_ref)
        _cast_resident(bu_ref, wu_ref)

    a = a_ref[...]
    gate = _dot(a, wg_ref[...])
    up = _dot(a, wu_ref[...])
    o_ref[...] = (gate * jax.nn.sigmoid(gate) * up).astype(o_ref.dtype)


def _mm_glu(a, wg, wu, tm, tn):
    M, K = a.shape
    N = wg.shape[1]
    return pl.pallas_call(
        _mm_glu_kernel,
        grid=(N // tn, M // tm),
        in_specs=[pl.BlockSpec((tm, K), lambda j, i: (i, 0)), pl.BlockSpec((K, tn), lambda j, i: (0, j)),
                  pl.BlockSpec((K, tn), lambda j, i: (0, j))],
        out_specs=pl.BlockSpec((tm, tn), lambda j, i: (i, j)),
        out_shape=jax.ShapeDtypeStruct((M, N), BF16),
        scratch_shapes=[pltpu.VMEM((K, tn), BF16), pltpu.VMEM((K, tn), BF16)],
        compiler_params=_params(("arbitrary", "arbitrary"),
                                _vmem_limit(tm * K * 2, 2 * K * tn * 4, tm * tn * 2,
                                            temps=2 * K * tn * 2 + 3 * tm * tn * 4)),
        name="mm_glu",
    )(a, wg, wu)


def _rope_kernel(z_ref, cos_ref, sin_ref, o_ref, *, n_rope_blocks, heads_per_block, k_heads_last):
    j = pl.program_id(1)
    cos = cos_ref[...]
    sin = sin_ref[...]
    lane = lax.broadcasted_iota(jnp.int32, cos.shape, 1)
    half = ROPE_DIMS // 2
    for h in range(heads_per_block):
        t = z_ref[:, h * HEAD_DIM:(h + 1) * HEAD_DIM]
        up = jnp.concatenate([t[:, half:], t[:, :half]], axis=1)
        down = jnp.concatenate([t[:, HEAD_DIM - half:], t[:, :HEAD_DIM - half]], axis=1)
        partner = jnp.where(lane < half, up, down)
        rot = t * cos + partner * sin
        if h >= k_heads_last:
            rot = jnp.where(j == n_rope_blocks - 1, t, rot)
        o_ref[:, h * HEAD_DIM:(h + 1) * HEAD_DIM] = rot.astype(o_ref.dtype)


def _rope_cast(z, cos_t, sin_t, off_q, width, kv_width):
    Lp = z.shape[0]
    tc = 2 * kv_width
    assert off_q % tc == 0 and width % tc == 0
    nblk = width // tc
    return pl.pallas_call(
        functools.partial(_rope_kernel, n_rope_blocks=nblk, heads_per_block=tc // HEAD_DIM,
                          k_heads_last=kv_width // HEAD_DIM),
        grid=(Lp // BLOCK, nblk),
        in_specs=[pl.BlockSpec((BLOCK, tc), lambda i, j: (i, off_q // tc + j)),
                  pl.BlockSpec((BLOCK, HEAD_DIM), lambda i, j: (i, 0)),
                  pl.BlockSpec((BLOCK, HEAD_DIM), lambda i, j: (i, 0))],
        out_specs=pl.BlockSpec((BLOCK, tc), lambda i, j: (i, j)),
        out_shape=jax.ShapeDtypeStruct((Lp, width), BF16),
        compiler_params=_params(("arbitrary", "arbitrary"), _vmem_limit(BLOCK * tc * 4, BLOCK * tc * 2)),
        name="rope_cast",
    )(z, cos_t, sin_t)


NEG_BIG = -1e30


def _attn_kernel(q_ref, kp_ref, kc_ref, kn_ref, km_ref, vp_ref, vc_ref, vn_ref, vm_ref, sink_ref, o_ref,
                 *, nb, group):
    i = pl.program_id(1)
    scale = HEAD_DIM ** -0.5
    zpad = jnp.zeros((BLOCK - N_META, HEAD_DIM), BF16)
    k_all = jnp.concatenate([kp_ref[...], kc_ref[...], kn_ref[...], km_ref[...], zpad], axis=0)
    v_all = jnp.concatenate([vp_ref[...], vc_ref[...], vn_ref[...], vm_ref[...], zpad], axis=0)
    q_all = jnp.concatenate([q_ref[:, g * HEAD_DIM:(g + 1) * HEAD_DIM] for g in range(group)], axis=0)
    s = _dot(q_all, k_all, NT) * scale
    rows = lax.broadcasted_iota(jnp.int32, s.shape, 0) % BLOCK
    cols = lax.broadcasted_iota(jnp.int32, s.shape, 1)
    rel = cols - BLOCK - rows
    kblk = i - 1 + cols // BLOCK
    band_ok = (cols < 3 * BLOCK) & (kblk >= 0) & (kblk < nb) & (jnp.abs(rel) <= WINDOW)
    meta_ok = (cols >= 3 * BLOCK) & (cols < 3 * BLOCK + N_META)
    s = jnp.where(band_ok | meta_ok, s, NEG_BIG)
    sink = sink_ref[:, 0:1]
    m = jnp.maximum(jnp.max(s, axis=-1, keepdims=True), sink)
    p = jnp.exp(s - m)
    denom = jnp.sum(p, axis=-1, keepdims=True) + jnp.exp(sink - m)
    o = _dot(p.astype(BF16), v_all) / denom
    for g in range(group):
        o_ref[:, g * HEAD_DIM:(g + 1) * HEAD_DIM] = o[g * BLOCK:(g + 1) * BLOCK, :].astype(o_ref.dtype)


def _attention(qkv, sink, S, n_heads, n_kv):
    group = n_heads // n_kv
    nb = S // BLOCK
    qw = group * HEAD_DIM
    kcol = n_heads
    vcol = n_heads + n_kv
    meta_blk = (S + PAD_ROWS - N_META) // N_META
    sink_rows = jnp.broadcast_to(sink.astype(F32).reshape(n_kv, group, 1, 1),
                                 (n_kv, group, BLOCK, LANES)).reshape(n_kv, group * BLOCK, LANES)

    def kv_specs(col0):
        return [pl.BlockSpec((BLOCK, HEAD_DIM), lambda h, i: (jnp.maximum(i - 1, 0), col0 + h)),
                pl.BlockSpec((BLOCK, HEAD_DIM), lambda h, i: (i, col0 + h)),
                pl.BlockSpec((BLOCK, HEAD_DIM), lambda h, i: (jnp.minimum(i + 1, nb - 1), col0 + h)),
                pl.BlockSpec((N_META, HEAD_DIM), lambda h, i: (meta_blk, col0 + h))]

    return pl.pallas_call(
        functools.partial(_attn_kernel, nb=nb, group=group),
        grid=(n_kv, nb),
        in_specs=[pl.BlockSpec((BLOCK, qw), lambda h, i: (i, h))] + kv_specs(kcol) + kv_specs(vcol)
                 + [pl.BlockSpec((None, group * BLOCK, LANES), lambda h, i: (h, 0, 0))],
        out_specs=pl.BlockSpec((BLOCK, qw), lambda h, i: (i, h)),
        out_shape=jax.ShapeDtypeStruct((S, n_heads * HEAD_DIM), BF16),
        compiler_params=_params(("arbitrary", "arbitrary"),
                                _vmem_limit(BLOCK * qw * 2 * 2, 8 * BLOCK * HEAD_DIM * 2,
                                            group * BLOCK * LANES * 4, temps=6 * group * BLOCK * 512 * 4)),
        name="attention",
    )(qkv, qkv, qkv, qkv, qkv, qkv, qkv, qkv, qkv, sink_rows)


def _split3(x):
    hi = x.astype(BF16)
    r1 = x - hi.astype(F32)
    mid = r1.astype(BF16)
    lo = (r1 - mid.astype(F32)).astype(BF16)
    return hi, mid, lo


def _split2(x):
    hi = x.astype(BF16)
    return hi, (x - hi.astype(F32)).astype(BF16)


def _dot_exact_rhs(x, m_bf16, dims=NN, left=False):
    parts = _split3(x)
    if left:
        return sum(_dot(m_bf16, p, dims) for p in parts)
    return sum(_dot(p, m_bf16, dims) for p in parts)


def _dotp(a, b, passes, dims=NN):
    if passes == 1:
        return _dot(a.astype(BF16), b.astype(BF16), dims)
    a_hi, a_lo = _split2(a)
    b_hi, b_lo = _split2(b)
    return _dot(a_hi, b_hi, dims) + (_dot(a_hi, b_lo, dims) + _dot(a_lo, b_hi, dims))


def _head_sum(x, ones_bd):
    outs = []
    for s in range(x.shape[1] // LANES):
        outs.append(_dot_exact_rhs(x[:, s * LANES:(s + 1) * LANES], ones_bd))
    return outs[0] if len(outs) == 1 else jnp.concatenate(outs, axis=1)


def _prep_kernel(r_ref, rp_ref, rn_ref, k_ref, kp_ref, kn_ref, v_ref, vp_ref, vn_ref,
                 lo_ref, lop_ref, lon_ref, gd_ref, gdp_ref, gdn_ref,
                 pc_ref, pl_ref, wl_ref, g2_ref, ones_ref,
                 ro_ref, vo_ref, kko_ref, lwf_ref, lwb_ref, kdf_ref, kdb_ref, bdf_ref, bdb_ref, go_ref,
                 buf_ref, *, n_tiles, tr, lora_tanh_cols):
    i = pl.program_id(0)

    def shifted(main_ref, prev_ref, next_ref, mup, mun):
        w = main_ref.shape[1]
        buf_ref[0:8, 0:w] = prev_ref[...]
        buf_ref[8:8 + tr, 0:w] = main_ref[...]
        buf_ref[8 + tr:16 + tr, 0:w] = next_ref[...]
        x = main_ref[...]
        prev = buf_ref[7:7 + tr, 0:w]
        nxt = buf_ref[9:9 + tr, 0:w]
        return x + mup * (prev - x) + mun * (nxt - x)

    r = shifted(r_ref, rp_ref, rn_ref, pc_ref[0:1, :], pc_ref[1:2, :])
    k = shifted(k_ref, kp_ref, kn_ref, pc_ref[2:3, :], pc_ref[3:4, :])
    v = shifted(v_ref, vp_ref, vn_ref, pc_ref[4:5, :], pc_ref[5:6, :])
    lo = shifted(lo_ref, lop_ref, lon_ref, pl_ref[0:1, 0:LORA_PAD], pl_ref[1:2, 0:LORA_PAD])
    gd = shifted(gd_ref, gdp_ref, gdn_ref, pl_ref[0:1, LORA_PAD:], pl_ref[1:2, LORA_PAD:])

    row = lax.broadcasted_iota(jnp.int32, (tr, 1), 0)
    valid = jnp.where((i < n_tiles - 1) | (row >= tr - N_META), 1.0, 0.0).astype(F32)

    lane = lax.broadcasted_iota(jnp.int32, lo.shape, 1)
    lo_act = jnp.where(lane < lora_tanh_cols, jnp.tanh(lo), lo).astype(BF16)
    dec_f = _dot(lo_act, wl_ref[0])
    dec_b = _dot(lo_act, wl_ref[1])
    apre_f = _dot(lo_act, wl_ref[2])
    apre_b = _dot(lo_act, wl_ref[3])
    g = _dot(jax.nn.sigmoid(lo).astype(BF16), g2_ref[0]) + _dot(jax.nn.sigmoid(gd).astype(BF16), g2_ref[1])

    def log_decay(dec, w0):
        w_log = -jax.nn.softplus(-(w0 + dec)) - 0.5
        return -jnp.exp(w_log)

    a_f = jax.nn.sigmoid(pc_ref[8:9, :] + apre_f)
    a_b = jax.nn.sigmoid(pc_ref[9:10, :] + apre_b)
    kk = k * pc_ref[10:11, :]
    ss = _head_sum(kk * kk, ones_ref[...])
    kk = kk / jnp.maximum(jnp.sqrt(ss), 1e-12) * valid
    k_a = pc_ref[11:12, :]
    kv = k * valid

    ro_ref[...] = r * valid
    vo_ref[...] = v * valid
    kko_ref[...] = kk
    lwf_ref[...] = log_decay(dec_f, pc_ref[6:7, :])
    lwb_ref[...] = log_decay(dec_b, pc_ref[7:8, :])
    kdf_ref[...] = kv * (1.0 + (a_f - 1.0) * k_a)
    kdb_ref[...] = kv * (1.0 + (a_b - 1.0) * k_a)
    bdf_ref[...] = kk * a_f
    bdb_ref[...] = kk * a_b
    go_ref[...] = g


def _rwkv_prep(z, off_r, C, pc, plo, wl, g2, ones_bd, lora_tanh_cols):
    Lp = z.shape[0]
    tr = BLOCK
    ct = _pick(C, (1024, 512))
    n_tiles = Lp // tr
    n8 = Lp // 8
    off_lo = off_r + 3 * C
    off_gd = off_lo + LORA_PAD
    assert off_r % ct == 0 and off_lo % LORA_PAD == 0

    def seg_specs(off, w, with_c):
        cb = off // w

        def col(c):
            return cb + c if with_c else cb
        return [pl.BlockSpec((tr, w), lambda i, c: (i, col(c))),
                pl.BlockSpec((8, w), lambda i, c: ((i * (tr // 8) + n8 - 1) % n8, col(c))),
                pl.BlockSpec((8, w), lambda i, c: (((i + 1) * (tr // 8)) % n8, col(c)))]

    in_specs = (seg_specs(off_r, ct, True) + seg_specs(off_r + C, ct, True) + seg_specs(off_r + 2 * C, ct, True)
                + seg_specs(off_lo, LORA_PAD, False) + seg_specs(off_gd, LORA_PAD, False)
                + [pl.BlockSpec((16, ct), lambda i, c: (0, c)),
                   pl.BlockSpec((8, 2 * LORA_PAD), lambda i, c: (0, 0)),
                   pl.BlockSpec((4, LORA_PAD, ct), lambda i, c: (0, 0, c)),
                   pl.BlockSpec((2, LORA_PAD, ct), lambda i, c: (0, 0, c)),
                   pl.BlockSpec((LANES, LANES), lambda i, c: (0, 0))])
    out_spec = pl.BlockSpec((tr, ct), lambda i, c: (i, c))
    out_sds = jax.ShapeDtypeStruct((Lp, C), F32)
    return pl.pallas_call(
        functools.partial(_prep_kernel, n_tiles=n_tiles, tr=tr, lora_tanh_cols=lora_tanh_cols),
        grid=(n_tiles, C // ct),
        in_specs=in_specs,
        out_specs=[out_spec] * 10,
        out_shape=[out_sds] * 10,
        scratch_shapes=[pltpu.VMEM((tr + 16, max(ct, LORA_PAD)), F32)],
        compiler_params=_params(("arbitrary", "arbitrary"),
                                _vmem_limit(3 * tr * ct * 4, 2 * tr * LORA_PAD * 4, 4 * LORA_PAD * ct * 2,
                                            2 * LORA_PAD * ct * 2, 10 * tr * ct * 4, temps=24 * tr * ct * 4)),
        name="rwkv_prep",
    )(*([z] * 15), pc, plo, wl, g2, ones_bd)


SCAN_PAIRS = 4


def _scan_consts():
    C = CHUNK
    ti = lax.broadcasted_iota(jnp.int32, (C, C), 0)
    si = lax.broadcasted_iota(jnp.int32, (C, C), 1)
    row = lax.broadcasted_iota(jnp.int32, (C, 2 * C), 0)
    col = lax.broadcasted_iota(jnp.int32, (C, 2 * C), 1) % C
    rr = lax.broadcasted_iota(jnp.int32, (LANES, LANES), 0)
    cc = lax.broadcasted_iota(jnp.int32, (LANES, LANES), 1)
    return dict(
        cum_f=jnp.where(si <= ti, 1.0, 0.0).astype(BF16), cum_b=jnp.where(si >= ti, 1.0, 0.0).astype(BF16),
        head0=lax.broadcasted_iota(jnp.int32, (C, LANES), 1) < RWKV_HEAD,
        strict_f=col < row, strict_b=col > row, incl_f=col <= row, incl_b=col >= row,
        eye_sbs=jnp.where(col == row, 1.0, 0.0).astype(F32),
        same_head=(rr // RWKV_HEAD) == (cc // RWKV_HEAD), eye=rr == cc)


def _scan_chains(chains, cst):
    C = CHUNK
    head0, same, eye = cst["head0"], cst["same_head"], cst["eye"]
    n = range(len(chains))
    rev = [c[7] for c in chains]
    r, v, al, lw, k, be, h = ([c[i] for c in chains] for i in range(7))
    strict = [cst["strict_b" if x else "strict_f"] for x in rev]
    incl = [cst["incl_b" if x else "incl_f"] for x in rev]

    def bd(x):
        xb = x.astype(BF16)
        zero = jnp.zeros_like(xb)
        return jnp.concatenate([jnp.where(head0, xb, zero), jnp.where(head0, zero, xb)], axis=0)

    def mm(a, b_bf16, dims=NN):
        return _dot(a.astype(BF16), b_bf16, dims)

    cl = [_dot_exact_rhs(lw[i], cst["cum_b" if rev[i] else "cum_f"], NN, left=True) for i in n]
    total = [cl[i][0:1, :] if rev[i] else cl[i][C - 1:C, :] for i in n]
    a_t = [al[i] * jnp.exp(cl[i] - lw[i]) for i in n]
    r_t = [r[i] * jnp.exp(cl[i]) for i in n]
    w_inv = [jnp.exp(-cl[i]) for i in n]
    w_rest = [jnp.exp(total[i] - cl[i]) for i in n]
    bd_v = [bd(v[i]) for i in n]
    sc = [mm(jnp.concatenate([a_t[i], r_t[i]], axis=0),
             jnp.concatenate([bd(k[i] * w_inv[i]), bd(be[i] * w_inv[i])], axis=0), NT) for i in n]
    a_ak = [jnp.where(strict[i], sc[i][0:C, 0:2 * C], 0.0) for i in n]
    n_ab = [jnp.where(strict[i], sc[i][0:C, 2 * C:], 0.0) for i in n]
    a_rk = [jnp.where(incl[i], sc[i][C:, 0:2 * C], 0.0) for i in n]
    a_rb = [jnp.where(incl[i], sc[i][C:, 2 * C:], 0.0) for i in n]

    t_inv = [cst["eye_sbs"] - n_ab[i] for i in n]
    sq = n_ab
    n_round = 1
    while (1 << n_round) < C:
        sq = [mm(sq[i], bd(sq[i])) for i in n]
        t_inv = [t_inv[i] + mm(t_inv[i], bd(sq[i])) for i in n]
        n_round += 1

    av = [mm(a_ak[i], bd_v[i]) for i in n]
    tp = [mm(t_inv[i], jnp.concatenate([bd(av[i]), bd(a_t[i])], axis=1)) for i in n]
    p0 = [tp[i][:, 0:LANES] for i in n]
    at = [tp[i][:, LANES:] for i in n]
    y0 = [mm(jnp.concatenate([a_rk[i], a_rb[i]], axis=1), jnp.concatenate([bd_v[i], bd(-p0[i])], axis=0)) for i in n]
    rh = [r_t[i] - mm(a_rb[i], bd(at[i])) for i in n]
    bt = [mm(be[i] * w_rest[i], jnp.concatenate([at[i], p0[i]], axis=1).astype(BF16), TN) for i in n]
    kv = [mm(k[i] * w_rest[i], v[i].astype(BF16), TN) for i in n]
    m_mat = [jnp.where(same, jnp.where(eye, jnp.broadcast_to(jnp.exp(total[i]), (LANES, LANES)), 0.0)
                       - bt[i][:, 0:LANES], 0.0) for i in n]
    g_mat = [jnp.where(same, kv[i] - bt[i][:, LANES:], 0.0) for i in n]
    h_b = [h[i].astype(BF16) for i in n]
    y = [mm(rh[i], h_b[i]) + y0[i] for i in n]
    h_new = [mm(m_mat[i], h_b[i]) + g_mat[i] for i in n]
    return list(zip(y, h_new))


def _scan_kernel(rf_ref, vf_ref, af_ref, lwf_ref, kf_ref, bf_ref,
                 rb_ref, vb_ref, ab_ref, lwb_ref, kb_ref, bb_ref,
                 yf_ref, yb_ref, hf_ref, hb_ref, *, pairs):
    @pl.when(pl.program_id(1) == 0)
    def _():
        hf_ref[...] = jnp.zeros(hf_ref.shape, F32)
        hb_ref[...] = jnp.zeros(hb_ref.shape, F32)

    cst = _scan_consts()
    chains = []
    for g in range(pairs):
        sl = slice(g * LANES, (g + 1) * LANES)
        chains.append((rf_ref[:, sl], vf_ref[:, sl], af_ref[:, sl], lwf_ref[:, sl], kf_ref[:, sl], bf_ref[:, sl],
                       hf_ref[g], False))
        chains.append((rb_ref[:, sl], vb_ref[:, sl], ab_ref[:, sl], lwb_ref[:, sl], kb_ref[:, sl], bb_ref[:, sl],
                       hb_ref[g], True))
    res = _scan_chains(chains, cst)
    for g in range(pairs):
        sl = slice(g * LANES, (g + 1) * LANES)
        yf_ref[:, sl], hf_ref[g] = res[2 * g]
        yb_ref[:, sl], hb_ref[g] = res[2 * g + 1]


def _rwkv_scan(r, v, kk, lw_f, lw_b, kd_f, kd_b, bd_f, bd_b, S):
    Lp, C = r.shape
    n_real = S // CHUNK
    n_chunks = Lp // CHUNK
    steps = n_real + 1
    pairs = _pick(C // LANES, (SCAN_PAIRS, 2, 1))
    w = pairs * LANES

    def fwd(p, s):
        return ((s + n_real + 1) % n_chunks, p)

    def bwd(p, s):
        return (n_real - s, p)

    spec_f = pl.BlockSpec((CHUNK, w), fwd)
    spec_b = pl.BlockSpec((CHUNK, w), bwd)
    out_sds = jax.ShapeDtypeStruct((Lp, C), F32)
    return pl.pallas_call(
        functools.partial(_scan_kernel, pairs=pairs),
        grid=(C // w, steps),
        in_specs=[spec_f] * 6 + [spec_b] * 6,
        out_specs=[spec_f, spec_b],
        out_shape=[out_sds, out_sds],
        scratch_shapes=[pltpu.VMEM((pairs, LANES, LANES), F32), pltpu.VMEM((pairs, LANES, LANES), F32)],
        compiler_params=_params(("arbitrary", "arbitrary"), 32 << 20),
        name="rwkv_scan",
    )(r, v, kk, lw_f, kd_f, bd_f, r, v, kk, lw_b, kd_b, bd_b)


def _post_kernel(yf_ref, yb_ref, r_ref, kdf_ref, kdb_ref, v_ref, g_ref, pq_ref, ones_ref, o_ref):
    ones_bd = ones_ref[...]
    inv_n = 1.0 / RWKV_HEAD
    y = yf_ref[...] + yb_ref[...]
    mean = _head_sum(y, ones_bd) * inv_n
    yc = y - mean
    var = _head_sum(yc * yc, ones_bd) * inv_n
    yn = yc * lax.rsqrt(var + GN_EPS) * pq_ref[0:1, :] + pq_ref[1:2, :]
    bonus = _head_sum(r_ref[...] * (kdf_ref[...] + kdb_ref[...]) * pq_ref[2:3, :], ones_bd) * v_ref[...]
    o_ref[...] = ((yn + bonus) * g_ref[...]).astype(o_ref.dtype)


def _rwkv_post(y_f, y_b, r, kd_f, kd_b, v, g, pq, ones_bd, S):
    C = r.shape[1]
    tr = 256
    ct = _pick(C, (1024, 512))
    spec = pl.BlockSpec((tr, ct), lambda i, c: (i, c))
    return pl.pallas_call(
        _post_kernel,
        grid=(S // tr, C // ct),
        in_specs=[spec] * 7 + [pl.BlockSpec((8, ct), lambda i, c: (0, c)),
                               pl.BlockSpec((LANES, LANES), lambda i, c: (0, 0))],
        out_specs=spec,
        out_shape=jax.ShapeDtypeStruct((S, C), BF16),
        compiler_params=_params(("arbitrary", "arbitrary"),
                                _vmem_limit(7 * tr * ct * 4, tr * ct * 2, temps=16 * tr * ct * 4)),
        name="rwkv_post",
    )(y_f, y_b, r, kd_f, kd_b, v, g, pq, ones_bd)


def _pad_cols(w, width):
    return jnp.pad(w, ((0, 0), (0, width - w.shape[1])))


def kernel(x, meta_tokens, norm_mix_w, w_in, b_gate, mu_prev, mu_next, dec_w0, dec_w2, iclr_a0, iclr_a2,
           gate_w2, k_k, k_a, r_k, ln_x_w, ln_x_b, attn_sink, w_proj_attn, w_proj_rwkv, w_out, norm_ffn_w,
           w_ffn_gate, w_ffn_up, w_ffn_down, norm_final_w):
    B, S, D = x.shape
    assert B == 1 and norm_mix_w.shape[0] == 1, "one sequence, one layer"
    C = k_k.shape[-1]
    AW = w_proj_attn.shape[1]
    n_heads = attn_sink.shape[-1]
    DL, IL, GL = dec_w2.shape[2], iclr_a2.shape[2], gate_w2.shape[1]
    shift_w = mu_prev.shape[-1]
    KVW = (w_in.shape[-1] - 2 * D - AW - shift_w) // 2
    n_kv = KVW // HEAD_DIM
    F = w_ffn_gate.shape[-1]
    lora_w = 2 * DL + 2 * IL
    assert shift_w == 3 * C + lora_w + GL and lora_w <= LORA_PAD and GL <= LORA_PAD
    assert S % BLOCK == 0 and C % LANES == 0 and n_heads * HEAD_DIM == AW
    Lp = S + PAD_ROWS

    off_q = 2 * D
    off_r = off_q + AW + 2 * KVW
    off_lo = off_r + 3 * C
    tn_in = 512
    n_in_pad = -(-w_in.shape[-1] // tn_in) * tn_in
    assert n_in_pad >= off_lo + 2 * LORA_PAD and off_lo % LORA_PAD == 0
    low_w = lora_w + GL
    gl_a = LORA_PAD - lora_w

    mu_p, mu_n = mu_prev[0], mu_next[0]
    zc = jnp.zeros((C,), F32)
    pc = jnp.stack([mu_p[:C], mu_n[:C], mu_p[C:2 * C], mu_n[C:2 * C], mu_p[2 * C:3 * C], mu_n[2 * C:3 * C],
                    dec_w0[0, 0], dec_w0[0, 1], iclr_a0[0, 0], iclr_a0[0, 1], k_k[0], k_a[0], zc, zc, zc, zc])
    plo = jnp.zeros((8, 2 * LORA_PAD), F32).at[0, :low_w].set(mu_p[3 * C:]).at[1, :low_w].set(mu_n[3 * C:])
    wl = jnp.zeros((4, LORA_PAD, C), F32)
    wl = wl.at[0, 0:DL].set(dec_w2[0, 0]).at[1, DL:2 * DL].set(dec_w2[0, 1])
    wl = wl.at[2, 2 * DL:2 * DL + IL].set(iclr_a2[0, 0]).at[3, 2 * DL + IL:lora_w].set(iclr_a2[0, 1])
    wl = wl.astype(BF16)
    g2 = jnp.zeros((2, LORA_PAD, C), F32).at[0, lora_w:].set(gate_w2[0, :gl_a]).at[1, :GL - gl_a].set(
        gate_w2[0, gl_a:]).astype(BF16)
    pq = jnp.zeros((8, C), F32).at[0].set(ln_x_w[0]).at[1].set(ln_x_b[0]).at[2].set(r_k[0])
    lane = jnp.arange(LANES)
    ones_bd = (lane[:, None] // RWKV_HEAD == lane[None, :] // RWKV_HEAD).astype(BF16)

    w_d = w_ffn_down[0].astype(BF16)

    pos = jnp.concatenate([jnp.arange(N_META, N_META + S), jnp.zeros((PAD_ROWS - N_META,), jnp.int32),
                           jnp.arange(N_META)]).astype(F32)
    inv = ROPE_THETA ** (-jnp.arange(0, ROPE_DIMS, 2, dtype=F32) / ROPE_DIMS)
    ang = pos[:, None] * inv[None, :]
    ones_rest = jnp.ones((Lp, HEAD_DIM - ROPE_DIMS), F32)
    cos_t = jnp.concatenate([jnp.cos(ang), jnp.cos(ang), ones_rest], axis=1)
    sin_t = jnp.concatenate([-jnp.sin(ang), jnp.sin(ang), 0.0 * ones_rest], axis=1)

    x2 = x[0]
    h_ext = _norm_ext(x2, meta_tokens.astype(x.dtype), norm_mix_w[0])
    z = _mm_in(h_ext, w_in[0], _pick(Lp, (1040, 640)), tn_in)
    qkv = _rope_cast(z, cos_t, sin_t, off_q, AW + 2 * KVW, KVW)
    o_att = _attention(qkv, attn_sink[0], S, n_heads, n_kv)
    r_s, v_s, kk, lw_f, lw_b, kd_f, kd_b, bd_f, bd_b, g = _rwkv_prep(
        z, off_r, C, pc, plo, wl, g2, ones_bd, 2 * DL)
    y_f, y_b = _rwkv_scan(r_s, v_s, kk, lw_f, lw_b, kd_f, kd_b, bd_f, bd_b, S)
    o_rwkv = _rwkv_post(y_f, y_b, r_s, kd_f, kd_b, v_s, g, pq, ones_bd, S)
    tm = _pick(S, (1024, 512))
    mixed = _mm_mix(o_att, w_proj_attn[0], o_rwkv, w_proj_rwkv[0], z, b_gate[0], S, D, tm, 256)
    h1 = _mm_resw(mixed, w_out[0], x2, tm, 256)
    hn = _norm(h1, norm_ffn_w[0], BF16)
    act = _mm_glu(hn, w_ffn_gate[0], w_ffn_up[0], tm, 256)
    h2 = _mm_res(act, w_d, h1, tm, 512, F // 2)
    y = _norm(h2, norm_final_w, x.dtype)
    return y[None]
```

```python
import functools

import jax
import jax.numpy as jnp
from jax import lax
from jax.experimental import pallas as pl
from jax.experimental.pallas import tpu as pltpu

F32 = jnp.float32
BF16 = jnp.bfloat16

N_META = 16
HEAD_DIM = 128
WINDOW = 128
BLOCK = 128
ROPE_DIMS = HEAD_DIM // 4
ROPE_THETA = 500000.0
RWKV_HEAD = 64
RMS_EPS = 1e-6
GN_EPS = 64e-5
LANES = 128
PAD_ROWS = BLOCK
CHUNK = 64
LORA_PAD = 512
VMEM_PHYSICAL = 64 * 1024 * 1024

NN = (((1,), (0,)), ((), ()))
NT = (((1,), (1,)), ((), ()))
TN = (((0,), (0,)), ((), ()))


def _vmem_limit(*block_bytes, temps=0):
    need = 2 * sum(block_bytes) + temps + (4 << 20)
    return int(min(max(need, 16 << 20), VMEM_PHYSICAL - (6 << 20)))


def _params(sem, limit):
    return pltpu.CompilerParams(dimension_semantics=sem, vmem_limit_bytes=limit)


def _pick(n, cands):
    for c in cands:
        if n % c == 0:
            return c
    raise ValueError(f"no tile in {cands} divides {n}")


def _rms(x, w):
    return x * lax.rsqrt(jnp.mean(x * x, axis=-1, keepdims=True) + RMS_EPS) * w


def _norm_ext_kernel(x_ref, meta_ref, w_ref, o_ref, *, nb):
    i = pl.program_id(0)

    @pl.when(i < nb)
    def _():
        o_ref[...] = _rms(x_ref[...], w_ref[...]).astype(o_ref.dtype)

    @pl.when(i == nb)
    def _():
        o_ref[...] = jnp.zeros(o_ref.shape, o_ref.dtype)
        o_ref[PAD_ROWS - N_META:, :] = _rms(meta_ref[...], w_ref[...]).astype(o_ref.dtype)


def _norm_ext(x, meta, w):
    S, D = x.shape
    nb = S // BLOCK
    return pl.pallas_call(
        functools.partial(_norm_ext_kernel, nb=nb),
        grid=(nb + 1,),
        in_specs=[pl.BlockSpec((BLOCK, D), lambda i: (jnp.minimum(i, nb - 1), 0)),
                  pl.BlockSpec((N_META, D), lambda i: (0, 0)),
                  pl.BlockSpec((1, D), lambda i: (0, 0))],
        out_specs=pl.BlockSpec((BLOCK, D), lambda i: (i, 0)),
        out_shape=jax.ShapeDtypeStruct((S + PAD_ROWS, D), BF16),
        compiler_params=_params(("arbitrary",), _vmem_limit(BLOCK * D * 4, BLOCK * D * 2)),
        name="norm_ext",
    )(x, meta, w.reshape(1, D))


def _norm_kernel(x_ref, w_ref, o_ref):
    o_ref[...] = _rms(x_ref[...], w_ref[...]).astype(o_ref.dtype)


def _norm(x, w, out_dtype, tr=256):
    S, D = x.shape
    return pl.pallas_call(
        _norm_kernel,
        grid=(S // tr,),
        in_specs=[pl.BlockSpec((tr, D), lambda i: (i, 0)), pl.BlockSpec((1, D), lambda i: (0, 0))],
        out_specs=pl.BlockSpec((tr, D), lambda i: (i, 0)),
        out_shape=jax.ShapeDtypeStruct((S, D), out_dtype),
        compiler_params=_params(("arbitrary",), _vmem_limit(tr * D * 4, tr * D * 4)),
        name="norm",
    )(x, w.reshape(1, D))


def _dot(a, b, dims=NN):
    return lax.dot_general(a, b, dims, preferred_element_type=F32)


def _cast_resident(src_ref, dst_ref, col0=None, n_valid=None):
    K = src_ref.shape[0]
    rows = _pick(K, (256, 128))

    def body(c, carry):
        sl = pl.ds(pl.multiple_of(c * rows, rows), rows)
        blk = src_ref[sl, :]
        if n_valid is not None:
            col = col0 + lax.broadcasted_iota(jnp.int32, blk.shape, 1)
            blk = jnp.where(col < n_valid, blk, 0.0)
        dst_ref[sl, :] = blk.astype(dst_ref.dtype)
        return carry

    lax.fori_loop(0, K // rows, body, 0)


def _mm_in_kernel(a_ref, b_ref, o_ref, bw_ref, *, n_valid):
    j, i = pl.program_id(0), pl.program_id(1)

    @pl.when(i == 0)
    def _():
        _cast_resident(b_ref, bw_ref, j * b_ref.shape[1], n_valid)

    o_ref[...] = _dot(a_ref[...], bw_ref[...])


def _mm_in(a, w, tm, tn):
    M, K = a.shape
    N = w.shape[1]
    nj = pl.cdiv(N, tn)
    return pl.pallas_call(
        functools.partial(_mm_in_kernel, n_valid=N),
        grid=(nj, M // tm),
        in_specs=[pl.BlockSpec((tm, K), lambda j, i: (i, 0)), pl.BlockSpec((K, tn), lambda j, i: (0, j))],
        out_specs=pl.BlockSpec((tm, tn), lambda j, i: (i, j)),
        out_shape=jax.ShapeDtypeStruct((M, nj * tn), F32),
        scratch_shapes=[pltpu.VMEM((K, tn), BF16)],
        compiler_params=_params(("arbitrary", "arbitrary"),
                                _vmem_limit(tm * K * 2, K * tn * 4, tm * tn * 4, temps=K * tn * 2 + tm * tn * 4)),
        name="mm_in",
    )(a, w)


def _mm_mix_kernel(a1_ref, b1_ref, a2_ref, b2_ref, z0_ref, z1_ref, bg_ref, o_ref, w1_ref, w2_ref):
    @pl.when(pl.program_id(1) == 0)
    def _():
        _cast_resident(b1_ref, w1_ref)
        _cast_resident(b2_ref, w2_ref)

    g0 = jax.nn.sigmoid(z0_ref[...] + bg_ref[0:1, :])
    g1 = jax.nn.sigmoid(z1_ref[...] + bg_ref[1:2, :])
    o_ref[...] = (g0 * _dot(a1_ref[...], w1_ref[...]) + g1 * _dot(a2_ref[...], w2_ref[...])).astype(o_ref.dtype)


def _mm_mix(o_att, w_pa, o_rwkv, w_pr, z, b_gate, S, D, tm, tn):
    K1, K2 = o_att.shape[1], o_rwkv.shape[1]
    nj = D // tn
    return pl.pallas_call(
        _mm_mix_kernel,
        grid=(nj, S // tm),
        in_specs=[pl.BlockSpec((tm, K1), lambda j, i: (i, 0)), pl.BlockSpec((K1, tn), lambda j, i: (0, j)),
                  pl.BlockSpec((tm, K2), lambda j, i: (i, 0)), pl.BlockSpec((K2, tn), lambda j, i: (0, j)),
                  pl.BlockSpec((tm, tn), lambda j, i: (i, j)), pl.BlockSpec((tm, tn), lambda j, i: (i, nj + j)),
                  pl.BlockSpec((2, tn), lambda j, i: (0, j))],
        out_specs=pl.BlockSpec((tm, tn), lambda j, i: (i, j)),
        out_shape=jax.ShapeDtypeStruct((S, D), BF16),
        scratch_shapes=[pltpu.VMEM((K1, tn), BF16), pltpu.VMEM((K2, tn), BF16)],
        compiler_params=_params(("arbitrary", "arbitrary"),
                                _vmem_limit(tm * K1 * 2, K1 * tn * 4, tm * K2 * 2, K2 * tn * 4,
                                            2 * tm * tn * 4, tm * tn * 2,
                                            temps=(K1 + K2) * tn * 2 + 3 * tm * tn * 4)),
        name="mm_mix",
    )(o_att, w_pa, o_rwkv, w_pr, z, z, b_gate)


def _mm_resw_kernel(a_ref, b_ref, r_ref, o_ref, bw_ref):
    @pl.when(pl.program_id(1) == 0)
    def _():
        _cast_resident(b_ref, bw_ref)

    o_ref[...] = r_ref[...] + _dot(a_ref[...], bw_ref[...])


def _mm_resw(a, w, res, tm, tn):
    M, K = a.shape
    N = w.shape[1]
    return pl.pallas_call(
        _mm_resw_kernel,
        grid=(N // tn, M // tm),
        in_specs=[pl.BlockSpec((tm, K), lambda j, i: (i, 0)), pl.BlockSpec((K, tn), lambda j, i: (0, j)),
                  pl.BlockSpec((tm, tn), lambda j, i: (i, j))],
        out_specs=pl.BlockSpec((tm, tn), lambda j, i: (i, j)),
        out_shape=jax.ShapeDtypeStruct((M, N), F32),
        scratch_shapes=[pltpu.VMEM((K, tn), BF16)],
        compiler_params=_params(("arbitrary", "arbitrary"),
                                _vmem_limit(tm * K * 2, K * tn * 4, 2 * tm * tn * 4, temps=K * tn * 2 + tm * tn * 4)),
        name="mm_out",
    )(a, w, res)


def _mm_res_kernel(a_ref, b_ref, r_ref, o_ref, *, nk):
    k = pl.program_id(2)
    part = _dot(a_ref[...], b_ref[...])
    if nk == 1:
        o_ref[...] = r_ref[...] + part
    else:
        @pl.when(k == 0)
        def _():
            o_ref[...] = r_ref[...] + part

        @pl.when(k > 0)
        def _():
            o_ref[...] += part


def _mm_res(a, b, res, tm, tn, tk):
    M, K = a.shape
    N = b.shape[1]
    nk = K // tk
    return pl.pallas_call(
        functools.partial(_mm_res_kernel, nk=nk),
        grid=(M // tm, N // tn, nk),
        in_specs=[pl.BlockSpec((tm, tk), lambda i, j, k: (i, k)), pl.BlockSpec((tk, tn), lambda i, j, k: (k, j)),
                  pl.BlockSpec((tm, tn), lambda i, j, k: (i, j))],
        out_specs=pl.BlockSpec((tm, tn), lambda i, j, k: (i, j)),
        out_shape=jax.ShapeDtypeStruct((M, N), F32),
        compiler_params=_params(("arbitrary", "arbitrary", "arbitrary"),
                                _vmem_limit(tm * tk * 2, tk * tn * 2, 2 * tm * tn * 4, temps=tm * tn * 4)),
        name="mm_down",
    )(a, b, res)


def _mm_glu_kernel(a_ref, bg_ref, bu_ref, o_ref, wg_ref, wu_ref):
    @pl.when(pl.program_id(1) == 0)
    def _():
        _cast_resident(bg_ref, wg_ref)
        _cast_resident(bu_ref, wu_ref)

    a = a_ref[...]
    gate = _dot(a, wg_ref[...])
    up = _dot(a, wu_ref[...])
    o_ref[...] = (gate * jax.nn.sigmoid(gate) * up).astype(o_ref.dtype)


def _mm_glu(a, wg, wu, tm, tn):
    M, K = a.shape
    N = wg.shape[1]
    return pl.pallas_call(
        _mm_glu_kernel,
        grid=(N // tn, M // tm),
        in_specs=[pl.BlockSpec((tm, K), lambda j, i: (i, 0)), pl.BlockSpec((K, tn), lambda j, i: (0, j)),
                  pl.BlockSpec((K, tn), lambda j, i: (0, j))],
        out_specs=pl.BlockSpec((tm, tn), lambda j, i: (i, j)),
        out_shape=jax.ShapeDtypeStruct((M, N), BF16),
        scratch_shapes=[pltpu.VMEM((K, tn), BF16), pltpu.VMEM((K, tn), BF16)],
        compiler_params=_params(("arbitrary", "arbitrary"),
                                _vmem_limit(tm * K * 2, 2 * K * tn * 4, tm * tn * 2,
                                            temps=2 * K * tn * 2 + 3 * tm * tn * 4)),
        name="mm_glu",
    )(a, wg, wu)


def _rope_kernel(z_ref, cos_ref, sin_ref, o_ref, *, n_rope_blocks, heads_per_block, k_heads_last):
    j = pl.program_id(1)
    cos = cos_ref[...]
    sin = sin_ref[...]
    lane = lax.broadcasted_iota(jnp.int32, cos.shape, 1)
    half = ROPE_DIMS // 2
    for h in range(heads_per_block):
        t = z_ref[:, h * HEAD_DIM:(h + 1) * HEAD_DIM]
        up = jnp.concatenate([t[:, half:], t[:, :half]], axis=1)
        down = jnp.concatenate([t[:, HEAD_DIM - half:], t[:, :HEAD_DIM - half]], axis=1)
        partner = jnp.where(lane < half, up, down)
        rot = t * cos + partner * sin
        if h >= k_heads_last:
            rot = jnp.where(j == n_rope_blocks - 1, t, rot)
        o_ref[:, h * HEAD_DIM:(h + 1) * HEAD_DIM] = rot.astype(o_ref.dtype)


def _rope_cast(z, cos_t, sin_t, off_q, width, kv_width):
    Lp = z.shape[0]
    tc = 2 * kv_width
    assert off_q % tc == 0 and width % tc == 0
    nblk = width // tc
    return pl.pallas_call(
        functools.partial(_rope_kernel, n_rope_blocks=nblk, heads_per_block=tc // HEAD_DIM,
                          k_heads_last=kv_width // HEAD_DIM),
        grid=(Lp // BLOCK, nblk),
        in_specs=[pl.BlockSpec((BLOCK, tc), lambda i, j: (i, off_q // tc + j)),
                  pl.BlockSpec((BLOCK, HEAD_DIM), lambda i, j: (i, 0)),
                  pl.BlockSpec((BLOCK, HEAD_DIM), lambda i, j: (i, 0))],
        out_specs=pl.BlockSpec((BLOCK, tc), lambda i, j: (i, j)),
        out_shape=jax.ShapeDtypeStruct((Lp, width), BF16),
        compiler_params=_params(("arbitrary", "arbitrary"), _vmem_limit(BLOCK * tc * 4, BLOCK * tc * 2)),
        name="rope_cast",
    )(z, cos_t, sin_t)


NEG_BIG = -1e30


def _attn_kernel(q_ref, kp_ref, kc_ref, kn_ref, km_ref, vp_ref, vc_ref, vn_ref, vm_ref, sink_ref, o_ref,
                 *, nb, group):
    i = pl.program_id(1)
    scale = HEAD_DIM ** -0.5
    zpad = jnp.zeros((BLOCK - N_META, HEAD_DIM), BF16)
    k_all = jnp.concatenate([kp_ref[...], kc_ref[...], kn_ref[...], km_ref[...], zpad], axis=0)
    v_all = jnp.concatenate([vp_ref[...], vc_ref[...], vn_ref[...], vm_ref[...], zpad], axis=0)
    q_all = jnp.concatenate([q_ref[:, g * HEAD_DIM:(g + 1) * HEAD_DIM] for g in range(group)], axis=0)
    s = _dot(q_all, k_all, NT) * scale
    rows = lax.broadcasted_iota(jnp.int32, s.shape, 0) % BLOCK
    cols = lax.broadcasted_iota(jnp.int32, s.shape, 1)
    rel = cols - BLOCK - rows
    kblk = i - 1 + cols // BLOCK
    band_ok = (cols < 3 * BLOCK) & (kblk >= 0) & (kblk < nb) & (jnp.abs(rel) <= WINDOW)
    meta_ok = (cols >= 3 * BLOCK) & (cols < 3 * BLOCK + N_META)
    s = jnp.where(band_ok | meta_ok, s, NEG_BIG)
    sink = sink_ref[:, 0:1]
    m = jnp.maximum(jnp.max(s, axis=-1, keepdims=True), sink)
    p = jnp.exp(s - m)
    denom = jnp.sum(p, axis=-1, keepdims=True) + jnp.exp(sink - m)
    o = _dot(p.astype(BF16), v_all) / denom
    for g in range(group):
        o_ref[:, g * HEAD_DIM:(g + 1) * HEAD_DIM] = o[g * BLOCK:(g + 1) * BLOCK, :].astype(o_ref.dtype)


def _attention(qkv, sink, S, n_heads, n_kv):
    group = n_heads // n_kv
    nb = S // BLOCK
    qw = group * HEAD_DIM
    kcol = n_heads
    vcol = n_heads + n_kv
    meta_blk = (S + PAD_ROWS - N_META) // N_META
    sink_rows = jnp.broadcast_to(sink.astype(F32).reshape(n_kv, group, 1, 1),
                                 (n_kv, group, BLOCK, LANES)).reshape(n_kv, group * BLOCK, LANES)

    def kv_specs(col0):
        return [pl.BlockSpec((BLOCK, HEAD_DIM), lambda h, i: (jnp.maximum(i - 1, 0), col0 + h)),
                pl.BlockSpec((BLOCK, HEAD_DIM), lambda h, i: (i, col0 + h)),
                pl.BlockSpec((BLOCK, HEAD_DIM), lambda h, i: (jnp.minimum(i + 1, nb - 1), col0 + h)),
                pl.BlockSpec((N_META, HEAD_DIM), lambda h, i: (meta_blk, col0 + h))]

    return pl.pallas_call(
        functools.partial(_attn_kernel, nb=nb, group=group),
        grid=(n_kv, nb),
        in_specs=[pl.BlockSpec((BLOCK, qw), lambda h, i: (i, h))] + kv_specs(kcol) + kv_specs(vcol)
                 + [pl.BlockSpec((None, group * BLOCK, LANES), lambda h, i: (h, 0, 0))],
        out_specs=pl.BlockSpec((BLOCK, qw), lambda h, i: (i, h)),
        out_shape=jax.ShapeDtypeStruct((S, n_heads * HEAD_DIM), BF16),
        compiler_params=_params(("arbitrary", "arbitrary"),
                                _vmem_limit(BLOCK * qw * 2 * 2, 8 * BLOCK * HEAD_DIM * 2,
                                            group * BLOCK * LANES * 4, temps=6 * group * BLOCK * 512 * 4)),
        name="attention",
    )(qkv, qkv, qkv, qkv, qkv, qkv, qkv, qkv, qkv, sink_rows)


def _rope(t, cos, sin):
    half = ROPE_DIMS // 2
    lane = lax.broadcasted_iota(jnp.int32, t.shape, 1)
    up = jnp.concatenate([t[:, half:], t[:, :half]], axis=1)
    down = jnp.concatenate([t[:, HEAD_DIM - half:], t[:, :HEAD_DIM - half]], axis=1)
    return t * cos + jnp.where(lane < half, up, down) * sin


def _attn_res_kernel(sink_ref, q_ref, k_ref, v_ref, cos_ref, sin_ref, o_ref, kb_ref, vb_ref, *, nb, group):
    h, i = pl.program_id(0), pl.program_id(1)
    lp = k_ref.shape[0]

    @pl.when(i == 0)
    def _():
        def body(c, carry):
            sl = pl.ds(pl.multiple_of(c * BLOCK, BLOCK), BLOCK)
            kb_ref[sl, :] = _rope(k_ref[sl, :], cos_ref[sl, :], sin_ref[sl, :]).astype(BF16)
            vb_ref[sl, :] = v_ref[sl, :].astype(BF16)
            return carry
        lax.fori_loop(0, lp // BLOCK, body, 0)

    kb0 = jnp.clip(i - 1, 0, nb - 3)
    win = pl.ds(pl.multiple_of(kb0 * BLOCK, BLOCK), 3 * BLOCK)
    zpad = jnp.zeros((BLOCK - N_META, HEAD_DIM), BF16)
    k_all = jnp.concatenate([kb_ref[win, :], kb_ref[lp - N_META:lp, :], zpad], axis=0)
    v_all = jnp.concatenate([vb_ref[win, :], vb_ref[lp - N_META:lp, :], zpad], axis=0)
    rows = lax.broadcasted_iota(jnp.int32, (BLOCK, 4 * BLOCK), 0)
    cols = lax.broadcasted_iota(jnp.int32, (BLOCK, 4 * BLOCK), 1)
    rel = cols - rows + (kb0 - i) * BLOCK
    ok = ((cols < 3 * BLOCK) & (jnp.abs(rel) <= WINDOW)) | ((cols >= 3 * BLOCK) & (cols < 3 * BLOCK + N_META))
    bias = jnp.where(ok, 0.0, NEG_BIG).astype(F32)
    qrows = pl.ds(pl.multiple_of(i * BLOCK, BLOCK), BLOCK)
    cos_q, sin_q = cos_ref[qrows, :], sin_ref[qrows, :]
    scale = HEAD_DIM ** -0.5
    n = range(group)
    q = [_rope(q_ref[:, g * HEAD_DIM:(g + 1) * HEAD_DIM], cos_q, sin_q).astype(BF16) for g in n]
    s = [_dot(q[g], k_all, NT) * scale + bias for g in n]
    sink = [sink_ref[h * group + g] for g in n]
    m = [jnp.maximum(jnp.max(s[g], axis=-1, keepdims=True), sink[g]) for g in n]
    p = [jnp.exp(s[g] - m[g]) for g in n]
    denom = [jnp.sum(p[g], axis=-1, keepdims=True) + jnp.exp(sink[g] - m[g]) for g in n]
    o = [_dot(p[g].astype(BF16), v_all) / denom[g] for g in n]
    for g in n:
        o_ref[:, g * HEAD_DIM:(g + 1) * HEAD_DIM] = o[g].astype(o_ref.dtype)


def _attention_res(z, cos_t, sin_t, sink, S, n_heads, n_kv, off_q):
    Lp = z.shape[0]
    group = n_heads // n_kv
    nb = S // BLOCK
    assert nb >= 3
    qw = group * HEAD_DIM
    kcol = off_q // HEAD_DIM + n_heads
    vcol = kcol + n_kv
    col_bytes = Lp * HEAD_DIM * 4
    return pl.pallas_call(
        functools.partial(_attn_res_kernel, nb=nb, group=group),
        grid_spec=pltpu.PrefetchScalarGridSpec(
            num_scalar_prefetch=1,
            grid=(n_kv, nb),
            in_specs=[pl.BlockSpec((BLOCK, qw), lambda h, i, s: (i, off_q // qw + h)),
                      pl.BlockSpec((Lp, HEAD_DIM), lambda h, i, s: (0, kcol + h)),
                      pl.BlockSpec((Lp, HEAD_DIM), lambda h, i, s: (0, vcol + h)),
                      pl.BlockSpec((Lp, HEAD_DIM), lambda h, i, s: (0, 0)),
                      pl.BlockSpec((Lp, HEAD_DIM), lambda h, i, s: (0, 0))],
            out_specs=pl.BlockSpec((BLOCK, qw), lambda h, i, s: (i, h)),
            scratch_shapes=[pltpu.VMEM((Lp, HEAD_DIM), BF16), pltpu.VMEM((Lp, HEAD_DIM), BF16)]),
        out_shape=jax.ShapeDtypeStruct((S, n_heads * HEAD_DIM), BF16),
        compiler_params=_params(("arbitrary", "arbitrary"),
                                _vmem_limit(BLOCK * qw * 4, 4 * col_bytes, BLOCK * qw * 2,
                                            temps=col_bytes + 8 * group * BLOCK * 512 * 4)),
        name="attention",
    )(sink.astype(F32), z, z, z, cos_t, sin_t)


def _split3(x):
    hi = x.astype(BF16)
    r1 = x - hi.astype(F32)
    mid = r1.astype(BF16)
    lo = (r1 - mid.astype(F32)).astype(BF16)
    return hi, mid, lo


def _split2(x):
    hi = x.astype(BF16)
    return hi, (x - hi.astype(F32)).astype(BF16)


def _dot_exact_rhs(x, m_bf16, dims=NN, left=False):
    parts = _split3(x)
    if left:
        return sum(_dot(m_bf16, p, dims) for p in parts)
    return sum(_dot(p, m_bf16, dims) for p in parts)


def _dotp(a, b, passes, dims=NN):
    if passes == 1:
        return _dot(a.astype(BF16), b.astype(BF16), dims)
    a_hi, a_lo = _split2(a)
    b_hi, b_lo = _split2(b)
    return _dot(a_hi, b_hi, dims) + (_dot(a_hi, b_lo, dims) + _dot(a_lo, b_hi, dims))


def _head_sum(x, ones_bd):
    outs = []
    for s in range(x.shape[1] // LANES):
        outs.append(_dot_exact_rhs(x[:, s * LANES:(s + 1) * LANES], ones_bd))
    return outs[0] if len(outs) == 1 else jnp.concatenate(outs, axis=1)


def _prep_kernel(r_ref, rp_ref, rn_ref, k_ref, kp_ref, kn_ref, v_ref, vp_ref, vn_ref,
                 lo_ref, lop_ref, lon_ref, gd_ref, gdp_ref, gdn_ref,
                 pc_ref, pl_ref, wl_ref, g2_ref, ones_ref,
                 ro_ref, vo_ref, kko_ref, lwf_ref, lwb_ref, kdf_ref, kdb_ref, bdf_ref, bdb_ref, go_ref,
                 buf_ref, *, n_tiles, tr, lora_tanh_cols):
    i = pl.program_id(0)

    def shifted(main_ref, prev_ref, next_ref, mup, mun):
        w = main_ref.shape[1]
        buf_ref[0:8, 0:w] = prev_ref[...]
        buf_ref[8:8 + tr, 0:w] = main_ref[...]
        buf_ref[8 + tr:16 + tr, 0:w] = next_ref[...]
        x = main_ref[...]
        prev = buf_ref[7:7 + tr, 0:w]
        nxt = buf_ref[9:9 + tr, 0:w]
        return x + mup * (prev - x) + mun * (nxt - x)

    r = shifted(r_ref, rp_ref, rn_ref, pc_ref[0:1, :], pc_ref[1:2, :])
    k = shifted(k_ref, kp_ref, kn_ref, pc_ref[2:3, :], pc_ref[3:4, :])
    v = shifted(v_ref, vp_ref, vn_ref, pc_ref[4:5, :], pc_ref[5:6, :])
    lo = shifted(lo_ref, lop_ref, lon_ref, pl_ref[0:1, 0:LORA_PAD], pl_ref[1:2, 0:LORA_PAD])
    gd = shifted(gd_ref, gdp_ref, gdn_ref, pl_ref[0:1, LORA_PAD:], pl_ref[1:2, LORA_PAD:])

    row = lax.broadcasted_iota(jnp.int32, (tr, 1), 0)
    valid = jnp.where((i < n_tiles - 1) | (row >= tr - N_META), 1.0, 0.0).astype(F32)

    lane = lax.broadcasted_iota(jnp.int32, lo.shape, 1)
    lo_act = jnp.where(lane < lora_tanh_cols, jnp.tanh(lo), lo).astype(BF16)
    dec_f = _dot(lo_act, wl_ref[0])
    dec_b = _dot(lo_act, wl_ref[1])
    apre_f = _dot(lo_act, wl_ref[2])
    apre_b = _dot(lo_act, wl_ref[3])
    g = _dot(jax.nn.sigmoid(lo).astype(BF16), g2_ref[0]) + _dot(jax.nn.sigmoid(gd).astype(BF16), g2_ref[1])

    def log_decay(dec, w0):
        w_log = -jax.nn.softplus(-(w0 + dec)) - 0.5
        return -jnp.exp(w_log)

    a_f = jax.nn.sigmoid(pc_ref[8:9, :] + apre_f)
    a_b = jax.nn.sigmoid(pc_ref[9:10, :] + apre_b)
    kk = k * pc_ref[10:11, :]
    ss = _head_sum(kk * kk, ones_ref[...])
    kk = kk / jnp.maximum(jnp.sqrt(ss), 1e-12) * valid
    k_a = pc_ref[11:12, :]
    kv = k * valid

    ro_ref[...] = r * valid
    vo_ref[...] = v * valid
    kko_ref[...] = kk
    lwf_ref[...] = log_decay(dec_f, pc_ref[6:7, :])
    lwb_ref[...] = log_decay(dec_b, pc_ref[7:8, :])
    kdf_ref[...] = kv * (1.0 + (a_f - 1.0) * k_a)
    kdb_ref[...] = kv * (1.0 + (a_b - 1.0) * k_a)
    bdf_ref[...] = kk * a_f
    bdb_ref[...] = kk * a_b
    go_ref[...] = g


def _rwkv_prep(z, off_r, C, pc, plo, wl, g2, ones_bd, lora_tanh_cols):
    Lp = z.shape[0]
    tr = BLOCK
    ct = _pick(C, (1024, 512))
    n_tiles = Lp // tr
    n8 = Lp // 8
    off_lo = off_r + 3 * C
    off_gd = off_lo + LORA_PAD
    assert off_r % ct == 0 and off_lo % LORA_PAD == 0

    def seg_specs(off, w, with_c):
        cb = off // w

        def col(c):
            return cb + c if with_c else cb
        return [pl.BlockSpec((tr, w), lambda i, c: (i, col(c))),
                pl.BlockSpec((8, w), lambda i, c: ((i * (tr // 8) + n8 - 1) % n8, col(c))),
                pl.BlockSpec((8, w), lambda i, c: (((i + 1) * (tr // 8)) % n8, col(c)))]

    in_specs = (seg_specs(off_r, ct, True) + seg_specs(off_r + C, ct, True) + seg_specs(off_r + 2 * C, ct, True)
                + seg_specs(off_lo, LORA_PAD, False) + seg_specs(off_gd, LORA_PAD, False)
                + [pl.BlockSpec((16, ct), lambda i, c: (0, c)),
                   pl.BlockSpec((8, 2 * LORA_PAD), lambda i, c: (0, 0)),
                   pl.BlockSpec((4, LORA_PAD, ct), lambda i, c: (0, 0, c)),
                   pl.BlockSpec((2, LORA_PAD, ct), lambda i, c: (0, 0, c)),
                   pl.BlockSpec((LANES, LANES), lambda i, c: (0, 0))])
    out_spec = pl.BlockSpec((tr, ct), lambda i, c: (i, c))
    out_sds = jax.ShapeDtypeStruct((Lp, C), F32)
    return pl.pallas_call(
        functools.partial(_prep_kernel, n_tiles=n_tiles, tr=tr, lora_tanh_cols=lora_tanh_cols),
        grid=(n_tiles, C // ct),
        in_specs=in_specs,
        out_specs=[out_spec] * 10,
        out_shape=[out_sds] * 10,
        scratch_shapes=[pltpu.VMEM((tr + 16, max(ct, LORA_PAD)), F32)],
        compiler_params=_params(("arbitrary", "arbitrary"),
                                _vmem_limit(3 * tr * ct * 4, 2 * tr * LORA_PAD * 4, 4 * LORA_PAD * ct * 2,
                                            2 * LORA_PAD * ct * 2, 10 * tr * ct * 4, temps=24 * tr * ct * 4)),
        name="rwkv_prep",
    )(*([z] * 15), pc, plo, wl, g2, ones_bd)


SCAN_PAIRS = 4


def _scan_consts():
    C = CHUNK
    ti = lax.broadcasted_iota(jnp.int32, (C, C), 0)
    si = lax.broadcasted_iota(jnp.int32, (C, C), 1)
    row = lax.broadcasted_iota(jnp.int32, (C, 2 * C), 0)
    col = lax.broadcasted_iota(jnp.int32, (C, 2 * C), 1) % C
    rr = lax.broadcasted_iota(jnp.int32, (LANES, LANES), 0)
    cc = lax.broadcasted_iota(jnp.int32, (LANES, LANES), 1)
    return dict(
        cum_f=jnp.where(si <= ti, 1.0, 0.0).astype(BF16), cum_b=jnp.where(si >= ti, 1.0, 0.0).astype(BF16),
        head0=lax.broadcasted_iota(jnp.int32, (C, LANES), 1) < RWKV_HEAD,
        strict_f=col < row, strict_b=col > row, incl_f=col <= row, incl_b=col >= row,
        eye_sbs=jnp.where(col == row, 1.0, 0.0).astype(F32),
        same_head=(rr // RWKV_HEAD) == (cc // RWKV_HEAD), eye=rr == cc)


def _scan_chains(chains, cst):
    C = CHUNK
    head0, same, eye = cst["head0"], cst["same_head"], cst["eye"]
    n = range(len(chains))
    rev = [c[7] for c in chains]
    r, v, al, lw, k, be, h = ([c[i] for c in chains] for i in range(7))
    strict = [cst["strict_b" if x else "strict_f"] for x in rev]
    incl = [cst["incl_b" if x else "incl_f"] for x in rev]

    def bd(x):
        xb = x.astype(BF16)
        zero = jnp.zeros_like(xb)
        return jnp.concatenate([jnp.where(head0, xb, zero), jnp.where(head0, zero, xb)], axis=0)

    def mm(a, b_bf16, dims=NN):
        return _dot(a.astype(BF16), b_bf16, dims)

    cl = [_dot_exact_rhs(lw[i], cst["cum_b" if rev[i] else "cum_f"], NN, left=True) for i in n]
    total = [cl[i][0:1, :] if rev[i] else cl[i][C - 1:C, :] for i in n]
    a_t = [al[i] * jnp.exp(cl[i] - lw[i]) for i in n]
    r_t = [r[i] * jnp.exp(cl[i]) for i in n]
    w_inv = [jnp.exp(-cl[i]) for i in n]
    w_rest = [jnp.exp(total[i] - cl[i]) for i in n]
    bd_v = [bd(v[i]) for i in n]
    sc = [mm(jnp.concatenate([a_t[i], r_t[i]], axis=0),
             jnp.concatenate([bd(k[i] * w_inv[i]), bd(be[i] * w_inv[i])], axis=0), NT) for i in n]
    a_ak = [jnp.where(strict[i], sc[i][0:C, 0:2 * C], 0.0) for i in n]
    n_ab = [jnp.where(strict[i], sc[i][0:C, 2 * C:], 0.0) for i in n]
    a_rk = [jnp.where(incl[i], sc[i][C:, 0:2 * C], 0.0) for i in n]
    a_rb = [jnp.where(incl[i], sc[i][C:, 2 * C:], 0.0) for i in n]

    t_inv = [cst["eye_sbs"] - n_ab[i] for i in n]
    sq = n_ab
    n_round = 1
    while (1 << n_round) < C:
        sq = [mm(sq[i], bd(sq[i])) for i in n]
        t_inv = [t_inv[i] + mm(t_inv[i], bd(sq[i])) for i in n]
        n_round += 1

    av = [mm(a_ak[i], bd_v[i]) for i in n]
    tp = [mm(t_inv[i], jnp.concatenate([bd(av[i]), bd(a_t[i])], axis=1)) for i in n]
    p0 = [tp[i][:, 0:LANES] for i in n]
    at = [tp[i][:, LANES:] for i in n]
    y0 = [mm(jnp.concatenate([a_rk[i], a_rb[i]], axis=1), jnp.concatenate([bd_v[i], bd(-p0[i])], axis=0)) for i in n]
    rh = [r_t[i] - mm(a_rb[i], bd(at[i])) for i in n]
    bt = [mm(be[i] * w_rest[i], jnp.concatenate([at[i], p0[i]], axis=1).astype(BF16), TN) for i in n]
    kv = [mm(k[i] * w_rest[i], v[i].astype(BF16), TN) for i in n]
    m_mat = [jnp.where(same, jnp.where(eye, jnp.broadcast_to(jnp.exp(total[i]), (LANES, LANES)), 0.0)
                       - bt[i][:, 0:LANES], 0.0) for i in n]
    g_mat = [jnp.where(same, kv[i] - bt[i][:, LANES:], 0.0) for i in n]
    h_b = [h[i].astype(BF16) for i in n]
    y = [mm(rh[i], h_b[i]) + y0[i] for i in n]
    h_new = [mm(m_mat[i], h_b[i]) + g_mat[i] for i in n]
    return list(zip(y, h_new))


def _scan_kernel(rf_ref, vf_ref, af_ref, lwf_ref, kf_ref, bf_ref,
                 rb_ref, vb_ref, ab_ref, lwb_ref, kb_ref, bb_ref,
                 yf_ref, yb_ref, hf_ref, hb_ref, *, pairs):
    @pl.when(pl.program_id(1) == 0)
    def _():
        hf_ref[...] = jnp.zeros(hf_ref.shape, F32)
        hb_ref[...] = jnp.zeros(hb_ref.shape, F32)

    cst = _scan_consts()
    chains = []
    for g in range(pairs):
        sl = slice(g * LANES, (g + 1) * LANES)
        chains.append((rf_ref[:, sl], vf_ref[:, sl], af_ref[:, sl], lwf_ref[:, sl], kf_ref[:, sl], bf_ref[:, sl],
                       hf_ref[g], False))
        chains.append((rb_ref[:, sl], vb_ref[:, sl], ab_ref[:, sl], lwb_ref[:, sl], kb_ref[:, sl], bb_ref[:, sl],
                       hb_ref[g], True))
    res = _scan_chains(chains, cst)
    for g in range(pairs):
        sl = slice(g * LANES, (g + 1) * LANES)
        yf_ref[:, sl], hf_ref[g] = res[2 * g]
        yb_ref[:, sl], hb_ref[g] = res[2 * g + 1]


def _rwkv_scan(r, v, kk, lw_f, lw_b, kd_f, kd_b, bd_f, bd_b, S):
    Lp, C = r.shape
    n_real = S // CHUNK
    n_chunks = Lp // CHUNK
    steps = n_real + 1
    pairs = _pick(C // LANES, (SCAN_PAIRS, 2, 1))
    w = pairs * LANES

    def fwd(p, s):
        return ((s + n_real + 1) % n_chunks, p)

    def bwd(p, s):
        return (n_real - s, p)

    spec_f = pl.BlockSpec((CHUNK, w), fwd)
    spec_b = pl.BlockSpec((CHUNK, w), bwd)
    out_sds = jax.ShapeDtypeStruct((Lp, C), F32)
    return pl.pallas_call(
        functools.partial(_scan_kernel, pairs=pairs),
        grid=(C // w, steps),
        in_specs=[spec_f] * 6 + [spec_b] * 6,
        out_specs=[spec_f, spec_b],
        out_shape=[out_sds, out_sds],
        scratch_shapes=[pltpu.VMEM((pairs, LANES, LANES), F32), pltpu.VMEM((pairs, LANES, LANES), F32)],
        compiler_params=_params(("arbitrary", "arbitrary"), 32 << 20),
        name="rwkv_scan",
    )(r, v, kk, lw_f, kd_f, bd_f, r, v, kk, lw_b, kd_b, bd_b)


def _post_kernel(yf_ref, yb_ref, r_ref, kdf_ref, kdb_ref, v_ref, g_ref, pq_ref, ones_ref, o_ref):
    ones_bd = ones_ref[...]
    inv_n = 1.0 / RWKV_HEAD
    y = yf_ref[...] + yb_ref[...]
    mean = _head_sum(y, ones_bd) * inv_n
    yc = y - mean
    var = _head_sum(yc * yc, ones_bd) * inv_n
    yn = yc * lax.rsqrt(var + GN_EPS) * pq_ref[0:1, :] + pq_ref[1:2, :]
    bonus = _head_sum(r_ref[...] * (kdf_ref[...] + kdb_ref[...]) * pq_ref[2:3, :], ones_bd) * v_ref[...]
    o_ref[...] = ((yn + bonus) * g_ref[...]).astype(o_ref.dtype)


def _rwkv_post(y_f, y_b, r, kd_f, kd_b, v, g, pq, ones_bd, S):
    C = r.shape[1]
    tr = 256
    ct = _pick(C, (1024, 512))
    spec = pl.BlockSpec((tr, ct), lambda i, c: (i, c))
    return pl.pallas_call(
        _post_kernel,
        grid=(S // tr, C // ct),
        in_specs=[spec] * 7 + [pl.BlockSpec((8, ct), lambda i, c: (0, c)),
                               pl.BlockSpec((LANES, LANES), lambda i, c: (0, 0))],
        out_specs=spec,
        out_shape=jax.ShapeDtypeStruct((S, C), BF16),
        compiler_params=_params(("arbitrary", "arbitrary"),
                                _vmem_limit(7 * tr * ct * 4, tr * ct * 2, temps=16 * tr * ct * 4)),
        name="rwkv_post",
    )(y_f, y_b, r, kd_f, kd_b, v, g, pq, ones_bd)


def _pad_cols(w, width):
    return jnp.pad(w, ((0, 0), (0, width - w.shape[1])))


def kernel(x, meta_tokens, norm_mix_w, w_in, b_gate, mu_prev, mu_next, dec_w0, dec_w2, iclr_a0, iclr_a2,
           gate_w2, k_k, k_a, r_k, ln_x_w, ln_x_b, attn_sink, w_proj_attn, w_proj_rwkv, w_out, norm_ffn_w,
           w_ffn_gate, w_ffn_up, w_ffn_down, norm_final_w):
    B, S, D = x.shape
    assert B == 1 and norm_mix_w.shape[0] == 1, "one sequence, one layer"
    C = k_k.shape[-1]
    AW = w_proj_attn.shape[1]
    n_heads = attn_sink.shape[-1]
    DL, IL, GL = dec_w2.shape[2], iclr_a2.shape[2], gate_w2.shape[1]
    shift_w = mu_prev.shape[-1]
    KVW = (w_in.shape[-1] - 2 * D - AW - shift_w) // 2
    n_kv = KVW // HEAD_DIM
    F = w_ffn_gate.shape[-1]
    lora_w = 2 * DL + 2 * IL
    assert shift_w == 3 * C + lora_w + GL and lora_w <= LORA_PAD and GL <= LORA_PAD
    assert S % BLOCK == 0 and C % LANES == 0 and n_heads * HEAD_DIM == AW
    Lp = S + PAD_ROWS

    off_q = 2 * D
    off_r = off_q + AW + 2 * KVW
    off_lo = off_r + 3 * C
    tn_in = 512
    n_in_pad = -(-w_in.shape[-1] // tn_in) * tn_in
    assert n_in_pad >= off_lo + 2 * LORA_PAD and off_lo % LORA_PAD == 0
    low_w = lora_w + GL
    gl_a = LORA_PAD - lora_w

    mu_p, mu_n = mu_prev[0], mu_next[0]
    zc = jnp.zeros((C,), F32)
    pc = jnp.stack([mu_p[:C], mu_n[:C], mu_p[C:2 * C], mu_n[C:2 * C], mu_p[2 * C:3 * C], mu_n[2 * C:3 * C],
                    dec_w0[0, 0], dec_w0[0, 1], iclr_a0[0, 0], iclr_a0[0, 1], k_k[0], k_a[0], zc, zc, zc, zc])
    plo = jnp.zeros((8, 2 * LORA_PAD), F32).at[0, :low_w].set(mu_p[3 * C:]).at[1, :low_w].set(mu_n[3 * C:])
    wl = jnp.zeros((4, LORA_PAD, C), F32)
    wl = wl.at[0, 0:DL].set(dec_w2[0, 0]).at[1, DL:2 * DL].set(dec_w2[0, 1])
    wl = wl.at[2, 2 * DL:2 * DL + IL].set(iclr_a2[0, 0]).at[3, 2 * DL + IL:lora_w].set(iclr_a2[0, 1])
    wl = wl.astype(BF16)
    g2 = jnp.zeros((2, LORA_PAD, C), F32).at[0, lora_w:].set(gate_w2[0, :gl_a]).at[1, :GL - gl_a].set(
        gate_w2[0, gl_a:]).astype(BF16)
    pq = jnp.zeros((8, C), F32).at[0].set(ln_x_w[0]).at[1].set(ln_x_b[0]).at[2].set(r_k[0])
    lane = jnp.arange(LANES)
    ones_bd = (lane[:, None] // RWKV_HEAD == lane[None, :] // RWKV_HEAD).astype(BF16)

    w_d = w_ffn_down[0].astype(BF16)

    pos = jnp.concatenate([jnp.arange(N_META, N_META + S), jnp.zeros((PAD_ROWS - N_META,), jnp.int32),
                           jnp.arange(N_META)]).astype(F32)
    inv = ROPE_THETA ** (-jnp.arange(0, ROPE_DIMS, 2, dtype=F32) / ROPE_DIMS)
    ang = pos[:, None] * inv[None, :]
    ones_rest = jnp.ones((Lp, HEAD_DIM - ROPE_DIMS), F32)
    cos_t = jnp.concatenate([jnp.cos(ang), jnp.cos(ang), ones_rest], axis=1)
    sin_t = jnp.concatenate([-jnp.sin(ang), jnp.sin(ang), 0.0 * ones_rest], axis=1)

    x2 = x[0]
    h_ext = _norm_ext(x2, meta_tokens.astype(x.dtype), norm_mix_w[0])
    z = _mm_in(h_ext, w_in[0], _pick(Lp, (1040, 640)), tn_in)
    o_att = _attention_res(z, cos_t, sin_t, attn_sink[0], S, n_heads, n_kv, off_q)
    r_s, v_s, kk, lw_f, lw_b, kd_f, kd_b, bd_f, bd_b, g = _rwkv_prep(
        z, off_r, C, pc, plo, wl, g2, ones_bd, 2 * DL)
    y_f, y_b = _rwkv_scan(r_s, v_s, kk, lw_f, lw_b, kd_f, kd_b, bd_f, bd_b, S)
    o_rwkv = _rwkv_post(y_f, y_b, r_s, kd_f, kd_b, v_s, g, pq, ones_bd, S)
    tm = _pick(S, (1024, 512))
    mixed = _mm_mix(o_att, w_proj_attn[0], o_rwkv, w_proj_rwkv[0], z, b_gate[0], S, D, tm, 512)
    h1 = _mm_resw(mixed, w_out[0], x2, tm, 512)
    hn = _norm(h1, norm_ffn_w[0], BF16)
    act = _mm_glu(hn, w_ffn_gate[0], w_ffn_up[0], tm, 256)
    h2 = _mm_res(act, w_d, h1, tm, 512, F // 2)
    y = _norm(h2, norm_final_w, x.dtype)
    return y[None]
```

```python
import functools
import math

import jax
import jax.numpy as jnp
from jax import lax
from jax.experimental import pallas as pl
from jax.experimental.pallas import tpu as pltpu

F32 = jnp.float32
BF16 = jnp.bfloat16

N_META = 16
HEAD_DIM = 128
WINDOW = 128
BLOCK = 128
ROPE_DIMS = HEAD_DIM // 4
ROPE_THETA = 500000.0
RWKV_HEAD = 64
RMS_EPS = 1e-6
GN_EPS = 64e-5
LANES = 128
PAD_ROWS = BLOCK
CHUNK = 64
LORA_PAD = 512
VMEM_PHYSICAL = 64 * 1024 * 1024

NN = (((1,), (0,)), ((), ()))
NT = (((1,), (1,)), ((), ()))
TN = (((0,), (0,)), ((), ()))


def _vmem_limit(*block_bytes, temps=0):
    need = 2 * sum(block_bytes) + temps + (4 << 20)
    return int(min(max(need, 16 << 20), VMEM_PHYSICAL - (6 << 20)))


def _params(sem, limit):
    return pltpu.CompilerParams(dimension_semantics=sem, vmem_limit_bytes=limit)


def _pick(n, cands):
    for c in cands:
        if n % c == 0:
            return c
    raise ValueError(f"no tile in {cands} divides {n}")


def _rms(x, w):
    return x * lax.rsqrt(jnp.mean(x * x, axis=-1, keepdims=True) + RMS_EPS) * w


def _norm_ext_kernel(x_ref, meta_ref, w_ref, o_ref, *, nb):
    i = pl.program_id(0)

    @pl.when(i < nb)
    def _():
        o_ref[...] = _rms(x_ref[...], w_ref[...]).astype(o_ref.dtype)

    @pl.when(i == nb)
    def _():
        o_ref[...] = jnp.zeros(o_ref.shape, o_ref.dtype)
        o_ref[PAD_ROWS - N_META:, :] = _rms(meta_ref[...], w_ref[...]).astype(o_ref.dtype)


def _norm_ext(x, meta, w):
    S, D = x.shape
    nb = S // BLOCK
    return pl.pallas_call(
        functools.partial(_norm_ext_kernel, nb=nb),
        grid=(nb + 1,),
        in_specs=[pl.BlockSpec((BLOCK, D), lambda i: (jnp.minimum(i, nb - 1), 0)),
                  pl.BlockSpec((N_META, D), lambda i: (0, 0)),
                  pl.BlockSpec((1, D), lambda i: (0, 0))],
        out_specs=pl.BlockSpec((BLOCK, D), lambda i: (i, 0)),
        out_shape=jax.ShapeDtypeStruct((S + PAD_ROWS, D), BF16),
        compiler_params=_params(("arbitrary",), _vmem_limit(BLOCK * D * 4, BLOCK * D * 2)),
        name="norm_ext",
    )(x, meta, w.reshape(1, D))


def _norm_kernel(x_ref, w_ref, o_ref):
    o_ref[...] = _rms(x_ref[...], w_ref[...]).astype(o_ref.dtype)


def _norm(x, w, out_dtype, tr=256):
    S, D = x.shape
    return pl.pallas_call(
        _norm_kernel,
        grid=(S // tr,),
        in_specs=[pl.BlockSpec((tr, D), lambda i: (i, 0)), pl.BlockSpec((1, D), lambda i: (0, 0))],
        out_specs=pl.BlockSpec((tr, D), lambda i: (i, 0)),
        out_shape=jax.ShapeDtypeStruct((S, D), out_dtype),
        compiler_params=_params(("arbitrary",), _vmem_limit(tr * D * 4, tr * D * 4)),
        name="norm",
    )(x, w.reshape(1, D))


def _dot(a, b, dims=NN):
    return lax.dot_general(a, b, dims, preferred_element_type=F32)


def _cast_resident(src_ref, dst_ref, col0=None, n_valid=None):
    K = src_ref.shape[0]
    rows = _pick(K, (256, 128))

    def body(c, carry):
        sl = pl.ds(pl.multiple_of(c * rows, rows), rows)
        blk = src_ref[sl, :]
        if n_valid is not None:
            col = col0 + lax.broadcasted_iota(jnp.int32, blk.shape, 1)
            blk = jnp.where(col < n_valid, blk, 0.0)
        dst_ref[sl, :] = blk.astype(dst_ref.dtype)
        return carry

    lax.fori_loop(0, K // rows, body, 0)


def _mm_in_kernel(a_ref, b_ref, o_ref, bw_ref, *, n_valid):
    j, i = pl.program_id(0), pl.program_id(1)

    @pl.when(i == 0)
    def _():
        _cast_resident(b_ref, bw_ref, j * b_ref.shape[1], n_valid)

    o_ref[...] = _dot(a_ref[...], bw_ref[...])


def _mm_in(a, w, tm, tn):
    M, K = a.shape
    N = w.shape[1]
    nj = pl.cdiv(N, tn)
    return pl.pallas_call(
        functools.partial(_mm_in_kernel, n_valid=N),
        grid=(nj, M // tm),
        in_specs=[pl.BlockSpec((tm, K), lambda j, i: (i, 0)), pl.BlockSpec((K, tn), lambda j, i: (0, j))],
        out_specs=pl.BlockSpec((tm, tn), lambda j, i: (i, j)),
        out_shape=jax.ShapeDtypeStruct((M, nj * tn), F32),
        scratch_shapes=[pltpu.VMEM((K, tn), BF16)],
        compiler_params=_params(("arbitrary", "arbitrary"),
                                _vmem_limit(tm * K * 2, K * tn * 4, tm * tn * 4, temps=K * tn * 2 + tm * tn * 4)),
        name="mm_in",
    )(a, w)


def _mm_mix_kernel(a1_ref, b1_ref, a2_ref, b2_ref, z0_ref, z1_ref, bg_ref, o_ref, w1_ref, w2_ref):
    @pl.when(pl.program_id(1) == 0)
    def _():
        _cast_resident(b1_ref, w1_ref)
        _cast_resident(b2_ref, w2_ref)

    g0 = jax.nn.sigmoid(z0_ref[...] + bg_ref[0:1, :])
    g1 = jax.nn.sigmoid(z1_ref[...] + bg_ref[1:2, :])
    o_ref[...] = (g0 * _dot(a1_ref[...], w1_ref[...]) + g1 * _dot(a2_ref[...], w2_ref[...])).astype(o_ref.dtype)


def _mm_mix(o_att, w_pa, o_rwkv, w_pr, z, b_gate, S, D, tm, tn):
    K1, K2 = o_att.shape[1], o_rwkv.shape[1]
    nj = D // tn
    return pl.pallas_call(
        _mm_mix_kernel,
        grid=(nj, S // tm),
        in_specs=[pl.BlockSpec((tm, K1), lambda j, i: (i, 0)), pl.BlockSpec((K1, tn), lambda j, i: (0, j)),
                  pl.BlockSpec((tm, K2), lambda j, i: (i, 0)), pl.BlockSpec((K2, tn), lambda j, i: (0, j)),
                  pl.BlockSpec((tm, tn), lambda j, i: (i, j)), pl.BlockSpec((tm, tn), lambda j, i: (i, nj + j)),
                  pl.BlockSpec((2, tn), lambda j, i: (0, j))],
        out_specs=pl.BlockSpec((tm, tn), lambda j, i: (i, j)),
        out_shape=jax.ShapeDtypeStruct((S, D), BF16),
        scratch_shapes=[pltpu.VMEM((K1, tn), BF16), pltpu.VMEM((K2, tn), BF16)],
        compiler_params=_params(("arbitrary", "arbitrary"),
                                _vmem_limit(tm * K1 * 2, K1 * tn * 4, tm * K2 * 2, K2 * tn * 4,
                                            2 * tm * tn * 4, tm * tn * 2,
                                            temps=(K1 + K2) * tn * 2 + 3 * tm * tn * 4)),
        name="mm_mix",
    )(o_att, w_pa, o_rwkv, w_pr, z, z, b_gate)


def _mm_resw_kernel(a_ref, b_ref, r_ref, o_ref, bw_ref):
    @pl.when(pl.program_id(1) == 0)
    def _():
        _cast_resident(b_ref, bw_ref)

    o_ref[...] = r_ref[...] + _dot(a_ref[...], bw_ref[...])


def _mm_resw(a, w, res, tm, tn):
    M, K = a.shape
    N = w.shape[1]
    return pl.pallas_call(
        _mm_resw_kernel,
        grid=(N // tn, M // tm),
        in_specs=[pl.BlockSpec((tm, K), lambda j, i: (i, 0)), pl.BlockSpec((K, tn), lambda j, i: (0, j)),
                  pl.BlockSpec((tm, tn), lambda j, i: (i, j))],
        out_specs=pl.BlockSpec((tm, tn), lambda j, i: (i, j)),
        out_shape=jax.ShapeDtypeStruct((M, N), F32),
        scratch_shapes=[pltpu.VMEM((K, tn), BF16)],
        compiler_params=_params(("arbitrary", "arbitrary"),
                                _vmem_limit(tm * K * 2, K * tn * 4, 2 * tm * tn * 4, temps=K * tn * 2 + tm * tn * 4)),
        name="mm_out",
    )(a, w, res)


def _mm_res_kernel(a_ref, b_ref, r_ref, o_ref, *, nk):
    k = pl.program_id(2)
    part = _dot(a_ref[...], b_ref[...])
    if nk == 1:
        o_ref[...] = r_ref[...] + part
    else:
        @pl.when(k == 0)
        def _():
            o_ref[...] = r_ref[...] + part

        @pl.when(k > 0)
        def _():
            o_ref[...] += part


def _mm_res(a, b, res, tm, tn, tk):
    M, K = a.shape
    N = b.shape[1]
    nk = K // tk
    return pl.pallas_call(
        functools.partial(_mm_res_kernel, nk=nk),
        grid=(M // tm, N // tn, nk),
        in_specs=[pl.BlockSpec((tm, tk), lambda i, j, k: (i, k)), pl.BlockSpec((tk, tn), lambda i, j, k: (k, j)),
                  pl.BlockSpec((tm, tn), lambda i, j, k: (i, j))],
        out_specs=pl.BlockSpec((tm, tn), lambda i, j, k: (i, j)),
        out_shape=jax.ShapeDtypeStruct((M, N), F32),
        compiler_params=_params(("arbitrary", "arbitrary", "arbitrary"),
                                _vmem_limit(tm * tk * 2, tk * tn * 2, 2 * tm * tn * 4, temps=tm * tn * 4)),
        name="mm_down",
    )(a, b, res)


def _mm_glu_kernel(a_ref, bg_ref, bu_ref, o_ref, wg_ref, wu_ref):
    @pl.when(pl.program_id(1) == 0)
    def _():
        _cast_resident(bg_ref, wg_ref)
        _cast_resident(bu_ref, wu_ref)

    a = a_ref[...]
    gate = _dot(a, wg_ref[...])
    up = _dot(a, wu_ref[...])
    o_ref[...] = (gate * jax.nn.sigmoid(gate) * up).astype(o_ref.dtype)


def _mm_glu(a, wg, wu, tm, tn):
    M, K = a.shape
    N = wg.shape[1]
    return pl.pallas_call(
        _mm_glu_kernel,
        grid=(N // tn, M // tm),
        in_specs=[pl.BlockSpec((tm, K), lambda j, i: (i, 0)), pl.BlockSpec((K, tn), lambda j, i: (0, j)),
                  pl.BlockSpec((K, tn), lambda j, i: (0, j))],
        out_specs=pl.BlockSpec((tm, tn), lambda j, i: (i, j)),
        out_shape=jax.ShapeDtypeStruct((M, N), BF16),
        scratch_shapes=[pltpu.VMEM((K, tn), BF16), pltpu.VMEM((K, tn), BF16)],
        compiler_params=_params(("arbitrary", "arbitrary"),
                                _vmem_limit(tm * K * 2, 2 * K * tn * 4, tm * tn * 2,
                                            temps=2 * K * tn * 2 + 3 * tm * tn * 4)),
        name="mm_glu",
    )(a, wg, wu)


def _rope_kernel(z_ref, cos_ref, sin_ref, o_ref, *, n_rope_blocks, heads_per_block, k_heads_last):
    j = pl.program_id(1)
    cos = cos_ref[...]
    sin = sin_ref[...]
    lane = lax.broadcasted_iota(jnp.int32, cos.shape, 1)
    half = ROPE_DIMS // 2
    for h in range(heads_per_block):
        t = z_ref[:, h * HEAD_DIM:(h + 1) * HEAD_DIM]
        up = jnp.concatenate([t[:, half:], t[:, :half]], axis=1)
        down = jnp.concatenate([t[:, HEAD_DIM - half:], t[:, :HEAD_DIM - half]], axis=1)
        partner = jnp.where(lane < half, up, down)
        rot = t * cos + partner * sin
        if h >= k_heads_last:
            rot = jnp.where(j == n_rope_blocks - 1, t, rot)
        o_ref[:, h * HEAD_DIM:(h + 1) * HEAD_DIM] = rot.astype(o_ref.dtype)


def _rope_cast(z, cos_t, sin_t, off_q, width, kv_width):
    Lp = z.shape[0]
    tc = 2 * kv_width
    assert off_q % tc == 0 and width % tc == 0
    nblk = width // tc
    return pl.pallas_call(
        functools.partial(_rope_kernel, n_rope_blocks=nblk, heads_per_block=tc // HEAD_DIM,
                          k_heads_last=kv_width // HEAD_DIM),
        grid=(Lp // BLOCK, nblk),
        in_specs=[pl.BlockSpec((BLOCK, tc), lambda i, j: (i, off_q // tc + j)),
                  pl.BlockSpec((BLOCK, HEAD_DIM), lambda i, j: (i, 0)),
                  pl.BlockSpec((BLOCK, HEAD_DIM), lambda i, j: (i, 0))],
        out_specs=pl.BlockSpec((BLOCK, tc), lambda i, j: (i, j)),
        out_shape=jax.ShapeDtypeStruct((Lp, width), BF16),
        compiler_params=_params(("arbitrary", "arbitrary"), _vmem_limit(BLOCK * tc * 4, BLOCK * tc * 2)),
        name="rope_cast",
    )(z, cos_t, sin_t)


NEG_BIG = -1e30


def _attn_kernel(q_ref, kp_ref, kc_ref, kn_ref, km_ref, vp_ref, vc_ref, vn_ref, vm_ref, sink_ref, o_ref,
                 *, nb, group):
    i = pl.program_id(1)
    scale = HEAD_DIM ** -0.5
    zpad = jnp.zeros((BLOCK - N_META, HEAD_DIM), BF16)
    k_all = jnp.concatenate([kp_ref[...], kc_ref[...], kn_ref[...], km_ref[...], zpad], axis=0)
    v_all = jnp.concatenate([vp_ref[...], vc_ref[...], vn_ref[...], vm_ref[...], zpad], axis=0)
    q_all = jnp.concatenate([q_ref[:, g * HEAD_DIM:(g + 1) * HEAD_DIM] for g in range(group)], axis=0)
    s = _dot(q_all, k_all, NT) * scale
    rows = lax.broadcasted_iota(jnp.int32, s.shape, 0) % BLOCK
    cols = lax.broadcasted_iota(jnp.int32, s.shape, 1)
    rel = cols - BLOCK - rows
    kblk = i - 1 + cols // BLOCK
    band_ok = (cols < 3 * BLOCK) & (kblk >= 0) & (kblk < nb) & (jnp.abs(rel) <= WINDOW)
    meta_ok = (cols >= 3 * BLOCK) & (cols < 3 * BLOCK + N_META)
    s = jnp.where(band_ok | meta_ok, s, NEG_BIG)
    sink = sink_ref[:, 0:1]
    m = jnp.maximum(jnp.max(s, axis=-1, keepdims=True), sink)
    p = jnp.exp(s - m)
    denom = jnp.sum(p, axis=-1, keepdims=True) + jnp.exp(sink - m)
    o = _dot(p.astype(BF16), v_all) / denom
    for g in range(group):
        o_ref[:, g * HEAD_DIM:(g + 1) * HEAD_DIM] = o[g * BLOCK:(g + 1) * BLOCK, :].astype(o_ref.dtype)


def _attention(qkv, sink, S, n_heads, n_kv):
    group = n_heads // n_kv
    nb = S // BLOCK
    qw = group * HEAD_DIM
    kcol = n_heads
    vcol = n_heads + n_kv
    meta_blk = (S + PAD_ROWS - N_META) // N_META
    sink_rows = jnp.broadcast_to(sink.astype(F32).reshape(n_kv, group, 1, 1),
                                 (n_kv, group, BLOCK, LANES)).reshape(n_kv, group * BLOCK, LANES)

    def kv_specs(col0):
        return [pl.BlockSpec((BLOCK, HEAD_DIM), lambda h, i: (jnp.maximum(i - 1, 0), col0 + h)),
                pl.BlockSpec((BLOCK, HEAD_DIM), lambda h, i: (i, col0 + h)),
                pl.BlockSpec((BLOCK, HEAD_DIM), lambda h, i: (jnp.minimum(i + 1, nb - 1), col0 + h)),
                pl.BlockSpec((N_META, HEAD_DIM), lambda h, i: (meta_blk, col0 + h))]

    return pl.pallas_call(
        functools.partial(_attn_kernel, nb=nb, group=group),
        grid=(n_kv, nb),
        in_specs=[pl.BlockSpec((BLOCK, qw), lambda h, i: (i, h))] + kv_specs(kcol) + kv_specs(vcol)
                 + [pl.BlockSpec((None, group * BLOCK, LANES), lambda h, i: (h, 0, 0))],
        out_specs=pl.BlockSpec((BLOCK, qw), lambda h, i: (i, h)),
        out_shape=jax.ShapeDtypeStruct((S, n_heads * HEAD_DIM), BF16),
        compiler_params=_params(("arbitrary", "arbitrary"),
                                _vmem_limit(BLOCK * qw * 2 * 2, 8 * BLOCK * HEAD_DIM * 2,
                                            group * BLOCK * LANES * 4, temps=6 * group * BLOCK * 512 * 4)),
        name="attention",
    )(qkv, qkv, qkv, qkv, qkv, qkv, qkv, qkv, qkv, sink_rows)


def _rope(t, cos, sin):
    half = ROPE_DIMS // 2
    lane = lax.broadcasted_iota(jnp.int32, t.shape, 1)
    up = jnp.concatenate([t[:, half:], t[:, :half]], axis=1)
    down = jnp.concatenate([t[:, HEAD_DIM - half:], t[:, :HEAD_DIM - half]], axis=1)
    return t * cos + jnp.where(lane < half, up, down) * sin


def _attn_res_kernel(sink_ref, q_ref, k_ref, v_ref, cos_ref, sin_ref, o_ref, kb_ref, vb_ref, *, nb, group):
    h, i = pl.program_id(0), pl.program_id(1)
    lp = k_ref.shape[0]

    @pl.when(i == 0)
    def _():
        def body(c, carry):
            sl = pl.ds(pl.multiple_of(c * BLOCK, BLOCK), BLOCK)
            kb_ref[sl, :] = _rope(k_ref[sl, :], cos_ref[sl, :], sin_ref[sl, :]).astype(BF16)
            vb_ref[sl, :] = v_ref[sl, :].astype(BF16)
            return carry
        lax.fori_loop(0, lp // BLOCK, body, 0)

    kb0 = jnp.clip(i - 1, 0, nb - 3)
    win = pl.ds(pl.multiple_of(kb0 * BLOCK, BLOCK), 3 * BLOCK)
    zpad = jnp.zeros((BLOCK - N_META, HEAD_DIM), BF16)
    k_all = jnp.concatenate([kb_ref[win, :], kb_ref[lp - N_META:lp, :], zpad], axis=0)
    v_all = jnp.concatenate([vb_ref[win, :], vb_ref[lp - N_META:lp, :], zpad], axis=0)
    rows = lax.broadcasted_iota(jnp.int32, (BLOCK, 4 * BLOCK), 0)
    cols = lax.broadcasted_iota(jnp.int32, (BLOCK, 4 * BLOCK), 1)
    rel = cols - rows + (kb0 - i) * BLOCK
    ok = ((cols < 3 * BLOCK) & (jnp.abs(rel) <= WINDOW)) | ((cols >= 3 * BLOCK) & (cols < 3 * BLOCK + N_META))
    bias = jnp.where(ok, 0.0, NEG_BIG).astype(F32)
    qrows = pl.ds(pl.multiple_of(i * BLOCK, BLOCK), BLOCK)
    cos_q, sin_q = cos_ref[qrows, :], sin_ref[qrows, :]
    scale = HEAD_DIM ** -0.5
    n = range(group)
    q = [_rope(q_ref[:, g * HEAD_DIM:(g + 1) * HEAD_DIM], cos_q, sin_q).astype(BF16) for g in n]
    s = [_dot(q[g], k_all, NT) * scale + bias for g in n]
    sink = [sink_ref[h * group + g] for g in n]
    m = [jnp.maximum(jnp.max(s[g], axis=-1, keepdims=True), sink[g]) for g in n]
    p = [jnp.exp(s[g] - m[g]) for g in n]
    denom = [jnp.sum(p[g], axis=-1, keepdims=True) + jnp.exp(sink[g] - m[g]) for g in n]
    o = [_dot(p[g].astype(BF16), v_all) / denom[g] for g in n]
    for g in n:
        o_ref[:, g * HEAD_DIM:(g + 1) * HEAD_DIM] = o[g].astype(o_ref.dtype)


def _attention_res(z, cos_t, sin_t, sink, S, n_heads, n_kv, off_q):
    Lp = z.shape[0]
    group = n_heads // n_kv
    nb = S // BLOCK
    assert nb >= 3
    qw = group * HEAD_DIM
    kcol = off_q // HEAD_DIM + n_heads
    vcol = kcol + n_kv
    col_bytes = Lp * HEAD_DIM * 4
    return pl.pallas_call(
        functools.partial(_attn_res_kernel, nb=nb, group=group),
        grid_spec=pltpu.PrefetchScalarGridSpec(
            num_scalar_prefetch=1,
            grid=(n_kv, nb),
            in_specs=[pl.BlockSpec((BLOCK, qw), lambda h, i, s: (i, off_q // qw + h)),
                      pl.BlockSpec((Lp, HEAD_DIM), lambda h, i, s: (0, kcol + h)),
                      pl.BlockSpec((Lp, HEAD_DIM), lambda h, i, s: (0, vcol + h)),
                      pl.BlockSpec((Lp, HEAD_DIM), lambda h, i, s: (0, 0)),
                      pl.BlockSpec((Lp, HEAD_DIM), lambda h, i, s: (0, 0))],
            out_specs=pl.BlockSpec((BLOCK, qw), lambda h, i, s: (i, h)),
            scratch_shapes=[pltpu.VMEM((Lp, HEAD_DIM), BF16), pltpu.VMEM((Lp, HEAD_DIM), BF16)]),
        out_shape=jax.ShapeDtypeStruct((S, n_heads * HEAD_DIM), BF16),
        compiler_params=_params(("arbitrary", "arbitrary"),
                                _vmem_limit(BLOCK * qw * 4, 4 * col_bytes, BLOCK * qw * 2,
                                            temps=col_bytes + 8 * group * BLOCK * 512 * 4)),
        name="attention",
    )(sink.astype(F32), z, z, z, cos_t, sin_t)


def _split3(x):
    hi = x.astype(BF16)
    r1 = x - hi.astype(F32)
    mid = r1.astype(BF16)
    lo = (r1 - mid.astype(F32)).astype(BF16)
    return hi, mid, lo


def _split2(x):
    hi = x.astype(BF16)
    return hi, (x - hi.astype(F32)).astype(BF16)


def _dot_exact_rhs(x, m_bf16, dims=NN, left=False):
    parts = _split3(x)
    if left:
        return sum(_dot(m_bf16, p, dims) for p in parts)
    return sum(_dot(p, m_bf16, dims) for p in parts)


def _dotp(a, b, passes, dims=NN):
    if passes == 1:
        return _dot(a.astype(BF16), b.astype(BF16), dims)
    a_hi, a_lo = _split2(a)
    b_hi, b_lo = _split2(b)
    return _dot(a_hi, b_hi, dims) + (_dot(a_hi, b_lo, dims) + _dot(a_lo, b_hi, dims))


def _head_sum(x, ones_bd):
    outs = []
    for s in range(x.shape[1] // LANES):
        outs.append(_dot_exact_rhs(x[:, s * LANES:(s + 1) * LANES], ones_bd))
    return outs[0] if len(outs) == 1 else jnp.concatenate(outs, axis=1)


def _prep_kernel(r_ref, rp_ref, rn_ref, k_ref, kp_ref, kn_ref, v_ref, vp_ref, vn_ref,
                 lo_ref, lop_ref, lon_ref, gd_ref, gdp_ref, gdn_ref,
                 pc_ref, pl_ref, wl_ref, g2_ref, ones_ref,
                 ro_ref, vo_ref, kko_ref, lwf_ref, lwb_ref, kdf_ref, kdb_ref, bdf_ref, bdb_ref, go_ref,
                 buf_ref, *, n_tiles, tr, lora_tanh_cols):
    i = pl.program_id(0)

    def shifted(main_ref, prev_ref, next_ref, mup, mun):
        w = main_ref.shape[1]
        buf_ref[0:8, 0:w] = prev_ref[...]
        buf_ref[8:8 + tr, 0:w] = main_ref[...]
        buf_ref[8 + tr:16 + tr, 0:w] = next_ref[...]
        x = main_ref[...]
        prev = buf_ref[7:7 + tr, 0:w]
        nxt = buf_ref[9:9 + tr, 0:w]
        return x + mup * (prev - x) + mun * (nxt - x)

    r = shifted(r_ref, rp_ref, rn_ref, pc_ref[0:1, :], pc_ref[1:2, :])
    k = shifted(k_ref, kp_ref, kn_ref, pc_ref[2:3, :], pc_ref[3:4, :])
    v = shifted(v_ref, vp_ref, vn_ref, pc_ref[4:5, :], pc_ref[5:6, :])
    lo = shifted(lo_ref, lop_ref, lon_ref, pl_ref[0:1, 0:LORA_PAD], pl_ref[1:2, 0:LORA_PAD])
    gd = shifted(gd_ref, gdp_ref, gdn_ref, pl_ref[0:1, LORA_PAD:], pl_ref[1:2, LORA_PAD:])

    row = lax.broadcasted_iota(jnp.int32, (tr, 1), 0)
    valid = jnp.where((i < n_tiles - 1) | (row >= tr - N_META), 1.0, 0.0).astype(F32)

    lane = lax.broadcasted_iota(jnp.int32, lo.shape, 1)
    lo_act = jnp.where(lane < lora_tanh_cols, jnp.tanh(lo), lo).astype(BF16)
    dec_f = _dot(lo_act, wl_ref[0])
    dec_b = _dot(lo_act, wl_ref[1])
    apre_f = _dot(lo_act, wl_ref[2])
    apre_b = _dot(lo_act, wl_ref[3])
    g = _dot(jax.nn.sigmoid(lo).astype(BF16), g2_ref[0]) + _dot(jax.nn.sigmoid(gd).astype(BF16), g2_ref[1])

    def log_decay(dec, w0):
        return -math.exp(-0.5) * jax.nn.sigmoid(w0 + dec)

    a_f = jax.nn.sigmoid(pc_ref[8:9, :] + apre_f)
    a_b = jax.nn.sigmoid(pc_ref[9:10, :] + apre_b)
    kk = k * pc_ref[10:11, :]
    ss = _head_sum(kk * kk, ones_ref[...])
    kk = kk / jnp.maximum(jnp.sqrt(ss), 1e-12) * valid
    k_a = pc_ref[11:12, :]
    kv = k * valid

    ro_ref[...] = r * valid
    vo_ref[...] = v * valid
    kko_ref[...] = kk
    lwf_ref[...] = log_decay(dec_f, pc_ref[6:7, :])
    lwb_ref[...] = log_decay(dec_b, pc_ref[7:8, :])
    kdf_ref[...] = kv * (1.0 + (a_f - 1.0) * k_a)
    kdb_ref[...] = kv * (1.0 + (a_b - 1.0) * k_a)
    bdf_ref[...] = kk * a_f
    bdb_ref[...] = kk * a_b
    go_ref[...] = g


def _rwkv_prep(z, off_r, C, pc, plo, wl, g2, ones_bd, lora_tanh_cols):
    Lp = z.shape[0]
    tr = BLOCK
    ct = _pick(C, (1024, 512))
    n_tiles = Lp // tr
    n8 = Lp // 8
    off_lo = off_r + 3 * C
    off_gd = off_lo + LORA_PAD
    assert off_r % ct == 0 and off_lo % LORA_PAD == 0

    def seg_specs(off, w, with_c):
        cb = off // w

        def col(c):
            return cb + c if with_c else cb
        return [pl.BlockSpec((tr, w), lambda i, c: (i, col(c))),
                pl.BlockSpec((8, w), lambda i, c: ((i * (tr // 8) + n8 - 1) % n8, col(c))),
                pl.BlockSpec((8, w), lambda i, c: (((i + 1) * (tr // 8)) % n8, col(c)))]

    in_specs = (seg_specs(off_r, ct, True) + seg_specs(off_r + C, ct, True) + seg_specs(off_r + 2 * C, ct, True)
                + seg_specs(off_lo, LORA_PAD, False) + seg_specs(off_gd, LORA_PAD, False)
                + [pl.BlockSpec((16, ct), lambda i, c: (0, c)),
                   pl.BlockSpec((8, 2 * LORA_PAD), lambda i, c: (0, 0)),
                   pl.BlockSpec((4, LORA_PAD, ct), lambda i, c: (0, 0, c)),
                   pl.BlockSpec((2, LORA_PAD, ct), lambda i, c: (0, 0, c)),
                   pl.BlockSpec((LANES, LANES), lambda i, c: (0, 0))])
    out_spec = pl.BlockSpec((tr, ct), lambda i, c: (i, c))
    out_sds = jax.ShapeDtypeStruct((Lp, C), F32)
    return pl.pallas_call(
        functools.partial(_prep_kernel, n_tiles=n_tiles, tr=tr, lora_tanh_cols=lora_tanh_cols),
        grid=(n_tiles, C // ct),
        in_specs=in_specs,
        out_specs=[out_spec] * 10,
        out_shape=[out_sds] * 10,
        scratch_shapes=[pltpu.VMEM((tr + 16, max(ct, LORA_PAD)), F32)],
        compiler_params=_params(("arbitrary", "arbitrary"),
                                _vmem_limit(3 * tr * ct * 4, 2 * tr * LORA_PAD * 4, 4 * LORA_PAD * ct * 2,
                                            2 * LORA_PAD * ct * 2, 10 * tr * ct * 4, temps=24 * tr * ct * 4)),
        name="rwkv_prep",
    )(*([z] * 15), pc, plo, wl, g2, ones_bd)


SCAN_PAIRS = 4


def _scan_consts():
    C = CHUNK
    row = lax.broadcasted_iota(jnp.int32, (C, 2 * C), 0)
    col = lax.broadcasted_iota(jnp.int32, (C, 2 * C), 1) % C
    rr = lax.broadcasted_iota(jnp.int32, (LANES, LANES), 0)
    cc = lax.broadcasted_iota(jnp.int32, (LANES, LANES), 1)
    return dict(
        row=lax.broadcasted_iota(jnp.int32, (C, LANES), 0),
        head0=lax.broadcasted_iota(jnp.int32, (C, LANES), 1) < RWKV_HEAD,
        strict_f=col < row, strict_b=col > row, incl_f=col <= row, incl_b=col >= row,
        eye_sbs=jnp.where(col == row, 1.0, 0.0).astype(F32),
        same_head=(rr // RWKV_HEAD) == (cc // RWKV_HEAD), eye=rr == cc)


def _scan_chains(chains, cst):
    C = CHUNK
    head0, same, eye = cst["head0"], cst["same_head"], cst["eye"]
    n = range(len(chains))
    rev = [c[7] for c in chains]
    r, v, al, lw, k, be, h = ([c[i] for c in chains] for i in range(7))
    strict = [cst["strict_b" if x else "strict_f"] for x in rev]
    incl = [cst["incl_b" if x else "incl_f"] for x in rev]

    def bd(x):
        xb = x.astype(BF16)
        zero = jnp.zeros_like(xb)
        return jnp.concatenate([jnp.where(head0, xb, zero), jnp.where(head0, zero, xb)], axis=0)

    def mm(a, b_bf16, dims=NN):
        return _dot(a.astype(BF16), b_bf16, dims)

    row = cst["row"]
    cl = list(lw)
    sh = 1
    while sh < C:
        cl = [cl[i] + (jnp.where(row < C - sh, pltpu.roll(cl[i], C - sh, 0), 0.0) if rev[i] else
                       jnp.where(row >= sh, pltpu.roll(cl[i], sh, 0), 0.0)) for i in n]
        sh *= 2
    total = [cl[i][0:1, :] if rev[i] else cl[i][C - 1:C, :] for i in n]
    a_t = [al[i] * jnp.exp(cl[i] - lw[i]) for i in n]
    r_t = [r[i] * jnp.exp(cl[i]) for i in n]
    w_inv = [jnp.exp(-cl[i]) for i in n]
    w_rest = [jnp.exp(total[i] - cl[i]) for i in n]
    bd_v = [bd(v[i]) for i in n]
    sc = [mm(jnp.concatenate([a_t[i], r_t[i]], axis=0),
             jnp.concatenate([bd(k[i] * w_inv[i]), bd(be[i] * w_inv[i])], axis=0), NT) for i in n]
    a_ak = [jnp.where(strict[i], sc[i][0:C, 0:2 * C], 0.0) for i in n]
    n_ab = [jnp.where(strict[i], sc[i][0:C, 2 * C:], 0.0) for i in n]
    a_rk = [jnp.where(incl[i], sc[i][C:, 0:2 * C], 0.0) for i in n]
    a_rb = [jnp.where(incl[i], sc[i][C:, 2 * C:], 0.0) for i in n]

    t_inv = [cst["eye_sbs"] - n_ab[i] for i in n]
    sq = [mm(n_ab[i], bd(n_ab[i])) for i in n]
    n_round = 2
    while (1 << n_round) < C:
        both = [mm(jnp.concatenate([t_inv[i], sq[i]], axis=0), bd(sq[i])) for i in n]
        t_inv = [t_inv[i] + both[i][0:C] for i in n]
        sq = [both[i][C:] for i in n]
        n_round += 1
    t_inv = [t_inv[i] + mm(t_inv[i], bd(sq[i])) for i in n]

    akv = [mm(jnp.concatenate([a_ak[i], a_rk[i]], axis=0), bd_v[i]) for i in n]
    tp = [mm(t_inv[i], jnp.concatenate([bd(akv[i][0:C]), bd(a_t[i])], axis=1)) for i in n]
    p0 = [tp[i][:, 0:LANES] for i in n]
    at = [tp[i][:, LANES:] for i in n]
    rb = [mm(a_rb[i], jnp.concatenate([bd(p0[i]), bd(at[i])], axis=1)) for i in n]
    y0 = [akv[i][C:] - rb[i][:, 0:LANES] for i in n]
    rh = [r_t[i] - rb[i][:, LANES:] for i in n]
    bt = [mm(be[i] * w_rest[i], jnp.concatenate([at[i], p0[i]], axis=1).astype(BF16), TN) for i in n]
    kv = [mm(k[i] * w_rest[i], v[i].astype(BF16), TN) for i in n]
    m_mat = [jnp.where(same, jnp.where(eye, jnp.broadcast_to(jnp.exp(total[i]), (LANES, LANES)), 0.0)
                       - bt[i][:, 0:LANES], 0.0) for i in n]
    g_mat = [jnp.where(same, kv[i] - bt[i][:, LANES:], 0.0) for i in n]
    yh = [mm(jnp.concatenate([rh[i], m_mat[i]], axis=0), h[i].astype(BF16)) for i in n]
    return [(yh[i][0:C] + y0[i], yh[i][C:] + g_mat[i]) for i in n]


def _scan_kernel(rf_ref, vf_ref, af_ref, lwf_ref, kf_ref, bf_ref,
                 rb_ref, vb_ref, ab_ref, lwb_ref, kb_ref, bb_ref,
                 yf_ref, yb_ref, hf_ref, hb_ref, *, pairs):
    @pl.when(pl.program_id(1) == 0)
    def _():
        hf_ref[...] = jnp.zeros(hf_ref.shape, F32)
        hb_ref[...] = jnp.zeros(hb_ref.shape, F32)

    cst = _scan_consts()
    chains = []
    for g in range(pairs):
        sl = slice(g * LANES, (g + 1) * LANES)
        chains.append((rf_ref[:, sl], vf_ref[:, sl], af_ref[:, sl], lwf_ref[:, sl], kf_ref[:, sl], bf_ref[:, sl],
                       hf_ref[g], False))
        chains.append((rb_ref[:, sl], vb_ref[:, sl], ab_ref[:, sl], lwb_ref[:, sl], kb_ref[:, sl], bb_ref[:, sl],
                       hb_ref[g], True))
    res = _scan_chains(chains, cst)
    for g in range(pairs):
        sl = slice(g * LANES, (g + 1) * LANES)
        yf_ref[:, sl], hf_ref[g] = res[2 * g]
        yb_ref[:, sl], hb_ref[g] = res[2 * g + 1]


def _rwkv_scan(r, v, kk, lw_f, lw_b, kd_f, kd_b, bd_f, bd_b, S):
    Lp, C = r.shape
    n_real = S // CHUNK
    n_chunks = Lp // CHUNK
    steps = n_real + 1
    pairs = _pick(C // LANES, (SCAN_PAIRS, 2, 1))
    w = pairs * LANES

    def fwd(p, s):
        return ((s + n_real + 1) % n_chunks, p)

    def bwd(p, s):
        return (n_real - s, p)

    spec_f = pl.BlockSpec((CHUNK, w), fwd)
    spec_b = pl.BlockSpec((CHUNK, w), bwd)
    out_sds = jax.ShapeDtypeStruct((S + CHUNK, C), F32)
    out_f = pl.BlockSpec((CHUNK, w), lambda p, s: ((s + n_real) % (n_real + 1), p))
    return pl.pallas_call(
        functools.partial(_scan_kernel, pairs=pairs),
        grid=(C // w, steps),
        in_specs=[spec_f] * 6 + [spec_b] * 6,
        out_specs=[out_f, spec_b],
        out_shape=[out_sds, out_sds],
        scratch_shapes=[pltpu.VMEM((pairs, LANES, LANES), F32), pltpu.VMEM((pairs, LANES, LANES), F32)],
        compiler_params=_params(("arbitrary", "arbitrary"), 32 << 20),
        name="rwkv_scan",
    )(r, v, kk, lw_f, kd_f, bd_f, r, v, kk, lw_b, kd_b, bd_b)


def _post_kernel(yf_ref, yb_ref, r_ref, kdf_ref, kdb_ref, v_ref, g_ref, pq_ref, ones_ref, o_ref):
    ones_bd = ones_ref[...]
    inv_n = 1.0 / RWKV_HEAD
    y = yf_ref[...] + yb_ref[...]
    mean = _head_sum(y, ones_bd) * inv_n
    yc = y - mean
    var = _head_sum(yc * yc, ones_bd) * inv_n
    yn = yc * lax.rsqrt(var + GN_EPS) * pq_ref[0:1, :] + pq_ref[1:2, :]
    bonus = _head_sum(r_ref[...] * (kdf_ref[...] + kdb_ref[...]) * pq_ref[2:3, :], ones_bd) * v_ref[...]
    o_ref[...] = ((yn + bonus) * g_ref[...]).astype(o_ref.dtype)


def _rwkv_post(y_f, y_b, r, kd_f, kd_b, v, g, pq, ones_bd, S):
    C = r.shape[1]
    tr = 256
    ct = _pick(C, (1024, 512))
    spec = pl.BlockSpec((tr, ct), lambda i, c: (i, c))
    return pl.pallas_call(
        _post_kernel,
        grid=(S // tr, C // ct),
        in_specs=[spec] * 7 + [pl.BlockSpec((8, ct), lambda i, c: (0, c)),
                               pl.BlockSpec((LANES, LANES), lambda i, c: (0, 0))],
        out_specs=spec,
        out_shape=jax.ShapeDtypeStruct((S, C), BF16),
        compiler_params=_params(("arbitrary", "arbitrary"),
                                _vmem_limit(7 * tr * ct * 4, tr * ct * 2, temps=16 * tr * ct * 4)),
        name="rwkv_post",
    )(y_f, y_b, r, kd_f, kd_b, v, g, pq, ones_bd)


def _pad_cols(w, width):
    return jnp.pad(w, ((0, 0), (0, width - w.shape[1])))


def kernel(x, meta_tokens, norm_mix_w, w_in, b_gate, mu_prev, mu_next, dec_w0, dec_w2, iclr_a0, iclr_a2,
           gate_w2, k_k, k_a, r_k, ln_x_w, ln_x_b, attn_sink, w_proj_attn, w_proj_rwkv, w_out, norm_ffn_w,
           w_ffn_gate, w_ffn_up, w_ffn_down, norm_final_w):
    B, S, D = x.shape
    assert B == 1 and norm_mix_w.shape[0] == 1, "one sequence, one layer"
    C = k_k.shape[-1]
    AW = w_proj_attn.shape[1]
    n_heads = attn_sink.shape[-1]
    DL, IL, GL = dec_w2.shape[2], iclr_a2.shape[2], gate_w2.shape[1]
    shift_w = mu_prev.shape[-1]
    KVW = (w_in.shape[-1] - 2 * D - AW - shift_w) // 2
    n_kv = KVW // HEAD_DIM
    F = w_ffn_gate.shape[-1]
    lora_w = 2 * DL + 2 * IL
    assert shift_w == 3 * C + lora_w + GL and lora_w <= LORA_PAD and GL <= LORA_PAD
    assert S % BLOCK == 0 and C % LANES == 0 and n_heads * HEAD_DIM == AW
    Lp = S + PAD_ROWS

    off_q = 2 * D
    off_r = off_q + AW + 2 * KVW
    off_lo = off_r + 3 * C
    tn_in = 512
    n_in_pad = -(-w_in.shape[-1] // tn_in) * tn_in
    assert n_in_pad >= off_lo + 2 * LORA_PAD and off_lo % LORA_PAD == 0
    low_w = lora_w + GL
    gl_a = LORA_PAD - lora_w

    mu_p, mu_n = mu_prev[0], mu_next[0]
    zc = jnp.zeros((C,), F32)
    pc = jnp.stack([mu_p[:C], mu_n[:C], mu_p[C:2 * C], mu_n[C:2 * C], mu_p[2 * C:3 * C], mu_n[2 * C:3 * C],
                    dec_w0[0, 0], dec_w0[0, 1], iclr_a0[0, 0], iclr_a0[0, 1], k_k[0], k_a[0], zc, zc, zc, zc])
    plo = jnp.zeros((8, 2 * LORA_PAD), F32).at[0, :low_w].set(mu_p[3 * C:]).at[1, :low_w].set(mu_n[3 * C:])
    wl = jnp.zeros((4, LORA_PAD, C), F32)
    wl = wl.at[0, 0:DL].set(dec_w2[0, 0]).at[1, DL:2 * DL].set(dec_w2[0, 1])
    wl = wl.at[2, 2 * DL:2 * DL + IL].set(iclr_a2[0, 0]).at[3, 2 * DL + IL:lora_w].set(iclr_a2[0, 1])
    wl = wl.astype(BF16)
    g2 = jnp.zeros((2, LORA_PAD, C), F32).at[0, lora_w:].set(gate_w2[0, :gl_a]).at[1, :GL - gl_a].set(
        gate_w2[0, gl_a:]).astype(BF16)
    pq = jnp.zeros((8, C), F32).at[0].set(ln_x_w[0]).at[1].set(ln_x_b[0]).at[2].set(r_k[0])
    lane = jnp.arange(LANES)
    ones_bd = (lane[:, None] // RWKV_HEAD == lane[None, :] // RWKV_HEAD).astype(BF16)

    w_d = w_ffn_down[0].astype(BF16)

    pos = jnp.concatenate([jnp.arange(N_META, N_META + S), jnp.zeros((PAD_ROWS - N_META,), jnp.int32),
                           jnp.arange(N_META)]).astype(F32)
    inv = ROPE_THETA ** (-jnp.arange(0, ROPE_DIMS, 2, dtype=F32) / ROPE_DIMS)
    ang = pos[:, None] * inv[None, :]
    ones_rest = jnp.ones((Lp, HEAD_DIM - ROPE_DIMS), F32)
    cos_t = jnp.concatenate([jnp.cos(ang), jnp.cos(ang), ones_rest], axis=1)
    sin_t = jnp.concatenate([-jnp.sin(ang), jnp.sin(ang), 0.0 * ones_rest], axis=1)

    x2 = x[0]
    h_ext = _norm_ext(x2, meta_tokens.astype(x.dtype), norm_mix_w[0])
    z = _mm_in(h_ext, w_in[0], _pick(Lp, (1040, 640)), tn_in)
    o_att = _attention_res(z, cos_t, sin_t, attn_sink[0], S, n_heads, n_kv, off_q)
    r_s, v_s, kk, lw_f, lw_b, kd_f, kd_b, bd_f, bd_b, g = _rwkv_prep(
        z, off_r, C, pc, plo, wl, g2, ones_bd, 2 * DL)
    y_f, y_b = _rwkv_scan(r_s, v_s, kk, lw_f, lw_b, kd_f, kd_b, bd_f, bd_b, S)
    o_rwkv = _rwkv_post(y_f, y_b, r_s, kd_f, kd_b, v_s, g, pq, ones_bd, S)
    tm = _pick(S, (1024, 512))
    mixed = _mm_mix(o_att, w_proj_attn[0], o_rwkv, w_proj_rwkv[0], z, b_gate[0], S, D, tm, 512)
    h1 = _mm_resw(mixed, w_out[0], x2, tm, 512)
    hn = _norm(h1, norm_ffn_w[0], BF16)
    act = _mm_glu(hn, w_ffn_gate[0], w_ffn_up[0], tm, 256)
    h2 = _mm_res(act, w_d, h1, tm, 512, F // 2)
    y = _norm(h2, norm_final_w, x.dtype)
    return y[None]
```

```python
import functools
import math

import jax
import jax.numpy as jnp
from jax import lax
from jax.experimental import pallas as pl
from jax.experimental.pallas import tpu as pltpu

F32 = jnp.float32
BF16 = jnp.bfloat16

N_META = 16
HEAD_DIM = 128
WINDOW = 128
BLOCK = 128
ROPE_DIMS = HEAD_DIM // 4
ROPE_THETA = 500000.0
RWKV_HEAD = 64
RMS_EPS = 1e-6
GN_EPS = 64e-5
LANES = 128
PAD_ROWS = BLOCK
CHUNK = 64
LORA_PAD = 512
VMEM_PHYSICAL = 64 * 1024 * 1024

NN = (((1,), (0,)), ((), ()))
NT = (((1,), (1,)), ((), ()))
TN = (((0,), (0,)), ((), ()))


def _vmem_limit(*block_bytes, temps=0):
    need = 2 * sum(block_bytes) + temps + (4 << 20)
    return int(min(max(need, 16 << 20), VMEM_PHYSICAL - (6 << 20)))


def _params(sem, limit):
    return pltpu.CompilerParams(dimension_semantics=sem, vmem_limit_bytes=limit)


def _pick(n, cands):
    for c in cands:
        if n % c == 0:
            return c
    raise ValueError(f"no tile in {cands} divides {n}")


def _rms(x, w):
    return x * lax.rsqrt(jnp.mean(x * x, axis=-1, keepdims=True) + RMS_EPS) * w


def _norm_ext_kernel(x_ref, meta_ref, w_ref, o_ref, *, nb):
    i = pl.program_id(0)

    @pl.when(i < nb)
    def _():
        o_ref[...] = _rms(x_ref[...], w_ref[...]).astype(o_ref.dtype)

    @pl.when(i == nb)
    def _():
        o_ref[...] = jnp.zeros(o_ref.shape, o_ref.dtype)
        o_ref[PAD_ROWS - N_META:, :] = _rms(meta_ref[...], w_ref[...]).astype(o_ref.dtype)


def _norm_ext(x, meta, w):
    S, D = x.shape
    nb = S // BLOCK
    return pl.pallas_call(
        functools.partial(_norm_ext_kernel, nb=nb),
        grid=(nb + 1,),
        in_specs=[pl.BlockSpec((BLOCK, D), lambda i: (jnp.minimum(i, nb - 1), 0)),
                  pl.BlockSpec((N_META, D), lambda i: (0, 0)),
                  pl.BlockSpec((1, D), lambda i: (0, 0))],
        out_specs=pl.BlockSpec((BLOCK, D), lambda i: (i, 0)),
        out_shape=jax.ShapeDtypeStruct((S + PAD_ROWS, D), BF16),
        compiler_params=_params(("arbitrary",), _vmem_limit(BLOCK * D * 4, BLOCK * D * 2)),
        name="norm_ext",
    )(x, meta, w.reshape(1, D))


def _norm_kernel(x_ref, w_ref, o_ref):
    o_ref[...] = _rms(x_ref[...], w_ref[...]).astype(o_ref.dtype)


def _norm(x, w, out_dtype, tr=256):
    S, D = x.shape
    return pl.pallas_call(
        _norm_kernel,
        grid=(S // tr,),
        in_specs=[pl.BlockSpec((tr, D), lambda i: (i, 0)), pl.BlockSpec((1, D), lambda i: (0, 0))],
        out_specs=pl.BlockSpec((tr, D), lambda i: (i, 0)),
        out_shape=jax.ShapeDtypeStruct((S, D), out_dtype),
        compiler_params=_params(("arbitrary",), _vmem_limit(tr * D * 4, tr * D * 4)),
        name="norm",
    )(x, w.reshape(1, D))


def _dot(a, b, dims=NN):
    return lax.dot_general(a, b, dims, preferred_element_type=F32)


def _cast_resident(src_ref, dst_ref, col0=None, n_valid=None):
    K = src_ref.shape[0]
    rows = _pick(K, (256, 128))

    def body(c, carry):
        sl = pl.ds(pl.multiple_of(c * rows, rows), rows)
        blk = src_ref[sl, :]
        if n_valid is not None:
            col = col0 + lax.broadcasted_iota(jnp.int32, blk.shape, 1)
            blk = jnp.where(col < n_valid, blk, 0.0)
        dst_ref[sl, :] = blk.astype(dst_ref.dtype)
        return carry

    lax.fori_loop(0, K // rows, body, 0)


def _mm_in_kernel(a_ref, b_ref, o_ref, bw_ref, *, n_valid):
    j, i = pl.program_id(0), pl.program_id(1)

    @pl.when(i == 0)
    def _():
        _cast_resident(b_ref, bw_ref, j * b_ref.shape[1], n_valid)

    o_ref[...] = _dot(a_ref[...], bw_ref[...])


def _mm_in(a, w, tm, tn):
    M, K = a.shape
    N = w.shape[1]
    nj = pl.cdiv(N, tn)
    return pl.pallas_call(
        functools.partial(_mm_in_kernel, n_valid=N),
        grid=(nj, M // tm),
        in_specs=[pl.BlockSpec((tm, K), lambda j, i: (i, 0)), pl.BlockSpec((K, tn), lambda j, i: (0, j))],
        out_specs=pl.BlockSpec((tm, tn), lambda j, i: (i, j)),
        out_shape=jax.ShapeDtypeStruct((M, nj * tn), F32),
        scratch_shapes=[pltpu.VMEM((K, tn), BF16)],
        compiler_params=_params(("arbitrary", "arbitrary"),
                                _vmem_limit(tm * K * 2, K * tn * 4, tm * tn * 4, temps=K * tn * 2 + tm * tn * 4)),
        name="mm_in",
    )(a, w)


def _mm_mix_kernel(a1_ref, b1_ref, a2_ref, b2_ref, z0_ref, z1_ref, bg_ref, o_ref, w1_ref, w2_ref):
    @pl.when(pl.program_id(1) == 0)
    def _():
        _cast_resident(b1_ref, w1_ref)
        _cast_resident(b2_ref, w2_ref)

    g0 = jax.nn.sigmoid(z0_ref[...] + bg_ref[0:1, :])
    g1 = jax.nn.sigmoid(z1_ref[...] + bg_ref[1:2, :])
    o_ref[...] = (g0 * _dot(a1_ref[...], w1_ref[...]) + g1 * _dot(a2_ref[...], w2_ref[...])).astype(o_ref.dtype)


def _mm_mix(o_att, w_pa, o_rwkv, w_pr, z, b_gate, S, D, tm, tn):
    K1, K2 = o_att.shape[1], o_rwkv.shape[1]
    nj = D // tn
    return pl.pallas_call(
        _mm_mix_kernel,
        grid=(nj, S // tm),
        in_specs=[pl.BlockSpec((tm, K1), lambda j, i: (i, 0)), pl.BlockSpec((K1, tn), lambda j, i: (0, j)),
                  pl.BlockSpec((tm, K2), lambda j, i: (i, 0)), pl.BlockSpec((K2, tn), lambda j, i: (0, j)),
                  pl.BlockSpec((tm, tn), lambda j, i: (i, j)), pl.BlockSpec((tm, tn), lambda j, i: (i, nj + j)),
                  pl.BlockSpec((2, tn), lambda j, i: (0, j))],
        out_specs=pl.BlockSpec((tm, tn), lambda j, i: (i, j)),
        out_shape=jax.ShapeDtypeStruct((S, D), BF16),
        scratch_shapes=[pltpu.VMEM((K1, tn), BF16), pltpu.VMEM((K2, tn), BF16)],
        compiler_params=_params(("arbitrary", "arbitrary"),
                                _vmem_limit(tm * K1 * 2, K1 * tn * 4, tm * K2 * 2, K2 * tn * 4,
                                            2 * tm * tn * 4, tm * tn * 2,
                                            temps=(K1 + K2) * tn * 2 + 3 * tm * tn * 4)),
        name="mm_mix",
    )(o_att, w_pa, o_rwkv, w_pr, z, z, b_gate)


def _mm_resw_kernel(a_ref, b_ref, r_ref, o_ref, bw_ref):
    @pl.when(pl.program_id(1) == 0)
    def _():
        _cast_resident(b_ref, bw_ref)

    o_ref[...] = r_ref[...] + _dot(a_ref[...], bw_ref[...])


def _mm_resw(a, w, res, tm, tn):
    M, K = a.shape
    N = w.shape[1]
    return pl.pallas_call(
        _mm_resw_kernel,
        grid=(N // tn, M // tm),
        in_specs=[pl.BlockSpec((tm, K), lambda j, i: (i, 0)), pl.BlockSpec((K, tn), lambda j, i: (0, j)),
                  pl.BlockSpec((tm, tn), lambda j, i: (i, j))],
        out_specs=pl.BlockSpec((tm, tn), lambda j, i: (i, j)),
        out_shape=jax.ShapeDtypeStruct((M, N), F32),
        scratch_shapes=[pltpu.VMEM((K, tn), BF16)],
        compiler_params=_params(("arbitrary", "arbitrary"),
                                _vmem_limit(tm * K * 2, K * tn * 4, 2 * tm * tn * 4, temps=K * tn * 2 + tm * tn * 4)),
        name="mm_out",
    )(a, w, res)


def _mm_res_kernel(a_ref, b_ref, r_ref, o_ref, *, nk):
    k = pl.program_id(2)
    part = _dot(a_ref[...], b_ref[...])
    if nk == 1:
        o_ref[...] = r_ref[...] + part
    else:
        @pl.when(k == 0)
        def _():
            o_ref[...] = r_ref[...] + part

        @pl.when(k > 0)
        def _():
            o_ref[...] += part


def _mm_res(a, b, res, tm, tn, tk):
    M, K = a.shape
    N = b.shape[1]
    nk = K // tk
    return pl.pallas_call(
        functools.partial(_mm_res_kernel, nk=nk),
        grid=(M // tm, N // tn, nk),
        in_specs=[pl.BlockSpec((tm, tk), lambda i, j, k: (i, k)), pl.BlockSpec((tk, tn), lambda i, j, k: (k, j)),
                  pl.BlockSpec((tm, tn), lambda i, j, k: (i, j))],
        out_specs=pl.BlockSpec((tm, tn), lambda i, j, k: (i, j)),
        out_shape=jax.ShapeDtypeStruct((M, N), F32),
        compiler_params=_params(("arbitrary", "arbitrary", "arbitrary"),
                                _vmem_limit(tm * tk * 2, tk * tn * 2, 2 * tm * tn * 4, temps=tm * tn * 4)),
        name="mm_down",
    )(a, b, res)


def _mm_glu_kernel(a_ref, bg_ref, bu_ref, o_ref, wg_ref, wu_ref):
    @pl.when(pl.program_id(1) == 0)
    def _():
        _cast_resident(bg_ref, wg_ref)
        _cast_resident(bu_ref, wu_ref)

    a = a_ref[...]
    gate = _dot(a, wg_ref[...])
    up = _dot(a, wu_ref[...])
    o_ref[...] = (gate * jax.nn.sigmoid(gate) * up).astype(o_ref.dtype)


def _mm_glu(a, wg, wu, tm, tn):
    M, K = a.shape
    N = wg.shape[1]
    return pl.pallas_call(
        _mm_glu_kernel,
        grid=(N // tn, M // tm),
        in_specs=[pl.BlockSpec((tm, K), lambda j, i: (i, 0)), pl.BlockSpec((K, tn), lambda j, i: (0, j)),
                  pl.BlockSpec((K, tn), lambda j, i: (0, j))],
        out_specs=pl.BlockSpec((tm, tn), lambda j, i: (i, j)),
        out_shape=jax.ShapeDtypeStruct((M, N), BF16),
        scratch_shapes=[pltpu.VMEM((K, tn), BF16), pltpu.VMEM((K, tn), BF16)],
        compiler_params=_params(("arbitrary", "arbitrary"),
                                _vmem_limit(tm * K * 2, 2 * K * tn * 4, tm * tn * 2,
                                            temps=2 * K * tn * 2 + 3 * tm * tn * 4)),
        name="mm_glu",
    )(a, wg, wu)


def _rope_kernel(z_ref, cos_ref, sin_ref, o_ref, *, n_rope_blocks, heads_per_block, k_heads_last):
    j = pl.program_id(1)
    cos = cos_ref[...]
    sin = sin_ref[...]
    lane = lax.broadcasted_iota(jnp.int32, cos.shape, 1)
    half = ROPE_DIMS // 2
    for h in range(heads_per_block):
        t = z_ref[:, h * HEAD_DIM:(h + 1) * HEAD_DIM]
        up = jnp.concatenate([t[:, half:], t[:, :half]], axis=1)
        down = jnp.concatenate([t[:, HEAD_DIM - half:], t[:, :HEAD_DIM - half]], axis=1)
        partner = jnp.where(lane < half, up, down)
        rot = t * cos + partner * sin
        if h >= k_heads_last:
            rot = jnp.where(j == n_rope_blocks - 1, t, rot)
        o_ref[:, h * HEAD_DIM:(h + 1) * HEAD_DIM] = rot.astype(o_ref.dtype)


def _rope_cast(z, cos_t, sin_t, off_q, width, kv_width):
    Lp = z.shape[0]
    tc = 2 * kv_width
    assert off_q % tc == 0 and width % tc == 0
    nblk = width // tc
    return pl.pallas_call(
        functools.partial(_rope_kernel, n_rope_blocks=nblk, heads_per_block=tc // HEAD_DIM,
                          k_heads_last=kv_width // HEAD_DIM),
        grid=(Lp // BLOCK, nblk),
        in_specs=[pl.BlockSpec((BLOCK, tc), lambda i, j: (i, off_q // tc + j)),
                  pl.BlockSpec((BLOCK, HEAD_DIM), lambda i, j: (i, 0)),
                  pl.BlockSpec((BLOCK, HEAD_DIM), lambda i, j: (i, 0))],
        out_specs=pl.BlockSpec((BLOCK, tc), lambda i, j: (i, j)),
        out_shape=jax.ShapeDtypeStruct((Lp, width), BF16),
        compiler_params=_params(("arbitrary", "arbitrary"), _vmem_limit(BLOCK * tc * 4, BLOCK * tc * 2)),
        name="rope_cast",
    )(z, cos_t, sin_t)


NEG_BIG = -1e30


def _attn_kernel(q_ref, kp_ref, kc_ref, kn_ref, km_ref, vp_ref, vc_ref, vn_ref, vm_ref, sink_ref, o_ref,
                 *, nb, group):
    i = pl.program_id(1)
    scale = HEAD_DIM ** -0.5
    zpad = jnp.zeros((BLOCK - N_META, HEAD_DIM), BF16)
    k_all = jnp.concatenate([kp_ref[...], kc_ref[...], kn_ref[...], km_ref[...], zpad], axis=0)
    v_all = jnp.concatenate([vp_ref[...], vc_ref[...], vn_ref[...], vm_ref[...], zpad], axis=0)
    q_all = jnp.concatenate([q_ref[:, g * HEAD_DIM:(g + 1) * HEAD_DIM] for g in range(group)], axis=0)
    s = _dot(q_all, k_all, NT) * scale
    rows = lax.broadcasted_iota(jnp.int32, s.shape, 0) % BLOCK
    cols = lax.broadcasted_iota(jnp.int32, s.shape, 1)
    rel = cols - BLOCK - rows
    kblk = i - 1 + cols // BLOCK
    band_ok = (cols < 3 * BLOCK) & (kblk >= 0) & (kblk < nb) & (jnp.abs(rel) <= WINDOW)
    meta_ok = (cols >= 3 * BLOCK) & (cols < 3 * BLOCK + N_META)
    s = jnp.where(band_ok | meta_ok, s, NEG_BIG)
    sink = sink_ref[:, 0:1]
    m = jnp.maximum(jnp.max(s, axis=-1, keepdims=True), sink)
    p = jnp.exp(s - m)
    denom = jnp.sum(p, axis=-1, keepdims=True) + jnp.exp(sink - m)
    o = _dot(p.astype(BF16), v_all) / denom
    for g in range(group):
        o_ref[:, g * HEAD_DIM:(g + 1) * HEAD_DIM] = o[g * BLOCK:(g + 1) * BLOCK, :].astype(o_ref.dtype)


def _attention(qkv, sink, S, n_heads, n_kv):
    group = n_heads // n_kv
    nb = S // BLOCK
    qw = group * HEAD_DIM
    kcol = n_heads
    vcol = n_heads + n_kv
    meta_blk = (S + PAD_ROWS - N_META) // N_META
    sink_rows = jnp.broadcast_to(sink.astype(F32).reshape(n_kv, group, 1, 1),
                                 (n_kv, group, BLOCK, LANES)).reshape(n_kv, group * BLOCK, LANES)

    def kv_specs(col0):
        return [pl.BlockSpec((BLOCK, HEAD_DIM), lambda h, i: (jnp.maximum(i - 1, 0), col0 + h)),
                pl.BlockSpec((BLOCK, HEAD_DIM), lambda h, i: (i, col0 + h)),
                pl.BlockSpec((BLOCK, HEAD_DIM), lambda h, i: (jnp.minimum(i + 1, nb - 1), col0 + h)),
                pl.BlockSpec((N_META, HEAD_DIM), lambda h, i: (meta_blk, col0 + h))]

    return pl.pallas_call(
        functools.partial(_attn_kernel, nb=nb, group=group),
        grid=(n_kv, nb),
        in_specs=[pl.BlockSpec((BLOCK, qw), lambda h, i: (i, h))] + kv_specs(kcol) + kv_specs(vcol)
                 + [pl.BlockSpec((None, group * BLOCK, LANES), lambda h, i: (h, 0, 0))],
        out_specs=pl.BlockSpec((BLOCK, qw), lambda h, i: (i, h)),
        out_shape=jax.ShapeDtypeStruct((S, n_heads * HEAD_DIM), BF16),
        compiler_params=_params(("arbitrary", "arbitrary"),
                                _vmem_limit(BLOCK * qw * 2 * 2, 8 * BLOCK * HEAD_DIM * 2,
                                            group * BLOCK * LANES * 4, temps=6 * group * BLOCK * 512 * 4)),
        name="attention",
    )(qkv, qkv, qkv, qkv, qkv, qkv, qkv, qkv, qkv, sink_rows)


def _rope(t, cos, sin):
    half = ROPE_DIMS // 2
    lane = lax.broadcasted_iota(jnp.int32, t.shape, 1)
    up = jnp.concatenate([t[:, half:], t[:, :half]], axis=1)
    down = jnp.concatenate([t[:, HEAD_DIM - half:], t[:, :HEAD_DIM - half]], axis=1)
    return t * cos + jnp.where(lane < half, up, down) * sin


def _attn_res_kernel(sink_ref, q_ref, k_ref, v_ref, cos_ref, sin_ref, o_ref, kb_ref, vb_ref, *, nb, group):
    h, i = pl.program_id(0), pl.program_id(1)
    lp = k_ref.shape[0]

    @pl.when(i == 0)
    def _():
        def body(c, carry):
            sl = pl.ds(pl.multiple_of(c * BLOCK, BLOCK), BLOCK)
            kb_ref[sl, :] = _rope(k_ref[sl, :], cos_ref[sl, :], sin_ref[sl, :]).astype(BF16)
            vb_ref[sl, :] = v_ref[sl, :].astype(BF16)
            return carry
        lax.fori_loop(0, lp // BLOCK, body, 0)

    kb0 = jnp.clip(i - 1, 0, nb - 3)
    win = pl.ds(pl.multiple_of(kb0 * BLOCK, BLOCK), 3 * BLOCK)
    zpad = jnp.zeros((BLOCK - N_META, HEAD_DIM), BF16)
    k_all = jnp.concatenate([kb_ref[win, :], kb_ref[lp - N_META:lp, :], zpad], axis=0)
    v_all = jnp.concatenate([vb_ref[win, :], vb_ref[lp - N_META:lp, :], zpad], axis=0)
    rows = lax.broadcasted_iota(jnp.int32, (BLOCK, 4 * BLOCK), 0)
    cols = lax.broadcasted_iota(jnp.int32, (BLOCK, 4 * BLOCK), 1)
    rel = cols - rows + (kb0 - i) * BLOCK
    ok = ((cols < 3 * BLOCK) & (jnp.abs(rel) <= WINDOW)) | ((cols >= 3 * BLOCK) & (cols < 3 * BLOCK + N_META))
    bias = jnp.where(ok, 0.0, NEG_BIG).astype(F32)
    qrows = pl.ds(pl.multiple_of(i * BLOCK, BLOCK), BLOCK)
    cos_q, sin_q = cos_ref[qrows, :], sin_ref[qrows, :]
    scale = HEAD_DIM ** -0.5
    n = range(group)
    q = [_rope(q_ref[:, g * HEAD_DIM:(g + 1) * HEAD_DIM], cos_q, sin_q).astype(BF16) for g in n]
    s = [_dot(q[g], k_all, NT) * scale + bias for g in n]
    sink = [sink_ref[h * group + g] for g in n]
    m = [jnp.maximum(jnp.max(s[g], axis=-1, keepdims=True), sink[g]) for g in n]
    p = [jnp.exp(s[g] - m[g]) for g in n]
    denom = [jnp.sum(p[g], axis=-1, keepdims=True) + jnp.exp(sink[g] - m[g]) for g in n]
    o = [_dot(p[g].astype(BF16), v_all) / denom[g] for g in n]
    for g in n:
        o_ref[:, g * HEAD_DIM:(g + 1) * HEAD_DIM] = o[g].astype(o_ref.dtype)


def _attention_res(z, cos_t, sin_t, sink, S, n_heads, n_kv, off_q):
    Lp = z.shape[0]
    group = n_heads // n_kv
    nb = S // BLOCK
    assert nb >= 3
    qw = group * HEAD_DIM
    kcol = off_q // HEAD_DIM + n_heads
    vcol = kcol + n_kv
    col_bytes = Lp * HEAD_DIM * 4
    return pl.pallas_call(
        functools.partial(_attn_res_kernel, nb=nb, group=group),
        grid_spec=pltpu.PrefetchScalarGridSpec(
            num_scalar_prefetch=1,
            grid=(n_kv, nb),
            in_specs=[pl.BlockSpec((BLOCK, qw), lambda h, i, s: (i, off_q // qw + h)),
                      pl.BlockSpec((Lp, HEAD_DIM), lambda h, i, s: (0, kcol + h)),
                      pl.BlockSpec((Lp, HEAD_DIM), lambda h, i, s: (0, vcol + h)),
                      pl.BlockSpec((Lp, HEAD_DIM), lambda h, i, s: (0, 0)),
                      pl.BlockSpec((Lp, HEAD_DIM), lambda h, i, s: (0, 0))],
            out_specs=pl.BlockSpec((BLOCK, qw), lambda h, i, s: (i, h)),
            scratch_shapes=[pltpu.VMEM((Lp, HEAD_DIM), BF16), pltpu.VMEM((Lp, HEAD_DIM), BF16)]),
        out_shape=jax.ShapeDtypeStruct((S, n_heads * HEAD_DIM), BF16),
        compiler_params=_params(("arbitrary", "arbitrary"),
                                _vmem_limit(BLOCK * qw * 4, 4 * col_bytes, BLOCK * qw * 2,
                                            temps=col_bytes + 8 * group * BLOCK * 512 * 4)),
        name="attention",
    )(sink.astype(F32), z, z, z, cos_t, sin_t)


def _split3(x):
    hi = x.astype(BF16)
    r1 = x - hi.astype(F32)
    mid = r1.astype(BF16)
    lo = (r1 - mid.astype(F32)).astype(BF16)
    return hi, mid, lo


def _split2(x):
    hi = x.astype(BF16)
    return hi, (x - hi.astype(F32)).astype(BF16)


def _dot_exact_rhs(x, m_bf16, dims=NN, left=False):
    parts = _split3(x)
    if left:
        return sum(_dot(m_bf16, p, dims) for p in parts)
    return sum(_dot(p, m_bf16, dims) for p in parts)


def _dotp(a, b, passes, dims=NN):
    if passes == 1:
        return _dot(a.astype(BF16), b.astype(BF16), dims)
    a_hi, a_lo = _split2(a)
    b_hi, b_lo = _split2(b)
    return _dot(a_hi, b_hi, dims) + (_dot(a_hi, b_lo, dims) + _dot(a_lo, b_hi, dims))


def _head_sum(x, ones_bd):
    outs = []
    for s in range(x.shape[1] // LANES):
        outs.append(_dot_exact_rhs(x[:, s * LANES:(s + 1) * LANES], ones_bd))
    return outs[0] if len(outs) == 1 else jnp.concatenate(outs, axis=1)


def _prep_kernel(r_ref, rp_ref, rn_ref, k_ref, kp_ref, kn_ref, v_ref, vp_ref, vn_ref,
                 lo_ref, lop_ref, lon_ref, gd_ref, gdp_ref, gdn_ref,
                 pc_ref, pl_ref, wl_ref, g2_ref, ones_ref,
                 ro_ref, vo_ref, kko_ref, lwf_ref, lwb_ref, kdf_ref, kdb_ref, bdf_ref, bdb_ref, go_ref, bo_ref,
                 buf_ref, *, n_tiles, tr, lora_tanh_cols):
    i = pl.program_id(0)

    def shifted(main_ref, prev_ref, next_ref, mup, mun):
        w = main_ref.shape[1]
        buf_ref[0:8, 0:w] = prev_ref[...]
        buf_ref[8:8 + tr, 0:w] = main_ref[...]
        buf_ref[8 + tr:16 + tr, 0:w] = next_ref[...]
        x = main_ref[...]
        prev = buf_ref[7:7 + tr, 0:w]
        nxt = buf_ref[9:9 + tr, 0:w]
        return x + mup * (prev - x) + mun * (nxt - x)

    r = shifted(r_ref, rp_ref, rn_ref, pc_ref[0:1, :], pc_ref[1:2, :])
    k = shifted(k_ref, kp_ref, kn_ref, pc_ref[2:3, :], pc_ref[3:4, :])
    v = shifted(v_ref, vp_ref, vn_ref, pc_ref[4:5, :], pc_ref[5:6, :])
    lo = shifted(lo_ref, lop_ref, lon_ref, pl_ref[0:1, 0:LORA_PAD], pl_ref[1:2, 0:LORA_PAD])
    gd = shifted(gd_ref, gdp_ref, gdn_ref, pl_ref[0:1, LORA_PAD:], pl_ref[1:2, LORA_PAD:])

    row = lax.broadcasted_iota(jnp.int32, (tr, 1), 0)
    valid = jnp.where((i < n_tiles - 1) | (row >= tr - N_META), 1.0, 0.0).astype(F32)

    lane = lax.broadcasted_iota(jnp.int32, lo.shape, 1)
    lo_act = jnp.where(lane < lora_tanh_cols, jnp.tanh(lo), lo).astype(BF16)
    dec_f = _dot(lo_act, wl_ref[0])
    dec_b = _dot(lo_act, wl_ref[1])
    apre_f = _dot(lo_act, wl_ref[2])
    apre_b = _dot(lo_act, wl_ref[3])
    g = _dot(jax.nn.sigmoid(lo).astype(BF16), g2_ref[0]) + _dot(jax.nn.sigmoid(gd).astype(BF16), g2_ref[1])

    def log_decay(dec, w0):
        return -math.exp(-0.5) * jax.nn.sigmoid(w0 + dec)

    a_f = jax.nn.sigmoid(pc_ref[8:9, :] + apre_f)
    a_b = jax.nn.sigmoid(pc_ref[9:10, :] + apre_b)
    kk = k * pc_ref[10:11, :]
    ss = _head_sum(kk * kk, ones_ref[...])
    kk = kk / jnp.maximum(jnp.sqrt(ss), 1e-12) * valid
    k_a = pc_ref[11:12, :]
    kv = k * valid

    rv = r * valid
    vv = v * valid
    kd_f = kv * (1.0 + (a_f - 1.0) * k_a)
    kd_b = kv * (1.0 + (a_b - 1.0) * k_a)
    ro_ref[...] = rv
    vo_ref[...] = vv
    kko_ref[...] = kk
    lwf_ref[...] = log_decay(dec_f, pc_ref[6:7, :])
    lwb_ref[...] = log_decay(dec_b, pc_ref[7:8, :])
    kdf_ref[...] = kd_f
    kdb_ref[...] = kd_b
    bdf_ref[...] = kk * a_f
    bdb_ref[...] = kk * a_b
    go_ref[...] = g
    bo_ref[...] = _head_sum(rv * (kd_f + kd_b) * pc_ref[12:13, :], ones_ref[...]) * vv


def _rwkv_prep(z, off_r, C, pc, plo, wl, g2, ones_bd, lora_tanh_cols):
    Lp = z.shape[0]
    tr = BLOCK
    ct = _pick(C, (1024, 512))
    n_tiles = Lp // tr
    n8 = Lp // 8
    off_lo = off_r + 3 * C
    off_gd = off_lo + LORA_PAD
    assert off_r % ct == 0 and off_lo % LORA_PAD == 0

    def seg_specs(off, w, with_c):
        cb = off // w

        def col(c):
            return cb + c if with_c else cb
        return [pl.BlockSpec((tr, w), lambda i, c: (i, col(c))),
                pl.BlockSpec((8, w), lambda i, c: ((i * (tr // 8) + n8 - 1) % n8, col(c))),
                pl.BlockSpec((8, w), lambda i, c: (((i + 1) * (tr // 8)) % n8, col(c)))]

    in_specs = (seg_specs(off_r, ct, True) + seg_specs(off_r + C, ct, True) + seg_specs(off_r + 2 * C, ct, True)
                + seg_specs(off_lo, LORA_PAD, False) + seg_specs(off_gd, LORA_PAD, False)
                + [pl.BlockSpec((16, ct), lambda i, c: (0, c)),
                   pl.BlockSpec((8, 2 * LORA_PAD), lambda i, c: (0, 0)),
                   pl.BlockSpec((4, LORA_PAD, ct), lambda i, c: (0, 0, c)),
                   pl.BlockSpec((2, LORA_PAD, ct), lambda i, c: (0, 0, c)),
                   pl.BlockSpec((LANES, LANES), lambda i, c: (0, 0))])
    out_spec = pl.BlockSpec((tr, ct), lambda i, c: (i, c))
    out_sds = jax.ShapeDtypeStruct((Lp, C), F32)
    return pl.pallas_call(
        functools.partial(_prep_kernel, n_tiles=n_tiles, tr=tr, lora_tanh_cols=lora_tanh_cols),
        grid=(n_tiles, C // ct),
        in_specs=in_specs,
        out_specs=[out_spec] * 11,
        out_shape=[out_sds] * 11,
        scratch_shapes=[pltpu.VMEM((tr + 16, max(ct, LORA_PAD)), F32)],
        compiler_params=_params(("arbitrary", "arbitrary"),
                                _vmem_limit(3 * tr * ct * 4, 2 * tr * LORA_PAD * 4, 4 * LORA_PAD * ct * 2,
                                            2 * LORA_PAD * ct * 2, 11 * tr * ct * 4, temps=24 * tr * ct * 4)),
        name="rwkv_prep",
    )(*([z] * 15), pc, plo, wl, g2, ones_bd)


SCAN_PAIRS = 8


def _scan_consts():
    C = CHUNK
    row = lax.broadcasted_iota(jnp.int32, (C, 2 * C), 0)
    col = lax.broadcasted_iota(jnp.int32, (C, 2 * C), 1) % C
    rr = lax.broadcasted_iota(jnp.int32, (LANES, LANES), 0)
    cc = lax.broadcasted_iota(jnp.int32, (LANES, LANES), 1)
    return dict(
        row=lax.broadcasted_iota(jnp.int32, (C, LANES), 0),
        head0=lax.broadcasted_iota(jnp.int32, (C, LANES), 1) < RWKV_HEAD,
        strict_f=col < row, strict_b=col > row, incl_f=col <= row, incl_b=col >= row,
        eye_sbs=jnp.where(col == row, 1.0, 0.0).astype(F32),
        same_head=(rr // RWKV_HEAD) == (cc // RWKV_HEAD), eye=rr == cc)


def _scan_chains(chains, cst):
    C = CHUNK
    head0, same, eye = cst["head0"], cst["same_head"], cst["eye"]
    n = range(len(chains))
    rev = [c[7] for c in chains]
    r, v, al, lw, k, be, h = ([c[i] for c in chains] for i in range(7))
    strict = [cst["strict_b" if x else "strict_f"] for x in rev]
    incl = [cst["incl_b" if x else "incl_f"] for x in rev]

    def bd(x):
        xb = x.astype(BF16)
        zero = jnp.zeros_like(xb)
        return jnp.concatenate([jnp.where(head0, xb, zero), jnp.where(head0, zero, xb)], axis=0)

    def mm(a, b_bf16, dims=NN):
        return _dot(a.astype(BF16), b_bf16, dims)

    row = cst["row"]
    cl = list(lw)
    sh = 1
    while sh < C:
        cl = [cl[i] + (jnp.where(row < C - sh, pltpu.roll(cl[i], C - sh, 0), 0.0) if rev[i] else
                       jnp.where(row >= sh, pltpu.roll(cl[i], sh, 0), 0.0)) for i in n]
        sh *= 2
    total = [cl[i][0:1, :] if rev[i] else cl[i][C - 1:C, :] for i in n]
    a_t = [al[i] * jnp.exp(cl[i] - lw[i]) for i in n]
    r_t = [r[i] * jnp.exp(cl[i]) for i in n]
    w_inv = [jnp.exp(-cl[i]) for i in n]
    w_rest = [jnp.exp(total[i] - cl[i]) for i in n]
    bd_v = [bd(v[i]) for i in n]
    sc = [mm(jnp.concatenate([a_t[i], r_t[i]], axis=0),
             jnp.concatenate([bd(k[i] * w_inv[i]), bd(be[i] * w_inv[i])], axis=0), NT) for i in n]
    a_ak = [jnp.where(strict[i], sc[i][0:C, 0:2 * C], 0.0) for i in n]
    n_ab = [jnp.where(strict[i], sc[i][0:C, 2 * C:], 0.0) for i in n]
    a_rk = [jnp.where(incl[i], sc[i][C:, 0:2 * C], 0.0) for i in n]
    a_rb = [jnp.where(incl[i], sc[i][C:, 2 * C:], 0.0) for i in n]

    t_inv = [cst["eye_sbs"] - n_ab[i] for i in n]
    sq = [mm(n_ab[i], bd(n_ab[i])) for i in n]
    n_round = 2
    while (1 << n_round) < C:
        both = [mm(jnp.concatenate([t_inv[i], sq[i]], axis=0), bd(sq[i])) for i in n]
        t_inv = [t_inv[i] + both[i][0:C] for i in n]
        sq = [both[i][C:] for i in n]
        n_round += 1
    t_inv = [t_inv[i] + mm(t_inv[i], bd(sq[i])) for i in n]

    akv = [mm(jnp.concatenate([a_ak[i], a_rk[i]], axis=0), bd_v[i]) for i in n]
    tp = [mm(t_inv[i], jnp.concatenate([bd(akv[i][0:C]), bd(a_t[i])], axis=1)) for i in n]
    p0 = [tp[i][:, 0:LANES] for i in n]
    at = [tp[i][:, LANES:] for i in n]
    rb = [mm(a_rb[i], jnp.concatenate([bd(p0[i]), bd(at[i])], axis=1)) for i in n]
    y0 = [akv[i][C:] - rb[i][:, 0:LANES] for i in n]
    rh = [r_t[i] - rb[i][:, LANES:] for i in n]
    bt = [mm(be[i] * w_rest[i], jnp.concatenate([at[i], p0[i]], axis=1).astype(BF16), TN) for i in n]
    kv = [mm(k[i] * w_rest[i], v[i].astype(BF16), TN) for i in n]
    m_mat = [jnp.where(same, jnp.where(eye, jnp.broadcast_to(jnp.exp(total[i]), (LANES, LANES)), 0.0)
                       - bt[i][:, 0:LANES], 0.0) for i in n]
    g_mat = [jnp.where(same, kv[i] - bt[i][:, LANES:], 0.0) for i in n]
    yh = [mm(jnp.concatenate([rh[i], m_mat[i]], axis=0), h[i].astype(BF16)) for i in n]
    return [(yh[i][0:C] + y0[i], yh[i][C:] + g_mat[i]) for i in n]


def _scan_kernel(rf_ref, vf_ref, af_ref, lwf_ref, kf_ref, bf_ref,
                 rb_ref, vb_ref, ab_ref, lwb_ref, kb_ref, bb_ref,
                 yf_ref, yb_ref, hf_ref, hb_ref, *, pairs):
    @pl.when(pl.program_id(1) == 0)
    def _():
        hf_ref[...] = jnp.zeros(hf_ref.shape, F32)
        hb_ref[...] = jnp.zeros(hb_ref.shape, F32)

    cst = _scan_consts()
    chains = []
    for g in range(pairs):
        sl = slice(g * LANES, (g + 1) * LANES)
        chains.append((rf_ref[:, sl], vf_ref[:, sl], af_ref[:, sl], lwf_ref[:, sl], kf_ref[:, sl], bf_ref[:, sl],
                       hf_ref[g], False))
        chains.append((rb_ref[:, sl], vb_ref[:, sl], ab_ref[:, sl], lwb_ref[:, sl], kb_ref[:, sl], bb_ref[:, sl],
                       hb_ref[g], True))
    res = _scan_chains(chains, cst)
    for g in range(pairs):
        sl = slice(g * LANES, (g + 1) * LANES)
        yf_ref[:, sl], hf_ref[g] = res[2 * g]
        yb_ref[:, sl], hb_ref[g] = res[2 * g + 1]


def _rwkv_scan(r, v, kk, lw_f, lw_b, kd_f, kd_b, bd_f, bd_b, S):
    Lp, C = r.shape
    n_real = S // CHUNK
    n_chunks = Lp // CHUNK
    steps = n_real + 1
    pairs = _pick(C // LANES, (SCAN_PAIRS, 2, 1))
    w = pairs * LANES

    def fwd(p, s):
        return ((s + n_real + 1) % n_chunks, p)

    def bwd(p, s):
        return (n_real - s, p)

    spec_f = pl.BlockSpec((CHUNK, w), fwd)
    spec_b = pl.BlockSpec((CHUNK, w), bwd)
    out_sds = jax.ShapeDtypeStruct((S + CHUNK, C), F32)
    out_f = pl.BlockSpec((CHUNK, w), lambda p, s: ((s + n_real) % (n_real + 1), p))
    return pl.pallas_call(
        functools.partial(_scan_kernel, pairs=pairs),
        grid=(C // w, steps),
        in_specs=[spec_f] * 6 + [spec_b] * 6,
        out_specs=[out_f, spec_b],
        out_shape=[out_sds, out_sds],
        scratch_shapes=[pltpu.VMEM((pairs, LANES, LANES), F32), pltpu.VMEM((pairs, LANES, LANES), F32)],
        compiler_params=_params(("arbitrary", "arbitrary"), 32 << 20),
        name="rwkv_scan",
    )(r, v, kk, lw_f, kd_f, bd_f, r, v, kk, lw_b, kd_b, bd_b)


def _post_kernel(yf_ref, yb_ref, bonus_ref, g_ref, pq_ref, ones_ref, o_ref):
    ones_bd = ones_ref[...]
    inv_n = 1.0 / RWKV_HEAD
    y = yf_ref[...] + yb_ref[...]
    mean = _head_sum(y, ones_bd) * inv_n
    yc = y - mean
    var = _head_sum(yc * yc, ones_bd) * inv_n
    yn = yc * lax.rsqrt(var + GN_EPS) * pq_ref[0:1, :] + pq_ref[1:2, :]
    o_ref[...] = ((yn + bonus_ref[...]) * g_ref[...]).astype(o_ref.dtype)


def _rwkv_post(y_f, y_b, bonus, g, pq, ones_bd, S):
    C = g.shape[1]
    tr = 256
    ct = _pick(C, (1024, 512))
    spec = pl.BlockSpec((tr, ct), lambda i, c: (i, c))
    return pl.pallas_call(
        _post_kernel,
        grid=(S // tr, C // ct),
        in_specs=[spec] * 4 + [pl.BlockSpec((8, ct), lambda i, c: (0, c)),
                               pl.BlockSpec((LANES, LANES), lambda i, c: (0, 0))],
        out_specs=spec,
        out_shape=jax.ShapeDtypeStruct((S, C), BF16),
        compiler_params=_params(("arbitrary", "arbitrary"),
                                _vmem_limit(4 * tr * ct * 4, tr * ct * 2, temps=16 * tr * ct * 4)),
        name="rwkv_post",
    )(y_f, y_b, bonus, g, pq, ones_bd)


def _pad_cols(w, width):
    return jnp.pad(w, ((0, 0), (0, width - w.shape[1])))


def kernel(x, meta_tokens, norm_mix_w, w_in, b_gate, mu_prev, mu_next, dec_w0, dec_w2, iclr_a0, iclr_a2,
           gate_w2, k_k, k_a, r_k, ln_x_w, ln_x_b, attn_sink, w_proj_attn, w_proj_rwkv, w_out, norm_ffn_w,
           w_ffn_gate, w_ffn_up, w_ffn_down, norm_final_w):
    B, S, D = x.shape
    assert B == 1 and norm_mix_w.shape[0] == 1, "one sequence, one layer"
    C = k_k.shape[-1]
    AW = w_proj_attn.shape[1]
    n_heads = attn_sink.shape[-1]
    DL, IL, GL = dec_w2.shape[2], iclr_a2.shape[2], gate_w2.shape[1]
    shift_w = mu_prev.shape[-1]
    KVW = (w_in.shape[-1] - 2 * D - AW - shift_w) // 2
    n_kv = KVW // HEAD_DIM
    F = w_ffn_gate.shape[-1]
    lora_w = 2 * DL + 2 * IL
    assert shift_w == 3 * C + lora_w + GL and lora_w <= LORA_PAD and GL <= LORA_PAD
    assert S % BLOCK == 0 and C % LANES == 0 and n_heads * HEAD_DIM == AW
    Lp = S + PAD_ROWS

    off_q = 2 * D
    off_r = off_q + AW + 2 * KVW
    off_lo = off_r + 3 * C
    tn_in = 512
    n_in_pad = -(-w_in.shape[-1] // tn_in) * tn_in
    assert n_in_pad >= off_lo + 2 * LORA_PAD and off_lo % LORA_PAD == 0
    low_w = lora_w + GL
    gl_a = LORA_PAD - lora_w

    mu_p, mu_n = mu_prev[0], mu_next[0]
    zc = jnp.zeros((C,), F32)
    pc = jnp.stack([mu_p[:C], mu_n[:C], mu_p[C:2 * C], mu_n[C:2 * C], mu_p[2 * C:3 * C], mu_n[2 * C:3 * C],
                    dec_w0[0, 0], dec_w0[0, 1], iclr_a0[0, 0], iclr_a0[0, 1], k_k[0], k_a[0], r_k[0], zc, zc, zc])
    plo = jnp.zeros((8, 2 * LORA_PAD), F32).at[0, :low_w].set(mu_p[3 * C:]).at[1, :low_w].set(mu_n[3 * C:])
    wl = jnp.zeros((4, LORA_PAD, C), F32)
    wl = wl.at[0, 0:DL].set(dec_w2[0, 0]).at[1, DL:2 * DL].set(dec_w2[0, 1])
    wl = wl.at[2, 2 * DL:2 * DL + IL].set(iclr_a2[0, 0]).at[3, 2 * DL + IL:lora_w].set(iclr_a2[0, 1])
    wl = wl.astype(BF16)
    g2 = jnp.zeros((2, LORA_PAD, C), F32).at[0, lora_w:].set(gate_w2[0, :gl_a]).at[1, :GL - gl_a].set(
        gate_w2[0, gl_a:]).astype(BF16)
    pq = jnp.zeros((8, C), F32).at[0].set(ln_x_w[0]).at[1].set(ln_x_b[0])
    lane = jnp.arange(LANES)
    ones_bd = (lane[:, None] // RWKV_HEAD == lane[None, :] // RWKV_HEAD).astype(BF16)

    w_d = w_ffn_down[0].astype(BF16)

    pos = jnp.concatenate([jnp.arange(N_META, N_META + S), jnp.zeros((PAD_ROWS - N_META,), jnp.int32),
                           jnp.arange(N_META)]).astype(F32)
    inv = ROPE_THETA ** (-jnp.arange(0, ROPE_DIMS, 2, dtype=F32) / ROPE_DIMS)
    ang = pos[:, None] * inv[None, :]
    ones_rest = jnp.ones((Lp, HEAD_DIM - ROPE_DIMS), F32)
    cos_t = jnp.concatenate([jnp.cos(ang), jnp.cos(ang), ones_rest], axis=1)
    sin_t = jnp.concatenate([-jnp.sin(ang), jnp.sin(ang), 0.0 * ones_rest], axis=1)

    x2 = x[0]
    h_ext = _norm_ext(x2, meta_tokens.astype(x.dtype), norm_mix_w[0])
    z = _mm_in(h_ext, w_in[0], _pick(Lp, (1040, 640)), tn_in)
    o_att = _attention_res(z, cos_t, sin_t, attn_sink[0], S, n_heads, n_kv, off_q)
    r_s, v_s, kk, lw_f, lw_b, kd_f, kd_b, bd_f, bd_b, g, bonus = _rwkv_prep(
        z, off_r, C, pc, plo, wl, g2, ones_bd, 2 * DL)
    y_f, y_b = _rwkv_scan(r_s, v_s, kk, lw_f, lw_b, kd_f, kd_b, bd_f, bd_b, S)
    o_rwkv = _rwkv_post(y_f, y_b, bonus, g, pq, ones_bd, S)
    tm = _pick(S, (1024, 512))
    mixed = _mm_mix(o_att, w_proj_attn[0], o_rwkv, w_proj_rwkv[0], z, b_gate[0], S, D, tm, 512)
    h1 = _mm_resw(mixed, w_out[0], x2, tm, 512)
    hn = _norm(h1, norm_ffn_w[0], BF16)
    act = _mm_glu(hn, w_ffn_gate[0], w_ffn_up[0], tm, 256)
    h2 = _mm_res(act, w_d, h1, tm, 512, F // 2)
    y = _norm(h2, norm_final_w, x.dtype)
    return y[None]
```

```python
import functools
import math

import jax
import jax.numpy as jnp
from jax import lax
from jax.experimental import pallas as pl
from jax.experimental.pallas import tpu as pltpu

F32 = jnp.float32
BF16 = jnp.bfloat16

N_META = 16
HEAD_DIM = 128
WINDOW = 128
BLOCK = 128
ROPE_DIMS = HEAD_DIM // 4
ROPE_THETA = 500000.0
RWKV_HEAD = 64
RMS_EPS = 1e-6
GN_EPS = 64e-5
LANES = 128
PAD_ROWS = BLOCK
CHUNK = 64
LORA_PAD = 512
VMEM_PHYSICAL = 64 * 1024 * 1024

NN = (((1,), (0,)), ((), ()))
NT = (((1,), (1,)), ((), ()))
TN = (((0,), (0,)), ((), ()))


def _vmem_limit(*block_bytes, temps=0):
    need = 2 * sum(block_bytes) + temps + (4 << 20)
    return int(min(max(need, 16 << 20), VMEM_PHYSICAL - (6 << 20)))


def _params(sem, limit):
    return pltpu.CompilerParams(dimension_semantics=sem, vmem_limit_bytes=limit)


def _pick(n, cands):
    for c in cands:
        if n % c == 0:
            return c
    raise ValueError(f"no tile in {cands} divides {n}")


def _rms(x, w):
    return x * lax.rsqrt(jnp.mean(x * x, axis=-1, keepdims=True) + RMS_EPS) * w


def _norm_ext_kernel(x_ref, meta_ref, w_ref, o_ref, *, nb):
    i = pl.program_id(0)

    @pl.when(i < nb)
    def _():
        o_ref[...] = _rms(x_ref[...], w_ref[...]).astype(o_ref.dtype)

    @pl.when(i == nb)
    def _():
        o_ref[...] = jnp.zeros(o_ref.shape, o_ref.dtype)
        o_ref[PAD_ROWS - N_META:, :] = _rms(meta_ref[...], w_ref[...]).astype(o_ref.dtype)


def _norm_ext(x, meta, w):
    S, D = x.shape
    nb = S // BLOCK
    return pl.pallas_call(
        functools.partial(_norm_ext_kernel, nb=nb),
        grid=(nb + 1,),
        in_specs=[pl.BlockSpec((BLOCK, D), lambda i: (jnp.minimum(i, nb - 1), 0)),
                  pl.BlockSpec((N_META, D), lambda i: (0, 0)),
                  pl.BlockSpec((1, D), lambda i: (0, 0))],
        out_specs=pl.BlockSpec((BLOCK, D), lambda i: (i, 0)),
        out_shape=jax.ShapeDtypeStruct((S + PAD_ROWS, D), BF16),
        compiler_params=_params(("arbitrary",), _vmem_limit(BLOCK * D * 4, BLOCK * D * 2)),
        name="norm_ext",
    )(x, meta, w.reshape(1, D))


def _norm_kernel(x_ref, w_ref, o_ref):
    o_ref[...] = _rms(x_ref[...], w_ref[...]).astype(o_ref.dtype)


def _norm(x, w, out_dtype, tr=256):
    S, D = x.shape
    return pl.pallas_call(
        _norm_kernel,
        grid=(S // tr,),
        in_specs=[pl.BlockSpec((tr, D), lambda i: (i, 0)), pl.BlockSpec((1, D), lambda i: (0, 0))],
        out_specs=pl.BlockSpec((tr, D), lambda i: (i, 0)),
        out_shape=jax.ShapeDtypeStruct((S, D), out_dtype),
        compiler_params=_params(("arbitrary",), _vmem_limit(tr * D * 4, tr * D * 4)),
        name="norm",
    )(x, w.reshape(1, D))


def _dot(a, b, dims=NN):
    return lax.dot_general(a, b, dims, preferred_element_type=F32)


def _cast_resident(src_ref, dst_ref, col0=None, n_valid=None):
    K = src_ref.shape[0]
    rows = _pick(K, (256, 128))

    def body(c, carry):
        sl = pl.ds(pl.multiple_of(c * rows, rows), rows)
        blk = src_ref[sl, :]
        if n_valid is not None:
            col = col0 + lax.broadcasted_iota(jnp.int32, blk.shape, 1)
            blk = jnp.where(col < n_valid, blk, 0.0)
        dst_ref[sl, :] = blk.astype(dst_ref.dtype)
        return carry

    lax.fori_loop(0, K // rows, body, 0)


def _mm_in_kernel(a_ref, b_ref, o_ref, bw_ref, *, n_valid):
    j, i = pl.program_id(0), pl.program_id(1)

    @pl.when(i == 0)
    def _():
        _cast_resident(b_ref, bw_ref, j * b_ref.shape[1], n_valid)

    o_ref[...] = _dot(a_ref[...], bw_ref[...])


def _mm_in(a, w, tm, tn):
    M, K = a.shape
    N = w.shape[1]
    nj = pl.cdiv(N, tn)
    return pl.pallas_call(
        functools.partial(_mm_in_kernel, n_valid=N),
        grid=(nj, M // tm),
        in_specs=[pl.BlockSpec((tm, K), lambda j, i: (i, 0)), pl.BlockSpec((K, tn), lambda j, i: (0, j))],
        out_specs=pl.BlockSpec((tm, tn), lambda j, i: (i, j)),
        out_shape=jax.ShapeDtypeStruct((M, nj * tn), F32),
        scratch_shapes=[pltpu.VMEM((K, tn), BF16)],
        compiler_params=_params(("arbitrary", "arbitrary"),
                                _vmem_limit(tm * K * 2, K * tn * 4, tm * tn * 4, temps=K * tn * 2 + tm * tn * 4)),
        name="mm_in",
    )(a, w)


def _mm_mix_kernel(a1_ref, b1_ref, a2_ref, b2_ref, z0_ref, z1_ref, bg_ref, o_ref, w1_ref, w2_ref):
    @pl.when(pl.program_id(1) == 0)
    def _():
        _cast_resident(b1_ref, w1_ref)
        _cast_resident(b2_ref, w2_ref)

    g0 = jax.nn.sigmoid(z0_ref[...] + bg_ref[0:1, :])
    g1 = jax.nn.sigmoid(z1_ref[...] + bg_ref[1:2, :])
    o_ref[...] = (g0 * _dot(a1_ref[...], w1_ref[...]) + g1 * _dot(a2_ref[...], w2_ref[...])).astype(o_ref.dtype)


def _mm_mix(o_att, w_pa, o_rwkv, w_pr, z, b_gate, S, D, tm, tn):
    K1, K2 = o_att.shape[1], o_rwkv.shape[1]
    nj = D // tn
    return pl.pallas_call(
        _mm_mix_kernel,
        grid=(nj, S // tm),
        in_specs=[pl.BlockSpec((tm, K1), lambda j, i: (i, 0)), pl.BlockSpec((K1, tn), lambda j, i: (0, j)),
                  pl.BlockSpec((tm, K2), lambda j, i: (i, 0)), pl.BlockSpec((K2, tn), lambda j, i: (0, j)),
                  pl.BlockSpec((tm, tn), lambda j, i: (i, j)), pl.BlockSpec((tm, tn), lambda j, i: (i, nj + j)),
                  pl.BlockSpec((2, tn), lambda j, i: (0, j))],
        out_specs=pl.BlockSpec((tm, tn), lambda j, i: (i, j)),
        out_shape=jax.ShapeDtypeStruct((S, D), BF16),
        scratch_shapes=[pltpu.VMEM((K1, tn), BF16), pltpu.VMEM((K2, tn), BF16)],
        compiler_params=_params(("arbitrary", "arbitrary"),
                                _vmem_limit(tm * K1 * 2, K1 * tn * 4, tm * K2 * 2, K2 * tn * 4,
                                            2 * tm * tn * 4, tm * tn * 2,
                                            temps=(K1 + K2) * tn * 2 + 3 * tm * tn * 4)),
        name="mm_mix",
    )(o_att, w_pa, o_rwkv, w_pr, z, z, b_gate)


def _mm_resw_kernel(a_ref, b_ref, r_ref, o_ref, bw_ref):
    @pl.when(pl.program_id(1) == 0)
    def _():
        _cast_resident(b_ref, bw_ref)

    o_ref[...] = r_ref[...] + _dot(a_ref[...], bw_ref[...])


def _mm_resw(a, w, res, tm, tn):
    M, K = a.shape
    N = w.shape[1]
    return pl.pallas_call(
        _mm_resw_kernel,
        grid=(N // tn, M // tm),
        in_specs=[pl.BlockSpec((tm, K), lambda j, i: (i, 0)), pl.BlockSpec((K, tn), lambda j, i: (0, j)),
                  pl.BlockSpec((tm, tn), lambda j, i: (i, j))],
        out_specs=pl.BlockSpec((tm, tn), lambda j, i: (i, j)),
        out_shape=jax.ShapeDtypeStruct((M, N), F32),
        scratch_shapes=[pltpu.VMEM((K, tn), BF16)],
        compiler_params=_params(("arbitrary", "arbitrary"),
                                _vmem_limit(tm * K * 2, K * tn * 4, 2 * tm * tn * 4, temps=K * tn * 2 + tm * tn * 4)),
        name="mm_out",
    )(a, w, res)


def _mm_res_kernel(a_ref, b_ref, r_ref, o_ref, *, nk):
    k = pl.program_id(2)
    part = _dot(a_ref[...], b_ref[...])
    if nk == 1:
        o_ref[...] = r_ref[...] + part
    else:
        @pl.when(k == 0)
        def _():
            o_ref[...] = r_ref[...] + part

        @pl.when(k > 0)
        def _():
            o_ref[...] += part


def _mm_res(a, b, res, tm, tn, tk):
    M, K = a.shape
    N = b.shape[1]
    nk = K // tk
    return pl.pallas_call(
        functools.partial(_mm_res_kernel, nk=nk),
        grid=(M // tm, N // tn, nk),
        in_specs=[pl.BlockSpec((tm, tk), lambda i, j, k: (i, k)), pl.BlockSpec((tk, tn), lambda i, j, k: (k, j)),
                  pl.BlockSpec((tm, tn), lambda i, j, k: (i, j))],
        out_specs=pl.BlockSpec((tm, tn), lambda i, j, k: (i, j)),
        out_shape=jax.ShapeDtypeStruct((M, N), F32),
        compiler_params=_params(("arbitrary", "arbitrary", "arbitrary"),
                                _vmem_limit(tm * tk * 2, tk * tn * 2, 2 * tm * tn * 4, temps=tm * tn * 4)),
        name="mm_down",
    )(a, b, res)


def _mm_glu_kernel(a_ref, bg_ref, bu_ref, o_ref, wg_ref, wu_ref):
    @pl.when(pl.program_id(1) == 0)
    def _():
        _cast_resident(bg_ref, wg_ref)
        _cast_resident(bu_ref, wu_ref)

    a = a_ref[...]
    gate = _dot(a, wg_ref[...])
    up = _dot(a, wu_ref[...])
    o_ref[...] = (gate * jax.nn.sigmoid(gate) * up).astype(o_ref.dtype)


def _mm_glu(a, wg, wu, tm, tn):
    M, K = a.shape
    N = wg.shape[1]
    return pl.pallas_call(
        _mm_glu_kernel,
        grid=(N // tn, M // tm),
        in_specs=[pl.BlockSpec((tm, K), lambda j, i: (i, 0)), pl.BlockSpec((K, tn), lambda j, i: (0, j)),
                  pl.BlockSpec((K, tn), lambda j, i: (0, j))],
        out_specs=pl.BlockSpec((tm, tn), lambda j, i: (i, j)),
        out_shape=jax.ShapeDtypeStruct((M, N), BF16),
        scratch_shapes=[pltpu.VMEM((K, tn), BF16), pltpu.VMEM((K, tn), BF16)],
        compiler_params=_params(("arbitrary", "arbitrary"),
                                _vmem_limit(tm * K * 2, 2 * K * tn * 4, tm * tn * 2,
                                            temps=2 * K * tn * 2 + 3 * tm * tn * 4)),
        name="mm_glu",
    )(a, wg, wu)


def _rope_kernel(z_ref, cos_ref, sin_ref, o_ref, *, n_rope_blocks, heads_per_block, k_heads_last):
    j = pl.program_id(1)
    cos = cos_ref[...]
    sin = sin_ref[...]
    lane = lax.broadcasted_iota(jnp.int32, cos.shape, 1)
    half = ROPE_DIMS // 2
    for h in range(heads_per_block):
        t = z_ref[:, h * HEAD_DIM:(h + 1) * HEAD_DIM]
        up = jnp.concatenate([t[:, half:], t[:, :half]], axis=1)
        down = jnp.concatenate([t[:, HEAD_DIM - half:], t[:, :HEAD_DIM - half]], axis=1)
        partner = jnp.where(lane < half, up, down)
        rot = t * cos + partner * sin
        if h >= k_heads_last:
            rot = jnp.where(j == n_rope_blocks - 1, t, rot)
        o_ref[:, h * HEAD_DIM:(h + 1) * HEAD_DIM] = rot.astype(o_ref.dtype)


def _rope_cast(z, cos_t, sin_t, off_q, width, kv_width):
    Lp = z.shape[0]
    tc = 2 * kv_width
    assert off_q % tc == 0 and width % tc == 0
    nblk = width // tc
    return pl.pallas_call(
        functools.partial(_rope_kernel, n_rope_blocks=nblk, heads_per_block=tc // HEAD_DIM,
                          k_heads_last=kv_width // HEAD_DIM),
        grid=(Lp // BLOCK, nblk),
        in_specs=[pl.BlockSpec((BLOCK, tc), lambda i, j: (i, off_q // tc + j)),
                  pl.BlockSpec((BLOCK, HEAD_DIM), lambda i, j: (i, 0)),
                  pl.BlockSpec((BLOCK, HEAD_DIM), lambda i, j: (i, 0))],
        out_specs=pl.BlockSpec((BLOCK, tc), lambda i, j: (i, j)),
        out_shape=jax.ShapeDtypeStruct((Lp, width), BF16),
        compiler_params=_params(("arbitrary", "arbitrary"), _vmem_limit(BLOCK * tc * 4, BLOCK * tc * 2)),
        name="rope_cast",
    )(z, cos_t, sin_t)


NEG_BIG = -1e30


def _attn_kernel(q_ref, kp_ref, kc_ref, kn_ref, km_ref, vp_ref, vc_ref, vn_ref, vm_ref, sink_ref, o_ref,
                 *, nb, group):
    i = pl.program_id(1)
    scale = HEAD_DIM ** -0.5
    zpad = jnp.zeros((BLOCK - N_META, HEAD_DIM), BF16)
    k_all = jnp.concatenate([kp_ref[...], kc_ref[...], kn_ref[...], km_ref[...], zpad], axis=0)
    v_all = jnp.concatenate([vp_ref[...], vc_ref[...], vn_ref[...], vm_ref[...], zpad], axis=0)
    q_all = jnp.concatenate([q_ref[:, g * HEAD_DIM:(g + 1) * HEAD_DIM] for g in range(group)], axis=0)
    s = _dot(q_all, k_all, NT) * scale
    rows = lax.broadcasted_iota(jnp.int32, s.shape, 0) % BLOCK
    cols = lax.broadcasted_iota(jnp.int32, s.shape, 1)
    rel = cols - BLOCK - rows
    kblk = i - 1 + cols // BLOCK
    band_ok = (cols < 3 * BLOCK) & (kblk >= 0) & (kblk < nb) & (jnp.abs(rel) <= WINDOW)
    meta_ok = (cols >= 3 * BLOCK) & (cols < 3 * BLOCK + N_META)
    s = jnp.where(band_ok | meta_ok, s, NEG_BIG)
    sink = sink_ref[:, 0:1]
    m = jnp.maximum(jnp.max(s, axis=-1, keepdims=True), sink)
    p = jnp.exp(s - m)
    denom = jnp.sum(p, axis=-1, keepdims=True) + jnp.exp(sink - m)
    o = _dot(p.astype(BF16), v_all) / denom
    for g in range(group):
        o_ref[:, g * HEAD_DIM:(g + 1) * HEAD_DIM] = o[g * BLOCK:(g + 1) * BLOCK, :].astype(o_ref.dtype)


def _attention(qkv, sink, S, n_heads, n_kv):
    group = n_heads // n_kv
    nb = S // BLOCK
    qw = group * HEAD_DIM
    kcol = n_heads
    vcol = n_heads + n_kv
    meta_blk = (S + PAD_ROWS - N_META) // N_META
    sink_rows = jnp.broadcast_to(sink.astype(F32).reshape(n_kv, group, 1, 1),
                                 (n_kv, group, BLOCK, LANES)).reshape(n_kv, group * BLOCK, LANES)

    def kv_specs(col0):
        return [pl.BlockSpec((BLOCK, HEAD_DIM), lambda h, i: (jnp.maximum(i - 1, 0), col0 + h)),
                pl.BlockSpec((BLOCK, HEAD_DIM), lambda h, i: (i, col0 + h)),
                pl.BlockSpec((BLOCK, HEAD_DIM), lambda h, i: (jnp.minimum(i + 1, nb - 1), col0 + h)),
                pl.BlockSpec((N_META, HEAD_DIM), lambda h, i: (meta_blk, col0 + h))]

    return pl.pallas_call(
        functools.partial(_attn_kernel, nb=nb, group=group),
        grid=(n_kv, nb),
        in_specs=[pl.BlockSpec((BLOCK, qw), lambda h, i: (i, h))] + kv_specs(kcol) + kv_specs(vcol)
                 + [pl.BlockSpec((None, group * BLOCK, LANES), lambda h, i: (h, 0, 0))],
        out_specs=pl.BlockSpec((BLOCK, qw), lambda h, i: (i, h)),
        out_shape=jax.ShapeDtypeStruct((S, n_heads * HEAD_DIM), BF16),
        compiler_params=_params(("arbitrary", "arbitrary"),
                                _vmem_limit(BLOCK * qw * 2 * 2, 8 * BLOCK * HEAD_DIM * 2,
                                            group * BLOCK * LANES * 4, temps=6 * group * BLOCK * 512 * 4)),
        name="attention",
    )(qkv, qkv, qkv, qkv, qkv, qkv, qkv, qkv, qkv, sink_rows)


def _rope(t, cos, sin):
    half = ROPE_DIMS // 2
    lane = lax.broadcasted_iota(jnp.int32, t.shape, 1)
    up = jnp.concatenate([t[:, half:], t[:, :half]], axis=1)
    down = jnp.concatenate([t[:, HEAD_DIM - half:], t[:, :HEAD_DIM - half]], axis=1)
    return t * cos + jnp.where(lane < half, up, down) * sin


def _attn_res_kernel(sink_ref, q_ref, k_ref, v_ref, cos_ref, sin_ref, o_ref, kb_ref, vb_ref,
                     *, nb, group, qblocks):
    h, i = pl.program_id(0), pl.program_id(1)
    lp = k_ref.shape[0]

    @pl.when(i == 0)
    def _():
        def body(c, carry):
            sl = pl.ds(pl.multiple_of(c * BLOCK, BLOCK), BLOCK)
            kb_ref[sl, :] = _rope(k_ref[sl, :], cos_ref[sl, :], sin_ref[sl, :]).astype(BF16)
            vb_ref[sl, :] = v_ref[sl, :].astype(BF16)
            return carry
        lax.fori_loop(0, lp // BLOCK, body, 0)

    zpad = jnp.zeros((BLOCK - N_META, HEAD_DIM), BF16)
    k_meta = jnp.concatenate([kb_ref[lp - N_META:lp, :], zpad], axis=0)
    v_meta = jnp.concatenate([vb_ref[lp - N_META:lp, :], zpad], axis=0)
    rows = lax.broadcasted_iota(jnp.int32, (BLOCK, 4 * BLOCK), 0)
    cols = lax.broadcasted_iota(jnp.int32, (BLOCK, 4 * BLOCK), 1)
    is_meta = (cols >= 3 * BLOCK) & (cols < 3 * BLOCK + N_META)
    scale = HEAD_DIM ** -0.5
    k_all, v_all, bias, q, sink = [], [], [], [], []
    for b in range(qblocks):
        qi = i * qblocks + b
        kb0 = jnp.clip(qi - 1, 0, nb - 3)
        win = pl.ds(pl.multiple_of(kb0 * BLOCK, BLOCK), 3 * BLOCK)
        k_b = jnp.concatenate([kb_ref[win, :], k_meta], axis=0)
        v_b = jnp.concatenate([vb_ref[win, :], v_meta], axis=0)
        rel = cols - rows + (kb0 - qi) * BLOCK
        ok = ((cols < 3 * BLOCK) & (jnp.abs(rel) <= WINDOW)) | is_meta
        bias_b = jnp.where(ok, 0.0, NEG_BIG).astype(F32)
        qrows = pl.ds(pl.multiple_of(qi * BLOCK, BLOCK), BLOCK)
        cos_q, sin_q = cos_ref[qrows, :], sin_ref[qrows, :]
        for g in range(group):
            k_all.append(k_b)
            v_all.append(v_b)
            bias.append(bias_b)
            q.append(_rope(q_ref[b * BLOCK:(b + 1) * BLOCK, g * HEAD_DIM:(g + 1) * HEAD_DIM],
                           cos_q, sin_q).astype(BF16))
            sink.append(sink_ref[h * group + g])
    n = range(len(q))
    s = [_dot(q[c], k_all[c], NT) * scale + bias[c] for c in n]
    m = [jnp.maximum(jnp.max(s[c], axis=-1, keepdims=True), sink[c]) for c in n]
    p = [jnp.exp(s[c] - m[c]) for c in n]
    denom = [jnp.sum(p[c], axis=-1, keepdims=True) + jnp.exp(sink[c] - m[c]) for c in n]
    o = [_dot(p[c].astype(BF16), v_all[c]) / denom[c] for c in n]
    for c in n:
        b, g = divmod(c, group)
        o_ref[b * BLOCK:(b + 1) * BLOCK, g * HEAD_DIM:(g + 1) * HEAD_DIM] = o[c].astype(o_ref.dtype)


def _attention_res(z, cos_t, sin_t, sink, S, n_heads, n_kv, off_q):
    Lp = z.shape[0]
    group = n_heads // n_kv
    nb = S // BLOCK
    assert nb >= 3
    qw = group * HEAD_DIM
    kcol = off_q // HEAD_DIM + n_heads
    vcol = kcol + n_kv
    col_bytes = Lp * HEAD_DIM * 4
    qblocks = _pick(nb, (2, 1))
    tq = qblocks * BLOCK
    return pl.pallas_call(
        functools.partial(_attn_res_kernel, nb=nb, group=group, qblocks=qblocks),
        grid_spec=pltpu.PrefetchScalarGridSpec(
            num_scalar_prefetch=1,
            grid=(n_kv, nb // qblocks),
            in_specs=[pl.BlockSpec((tq, qw), lambda h, i, s: (i, off_q // qw + h)),
                      pl.BlockSpec((Lp, HEAD_DIM), lambda h, i, s: (0, kcol + h)),
                      pl.BlockSpec((Lp, HEAD_DIM), lambda h, i, s: (0, vcol + h)),
                      pl.BlockSpec((Lp, HEAD_DIM), lambda h, i, s: (0, 0)),
                      pl.BlockSpec((Lp, HEAD_DIM), lambda h, i, s: (0, 0))],
            out_specs=pl.BlockSpec((tq, qw), lambda h, i, s: (i, h)),
            scratch_shapes=[pltpu.VMEM((Lp, HEAD_DIM), BF16), pltpu.VMEM((Lp, HEAD_DIM), BF16)]),
        out_shape=jax.ShapeDtypeStruct((S, n_heads * HEAD_DIM), BF16),
        compiler_params=_params(("arbitrary", "arbitrary"),
                                _vmem_limit(tq * qw * 4, 4 * col_bytes, tq * qw * 2,
                                            temps=col_bytes + 8 * group * tq * 512 * 4)),
        name="attention",
    )(sink.astype(F32), z, z, z, cos_t, sin_t)


def _split3(x):
    hi = x.astype(BF16)
    r1 = x - hi.astype(F32)
    mid = r1.astype(BF16)
    lo = (r1 - mid.astype(F32)).astype(BF16)
    return hi, mid, lo


def _split2(x):
    hi = x.astype(BF16)
    return hi, (x - hi.astype(F32)).astype(BF16)


def _dot_exact_rhs(x, m_bf16, dims=NN, left=False):
    parts = _split3(x)
    if left:
        return sum(_dot(m_bf16, p, dims) for p in parts)
    return sum(_dot(p, m_bf16, dims) for p in parts)


def _dotp(a, b, passes, dims=NN):
    if passes == 1:
        return _dot(a.astype(BF16), b.astype(BF16), dims)
    a_hi, a_lo = _split2(a)
    b_hi, b_lo = _split2(b)
    return _dot(a_hi, b_hi, dims) + (_dot(a_hi, b_lo, dims) + _dot(a_lo, b_hi, dims))


def _head_sum(x, ones_bd):
    outs = []
    for s in range(x.shape[1] // LANES):
        outs.append(_dot_exact_rhs(x[:, s * LANES:(s + 1) * LANES], ones_bd))
    return outs[0] if len(outs) == 1 else jnp.concatenate(outs, axis=1)


def _prep_kernel(r_ref, rp_ref, rn_ref, k_ref, kp_ref, kn_ref, v_ref, vp_ref, vn_ref,
                 lo_ref, lop_ref, lon_ref, gd_ref, gdp_ref, gdn_ref,
                 pc_ref, pl_ref, wl_ref, g2_ref, ones_ref,
                 ro_ref, vo_ref, kko_ref, lwf_ref, lwb_ref, kdf_ref, kdb_ref, bdf_ref, bdb_ref, go_ref, bo_ref,
                 buf_ref, *, n_tiles, tr, lora_tanh_cols):
    i = pl.program_id(1)

    def shifted(main_ref, prev_ref, next_ref, mup, mun):
        w = main_ref.shape[1]
        buf_ref[0:8, 0:w] = prev_ref[...]
        buf_ref[8:8 + tr, 0:w] = main_ref[...]
        buf_ref[8 + tr:16 + tr, 0:w] = next_ref[...]
        x = main_ref[...]
        prev = buf_ref[7:7 + tr, 0:w]
        nxt = buf_ref[9:9 + tr, 0:w]
        return x + mup * (prev - x) + mun * (nxt - x)

    r = shifted(r_ref, rp_ref, rn_ref, pc_ref[0:1, :], pc_ref[1:2, :])
    k = shifted(k_ref, kp_ref, kn_ref, pc_ref[2:3, :], pc_ref[3:4, :])
    v = shifted(v_ref, vp_ref, vn_ref, pc_ref[4:5, :], pc_ref[5:6, :])
    lo = shifted(lo_ref, lop_ref, lon_ref, pl_ref[0:1, 0:LORA_PAD], pl_ref[1:2, 0:LORA_PAD])
    gd = shifted(gd_ref, gdp_ref, gdn_ref, pl_ref[0:1, LORA_PAD:], pl_ref[1:2, LORA_PAD:])

    row = lax.broadcasted_iota(jnp.int32, (tr, 1), 0)
    valid = jnp.where((i < n_tiles - 1) | (row >= tr - N_META), 1.0, 0.0).astype(F32)

    lane = lax.broadcasted_iota(jnp.int32, lo.shape, 1)
    lo_act = jnp.where(lane < lora_tanh_cols, jnp.tanh(lo), lo).astype(BF16)
    dec_f = _dot(lo_act, wl_ref[0])
    dec_b = _dot(lo_act, wl_ref[1])
    apre_f = _dot(lo_act, wl_ref[2])
    apre_b = _dot(lo_act, wl_ref[3])
    g = _dot(jax.nn.sigmoid(lo).astype(BF16), g2_ref[0]) + _dot(jax.nn.sigmoid(gd).astype(BF16), g2_ref[1])

    def log_decay(dec, w0):
        return -math.exp(-0.5) * jax.nn.sigmoid(w0 + dec)

    a_f = jax.nn.sigmoid(pc_ref[8:9, :] + apre_f)
    a_b = jax.nn.sigmoid(pc_ref[9:10, :] + apre_b)
    kk = k * pc_ref[10:11, :]
    ss = _head_sum(kk * kk, ones_ref[...])
    kk = kk / jnp.maximum(jnp.sqrt(ss), 1e-12) * valid
    k_a = pc_ref[11:12, :]
    kv = k * valid

    rv = r * valid
    vv = v * valid
    kd_f = kv * (1.0 + (a_f - 1.0) * k_a)
    kd_b = kv * (1.0 + (a_b - 1.0) * k_a)
    ro_ref[...] = rv
    vo_ref[...] = vv
    kko_ref[...] = kk
    lwf_ref[...] = log_decay(dec_f, pc_ref[6:7, :])
    lwb_ref[...] = log_decay(dec_b, pc_ref[7:8, :])
    kdf_ref[...] = kd_f
    kdb_ref[...] = kd_b
    bdf_ref[...] = kk * a_f
    bdb_ref[...] = kk * a_b
    go_ref[...] = g
    bo_ref[...] = _head_sum(rv * (kd_f + kd_b) * pc_ref[12:13, :], ones_ref[...]) * vv


def _rwkv_prep(z, off_r, C, pc, plo, wl, g2, ones_bd, lora_tanh_cols):
    Lp = z.shape[0]
    tr = BLOCK
    ct = _pick(C, (1024, 512))
    n_tiles = Lp // tr
    n8 = Lp // 8
    off_lo = off_r + 3 * C
    off_gd = off_lo + LORA_PAD
    assert off_r % ct == 0 and off_lo % LORA_PAD == 0

    def seg_specs(off, w, with_c):
        cb = off // w

        def col(c):
            return cb + c if with_c else cb
        return [pl.BlockSpec((tr, w), lambda c, i: (i, col(c))),
                pl.BlockSpec((8, w), lambda c, i: ((i * (tr // 8) + n8 - 1) % n8, col(c))),
                pl.BlockSpec((8, w), lambda c, i: (((i + 1) * (tr // 8)) % n8, col(c)))]

    in_specs = (seg_specs(off_r, ct, True) + seg_specs(off_r + C, ct, True) + seg_specs(off_r + 2 * C, ct, True)
                + seg_specs(off_lo, LORA_PAD, False) + seg_specs(off_gd, LORA_PAD, False)
                + [pl.BlockSpec((16, ct), lambda c, i: (0, c)),
                   pl.BlockSpec((8, 2 * LORA_PAD), lambda c, i: (0, 0)),
                   pl.BlockSpec((4, LORA_PAD, ct), lambda c, i: (0, 0, c)),
                   pl.BlockSpec((2, LORA_PAD, ct), lambda c, i: (0, 0, c)),
                   pl.BlockSpec((LANES, LANES), lambda c, i: (0, 0))])
    out_spec = pl.BlockSpec((tr, ct), lambda c, i: (i, c))
    out_sds = jax.ShapeDtypeStruct((Lp, C), F32)
    return pl.pallas_call(
        functools.partial(_prep_kernel, n_tiles=n_tiles, tr=tr, lora_tanh_cols=lora_tanh_cols),
        grid=(C // ct, n_tiles),
        in_specs=in_specs,
        out_specs=[out_spec] * 11,
        out_shape=[out_sds] * 11,
        scratch_shapes=[pltpu.VMEM((tr + 16, max(ct, LORA_PAD)), F32)],
        compiler_params=_params(("arbitrary", "arbitrary"),
                                _vmem_limit(3 * tr * ct * 4, 2 * tr * LORA_PAD * 4, 4 * LORA_PAD * ct * 2,
                                            2 * LORA_PAD * ct * 2, 11 * tr * ct * 4, temps=24 * tr * ct * 4)),
        name="rwkv_prep",
    )(*([z] * 15), pc, plo, wl, g2, ones_bd)


SCAN_PAIRS = 8


def _scan_consts():
    C = CHUNK
    row = lax.broadcasted_iota(jnp.int32, (C, 2 * C), 0)
    col = lax.broadcasted_iota(jnp.int32, (C, 2 * C), 1) % C
    rr = lax.broadcasted_iota(jnp.int32, (LANES, LANES), 0)
    cc = lax.broadcasted_iota(jnp.int32, (LANES, LANES), 1)
    return dict(
        row=lax.broadcasted_iota(jnp.int32, (C, LANES), 0),
        head0=lax.broadcasted_iota(jnp.int32, (C, LANES), 1) < RWKV_HEAD,
        strict_f=col < row, strict_b=col > row, incl_f=col <= row, incl_b=col >= row,
        eye_sbs=jnp.where(col == row, 1.0, 0.0).astype(F32),
        same_head=(rr // RWKV_HEAD) == (cc // RWKV_HEAD), eye=rr == cc)


def _scan_chains(chains, cst):
    C = CHUNK
    head0, same, eye = cst["head0"], cst["same_head"], cst["eye"]
    n = range(len(chains))
    rev = [c[7] for c in chains]
    r, v, al, lw, k, be, h = ([c[i] for c in chains] for i in range(7))
    strict = [cst["strict_b" if x else "strict_f"] for x in rev]
    incl = [cst["incl_b" if x else "incl_f"] for x in rev]

    def bd(x):
        xb = x.astype(BF16)
        zero = jnp.zeros_like(xb)
        return jnp.concatenate([jnp.where(head0, xb, zero), jnp.where(head0, zero, xb)], axis=0)

    def mm(a, b_bf16, dims=NN):
        return _dot(a.astype(BF16), b_bf16, dims)

    row = cst["row"]
    cl = list(lw)
    sh = 1
    while sh < C:
        cl = [cl[i] + (jnp.where(row < C - sh, pltpu.roll(cl[i], C - sh, 0), 0.0) if rev[i] else
                       jnp.where(row >= sh, pltpu.roll(cl[i], sh, 0), 0.0)) for i in n]
        sh *= 2
    total = [cl[i][0:1, :] if rev[i] else cl[i][C - 1:C, :] for i in n]
    a_t = [al[i] * jnp.exp(cl[i] - lw[i]) for i in n]
    r_t = [r[i] * jnp.exp(cl[i]) for i in n]
    w_inv = [jnp.exp(-cl[i]) for i in n]
    w_rest = [jnp.exp(total[i] - cl[i]) for i in n]
    bd_v = [bd(v[i]) for i in n]
    sc = [mm(jnp.concatenate([a_t[i], r_t[i]], axis=0),
             jnp.concatenate([bd(k[i] * w_inv[i]), bd(be[i] * w_inv[i])], axis=0), NT) for i in n]
    a_ak = [jnp.where(strict[i], sc[i][0:C, 0:2 * C], 0.0) for i in n]
    n_ab = [jnp.where(strict[i], sc[i][0:C, 2 * C:], 0.0) for i in n]
    a_rk = [jnp.where(incl[i], sc[i][C:, 0:2 * C], 0.0) for i in n]
    a_rb = [jnp.where(incl[i], sc[i][C:, 2 * C:], 0.0) for i in n]

    t_inv = [cst["eye_sbs"] - n_ab[i] for i in n]
    sq = [mm(n_ab[i], bd(n_ab[i])) for i in n]
    n_round = 2
    while (1 << n_round) < C:
        both = [mm(jnp.concatenate([t_inv[i], sq[i]], axis=0), bd(sq[i])) for i in n]
        t_inv = [t_inv[i] + both[i][0:C] for i in n]
        sq = [both[i][C:] for i in n]
        n_round += 1
    t_inv = [t_inv[i] + mm(t_inv[i], bd(sq[i])) for i in n]

    akv = [mm(jnp.concatenate([a_ak[i], a_rk[i]], axis=0), bd_v[i]) for i in n]
    tp = [mm(t_inv[i], jnp.concatenate([bd(akv[i][0:C]), bd(a_t[i])], axis=1)) for i in n]
    p0 = [tp[i][:, 0:LANES] for i in n]
    at = [tp[i][:, LANES:] for i in n]
    rb = [mm(a_rb[i], jnp.concatenate([bd(p0[i]), bd(at[i])], axis=1)) for i in n]
    y0 = [akv[i][C:] - rb[i][:, 0:LANES] for i in n]
    rh = [r_t[i] - rb[i][:, LANES:] for i in n]
    bt = [mm(be[i] * w_rest[i], jnp.concatenate([at[i], p0[i]], axis=1).astype(BF16), TN) for i in n]
    kv = [mm(k[i] * w_rest[i], v[i].astype(BF16), TN) for i in n]
    m_mat = [jnp.where(same, jnp.where(eye, jnp.broadcast_to(jnp.exp(total[i]), (LANES, LANES)), 0.0)
                       - bt[i][:, 0:LANES], 0.0) for i in n]
    g_mat = [jnp.where(same, kv[i] - bt[i][:, LANES:], 0.0) for i in n]
    yh = [mm(jnp.concatenate([rh[i], m_mat[i]], axis=0), h[i].astype(BF16)) for i in n]
    return [(yh[i][0:C] + y0[i], yh[i][C:] + g_mat[i]) for i in n]


def _scan_kernel(rf_ref, vf_ref, af_ref, lwf_ref, kf_ref, bf_ref,
                 rb_ref, vb_ref, ab_ref, lwb_ref, kb_ref, bb_ref,
                 yf_ref, yb_ref, hf_ref, hb_ref, *, pairs):
    @pl.when(pl.program_id(1) == 0)
    def _():
        hf_ref[...] = jnp.zeros(hf_ref.shape, F32)
        hb_ref[...] = jnp.zeros(hb_ref.shape, F32)

    cst = _scan_consts()
    chains = []
    for g in range(pairs):
        sl = slice(g * LANES, (g + 1) * LANES)
        chains.append((rf_ref[:, sl], vf_ref[:, sl], af_ref[:, sl], lwf_ref[:, sl], kf_ref[:, sl], bf_ref[:, sl],
                       hf_ref[g], False))
        chains.append((rb_ref[:, sl], vb_ref[:, sl], ab_ref[:, sl], lwb_ref[:, sl], kb_ref[:, sl], bb_ref[:, sl],
                       hb_ref[g], True))
    res = _scan_chains(chains, cst)
    for g in range(pairs):
        sl = slice(g * LANES, (g + 1) * LANES)
        yf_ref[:, sl], hf_ref[g] = res[2 * g]
        yb_ref[:, sl], hb_ref[g] = res[2 * g + 1]


def _rwkv_scan(r, v, kk, lw_f, lw_b, kd_f, kd_b, bd_f, bd_b, S):
    Lp, C = r.shape
    n_real = S // CHUNK
    n_chunks = Lp // CHUNK
    steps = n_real + 1
    pairs = _pick(C // LANES, (SCAN_PAIRS, 2, 1))
    w = pairs * LANES

    def fwd(p, s):
        return ((s + n_real + 1) % n_chunks, p)

    def bwd(p, s):
        return (n_real - s, p)

    spec_f = pl.BlockSpec((CHUNK, w), fwd)
    spec_b = pl.BlockSpec((CHUNK, w), bwd)
    out_sds = jax.ShapeDtypeStruct((S + CHUNK, C), F32)
    out_f = pl.BlockSpec((CHUNK, w), lambda p, s: ((s + n_real) % (n_real + 1), p))
    return pl.pallas_call(
        functools.partial(_scan_kernel, pairs=pairs),
        grid=(C // w, steps),
        in_specs=[spec_f] * 6 + [spec_b] * 6,
        out_specs=[out_f, spec_b],
        out_shape=[out_sds, out_sds],
        scratch_shapes=[pltpu.VMEM((pairs, LANES, LANES), F32), pltpu.VMEM((pairs, LANES, LANES), F32)],
        compiler_params=_params(("arbitrary", "arbitrary"), 32 << 20),
        name="rwkv_scan",
    )(r, v, kk, lw_f, kd_f, bd_f, r, v, kk, lw_b, kd_b, bd_b)


def _post_kernel(yf_ref, yb_ref, bonus_ref, g_ref, pq_ref, ones_ref, o_ref):
    ones_bd = ones_ref[...]
    inv_n = 1.0 / RWKV_HEAD
    y = yf_ref[...] + yb_ref[...]
    mean = _head_sum(y, ones_bd) * inv_n
    yc = y - mean
    var = _head_sum(yc * yc, ones_bd) * inv_n
    yn = yc * lax.rsqrt(var + GN_EPS) * pq_ref[0:1, :] + pq_ref[1:2, :]
    o_ref[...] = ((yn + bonus_ref[...]) * g_ref[...]).astype(o_ref.dtype)


def _rwkv_post(y_f, y_b, bonus, g, pq, ones_bd, S):
    C = g.shape[1]
    tr = 256
    ct = _pick(C, (1024, 512))
    spec = pl.BlockSpec((tr, ct), lambda i, c: (i, c))
    return pl.pallas_call(
        _post_kernel,
        grid=(S // tr, C // ct),
        in_specs=[spec] * 4 + [pl.BlockSpec((8, ct), lambda i, c: (0, c)),
                               pl.BlockSpec((LANES, LANES), lambda i, c: (0, 0))],
        out_specs=spec,
        out_shape=jax.ShapeDtypeStruct((S, C), BF16),
        compiler_params=_params(("arbitrary", "arbitrary"),
                                _vmem_limit(4 * tr * ct * 4, tr * ct * 2, temps=16 * tr * ct * 4)),
        name="rwkv_post",
    )(y_f, y_b, bonus, g, pq, ones_bd)


def _pad_cols(w, width):
    return jnp.pad(w, ((0, 0), (0, width - w.shape[1])))


def kernel(x, meta_tokens, norm_mix_w, w_in, b_gate, mu_prev, mu_next, dec_w0, dec_w2, iclr_a0, iclr_a2,
           gate_w2, k_k, k_a, r_k, ln_x_w, ln_x_b, attn_sink, w_proj_attn, w_proj_rwkv, w_out, norm_ffn_w,
           w_ffn_gate, w_ffn_up, w_ffn_down, norm_final_w):
    B, S, D = x.shape
    assert B == 1 and norm_mix_w.shape[0] == 1, "one sequence, one layer"
    C = k_k.shape[-1]
    AW = w_proj_attn.shape[1]
    n_heads = attn_sink.shape[-1]
    DL, IL, GL = dec_w2.shape[2], iclr_a2.shape[2], gate_w2.shape[1]
    shift_w = mu_prev.shape[-1]
    KVW = (w_in.shape[-1] - 2 * D - AW - shift_w) // 2
    n_kv = KVW // HEAD_DIM
    F = w_ffn_gate.shape[-1]
    lora_w = 2 * DL + 2 * IL
    assert shift_w == 3 * C + lora_w + GL and lora_w <= LORA_PAD and GL <= LORA_PAD
    assert S % BLOCK == 0 and C % LANES == 0 and n_heads * HEAD_DIM == AW
    Lp = S + PAD_ROWS

    off_q = 2 * D
    off_r = off_q + AW + 2 * KVW
    off_lo = off_r + 3 * C
    tn_in = 512
    n_in_pad = -(-w_in.shape[-1] // tn_in) * tn_in
    assert n_in_pad >= off_lo + 2 * LORA_PAD and off_lo % LORA_PAD == 0
    low_w = lora_w + GL
    gl_a = LORA_PAD - lora_w

    mu_p, mu_n = mu_prev[0], mu_next[0]
    zc = jnp.zeros((C,), F32)
    pc = jnp.stack([mu_p[:C], mu_n[:C], mu_p[C:2 * C], mu_n[C:2 * C], mu_p[2 * C:3 * C], mu_n[2 * C:3 * C],
                    dec_w0[0, 0], dec_w0[0, 1], iclr_a0[0, 0], iclr_a0[0, 1], k_k[0], k_a[0], r_k[0], zc, zc, zc])
    plo = jnp.zeros((8, 2 * LORA_PAD), F32).at[0, :low_w].set(mu_p[3 * C:]).at[1, :low_w].set(mu_n[3 * C:])
    wl = jnp.zeros((4, LORA_PAD, C), F32)
    wl = wl.at[0, 0:DL].set(dec_w2[0, 0]).at[1, DL:2 * DL].set(dec_w2[0, 1])
    wl = wl.at[2, 2 * DL:2 * DL + IL].set(iclr_a2[0, 0]).at[3, 2 * DL + IL:lora_w].set(iclr_a2[0, 1])
    wl = wl.astype(BF16)
    g2 = jnp.zeros((2, LORA_PAD, C), F32).at[0, lora_w:].set(gate_w2[0, :gl_a]).at[1, :GL - gl_a].set(
        gate_w2[0, gl_a:]).astype(BF16)
    pq = jnp.zeros((8, C), F32).at[0].set(ln_x_w[0]).at[1].set(ln_x_b[0])
    lane = jnp.arange(LANES)
    ones_bd = (lane[:, None] // RWKV_HEAD == lane[None, :] // RWKV_HEAD).astype(BF16)

    w_d = w_ffn_down[0].astype(BF16)

    pos = jnp.concatenate([jnp.arange(N_META, N_META + S), jnp.zeros((PAD_ROWS - N_META,), jnp.int32),
                           jnp.arange(N_META)]).astype(F32)
    inv = ROPE_THETA ** (-jnp.arange(0, ROPE_DIMS, 2, dtype=F32) / ROPE_DIMS)
    ang = pos[:, None] * inv[None, :]
    ones_rest = jnp.ones((Lp, HEAD_DIM - ROPE_DIMS), F32)
    cos_t = jnp.concatenate([jnp.cos(ang), jnp.cos(ang), ones_rest], axis=1)
    sin_t = jnp.concatenate([-jnp.sin(ang), jnp.sin(ang), 0.0 * ones_rest], axis=1)

    x2 = x[0]
    h_ext = _norm_ext(x2, meta_tokens.astype(x.dtype), norm_mix_w[0])
    z = _mm_in(h_ext, w_in[0], _pick(Lp, (1040, 640)), tn_in)
    o_att = _attention_res(z, cos_t, sin_t, attn_sink[0], S, n_heads, n_kv, off_q)
    r_s, v_s, kk, lw_f, lw_b, kd_f, kd_b, bd_f, bd_b, g, bonus = _rwkv_prep(
        z, off_r, C, pc, plo, wl, g2, ones_bd, 2 * DL)
    y_f, y_b = _rwkv_scan(r_s, v_s, kk, lw_f, lw_b, kd_f, kd_b, bd_f, bd_b, S)
    o_rwkv = _rwkv_post(y_f, y_b, bonus, g, pq, ones_bd, S)
    tm = _pick(S, (1024, 512))
    mixed = _mm_mix(o_att, w_proj_attn[0], o_rwkv, w_proj_rwkv[0], z, b_gate[0], S, D, tm, 512)
    h1 = _mm_resw(mixed, w_out[0], x2, tm, 512)
    hn = _norm(h1, norm_ffn_w[0], BF16)
    act = _mm_glu(hn, w_ffn_gate[0], w_ffn_up[0], tm, 256)
    h2 = _mm_res(act, w_d, h1, tm, 512, F // 2)
    y = _norm(h2, norm_final_w, x.dtype)
    return y[None]
```

```python
import functools
import math

import jax
import jax.numpy as jnp
from jax import lax
from jax.experimental import pallas as pl
from jax.experimental.pallas import tpu as pltpu

F32 = jnp.float32
BF16 = jnp.bfloat16

N_META = 16
HEAD_DIM = 128
WINDOW = 128
BLOCK = 128
ROPE_DIMS = HEAD_DIM // 4
ROPE_THETA = 500000.0
RWKV_HEAD = 64
RMS_EPS = 1e-6
GN_EPS = 64e-5
LANES = 128
PAD_ROWS = BLOCK
CHUNK = 64
LORA_PAD = 512
VMEM_PHYSICAL = 64 * 1024 * 1024

NN = (((1,), (0,)), ((), ()))
NT = (((1,), (1,)), ((), ()))
TN = (((0,), (0,)), ((), ()))


def _vmem_limit(*block_bytes, temps=0):
    need = 2 * sum(block_bytes) + temps + (4 << 20)
    return int(min(max(need, 16 << 20), VMEM_PHYSICAL - (6 << 20)))


def _params(sem, limit):
    return pltpu.CompilerParams(dimension_semantics=sem, vmem_limit_bytes=limit)


def _pick(n, cands):
    for c in cands:
        if n % c == 0:
            return c
    raise ValueError(f"no tile in {cands} divides {n}")


def _rms(x, w):
    return x * lax.rsqrt(jnp.mean(x * x, axis=-1, keepdims=True) + RMS_EPS) * w


def _norm_ext_kernel(x_ref, meta_ref, w_ref, o_ref, *, nb):
    i = pl.program_id(0)

    @pl.when(i < nb)
    def _():
        o_ref[...] = _rms(x_ref[...], w_ref[...]).astype(o_ref.dtype)

    @pl.when(i == nb)
    def _():
        o_ref[...] = jnp.zeros(o_ref.shape, o_ref.dtype)
        o_ref[PAD_ROWS - N_META:, :] = _rms(meta_ref[...], w_ref[...]).astype(o_ref.dtype)


def _norm_ext(x, meta, w):
    S, D = x.shape
    nb = S // BLOCK
    return pl.pallas_call(
        functools.partial(_norm_ext_kernel, nb=nb),
        grid=(nb + 1,),
        in_specs=[pl.BlockSpec((BLOCK, D), lambda i: (jnp.minimum(i, nb - 1), 0)),
                  pl.BlockSpec((N_META, D), lambda i: (0, 0)),
                  pl.BlockSpec((1, D), lambda i: (0, 0))],
        out_specs=pl.BlockSpec((BLOCK, D), lambda i: (i, 0)),
        out_shape=jax.ShapeDtypeStruct((S + PAD_ROWS, D), BF16),
        compiler_params=_params(("arbitrary",), _vmem_limit(BLOCK * D * 4, BLOCK * D * 2)),
        name="norm_ext",
    )(x, meta, w.reshape(1, D))


def _norm_kernel(x_ref, w_ref, o_ref):
    o_ref[...] = _rms(x_ref[...], w_ref[...]).astype(o_ref.dtype)


def _norm(x, w, out_dtype, tr=256):
    S, D = x.shape
    return pl.pallas_call(
        _norm_kernel,
        grid=(S // tr,),
        in_specs=[pl.BlockSpec((tr, D), lambda i: (i, 0)), pl.BlockSpec((1, D), lambda i: (0, 0))],
        out_specs=pl.BlockSpec((tr, D), lambda i: (i, 0)),
        out_shape=jax.ShapeDtypeStruct((S, D), out_dtype),
        compiler_params=_params(("arbitrary",), _vmem_limit(tr * D * 4, tr * D * 4)),
        name="norm",
    )(x, w.reshape(1, D))


def _dot(a, b, dims=NN):
    return lax.dot_general(a, b, dims, preferred_element_type=F32)


def _cast_resident(src_ref, dst_ref, col0=None, n_valid=None):
    K = src_ref.shape[0]
    rows = _pick(K, (256, 128))

    def body(c, carry):
        sl = pl.ds(pl.multiple_of(c * rows, rows), rows)
        blk = src_ref[sl, :]
        if n_valid is not None:
            col = col0 + lax.broadcasted_iota(jnp.int32, blk.shape, 1)
            blk = jnp.where(col < n_valid, blk, 0.0)
        dst_ref[sl, :] = blk.astype(dst_ref.dtype)
        return carry

    lax.fori_loop(0, K // rows, body, 0)


def _mm_in_kernel(a_ref, b_ref, o_ref, bw_ref, *, n_valid):
    j, i = pl.program_id(0), pl.program_id(1)

    @pl.when(i == 0)
    def _():
        _cast_resident(b_ref, bw_ref, j * b_ref.shape[1], n_valid)

    o_ref[...] = _dot(a_ref[...], bw_ref[...])


def _mm_in(a, w, tm, tn):
    M, K = a.shape
    N = w.shape[1]
    nj = pl.cdiv(N, tn)
    return pl.pallas_call(
        functools.partial(_mm_in_kernel, n_valid=N),
        grid=(nj, M // tm),
        in_specs=[pl.BlockSpec((tm, K), lambda j, i: (i, 0)), pl.BlockSpec((K, tn), lambda j, i: (0, j))],
        out_specs=pl.BlockSpec((tm, tn), lambda j, i: (i, j)),
        out_shape=jax.ShapeDtypeStruct((M, nj * tn), F32),
        scratch_shapes=[pltpu.VMEM((K, tn), BF16)],
        compiler_params=_params(("arbitrary", "arbitrary"),
                                _vmem_limit(tm * K * 2, K * tn * 4, tm * tn * 4, temps=K * tn * 2 + tm * tn * 4)),
        name="mm_in",
    )(a, w)


def _mm_mix_kernel(a1_ref, b1_ref, a2_ref, b2_ref, z0_ref, z1_ref, bg_ref, o_ref, w1_ref, w2_ref):
    @pl.when(pl.program_id(1) == 0)
    def _():
        _cast_resident(b1_ref, w1_ref)
        _cast_resident(b2_ref, w2_ref)

    g0 = jax.nn.sigmoid(z0_ref[...] + bg_ref[0:1, :])
    g1 = jax.nn.sigmoid(z1_ref[...] + bg_ref[1:2, :])
    o_ref[...] = (g0 * _dot(a1_ref[...], w1_ref[...]) + g1 * _dot(a2_ref[...], w2_ref[...])).astype(o_ref.dtype)


def _mm_mix(o_att, w_pa, o_rwkv, w_pr, z, b_gate, S, D, tm, tn):
    K1, K2 = o_att.shape[1], o_rwkv.shape[1]
    nj = D // tn
    return pl.pallas_call(
        _mm_mix_kernel,
        grid=(nj, S // tm),
        in_specs=[pl.BlockSpec((tm, K1), lambda j, i: (i, 0)), pl.BlockSpec((K1, tn), lambda j, i: (0, j)),
                  pl.BlockSpec((tm, K2), lambda j, i: (i, 0)), pl.BlockSpec((K2, tn), lambda j, i: (0, j)),
                  pl.BlockSpec((tm, tn), lambda j, i: (i, j)), pl.BlockSpec((tm, tn), lambda j, i: (i, nj + j)),
                  pl.BlockSpec((2, tn), lambda j, i: (0, j))],
        out_specs=pl.BlockSpec((tm, tn), lambda j, i: (i, j)),
        out_shape=jax.ShapeDtypeStruct((S, D), BF16),
        scratch_shapes=[pltpu.VMEM((K1, tn), BF16), pltpu.VMEM((K2, tn), BF16)],
        compiler_params=_params(("arbitrary", "arbitrary"),
                                _vmem_limit(tm * K1 * 2, K1 * tn * 4, tm * K2 * 2, K2 * tn * 4,
                                            2 * tm * tn * 4, tm * tn * 2,
                                            temps=(K1 + K2) * tn * 2 + 3 * tm * tn * 4)),
        name="mm_mix",
    )(o_att, w_pa, o_rwkv, w_pr, z, z, b_gate)


def _mm_resw_kernel(a_ref, b_ref, r_ref, o_ref, bw_ref):
    @pl.when(pl.program_id(1) == 0)
    def _():
        _cast_resident(b_ref, bw_ref)

    o_ref[...] = r_ref[...] + _dot(a_ref[...], bw_ref[...])


def _mm_resw(a, w, res, tm, tn):
    M, K = a.shape
    N = w.shape[1]
    return pl.pallas_call(
        _mm_resw_kernel,
        grid=(N // tn, M // tm),
        in_specs=[pl.BlockSpec((tm, K), lambda j, i: (i, 0)), pl.BlockSpec((K, tn), lambda j, i: (0, j)),
                  pl.BlockSpec((tm, tn), lambda j, i: (i, j))],
        out_specs=pl.BlockSpec((tm, tn), lambda j, i: (i, j)),
        out_shape=jax.ShapeDtypeStruct((M, N), F32),
        scratch_shapes=[pltpu.VMEM((K, tn), BF16)],
        compiler_params=_params(("arbitrary", "arbitrary"),
                                _vmem_limit(tm * K * 2, K * tn * 4, 2 * tm * tn * 4, temps=K * tn * 2 + tm * tn * 4)),
        name="mm_out",
    )(a, w, res)


def _mm_res_kernel(a_ref, b_ref, r_ref, o_ref, *, nk):
    k = pl.program_id(2)
    part = _dot(a_ref[...], b_ref[...])
    if nk == 1:
        o_ref[...] = r_ref[...] + part
    else:
        @pl.when(k == 0)
        def _():
            o_ref[...] = r_ref[...] + part

        @pl.when(k > 0)
        def _():
            o_ref[...] += part


def _mm_res(a, b, res, tm, tn, tk):
    M, K = a.shape
    N = b.shape[1]
    nk = K // tk
    return pl.pallas_call(
        functools.partial(_mm_res_kernel, nk=nk),
        grid=(M // tm, N // tn, nk),
        in_specs=[pl.BlockSpec((tm, tk), lambda i, j, k: (i, k)), pl.BlockSpec((tk, tn), lambda i, j, k: (k, j)),
                  pl.BlockSpec((tm, tn), lambda i, j, k: (i, j))],
        out_specs=pl.BlockSpec((tm, tn), lambda i, j, k: (i, j)),
        out_shape=jax.ShapeDtypeStruct((M, N), F32),
        compiler_params=_params(("arbitrary", "arbitrary", "arbitrary"),
                                _vmem_limit(tm * tk * 2, tk * tn * 2, 2 * tm * tn * 4, temps=tm * tn * 4)),
        name="mm_down",
    )(a, b, res)


def _mm_glu_kernel(a_ref, bg_ref, bu_ref, o_ref, wg_ref, wu_ref):
    @pl.when(pl.program_id(1) == 0)
    def _():
        _cast_resident(bg_ref, wg_ref)
        _cast_resident(bu_ref, wu_ref)

    a = a_ref[...]
    gate = _dot(a, wg_ref[...])
    up = _dot(a, wu_ref[...])
    o_ref[...] = (gate * jax.nn.sigmoid(gate) * up).astype(o_ref.dtype)


def _mm_glu(a, wg, wu, tm, tn):
    M, K = a.shape
    N = wg.shape[1]
    return pl.pallas_call(
        _mm_glu_kernel,
        grid=(N // tn, M // tm),
        in_specs=[pl.BlockSpec((tm, K), lambda j, i: (i, 0)), pl.BlockSpec((K, tn), lambda j, i: (0, j)),
                  pl.BlockSpec((K, tn), lambda j, i: (0, j))],
        out_specs=pl.BlockSpec((tm, tn), lambda j, i: (i, j)),
        out_shape=jax.ShapeDtypeStruct((M, N), BF16),
        scratch_shapes=[pltpu.VMEM((K, tn), BF16), pltpu.VMEM((K, tn), BF16)],
        compiler_params=_params(("arbitrary", "arbitrary"),
                                _vmem_limit(tm * K * 2, 2 * K * tn * 4, tm * tn * 2,
                                            temps=2 * K * tn * 2 + 3 * tm * tn * 4)),
        name="mm_glu",
    )(a, wg, wu)


def _rope_kernel(z_ref, cos_ref, sin_ref, o_ref, *, n_rope_blocks, heads_per_block, k_heads_last):
    j = pl.program_id(1)
    cos = cos_ref[...]
    sin = sin_ref[...]
    lane = lax.broadcasted_iota(jnp.int32, cos.shape, 1)
    half = ROPE_DIMS // 2
    for h in range(heads_per_block):
        t = z_ref[:, h * HEAD_DIM:(h + 1) * HEAD_DIM]
        up = jnp.concatenate([t[:, half:], t[:, :half]], axis=1)
        down = jnp.concatenate([t[:, HEAD_DIM - half:], t[:, :HEAD_DIM - half]], axis=1)
        partner = jnp.where(lane < half, up, down)
        rot = t * cos + partner * sin
        if h >= k_heads_last:
            rot = jnp.where(j == n_rope_blocks - 1, t, rot)
        o_ref[:, h * HEAD_DIM:(h + 1) * HEAD_DIM] = rot.astype(o_ref.dtype)


def _rope_cast(z, cos_t, sin_t, off_q, width, kv_width):
    Lp = z.shape[0]
    tc = 2 * kv_width
    assert off_q % tc == 0 and width % tc == 0
    nblk = width // tc
    return pl.pallas_call(
        functools.partial(_rope_kernel, n_rope_blocks=nblk, heads_per_block=tc // HEAD_DIM,
                          k_heads_last=kv_width // HEAD_DIM),
        grid=(Lp // BLOCK, nblk),
        in_specs=[pl.BlockSpec((BLOCK, tc), lambda i, j: (i, off_q // tc + j)),
                  pl.BlockSpec((BLOCK, HEAD_DIM), lambda i, j: (i, 0)),
                  pl.BlockSpec((BLOCK, HEAD_DIM), lambda i, j: (i, 0))],
        out_specs=pl.BlockSpec((BLOCK, tc), lambda i, j: (i, j)),
        out_shape=jax.ShapeDtypeStruct((Lp, width), BF16),
        compiler_params=_params(("arbitrary", "arbitrary"), _vmem_limit(BLOCK * tc * 4, BLOCK * tc * 2)),
        name="rope_cast",
    )(z, cos_t, sin_t)


NEG_BIG = -1e30


def _attn_kernel(q_ref, kp_ref, kc_ref, kn_ref, km_ref, vp_ref, vc_ref, vn_ref, vm_ref, sink_ref, o_ref,
                 *, nb, group):
    i = pl.program_id(1)
    scale = HEAD_DIM ** -0.5
    zpad = jnp.zeros((BLOCK - N_META, HEAD_DIM), BF16)
    k_all = jnp.concatenate([kp_ref[...], kc_ref[...], kn_ref[...], km_ref[...], zpad], axis=0)
    v_all = jnp.concatenate([vp_ref[...], vc_ref[...], vn_ref[...], vm_ref[...], zpad], axis=0)
    q_all = jnp.concatenate([q_ref[:, g * HEAD_DIM:(g + 1) * HEAD_DIM] for g in range(group)], axis=0)
    s = _dot(q_all, k_all, NT) * scale
    rows = lax.broadcasted_iota(jnp.int32, s.shape, 0) % BLOCK
    cols = lax.broadcasted_iota(jnp.int32, s.shape, 1)
    rel = cols - BLOCK - rows
    kblk = i - 1 + cols // BLOCK
    band_ok = (cols < 3 * BLOCK) & (kblk >= 0) & (kblk < nb) & (jnp.abs(rel) <= WINDOW)
    meta_ok = (cols >= 3 * BLOCK) & (cols < 3 * BLOCK + N_META)
    s = jnp.where(band_ok | meta_ok, s, NEG_BIG)
    sink = sink_ref[:, 0:1]
    m = jnp.maximum(jnp.max(s, axis=-1, keepdims=True), sink)
    p = jnp.exp(s - m)
    denom = jnp.sum(p, axis=-1, keepdims=True) + jnp.exp(sink - m)
    o = _dot(p.astype(BF16), v_all) / denom
    for g in range(group):
        o_ref[:, g * HEAD_DIM:(g + 1) * HEAD_DIM] = o[g * BLOCK:(g + 1) * BLOCK, :].astype(o_ref.dtype)


def _attention(qkv, sink, S, n_heads, n_kv):
    group = n_heads // n_kv
    nb = S // BLOCK
    qw = group * HEAD_DIM
    kcol = n_heads
    vcol = n_heads + n_kv
    meta_blk = (S + PAD_ROWS - N_META) // N_META
    sink_rows = jnp.broadcast_to(sink.astype(F32).reshape(n_kv, group, 1, 1),
                                 (n_kv, group, BLOCK, LANES)).reshape(n_kv, group * BLOCK, LANES)

    def kv_specs(col0):
        return [pl.BlockSpec((BLOCK, HEAD_DIM), lambda h, i: (jnp.maximum(i - 1, 0), col0 + h)),
                pl.BlockSpec((BLOCK, HEAD_DIM), lambda h, i: (i, col0 + h)),
                pl.BlockSpec((BLOCK, HEAD_DIM), lambda h, i: (jnp.minimum(i + 1, nb - 1), col0 + h)),
                pl.BlockSpec((N_META, HEAD_DIM), lambda h, i: (meta_blk, col0 + h))]

    return pl.pallas_call(
        functools.partial(_attn_kernel, nb=nb, group=group),
        grid=(n_kv, nb),
        in_specs=[pl.BlockSpec((BLOCK, qw), lambda h, i: (i, h))] + kv_specs(kcol) + kv_specs(vcol)
                 + [pl.BlockSpec((None, group * BLOCK, LANES), lambda h, i: (h, 0, 0))],
        out_specs=pl.BlockSpec((BLOCK, qw), lambda h, i: (i, h)),
        out_shape=jax.ShapeDtypeStruct((S, n_heads * HEAD_DIM), BF16),
        compiler_params=_params(("arbitrary", "arbitrary"),
                                _vmem_limit(BLOCK * qw * 2 * 2, 8 * BLOCK * HEAD_DIM * 2,
                                            group * BLOCK * LANES * 4, temps=6 * group * BLOCK * 512 * 4)),
        name="attention",
    )(qkv, qkv, qkv, qkv, qkv, qkv, qkv, qkv, qkv, sink_rows)


def _rope(t, cos, sin):
    half = ROPE_DIMS // 2
    lane = lax.broadcasted_iota(jnp.int32, t.shape, 1)
    up = jnp.concatenate([t[:, half:], t[:, :half]], axis=1)
    down = jnp.concatenate([t[:, HEAD_DIM - half:], t[:, :HEAD_DIM - half]], axis=1)
    return t * cos + jnp.where(lane < half, up, down) * sin


def _attn_res_kernel(sink_ref, q_ref, k_ref, v_ref, cos_ref, sin_ref, o_ref, kb_ref, vb_ref,
                     *, nb, group, qblocks):
    h, i = pl.program_id(0), pl.program_id(1)
    lp = k_ref.shape[0]

    @pl.when(i == 0)
    def _():
        def body(c, carry):
            sl = pl.ds(pl.multiple_of(c * BLOCK, BLOCK), BLOCK)
            kb_ref[sl, :] = _rope(k_ref[sl, :], cos_ref[sl, :], sin_ref[sl, :]).astype(BF16)
            vb_ref[sl, :] = v_ref[sl, :].astype(BF16)
            return carry
        lax.fori_loop(0, lp // BLOCK, body, 0)

    zpad = jnp.zeros((BLOCK - N_META, HEAD_DIM), BF16)
    k_meta = jnp.concatenate([kb_ref[lp - N_META:lp, :], zpad], axis=0)
    v_meta = jnp.concatenate([vb_ref[lp - N_META:lp, :], zpad], axis=0)
    rows = lax.broadcasted_iota(jnp.int32, (BLOCK, 4 * BLOCK), 0)
    cols = lax.broadcasted_iota(jnp.int32, (BLOCK, 4 * BLOCK), 1)
    is_meta = (cols >= 3 * BLOCK) & (cols < 3 * BLOCK + N_META)
    scale = HEAD_DIM ** -0.5
    k_all, v_all, bias, q, sink = [], [], [], [], []
    for b in range(qblocks):
        qi = i * qblocks + b
        kb0 = jnp.clip(qi - 1, 0, nb - 3)
        win = pl.ds(pl.multiple_of(kb0 * BLOCK, BLOCK), 3 * BLOCK)
        k_b = jnp.concatenate([kb_ref[win, :], k_meta], axis=0)
        v_b = jnp.concatenate([vb_ref[win, :], v_meta], axis=0)
        rel = cols - rows + (kb0 - qi) * BLOCK
        ok = ((cols < 3 * BLOCK) & (jnp.abs(rel) <= WINDOW)) | is_meta
        bias_b = jnp.where(ok, 0.0, NEG_BIG).astype(F32)
        qrows = pl.ds(pl.multiple_of(qi * BLOCK, BLOCK), BLOCK)
        cos_q, sin_q = cos_ref[qrows, :], sin_ref[qrows, :]
        for g in range(group):
            k_all.append(k_b)
            v_all.append(v_b)
            bias.append(bias_b)
            q.append(_rope(q_ref[b * BLOCK:(b + 1) * BLOCK, g * HEAD_DIM:(g + 1) * HEAD_DIM],
                           cos_q, sin_q).astype(BF16))
            sink.append(sink_ref[h * group + g])
    n = range(len(q))
    s = [_dot(q[c], k_all[c], NT) * scale + bias[c] for c in n]
    m = [jnp.maximum(jnp.max(s[c], axis=-1, keepdims=True), sink[c]) for c in n]
    p = [jnp.exp(s[c] - m[c]) for c in n]
    denom = [jnp.sum(p[c], axis=-1, keepdims=True) + jnp.exp(sink[c] - m[c]) for c in n]
    o = [_dot(p[c].astype(BF16), v_all[c]) / denom[c] for c in n]
    for c in n:
        b, g = divmod(c, group)
        o_ref[b * BLOCK:(b + 1) * BLOCK, g * HEAD_DIM:(g + 1) * HEAD_DIM] = o[c].astype(o_ref.dtype)


def _attention_res(z, cos_t, sin_t, sink, S, n_heads, n_kv, off_q):
    Lp = z.shape[0]
    group = n_heads // n_kv
    nb = S // BLOCK
    assert nb >= 3
    qw = group * HEAD_DIM
    kcol = off_q // HEAD_DIM + n_heads
    vcol = kcol + n_kv
    col_bytes = Lp * HEAD_DIM * 4
    qblocks = _pick(nb, (2, 1))
    tq = qblocks * BLOCK
    return pl.pallas_call(
        functools.partial(_attn_res_kernel, nb=nb, group=group, qblocks=qblocks),
        grid_spec=pltpu.PrefetchScalarGridSpec(
            num_scalar_prefetch=1,
            grid=(n_kv, nb // qblocks),
            in_specs=[pl.BlockSpec((tq, qw), lambda h, i, s: (i, off_q // qw + h)),
                      pl.BlockSpec((Lp, HEAD_DIM), lambda h, i, s: (0, kcol + h)),
                      pl.BlockSpec((Lp, HEAD_DIM), lambda h, i, s: (0, vcol + h)),
                      pl.BlockSpec((Lp, HEAD_DIM), lambda h, i, s: (0, 0)),
                      pl.BlockSpec((Lp, HEAD_DIM), lambda h, i, s: (0, 0))],
            out_specs=pl.BlockSpec((tq, qw), lambda h, i, s: (i, h)),
            scratch_shapes=[pltpu.VMEM((Lp, HEAD_DIM), BF16), pltpu.VMEM((Lp, HEAD_DIM), BF16)]),
        out_shape=jax.ShapeDtypeStruct((S, n_heads * HEAD_DIM), BF16),
        compiler_params=_params(("arbitrary", "arbitrary"),
                                _vmem_limit(tq * qw * 4, 4 * col_bytes, tq * qw * 2,
                                            temps=col_bytes + 8 * group * tq * 512 * 4)),
        name="attention",
    )(sink.astype(F32), z, z, z, cos_t, sin_t)


def _split3(x):
    hi = x.astype(BF16)
    r1 = x - hi.astype(F32)
    mid = r1.astype(BF16)
    lo = (r1 - mid.astype(F32)).astype(BF16)
    return hi, mid, lo


def _split2(x):
    hi = x.astype(BF16)
    return hi, (x - hi.astype(F32)).astype(BF16)


def _dot_exact_rhs(x, m_bf16, dims=NN, left=False):
    parts = _split3(x)
    if left:
        return sum(_dot(m_bf16, p, dims) for p in parts)
    return sum(_dot(p, m_bf16, dims) for p in parts)


def _dotp(a, b, passes, dims=NN):
    if passes == 1:
        return _dot(a.astype(BF16), b.astype(BF16), dims)
    a_hi, a_lo = _split2(a)
    b_hi, b_lo = _split2(b)
    return _dot(a_hi, b_hi, dims) + (_dot(a_hi, b_lo, dims) + _dot(a_lo, b_hi, dims))


def _head_sum(x, ones_bd):
    outs = []
    for s in range(x.shape[1] // LANES):
        outs.append(_dot_exact_rhs(x[:, s * LANES:(s + 1) * LANES], ones_bd))
    return outs[0] if len(outs) == 1 else jnp.concatenate(outs, axis=1)


def _prep_kernel(r_ref, rp_ref, rn_ref, k_ref, kp_ref, kn_ref, v_ref, vp_ref, vn_ref,
                 lo_ref, lop_ref, lon_ref, gd_ref, gdp_ref, gdn_ref,
                 pc_ref, pl_ref, wl_ref, g2_ref, ones_ref,
                 ro_ref, vo_ref, kko_ref, lwf_ref, lwb_ref, kdf_ref, kdb_ref, bdf_ref, bdb_ref, go_ref, bo_ref,
                 buf_ref, *, n_tiles, tr, lora_tanh_cols):
    i = pl.program_id(1)

    def shifted(main_ref, prev_ref, next_ref, mup, mun):
        w = main_ref.shape[1]
        buf_ref[0:8, 0:w] = prev_ref[...]
        buf_ref[8:8 + tr, 0:w] = main_ref[...]
        buf_ref[8 + tr:16 + tr, 0:w] = next_ref[...]
        x = main_ref[...]
        prev = buf_ref[7:7 + tr, 0:w]
        nxt = buf_ref[9:9 + tr, 0:w]
        return x + mup * (prev - x) + mun * (nxt - x)

    r = shifted(r_ref, rp_ref, rn_ref, pc_ref[0:1, :], pc_ref[1:2, :])
    k = shifted(k_ref, kp_ref, kn_ref, pc_ref[2:3, :], pc_ref[3:4, :])
    v = shifted(v_ref, vp_ref, vn_ref, pc_ref[4:5, :], pc_ref[5:6, :])
    lo = shifted(lo_ref, lop_ref, lon_ref, pl_ref[0:1, 0:LORA_PAD], pl_ref[1:2, 0:LORA_PAD])
    gd = shifted(gd_ref, gdp_ref, gdn_ref, pl_ref[0:1, LORA_PAD:], pl_ref[1:2, LORA_PAD:])

    row = lax.broadcasted_iota(jnp.int32, (tr, 1), 0)
    valid = jnp.where((i < n_tiles - 1) | (row >= tr - N_META), 1.0, 0.0).astype(F32)

    lane = lax.broadcasted_iota(jnp.int32, lo.shape, 1)
    lo_act = jnp.where(lane < lora_tanh_cols, jnp.tanh(lo), lo).astype(BF16)
    dec_f = _dot(lo_act, wl_ref[0])
    dec_b = _dot(lo_act, wl_ref[1])
    apre_f = _dot(lo_act, wl_ref[2])
    apre_b = _dot(lo_act, wl_ref[3])
    g = _dot(jax.nn.sigmoid(lo).astype(BF16), g2_ref[0]) + _dot(jax.nn.sigmoid(gd).astype(BF16), g2_ref[1])

    def log_decay(dec, w0):
        return -math.exp(-0.5) * jax.nn.sigmoid(w0 + dec)

    a_f = jax.nn.sigmoid(pc_ref[8:9, :] + apre_f)
    a_b = jax.nn.sigmoid(pc_ref[9:10, :] + apre_b)
    kk = k * pc_ref[10:11, :]
    ss = _head_sum(kk * kk, ones_ref[...])
    kk = kk / jnp.maximum(jnp.sqrt(ss), 1e-12) * valid
    k_a = pc_ref[11:12, :]
    kv = k * valid

    rv = r * valid
    vv = v * valid
    kd_f = kv * (1.0 + (a_f - 1.0) * k_a)
    kd_b = kv * (1.0 + (a_b - 1.0) * k_a)
    ro_ref[...] = rv
    vo_ref[...] = vv
    kko_ref[...] = kk
    lwf_ref[...] = log_decay(dec_f, pc_ref[6:7, :])
    lwb_ref[...] = log_decay(dec_b, pc_ref[7:8, :])
    kdf_ref[...] = kd_f
    kdb_ref[...] = kd_b
    bdf_ref[...] = kk * a_f
    bdb_ref[...] = kk * a_b
    go_ref[...] = g
    bo_ref[...] = _head_sum(rv * (kd_f + kd_b) * pc_ref[12:13, :], ones_ref[...]) * vv


def _rwkv_prep(z, off_r, C, pc, plo, wl, g2, ones_bd, lora_tanh_cols):
    Lp = z.shape[0]
    tr = BLOCK
    ct = _pick(C, (1024, 512))
    n_tiles = Lp // tr
    n8 = Lp // 8
    off_lo = off_r + 3 * C
    off_gd = off_lo + LORA_PAD
    assert off_r % ct == 0 and off_lo % LORA_PAD == 0

    def seg_specs(off, w, with_c):
        cb = off // w

        def col(c):
            return cb + c if with_c else cb
        return [pl.BlockSpec((tr, w), lambda c, i: (i, col(c))),
                pl.BlockSpec((8, w), lambda c, i: ((i * (tr // 8) + n8 - 1) % n8, col(c))),
                pl.BlockSpec((8, w), lambda c, i: (((i + 1) * (tr // 8)) % n8, col(c)))]

    in_specs = (seg_specs(off_r, ct, True) + seg_specs(off_r + C, ct, True) + seg_specs(off_r + 2 * C, ct, True)
                + seg_specs(off_lo, LORA_PAD, False) + seg_specs(off_gd, LORA_PAD, False)
                + [pl.BlockSpec((16, ct), lambda c, i: (0, c)),
                   pl.BlockSpec((8, 2 * LORA_PAD), lambda c, i: (0, 0)),
                   pl.BlockSpec((4, LORA_PAD, ct), lambda c, i: (0, 0, c)),
                   pl.BlockSpec((2, LORA_PAD, ct), lambda c, i: (0, 0, c)),
                   pl.BlockSpec((LANES, LANES), lambda c, i: (0, 0))])
    out_spec = pl.BlockSpec((tr, ct), lambda c, i: (i, c))
    out_sds = jax.ShapeDtypeStruct((Lp, C), F32)
    return pl.pallas_call(
        functools.partial(_prep_kernel, n_tiles=n_tiles, tr=tr, lora_tanh_cols=lora_tanh_cols),
        grid=(C // ct, n_tiles),
        in_specs=in_specs,
        out_specs=[out_spec] * 11,
        out_shape=[out_sds] * 11,
        scratch_shapes=[pltpu.VMEM((tr + 16, max(ct, LORA_PAD)), F32)],
        compiler_params=_params(("arbitrary", "arbitrary"),
                                _vmem_limit(3 * tr * ct * 4, 2 * tr * LORA_PAD * 4, 4 * LORA_PAD * ct * 2,
                                            2 * LORA_PAD * ct * 2, 11 * tr * ct * 4, temps=24 * tr * ct * 4)),
        name="rwkv_prep",
    )(*([z] * 15), pc, plo, wl, g2, ones_bd)


SCAN_PAIRS = 8


def _scan_consts():
    C = CHUNK
    row = lax.broadcasted_iota(jnp.int32, (C, 2 * C), 0)
    col = lax.broadcasted_iota(jnp.int32, (C, 2 * C), 1) % C
    rr = lax.broadcasted_iota(jnp.int32, (LANES, LANES), 0)
    cc = lax.broadcasted_iota(jnp.int32, (LANES, LANES), 1)
    return dict(
        row=lax.broadcasted_iota(jnp.int32, (C, LANES), 0),
        head0=lax.broadcasted_iota(jnp.int32, (C, LANES), 1) < RWKV_HEAD,
        strict_f=col < row, strict_b=col > row, incl_f=col <= row, incl_b=col >= row,
        eye_sbs=jnp.where(col == row, 1.0, 0.0).astype(F32),
        same_head=(rr // RWKV_HEAD) == (cc // RWKV_HEAD), eye=rr == cc)


def _scan_chains(chains, cst):
    C = CHUNK
    head0, same, eye = cst["head0"], cst["same_head"], cst["eye"]
    n = range(len(chains))
    rev = [c[7] for c in chains]
    r, v, al, lw, k, be, h = ([c[i] for c in chains] for i in range(7))
    strict = [cst["strict_b" if x else "strict_f"] for x in rev]
    incl = [cst["incl_b" if x else "incl_f"] for x in rev]

    def bd(x):
        xb = x.astype(BF16)
        zero = jnp.zeros_like(xb)
        return jnp.concatenate([jnp.where(head0, xb, zero), jnp.where(head0, zero, xb)], axis=0)

    def mm(a, b_bf16, dims=NN):
        return _dot(a.astype(BF16), b_bf16, dims)

    row = cst["row"]
    cl = list(lw)
    sh = 1
    while sh < C:
        cl = [cl[i] + (jnp.where(row < C - sh, pltpu.roll(cl[i], C - sh, 0), 0.0) if rev[i] else
                       jnp.where(row >= sh, pltpu.roll(cl[i], sh, 0), 0.0)) for i in n]
        sh *= 2
    total = [cl[i][0:1, :] if rev[i] else cl[i][C - 1:C, :] for i in n]
    a_t = [al[i] * jnp.exp(cl[i] - lw[i]) for i in n]
    r_t = [r[i] * jnp.exp(cl[i]) for i in n]
    w_inv = [jnp.exp(-cl[i]) for i in n]
    w_rest = [jnp.exp(total[i] - cl[i]) for i in n]
    bd_v = [bd(v[i]) for i in n]
    sc = [mm(jnp.concatenate([a_t[i], r_t[i]], axis=0),
             jnp.concatenate([bd(k[i] * w_inv[i]), bd(be[i] * w_inv[i])], axis=0), NT) for i in n]
    a_ak = [jnp.where(strict[i], sc[i][0:C, 0:2 * C], 0.0) for i in n]
    n_ab = [jnp.where(strict[i], sc[i][0:C, 2 * C:], 0.0) for i in n]
    a_rk = [jnp.where(incl[i], sc[i][C:, 0:2 * C], 0.0) for i in n]
    a_rb = [jnp.where(incl[i], sc[i][C:, 2 * C:], 0.0) for i in n]

    t_inv = [cst["eye_sbs"] - n_ab[i] for i in n]
    sq = [mm(n_ab[i], bd(n_ab[i])) for i in n]
    n_round = 2
    while (1 << n_round) < C:
        both = [mm(jnp.concatenate([t_inv[i], sq[i]], axis=0), bd(sq[i])) for i in n]
        t_inv = [t_inv[i] + both[i][0:C] for i in n]
        sq = [both[i][C:] for i in n]
        n_round += 1
    t_inv = [t_inv[i] + mm(t_inv[i], bd(sq[i])) for i in n]

    akv = [mm(jnp.concatenate([a_ak[i], a_rk[i]], axis=0), bd_v[i]) for i in n]
    tp = [mm(t_inv[i], jnp.concatenate([bd(akv[i][0:C]), bd(a_t[i])], axis=1)) for i in n]
    p0 = [tp[i][:, 0:LANES] for i in n]
    at = [tp[i][:, LANES:] for i in n]
    rb = [mm(a_rb[i], jnp.concatenate([bd(p0[i]), bd(at[i])], axis=1)) for i in n]
    y0 = [akv[i][C:] - rb[i][:, 0:LANES] for i in n]
    rh = [r_t[i] - rb[i][:, LANES:] for i in n]
    bt = [mm(be[i] * w_rest[i], jnp.concatenate([at[i], p0[i]], axis=1).astype(BF16), TN) for i in n]
    kv = [mm(k[i] * w_rest[i], v[i].astype(BF16), TN) for i in n]
    m_mat = [jnp.where(same, jnp.where(eye, jnp.broadcast_to(jnp.exp(total[i]), (LANES, LANES)), 0.0)
                       - bt[i][:, 0:LANES], 0.0) for i in n]
    g_mat = [jnp.where(same, kv[i] - bt[i][:, LANES:], 0.0) for i in n]
    yh = [mm(jnp.concatenate([rh[i], m_mat[i]], axis=0), h[i].astype(BF16)) for i in n]
    return [(yh[i][0:C] + y0[i], yh[i][C:] + g_mat[i]) for i in n]


def _scan_kernel(rf_ref, vf_ref, af_ref, lwf_ref, kf_ref, bf_ref,
                 rb_ref, vb_ref, ab_ref, lwb_ref, kb_ref, bb_ref,
                 yf_ref, yb_ref, hf_ref, hb_ref, *, pairs):
    @pl.when(pl.program_id(1) == 0)
    def _():
        hf_ref[...] = jnp.zeros(hf_ref.shape, F32)
        hb_ref[...] = jnp.zeros(hb_ref.shape, F32)

    cst = _scan_consts()
    chains = []
    for g in range(pairs):
        sl = slice(g * LANES, (g + 1) * LANES)
        chains.append((rf_ref[:, sl], vf_ref[:, sl], af_ref[:, sl], lwf_ref[:, sl], kf_ref[:, sl], bf_ref[:, sl],
                       hf_ref[g], False))
        chains.append((rb_ref[:, sl], vb_ref[:, sl], ab_ref[:, sl], lwb_ref[:, sl], kb_ref[:, sl], bb_ref[:, sl],
                       hb_ref[g], True))
    res = _scan_chains(chains, cst)
    for g in range(pairs):
        sl = slice(g * LANES, (g + 1) * LANES)
        yf_ref[:, sl], hf_ref[g] = res[2 * g]
        yb_ref[:, sl], hb_ref[g] = res[2 * g + 1]


def _rwkv_scan(r, v, kk, lw_f, lw_b, kd_f, kd_b, bd_f, bd_b, S):
    Lp, C = r.shape
    n_real = S // CHUNK
    n_chunks = Lp // CHUNK
    steps = n_real + 1
    pairs = _pick(C // LANES, (SCAN_PAIRS, 2, 1))
    w = pairs * LANES

    def fwd(p, s):
        return ((s + n_real + 1) % n_chunks, p)

    def bwd(p, s):
        return (n_real - s, p)

    spec_f = pl.BlockSpec((CHUNK, w), fwd)
    spec_b = pl.BlockSpec((CHUNK, w), bwd)
    out_sds = jax.ShapeDtypeStruct((S + CHUNK, C), F32)
    out_f = pl.BlockSpec((CHUNK, w), lambda p, s: ((s + n_real) % (n_real + 1), p))
    return pl.pallas_call(
        functools.partial(_scan_kernel, pairs=pairs),
        grid=(C // w, steps),
        in_specs=[spec_f] * 6 + [spec_b] * 6,
        out_specs=[out_f, spec_b],
        out_shape=[out_sds, out_sds],
        scratch_shapes=[pltpu.VMEM((pairs, LANES, LANES), F32), pltpu.VMEM((pairs, LANES, LANES), F32)],
        compiler_params=_params(("arbitrary", "arbitrary"), 32 << 20),
        name="rwkv_scan",
    )(r, v, kk, lw_f, kd_f, bd_f, r, v, kk, lw_b, kd_b, bd_b)


def _post_kernel(yf_ref, yb_ref, bonus_ref, g_ref, pq_ref, ones_ref, o_ref):
    ones_bd = ones_ref[...]
    inv_n = 1.0 / RWKV_HEAD
    y = yf_ref[...] + yb_ref[...]
    mean = _head_sum(y, ones_bd) * inv_n
    yc = y - mean
    var = _head_sum(yc * yc, ones_bd) * inv_n
    yn = yc * lax.rsqrt(var + GN_EPS) * pq_ref[0:1, :] + pq_ref[1:2, :]
    o_ref[...] = ((yn + bonus_ref[...]) * g_ref[...]).astype(o_ref.dtype)


def _rwkv_post(y_f, y_b, bonus, g, pq, ones_bd, S):
    C = g.shape[1]
    tr = 256
    ct = _pick(C, (1024, 512))
    spec = pl.BlockSpec((tr, ct), lambda i, c: (i, c))
    return pl.pallas_call(
        _post_kernel,
        grid=(S // tr, C // ct),
        in_specs=[spec] * 4 + [pl.BlockSpec((8, ct), lambda i, c: (0, c)),
                               pl.BlockSpec((LANES, LANES), lambda i, c: (0, 0))],
        out_specs=spec,
        out_shape=jax.ShapeDtypeStruct((S, C), BF16),
        compiler_params=_params(("arbitrary", "arbitrary"),
                                _vmem_limit(4 * tr * ct * 4, tr * ct * 2, temps=16 * tr * ct * 4)),
        name="rwkv_post",
    )(y_f, y_b, bonus, g, pq, ones_bd)


def _pad_cols(w, width):
    return jnp.pad(w, ((0, 0), (0, width - w.shape[1])))


def kernel(x, meta_tokens, norm_mix_w, w_in, b_gate, mu_prev, mu_next, dec_w0, dec_w2, iclr_a0, iclr_a2,
           gate_w2, k_k, k_a, r_k, ln_x_w, ln_x_b, attn_sink, w_proj_attn, w_proj_rwkv, w_out, norm_ffn_w,
           w_ffn_gate, w_ffn_up, w_ffn_down, norm_final_w):
    B, S, D = x.shape
    assert B == 1 and norm_mix_w.shape[0] == 1, "one sequence, one layer"
    C = k_k.shape[-1]
    AW = w_proj_attn.shape[1]
    n_heads = attn_sink.shape[-1]
    DL, IL, GL = dec_w2.shape[2], iclr_a2.shape[2], gate_w2.shape[1]
    shift_w = mu_prev.shape[-1]
    KVW = (w_in.shape[-1] - 2 * D - AW - shift_w) // 2
    n_kv = KVW // HEAD_DIM
    F = w_ffn_gate.shape[-1]
    lora_w = 2 * DL + 2 * IL
    assert shift_w == 3 * C + lora_w + GL and lora_w <= LORA_PAD and GL <= LORA_PAD
    assert S % BLOCK == 0 and C % LANES == 0 and n_heads * HEAD_DIM == AW
    Lp = S + PAD_ROWS

    off_q = 2 * D
    off_r = off_q + AW + 2 * KVW
    off_lo = off_r + 3 * C
    tn_in = 512
    n_in_pad = -(-w_in.shape[-1] // tn_in) * tn_in
    assert n_in_pad >= off_lo + 2 * LORA_PAD and off_lo % LORA_PAD == 0
    low_w = lora_w + GL
    gl_a = LORA_PAD - lora_w

    mu_p, mu_n = mu_prev[0], mu_next[0]
    zc = jnp.zeros((C,), F32)
    pc = jnp.stack([mu_p[:C], mu_n[:C], mu_p[C:2 * C], mu_n[C:2 * C], mu_p[2 * C:3 * C], mu_n[2 * C:3 * C],
                    dec_w0[0, 0], dec_w0[0, 1], iclr_a0[0, 0], iclr_a0[0, 1], k_k[0], k_a[0], r_k[0], zc, zc, zc])
    plo = jnp.zeros((8, 2 * LORA_PAD), F32).at[0, :low_w].set(mu_p[3 * C:]).at[1, :low_w].set(mu_n[3 * C:])
    wl = jnp.zeros((4, LORA_PAD, C), F32)
    wl = wl.at[0, 0:DL].set(dec_w2[0, 0]).at[1, DL:2 * DL].set(dec_w2[0, 1])
    wl = wl.at[2, 2 * DL:2 * DL + IL].set(iclr_a2[0, 0]).at[3, 2 * DL + IL:lora_w].set(iclr_a2[0, 1])
    wl = wl.astype(BF16)
    g2 = jnp.zeros((2, LORA_PAD, C), F32).at[0, lora_w:].set(gate_w2[0, :gl_a]).at[1, :GL - gl_a].set(
        gate_w2[0, gl_a:]).astype(BF16)
    pq = jnp.zeros((8, C), F32).at[0].set(ln_x_w[0]).at[1].set(ln_x_b[0])
    lane = jnp.arange(LANES)
    ones_bd = (lane[:, None] // RWKV_HEAD == lane[None, :] // RWKV_HEAD).astype(BF16)

    w_d = w_ffn_down[0].astype(BF16)

    pos = jnp.concatenate([jnp.arange(N_META, N_META + S), jnp.zeros((PAD_ROWS - N_META,), jnp.int32),
                           jnp.arange(N_META)]).astype(F32)
    inv = ROPE_THETA ** (-jnp.arange(0, ROPE_DIMS, 2, dtype=F32) / ROPE_DIMS)
    ang = pos[:, None] * inv[None, :]
    ones_rest = jnp.ones((Lp, HEAD_DIM - ROPE_DIMS), F32)
    cos_t = jnp.concatenate([jnp.cos(ang), jnp.cos(ang), ones_rest], axis=1)
    sin_t = jnp.concatenate([-jnp.sin(ang), jnp.sin(ang), 0.0 * ones_rest], axis=1)

    x2 = x[0]
    h_ext = _norm_ext(x2, meta_tokens.astype(x.dtype), norm_mix_w[0])
    z = _mm_in(h_ext, w_in[0], _pick(Lp, (1040, 640)), tn_in)
    o_att = _attention_res(z, cos_t, sin_t, attn_sink[0], S, n_heads, n_kv, off_q)
    r_s, v_s, kk, lw_f, lw_b, kd_f, kd_b, bd_f, bd_b, g, bonus = _rwkv_prep(
        z, off_r, C, pc, plo, wl, g2, ones_bd, 2 * DL)
    y_f, y_b = _rwkv_scan(r_s, v_s, kk, lw_f, lw_b, kd_f, kd_b, bd_f, bd_b, S)
    o_rwkv = _rwkv_post(y_f, y_b, bonus, g, pq, ones_bd, S)
    tm = _pick(S, (1024, 512))
    mixed = _mm_mix(o_att, w_proj_attn[0], o_rwkv, w_proj_rwkv[0], z, b_gate[0], S, D, tm, 512)
    h1 = _mm_resw(mixed, w_out[0], x2, tm, 512)
    hn = _norm(h1, norm_ffn_w[0], BF16)
    act = _mm_glu(hn, w_ffn_gate[0], w_ffn_up[0], _pick(S, (2048, 512)), 256)
    h2 = _mm_res(act, w_d, h1, tm, 512, F // 2)
    y = _norm(h2, norm_final_w, x.dtype)
    return y[None]
```

```python
import functools
import math

import jax
import jax.numpy as jnp
from jax import lax
from jax.experimental import pallas as pl
from jax.experimental.pallas import tpu as pltpu

F32 = jnp.float32
BF16 = jnp.bfloat16

N_META = 16
HEAD_DIM = 128
WINDOW = 128
BLOCK = 128
ROPE_DIMS = HEAD_DIM // 4
ROPE_THETA = 500000.0
RWKV_HEAD = 64
RMS_EPS = 1e-6
GN_EPS = 64e-5
LANES = 128
PAD_ROWS = BLOCK
CHUNK = 64
LORA_PAD = 512
VMEM_PHYSICAL = 64 * 1024 * 1024

NN = (((1,), (0,)), ((), ()))
NT = (((1,), (1,)), ((), ()))
TN = (((0,), (0,)), ((), ()))


def _vmem_limit(*block_bytes, temps=0):
    need = 2 * sum(block_bytes) + temps + (4 << 20)
    return int(min(max(need, 16 << 20), VMEM_PHYSICAL - (6 << 20)))


def _params(sem, limit):
    return pltpu.CompilerParams(dimension_semantics=sem, vmem_limit_bytes=limit)


def _pick(n, cands):
    for c in cands:
        if n % c == 0:
            return c
    raise ValueError(f"no tile in {cands} divides {n}")


def _rms(x, w):
    return x * lax.rsqrt(jnp.mean(x * x, axis=-1, keepdims=True) + RMS_EPS) * w


def _norm_ext_kernel(x_ref, meta_ref, w_ref, o_ref, *, nb):
    i = pl.program_id(0)

    @pl.when(i < nb)
    def _():
        o_ref[...] = _rms(x_ref[...], w_ref[...]).astype(o_ref.dtype)

    @pl.when(i == nb)
    def _():
        o_ref[...] = jnp.zeros(o_ref.shape, o_ref.dtype)
        o_ref[PAD_ROWS - N_META:, :] = _rms(meta_ref[...], w_ref[...]).astype(o_ref.dtype)


def _norm_ext(x, meta, w):
    S, D = x.shape
    nb = S // BLOCK
    return pl.pallas_call(
        functools.partial(_norm_ext_kernel, nb=nb),
        grid=(nb + 1,),
        in_specs=[pl.BlockSpec((BLOCK, D), lambda i: (jnp.minimum(i, nb - 1), 0)),
                  pl.BlockSpec((N_META, D), lambda i: (0, 0)),
                  pl.BlockSpec((1, D), lambda i: (0, 0))],
        out_specs=pl.BlockSpec((BLOCK, D), lambda i: (i, 0)),
        out_shape=jax.ShapeDtypeStruct((S + PAD_ROWS, D), BF16),
        compiler_params=_params(("arbitrary",), _vmem_limit(BLOCK * D * 4, BLOCK * D * 2)),
        name="norm_ext",
    )(x, meta, w.reshape(1, D))


def _norm_kernel(x_ref, w_ref, o_ref):
    o_ref[...] = _rms(x_ref[...], w_ref[...]).astype(o_ref.dtype)


def _norm(x, w, out_dtype, tr=256):
    S, D = x.shape
    return pl.pallas_call(
        _norm_kernel,
        grid=(S // tr,),
        in_specs=[pl.BlockSpec((tr, D), lambda i: (i, 0)), pl.BlockSpec((1, D), lambda i: (0, 0))],
        out_specs=pl.BlockSpec((tr, D), lambda i: (i, 0)),
        out_shape=jax.ShapeDtypeStruct((S, D), out_dtype),
        compiler_params=_params(("arbitrary",), _vmem_limit(tr * D * 4, tr * D * 4)),
        name="norm",
    )(x, w.reshape(1, D))


def _dot(a, b, dims=NN):
    return lax.dot_general(a, b, dims, preferred_element_type=F32)


def _cast_resident(src_ref, dst_ref, col0=None, n_valid=None):
    K = src_ref.shape[0]
    rows = _pick(K, (256, 128))

    def body(c, carry):
        sl = pl.ds(pl.multiple_of(c * rows, rows), rows)
        blk = src_ref[sl, :]
        if n_valid is not None:
            col = col0 + lax.broadcasted_iota(jnp.int32, blk.shape, 1)
            blk = jnp.where(col < n_valid, blk, 0.0)
        dst_ref[sl, :] = blk.astype(dst_ref.dtype)
        return carry

    lax.fori_loop(0, K // rows, body, 0)


def _mm_in_kernel(a_ref, b_ref, o_ref, bw_ref, *, n_valid):
    j, i = pl.program_id(0), pl.program_id(1)

    @pl.when(i == 0)
    def _():
        _cast_resident(b_ref, bw_ref, j * b_ref.shape[1], n_valid)

    o_ref[...] = _dot(a_ref[...], bw_ref[...])


def _mm_in(a, w, tm, tn):
    M, K = a.shape
    N = w.shape[1]
    nj = pl.cdiv(N, tn)
    return pl.pallas_call(
        functools.partial(_mm_in_kernel, n_valid=N),
        grid=(nj, M // tm),
        in_specs=[pl.BlockSpec((tm, K), lambda j, i: (i, 0)), pl.BlockSpec((K, tn), lambda j, i: (0, j))],
        out_specs=pl.BlockSpec((tm, tn), lambda j, i: (i, j)),
        out_shape=jax.ShapeDtypeStruct((M, nj * tn), F32),
        scratch_shapes=[pltpu.VMEM((K, tn), BF16)],
        compiler_params=_params(("arbitrary", "arbitrary"),
                                _vmem_limit(tm * K * 2, K * tn * 4, tm * tn * 4, temps=K * tn * 2 + tm * tn * 4)),
        name="mm_in",
    )(a, w)


def _mm_mix_kernel(a1_ref, b1_ref, a2_ref, b2_ref, z0_ref, z1_ref, bg_ref, o_ref, w1_ref, w2_ref):
    @pl.when(pl.program_id(1) == 0)
    def _():
        _cast_resident(b1_ref, w1_ref)
        _cast_resident(b2_ref, w2_ref)

    g0 = jax.nn.sigmoid(z0_ref[...] + bg_ref[0:1, :])
    g1 = jax.nn.sigmoid(z1_ref[...] + bg_ref[1:2, :])
    o_ref[...] = (g0 * _dot(a1_ref[...], w1_ref[...]) + g1 * _dot(a2_ref[...], w2_ref[...])).astype(o_ref.dtype)


def _mm_mix(o_att, w_pa, o_rwkv, w_pr, z, b_gate, S, D, tm, tn):
    K1, K2 = o_att.shape[1], o_rwkv.shape[1]
    nj = D // tn
    return pl.pallas_call(
        _mm_mix_kernel,
        grid=(nj, S // tm),
        in_specs=[pl.BlockSpec((tm, K1), lambda j, i: (i, 0)), pl.BlockSpec((K1, tn), lambda j, i: (0, j)),
                  pl.BlockSpec((tm, K2), lambda j, i: (i, 0)), pl.BlockSpec((K2, tn), lambda j, i: (0, j)),
                  pl.BlockSpec((tm, tn), lambda j, i: (i, j)), pl.BlockSpec((tm, tn), lambda j, i: (i, nj + j)),
                  pl.BlockSpec((2, tn), lambda j, i: (0, j))],
        out_specs=pl.BlockSpec((tm, tn), lambda j, i: (i, j)),
        out_shape=jax.ShapeDtypeStruct((S, D), BF16),
        scratch_shapes=[pltpu.VMEM((K1, tn), BF16), pltpu.VMEM((K2, tn), BF16)],
        compiler_params=_params(("arbitrary", "arbitrary"),
                                _vmem_limit(tm * K1 * 2, K1 * tn * 4, tm * K2 * 2, K2 * tn * 4,
                                            2 * tm * tn * 4, tm * tn * 2,
                                            temps=(K1 + K2) * tn * 2 + 3 * tm * tn * 4)),
        name="mm_mix",
    )(o_att, w_pa, o_rwkv, w_pr, z, z, b_gate)


def _mm_resw_kernel(a_ref, b_ref, r_ref, o_ref, bw_ref):
    @pl.when(pl.program_id(1) == 0)
    def _():
        _cast_resident(b_ref, bw_ref)

    o_ref[...] = r_ref[...] + _dot(a_ref[...], bw_ref[...])


def _mm_resw(a, w, res, tm, tn):
    M, K = a.shape
    N = w.shape[1]
    return pl.pallas_call(
        _mm_resw_kernel,
        grid=(N // tn, M // tm),
        in_specs=[pl.BlockSpec((tm, K), lambda j, i: (i, 0)), pl.BlockSpec((K, tn), lambda j, i: (0, j)),
                  pl.BlockSpec((tm, tn), lambda j, i: (i, j))],
        out_specs=pl.BlockSpec((tm, tn), lambda j, i: (i, j)),
        out_shape=jax.ShapeDtypeStruct((M, N), F32),
        scratch_shapes=[pltpu.VMEM((K, tn), BF16)],
        compiler_params=_params(("arbitrary", "arbitrary"),
                                _vmem_limit(tm * K * 2, K * tn * 4, 2 * tm * tn * 4, temps=K * tn * 2 + tm * tn * 4)),
        name="mm_out",
    )(a, w, res)


def _mm_res_kernel(a_ref, b_ref, r_ref, o_ref, *, nk):
    k = pl.program_id(2)
    part = _dot(a_ref[...], b_ref[...])
    if nk == 1:
        o_ref[...] = r_ref[...] + part
    else:
        @pl.when(k == 0)
        def _():
            o_ref[...] = r_ref[...] + part

        @pl.when(k > 0)
        def _():
            o_ref[...] += part


def _mm_res(a, b, res, tm, tn, tk):
    M, K = a.shape
    N = b.shape[1]
    nk = K // tk
    return pl.pallas_call(
        functools.partial(_mm_res_kernel, nk=nk),
        grid=(M // tm, N // tn, nk),
        in_specs=[pl.BlockSpec((tm, tk), lambda i, j, k: (i, k)), pl.BlockSpec((tk, tn), lambda i, j, k: (k, j)),
                  pl.BlockSpec((tm, tn), lambda i, j, k: (i, j))],
        out_specs=pl.BlockSpec((tm, tn), lambda i, j, k: (i, j)),
        out_shape=jax.ShapeDtypeStruct((M, N), F32),
        compiler_params=_params(("arbitrary", "arbitrary", "arbitrary"),
                                _vmem_limit(tm * tk * 2, tk * tn * 2, 2 * tm * tn * 4, temps=tm * tn * 4)),
        name="mm_down",
    )(a, b, res)


def _mm_glu_kernel(a_ref, bg_ref, bu_ref, o_ref, wg_ref, wu_ref):
    @pl.when(pl.program_id(1) == 0)
    def _():
        _cast_resident(bg_ref, wg_ref)
        _cast_resident(bu_ref, wu_ref)

    a = a_ref[...]
    gate = _dot(a, wg_ref[...])
    up = _dot(a, wu_ref[...])
    o_ref[...] = (gate * jax.nn.sigmoid(gate) * up).astype(o_ref.dtype)


def _mm_glu(a, wg, wu, tm, tn):
    M, K = a.shape
    N = wg.shape[1]
    return pl.pallas_call(
        _mm_glu_kernel,
        grid=(N // tn, M // tm),
        in_specs=[pl.BlockSpec((tm, K), lambda j, i: (i, 0)), pl.BlockSpec((K, tn), lambda j, i: (0, j)),
                  pl.BlockSpec((K, tn), lambda j, i: (0, j))],
        out_specs=pl.BlockSpec((tm, tn), lambda j, i: (i, j)),
        out_shape=jax.ShapeDtypeStruct((M, N), BF16),
        scratch_shapes=[pltpu.VMEM((K, tn), BF16), pltpu.VMEM((K, tn), BF16)],
        compiler_params=_params(("arbitrary", "arbitrary"),
                                _vmem_limit(tm * K * 2, 2 * K * tn * 4, tm * tn * 2,
                                            temps=2 * K * tn * 2 + 3 * tm * tn * 4)),
        name="mm_glu",
    )(a, wg, wu)


def _rope_kernel(z_ref, cos_ref, sin_ref, o_ref, *, n_rope_blocks, heads_per_block, k_heads_last):
    j = pl.program_id(1)
    cos = cos_ref[...]
    sin = sin_ref[...]
    lane = lax.broadcasted_iota(jnp.int32, cos.shape, 1)
    half = ROPE_DIMS // 2
    for h in range(heads_per_block):
        t = z_ref[:, h * HEAD_DIM:(h + 1) * HEAD_DIM]
        up = jnp.concatenate([t[:, half:], t[:, :half]], axis=1)
        down = jnp.concatenate([t[:, HEAD_DIM - half:], t[:, :HEAD_DIM - half]], axis=1)
        partner = jnp.where(lane < half, up, down)
        rot = t * cos + partner * sin
        if h >= k_heads_last:
            rot = jnp.where(j == n_rope_blocks - 1, t, rot)
        o_ref[:, h * HEAD_DIM:(h + 1) * HEAD_DIM] = rot.astype(o_ref.dtype)


def _rope_cast(z, cos_t, sin_t, off_q, width, kv_width):
    Lp = z.shape[0]
    tc = 2 * kv_width
    assert off_q % tc == 0 and width % tc == 0
    nblk = width // tc
    return pl.pallas_call(
        functools.partial(_rope_kernel, n_rope_blocks=nblk, heads_per_block=tc // HEAD_DIM,
                          k_heads_last=kv_width // HEAD_DIM),
        grid=(Lp // BLOCK, nblk),
        in_specs=[pl.BlockSpec((BLOCK, tc), lambda i, j: (i, off_q // tc + j)),
                  pl.BlockSpec((BLOCK, HEAD_DIM), lambda i, j: (i, 0)),
                  pl.BlockSpec((BLOCK, HEAD_DIM), lambda i, j: (i, 0))],
        out_specs=pl.BlockSpec((BLOCK, tc), lambda i, j: (i, j)),
        out_shape=jax.ShapeDtypeStruct((Lp, width), BF16),
        compiler_params=_params(("arbitrary", "arbitrary"), _vmem_limit(BLOCK * tc * 4, BLOCK * tc * 2)),
        name="rope_cast",
    )(z, cos_t, sin_t)


NEG_BIG = -1e30


def _attn_kernel(q_ref, kp_ref, kc_ref, kn_ref, km_ref, vp_ref, vc_ref, vn_ref, vm_ref, sink_ref, o_ref,
                 *, nb, group):
    i = pl.program_id(1)
    scale = HEAD_DIM ** -0.5
    zpad = jnp.zeros((BLOCK - N_META, HEAD_DIM), BF16)
    k_all = jnp.concatenate([kp_ref[...], kc_ref[...], kn_ref[...], km_ref[...], zpad], axis=0)
    v_all = jnp.concatenate([vp_ref[...], vc_ref[...], vn_ref[...], vm_ref[...], zpad], axis=0)
    q_all = jnp.concatenate([q_ref[:, g * HEAD_DIM:(g + 1) * HEAD_DIM] for g in range(group)], axis=0)
    s = _dot(q_all, k_all, NT) * scale
    rows = lax.broadcasted_iota(jnp.int32, s.shape, 0) % BLOCK
    cols = lax.broadcasted_iota(jnp.int32, s.shape, 1)
    rel = cols - BLOCK - rows
    kblk = i - 1 + cols // BLOCK
    band_ok = (cols < 3 * BLOCK) & (kblk >= 0) & (kblk < nb) & (jnp.abs(rel) <= WINDOW)
    meta_ok = (cols >= 3 * BLOCK) & (cols < 3 * BLOCK + N_META)
    s = jnp.where(band_ok | meta_ok, s, NEG_BIG)
    sink = sink_ref[:, 0:1]
    m = jnp.maximum(jnp.max(s, axis=-1, keepdims=True), sink)
    p = jnp.exp(s - m)
    denom = jnp.sum(p, axis=-1, keepdims=True) + jnp.exp(sink - m)
    o = _dot(p.astype(BF16), v_all) / denom
    for g in range(group):
        o_ref[:, g * HEAD_DIM:(g + 1) * HEAD_DIM] = o[g * BLOCK:(g + 1) * BLOCK, :].astype(o_ref.dtype)


def _attention(qkv, sink, S, n_heads, n_kv):
    group = n_heads // n_kv
    nb = S // BLOCK
    qw = group * HEAD_DIM
    kcol = n_heads
    vcol = n_heads + n_kv
    meta_blk = (S + PAD_ROWS - N_META) // N_META
    sink_rows = jnp.broadcast_to(sink.astype(F32).reshape(n_kv, group, 1, 1),
                                 (n_kv, group, BLOCK, LANES)).reshape(n_kv, group * BLOCK, LANES)

    def kv_specs(col0):
        return [pl.BlockSpec((BLOCK, HEAD_DIM), lambda h, i: (jnp.maximum(i - 1, 0), col0 + h)),
                pl.BlockSpec((BLOCK, HEAD_DIM), lambda h, i: (i, col0 + h)),
                pl.BlockSpec((BLOCK, HEAD_DIM), lambda h, i: (jnp.minimum(i + 1, nb - 1), col0 + h)),
                pl.BlockSpec((N_META, HEAD_DIM), lambda h, i: (meta_blk, col0 + h))]

    return pl.pallas_call(
        functools.partial(_attn_kernel, nb=nb, group=group),
        grid=(n_kv, nb),
        in_specs=[pl.BlockSpec((BLOCK, qw), lambda h, i: (i, h))] + kv_specs(kcol) + kv_specs(vcol)
                 + [pl.BlockSpec((None, group * BLOCK, LANES), lambda h, i: (h, 0, 0))],
        out_specs=pl.BlockSpec((BLOCK, qw), lambda h, i: (i, h)),
        out_shape=jax.ShapeDtypeStruct((S, n_heads * HEAD_DIM), BF16),
        compiler_params=_params(("arbitrary", "arbitrary"),
                                _vmem_limit(BLOCK * qw * 2 * 2, 8 * BLOCK * HEAD_DIM * 2,
                                            group * BLOCK * LANES * 4, temps=6 * group * BLOCK * 512 * 4)),
        name="attention",
    )(qkv, qkv, qkv, qkv, qkv, qkv, qkv, qkv, qkv, sink_rows)


def _rope(t, cos, sin):
    half = ROPE_DIMS // 2
    lane = lax.broadcasted_iota(jnp.int32, t.shape, 1)
    up = jnp.concatenate([t[:, half:], t[:, :half]], axis=1)
    down = jnp.concatenate([t[:, HEAD_DIM - half:], t[:, :HEAD_DIM - half]], axis=1)
    return t * cos + jnp.where(lane < half, up, down) * sin


def _attn_res_kernel(sink_ref, q_ref, k_ref, v_ref, cos_ref, sin_ref, o_ref, kb_ref, vb_ref,
                     *, nb, group, qblocks):
    h, i = pl.program_id(0), pl.program_id(1)
    lp = k_ref.shape[0]

    @pl.when(i == 0)
    def _():
        def body(c, carry):
            sl = pl.ds(pl.multiple_of(c * BLOCK, BLOCK), BLOCK)
            kb_ref[sl, :] = _rope(k_ref[sl, :], cos_ref[sl, :], sin_ref[sl, :]).astype(BF16)
            vb_ref[sl, :] = v_ref[sl, :].astype(BF16)
            return carry
        lax.fori_loop(0, lp // BLOCK, body, 0)

    zpad = jnp.zeros((BLOCK - N_META, HEAD_DIM), BF16)
    k_meta = jnp.concatenate([kb_ref[lp - N_META:lp, :], zpad], axis=0)
    v_meta = jnp.concatenate([vb_ref[lp - N_META:lp, :], zpad], axis=0)
    rows = lax.broadcasted_iota(jnp.int32, (BLOCK, 4 * BLOCK), 0)
    cols = lax.broadcasted_iota(jnp.int32, (BLOCK, 4 * BLOCK), 1)
    is_meta = (cols >= 3 * BLOCK) & (cols < 3 * BLOCK + N_META)
    scale = HEAD_DIM ** -0.5
    k_all, v_all, bias, q, sink = [], [], [], [], []
    for b in range(qblocks):
        qi = i * qblocks + b
        kb0 = jnp.clip(qi - 1, 0, nb - 3)
        win = pl.ds(pl.multiple_of(kb0 * BLOCK, BLOCK), 3 * BLOCK)
        k_b = jnp.concatenate([kb_ref[win, :], k_meta], axis=0)
        v_b = jnp.concatenate([vb_ref[win, :], v_meta], axis=0)
        rel = cols - rows + (kb0 - qi) * BLOCK
        ok = ((cols < 3 * BLOCK) & (jnp.abs(rel) <= WINDOW)) | is_meta
        bias_b = jnp.where(ok, 0.0, NEG_BIG).astype(F32)
        qrows = pl.ds(pl.multiple_of(qi * BLOCK, BLOCK), BLOCK)
        cos_q, sin_q = cos_ref[qrows, :], sin_ref[qrows, :]
        for g in range(group):
            k_all.append(k_b)
            v_all.append(v_b)
            bias.append(bias_b)
            q.append(_rope(q_ref[b * BLOCK:(b + 1) * BLOCK, g * HEAD_DIM:(g + 1) * HEAD_DIM],
                           cos_q, sin_q).astype(BF16))
            sink.append(sink_ref[h * group + g])
    n = range(len(q))
    s = [_dot(q[c], k_all[c], NT) * scale + bias[c] for c in n]
    m = [jnp.maximum(jnp.max(s[c], axis=-1, keepdims=True), sink[c]) for c in n]
    p = [jnp.exp(s[c] - m[c]) for c in n]
    denom = [jnp.sum(p[c], axis=-1, keepdims=True) + jnp.exp(sink[c] - m[c]) for c in n]
    o = [_dot(p[c].astype(BF16), v_all[c]) / denom[c] for c in n]
    for c in n:
        b, g = divmod(c, group)
        o_ref[b * BLOCK:(b + 1) * BLOCK, g * HEAD_DIM:(g + 1) * HEAD_DIM] = o[c].astype(o_ref.dtype)


def _attention_res(z, cos_t, sin_t, sink, S, n_heads, n_kv, off_q):
    Lp = z.shape[0]
    group = n_heads // n_kv
    nb = S // BLOCK
    assert nb >= 3
    qw = group * HEAD_DIM
    kcol = off_q // HEAD_DIM + n_heads
    vcol = kcol + n_kv
    col_bytes = Lp * HEAD_DIM * 4
    qblocks = _pick(nb, (2, 1))
    tq = qblocks * BLOCK
    return pl.pallas_call(
        functools.partial(_attn_res_kernel, nb=nb, group=group, qblocks=qblocks),
        grid_spec=pltpu.PrefetchScalarGridSpec(
            num_scalar_prefetch=1,
            grid=(n_kv, nb // qblocks),
            in_specs=[pl.BlockSpec((tq, qw), lambda h, i, s: (i, off_q // qw + h)),
                      pl.BlockSpec((Lp, HEAD_DIM), lambda h, i, s: (0, kcol + h)),
                      pl.BlockSpec((Lp, HEAD_DIM), lambda h, i, s: (0, vcol + h)),
                      pl.BlockSpec((Lp, HEAD_DIM), lambda h, i, s: (0, 0)),
                      pl.BlockSpec((Lp, HEAD_DIM), lambda h, i, s: (0, 0))],
            out_specs=pl.BlockSpec((tq, qw), lambda h, i, s: (i, h)),
            scratch_shapes=[pltpu.VMEM((Lp, HEAD_DIM), BF16), pltpu.VMEM((Lp, HEAD_DIM), BF16)]),
        out_shape=jax.ShapeDtypeStruct((S, n_heads * HEAD_DIM), BF16),
        compiler_params=_params(("arbitrary", "arbitrary"),
                                _vmem_limit(tq * qw * 4, 4 * col_bytes, tq * qw * 2,
                                            temps=col_bytes + 8 * group * tq * 512 * 4)),
        name="attention",
    )(sink.astype(F32), z, z, z, cos_t, sin_t)


def _split3(x):
    hi = x.astype(BF16)
    r1 = x - hi.astype(F32)
    mid = r1.astype(BF16)
    lo = (r1 - mid.astype(F32)).astype(BF16)
    return hi, mid, lo


def _split2(x):
    hi = x.astype(BF16)
    return hi, (x - hi.astype(F32)).astype(BF16)


def _dot_exact_rhs(x, m_bf16, dims=NN, left=False):
    parts = _split3(x)
    if left:
        return sum(_dot(m_bf16, p, dims) for p in parts)
    return sum(_dot(p, m_bf16, dims) for p in parts)


def _dotp(a, b, passes, dims=NN):
    if passes == 1:
        return _dot(a.astype(BF16), b.astype(BF16), dims)
    a_hi, a_lo = _split2(a)
    b_hi, b_lo = _split2(b)
    return _dot(a_hi, b_hi, dims) + (_dot(a_hi, b_lo, dims) + _dot(a_lo, b_hi, dims))


def _head_sum(x, ones_bd):
    outs = []
    for s in range(x.shape[1] // LANES):
        outs.append(_dot_exact_rhs(x[:, s * LANES:(s + 1) * LANES], ones_bd))
    return outs[0] if len(outs) == 1 else jnp.concatenate(outs, axis=1)


def _prep_kernel(r_ref, rp_ref, rn_ref, k_ref, kp_ref, kn_ref, v_ref, vp_ref, vn_ref,
                 lo_ref, lop_ref, lon_ref, gd_ref, gdp_ref, gdn_ref,
                 pc_ref, pl_ref, wl_ref, g2_ref, ones_ref,
                 ro_ref, vo_ref, kko_ref, lwf_ref, lwb_ref, kdf_ref, kdb_ref, bdf_ref, bdb_ref, go_ref, bo_ref,
                 buf_ref, *, n_tiles, tr, lora_tanh_cols):
    i = pl.program_id(1)

    def shifted(main_ref, prev_ref, next_ref, mup, mun):
        w = main_ref.shape[1]
        buf_ref[0:8, 0:w] = prev_ref[...]
        buf_ref[8:8 + tr, 0:w] = main_ref[...]
        buf_ref[8 + tr:16 + tr, 0:w] = next_ref[...]
        x = main_ref[...]
        prev = buf_ref[7:7 + tr, 0:w]
        nxt = buf_ref[9:9 + tr, 0:w]
        return x + mup * (prev - x) + mun * (nxt - x)

    r = shifted(r_ref, rp_ref, rn_ref, pc_ref[0:1, :], pc_ref[1:2, :])
    k = shifted(k_ref, kp_ref, kn_ref, pc_ref[2:3, :], pc_ref[3:4, :])
    v = shifted(v_ref, vp_ref, vn_ref, pc_ref[4:5, :], pc_ref[5:6, :])
    lo = shifted(lo_ref, lop_ref, lon_ref, pl_ref[0:1, 0:LORA_PAD], pl_ref[1:2, 0:LORA_PAD])
    gd = shifted(gd_ref, gdp_ref, gdn_ref, pl_ref[0:1, LORA_PAD:], pl_ref[1:2, LORA_PAD:])

    row = lax.broadcasted_iota(jnp.int32, (tr, 1), 0)
    valid = jnp.where((i < n_tiles - 1) | (row >= tr - N_META), 1.0, 0.0).astype(F32)

    lane = lax.broadcasted_iota(jnp.int32, lo.shape, 1)
    lo_act = jnp.where(lane < lora_tanh_cols, jnp.tanh(lo), lo).astype(BF16)
    dec_f = _dot(lo_act, wl_ref[0])
    dec_b = _dot(lo_act, wl_ref[1])
    apre_f = _dot(lo_act, wl_ref[2])
    apre_b = _dot(lo_act, wl_ref[3])
    g = _dot(jax.nn.sigmoid(lo).astype(BF16), g2_ref[0]) + _dot(jax.nn.sigmoid(gd).astype(BF16), g2_ref[1])

    def log_decay(dec, w0):
        return -math.exp(-0.5) * jax.nn.sigmoid(w0 + dec)

    a_f = jax.nn.sigmoid(pc_ref[8:9, :] + apre_f)
    a_b = jax.nn.sigmoid(pc_ref[9:10, :] + apre_b)
    kk = k * pc_ref[10:11, :]
    ss = _head_sum(kk * kk, ones_ref[...])
    kk = kk / jnp.maximum(jnp.sqrt(ss), 1e-12) * valid
    k_a = pc_ref[11:12, :]
    kv = k * valid

    rv = r * valid
    vv = v * valid
    kd_f = kv * (1.0 + (a_f - 1.0) * k_a)
    kd_b = kv * (1.0 + (a_b - 1.0) * k_a)
    ro_ref[...] = rv
    vo_ref[...] = vv
    kko_ref[...] = kk
    lwf_ref[...] = log_decay(dec_f, pc_ref[6:7, :])
    lwb_ref[...] = log_decay(dec_b, pc_ref[7:8, :])
    kdf_ref[...] = kd_f
    kdb_ref[...] = kd_b
    bdf_ref[...] = kk * a_f
    bdb_ref[...] = kk * a_b
    go_ref[...] = g
    bo_ref[...] = _head_sum(rv * (kd_f + kd_b) * pc_ref[12:13, :], ones_ref[...]) * vv


def _rwkv_prep(z, off_r, C, pc, plo, wl, g2, ones_bd, lora_tanh_cols):
    Lp = z.shape[0]
    tr = BLOCK
    ct = _pick(C, (1024, 512))
    n_tiles = Lp // tr
    n8 = Lp // 8
    off_lo = off_r + 3 * C
    off_gd = off_lo + LORA_PAD
    assert off_r % ct == 0 and off_lo % LORA_PAD == 0

    def seg_specs(off, w, with_c):
        cb = off // w

        def col(c):
            return cb + c if with_c else cb
        return [pl.BlockSpec((tr, w), lambda c, i: (i, col(c))),
                pl.BlockSpec((8, w), lambda c, i: ((i * (tr // 8) + n8 - 1) % n8, col(c))),
                pl.BlockSpec((8, w), lambda c, i: (((i + 1) * (tr // 8)) % n8, col(c)))]

    in_specs = (seg_specs(off_r, ct, True) + seg_specs(off_r + C, ct, True) + seg_specs(off_r + 2 * C, ct, True)
                + seg_specs(off_lo, LORA_PAD, False) + seg_specs(off_gd, LORA_PAD, False)
                + [pl.BlockSpec((16, ct), lambda c, i: (0, c)),
                   pl.BlockSpec((8, 2 * LORA_PAD), lambda c, i: (0, 0)),
                   pl.BlockSpec((4, LORA_PAD, ct), lambda c, i: (0, 0, c)),
                   pl.BlockSpec((2, LORA_PAD, ct), lambda c, i: (0, 0, c)),
                   pl.BlockSpec((LANES, LANES), lambda c, i: (0, 0))])
    out_spec = pl.BlockSpec((tr, ct), lambda c, i: (i, c))
    out_sds = jax.ShapeDtypeStruct((Lp, C), F32)
    return pl.pallas_call(
        functools.partial(_prep_kernel, n_tiles=n_tiles, tr=tr, lora_tanh_cols=lora_tanh_cols),
        grid=(C // ct, n_tiles),
        in_specs=in_specs,
        out_specs=[out_spec] * 11,
        out_shape=[out_sds] * 11,
        scratch_shapes=[pltpu.VMEM((tr + 16, max(ct, LORA_PAD)), F32)],
        compiler_params=_params(("arbitrary", "arbitrary"),
                                _vmem_limit(3 * tr * ct * 4, 2 * tr * LORA_PAD * 4, 4 * LORA_PAD * ct * 2,
                                            2 * LORA_PAD * ct * 2, 11 * tr * ct * 4, temps=24 * tr * ct * 4)),
        name="rwkv_prep",
    )(*([z] * 15), pc, plo, wl, g2, ones_bd)


SCAN_PAIRS = 8


def _scan_consts():
    C = CHUNK
    row = lax.broadcasted_iota(jnp.int32, (C, 2 * C), 0)
    col = lax.broadcasted_iota(jnp.int32, (C, 2 * C), 1) % C
    rr = lax.broadcasted_iota(jnp.int32, (LANES, LANES), 0)
    cc = lax.broadcasted_iota(jnp.int32, (LANES, LANES), 1)
    return dict(
        row=lax.broadcasted_iota(jnp.int32, (C, LANES), 0),
        head0=lax.broadcasted_iota(jnp.int32, (C, LANES), 1) < RWKV_HEAD,
        strict_f=col < row, strict_b=col > row, incl_f=col <= row, incl_b=col >= row,
        eye_sbs=jnp.where(col == row, 1.0, 0.0).astype(F32),
        same_head=(rr // RWKV_HEAD) == (cc // RWKV_HEAD), eye=rr == cc)


def _scan_chains(chains, cst):
    C = CHUNK
    head0, same, eye = cst["head0"], cst["same_head"], cst["eye"]
    n = range(len(chains))
    rev = [c[7] for c in chains]
    r, v, al, lw, k, be, h = ([c[i] for c in chains] for i in range(7))
    strict = [cst["strict_b" if x else "strict_f"] for x in rev]
    incl = [cst["incl_b" if x else "incl_f"] for x in rev]

    def bd(x):
        xb = x.astype(BF16)
        zero = jnp.zeros_like(xb)
        return jnp.concatenate([jnp.where(head0, xb, zero), jnp.where(head0, zero, xb)], axis=0)

    def mm(a, b_bf16, dims=NN):
        return _dot(a.astype(BF16), b_bf16, dims)

    row = cst["row"]
    cl = list(lw)
    sh = 1
    while sh < C:
        cl = [cl[i] + (jnp.where(row < C - sh, pltpu.roll(cl[i], C - sh, 0), 0.0) if rev[i] else
                       jnp.where(row >= sh, pltpu.roll(cl[i], sh, 0), 0.0)) for i in n]
        sh *= 2
    total = [cl[i][0:1, :] if rev[i] else cl[i][C - 1:C, :] for i in n]
    a_t = [al[i] * jnp.exp(cl[i] - lw[i]) for i in n]
    r_t = [r[i] * jnp.exp(cl[i]) for i in n]
    w_inv = [jnp.exp(-cl[i]) for i in n]
    w_rest = [jnp.exp(total[i] - cl[i]) for i in n]
    bd_v = [bd(v[i]) for i in n]
    sc = [mm(jnp.concatenate([a_t[i], r_t[i]], axis=0),
             jnp.concatenate([bd(k[i] * w_inv[i]), bd(be[i] * w_inv[i])], axis=0), NT) for i in n]
    a_ak = [jnp.where(strict[i], sc[i][0:C, 0:2 * C], 0.0) for i in n]
    n_ab = [jnp.where(strict[i], sc[i][0:C, 2 * C:], 0.0) for i in n]
    a_rk = [jnp.where(incl[i], sc[i][C:, 0:2 * C], 0.0) for i in n]
    a_rb = [jnp.where(incl[i], sc[i][C:, 2 * C:], 0.0) for i in n]

    t_inv = [cst["eye_sbs"] - n_ab[i] for i in n]
    sq = [mm(n_ab[i], bd(n_ab[i])) for i in n]
    n_round = 2
    while (1 << n_round) < C:
        both = [mm(jnp.concatenate([t_inv[i], sq[i]], axis=0), bd(sq[i])) for i in n]
        t_inv = [t_inv[i] + both[i][0:C] for i in n]
        sq = [both[i][C:] for i in n]
        n_round += 1
    t_inv = [t_inv[i] + mm(t_inv[i], bd(sq[i])) for i in n]

    akv = [mm(jnp.concatenate([a_ak[i], a_rk[i]], axis=0), bd_v[i]) for i in n]
    tp = [mm(t_inv[i], jnp.concatenate([bd(akv[i][0:C]), bd(a_t[i])], axis=1)) for i in n]
    p0 = [tp[i][:, 0:LANES] for i in n]
    at = [tp[i][:, LANES:] for i in n]
    rb = [mm(a_rb[i], jnp.concatenate([bd(p0[i]), bd(at[i])], axis=1)) for i in n]
    y0 = [akv[i][C:] - rb[i][:, 0:LANES] for i in n]
    rh = [r_t[i] - rb[i][:, LANES:] for i in n]
    bt = [mm(be[i] * w_rest[i], jnp.concatenate([at[i], p0[i]], axis=1).astype(BF16), TN) for i in n]
    kv = [mm(k[i] * w_rest[i], v[i].astype(BF16), TN) for i in n]
    m_mat = [jnp.where(same, jnp.where(eye, jnp.broadcast_to(jnp.exp(total[i]), (LANES, LANES)), 0.0)
                       - bt[i][:, 0:LANES], 0.0) for i in n]
    g_mat = [jnp.where(same, kv[i] - bt[i][:, LANES:], 0.0) for i in n]
    yh = [mm(jnp.concatenate([rh[i], m_mat[i]], axis=0), h[i].astype(BF16)) for i in n]
    return [(yh[i][0:C] + y0[i], yh[i][C:] + g_mat[i]) for i in n]


def _scan_kernel(rf_ref, vf_ref, af_ref, lwf_ref, kf_ref, bf_ref,
                 rb_ref, vb_ref, ab_ref, lwb_ref, kb_ref, bb_ref,
                 yf_ref, yb_ref, hf_ref, hb_ref, *, pairs):
    @pl.when(pl.program_id(1) == 0)
    def _():
        hf_ref[...] = jnp.zeros(hf_ref.shape, F32)
        hb_ref[...] = jnp.zeros(hb_ref.shape, F32)

    cst = _scan_consts()
    chains = []
    for g in range(pairs):
        sl = slice(g * LANES, (g + 1) * LANES)
        chains.append((rf_ref[:, sl], vf_ref[:, sl], af_ref[:, sl], lwf_ref[:, sl], kf_ref[:, sl], bf_ref[:, sl],
                       hf_ref[g], False))
        chains.append((rb_ref[:, sl], vb_ref[:, sl], ab_ref[:, sl], lwb_ref[:, sl], kb_ref[:, sl], bb_ref[:, sl],
                       hb_ref[g], True))
    res = _scan_chains(chains, cst)
    for g in range(pairs):
        sl = slice(g * LANES, (g + 1) * LANES)
        yf_ref[:, sl], hf_ref[g] = res[2 * g]
        yb_ref[:, sl], hb_ref[g] = res[2 * g + 1]


def _rwkv_scan(r, v, kk, lw_f, lw_b, kd_f, kd_b, bd_f, bd_b, S):
    Lp, C = r.shape
    n_real = S // CHUNK
    n_chunks = Lp // CHUNK
    steps = n_real + 1
    pairs = _pick(C // LANES, (SCAN_PAIRS, 2, 1))
    w = pairs * LANES

    def fwd(p, s):
        return ((s + n_real + 1) % n_chunks, p)

    def bwd(p, s):
        return (n_real - s, p)

    spec_f = pl.BlockSpec((CHUNK, w), fwd)
    spec_b = pl.BlockSpec((CHUNK, w), bwd)
    out_sds = jax.ShapeDtypeStruct((S + CHUNK, C), F32)
    out_f = pl.BlockSpec((CHUNK, w), lambda p, s: ((s + n_real) % (n_real + 1), p))
    return pl.pallas_call(
        functools.partial(_scan_kernel, pairs=pairs),
        grid=(C // w, steps),
        in_specs=[spec_f] * 6 + [spec_b] * 6,
        out_specs=[out_f, spec_b],
        out_shape=[out_sds, out_sds],
        scratch_shapes=[pltpu.VMEM((pairs, LANES, LANES), F32), pltpu.VMEM((pairs, LANES, LANES), F32)],
        compiler_params=_params(("arbitrary", "arbitrary"), 32 << 20),
        name="rwkv_scan",
    )(r, v, kk, lw_f, kd_f, bd_f, r, v, kk, lw_b, kd_b, bd_b)


def _post_kernel(yf_ref, yb_ref, bonus_ref, g_ref, pq_ref, ones_ref, o_ref):
    ones_bd = ones_ref[...]
    inv_n = 1.0 / RWKV_HEAD
    y = yf_ref[...] + yb_ref[...]
    mean = _head_sum(y, ones_bd) * inv_n
    yc = y - mean
    var = _head_sum(yc * yc, ones_bd) * inv_n
    yn = yc * lax.rsqrt(var + GN_EPS) * pq_ref[0:1, :] + pq_ref[1:2, :]
    o_ref[...] = ((yn + bonus_ref[...]) * g_ref[...]).astype(o_ref.dtype)


def _rwkv_post(y_f, y_b, bonus, g, pq, ones_bd, S):
    C = g.shape[1]
    tr = 256
    ct = _pick(C, (1024, 512))
    spec = pl.BlockSpec((tr, ct), lambda i, c: (i, c))
    return pl.pallas_call(
        _post_kernel,
        grid=(S // tr, C // ct),
        in_specs=[spec] * 4 + [pl.BlockSpec((8, ct), lambda i, c: (0, c)),
                               pl.BlockSpec((LANES, LANES), lambda i, c: (0, 0))],
        out_specs=spec,
        out_shape=jax.ShapeDtypeStruct((S, C), BF16),
        compiler_params=_params(("arbitrary", "arbitrary"),
                                _vmem_limit(4 * tr * ct * 4, tr * ct * 2, temps=16 * tr * ct * 4)),
        name="rwkv_post",
    )(y_f, y_b, bonus, g, pq, ones_bd)


def _pad_cols(w, width):
    return jnp.pad(w, ((0, 0), (0, width - w.shape[1])))


def kernel(x, meta_tokens, norm_mix_w, w_in, b_gate, mu_prev, mu_next, dec_w0, dec_w2, iclr_a0, iclr_a2,
           gate_w2, k_k, k_a, r_k, ln_x_w, ln_x_b, attn_sink, w_proj_attn, w_proj_rwkv, w_out, norm_ffn_w,
           w_ffn_gate, w_ffn_up, w_ffn_down, norm_final_w):
    B, S, D = x.shape
    assert B == 1 and norm_mix_w.shape[0] == 1, "one sequence, one layer"
    C = k_k.shape[-1]
    AW = w_proj_attn.shape[1]
    n_heads = attn_sink.shape[-1]
    DL, IL, GL = dec_w2.shape[2], iclr_a2.shape[2], gate_w2.shape[1]
    shift_w = mu_prev.shape[-1]
    KVW = (w_in.shape[-1] - 2 * D - AW - shift_w) // 2
    n_kv = KVW // HEAD_DIM
    F = w_ffn_gate.shape[-1]
    lora_w = 2 * DL + 2 * IL
    assert shift_w == 3 * C + lora_w + GL and lora_w <= LORA_PAD and GL <= LORA_PAD
    assert S % BLOCK == 0 and C % LANES == 0 and n_heads * HEAD_DIM == AW
    Lp = S + PAD_ROWS

    off_q = 2 * D
    off_r = off_q + AW + 2 * KVW
    off_lo = off_r + 3 * C
    tn_in = 512
    n_in_pad = -(-w_in.shape[-1] // tn_in) * tn_in
    assert n_in_pad >= off_lo + 2 * LORA_PAD and off_lo % LORA_PAD == 0
    low_w = lora_w + GL
    gl_a = LORA_PAD - lora_w

    mu_p, mu_n = mu_prev[0], mu_next[0]
    zc = jnp.zeros((C,), F32)
    pc = jnp.stack([mu_p[:C], mu_n[:C], mu_p[C:2 * C], mu_n[C:2 * C], mu_p[2 * C:3 * C], mu_n[2 * C:3 * C],
                    dec_w0[0, 0], dec_w0[0, 1], iclr_a0[0, 0], iclr_a0[0, 1], k_k[0], k_a[0], r_k[0], zc, zc, zc])
    plo = jnp.zeros((8, 2 * LORA_PAD), F32).at[0, :low_w].set(mu_p[3 * C:]).at[1, :low_w].set(mu_n[3 * C:])
    wl = jnp.zeros((4, LORA_PAD, C), F32)
    wl = wl.at[0, 0:DL].set(dec_w2[0, 0]).at[1, DL:2 * DL].set(dec_w2[0, 1])
    wl = wl.at[2, 2 * DL:2 * DL + IL].set(iclr_a2[0, 0]).at[3, 2 * DL + IL:lora_w].set(iclr_a2[0, 1])
    wl = wl.astype(BF16)
    g2 = jnp.zeros((2, LORA_PAD, C), F32).at[0, lora_w:].set(gate_w2[0, :gl_a]).at[1, :GL - gl_a].set(
        gate_w2[0, gl_a:]).astype(BF16)
    pq = jnp.zeros((8, C), F32).at[0].set(ln_x_w[0]).at[1].set(ln_x_b[0])
    lane = jnp.arange(LANES)
    ones_bd = (lane[:, None] // RWKV_HEAD == lane[None, :] // RWKV_HEAD).astype(BF16)

    w_d = w_ffn_down[0].astype(BF16)

    pos = jnp.concatenate([jnp.arange(N_META, N_META + S), jnp.zeros((PAD_ROWS - N_META,), jnp.int32),
                           jnp.arange(N_META)]).astype(F32)
    inv = ROPE_THETA ** (-jnp.arange(0, ROPE_DIMS, 2, dtype=F32) / ROPE_DIMS)
    ang = pos[:, None] * inv[None, :]
    ones_rest = jnp.ones((Lp, HEAD_DIM - ROPE_DIMS), F32)
    cos_t = jnp.concatenate([jnp.cos(ang), jnp.cos(ang), ones_rest], axis=1)
    sin_t = jnp.concatenate([-jnp.sin(ang), jnp.sin(ang), 0.0 * ones_rest], axis=1)

    x2 = x[0]
    h_ext = _norm_ext(x2, meta_tokens.astype(x.dtype), norm_mix_w[0])
    z = _mm_in(h_ext, w_in[0], _pick(Lp, (1664, 640)), tn_in)
    o_att = _attention_res(z, cos_t, sin_t, attn_sink[0], S, n_heads, n_kv, off_q)
    r_s, v_s, kk, lw_f, lw_b, kd_f, kd_b, bd_f, bd_b, g, bonus = _rwkv_prep(
        z, off_r, C, pc, plo, wl, g2, ones_bd, 2 * DL)
    y_f, y_b = _rwkv_scan(r_s, v_s, kk, lw_f, lw_b, kd_f, kd_b, bd_f, bd_b, S)
    o_rwkv = _rwkv_post(y_f, y_b, bonus, g, pq, ones_bd, S)
    tm = _pick(S, (1024, 512))
    mixed = _mm_mix(o_att, w_proj_attn[0], o_rwkv, w_proj_rwkv[0], z, b_gate[0], S, D, tm, 512)
    h1 = _mm_resw(mixed, w_out[0], x2, tm, 512)
    hn = _norm(h1, norm_ffn_w[0], BF16)
    act = _mm_glu(hn, w_ffn_gate[0], w_ffn_up[0], _pick(S, (2048, 512)), 256)
    h2 = _mm_res(act, w_d, h1, tm, 512, F // 2)
    y = _norm(h2, norm_final_w, x.dtype)
    return y[None]
```

```python
import functools
import math

import jax
import jax.numpy as jnp
from jax import lax
from jax.experimental import pallas as pl
from jax.experimental.pallas import tpu as pltpu

F32 = jnp.float32
BF16 = jnp.bfloat16

N_META = 16
HEAD_DIM = 128
WINDOW = 128
BLOCK = 128
ROPE_DIMS = HEAD_DIM // 4
ROPE_THETA = 500000.0
RWKV_HEAD = 64
RMS_EPS = 1e-6
GN_EPS = 64e-5
LANES = 128
PAD_ROWS = BLOCK
CHUNK = 64
LORA_PAD = 512
VMEM_PHYSICAL = 64 * 1024 * 1024

NN = (((1,), (0,)), ((), ()))
NT = (((1,), (1,)), ((), ()))
TN = (((0,), (0,)), ((), ()))


def _vmem_limit(*block_bytes, temps=0):
    need = 2 * sum(block_bytes) + temps + (4 << 20)
    return int(min(max(need, 16 << 20), VMEM_PHYSICAL - (6 << 20)))


def _params(sem, limit):
    return pltpu.CompilerParams(dimension_semantics=sem, vmem_limit_bytes=limit)


def _pick(n, cands):
    for c in cands:
        if n % c == 0:
            return c
    raise ValueError(f"no tile in {cands} divides {n}")


def _rms(x, w):
    return x * lax.rsqrt(jnp.mean(x * x, axis=-1, keepdims=True) + RMS_EPS) * w


def _norm_ext_kernel(x_ref, meta_ref, w_ref, o_ref, *, n_steps, real_last):
    i = pl.program_id(0)
    tr = o_ref.shape[0]

    @pl.when(i < n_steps - 1)
    def _():
        o_ref[...] = _rms(x_ref[...], w_ref[...]).astype(o_ref.dtype)

    @pl.when(i == n_steps - 1)
    def _():
        if real_last:
            o_ref[0:real_last, :] = _rms(x_ref[0:real_last, :], w_ref[...]).astype(o_ref.dtype)
        o_ref[real_last:tr - N_META, :] = jnp.zeros((tr - N_META - real_last, o_ref.shape[1]), o_ref.dtype)
        o_ref[tr - N_META:, :] = _rms(meta_ref[...], w_ref[...]).astype(o_ref.dtype)


def _norm_ext(x, meta, w):
    S, D = x.shape
    Lp = S + PAD_ROWS
    tr = _pick(Lp, (640, BLOCK))
    n_steps = Lp // tr
    real_last = S - (n_steps - 1) * tr
    last_x_blk = (S - 1) // tr
    return pl.pallas_call(
        functools.partial(_norm_ext_kernel, n_steps=n_steps, real_last=real_last),
        grid=(n_steps,),
        in_specs=[pl.BlockSpec((tr, D), lambda i: (jnp.minimum(i, last_x_blk), 0)),
                  pl.BlockSpec((N_META, D), lambda i: (0, 0)),
                  pl.BlockSpec((1, D), lambda i: (0, 0))],
        out_specs=pl.BlockSpec((tr, D), lambda i: (i, 0)),
        out_shape=jax.ShapeDtypeStruct((Lp, D), BF16),
        compiler_params=_params(("arbitrary",), _vmem_limit(tr * D * 4, tr * D * 2, temps=2 * tr * D * 4)),
        name="norm_ext",
    )(x, meta, w.reshape(1, D))


def _norm_kernel(x_ref, w_ref, o_ref):
    o_ref[...] = _rms(x_ref[...], w_ref[...]).astype(o_ref.dtype)


def _norm(x, w, out_dtype, tr=512):
    S, D = x.shape
    return pl.pallas_call(
        _norm_kernel,
        grid=(S // tr,),
        in_specs=[pl.BlockSpec((tr, D), lambda i: (i, 0)), pl.BlockSpec((1, D), lambda i: (0, 0))],
        out_specs=pl.BlockSpec((tr, D), lambda i: (i, 0)),
        out_shape=jax.ShapeDtypeStruct((S, D), out_dtype),
        compiler_params=_params(("arbitrary",), _vmem_limit(tr * D * 4, tr * D * 4, temps=2 * tr * D * 4)),
        name="norm",
    )(x, w.reshape(1, D))


def _dot(a, b, dims=NN):
    return lax.dot_general(a, b, dims, preferred_element_type=F32)


def _cast_resident(src_ref, dst_ref, col0=None, n_valid=None):
    K = src_ref.shape[0]
    rows = _pick(K, (256, 128))

    def body(c, carry):
        sl = pl.ds(pl.multiple_of(c * rows, rows), rows)
        blk = src_ref[sl, :]
        if n_valid is not None:
            col = col0 + lax.broadcasted_iota(jnp.int32, blk.shape, 1)
            blk = jnp.where(col < n_valid, blk, 0.0)
        dst_ref[sl, :] = blk.astype(dst_ref.dtype)
        return carry

    lax.fori_loop(0, K // rows, body, 0)


def _mm_in_kernel(a_ref, b_ref, o_ref, bw_ref, *, n_valid):
    j, i = pl.program_id(0), pl.program_id(1)

    @pl.when(i == 0)
    def _():
        _cast_resident(b_ref, bw_ref, j * b_ref.shape[1], n_valid)

    o_ref[...] = _dot(a_ref[...], bw_ref[...])


def _mm_in(a, w, tm, tn):
    M, K = a.shape
    N = w.shape[1]
    nj = pl.cdiv(N, tn)
    return pl.pallas_call(
        functools.partial(_mm_in_kernel, n_valid=N),
        grid=(nj, M // tm),
        in_specs=[pl.BlockSpec((tm, K), lambda j, i: (i, 0)), pl.BlockSpec((K, tn), lambda j, i: (0, j))],
        out_specs=pl.BlockSpec((tm, tn), lambda j, i: (i, j)),
        out_shape=jax.ShapeDtypeStruct((M, nj * tn), F32),
        scratch_shapes=[pltpu.VMEM((K, tn), BF16)],
        compiler_params=_params(("arbitrary", "arbitrary"),
                                _vmem_limit(tm * K * 2, K * tn * 4, tm * tn * 4, temps=K * tn * 2 + tm * tn * 4)),
        name="mm_in",
    )(a, w)


def _mm_mix_kernel(a1_ref, b1_ref, a2_ref, b2_ref, z0_ref, z1_ref, bg_ref, o_ref, w1_ref, w2_ref):
    @pl.when(pl.program_id(1) == 0)
    def _():
        _cast_resident(b1_ref, w1_ref)
        _cast_resident(b2_ref, w2_ref)

    g0 = jax.nn.sigmoid(z0_ref[...] + bg_ref[0:1, :])
    g1 = jax.nn.sigmoid(z1_ref[...] + bg_ref[1:2, :])
    o_ref[...] = (g0 * _dot(a1_ref[...], w1_ref[...]) + g1 * _dot(a2_ref[...], w2_ref[...])).astype(o_ref.dtype)


def _mm_mix(o_att, w_pa, o_rwkv, w_pr, z, b_gate, S, D, tm, tn):
    K1, K2 = o_att.shape[1], o_rwkv.shape[1]
    nj = D // tn
    return pl.pallas_call(
        _mm_mix_kernel,
        grid=(nj, S // tm),
        in_specs=[pl.BlockSpec((tm, K1), lambda j, i: (i, 0)), pl.BlockSpec((K1, tn), lambda j, i: (0, j)),
                  pl.BlockSpec((tm, K2), lambda j, i: (i, 0)), pl.BlockSpec((K2, tn), lambda j, i: (0, j)),
                  pl.BlockSpec((tm, tn), lambda j, i: (i, j)), pl.BlockSpec((tm, tn), lambda j, i: (i, nj + j)),
                  pl.BlockSpec((2, tn), lambda j, i: (0, j))],
        out_specs=pl.BlockSpec((tm, tn), lambda j, i: (i, j)),
        out_shape=jax.ShapeDtypeStruct((S, D), BF16),
        scratch_shapes=[pltpu.VMEM((K1, tn), BF16), pltpu.VMEM((K2, tn), BF16)],
        compiler_params=_params(("arbitrary", "arbitrary"),
                                _vmem_limit(tm * K1 * 2, K1 * tn * 4, tm * K2 * 2, K2 * tn * 4,
                                            2 * tm * tn * 4, tm * tn * 2,
                                            temps=(K1 + K2) * tn * 2 + 3 * tm * tn * 4)),
        name="mm_mix",
    )(o_att, w_pa, o_rwkv, w_pr, z, z, b_gate)


def _mm_resw_kernel(a_ref, b_ref, r_ref, o_ref, bw_ref):
    @pl.when(pl.program_id(1) == 0)
    def _():
        _cast_resident(b_ref, bw_ref)

    o_ref[...] = r_ref[...] + _dot(a_ref[...], bw_ref[...])


def _mm_resw(a, w, res, tm, tn):
    M, K = a.shape
    N = w.shape[1]
    return pl.pallas_call(
        _mm_resw_kernel,
        grid=(N // tn, M // tm),
        in_specs=[pl.BlockSpec((tm, K), lambda j, i: (i, 0)), pl.BlockSpec((K, tn), lambda j, i: (0, j)),
                  pl.BlockSpec((tm, tn), lambda j, i: (i, j))],
        out_specs=pl.BlockSpec((tm, tn), lambda j, i: (i, j)),
        out_shape=jax.ShapeDtypeStruct((M, N), F32),
        scratch_shapes=[pltpu.VMEM((K, tn), BF16)],
        compiler_params=_params(("arbitrary", "arbitrary"),
                                _vmem_limit(tm * K * 2, K * tn * 4, 2 * tm * tn * 4, temps=K * tn * 2 + tm * tn * 4)),
        name="mm_out",
    )(a, w, res)


def _mm_res_kernel(a_ref, b_ref, r_ref, o_ref, *, nk):
    k = pl.program_id(2)
    part = _dot(a_ref[...], b_ref[...])
    if nk == 1:
        o_ref[...] = r_ref[...] + part
    else:
        @pl.when(k == 0)
        def _():
            o_ref[...] = r_ref[...] + part

        @pl.when(k > 0)
        def _():
            o_ref[...] += part


def _mm_res(a, b, res, tm, tn, tk):
    M, K = a.shape
    N = b.shape[1]
    nk = K // tk
    return pl.pallas_call(
        functools.partial(_mm_res_kernel, nk=nk),
        grid=(M // tm, N // tn, nk),
        in_specs=[pl.BlockSpec((tm, tk), lambda i, j, k: (i, k)), pl.BlockSpec((tk, tn), lambda i, j, k: (k, j)),
                  pl.BlockSpec((tm, tn), lambda i, j, k: (i, j))],
        out_specs=pl.BlockSpec((tm, tn), lambda i, j, k: (i, j)),
        out_shape=jax.ShapeDtypeStruct((M, N), F32),
        compiler_params=_params(("arbitrary", "arbitrary", "arbitrary"),
                                _vmem_limit(tm * tk * 2, tk * tn * 2, 2 * tm * tn * 4, temps=tm * tn * 4)),
        name="mm_down",
    )(a, b, res)


def _mm_glu_kernel(a_ref, bg_ref, bu_ref, o_ref, wg_ref, wu_ref):
    @pl.when(pl.program_id(1) == 0)
    def _():
        _cast_resident(bg_ref, wg_ref)
        _cast_resident(bu_ref, wu_ref)

    a = a_ref[...]
    gate = _dot(a, wg_ref[...])
    up = _dot(a, wu_ref[...])
    o_ref[...] = (gate * jax.nn.sigmoid(gate) * up).astype(o_ref.dtype)


def _mm_glu(a, wg, wu, tm, tn):
    M, K = a.shape
    N = wg.shape[1]
    return pl.pallas_call(
        _mm_glu_kernel,
        grid=(N // tn, M // tm),
        in_specs=[pl.BlockSpec((tm, K), lambda j, i: (i, 0)), pl.BlockSpec((K, tn), lambda j, i: (0, j)),
                  pl.BlockSpec((K, tn), lambda j, i: (0, j))],
        out_specs=pl.BlockSpec((tm, tn), lambda j, i: (i, j)),
        out_shape=jax.ShapeDtypeStruct((M, N), BF16),
        scratch_shapes=[pltpu.VMEM((K, tn), BF16), pltpu.VMEM((K, tn), BF16)],
        compiler_params=_params(("arbitrary", "arbitrary"),
                                _vmem_limit(tm * K * 2, 2 * K * tn * 4, tm * tn * 2,
                                            temps=2 * K * tn * 2 + 3 * tm * tn * 4)),
        name="mm_glu",
    )(a, wg, wu)


def _rope_kernel(z_ref, cos_ref, sin_ref, o_ref, *, n_rope_blocks, heads_per_block, k_heads_last):
    j = pl.program_id(1)
    cos = cos_ref[...]
    sin = sin_ref[...]
    lane = lax.broadcasted_iota(jnp.int32, cos.shape, 1)
    half = ROPE_DIMS // 2
    for h in range(heads_per_block):
        t = z_ref[:, h * HEAD_DIM:(h + 1) * HEAD_DIM]
        up = jnp.concatenate([t[:, half:], t[:, :half]], axis=1)
        down = jnp.concatenate([t[:, HEAD_DIM - half:], t[:, :HEAD_DIM - half]], axis=1)
        partner = jnp.where(lane < half, up, down)
        rot = t * cos + partner * sin
        if h >= k_heads_last:
            rot = jnp.where(j == n_rope_blocks - 1, t, rot)
        o_ref[:, h * HEAD_DIM:(h + 1) * HEAD_DIM] = rot.astype(o_ref.dtype)


def _rope_cast(z, cos_t, sin_t, off_q, width, kv_width):
    Lp = z.shape[0]
    tc = 2 * kv_width
    assert off_q % tc == 0 and width % tc == 0
    nblk = width // tc
    return pl.pallas_call(
        functools.partial(_rope_kernel, n_rope_blocks=nblk, heads_per_block=tc // HEAD_DIM,
                          k_heads_last=kv_width // HEAD_DIM),
        grid=(Lp // BLOCK, nblk),
        in_specs=[pl.BlockSpec((BLOCK, tc), lambda i, j: (i, off_q // tc + j)),
                  pl.BlockSpec((BLOCK, HEAD_DIM), lambda i, j: (i, 0)),
                  pl.BlockSpec((BLOCK, HEAD_DIM), lambda i, j: (i, 0))],
        out_specs=pl.BlockSpec((BLOCK, tc), lambda i, j: (i, j)),
        out_shape=jax.ShapeDtypeStruct((Lp, width), BF16),
        compiler_params=_params(("arbitrary", "arbitrary"), _vmem_limit(BLOCK * tc * 4, BLOCK * tc * 2)),
        name="rope_cast",
    )(z, cos_t, sin_t)


NEG_BIG = -1e30


def _attn_kernel(q_ref, kp_ref, kc_ref, kn_ref, km_ref, vp_ref, vc_ref, vn_ref, vm_ref, sink_ref, o_ref,
                 *, nb, group):
    i = pl.program_id(1)
    scale = HEAD_DIM ** -0.5
    zpad = jnp.zeros((BLOCK - N_META, HEAD_DIM), BF16)
    k_all = jnp.concatenate([kp_ref[...], kc_ref[...], kn_ref[...], km_ref[...], zpad], axis=0)
    v_all = jnp.concatenate([vp_ref[...], vc_ref[...], vn_ref[...], vm_ref[...], zpad], axis=0)
    q_all = jnp.concatenate([q_ref[:, g * HEAD_DIM:(g + 1) * HEAD_DIM] for g in range(group)], axis=0)
    s = _dot(q_all, k_all, NT) * scale
    rows = lax.broadcasted_iota(jnp.int32, s.shape, 0) % BLOCK
    cols = lax.broadcasted_iota(jnp.int32, s.shape, 1)
    rel = cols - BLOCK - rows
    kblk = i - 1 + cols // BLOCK
    band_ok = (cols < 3 * BLOCK) & (kblk >= 0) & (kblk < nb) & (jnp.abs(rel) <= WINDOW)
    meta_ok = (cols >= 3 * BLOCK) & (cols < 3 * BLOCK + N_META)
    s = jnp.where(band_ok | meta_ok, s, NEG_BIG)
    sink = sink_ref[:, 0:1]
    m = jnp.maximum(jnp.max(s, axis=-1, keepdims=True), sink)
    p = jnp.exp(s - m)
    denom = jnp.sum(p, axis=-1, keepdims=True) + jnp.exp(sink - m)
    o = _dot(p.astype(BF16), v_all) / denom
    for g in range(group):
        o_ref[:, g * HEAD_DIM:(g + 1) * HEAD_DIM] = o[g * BLOCK:(g + 1) * BLOCK, :].astype(o_ref.dtype)


def _attention(qkv, sink, S, n_heads, n_kv):
    group = n_heads // n_kv
    nb = S // BLOCK
    qw = group * HEAD_DIM
    kcol = n_heads
    vcol = n_heads + n_kv
    meta_blk = (S + PAD_ROWS - N_META) // N_META
    sink_rows = jnp.broadcast_to(sink.astype(F32).reshape(n_kv, group, 1, 1),
                                 (n_kv, group, BLOCK, LANES)).reshape(n_kv, group * BLOCK, LANES)

    def kv_specs(col0):
        return [pl.BlockSpec((BLOCK, HEAD_DIM), lambda h, i: (jnp.maximum(i - 1, 0), col0 + h)),
                pl.BlockSpec((BLOCK, HEAD_DIM), lambda h, i: (i, col0 + h)),
                pl.BlockSpec((BLOCK, HEAD_DIM), lambda h, i: (jnp.minimum(i + 1, nb - 1), col0 + h)),
                pl.BlockSpec((N_META, HEAD_DIM), lambda h, i: (meta_blk, col0 + h))]

    return pl.pallas_call(
        functools.partial(_attn_kernel, nb=nb, group=group),
        grid=(n_kv, nb),
        in_specs=[pl.BlockSpec((BLOCK, qw), lambda h, i: (i, h))] + kv_specs(kcol) + kv_specs(vcol)
                 + [pl.BlockSpec((None, group * BLOCK, LANES), lambda h, i: (h, 0, 0))],
        out_specs=pl.BlockSpec((BLOCK, qw), lambda h, i: (i, h)),
        out_shape=jax.ShapeDtypeStruct((S, n_heads * HEAD_DIM), BF16),
        compiler_params=_params(("arbitrary", "arbitrary"),
                                _vmem_limit(BLOCK * qw * 2 * 2, 8 * BLOCK * HEAD_DIM * 2,
                                            group * BLOCK * LANES * 4, temps=6 * group * BLOCK * 512 * 4)),
        name="attention",
    )(qkv, qkv, qkv, qkv, qkv, qkv, qkv, qkv, qkv, sink_rows)


def _rope(t, cos, sin):
    half = ROPE_DIMS // 2
    lane = lax.broadcasted_iota(jnp.int32, t.shape, 1)
    up = jnp.concatenate([t[:, half:], t[:, :half]], axis=1)
    down = jnp.concatenate([t[:, HEAD_DIM - half:], t[:, :HEAD_DIM - half]], axis=1)
    return t * cos + jnp.where(lane < half, up, down) * sin


def _attn_res_kernel(sink_ref, q_ref, k_ref, v_ref, cos_ref, sin_ref, o_ref, kb_ref, vb_ref,
                     *, nb, group, qblocks):
    h, i = pl.program_id(0), pl.program_id(1)
    lp = k_ref.shape[0]

    @pl.when(i == 0)
    def _():
        def body(c, carry):
            sl = pl.ds(pl.multiple_of(c * BLOCK, BLOCK), BLOCK)
            kb_ref[sl, :] = _rope(k_ref[sl, :], cos_ref[sl, :], sin_ref[sl, :]).astype(BF16)
            vb_ref[sl, :] = v_ref[sl, :].astype(BF16)
            return carry
        lax.fori_loop(0, lp // BLOCK, body, 0)

    zpad = jnp.zeros((BLOCK - N_META, HEAD_DIM), BF16)
    k_meta = jnp.concatenate([kb_ref[lp - N_META:lp, :], zpad], axis=0)
    v_meta = jnp.concatenate([vb_ref[lp - N_META:lp, :], zpad], axis=0)
    rows = lax.broadcasted_iota(jnp.int32, (BLOCK, 4 * BLOCK), 0)
    cols = lax.broadcasted_iota(jnp.int32, (BLOCK, 4 * BLOCK), 1)
    is_meta = (cols >= 3 * BLOCK) & (cols < 3 * BLOCK + N_META)
    scale = HEAD_DIM ** -0.5
    k_all, v_all, bias, q, sink = [], [], [], [], []
    for b in range(qblocks):
        qi = i * qblocks + b
        kb0 = jnp.clip(qi - 1, 0, nb - 3)
        win = pl.ds(pl.multiple_of(kb0 * BLOCK, BLOCK), 3 * BLOCK)
        k_b = jnp.concatenate([kb_ref[win, :], k_meta], axis=0)
        v_b = jnp.concatenate([vb_ref[win, :], v_meta], axis=0)
        rel = cols - rows + (kb0 - qi) * BLOCK
        ok = ((cols < 3 * BLOCK) & (jnp.abs(rel) <= WINDOW)) | is_meta
        bias_b = jnp.where(ok, 0.0, NEG_BIG).astype(F32)
        qrows = pl.ds(pl.multiple_of(qi * BLOCK, BLOCK), BLOCK)
        cos_q, sin_q = cos_ref[qrows, :], sin_ref[qrows, :]
        for g in range(group):
            k_all.append(k_b)
            v_all.append(v_b)
            bias.append(bias_b)
            q.append(_rope(q_ref[b * BLOCK:(b + 1) * BLOCK, g * HEAD_DIM:(g + 1) * HEAD_DIM],
                           cos_q, sin_q).astype(BF16))
            sink.append(sink_ref[h * group + g])
    n = range(len(q))
    s = [_dot(q[c], k_all[c], NT) * scale + bias[c] for c in n]
    m = [jnp.maximum(jnp.max(s[c], axis=-1, keepdims=True), sink[c]) for c in n]
    p = [jnp.exp(s[c] - m[c]) for c in n]
    denom = [jnp.sum(p[c], axis=-1, keepdims=True) + jnp.exp(sink[c] - m[c]) for c in n]
    o = [_dot(p[c].astype(BF16), v_all[c]) / denom[c] for c in n]
    for c in n:
        b, g = divmod(c, group)
        o_ref[b * BLOCK:(b + 1) * BLOCK, g * HEAD_DIM:(g + 1) * HEAD_DIM] = o[c].astype(o_ref.dtype)


def _attention_res(z, cos_t, sin_t, sink, S, n_heads, n_kv, off_q):
    Lp = z.shape[0]
    group = n_heads // n_kv
    nb = S // BLOCK
    assert nb >= 3
    qw = group * HEAD_DIM
    kcol = off_q // HEAD_DIM + n_heads
    vcol = kcol + n_kv
    col_bytes = Lp * HEAD_DIM * 4
    qblocks = _pick(nb, (2, 1))
    tq = qblocks * BLOCK
    return pl.pallas_call(
        functools.partial(_attn_res_kernel, nb=nb, group=group, qblocks=qblocks),
        grid_spec=pltpu.PrefetchScalarGridSpec(
            num_scalar_prefetch=1,
            grid=(n_kv, nb // qblocks),
            in_specs=[pl.BlockSpec((tq, qw), lambda h, i, s: (i, off_q // qw + h)),
                      pl.BlockSpec((Lp, HEAD_DIM), lambda h, i, s: (0, kcol + h)),
                      pl.BlockSpec((Lp, HEAD_DIM), lambda h, i, s: (0, vcol + h)),
                      pl.BlockSpec((Lp, HEAD_DIM), lambda h, i, s: (0, 0)),
                      pl.BlockSpec((Lp, HEAD_DIM), lambda h, i, s: (0, 0))],
            out_specs=pl.BlockSpec((tq, qw), lambda h, i, s: (i, h)),
            scratch_shapes=[pltpu.VMEM((Lp, HEAD_DIM), BF16), pltpu.VMEM((Lp, HEAD_DIM), BF16)]),
        out_shape=jax.ShapeDtypeStruct((S, n_heads * HEAD_DIM), BF16),
        compiler_params=_params(("arbitrary", "arbitrary"),
                                _vmem_limit(tq * qw * 4, 4 * col_bytes, tq * qw * 2,
                                            temps=col_bytes + 8 * group * tq * 512 * 4)),
        name="attention",
    )(sink.astype(F32), z, z, z, cos_t, sin_t)


def _split3(x):
    hi = x.astype(BF16)
    r1 = x - hi.astype(F32)
    mid = r1.astype(BF16)
    lo = (r1 - mid.astype(F32)).astype(BF16)
    return hi, mid, lo


def _split2(x):
    hi = x.astype(BF16)
    return hi, (x - hi.astype(F32)).astype(BF16)


def _dot_exact_rhs(x, m_bf16, dims=NN, left=False):
    parts = _split3(x)
    if left:
        return sum(_dot(m_bf16, p, dims) for p in parts)
    return sum(_dot(p, m_bf16, dims) for p in parts)


def _dotp(a, b, passes, dims=NN):
    if passes == 1:
        return _dot(a.astype(BF16), b.astype(BF16), dims)
    a_hi, a_lo = _split2(a)
    b_hi, b_lo = _split2(b)
    return _dot(a_hi, b_hi, dims) + (_dot(a_hi, b_lo, dims) + _dot(a_lo, b_hi, dims))


def _head_sum(x, ones_bd):
    outs = []
    for s in range(x.shape[1] // LANES):
        outs.append(_dot_exact_rhs(x[:, s * LANES:(s + 1) * LANES], ones_bd))
    return outs[0] if len(outs) == 1 else jnp.concatenate(outs, axis=1)


def _prep_kernel(r_ref, rp_ref, rn_ref, k_ref, kp_ref, kn_ref, v_ref, vp_ref, vn_ref,
                 lo_ref, lop_ref, lon_ref, gd_ref, gdp_ref, gdn_ref,
                 pc_ref, pl_ref, wl_ref, g2_ref, ones_ref,
                 ro_ref, vo_ref, kko_ref, lwf_ref, lwb_ref, kdf_ref, kdb_ref, bdf_ref, bdb_ref, go_ref, bo_ref,
                 *, n_tiles, tr, lora_tanh_cols):
    i = pl.program_id(1)

    def shifted(main_ref, prev_ref, next_ref, mup, mun):
        x = main_ref[...]
        row8 = lax.broadcasted_iota(jnp.int32, (8, x.shape[1]), 0)
        down = pltpu.roll(x, 1, 0)
        up = pltpu.roll(x, tr - 1, 0)
        prev = jnp.concatenate([jnp.where(row8 == 0, prev_ref[7:8, :], down[0:8]), down[8:]], axis=0)
        nxt = jnp.concatenate([up[:tr - 8], jnp.where(row8 == 7, next_ref[0:1, :], up[tr - 8:])], axis=0)
        return x + mup * (prev - x) + mun * (nxt - x)

    r = shifted(r_ref, rp_ref, rn_ref, pc_ref[0:1, :], pc_ref[1:2, :])
    k = shifted(k_ref, kp_ref, kn_ref, pc_ref[2:3, :], pc_ref[3:4, :])
    v = shifted(v_ref, vp_ref, vn_ref, pc_ref[4:5, :], pc_ref[5:6, :])
    lo = shifted(lo_ref, lop_ref, lon_ref, pl_ref[0:1, 0:LORA_PAD], pl_ref[1:2, 0:LORA_PAD])
    gd = shifted(gd_ref, gdp_ref, gdn_ref, pl_ref[0:1, LORA_PAD:], pl_ref[1:2, LORA_PAD:])

    row = lax.broadcasted_iota(jnp.int32, (tr, 1), 0)
    valid = jnp.where((i < n_tiles - 1) | (row >= tr - N_META), 1.0, 0.0).astype(F32)

    lane = lax.broadcasted_iota(jnp.int32, lo.shape, 1)
    lo_act = jnp.where(lane < lora_tanh_cols, jnp.tanh(lo), lo).astype(BF16)
    dec_f = _dot(lo_act, wl_ref[0])
    dec_b = _dot(lo_act, wl_ref[1])
    apre_f = _dot(lo_act, wl_ref[2])
    apre_b = _dot(lo_act, wl_ref[3])
    g = _dot(jax.nn.sigmoid(lo).astype(BF16), g2_ref[0]) + _dot(jax.nn.sigmoid(gd).astype(BF16), g2_ref[1])

    def log_decay(dec, w0):
        return -math.exp(-0.5) * jax.nn.sigmoid(w0 + dec)

    a_f = jax.nn.sigmoid(pc_ref[8:9, :] + apre_f)
    a_b = jax.nn.sigmoid(pc_ref[9:10, :] + apre_b)
    kk = k * pc_ref[10:11, :]
    ss = _head_sum(kk * kk, ones_ref[...])
    kk = kk / jnp.maximum(jnp.sqrt(ss), 1e-12) * valid
    k_a = pc_ref[11:12, :]
    kv = k * valid

    rv = r * valid
    vv = v * valid
    kd_f = kv * (1.0 + (a_f - 1.0) * k_a)
    kd_b = kv * (1.0 + (a_b - 1.0) * k_a)
    ro_ref[...] = rv
    vo_ref[...] = vv
    kko_ref[...] = kk
    lwf_ref[...] = log_decay(dec_f, pc_ref[6:7, :])
    lwb_ref[...] = log_decay(dec_b, pc_ref[7:8, :])
    kdf_ref[...] = kd_f
    kdb_ref[...] = kd_b
    bdf_ref[...] = kk * a_f
    bdb_ref[...] = kk * a_b
    go_ref[...] = g
    bo_ref[...] = _head_sum(rv * (kd_f + kd_b) * pc_ref[12:13, :], ones_ref[...]) * vv


def _rwkv_prep(z, off_r, C, pc, plo, wl, g2, ones_bd, lora_tanh_cols):
    Lp = z.shape[0]
    tr = BLOCK
    ct = _pick(C, (1024, 512))
    n_tiles = Lp // tr
    n8 = Lp // 8
    off_lo = off_r + 3 * C
    off_gd = off_lo + LORA_PAD
    assert off_r % ct == 0 and off_lo % LORA_PAD == 0

    def seg_specs(off, w, with_c):
        cb = off // w

        def col(c):
            return cb + c if with_c else cb
        return [pl.BlockSpec((tr, w), lambda c, i: (i, col(c))),
                pl.BlockSpec((8, w), lambda c, i: ((i * (tr // 8) + n8 - 1) % n8, col(c))),
                pl.BlockSpec((8, w), lambda c, i: (((i + 1) * (tr // 8)) % n8, col(c)))]

    in_specs = (seg_specs(off_r, ct, True) + seg_specs(off_r + C, ct, True) + seg_specs(off_r + 2 * C, ct, True)
                + seg_specs(off_lo, LORA_PAD, False) + seg_specs(off_gd, LORA_PAD, False)
                + [pl.BlockSpec((16, ct), lambda c, i: (0, c)),
                   pl.BlockSpec((8, 2 * LORA_PAD), lambda c, i: (0, 0)),
                   pl.BlockSpec((4, LORA_PAD, ct), lambda c, i: (0, 0, c)),
                   pl.BlockSpec((2, LORA_PAD, ct), lambda c, i: (0, 0, c)),
                   pl.BlockSpec((LANES, LANES), lambda c, i: (0, 0))])
    out_spec = pl.BlockSpec((tr, ct), lambda c, i: (i, c))
    out_sds = jax.ShapeDtypeStruct((Lp, C), F32)
    return pl.pallas_call(
        functools.partial(_prep_kernel, n_tiles=n_tiles, tr=tr, lora_tanh_cols=lora_tanh_cols),
        grid=(C // ct, n_tiles),
        in_specs=in_specs,
        out_specs=[out_spec] * 11,
        out_shape=[out_sds] * 11,
        compiler_params=_params(("arbitrary", "arbitrary"),
                                _vmem_limit(3 * tr * ct * 4, 2 * tr * LORA_PAD * 4, 4 * LORA_PAD * ct * 2,
                                            2 * LORA_PAD * ct * 2, 11 * tr * ct * 4, temps=24 * tr * ct * 4)),
        name="rwkv_prep",
    )(*([z] * 15), pc, plo, wl, g2, ones_bd)


SCAN_PAIRS = 8


def _scan_consts():
    C = CHUNK
    row = lax.broadcasted_iota(jnp.int32, (C, 2 * C), 0)
    col = lax.broadcasted_iota(jnp.int32, (C, 2 * C), 1) % C
    rr = lax.broadcasted_iota(jnp.int32, (LANES, LANES), 0)
    cc = lax.broadcasted_iota(jnp.int32, (LANES, LANES), 1)
    return dict(
        row=lax.broadcasted_iota(jnp.int32, (C, LANES), 0),
        head0=lax.broadcasted_iota(jnp.int32, (C, LANES), 1) < RWKV_HEAD,
        strict_f=col < row, strict_b=col > row, incl_f=col <= row, incl_b=col >= row,
        eye_sbs=jnp.where(col == row, 1.0, 0.0).astype(F32),
        same_head=(rr // RWKV_HEAD) == (cc // RWKV_HEAD), eye=rr == cc)


def _scan_chains(chains, cst):
    C = CHUNK
    head0, same, eye = cst["head0"], cst["same_head"], cst["eye"]
    n = range(len(chains))
    rev = [c[7] for c in chains]
    r, v, al, lw, k, be, h = ([c[i] for c in chains] for i in range(7))
    strict = [cst["strict_b" if x else "strict_f"] for x in rev]
    incl = [cst["incl_b" if x else "incl_f"] for x in rev]

    def bd(x):
        xb = x.astype(BF16)
        zero = jnp.zeros_like(xb)
        return jnp.concatenate([jnp.where(head0, xb, zero), jnp.where(head0, zero, xb)], axis=0)

    def mm(a, b_bf16, dims=NN):
        return _dot(a.astype(BF16), b_bf16, dims)

    row = cst["row"]
    cl = list(lw)
    sh = 1
    while sh < C:
        cl = [cl[i] + (jnp.where(row < C - sh, pltpu.roll(cl[i], C - sh, 0), 0.0) if rev[i] else
                       jnp.where(row >= sh, pltpu.roll(cl[i], sh, 0), 0.0)) for i in n]
        sh *= 2
    total = [cl[i][0:1, :] if rev[i] else cl[i][C - 1:C, :] for i in n]
    a_t = [al[i] * jnp.exp(cl[i] - lw[i]) for i in n]
    r_t = [r[i] * jnp.exp(cl[i]) for i in n]
    w_inv = [jnp.exp(-cl[i]) for i in n]
    w_rest = [jnp.exp(total[i] - cl[i]) for i in n]
    bd_v = [bd(v[i]) for i in n]
    sc = [mm(jnp.concatenate([a_t[i], r_t[i]], axis=0),
             jnp.concatenate([bd(k[i] * w_inv[i]), bd(be[i] * w_inv[i])], axis=0), NT) for i in n]
    a_ak = [jnp.where(strict[i], sc[i][0:C, 0:2 * C], 0.0) for i in n]
    n_ab = [jnp.where(strict[i], sc[i][0:C, 2 * C:], 0.0) for i in n]
    a_rk = [jnp.where(incl[i], sc[i][C:, 0:2 * C], 0.0) for i in n]
    a_rb = [jnp.where(incl[i], sc[i][C:, 2 * C:], 0.0) for i in n]

    t_inv = [cst["eye_sbs"] - n_ab[i] for i in n]
    sq = [mm(n_ab[i], bd(n_ab[i])) for i in n]
    n_round = 2
    while (1 << n_round) < C:
        both = [mm(jnp.concatenate([t_inv[i], sq[i]], axis=0), bd(sq[i])) for i in n]
        t_inv = [t_inv[i] + both[i][0:C] for i in n]
        sq = [both[i][C:] for i in n]
        n_round += 1
    t_inv = [t_inv[i] + mm(t_inv[i], bd(sq[i])) for i in n]

    akv = [mm(jnp.concatenate([a_ak[i], a_rk[i]], axis=0), bd_v[i]) for i in n]
    tp = [mm(t_inv[i], jnp.concatenate([bd(akv[i][0:C]), bd(a_t[i])], axis=1)) for i in n]
    p0 = [tp[i][:, 0:LANES] for i in n]
    at = [tp[i][:, LANES:] for i in n]
    rb = [mm(a_rb[i], jnp.concatenate([bd(p0[i]), bd(at[i])], axis=1)) for i in n]
    y0 = [akv[i][C:] - rb[i][:, 0:LANES] for i in n]
    rh = [r_t[i] - rb[i][:, LANES:] for i in n]
    bt = [mm(be[i] * w_rest[i], jnp.concatenate([at[i], p0[i]], axis=1).astype(BF16), TN) for i in n]
    kv = [mm(k[i] * w_rest[i], v[i].astype(BF16), TN) for i in n]
    m_mat = [jnp.where(same, jnp.where(eye, jnp.broadcast_to(jnp.exp(total[i]), (LANES, LANES)), 0.0)
                       - bt[i][:, 0:LANES], 0.0) for i in n]
    g_mat = [jnp.where(same, kv[i] - bt[i][:, LANES:], 0.0) for i in n]
    yh = [mm(jnp.concatenate([rh[i], m_mat[i]], axis=0), h[i].astype(BF16)) for i in n]
    return [(yh[i][0:C] + y0[i], yh[i][C:] + g_mat[i]) for i in n]


def _scan_kernel(rf_ref, vf_ref, af_ref, lwf_ref, kf_ref, bf_ref,
                 rb_ref, vb_ref, ab_ref, lwb_ref, kb_ref, bb_ref,
                 yf_ref, yb_ref, hf_ref, hb_ref, *, pairs):
    @pl.when(pl.program_id(1) == 0)
    def _():
        hf_ref[...] = jnp.zeros(hf_ref.shape, F32)
        hb_ref[...] = jnp.zeros(hb_ref.shape, F32)

    cst = _scan_consts()
    chains = []
    for g in range(pairs):
        sl = slice(g * LANES, (g + 1) * LANES)
        chains.append((rf_ref[:, sl], vf_ref[:, sl], af_ref[:, sl], lwf_ref[:, sl], kf_ref[:, sl], bf_ref[:, sl],
                       hf_ref[g], False))
        chains.append((rb_ref[:, sl], vb_ref[:, sl], ab_ref[:, sl], lwb_ref[:, sl], kb_ref[:, sl], bb_ref[:, sl],
                       hb_ref[g], True))
    res = _scan_chains(chains, cst)
    for g in range(pairs):
        sl = slice(g * LANES, (g + 1) * LANES)
        yf_ref[:, sl], hf_ref[g] = res[2 * g]
        yb_ref[:, sl], hb_ref[g] = res[2 * g + 1]


def _rwkv_scan(r, v, kk, lw_f, lw_b, kd_f, kd_b, bd_f, bd_b, S):
    Lp, C = r.shape
    n_real = S // CHUNK
    n_chunks = Lp // CHUNK
    steps = n_real + 1
    pairs = _pick(C // LANES, (SCAN_PAIRS, 2, 1))
    w = pairs * LANES

    def fwd(p, s):
        return ((s + n_real + 1) % n_chunks, p)

    def bwd(p, s):
        return (n_real - s, p)

    spec_f = pl.BlockSpec((CHUNK, w), fwd)
    spec_b = pl.BlockSpec((CHUNK, w), bwd)
    out_sds = jax.ShapeDtypeStruct((S + CHUNK, C), F32)
    out_f = pl.BlockSpec((CHUNK, w), lambda p, s: ((s + n_real) % (n_real + 1), p))
    return pl.pallas_call(
        functools.partial(_scan_kernel, pairs=pairs),
        grid=(C // w, steps),
        in_specs=[spec_f] * 6 + [spec_b] * 6,
        out_specs=[out_f, spec_b],
        out_shape=[out_sds, out_sds],
        scratch_shapes=[pltpu.VMEM((pairs, LANES, LANES), F32), pltpu.VMEM((pairs, LANES, LANES), F32)],
        compiler_params=_params(("arbitrary", "arbitrary"), 32 << 20),
        name="rwkv_scan",
    )(r, v, kk, lw_f, kd_f, bd_f, r, v, kk, lw_b, kd_b, bd_b)


def _post_kernel(yf_ref, yb_ref, bonus_ref, g_ref, pq_ref, ones_ref, o_ref):
    ones_bd = ones_ref[...]
    inv_n = 1.0 / RWKV_HEAD
    y = yf_ref[...] + yb_ref[...]
    mean = _head_sum(y, ones_bd) * inv_n
    yc = y - mean
    var = _head_sum(yc * yc, ones_bd) * inv_n
    yn = yc * lax.rsqrt(var + GN_EPS) * pq_ref[0:1, :] + pq_ref[1:2, :]
    o_ref[...] = ((yn + bonus_ref[...]) * g_ref[...]).astype(o_ref.dtype)


def _rwkv_post(y_f, y_b, bonus, g, pq, ones_bd, S):
    C = g.shape[1]
    tr = 512
    ct = _pick(C, (1024, 512))
    spec = pl.BlockSpec((tr, ct), lambda i, c: (i, c))
    return pl.pallas_call(
        _post_kernel,
        grid=(S // tr, C // ct),
        in_specs=[spec] * 4 + [pl.BlockSpec((8, ct), lambda i, c: (0, c)),
                               pl.BlockSpec((LANES, LANES), lambda i, c: (0, 0))],
        out_specs=spec,
        out_shape=jax.ShapeDtypeStruct((S, C), BF16),
        compiler_params=_params(("arbitrary", "arbitrary"),
                                _vmem_limit(4 * tr * ct * 4, tr * ct * 2, temps=16 * tr * ct * 4)),
        name="rwkv_post",
    )(y_f, y_b, bonus, g, pq, ones_bd)


def _pad_cols(w, width):
    return jnp.pad(w, ((0, 0), (0, width - w.shape[1])))


def kernel(x, meta_tokens, norm_mix_w, w_in, b_gate, mu_prev, mu_next, dec_w0, dec_w2, iclr_a0, iclr_a2,
           gate_w2, k_k, k_a, r_k, ln_x_w, ln_x_b, attn_sink, w_proj_attn, w_proj_rwkv, w_out, norm_ffn_w,
           w_ffn_gate, w_ffn_up, w_ffn_down, norm_final_w):
    B, S, D = x.shape
    assert B == 1 and norm_mix_w.shape[0] == 1, "one sequence, one layer"
    C = k_k.shape[-1]
    AW = w_proj_attn.shape[1]
    n_heads = attn_sink.shape[-1]
    DL, IL, GL = dec_w2.shape[2], iclr_a2.shape[2], gate_w2.shape[1]
    shift_w = mu_prev.shape[-1]
    KVW = (w_in.shape[-1] - 2 * D - AW - shift_w) // 2
    n_kv = KVW // HEAD_DIM
    F = w_ffn_gate.shape[-1]
    lora_w = 2 * DL + 2 * IL
    assert shift_w == 3 * C + lora_w + GL and lora_w <= LORA_PAD and GL <= LORA_PAD
    assert S % BLOCK == 0 and C % LANES == 0 and n_heads * HEAD_DIM == AW
    Lp = S + PAD_ROWS

    off_q = 2 * D
    off_r = off_q + AW + 2 * KVW
    off_lo = off_r + 3 * C
    tn_in = 512
    n_in_pad = -(-w_in.shape[-1] // tn_in) * tn_in
    assert n_in_pad >= off_lo + 2 * LORA_PAD and off_lo % LORA_PAD == 0
    low_w = lora_w + GL
    gl_a = LORA_PAD - lora_w

    mu_p, mu_n = mu_prev[0], mu_next[0]
    zc = jnp.zeros((C,), F32)
    pc = jnp.stack([mu_p[:C], mu_n[:C], mu_p[C:2 * C], mu_n[C:2 * C], mu_p[2 * C:3 * C], mu_n[2 * C:3 * C],
                    dec_w0[0, 0], dec_w0[0, 1], iclr_a0[0, 0], iclr_a0[0, 1], k_k[0], k_a[0], r_k[0], zc, zc, zc])
    plo = jnp.zeros((8, 2 * LORA_PAD), F32).at[0, :low_w].set(mu_p[3 * C:]).at[1, :low_w].set(mu_n[3 * C:])
    wl = jnp.zeros((4, LORA_PAD, C), F32)
    wl = wl.at[0, 0:DL].set(dec_w2[0, 0]).at[1, DL:2 * DL].set(dec_w2[0, 1])
    wl = wl.at[2, 2 * DL:2 * DL + IL].set(iclr_a2[0, 0]).at[3, 2 * DL + IL:lora_w].set(iclr_a2[0, 1])
    wl = wl.astype(BF16)
    g2 = jnp.zeros((2, LORA_PAD, C), F32).at[0, lora_w:].set(gate_w2[0, :gl_a]).at[1, :GL - gl_a].set(
        gate_w2[0, gl_a:]).astype(BF16)
    pq = jnp.zeros((8, C), F32).at[0].set(ln_x_w[0]).at[1].set(ln_x_b[0])
    lane = jnp.arange(LANES)
    ones_bd = (lane[:, None] // RWKV_HEAD == lane[None, :] // RWKV_HEAD).astype(BF16)

    w_d = w_ffn_down[0].astype(BF16)

    pos = jnp.concatenate([jnp.arange(N_META, N_META + S), jnp.zeros((PAD_ROWS - N_META,), jnp.int32),
                           jnp.arange(N_META)]).astype(F32)
    inv = ROPE_THETA ** (-jnp.arange(0, ROPE_DIMS, 2, dtype=F32) / ROPE_DIMS)
    ang = pos[:, None] * inv[None, :]
    ones_rest = jnp.ones((Lp, HEAD_DIM - ROPE_DIMS), F32)
    cos_t = jnp.concatenate([jnp.cos(ang), jnp.cos(ang), ones_rest], axis=1)
    sin_t = jnp.concatenate([-jnp.sin(ang), jnp.sin(ang), 0.0 * ones_rest], axis=1)

    x2 = x[0]
    h_ext = _norm_ext(x2, meta_tokens.astype(x.dtype), norm_mix_w[0])
    z = _mm_in(h_ext, w_in[0], _pick(Lp, (1664, 640)), tn_in)
    o_att = _attention_res(z, cos_t, sin_t, attn_sink[0], S, n_heads, n_kv, off_q)
    r_s, v_s, kk, lw_f, lw_b, kd_f, kd_b, bd_f, bd_b, g, bonus = _rwkv_prep(
        z, off_r, C, pc, plo, wl, g2, ones_bd, 2 * DL)
    y_f, y_b = _rwkv_scan(r_s, v_s, kk, lw_f, lw_b, kd_f, kd_b, bd_f, bd_b, S)
    o_rwkv = _rwkv_post(y_f, y_b, bonus, g, pq, ones_bd, S)
    tm = _pick(S, (1024, 512))
    mixed = _mm_mix(o_att, w_proj_attn[0], o_rwkv, w_proj_rwkv[0], z, b_gate[0], S, D, tm, 512)
    h1 = _mm_resw(mixed, w_out[0], x2, tm, 512)
    hn = _norm(h1, norm_ffn_w[0], BF16)
    act = _mm_glu(hn, w_ffn_gate[0], w_ffn_up[0], _pick(S, (2048, 512)), 256)
    h2 = _mm_res(act, w_d, h1, tm, 512, F // 2)
    y = _norm(h2, norm_final_w, x.dtype)
    return y[None]
```

```python
import functools
import math

import jax
import jax.numpy as jnp
from jax import lax
from jax.experimental import pallas as pl
from jax.experimental.pallas import tpu as pltpu

F32 = jnp.float32
BF16 = jnp.bfloat16

N_META = 16
HEAD_DIM = 128
WINDOW = 128
BLOCK = 128
ROPE_DIMS = HEAD_DIM // 4
ROPE_THETA = 500000.0
RWKV_HEAD = 64
RMS_EPS = 1e-6
GN_EPS = 64e-5
LANES = 128
PAD_ROWS = BLOCK
CHUNK = 64
LORA_PAD = 512
VMEM_PHYSICAL = 64 * 1024 * 1024

NN = (((1,), (0,)), ((), ()))
NT = (((1,), (1,)), ((), ()))
TN = (((0,), (0,)), ((), ()))


def _vmem_limit(*block_bytes, temps=0):
    need = 2 * sum(block_bytes) + temps + (4 << 20)
    return int(min(max(need, 16 << 20), VMEM_PHYSICAL - (4 << 20)))


def _params(sem, limit):
    return pltpu.CompilerParams(dimension_semantics=sem, vmem_limit_bytes=limit)


def _pick(n, cands):
    for c in cands:
        if n % c == 0:
            return c
    raise ValueError(f"no tile in {cands} divides {n}")


def _rms(x, w):
    return x * lax.rsqrt(jnp.mean(x * x, axis=-1, keepdims=True) + RMS_EPS) * w


def _norm_ext_kernel(x_ref, meta_ref, w_ref, o_ref, *, n_steps, real_last):
    i = pl.program_id(0)
    tr = o_ref.shape[0]

    @pl.when(i < n_steps - 1)
    def _():
        o_ref[...] = _rms(x_ref[...], w_ref[...]).astype(o_ref.dtype)

    @pl.when(i == n_steps - 1)
    def _():
        if real_last:
            o_ref[0:real_last, :] = _rms(x_ref[0:real_last, :], w_ref[...]).astype(o_ref.dtype)
        o_ref[real_last:tr - N_META, :] = jnp.zeros((tr - N_META - real_last, o_ref.shape[1]), o_ref.dtype)
        o_ref[tr - N_META:, :] = _rms(meta_ref[...], w_ref[...]).astype(o_ref.dtype)


def _norm_ext(x, meta, w):
    S, D = x.shape
    Lp = S + PAD_ROWS
    tr = _pick(Lp, (640, BLOCK))
    n_steps = Lp // tr
    real_last = S - (n_steps - 1) * tr
    last_x_blk = (S - 1) // tr
    return pl.pallas_call(
        functools.partial(_norm_ext_kernel, n_steps=n_steps, real_last=real_last),
        grid=(n_steps,),
        in_specs=[pl.BlockSpec((tr, D), lambda i: (jnp.minimum(i, last_x_blk), 0)),
                  pl.BlockSpec((N_META, D), lambda i: (0, 0)),
                  pl.BlockSpec((1, D), lambda i: (0, 0))],
        out_specs=pl.BlockSpec((tr, D), lambda i: (i, 0)),
        out_shape=jax.ShapeDtypeStruct((Lp, D), BF16),
        compiler_params=_params(("arbitrary",), _vmem_limit(tr * D * 4, tr * D * 2, temps=2 * tr * D * 4)),
        name="norm_ext",
    )(x, meta, w.reshape(1, D))


def _norm_kernel(x_ref, w_ref, o_ref):
    o_ref[...] = _rms(x_ref[...], w_ref[...]).astype(o_ref.dtype)


def _norm(x, w, out_dtype, tr=512):
    S, D = x.shape
    return pl.pallas_call(
        _norm_kernel,
        grid=(S // tr,),
        in_specs=[pl.BlockSpec((tr, D), lambda i: (i, 0)), pl.BlockSpec((1, D), lambda i: (0, 0))],
        out_specs=pl.BlockSpec((tr, D), lambda i: (i, 0)),
        out_shape=jax.ShapeDtypeStruct((S, D), out_dtype),
        compiler_params=_params(("arbitrary",), _vmem_limit(tr * D * 4, tr * D * 4, temps=2 * tr * D * 4)),
        name="norm",
    )(x, w.reshape(1, D))


def _dot(a, b, dims=NN):
    return lax.dot_general(a, b, dims, preferred_element_type=F32)


def _cast_resident(src_ref, dst_ref, col0=None, n_valid=None):
    K = src_ref.shape[0]
    rows = _pick(K, (256, 128))

    def body(c, carry):
        sl = pl.ds(pl.multiple_of(c * rows, rows), rows)
        blk = src_ref[sl, :]
        if n_valid is not None:
            col = col0 + lax.broadcasted_iota(jnp.int32, blk.shape, 1)
            blk = jnp.where(col < n_valid, blk, 0.0)
        dst_ref[sl, :] = blk.astype(dst_ref.dtype)
        return carry

    lax.fori_loop(0, K // rows, body, 0)


def _mm_in_kernel(a_ref, b_ref, o_ref, bw_ref, *, n_valid):
    j, i = pl.program_id(0), pl.program_id(1)

    @pl.when(i == 0)
    def _():
        _cast_resident(b_ref, bw_ref, j * b_ref.shape[1], n_valid)

    o_ref[...] = _dot(a_ref[...], bw_ref[...])


def _mm_in(a, w, tm, tn):
    M, K = a.shape
    N = w.shape[1]
    nj = pl.cdiv(N, tn)
    return pl.pallas_call(
        functools.partial(_mm_in_kernel, n_valid=N),
        grid=(nj, M // tm),
        in_specs=[pl.BlockSpec((tm, K), lambda j, i: (i, 0)), pl.BlockSpec((K, tn), lambda j, i: (0, j))],
        out_specs=pl.BlockSpec((tm, tn), lambda j, i: (i, j)),
        out_shape=jax.ShapeDtypeStruct((M, nj * tn), F32),
        scratch_shapes=[pltpu.VMEM((K, tn), BF16)],
        compiler_params=_params(("arbitrary", "arbitrary"),
                                _vmem_limit(tm * K * 2, K * tn * 4, tm * tn * 4, temps=K * tn * 2 + tm * tn * 4)),
        name="mm_in",
    )(a, w)


def _mm_mix_kernel(a1_ref, b1_ref, a2_ref, b2_ref, z0_ref, z1_ref, bg_ref, o_ref, w1_ref, w2_ref):
    @pl.when(pl.program_id(1) == 0)
    def _():
        _cast_resident(b1_ref, w1_ref)
        _cast_resident(b2_ref, w2_ref)

    g0 = jax.nn.sigmoid(z0_ref[...] + bg_ref[0:1, :])
    g1 = jax.nn.sigmoid(z1_ref[...] + bg_ref[1:2, :])
    o_ref[...] = (g0 * _dot(a1_ref[...], w1_ref[...]) + g1 * _dot(a2_ref[...], w2_ref[...])).astype(o_ref.dtype)


def _mm_mix(o_att, w_pa, o_rwkv, w_pr, z, b_gate, S, D, tm, tn):
    K1, K2 = o_att.shape[1], o_rwkv.shape[1]
    nj = D // tn
    return pl.pallas_call(
        _mm_mix_kernel,
        grid=(nj, S // tm),
        in_specs=[pl.BlockSpec((tm, K1), lambda j, i: (i, 0)), pl.BlockSpec((K1, tn), lambda j, i: (0, j)),
                  pl.BlockSpec((tm, K2), lambda j, i: (i, 0)), pl.BlockSpec((K2, tn), lambda j, i: (0, j)),
                  pl.BlockSpec((tm, tn), lambda j, i: (i, j)), pl.BlockSpec((tm, tn), lambda j, i: (i, nj + j)),
                  pl.BlockSpec((2, tn), lambda j, i: (0, j))],
        out_specs=pl.BlockSpec((tm, tn), lambda j, i: (i, j)),
        out_shape=jax.ShapeDtypeStruct((S, D), BF16),
        scratch_shapes=[pltpu.VMEM((K1, tn), BF16), pltpu.VMEM((K2, tn), BF16)],
        compiler_params=_params(("arbitrary", "arbitrary"),
                                _vmem_limit(tm * K1 * 2, K1 * tn * 4, tm * K2 * 2, K2 * tn * 4,
                                            2 * tm * tn * 4, tm * tn * 2,
                                            temps=(K1 + K2) * tn * 2 + 3 * tm * tn * 4)),
        name="mm_mix",
    )(o_att, w_pa, o_rwkv, w_pr, z, z, b_gate)


def _mm_resw_kernel(a_ref, b_ref, r_ref, o_ref, bw_ref):
    @pl.when(pl.program_id(1) == 0)
    def _():
        _cast_resident(b_ref, bw_ref)

    o_ref[...] = r_ref[...] + _dot(a_ref[...], bw_ref[...])


def _mm_resw(a, w, res, tm, tn):
    M, K = a.shape
    N = w.shape[1]
    return pl.pallas_call(
        _mm_resw_kernel,
        grid=(N // tn, M // tm),
        in_specs=[pl.BlockSpec((tm, K), lambda j, i: (i, 0)), pl.BlockSpec((K, tn), lambda j, i: (0, j)),
                  pl.BlockSpec((tm, tn), lambda j, i: (i, j))],
        out_specs=pl.BlockSpec((tm, tn), lambda j, i: (i, j)),
        out_shape=jax.ShapeDtypeStruct((M, N), F32),
        scratch_shapes=[pltpu.VMEM((K, tn), BF16)],
        compiler_params=_params(("arbitrary", "arbitrary"),
                                _vmem_limit(tm * K * 2, K * tn * 4, 2 * tm * tn * 4, temps=K * tn * 2 + tm * tn * 4)),
        name="mm_out",
    )(a, w, res)


def _weight_copy(w_hbm, land_ref, sem, jj, tn, width):
    return pltpu.make_async_copy(w_hbm.at[:, pl.ds(pl.multiple_of(jj * tn, LANES), width)],
                                 land_ref.at[:, pl.ds(0, width)], sem)


def _weight_copy_op(op, w_hbm, land_ref, sem, jj, tn, nj, n_cols):
    rem = n_cols - (nj - 1) * tn
    if rem == tn:
        getattr(_weight_copy(w_hbm, land_ref, sem, jj, tn, tn), op)()
        return

    @pl.when(jj < nj - 1)
    def _():
        getattr(_weight_copy(w_hbm, land_ref, sem, jj, tn, tn), op)()

    @pl.when(jj == nj - 1)
    def _():
        getattr(_weight_copy(w_hbm, land_ref, sem, jj, tn, rem), op)()


def _stream_weights(weights, j, i, nj, tn, n_cols):
    @pl.when(i == 0)
    def _():
        @pl.when(j == 0)
        def _():
            for w_hbm, land_ref, _, sem in weights:
                _weight_copy_op("start", w_hbm, land_ref, sem, j, tn, nj, n_cols)

        for w_hbm, land_ref, bw_ref, sem in weights:
            _weight_copy_op("wait", w_hbm, land_ref, sem, j, tn, nj, n_cols)
            _cast_resident(land_ref, bw_ref, j * tn, n_cols if n_cols % tn else None)

        @pl.when(j + 1 < nj)
        def _():
            for w_hbm, land_ref, _, sem in weights:
                _weight_copy_op("start", w_hbm, land_ref, sem, j + 1, tn, nj, n_cols)


def _weight_scratch(ks, tn):
    return ([pltpu.VMEM((k, tn), F32) for k in ks] + [pltpu.VMEM((k, tn), BF16) for k in ks]
            + [pltpu.SemaphoreType.DMA(()) for _ in ks])


def _mm_in_s_kernel(a_ref, w_hbm, o_ref, land_ref, bw_ref, sem, *, nj, tn, n_cols):
    j, i = pl.program_id(0), pl.program_id(1)
    _stream_weights([(w_hbm, land_ref, bw_ref, sem)], j, i, nj, tn, n_cols)
    o_ref[...] = _dot(a_ref[...], bw_ref[...])


def _mm_in_s(a, w, tm, tn):
    M, K = a.shape
    N = w.shape[1]
    nj = pl.cdiv(N, tn)
    return pl.pallas_call(
        functools.partial(_mm_in_s_kernel, nj=nj, tn=tn, n_cols=N),
        grid=(nj, M // tm),
        in_specs=[pl.BlockSpec((tm, K), lambda j, i: (i, 0)), pl.BlockSpec(memory_space=pl.ANY)],
        out_specs=pl.BlockSpec((tm, tn), lambda j, i: (i, j)),
        out_shape=jax.ShapeDtypeStruct((M, nj * tn), F32),
        scratch_shapes=_weight_scratch((K,), tn),
        compiler_params=_params(("arbitrary", "arbitrary"),
                                _vmem_limit(tm * K * 2, tm * tn * 4, temps=K * tn * 6 + 2 * tm * tn * 4)),
        name="mm_in",
    )(a, w)


def _mm_mix_s_kernel(a1_ref, a2_ref, z0_ref, z1_ref, bg_ref, w1_hbm, w2_hbm, o_ref,
                     l1_ref, l2_ref, b1_ref, b2_ref, s1, s2, *, nj, tn, n_cols):
    j, i = pl.program_id(0), pl.program_id(1)
    _stream_weights([(w1_hbm, l1_ref, b1_ref, s1), (w2_hbm, l2_ref, b2_ref, s2)], j, i, nj, tn, n_cols)
    g0 = jax.nn.sigmoid(z0_ref[...] + bg_ref[0:1, :])
    g1 = jax.nn.sigmoid(z1_ref[...] + bg_ref[1:2, :])
    o_ref[...] = (g0 * _dot(a1_ref[...], b1_ref[...]) + g1 * _dot(a2_ref[...], b2_ref[...])).astype(o_ref.dtype)


def _mm_mix_s(o_att, w_pa, o_rwkv, w_pr, z, b_gate, S, D, tm, tn):
    K1, K2 = o_att.shape[1], o_rwkv.shape[1]
    assert D % tn == 0
    nj = D // tn
    return pl.pallas_call(
        functools.partial(_mm_mix_s_kernel, nj=nj, tn=tn, n_cols=D),
        grid=(nj, S // tm),
        in_specs=[pl.BlockSpec((tm, K1), lambda j, i: (i, 0)), pl.BlockSpec((tm, K2), lambda j, i: (i, 0)),
                  pl.BlockSpec((tm, tn), lambda j, i: (i, j)), pl.BlockSpec((tm, tn), lambda j, i: (i, nj + j)),
                  pl.BlockSpec((2, tn), lambda j, i: (0, j)),
                  pl.BlockSpec(memory_space=pl.ANY), pl.BlockSpec(memory_space=pl.ANY)],
        out_specs=pl.BlockSpec((tm, tn), lambda j, i: (i, j)),
        out_shape=jax.ShapeDtypeStruct((S, D), BF16),
        scratch_shapes=_weight_scratch((K1, K2), tn),
        compiler_params=_params(("arbitrary", "arbitrary"),
                                _vmem_limit(tm * (K1 + K2) * 2, 2 * tm * tn * 4, tm * tn * 2,
                                            temps=(K1 + K2) * tn * 6 + 3 * tm * tn * 4)),
        name="mm_mix",
    )(o_att, o_rwkv, z, z, b_gate, w_pa, w_pr)


def _mm_out_s_kernel(a_ref, r_ref, w_hbm, o_ref, land_ref, bw_ref, sem, *, nj, tn, n_cols):
    j, i = pl.program_id(0), pl.program_id(1)
    _stream_weights([(w_hbm, land_ref, bw_ref, sem)], j, i, nj, tn, n_cols)
    o_ref[...] = r_ref[...] + _dot(a_ref[...], bw_ref[...])


def _mm_out_s(a, w, res, tm, tn):
    M, K = a.shape
    N = w.shape[1]
    assert N % tn == 0
    nj = N // tn
    return pl.pallas_call(
        functools.partial(_mm_out_s_kernel, nj=nj, tn=tn, n_cols=N),
        grid=(nj, M // tm),
        in_specs=[pl.BlockSpec((tm, K), lambda j, i: (i, 0)), pl.BlockSpec((tm, tn), lambda j, i: (i, j)),
                  pl.BlockSpec(memory_space=pl.ANY)],
        out_specs=pl.BlockSpec((tm, tn), lambda j, i: (i, j)),
        out_shape=jax.ShapeDtypeStruct((M, N), F32),
        scratch_shapes=_weight_scratch((K,), tn),
        compiler_params=_params(("arbitrary", "arbitrary"),
                                _vmem_limit(tm * K * 2, 2 * tm * tn * 4, temps=K * tn * 6 + tm * tn * 4)),
        name="mm_out",
    )(a, res, w)


def _mm_glu_s_kernel(a_ref, wg_hbm, wu_hbm, o_ref, lg_ref, lu_ref, bg_ref, bu_ref, sg, su, *, nj, tn, n_cols):
    j, i = pl.program_id(0), pl.program_id(1)
    _stream_weights([(wg_hbm, lg_ref, bg_ref, sg), (wu_hbm, lu_ref, bu_ref, su)], j, i, nj, tn, n_cols)
    a = a_ref[...]
    gate = _dot(a, bg_ref[...])
    up = _dot(a, bu_ref[...])
    o_ref[...] = (gate * jax.nn.sigmoid(gate) * up).astype(o_ref.dtype)


def _mm_glu_s(a, wg, wu, tm, tn):
    M, K = a.shape
    N = wg.shape[1]
    nj = pl.cdiv(N, tn)
    return pl.pallas_call(
        functools.partial(_mm_glu_s_kernel, nj=nj, tn=tn, n_cols=N),
        grid=(nj, M // tm),
        in_specs=[pl.BlockSpec((tm, K), lambda j, i: (i, 0)),
                  pl.BlockSpec(memory_space=pl.ANY), pl.BlockSpec(memory_space=pl.ANY)],
        out_specs=pl.BlockSpec((tm, tn), lambda j, i: (i, j)),
        out_shape=jax.ShapeDtypeStruct((M, N), BF16),
        scratch_shapes=_weight_scratch((K, K), tn),
        compiler_params=_params(("arbitrary", "arbitrary"),
                                _vmem_limit(tm * K * 2, tm * tn * 2, temps=2 * K * tn * 6 + 3 * tm * tn * 4)),
        name="mm_glu",
    )(a, wg, wu)


def _mm_res_kernel(a_ref, b_ref, r_ref, o_ref, *, nk):
    k = pl.program_id(2)
    part = _dot(a_ref[...], b_ref[...])
    if nk == 1:
        o_ref[...] = r_ref[...] + part
    else:
        @pl.when(k == 0)
        def _():
            o_ref[...] = r_ref[...] + part

        @pl.when(k > 0)
        def _():
            o_ref[...] += part


def _mm_res(a, b, res, tm, tn, tk):
    M, K = a.shape
    N = b.shape[1]
    nk = K // tk
    return pl.pallas_call(
        functools.partial(_mm_res_kernel, nk=nk),
        grid=(M // tm, N // tn, nk),
        in_specs=[pl.BlockSpec((tm, tk), lambda i, j, k: (i, k)), pl.BlockSpec((tk, tn), lambda i, j, k: (k, j)),
                  pl.BlockSpec((tm, tn), lambda i, j, k: (i, j))],
        out_specs=pl.BlockSpec((tm, tn), lambda i, j, k: (i, j)),
        out_shape=jax.ShapeDtypeStruct((M, N), F32),
        compiler_params=_params(("arbitrary", "arbitrary", "arbitrary"),
                                _vmem_limit(tm * tk * 2, tk * tn * 2, 2 * tm * tn * 4, temps=tm * tn * 4)),
        name="mm_down",
    )(a, b, res)


def _mm_glu_kernel(a_ref, bg_ref, bu_ref, o_ref, wg_ref, wu_ref):
    @pl.when(pl.program_id(1) == 0)
    def _():
        _cast_resident(bg_ref, wg_ref)
        _cast_resident(bu_ref, wu_ref)

    a = a_ref[...]
    gate = _dot(a, wg_ref[...])
    up = _dot(a, wu_ref[...])
    o_ref[...] = (gate * jax.nn.sigmoid(gate) * up).astype(o_ref.dtype)


def _mm_glu(a, wg, wu, tm, tn):
    M, K = a.shape
    N = wg.shape[1]
    return pl.pallas_call(
        _mm_glu_kernel,
        grid=(N // tn, M // tm),
        in_specs=[pl.BlockSpec((tm, K), lambda j, i: (i, 0)), pl.BlockSpec((K, tn), lambda j, i: (0, j)),
                  pl.BlockSpec((K, tn), lambda j, i: (0, j))],
        out_specs=pl.BlockSpec((tm, tn), lambda j, i: (i, j)),
        out_shape=jax.ShapeDtypeStruct((M, N), BF16),
        scratch_shapes=[pltpu.VMEM((K, tn), BF16), pltpu.VMEM((K, tn), BF16)],
        compiler_params=_params(("arbitrary", "arbitrary"),
                                _vmem_limit(tm * K * 2, 2 * K * tn * 4, tm * tn * 2,
                                            temps=2 * K * tn * 2 + 3 * tm * tn * 4)),
        name="mm_glu",
    )(a, wg, wu)


def _rope_kernel(z_ref, cos_ref, sin_ref, o_ref, *, n_rope_blocks, heads_per_block, k_heads_last):
    j = pl.program_id(1)
    cos = cos_ref[...]
    sin = sin_ref[...]
    lane = lax.broadcasted_iota(jnp.int32, cos.shape, 1)
    half = ROPE_DIMS // 2
    for h in range(heads_per_block):
        t = z_ref[:, h * HEAD_DIM:(h + 1) * HEAD_DIM]
        up = jnp.concatenate([t[:, half:], t[:, :half]], axis=1)
        down = jnp.concatenate([t[:, HEAD_DIM - half:], t[:, :HEAD_DIM - half]], axis=1)
        partner = jnp.where(lane < half, up, down)
        rot = t * cos + partner * sin
        if h >= k_heads_last:
            rot = jnp.where(j == n_rope_blocks - 1, t, rot)
        o_ref[:, h * HEAD_DIM:(h + 1) * HEAD_DIM] = rot.astype(o_ref.dtype)


def _rope_cast(z, cos_t, sin_t, off_q, width, kv_width):
    Lp = z.shape[0]
    tc = 2 * kv_width
    assert off_q % tc == 0 and width % tc == 0
    nblk = width // tc
    return pl.pallas_call(
        functools.partial(_rope_kernel, n_rope_blocks=nblk, heads_per_block=tc // HEAD_DIM,
                          k_heads_last=kv_width // HEAD_DIM),
        grid=(Lp // BLOCK, nblk),
        in_specs=[pl.BlockSpec((BLOCK, tc), lambda i, j: (i, off_q // tc + j)),
                  pl.BlockSpec((BLOCK, HEAD_DIM), lambda i, j: (i, 0)),
                  pl.BlockSpec((BLOCK, HEAD_DIM), lambda i, j: (i, 0))],
        out_specs=pl.BlockSpec((BLOCK, tc), lambda i, j: (i, j)),
        out_shape=jax.ShapeDtypeStruct((Lp, width), BF16),
        compiler_params=_params(("arbitrary", "arbitrary"), _vmem_limit(BLOCK * tc * 4, BLOCK * tc * 2)),
        name="rope_cast",
    )(z, cos_t, sin_t)


NEG_BIG = -1e30


def _attn_kernel(q_ref, kp_ref, kc_ref, kn_ref, km_ref, vp_ref, vc_ref, vn_ref, vm_ref, sink_ref, o_ref,
                 *, nb, group):
    i = pl.program_id(1)
    scale = HEAD_DIM ** -0.5
    zpad = jnp.zeros((BLOCK - N_META, HEAD_DIM), BF16)
    k_all = jnp.concatenate([kp_ref[...], kc_ref[...], kn_ref[...], km_ref[...], zpad], axis=0)
    v_all = jnp.concatenate([vp_ref[...], vc_ref[...], vn_ref[...], vm_ref[...], zpad], axis=0)
    q_all = jnp.concatenate([q_ref[:, g * HEAD_DIM:(g + 1) * HEAD_DIM] for g in range(group)], axis=0)
    s = _dot(q_all, k_all, NT) * scale
    rows = lax.broadcasted_iota(jnp.int32, s.shape, 0) % BLOCK
    cols = lax.broadcasted_iota(jnp.int32, s.shape, 1)
    rel = cols - BLOCK - rows
    kblk = i - 1 + cols // BLOCK
    band_ok = (cols < 3 * BLOCK) & (kblk >= 0) & (kblk < nb) & (jnp.abs(rel) <= WINDOW)
    meta_ok = (cols >= 3 * BLOCK) & (cols < 3 * BLOCK + N_META)
    s = jnp.where(band_ok | meta_ok, s, NEG_BIG)
    sink = sink_ref[:, 0:1]
    m = jnp.maximum(jnp.max(s, axis=-1, keepdims=True), sink)
    p = jnp.exp(s - m)
    denom = jnp.sum(p, axis=-1, keepdims=True) + jnp.exp(sink - m)
    o = _dot(p.astype(BF16), v_all) / denom
    for g in range(group):
        o_ref[:, g * HEAD_DIM:(g + 1) * HEAD_DIM] = o[g * BLOCK:(g + 1) * BLOCK, :].astype(o_ref.dtype)


def _attention(qkv, sink, S, n_heads, n_kv):
    group = n_heads // n_kv
    nb = S // BLOCK
    qw = group * HEAD_DIM
    kcol = n_heads
    vcol = n_heads + n_kv
    meta_blk = (S + PAD_ROWS - N_META) // N_META
    sink_rows = jnp.broadcast_to(sink.astype(F32).reshape(n_kv, group, 1, 1),
                                 (n_kv, group, BLOCK, LANES)).reshape(n_kv, group * BLOCK, LANES)

    def kv_specs(col0):
        return [pl.BlockSpec((BLOCK, HEAD_DIM), lambda h, i: (jnp.maximum(i - 1, 0), col0 + h)),
                pl.BlockSpec((BLOCK, HEAD_DIM), lambda h, i: (i, col0 + h)),
                pl.BlockSpec((BLOCK, HEAD_DIM), lambda h, i: (jnp.minimum(i + 1, nb - 1), col0 + h)),
                pl.BlockSpec((N_META, HEAD_DIM), lambda h, i: (meta_blk, col0 + h))]

    return pl.pallas_call(
        functools.partial(_attn_kernel, nb=nb, group=group),
        grid=(n_kv, nb),
        in_specs=[pl.BlockSpec((BLOCK, qw), lambda h, i: (i, h))] + kv_specs(kcol) + kv_specs(vcol)
                 + [pl.BlockSpec((None, group * BLOCK, LANES), lambda h, i: (h, 0, 0))],
        out_specs=pl.BlockSpec((BLOCK, qw), lambda h, i: (i, h)),
        out_shape=jax.ShapeDtypeStruct((S, n_heads * HEAD_DIM), BF16),
        compiler_params=_params(("arbitrary", "arbitrary"),
                                _vmem_limit(BLOCK * qw * 2 * 2, 8 * BLOCK * HEAD_DIM * 2,
                                            group * BLOCK * LANES * 4, temps=6 * group * BLOCK * 512 * 4)),
        name="attention",
    )(qkv, qkv, qkv, qkv, qkv, qkv, qkv, qkv, qkv, sink_rows)


def _rope(t, cos, sin):
    half = ROPE_DIMS // 2
    lane = lax.broadcasted_iota(jnp.int32, t.shape, 1)
    up = jnp.concatenate([t[:, half:], t[:, :half]], axis=1)
    down = jnp.concatenate([t[:, HEAD_DIM - half:], t[:, :HEAD_DIM - half]], axis=1)
    return t * cos + jnp.where(lane < half, up, down) * sin


def _attn_res_kernel(sink_ref, q_ref, k_ref, v_ref, cos_ref, sin_ref, o_ref, kb_ref, vb_ref,
                     *, nb, group, qblocks):
    h, i = pl.program_id(0), pl.program_id(1)
    lp = k_ref.shape[0]

    @pl.when(i == 0)
    def _():
        def body(c, carry):
            sl = pl.ds(pl.multiple_of(c * BLOCK, BLOCK), BLOCK)
            kb_ref[sl, :] = _rope(k_ref[sl, :], cos_ref[sl, :], sin_ref[sl, :]).astype(BF16)
            vb_ref[sl, :] = v_ref[sl, :].astype(BF16)
            return carry
        lax.fori_loop(0, lp // BLOCK, body, 0)

    zpad = jnp.zeros((BLOCK - N_META, HEAD_DIM), BF16)
    k_meta = jnp.concatenate([kb_ref[lp - N_META:lp, :], zpad], axis=0)
    v_meta = jnp.concatenate([vb_ref[lp - N_META:lp, :], zpad], axis=0)
    rows = lax.broadcasted_iota(jnp.int32, (BLOCK, 4 * BLOCK), 0)
    cols = lax.broadcasted_iota(jnp.int32, (BLOCK, 4 * BLOCK), 1)
    is_meta = (cols >= 3 * BLOCK) & (cols < 3 * BLOCK + N_META)
    scale = HEAD_DIM ** -0.5
    k_all, v_all, bias, q, sink = [], [], [], [], []
    for b in range(qblocks):
        qi = i * qblocks + b
        kb0 = jnp.clip(qi - 1, 0, nb - 3)
        win = pl.ds(pl.multiple_of(kb0 * BLOCK, BLOCK), 3 * BLOCK)
        k_b = jnp.concatenate([kb_ref[win, :], k_meta], axis=0)
        v_b = jnp.concatenate([vb_ref[win, :], v_meta], axis=0)
        rel = cols - rows + (kb0 - qi) * BLOCK
        ok = ((cols < 3 * BLOCK) & (jnp.abs(rel) <= WINDOW)) | is_meta
        bias_b = jnp.where(ok, 0.0, NEG_BIG).astype(F32)
        qrows = pl.ds(pl.multiple_of(qi * BLOCK, BLOCK), BLOCK)
        cos_q, sin_q = cos_ref[qrows, :], sin_ref[qrows, :]
        for g in range(group):
            k_all.append(k_b)
            v_all.append(v_b)
            bias.append(bias_b)
            q.append(_rope(q_ref[b * BLOCK:(b + 1) * BLOCK, g * HEAD_DIM:(g + 1) * HEAD_DIM],
                           cos_q, sin_q).astype(BF16))
            sink.append(sink_ref[h * group + g])
    n = range(len(q))
    s = [_dot(q[c], k_all[c], NT) * scale + bias[c] for c in n]
    m = [jnp.maximum(jnp.max(s[c], axis=-1, keepdims=True), sink[c]) for c in n]
    p = [jnp.exp(s[c] - m[c]) for c in n]
    denom = [jnp.sum(p[c], axis=-1, keepdims=True) + jnp.exp(sink[c] - m[c]) for c in n]
    o = [_dot(p[c].astype(BF16), v_all[c]) / denom[c] for c in n]
    for c in n:
        b, g = divmod(c, group)
        o_ref[b * BLOCK:(b + 1) * BLOCK, g * HEAD_DIM:(g + 1) * HEAD_DIM] = o[c].astype(o_ref.dtype)


def _attention_res(z, cos_t, sin_t, sink, S, n_heads, n_kv, off_q):
    Lp = z.shape[0]
    group = n_heads // n_kv
    nb = S // BLOCK
    assert nb >= 3
    qw = group * HEAD_DIM
    kcol = off_q // HEAD_DIM + n_heads
    vcol = kcol + n_kv
    col_bytes = Lp * HEAD_DIM * 4
    qblocks = _pick(nb, (2, 1))
    tq = qblocks * BLOCK
    return pl.pallas_call(
        functools.partial(_attn_res_kernel, nb=nb, group=group, qblocks=qblocks),
        grid_spec=pltpu.PrefetchScalarGridSpec(
            num_scalar_prefetch=1,
            grid=(n_kv, nb // qblocks),
            in_specs=[pl.BlockSpec((tq, qw), lambda h, i, s: (i, off_q // qw + h)),
                      pl.BlockSpec((Lp, HEAD_DIM), lambda h, i, s: (0, kcol + h)),
                      pl.BlockSpec((Lp, HEAD_DIM), lambda h, i, s: (0, vcol + h)),
                      pl.BlockSpec((Lp, HEAD_DIM), lambda h, i, s: (0, 0)),
                      pl.BlockSpec((Lp, HEAD_DIM), lambda h, i, s: (0, 0))],
            out_specs=pl.BlockSpec((tq, qw), lambda h, i, s: (i, h)),
            scratch_shapes=[pltpu.VMEM((Lp, HEAD_DIM), BF16), pltpu.VMEM((Lp, HEAD_DIM), BF16)]),
        out_shape=jax.ShapeDtypeStruct((S, n_heads * HEAD_DIM), BF16),
        compiler_params=_params(("arbitrary", "arbitrary"),
                                _vmem_limit(tq * qw * 4, 4 * col_bytes, tq * qw * 2,
                                            temps=col_bytes + 8 * group * tq * 512 * 4)),
        name="attention",
    )(sink.astype(F32), z, z, z, cos_t, sin_t)


def _split3(x):
    hi = x.astype(BF16)
    r1 = x - hi.astype(F32)
    mid = r1.astype(BF16)
    lo = (r1 - mid.astype(F32)).astype(BF16)
    return hi, mid, lo


def _split2(x):
    hi = x.astype(BF16)
    return hi, (x - hi.astype(F32)).astype(BF16)


def _dot_exact_rhs(x, m_bf16, dims=NN, left=False):
    parts = _split3(x)
    if left:
        return sum(_dot(m_bf16, p, dims) for p in parts)
    return sum(_dot(p, m_bf16, dims) for p in parts)


def _dotp(a, b, passes, dims=NN):
    if passes == 1:
        return _dot(a.astype(BF16), b.astype(BF16), dims)
    a_hi, a_lo = _split2(a)
    b_hi, b_lo = _split2(b)
    return _dot(a_hi, b_hi, dims) + (_dot(a_hi, b_lo, dims) + _dot(a_lo, b_hi, dims))


def _head_sum(x, ones_bd):
    outs = []
    for s in range(x.shape[1] // LANES):
        outs.append(_dot_exact_rhs(x[:, s * LANES:(s + 1) * LANES], ones_bd))
    return outs[0] if len(outs) == 1 else jnp.concatenate(outs, axis=1)


def _prep_kernel(r_ref, rp_ref, rn_ref, k_ref, kp_ref, kn_ref, v_ref, vp_ref, vn_ref,
                 lo_ref, lop_ref, lon_ref, gd_ref, gdp_ref, gdn_ref,
                 pc_ref, pl_ref, wl_ref, g2_ref, ones_ref,
                 ro_ref, vo_ref, kko_ref, lwf_ref, lwb_ref, kdf_ref, kdb_ref, bdf_ref, bdb_ref, go_ref, bo_ref,
                 *, n_tiles, tr, lora_tanh_cols):
    i = pl.program_id(1)

    def shifted(main_ref, prev_ref, next_ref, mup, mun):
        x = main_ref[...]
        row8 = lax.broadcasted_iota(jnp.int32, (8, x.shape[1]), 0)
        down = pltpu.roll(x, 1, 0)
        up = pltpu.roll(x, tr - 1, 0)
        prev = jnp.concatenate([jnp.where(row8 == 0, prev_ref[7:8, :], down[0:8]), down[8:]], axis=0)
        nxt = jnp.concatenate([up[:tr - 8], jnp.where(row8 == 7, next_ref[0:1, :], up[tr - 8:])], axis=0)
        return x + mup * (prev - x) + mun * (nxt - x)

    r = shifted(r_ref, rp_ref, rn_ref, pc_ref[0:1, :], pc_ref[1:2, :])
    k = shifted(k_ref, kp_ref, kn_ref, pc_ref[2:3, :], pc_ref[3:4, :])
    v = shifted(v_ref, vp_ref, vn_ref, pc_ref[4:5, :], pc_ref[5:6, :])
    lo = shifted(lo_ref, lop_ref, lon_ref, pl_ref[0:1, 0:LORA_PAD], pl_ref[1:2, 0:LORA_PAD])
    gd = shifted(gd_ref, gdp_ref, gdn_ref, pl_ref[0:1, LORA_PAD:], pl_ref[1:2, LORA_PAD:])

    row = lax.broadcasted_iota(jnp.int32, (tr, 1), 0)
    valid = jnp.where((i < n_tiles - 1) | (row >= tr - N_META), 1.0, 0.0).astype(F32)

    lane = lax.broadcasted_iota(jnp.int32, lo.shape, 1)
    lo_act = jnp.where(lane < lora_tanh_cols, jnp.tanh(lo), lo).astype(BF16)
    dec_f = _dot(lo_act, wl_ref[0])
    dec_b = _dot(lo_act, wl_ref[1])
    apre_f = _dot(lo_act, wl_ref[2])
    apre_b = _dot(lo_act, wl_ref[3])
    g = _dot(jax.nn.sigmoid(lo).astype(BF16), g2_ref[0]) + _dot(jax.nn.sigmoid(gd).astype(BF16), g2_ref[1])

    def log_decay(dec, w0):
        return -math.exp(-0.5) * jax.nn.sigmoid(w0 + dec)

    a_f = jax.nn.sigmoid(pc_ref[8:9, :] + apre_f)
    a_b = jax.nn.sigmoid(pc_ref[9:10, :] + apre_b)
    kk = k * pc_ref[10:11, :]
    ss = _head_sum(kk * kk, ones_ref[...])
    kk = kk / jnp.maximum(jnp.sqrt(ss), 1e-12) * valid
    k_a = pc_ref[11:12, :]
    kv = k * valid

    rv = r * valid
    vv = v * valid
    kd_f = kv * (1.0 + (a_f - 1.0) * k_a)
    kd_b = kv * (1.0 + (a_b - 1.0) * k_a)
    ro_ref[...] = rv
    vo_ref[...] = vv
    kko_ref[...] = kk
    lwf_ref[...] = log_decay(dec_f, pc_ref[6:7, :])
    lwb_ref[...] = log_decay(dec_b, pc_ref[7:8, :])
    kdf_ref[...] = kd_f
    kdb_ref[...] = kd_b
    bdf_ref[...] = kk * a_f
    bdb_ref[...] = kk * a_b
    go_ref[...] = g
    bo_ref[...] = _head_sum(rv * (kd_f + kd_b) * pc_ref[12:13, :], ones_ref[...]) * vv


def _rwkv_prep(z, off_r, C, pc, plo, wl, g2, ones_bd, lora_tanh_cols):
    Lp = z.shape[0]
    tr = BLOCK
    ct = _pick(C, (1024, 512))
    n_tiles = Lp // tr
    n8 = Lp // 8
    off_lo = off_r + 3 * C
    off_gd = off_lo + LORA_PAD
    assert off_r % ct == 0 and off_lo % LORA_PAD == 0

    def seg_specs(off, w, with_c):
        cb = off // w

        def col(c):
            return cb + c if with_c else cb
        return [pl.BlockSpec((tr, w), lambda c, i: (i, col(c))),
                pl.BlockSpec((8, w), lambda c, i: ((i * (tr // 8) + n8 - 1) % n8, col(c))),
                pl.BlockSpec((8, w), lambda c, i: (((i + 1) * (tr // 8)) % n8, col(c)))]

    in_specs = (seg_specs(off_r, ct, True) + seg_specs(off_r + C, ct, True) + seg_specs(off_r + 2 * C, ct, True)
                + seg_specs(off_lo, LORA_PAD, False) + seg_specs(off_gd, LORA_PAD, False)
                + [pl.BlockSpec((16, ct), lambda c, i: (0, c)),
                   pl.BlockSpec((8, 2 * LORA_PAD), lambda c, i: (0, 0)),
                   pl.BlockSpec((4, LORA_PAD, ct), lambda c, i: (0, 0, c)),
                   pl.BlockSpec((2, LORA_PAD, ct), lambda c, i: (0, 0, c)),
                   pl.BlockSpec((LANES, LANES), lambda c, i: (0, 0))])
    out_spec = pl.BlockSpec((tr, ct), lambda c, i: (i, c))
    out_sds = jax.ShapeDtypeStruct((Lp, C), F32)
    return pl.pallas_call(
        functools.partial(_prep_kernel, n_tiles=n_tiles, tr=tr, lora_tanh_cols=lora_tanh_cols),
        grid=(C // ct, n_tiles),
        in_specs=in_specs,
        out_specs=[out_spec] * 11,
        out_shape=[out_sds] * 11,
        compiler_params=_params(("arbitrary", "arbitrary"),
                                _vmem_limit(3 * tr * ct * 4, 2 * tr * LORA_PAD * 4, 4 * LORA_PAD * ct * 2,
                                            2 * LORA_PAD * ct * 2, 11 * tr * ct * 4, temps=24 * tr * ct * 4)),
        name="rwkv_prep",
    )(*([z] * 15), pc, plo, wl, g2, ones_bd)


SCAN_PAIRS = 8


def _scan_consts():
    C = CHUNK
    row = lax.broadcasted_iota(jnp.int32, (C, 2 * C), 0)
    col = lax.broadcasted_iota(jnp.int32, (C, 2 * C), 1) % C
    rr = lax.broadcasted_iota(jnp.int32, (LANES, LANES), 0)
    cc = lax.broadcasted_iota(jnp.int32, (LANES, LANES), 1)
    return dict(
        row=lax.broadcasted_iota(jnp.int32, (C, LANES), 0),
        head0=lax.broadcasted_iota(jnp.int32, (C, LANES), 1) < RWKV_HEAD,
        strict_f=col < row, strict_b=col > row, incl_f=col <= row, incl_b=col >= row,
        eye_sbs=jnp.where(col == row, 1.0, 0.0).astype(F32),
        same_head=(rr // RWKV_HEAD) == (cc // RWKV_HEAD), eye=rr == cc)


def _scan_chains(chains, cst):
    C = CHUNK
    head0, same, eye = cst["head0"], cst["same_head"], cst["eye"]
    n = range(len(chains))
    rev = [c[7] for c in chains]
    r, v, al, lw, k, be, h = ([c[i] for c in chains] for i in range(7))
    strict = [cst["strict_b" if x else "strict_f"] for x in rev]
    incl = [cst["incl_b" if x else "incl_f"] for x in rev]

    def bd(x):
        xb = x.astype(BF16)
        zero = jnp.zeros_like(xb)
        return jnp.concatenate([jnp.where(head0, xb, zero), jnp.where(head0, zero, xb)], axis=0)

    def mm(a, b_bf16, dims=NN):
        return _dot(a.astype(BF16), b_bf16, dims)

    row = cst["row"]
    cl = list(lw)
    sh = 1
    while sh < C:
        cl = [cl[i] + (jnp.where(row < C - sh, pltpu.roll(cl[i], C - sh, 0), 0.0) if rev[i] else
                       jnp.where(row >= sh, pltpu.roll(cl[i], sh, 0), 0.0)) for i in n]
        sh *= 2
    total = [cl[i][0:1, :] if rev[i] else cl[i][C - 1:C, :] for i in n]
    a_t = [al[i] * jnp.exp(cl[i] - lw[i]) for i in n]
    r_t = [r[i] * jnp.exp(cl[i]) for i in n]
    w_inv = [jnp.exp(-cl[i]) for i in n]
    w_rest = [jnp.exp(total[i] - cl[i]) for i in n]
    bd_v = [bd(v[i]) for i in n]
    sc = [mm(jnp.concatenate([a_t[i], r_t[i]], axis=0),
             jnp.concatenate([bd(k[i] * w_inv[i]), bd(be[i] * w_inv[i])], axis=0), NT) for i in n]
    a_ak = [jnp.where(strict[i], sc[i][0:C, 0:2 * C], 0.0) for i in n]
    n_ab = [jnp.where(strict[i], sc[i][0:C, 2 * C:], 0.0) for i in n]
    a_rk = [jnp.where(incl[i], sc[i][C:, 0:2 * C], 0.0) for i in n]
    a_rb = [jnp.where(incl[i], sc[i][C:, 2 * C:], 0.0) for i in n]

    t_inv = [cst["eye_sbs"] - n_ab[i] for i in n]
    sq = [mm(n_ab[i], bd(n_ab[i])) for i in n]
    n_round = 2
    while (1 << n_round) < C:
        both = [mm(jnp.concatenate([t_inv[i], sq[i]], axis=0), bd(sq[i])) for i in n]
        t_inv = [t_inv[i] + both[i][0:C] for i in n]
        sq = [both[i][C:] for i in n]
        n_round += 1
    t_inv = [t_inv[i] + mm(t_inv[i], bd(sq[i])) for i in n]

    akv = [mm(jnp.concatenate([a_ak[i], a_rk[i]], axis=0), bd_v[i]) for i in n]
    tp = [mm(t_inv[i], jnp.concatenate([bd(akv[i][0:C]), bd(a_t[i])], axis=1)) for i in n]
    p0 = [tp[i][:, 0:LANES] for i in n]
    at = [tp[i][:, LANES:] for i in n]
    rb = [mm(a_rb[i], jnp.concatenate([bd(p0[i]), bd(at[i])], axis=1)) for i in n]
    y0 = [akv[i][C:] - rb[i][:, 0:LANES] for i in n]
    rh = [r_t[i] - rb[i][:, LANES:] for i in n]
    bt = [mm(be[i] * w_rest[i], jnp.concatenate([at[i], p0[i]], axis=1).astype(BF16), TN) for i in n]
    kv = [mm(k[i] * w_rest[i], v[i].astype(BF16), TN) for i in n]
    m_mat = [jnp.where(same, jnp.where(eye, jnp.broadcast_to(jnp.exp(total[i]), (LANES, LANES)), 0.0)
                       - bt[i][:, 0:LANES], 0.0) for i in n]
    g_mat = [jnp.where(same, kv[i] - bt[i][:, LANES:], 0.0) for i in n]
    yh = [mm(jnp.concatenate([rh[i], m_mat[i]], axis=0), h[i].astype(BF16)) for i in n]
    return [(yh[i][0:C] + y0[i], yh[i][C:] + g_mat[i]) for i in n]


def _scan_kernel(rf_ref, vf_ref, af_ref, lwf_ref, kf_ref, bf_ref,
                 rb_ref, vb_ref, ab_ref, lwb_ref, kb_ref, bb_ref,
                 yf_ref, yb_ref, hf_ref, hb_ref, *, pairs):
    @pl.when(pl.program_id(1) == 0)
    def _():
        hf_ref[...] = jnp.zeros(hf_ref.shape, F32)
        hb_ref[...] = jnp.zeros(hb_ref.shape, F32)

    cst = _scan_consts()
    chains = []
    for g in range(pairs):
        sl = slice(g * LANES, (g + 1) * LANES)
        chains.append((rf_ref[:, sl], vf_ref[:, sl], af_ref[:, sl], lwf_ref[:, sl], kf_ref[:, sl], bf_ref[:, sl],
                       hf_ref[g], False))
        chains.append((rb_ref[:, sl], vb_ref[:, sl], ab_ref[:, sl], lwb_ref[:, sl], kb_ref[:, sl], bb_ref[:, sl],
                       hb_ref[g], True))
    res = _scan_chains(chains, cst)
    for g in range(pairs):
        sl = slice(g * LANES, (g + 1) * LANES)
        yf_ref[:, sl], hf_ref[g] = res[2 * g]
        yb_ref[:, sl], hb_ref[g] = res[2 * g + 1]


def _rwkv_scan(r, v, kk, lw_f, lw_b, kd_f, kd_b, bd_f, bd_b, S):
    Lp, C = r.shape
    n_real = S // CHUNK
    n_chunks = Lp // CHUNK
    steps = n_real + 1
    pairs = _pick(C // LANES, (SCAN_PAIRS, 2, 1))
    w = pairs * LANES

    def fwd(p, s):
        return ((s + n_real + 1) % n_chunks, p)

    def bwd(p, s):
        return (n_real - s, p)

    spec_f = pl.BlockSpec((CHUNK, w), fwd)
    spec_b = pl.BlockSpec((CHUNK, w), bwd)
    out_sds = jax.ShapeDtypeStruct((S + CHUNK, C), F32)
    out_f = pl.BlockSpec((CHUNK, w), lambda p, s: ((s + n_real) % (n_real + 1), p))
    return pl.pallas_call(
        functools.partial(_scan_kernel, pairs=pairs),
        grid=(C // w, steps),
        in_specs=[spec_f] * 6 + [spec_b] * 6,
        out_specs=[out_f, spec_b],
        out_shape=[out_sds, out_sds],
        scratch_shapes=[pltpu.VMEM((pairs, LANES, LANES), F32), pltpu.VMEM((pairs, LANES, LANES), F32)],
        compiler_params=_params(("arbitrary", "arbitrary"), 32 << 20),
        name="rwkv_scan",
    )(r, v, kk, lw_f, kd_f, bd_f, r, v, kk, lw_b, kd_b, bd_b)


def _post_kernel(yf_ref, yb_ref, bonus_ref, g_ref, pq_ref, ones_ref, o_ref):
    ones_bd = ones_ref[...]
    inv_n = 1.0 / RWKV_HEAD
    y = yf_ref[...] + yb_ref[...]
    mean = _head_sum(y, ones_bd) * inv_n
    yc = y - mean
    var = _head_sum(yc * yc, ones_bd) * inv_n
    yn = yc * lax.rsqrt(var + GN_EPS) * pq_ref[0:1, :] + pq_ref[1:2, :]
    o_ref[...] = ((yn + bonus_ref[...]) * g_ref[...]).astype(o_ref.dtype)


def _rwkv_post(y_f, y_b, bonus, g, pq, ones_bd, S):
    C = g.shape[1]
    tr = 512
    ct = _pick(C, (1024, 512))
    spec = pl.BlockSpec((tr, ct), lambda i, c: (i, c))
    return pl.pallas_call(
        _post_kernel,
        grid=(S // tr, C // ct),
        in_specs=[spec] * 4 + [pl.BlockSpec((8, ct), lambda i, c: (0, c)),
                               pl.BlockSpec((LANES, LANES), lambda i, c: (0, 0))],
        out_specs=spec,
        out_shape=jax.ShapeDtypeStruct((S, C), BF16),
        compiler_params=_params(("arbitrary", "arbitrary"),
                                _vmem_limit(4 * tr * ct * 4, tr * ct * 2, temps=16 * tr * ct * 4)),
        name="rwkv_post",
    )(y_f, y_b, bonus, g, pq, ones_bd)


def _pad_cols(w, width):
    return jnp.pad(w, ((0, 0), (0, width - w.shape[1])))


def kernel(x, meta_tokens, norm_mix_w, w_in, b_gate, mu_prev, mu_next, dec_w0, dec_w2, iclr_a0, iclr_a2,
           gate_w2, k_k, k_a, r_k, ln_x_w, ln_x_b, attn_sink, w_proj_attn, w_proj_rwkv, w_out, norm_ffn_w,
           w_ffn_gate, w_ffn_up, w_ffn_down, norm_final_w):
    B, S, D = x.shape
    assert B == 1 and norm_mix_w.shape[0] == 1, "one sequence, one layer"
    C = k_k.shape[-1]
    AW = w_proj_attn.shape[1]
    n_heads = attn_sink.shape[-1]
    DL, IL, GL = dec_w2.shape[2], iclr_a2.shape[2], gate_w2.shape[1]
    shift_w = mu_prev.shape[-1]
    KVW = (w_in.shape[-1] - 2 * D - AW - shift_w) // 2
    n_kv = KVW // HEAD_DIM
    F = w_ffn_gate.shape[-1]
    lora_w = 2 * DL + 2 * IL
    assert shift_w == 3 * C + lora_w + GL and lora_w <= LORA_PAD and GL <= LORA_PAD
    assert S % BLOCK == 0 and C % LANES == 0 and n_heads * HEAD_DIM == AW
    Lp = S + PAD_ROWS

    off_q = 2 * D
    off_r = off_q + AW + 2 * KVW
    off_lo = off_r + 3 * C
    tn_in = 1024
    n_in_pad = -(-w_in.shape[-1] // tn_in) * tn_in
    assert n_in_pad >= off_lo + 2 * LORA_PAD and off_lo % LORA_PAD == 0
    low_w = lora_w + GL
    gl_a = LORA_PAD - lora_w

    mu_p, mu_n = mu_prev[0], mu_next[0]
    zc = jnp.zeros((C,), F32)
    pc = jnp.stack([mu_p[:C], mu_n[:C], mu_p[C:2 * C], mu_n[C:2 * C], mu_p[2 * C:3 * C], mu_n[2 * C:3 * C],
                    dec_w0[0, 0], dec_w0[0, 1], iclr_a0[0, 0], iclr_a0[0, 1], k_k[0], k_a[0], r_k[0], zc, zc, zc])
    plo = jnp.zeros((8, 2 * LORA_PAD), F32).at[0, :low_w].set(mu_p[3 * C:]).at[1, :low_w].set(mu_n[3 * C:])
    wl = jnp.zeros((4, LORA_PAD, C), F32)
    wl = wl.at[0, 0:DL].set(dec_w2[0, 0]).at[1, DL:2 * DL].set(dec_w2[0, 1])
    wl = wl.at[2, 2 * DL:2 * DL + IL].set(iclr_a2[0, 0]).at[3, 2 * DL + IL:lora_w].set(iclr_a2[0, 1])
    wl = wl.astype(BF16)
    g2 = jnp.zeros((2, LORA_PAD, C), F32).at[0, lora_w:].set(gate_w2[0, :gl_a]).at[1, :GL - gl_a].set(
        gate_w2[0, gl_a:]).astype(BF16)
    pq = jnp.zeros((8, C), F32).at[0].set(ln_x_w[0]).at[1].set(ln_x_b[0])
    lane = jnp.arange(LANES)
    ones_bd = (lane[:, None] // RWKV_HEAD == lane[None, :] // RWKV_HEAD).astype(BF16)

    w_d = w_ffn_down[0].astype(BF16)

    pos = jnp.concatenate([jnp.arange(N_META, N_META + S), jnp.zeros((PAD_ROWS - N_META,), jnp.int32),
                           jnp.arange(N_META)]).astype(F32)
    inv = ROPE_THETA ** (-jnp.arange(0, ROPE_DIMS, 2, dtype=F32) / ROPE_DIMS)
    ang = pos[:, None] * inv[None, :]
    ones_rest = jnp.ones((Lp, HEAD_DIM - ROPE_DIMS), F32)
    cos_t = jnp.concatenate([jnp.cos(ang), jnp.cos(ang), ones_rest], axis=1)
    sin_t = jnp.concatenate([-jnp.sin(ang), jnp.sin(ang), 0.0 * ones_rest], axis=1)

    x2 = x[0]
    h_ext = _norm_ext(x2, meta_tokens.astype(x.dtype), norm_mix_w[0])
    z = _mm_in_s(h_ext, w_in[0], _pick(Lp, (1040, 640)), tn_in)
    o_att = _attention_res(z, cos_t, sin_t, attn_sink[0], S, n_heads, n_kv, off_q)
    r_s, v_s, kk, lw_f, lw_b, kd_f, kd_b, bd_f, bd_b, g, bonus = _rwkv_prep(
        z, off_r, C, pc, plo, wl, g2, ones_bd, 2 * DL)
    y_f, y_b = _rwkv_scan(r_s, v_s, kk, lw_f, lw_b, kd_f, kd_b, bd_f, bd_b, S)
    o_rwkv = _rwkv_post(y_f, y_b, bonus, g, pq, ones_bd, S)
    tm = _pick(S, (1024, 512))
    tn_d = _pick(D, (1024, 512))
    mixed = _mm_mix_s(o_att, w_proj_attn[0], o_rwkv, w_proj_rwkv[0], z, b_gate[0], S, D, 512, tn_d)
    h1 = _mm_out_s(mixed, w_out[0], x2, 512, tn_d)
    hn = _norm(h1, norm_ffn_w[0], BF16)
    act = _mm_glu_s(hn, w_ffn_gate[0], w_ffn_up[0], tm, 512)
    h2 = _mm_res(act, w_d, h1, tm, 512, F // 2)
    y = _norm(h2, norm_final_w, x.dtype)
    return y[None]
```

```python
import functools
import math

import jax
import jax.numpy as jnp
from jax import lax
from jax.experimental import pallas as pl
from jax.experimental.pallas import tpu as pltpu

F32 = jnp.float32
BF16 = jnp.bfloat16

N_META = 16
HEAD_DIM = 128
WINDOW = 128
BLOCK = 128
ROPE_DIMS = HEAD_DIM // 4
ROPE_THETA = 500000.0
RWKV_HEAD = 64
RMS_EPS = 1e-6
GN_EPS = 64e-5
LANES = 128
PAD_ROWS = BLOCK
CHUNK = 64
LORA_PAD = 512
VMEM_PHYSICAL = 64 * 1024 * 1024

NN = (((1,), (0,)), ((), ()))
NT = (((1,), (1,)), ((), ()))
TN = (((0,), (0,)), ((), ()))


def _vmem_limit(*block_bytes, temps=0):
    need = 2 * sum(block_bytes) + temps + (4 << 20)
    return int(min(max(need, 16 << 20), VMEM_PHYSICAL - (4 << 20)))


def _params(sem, limit):
    return pltpu.CompilerParams(dimension_semantics=sem, vmem_limit_bytes=limit)


def _pick(n, cands):
    for c in cands:
        if n % c == 0:
            return c
    raise ValueError(f"no tile in {cands} divides {n}")


def _rms(x, w):
    return x * lax.rsqrt(jnp.mean(x * x, axis=-1, keepdims=True) + RMS_EPS) * w


def _norm_ext_kernel(x_ref, meta_ref, w_ref, o_ref, *, n_steps, real_last):
    i = pl.program_id(0)
    tr = o_ref.shape[0]

    @pl.when(i < n_steps - 1)
    def _():
        o_ref[...] = _rms(x_ref[...], w_ref[...]).astype(o_ref.dtype)

    @pl.when(i == n_steps - 1)
    def _():
        if real_last:
            o_ref[0:real_last, :] = _rms(x_ref[0:real_last, :], w_ref[...]).astype(o_ref.dtype)
        o_ref[real_last:tr - N_META, :] = jnp.zeros((tr - N_META - real_last, o_ref.shape[1]), o_ref.dtype)
        o_ref[tr - N_META:, :] = _rms(meta_ref[...], w_ref[...]).astype(o_ref.dtype)


def _norm_ext(x, meta, w):
    S, D = x.shape
    Lp = S + PAD_ROWS
    tr = _pick(Lp, (640, BLOCK))
    n_steps = Lp // tr
    real_last = S - (n_steps - 1) * tr
    last_x_blk = (S - 1) // tr
    return pl.pallas_call(
        functools.partial(_norm_ext_kernel, n_steps=n_steps, real_last=real_last),
        grid=(n_steps,),
        in_specs=[pl.BlockSpec((tr, D), lambda i: (jnp.minimum(i, last_x_blk), 0)),
                  pl.BlockSpec((N_META, D), lambda i: (0, 0)),
                  pl.BlockSpec((1, D), lambda i: (0, 0))],
        out_specs=pl.BlockSpec((tr, D), lambda i: (i, 0)),
        out_shape=jax.ShapeDtypeStruct((Lp, D), BF16),
        compiler_params=_params(("arbitrary",), _vmem_limit(tr * D * 4, tr * D * 2, temps=2 * tr * D * 4)),
        name="norm_ext",
    )(x, meta, w.reshape(1, D))


def _norm_kernel(x_ref, w_ref, o_ref):
    o_ref[...] = _rms(x_ref[...], w_ref[...]).astype(o_ref.dtype)


def _norm(x, w, out_dtype, tr=512):
    S, D = x.shape
    return pl.pallas_call(
        _norm_kernel,
        grid=(S // tr,),
        in_specs=[pl.BlockSpec((tr, D), lambda i: (i, 0)), pl.BlockSpec((1, D), lambda i: (0, 0))],
        out_specs=pl.BlockSpec((tr, D), lambda i: (i, 0)),
        out_shape=jax.ShapeDtypeStruct((S, D), out_dtype),
        compiler_params=_params(("arbitrary",), _vmem_limit(tr * D * 4, tr * D * 4, temps=2 * tr * D * 4)),
        name="norm",
    )(x, w.reshape(1, D))


def _dot(a, b, dims=NN):
    return lax.dot_general(a, b, dims, preferred_element_type=F32)


def _cast_resident(src_ref, dst_ref, col0=None, n_valid=None):
    K = src_ref.shape[0]
    rows = _pick(K, (256, 128))

    def body(c, carry):
        sl = pl.ds(pl.multiple_of(c * rows, rows), rows)
        blk = src_ref[sl, :]
        if n_valid is not None:
            col = col0 + lax.broadcasted_iota(jnp.int32, blk.shape, 1)
            blk = jnp.where(col < n_valid, blk, 0.0)
        dst_ref[sl, :] = blk.astype(dst_ref.dtype)
        return carry

    lax.fori_loop(0, K // rows, body, 0)


def _mm_in_kernel(a_ref, b_ref, o_ref, bw_ref, *, n_valid):
    j, i = pl.program_id(0), pl.program_id(1)

    @pl.when(i == 0)
    def _():
        _cast_resident(b_ref, bw_ref, j * b_ref.shape[1], n_valid)

    o_ref[...] = _dot(a_ref[...], bw_ref[...])


def _mm_in(a, w, tm, tn):
    M, K = a.shape
    N = w.shape[1]
    nj = pl.cdiv(N, tn)
    return pl.pallas_call(
        functools.partial(_mm_in_kernel, n_valid=N),
        grid=(nj, M // tm),
        in_specs=[pl.BlockSpec((tm, K), lambda j, i: (i, 0)), pl.BlockSpec((K, tn), lambda j, i: (0, j))],
        out_specs=pl.BlockSpec((tm, tn), lambda j, i: (i, j)),
        out_shape=jax.ShapeDtypeStruct((M, nj * tn), F32),
        scratch_shapes=[pltpu.VMEM((K, tn), BF16)],
        compiler_params=_params(("arbitrary", "arbitrary"),
                                _vmem_limit(tm * K * 2, K * tn * 4, tm * tn * 4, temps=K * tn * 2 + tm * tn * 4)),
        name="mm_in",
    )(a, w)


def _mm_mix_kernel(a1_ref, b1_ref, a2_ref, b2_ref, z0_ref, z1_ref, bg_ref, o_ref, w1_ref, w2_ref):
    @pl.when(pl.program_id(1) == 0)
    def _():
        _cast_resident(b1_ref, w1_ref)
        _cast_resident(b2_ref, w2_ref)

    g0 = jax.nn.sigmoid(z0_ref[...] + bg_ref[0:1, :])
    g1 = jax.nn.sigmoid(z1_ref[...] + bg_ref[1:2, :])
    o_ref[...] = (g0 * _dot(a1_ref[...], w1_ref[...]) + g1 * _dot(a2_ref[...], w2_ref[...])).astype(o_ref.dtype)


def _mm_mix(o_att, w_pa, o_rwkv, w_pr, z, b_gate, S, D, tm, tn):
    K1, K2 = o_att.shape[1], o_rwkv.shape[1]
    nj = D // tn
    return pl.pallas_call(
        _mm_mix_kernel,
        grid=(nj, S // tm),
        in_specs=[pl.BlockSpec((tm, K1), lambda j, i: (i, 0)), pl.BlockSpec((K1, tn), lambda j, i: (0, j)),
                  pl.BlockSpec((tm, K2), lambda j, i: (i, 0)), pl.BlockSpec((K2, tn), lambda j, i: (0, j)),
                  pl.BlockSpec((tm, tn), lambda j, i: (i, j)), pl.BlockSpec((tm, tn), lambda j, i: (i, nj + j)),
                  pl.BlockSpec((2, tn), lambda j, i: (0, j))],
        out_specs=pl.BlockSpec((tm, tn), lambda j, i: (i, j)),
        out_shape=jax.ShapeDtypeStruct((S, D), BF16),
        scratch_shapes=[pltpu.VMEM((K1, tn), BF16), pltpu.VMEM((K2, tn), BF16)],
        compiler_params=_params(("arbitrary", "arbitrary"),
                                _vmem_limit(tm * K1 * 2, K1 * tn * 4, tm * K2 * 2, K2 * tn * 4,
                                            2 * tm * tn * 4, tm * tn * 2,
                                            temps=(K1 + K2) * tn * 2 + 3 * tm * tn * 4)),
        name="mm_mix",
    )(o_att, w_pa, o_rwkv, w_pr, z, z, b_gate)


def _mm_resw_kernel(a_ref, b_ref, r_ref, o_ref, bw_ref):
    @pl.when(pl.program_id(1) == 0)
    def _():
        _cast_resident(b_ref, bw_ref)

    o_ref[...] = r_ref[...] + _dot(a_ref[...], bw_ref[...])


def _mm_resw(a, w, res, tm, tn):
    M, K = a.shape
    N = w.shape[1]
    return pl.pallas_call(
        _mm_resw_kernel,
        grid=(N // tn, M // tm),
        in_specs=[pl.BlockSpec((tm, K), lambda j, i: (i, 0)), pl.BlockSpec((K, tn), lambda j, i: (0, j)),
                  pl.BlockSpec((tm, tn), lambda j, i: (i, j))],
        out_specs=pl.BlockSpec((tm, tn), lambda j, i: (i, j)),
        out_shape=jax.ShapeDtypeStruct((M, N), F32),
        scratch_shapes=[pltpu.VMEM((K, tn), BF16)],
        compiler_params=_params(("arbitrary", "arbitrary"),
                                _vmem_limit(tm * K * 2, K * tn * 4, 2 * tm * tn * 4, temps=K * tn * 2 + tm * tn * 4)),
        name="mm_out",
    )(a, w, res)


def _weight_copy(w_hbm, land_ref, sem, jj, tn, width):
    return pltpu.make_async_copy(w_hbm.at[:, pl.ds(pl.multiple_of(jj * tn, LANES), width)],
                                 land_ref.at[:, pl.ds(0, width)], sem)


def _weight_copy_op(op, w_hbm, land_ref, sem, jj, tn, nj, n_cols):
    rem = n_cols - (nj - 1) * tn
    if rem == tn:
        getattr(_weight_copy(w_hbm, land_ref, sem, jj, tn, tn), op)()
        return

    @pl.when(jj < nj - 1)
    def _():
        getattr(_weight_copy(w_hbm, land_ref, sem, jj, tn, tn), op)()

    @pl.when(jj == nj - 1)
    def _():
        getattr(_weight_copy(w_hbm, land_ref, sem, jj, tn, rem), op)()


def _stream_weights(weights, j, i, nj, tn, n_cols):
    @pl.when(i == 0)
    def _():
        @pl.when(j == 0)
        def _():
            for w_hbm, land_ref, _, sem in weights:
                _weight_copy_op("start", w_hbm, land_ref, sem, j, tn, nj, n_cols)

        for w_hbm, land_ref, bw_ref, sem in weights:
            _weight_copy_op("wait", w_hbm, land_ref, sem, j, tn, nj, n_cols)
            _cast_resident(land_ref, bw_ref, j * tn, n_cols if n_cols % tn else None)

        @pl.when(j + 1 < nj)
        def _():
            for w_hbm, land_ref, _, sem in weights:
                _weight_copy_op("start", w_hbm, land_ref, sem, j + 1, tn, nj, n_cols)


def _weight_scratch(ks, tn):
    return ([pltpu.VMEM((k, tn), F32) for k in ks] + [pltpu.VMEM((k, tn), BF16) for k in ks]
            + [pltpu.SemaphoreType.DMA(()) for _ in ks])


def _mm_in_s_kernel(a_ref, w_hbm, o_ref, land_ref, bw_ref, sem, *, nj, tn, n_cols):
    j, i = pl.program_id(0), pl.program_id(1)
    _stream_weights([(w_hbm, land_ref, bw_ref, sem)], j, i, nj, tn, n_cols)
    o_ref[...] = _dot(a_ref[...], bw_ref[...])


def _mm_in_s(a, w, tm, tn):
    M, K = a.shape
    N = w.shape[1]
    nj = pl.cdiv(N, tn)
    return pl.pallas_call(
        functools.partial(_mm_in_s_kernel, nj=nj, tn=tn, n_cols=N),
        grid=(nj, M // tm),
        in_specs=[pl.BlockSpec((tm, K), lambda j, i: (i, 0)), pl.BlockSpec(memory_space=pl.ANY)],
        out_specs=pl.BlockSpec((tm, tn), lambda j, i: (i, j)),
        out_shape=jax.ShapeDtypeStruct((M, nj * tn), F32),
        scratch_shapes=_weight_scratch((K,), tn),
        compiler_params=_params(("arbitrary", "arbitrary"),
                                _vmem_limit(tm * K * 2, tm * tn * 4, temps=K * tn * 6 + 2 * tm * tn * 4)),
        name="mm_in",
    )(a, w)


def _mm_mix_s_kernel(a1_ref, a2_ref, z0_ref, z1_ref, bg_ref, w1_hbm, w2_hbm, o_ref,
                     l1_ref, l2_ref, b1_ref, b2_ref, s1, s2, *, nj, tn, n_cols):
    j, i = pl.program_id(0), pl.program_id(1)
    _stream_weights([(w1_hbm, l1_ref, b1_ref, s1), (w2_hbm, l2_ref, b2_ref, s2)], j, i, nj, tn, n_cols)
    g0 = jax.nn.sigmoid(z0_ref[...] + bg_ref[0:1, :])
    g1 = jax.nn.sigmoid(z1_ref[...] + bg_ref[1:2, :])
    o_ref[...] = (g0 * _dot(a1_ref[...], b1_ref[...]) + g1 * _dot(a2_ref[...], b2_ref[...])).astype(o_ref.dtype)


def _mm_mix_s(o_att, w_pa, o_rwkv, w_pr, z, b_gate, S, D, tm, tn):
    K1, K2 = o_att.shape[1], o_rwkv.shape[1]
    assert D % tn == 0
    nj = D // tn
    return pl.pallas_call(
        functools.partial(_mm_mix_s_kernel, nj=nj, tn=tn, n_cols=D),
        grid=(nj, S // tm),
        in_specs=[pl.BlockSpec((tm, K1), lambda j, i: (i, 0)), pl.BlockSpec((tm, K2), lambda j, i: (i, 0)),
                  pl.BlockSpec((tm, tn), lambda j, i: (i, j)), pl.BlockSpec((tm, tn), lambda j, i: (i, nj + j)),
                  pl.BlockSpec((2, tn), lambda j, i: (0, j)),
                  pl.BlockSpec(memory_space=pl.ANY), pl.BlockSpec(memory_space=pl.ANY)],
        out_specs=pl.BlockSpec((tm, tn), lambda j, i: (i, j)),
        out_shape=jax.ShapeDtypeStruct((S, D), BF16),
        scratch_shapes=_weight_scratch((K1, K2), tn),
        compiler_params=_params(("arbitrary", "arbitrary"),
                                _vmem_limit(tm * (K1 + K2) * 2, 2 * tm * tn * 4, tm * tn * 2,
                                            temps=(K1 + K2) * tn * 6 + 3 * tm * tn * 4)),
        name="mm_mix",
    )(o_att, o_rwkv, z, z, b_gate, w_pa, w_pr)


def _mm_out_s_kernel(a_ref, r_ref, w_hbm, o_ref, land_ref, bw_ref, sem, *, nj, tn, n_cols):
    j, i = pl.program_id(0), pl.program_id(1)
    _stream_weights([(w_hbm, land_ref, bw_ref, sem)], j, i, nj, tn, n_cols)
    o_ref[...] = r_ref[...] + _dot(a_ref[...], bw_ref[...])


def _mm_out_s(a, w, res, tm, tn):
    M, K = a.shape
    N = w.shape[1]
    assert N % tn == 0
    nj = N // tn
    return pl.pallas_call(
        functools.partial(_mm_out_s_kernel, nj=nj, tn=tn, n_cols=N),
        grid=(nj, M // tm),
        in_specs=[pl.BlockSpec((tm, K), lambda j, i: (i, 0)), pl.BlockSpec((tm, tn), lambda j, i: (i, j)),
                  pl.BlockSpec(memory_space=pl.ANY)],
        out_specs=pl.BlockSpec((tm, tn), lambda j, i: (i, j)),
        out_shape=jax.ShapeDtypeStruct((M, N), F32),
        scratch_shapes=_weight_scratch((K,), tn),
        compiler_params=_params(("arbitrary", "arbitrary"),
                                _vmem_limit(tm * K * 2, 2 * tm * tn * 4, temps=K * tn * 6 + tm * tn * 4)),
        name="mm_out",
    )(a, res, w)


def _mm_glu_s_kernel(a_ref, wg_hbm, wu_hbm, o_ref, lg_ref, lu_ref, bg_ref, bu_ref, sg, su, *, nj, tn, n_cols):
    j, i = pl.program_id(0), pl.program_id(1)
    _stream_weights([(wg_hbm, lg_ref, bg_ref, sg), (wu_hbm, lu_ref, bu_ref, su)], j, i, nj, tn, n_cols)
    a = a_ref[...]
    gate = _dot(a, bg_ref[...])
    up = _dot(a, bu_ref[...])
    o_ref[...] = (gate * jax.nn.sigmoid(gate) * up).astype(o_ref.dtype)


def _mm_glu_s(a, wg, wu, tm, tn):
    M, K = a.shape
    N = wg.shape[1]
    nj = pl.cdiv(N, tn)
    return pl.pallas_call(
        functools.partial(_mm_glu_s_kernel, nj=nj, tn=tn, n_cols=N),
        grid=(nj, M // tm),
        in_specs=[pl.BlockSpec((tm, K), lambda j, i: (i, 0)),
                  pl.BlockSpec(memory_space=pl.ANY), pl.BlockSpec(memory_space=pl.ANY)],
        out_specs=pl.BlockSpec((tm, tn), lambda j, i: (i, j)),
        out_shape=jax.ShapeDtypeStruct((M, N), BF16),
        scratch_shapes=_weight_scratch((K, K), tn),
        compiler_params=_params(("arbitrary", "arbitrary"),
                                _vmem_limit(tm * K * 2, tm * tn * 2, temps=2 * K * tn * 6 + 3 * tm * tn * 4)),
        name="mm_glu",
    )(a, wg, wu)


def _mm_res_kernel(a_ref, b_ref, r_ref, o_ref, *, nk):
    k = pl.program_id(2)
    if nk == 1:
        o_ref[...] = r_ref[...] + _dot(a_ref[...], b_ref[...])
    else:
        @pl.when(k == 0)
        def _():
            o_ref[...] = r_ref[...] + _dot(a_ref[...], b_ref[...])

        @pl.when(k > 0)
        def _():
            o_ref[...] = o_ref[...] + _dot(a_ref[...], b_ref[...])


def _mm_res(a, b, res, tm, tn, tk):
    M, K = a.shape
    N = b.shape[1]
    nk = K // tk
    return pl.pallas_call(
        functools.partial(_mm_res_kernel, nk=nk),
        grid=(M // tm, N // tn, nk),
        in_specs=[pl.BlockSpec((tm, tk), lambda i, j, k: (i, k)), pl.BlockSpec((tk, tn), lambda i, j, k: (k, j)),
                  pl.BlockSpec((tm, tn), lambda i, j, k: (i, j))],
        out_specs=pl.BlockSpec((tm, tn), lambda i, j, k: (i, j)),
        out_shape=jax.ShapeDtypeStruct((M, N), F32),
        compiler_params=_params(("arbitrary", "arbitrary", "arbitrary"),
                                _vmem_limit(tm * tk * 2, tk * tn * 2, 2 * tm * tn * 4, temps=tm * tn * 4)),
        name="mm_down",
    )(a, b, res)


def _mm_glu_kernel(a_ref, bg_ref, bu_ref, o_ref, wg_ref, wu_ref):
    @pl.when(pl.program_id(1) == 0)
    def _():
        _cast_resident(bg_ref, wg_ref)
        _cast_resident(bu_ref, wu_ref)

    a = a_ref[...]
    gate = _dot(a, wg_ref[...])
    up = _dot(a, wu_ref[...])
    o_ref[...] = (gate * jax.nn.sigmoid(gate) * up).astype(o_ref.dtype)


def _mm_glu(a, wg, wu, tm, tn):
    M, K = a.shape
    N = wg.shape[1]
    return pl.pallas_call(
        _mm_glu_kernel,
        grid=(N // tn, M // tm),
        in_specs=[pl.BlockSpec((tm, K), lambda j, i: (i, 0)), pl.BlockSpec((K, tn), lambda j, i: (0, j)),
                  pl.BlockSpec((K, tn), lambda j, i: (0, j))],
        out_specs=pl.BlockSpec((tm, tn), lambda j, i: (i, j)),
        out_shape=jax.ShapeDtypeStruct((M, N), BF16),
        scratch_shapes=[pltpu.VMEM((K, tn), BF16), pltpu.VMEM((K, tn), BF16)],
        compiler_params=_params(("arbitrary", "arbitrary"),
                                _vmem_limit(tm * K * 2, 2 * K * tn * 4, tm * tn * 2,
                                            temps=2 * K * tn * 2 + 3 * tm * tn * 4)),
        name="mm_glu",
    )(a, wg, wu)


def _rope_kernel(z_ref, cos_ref, sin_ref, o_ref, *, n_rope_blocks, heads_per_block, k_heads_last):
    j = pl.program_id(1)
    cos = cos_ref[...]
    sin = sin_ref[...]
    lane = lax.broadcasted_iota(jnp.int32, cos.shape, 1)
    half = ROPE_DIMS // 2
    for h in range(heads_per_block):
        t = z_ref[:, h * HEAD_DIM:(h + 1) * HEAD_DIM]
        up = jnp.concatenate([t[:, half:], t[:, :half]], axis=1)
        down = jnp.concatenate([t[:, HEAD_DIM - half:], t[:, :HEAD_DIM - half]], axis=1)
        partner = jnp.where(lane < half, up, down)
        rot = t * cos + partner * sin
        if h >= k_heads_last:
            rot = jnp.where(j == n_rope_blocks - 1, t, rot)
        o_ref[:, h * HEAD_DIM:(h + 1) * HEAD_DIM] = rot.astype(o_ref.dtype)


def _rope_cast(z, cos_t, sin_t, off_q, width, kv_width):
    Lp = z.shape[0]
    tc = 2 * kv_width
    assert off_q % tc == 0 and width % tc == 0
    nblk = width // tc
    return pl.pallas_call(
        functools.partial(_rope_kernel, n_rope_blocks=nblk, heads_per_block=tc // HEAD_DIM,
                          k_heads_last=kv_width // HEAD_DIM),
        grid=(Lp // BLOCK, nblk),
        in_specs=[pl.BlockSpec((BLOCK, tc), lambda i, j: (i, off_q // tc + j)),
                  pl.BlockSpec((BLOCK, HEAD_DIM), lambda i, j: (i, 0)),
                  pl.BlockSpec((BLOCK, HEAD_DIM), lambda i, j: (i, 0))],
        out_specs=pl.BlockSpec((BLOCK, tc), lambda i, j: (i, j)),
        out_shape=jax.ShapeDtypeStruct((Lp, width), BF16),
        compiler_params=_params(("arbitrary", "arbitrary"), _vmem_limit(BLOCK * tc * 4, BLOCK * tc * 2)),
        name="rope_cast",
    )(z, cos_t, sin_t)


NEG_BIG = -1e30


def _attn_kernel(q_ref, kp_ref, kc_ref, kn_ref, km_ref, vp_ref, vc_ref, vn_ref, vm_ref, sink_ref, o_ref,
                 *, nb, group):
    i = pl.program_id(1)
    scale = HEAD_DIM ** -0.5
    zpad = jnp.zeros((BLOCK - N_META, HEAD_DIM), BF16)
    k_all = jnp.concatenate([kp_ref[...], kc_ref[...], kn_ref[...], km_ref[...], zpad], axis=0)
    v_all = jnp.concatenate([vp_ref[...], vc_ref[...], vn_ref[...], vm_ref[...], zpad], axis=0)
    q_all = jnp.concatenate([q_ref[:, g * HEAD_DIM:(g + 1) * HEAD_DIM] for g in range(group)], axis=0)
    s = _dot(q_all, k_all, NT) * scale
    rows = lax.broadcasted_iota(jnp.int32, s.shape, 0) % BLOCK
    cols = lax.broadcasted_iota(jnp.int32, s.shape, 1)
    rel = cols - BLOCK - rows
    kblk = i - 1 + cols // BLOCK
    band_ok = (cols < 3 * BLOCK) & (kblk >= 0) & (kblk < nb) & (jnp.abs(rel) <= WINDOW)
    meta_ok = (cols >= 3 * BLOCK) & (cols < 3 * BLOCK + N_META)
    s = jnp.where(band_ok | meta_ok, s, NEG_BIG)
    sink = sink_ref[:, 0:1]
    m = jnp.maximum(jnp.max(s, axis=-1, keepdims=True), sink)
    p = jnp.exp(s - m)
    denom = jnp.sum(p, axis=-1, keepdims=True) + jnp.exp(sink - m)
    o = _dot(p.astype(BF16), v_all) / denom
    for g in range(group):
        o_ref[:, g * HEAD_DIM:(g + 1) * HEAD_DIM] = o[g * BLOCK:(g + 1) * BLOCK, :].astype(o_ref.dtype)


def _attention(qkv, sink, S, n_heads, n_kv):
    group = n_heads // n_kv
    nb = S // BLOCK
    qw = group * HEAD_DIM
    kcol = n_heads
    vcol = n_heads + n_kv
    meta_blk = (S + PAD_ROWS - N_META) // N_META
    sink_rows = jnp.broadcast_to(sink.astype(F32).reshape(n_kv, group, 1, 1),
                                 (n_kv, group, BLOCK, LANES)).reshape(n_kv, group * BLOCK, LANES)

    def kv_specs(col0):
        return [pl.BlockSpec((BLOCK, HEAD_DIM), lambda h, i: (jnp.maximum(i - 1, 0), col0 + h)),
                pl.BlockSpec((BLOCK, HEAD_DIM), lambda h, i: (i, col0 + h)),
                pl.BlockSpec((BLOCK, HEAD_DIM), lambda h, i: (jnp.minimum(i + 1, nb - 1), col0 + h)),
                pl.BlockSpec((N_META, HEAD_DIM), lambda h, i: (meta_blk, col0 + h))]

    return pl.pallas_call(
        functools.partial(_attn_kernel, nb=nb, group=group),
        grid=(n_kv, nb),
        in_specs=[pl.BlockSpec((BLOCK, qw), lambda h, i: (i, h))] + kv_specs(kcol) + kv_specs(vcol)
                 + [pl.BlockSpec((None, group * BLOCK, LANES), lambda h, i: (h, 0, 0))],
        out_specs=pl.BlockSpec((BLOCK, qw), lambda h, i: (i, h)),
        out_shape=jax.ShapeDtypeStruct((S, n_heads * HEAD_DIM), BF16),
        compiler_params=_params(("arbitrary", "arbitrary"),
                                _vmem_limit(BLOCK * qw * 2 * 2, 8 * BLOCK * HEAD_DIM * 2,
                                            group * BLOCK * LANES * 4, temps=6 * group * BLOCK * 512 * 4)),
        name="attention",
    )(qkv, qkv, qkv, qkv, qkv, qkv, qkv, qkv, qkv, sink_rows)


def _rope(t, cos, sin):
    half = ROPE_DIMS // 2
    lane = lax.broadcasted_iota(jnp.int32, t.shape, 1)
    up = jnp.concatenate([t[:, half:], t[:, :half]], axis=1)
    down = jnp.concatenate([t[:, HEAD_DIM - half:], t[:, :HEAD_DIM - half]], axis=1)
    return t * cos + jnp.where(lane < half, up, down) * sin


def _attn_res_kernel(sink_ref, q_ref, k_ref, v_ref, cos_ref, sin_ref, o_ref, kb_ref, vb_ref,
                     *, nb, group, qblocks):
    h, i = pl.program_id(0), pl.program_id(1)
    lp = k_ref.shape[0]

    @pl.when(i == 0)
    def _():
        def body(c, carry):
            sl = pl.ds(pl.multiple_of(c * BLOCK, BLOCK), BLOCK)
            kb_ref[sl, :] = _rope(k_ref[sl, :], cos_ref[sl, :], sin_ref[sl, :]).astype(BF16)
            vb_ref[sl, :] = v_ref[sl, :].astype(BF16)
            return carry
        lax.fori_loop(0, lp // BLOCK, body, 0)

    zpad = jnp.zeros((BLOCK - N_META, HEAD_DIM), BF16)
    k_meta = jnp.concatenate([kb_ref[lp - N_META:lp, :], zpad], axis=0)
    v_meta = jnp.concatenate([vb_ref[lp - N_META:lp, :], zpad], axis=0)
    rows = lax.broadcasted_iota(jnp.int32, (BLOCK, 4 * BLOCK), 0)
    cols = lax.broadcasted_iota(jnp.int32, (BLOCK, 4 * BLOCK), 1)
    is_meta = (cols >= 3 * BLOCK) & (cols < 3 * BLOCK + N_META)
    scale = HEAD_DIM ** -0.5
    k_all, v_all, bias, q, sink = [], [], [], [], []
    for b in range(qblocks):
        qi = i * qblocks + b
        kb0 = jnp.clip(qi - 1, 0, nb - 3)
        win = pl.ds(pl.multiple_of(kb0 * BLOCK, BLOCK), 3 * BLOCK)
        k_b = jnp.concatenate([kb_ref[win, :], k_meta], axis=0)
        v_b = jnp.concatenate([vb_ref[win, :], v_meta], axis=0)
        rel = cols - rows + (kb0 - qi) * BLOCK
        ok = ((cols < 3 * BLOCK) & (jnp.abs(rel) <= WINDOW)) | is_meta
        bias_b = jnp.where(ok, 0.0, NEG_BIG).astype(F32)
        qrows = pl.ds(pl.multiple_of(qi * BLOCK, BLOCK), BLOCK)
        cos_q, sin_q = cos_ref[qrows, :], sin_ref[qrows, :]
        for g in range(group):
            k_all.append(k_b)
            v_all.append(v_b)
            bias.append(bias_b)
            q.append(_rope(q_ref[b * BLOCK:(b + 1) * BLOCK, g * HEAD_DIM:(g + 1) * HEAD_DIM],
                           cos_q, sin_q).astype(BF16))
            sink.append(sink_ref[h * group + g])
    n = range(len(q))
    s = [_dot(q[c], k_all[c], NT) * scale + bias[c] for c in n]
    m = [jnp.maximum(jnp.max(s[c], axis=-1, keepdims=True), sink[c]) for c in n]
    p = [jnp.exp(s[c] - m[c]) for c in n]
    denom = [jnp.sum(p[c], axis=-1, keepdims=True) + jnp.exp(sink[c] - m[c]) for c in n]
    o = [_dot(p[c].astype(BF16), v_all[c]) / denom[c] for c in n]
    for c in n:
        b, g = divmod(c, group)
        o_ref[b * BLOCK:(b + 1) * BLOCK, g * HEAD_DIM:(g + 1) * HEAD_DIM] = o[c].astype(o_ref.dtype)


def _attention_res(z, cos_t, sin_t, sink, S, n_heads, n_kv, off_q):
    Lp = z.shape[0]
    group = n_heads // n_kv
    nb = S // BLOCK
    assert nb >= 3
    qw = group * HEAD_DIM
    kcol = off_q // HEAD_DIM + n_heads
    vcol = kcol + n_kv
    col_bytes = Lp * HEAD_DIM * 4
    qblocks = _pick(nb, (2, 1))
    tq = qblocks * BLOCK
    return pl.pallas_call(
        functools.partial(_attn_res_kernel, nb=nb, group=group, qblocks=qblocks),
        grid_spec=pltpu.PrefetchScalarGridSpec(
            num_scalar_prefetch=1,
            grid=(n_kv, nb // qblocks),
            in_specs=[pl.BlockSpec((tq, qw), lambda h, i, s: (i, off_q // qw + h)),
                      pl.BlockSpec((Lp, HEAD_DIM), lambda h, i, s: (0, kcol + h)),
                      pl.BlockSpec((Lp, HEAD_DIM), lambda h, i, s: (0, vcol + h)),
                      pl.BlockSpec((Lp, HEAD_DIM), lambda h, i, s: (0, 0)),
                      pl.BlockSpec((Lp, HEAD_DIM), lambda h, i, s: (0, 0))],
            out_specs=pl.BlockSpec((tq, qw), lambda h, i, s: (i, h)),
            scratch_shapes=[pltpu.VMEM((Lp, HEAD_DIM), BF16), pltpu.VMEM((Lp, HEAD_DIM), BF16)]),
        out_shape=jax.ShapeDtypeStruct((S, n_heads * HEAD_DIM), BF16),
        compiler_params=_params(("arbitrary", "arbitrary"),
                                _vmem_limit(tq * qw * 4, 4 * col_bytes, tq * qw * 2,
                                            temps=col_bytes + 8 * group * tq * 512 * 4)),
        name="attention",
    )(sink.astype(F32), z, z, z, cos_t, sin_t)


def _split3(x):
    hi = x.astype(BF16)
    r1 = x - hi.astype(F32)
    mid = r1.astype(BF16)
    lo = (r1 - mid.astype(F32)).astype(BF16)
    return hi, mid, lo


def _split2(x):
    hi = x.astype(BF16)
    return hi, (x - hi.astype(F32)).astype(BF16)


def _dot_exact_rhs(x, m_bf16, dims=NN, left=False):
    parts = _split3(x)
    if left:
        return sum(_dot(m_bf16, p, dims) for p in parts)
    return sum(_dot(p, m_bf16, dims) for p in parts)


def _dotp(a, b, passes, dims=NN):
    if passes == 1:
        return _dot(a.astype(BF16), b.astype(BF16), dims)
    a_hi, a_lo = _split2(a)
    b_hi, b_lo = _split2(b)
    return _dot(a_hi, b_hi, dims) + (_dot(a_hi, b_lo, dims) + _dot(a_lo, b_hi, dims))


def _head_sum(x, ones_bd):
    outs = []
    for s in range(x.shape[1] // LANES):
        outs.append(_dot_exact_rhs(x[:, s * LANES:(s + 1) * LANES], ones_bd))
    return outs[0] if len(outs) == 1 else jnp.concatenate(outs, axis=1)


def _prep_kernel(r_ref, rp_ref, rn_ref, k_ref, kp_ref, kn_ref, v_ref, vp_ref, vn_ref,
                 lo_ref, lop_ref, lon_ref, gd_ref, gdp_ref, gdn_ref,
                 pc_ref, pl_ref, wl_ref, g2_ref, ones_ref,
                 ro_ref, vo_ref, kko_ref, lwf_ref, lwb_ref, kdf_ref, kdb_ref, bdf_ref, bdb_ref, go_ref, bo_ref,
                 *, n_tiles, tr, lora_tanh_cols):
    i = pl.program_id(1)

    def shifted(main_ref, prev_ref, next_ref, mup, mun):
        x = main_ref[...]
        row8 = lax.broadcasted_iota(jnp.int32, (8, x.shape[1]), 0)
        down = pltpu.roll(x, 1, 0)
        up = pltpu.roll(x, tr - 1, 0)
        prev = jnp.concatenate([jnp.where(row8 == 0, prev_ref[7:8, :], down[0:8]), down[8:]], axis=0)
        nxt = jnp.concatenate([up[:tr - 8], jnp.where(row8 == 7, next_ref[0:1, :], up[tr - 8:])], axis=0)
        return x + mup * (prev - x) + mun * (nxt - x)

    r = shifted(r_ref, rp_ref, rn_ref, pc_ref[0:1, :], pc_ref[1:2, :])
    k = shifted(k_ref, kp_ref, kn_ref, pc_ref[2:3, :], pc_ref[3:4, :])
    v = shifted(v_ref, vp_ref, vn_ref, pc_ref[4:5, :], pc_ref[5:6, :])
    lo = shifted(lo_ref, lop_ref, lon_ref, pl_ref[0:1, 0:LORA_PAD], pl_ref[1:2, 0:LORA_PAD])
    gd = shifted(gd_ref, gdp_ref, gdn_ref, pl_ref[0:1, LORA_PAD:], pl_ref[1:2, LORA_PAD:])

    row = lax.broadcasted_iota(jnp.int32, (tr, 1), 0)
    valid = jnp.where((i < n_tiles - 1) | (row >= tr - N_META), 1.0, 0.0).astype(F32)

    lane = lax.broadcasted_iota(jnp.int32, lo.shape, 1)
    lo_act = jnp.where(lane < lora_tanh_cols, jnp.tanh(lo), lo).astype(BF16)
    dec_f = _dot(lo_act, wl_ref[0])
    dec_b = _dot(lo_act, wl_ref[1])
    apre_f = _dot(lo_act, wl_ref[2])
    apre_b = _dot(lo_act, wl_ref[3])
    g = _dot(jax.nn.sigmoid(lo).astype(BF16), g2_ref[0]) + _dot(jax.nn.sigmoid(gd).astype(BF16), g2_ref[1])

    def log_decay(dec, w0):
        return -math.exp(-0.5) * jax.nn.sigmoid(w0 + dec)

    a_f = jax.nn.sigmoid(pc_ref[8:9, :] + apre_f)
    a_b = jax.nn.sigmoid(pc_ref[9:10, :] + apre_b)
    kk = k * pc_ref[10:11, :]
    ss = _head_sum(kk * kk, ones_ref[...])
    kk = kk / jnp.maximum(jnp.sqrt(ss), 1e-12) * valid
    k_a = pc_ref[11:12, :]
    kv = k * valid

    rv = r * valid
    vv = v * valid
    kd_f = kv * (1.0 + (a_f - 1.0) * k_a)
    kd_b = kv * (1.0 + (a_b - 1.0) * k_a)
    ro_ref[...] = rv
    vo_ref[...] = vv
    kko_ref[...] = kk
    lwf_ref[...] = log_decay(dec_f, pc_ref[6:7, :])
    lwb_ref[...] = log_decay(dec_b, pc_ref[7:8, :])
    kdf_ref[...] = kd_f
    kdb_ref[...] = kd_b
    bdf_ref[...] = kk * a_f
    bdb_ref[...] = kk * a_b
    go_ref[...] = g
    bo_ref[...] = _head_sum(rv * (kd_f + kd_b) * pc_ref[12:13, :], ones_ref[...]) * vv


def _rwkv_prep(z, off_r, C, pc, plo, wl, g2, ones_bd, lora_tanh_cols):
    Lp = z.shape[0]
    tr = BLOCK
    ct = _pick(C, (1024, 512))
    n_tiles = Lp // tr
    n8 = Lp // 8
    off_lo = off_r + 3 * C
    off_gd = off_lo + LORA_PAD
    assert off_r % ct == 0 and off_lo % LORA_PAD == 0

    def seg_specs(off, w, with_c):
        cb = off // w

        def col(c):
            return cb + c if with_c else cb
        return [pl.BlockSpec((tr, w), lambda c, i: (i, col(c))),
                pl.BlockSpec((8, w), lambda c, i: ((i * (tr // 8) + n8 - 1) % n8, col(c))),
                pl.BlockSpec((8, w), lambda c, i: (((i + 1) * (tr // 8)) % n8, col(c)))]

    in_specs = (seg_specs(off_r, ct, True) + seg_specs(off_r + C, ct, True) + seg_specs(off_r + 2 * C, ct, True)
                + seg_specs(off_lo, LORA_PAD, False) + seg_specs(off_gd, LORA_PAD, False)
                + [pl.BlockSpec((16, ct), lambda c, i: (0, c)),
                   pl.BlockSpec((8, 2 * LORA_PAD), lambda c, i: (0, 0)),
                   pl.BlockSpec((4, LORA_PAD, ct), lambda c, i: (0, 0, c)),
                   pl.BlockSpec((2, LORA_PAD, ct), lambda c, i: (0, 0, c)),
                   pl.BlockSpec((LANES, LANES), lambda c, i: (0, 0))])
    out_spec = pl.BlockSpec((tr, ct), lambda c, i: (i, c))
    out_sds = jax.ShapeDtypeStruct((Lp, C), F32)
    return pl.pallas_call(
        functools.partial(_prep_kernel, n_tiles=n_tiles, tr=tr, lora_tanh_cols=lora_tanh_cols),
        grid=(C // ct, n_tiles),
        in_specs=in_specs,
        out_specs=[out_spec] * 11,
        out_shape=[out_sds] * 11,
        compiler_params=_params(("arbitrary", "arbitrary"),
                                _vmem_limit(3 * tr * ct * 4, 2 * tr * LORA_PAD * 4, 4 * LORA_PAD * ct * 2,
                                            2 * LORA_PAD * ct * 2, 11 * tr * ct * 4, temps=24 * tr * ct * 4)),
        name="rwkv_prep",
    )(*([z] * 15), pc, plo, wl, g2, ones_bd)


SCAN_PAIRS = 8


def _scan_consts():
    C = CHUNK
    row = lax.broadcasted_iota(jnp.int32, (C, 2 * C), 0)
    col = lax.broadcasted_iota(jnp.int32, (C, 2 * C), 1) % C
    rr = lax.broadcasted_iota(jnp.int32, (LANES, LANES), 0)
    cc = lax.broadcasted_iota(jnp.int32, (LANES, LANES), 1)
    return dict(
        row=lax.broadcasted_iota(jnp.int32, (C, LANES), 0),
        head0=lax.broadcasted_iota(jnp.int32, (C, LANES), 1) < RWKV_HEAD,
        strict_f=col < row, strict_b=col > row, incl_f=col <= row, incl_b=col >= row,
        eye_sbs=jnp.where(col == row, 1.0, 0.0).astype(F32),
        same_head=(rr // RWKV_HEAD) == (cc // RWKV_HEAD), eye=rr == cc)


def _scan_chains(chains, cst):
    C = CHUNK
    head0, same, eye = cst["head0"], cst["same_head"], cst["eye"]
    n = range(len(chains))
    rev = [c[7] for c in chains]
    r, v, al, lw, k, be, h = ([c[i] for c in chains] for i in range(7))
    strict = [cst["strict_b" if x else "strict_f"] for x in rev]
    incl = [cst["incl_b" if x else "incl_f"] for x in rev]

    def bd(x):
        xb = x.astype(BF16)
        zero = jnp.zeros_like(xb)
        return jnp.concatenate([jnp.where(head0, xb, zero), jnp.where(head0, zero, xb)], axis=0)

    def mm(a, b_bf16, dims=NN):
        return _dot(a.astype(BF16), b_bf16, dims)

    row = cst["row"]
    cl = list(lw)
    sh = 1
    while sh < C:
        cl = [cl[i] + (jnp.where(row < C - sh, pltpu.roll(cl[i], C - sh, 0), 0.0) if rev[i] else
                       jnp.where(row >= sh, pltpu.roll(cl[i], sh, 0), 0.0)) for i in n]
        sh *= 2
    total = [cl[i][0:1, :] if rev[i] else cl[i][C - 1:C, :] for i in n]
    a_t = [al[i] * jnp.exp(cl[i] - lw[i]) for i in n]
    r_t = [r[i] * jnp.exp(cl[i]) for i in n]
    w_inv = [jnp.exp(-cl[i]) for i in n]
    w_rest = [jnp.exp(total[i] - cl[i]) for i in n]
    bd_v = [bd(v[i]) for i in n]
    sc = [mm(jnp.concatenate([a_t[i], r_t[i]], axis=0),
             jnp.concatenate([bd(k[i] * w_inv[i]), bd(be[i] * w_inv[i])], axis=0), NT) for i in n]
    a_ak = [jnp.where(strict[i], sc[i][0:C, 0:2 * C], 0.0) for i in n]
    n_ab = [jnp.where(strict[i], sc[i][0:C, 2 * C:], 0.0) for i in n]
    a_rk = [jnp.where(incl[i], sc[i][C:, 0:2 * C], 0.0) for i in n]
    a_rb = [jnp.where(incl[i], sc[i][C:, 2 * C:], 0.0) for i in n]

    t_inv = [cst["eye_sbs"] - n_ab[i] for i in n]
    sq = [mm(n_ab[i], bd(n_ab[i])) for i in n]
    n_round = 2
    while (1 << n_round) < C:
        both = [mm(jnp.concatenate([t_inv[i], sq[i]], axis=0), bd(sq[i])) for i in n]
        t_inv = [t_inv[i] + both[i][0:C] for i in n]
        sq = [both[i][C:] for i in n]
        n_round += 1
    t_inv = [t_inv[i] + mm(t_inv[i], bd(sq[i])) for i in n]

    akv = [mm(jnp.concatenate([a_ak[i], a_rk[i]], axis=0), bd_v[i]) for i in n]
    tp = [mm(t_inv[i], jnp.concatenate([bd(akv[i][0:C]), bd(a_t[i])], axis=1)) for i in n]
    p0 = [tp[i][:, 0:LANES] for i in n]
    at = [tp[i][:, LANES:] for i in n]
    rb = [mm(a_rb[i], jnp.concatenate([bd(p0[i]), bd(at[i])], axis=1)) for i in n]
    y0 = [akv[i][C:] - rb[i][:, 0:LANES] for i in n]
    rh = [r_t[i] - rb[i][:, LANES:] for i in n]
    bt = [mm(be[i] * w_rest[i], jnp.concatenate([at[i], p0[i]], axis=1).astype(BF16), TN) for i in n]
    kv = [mm(k[i] * w_rest[i], v[i].astype(BF16), TN) for i in n]
    m_mat = [jnp.where(same, jnp.where(eye, jnp.broadcast_to(jnp.exp(total[i]), (LANES, LANES)), 0.0)
                       - bt[i][:, 0:LANES], 0.0) for i in n]
    g_mat = [jnp.where(same, kv[i] - bt[i][:, LANES:], 0.0) for i in n]
    yh = [mm(jnp.concatenate([rh[i], m_mat[i]], axis=0), h[i].astype(BF16)) for i in n]
    return [(yh[i][0:C] + y0[i], yh[i][C:] + g_mat[i]) for i in n]


def _scan_kernel(rf_ref, vf_ref, af_ref, lwf_ref, kf_ref, bf_ref,
                 rb_ref, vb_ref, ab_ref, lwb_ref, kb_ref, bb_ref,
                 yf_ref, yb_ref, hf_ref, hb_ref, *, pairs):
    @pl.when(pl.program_id(1) == 0)
    def _():
        hf_ref[...] = jnp.zeros(hf_ref.shape, F32)
        hb_ref[...] = jnp.zeros(hb_ref.shape, F32)

    cst = _scan_consts()
    chains = []
    for g in range(pairs):
        sl = slice(g * LANES, (g + 1) * LANES)
        chains.append((rf_ref[:, sl], vf_ref[:, sl], af_ref[:, sl], lwf_ref[:, sl], kf_ref[:, sl], bf_ref[:, sl],
                       hf_ref[g], False))
        chains.append((rb_ref[:, sl], vb_ref[:, sl], ab_ref[:, sl], lwb_ref[:, sl], kb_ref[:, sl], bb_ref[:, sl],
                       hb_ref[g], True))
    res = _scan_chains(chains, cst)
    for g in range(pairs):
        sl = slice(g * LANES, (g + 1) * LANES)
        yf_ref[:, sl], hf_ref[g] = res[2 * g]
        yb_ref[:, sl], hb_ref[g] = res[2 * g + 1]


def _rwkv_scan(r, v, kk, lw_f, lw_b, kd_f, kd_b, bd_f, bd_b, S):
    Lp, C = r.shape
    n_real = S // CHUNK
    n_chunks = Lp // CHUNK
    steps = n_real + 1
    pairs = _pick(C // LANES, (SCAN_PAIRS, 2, 1))
    w = pairs * LANES

    def fwd(p, s):
        return ((s + n_real + 1) % n_chunks, p)

    def bwd(p, s):
        return (n_real - s, p)

    spec_f = pl.BlockSpec((CHUNK, w), fwd)
    spec_b = pl.BlockSpec((CHUNK, w), bwd)
    out_sds = jax.ShapeDtypeStruct((S + CHUNK, C), F32)
    out_f = pl.BlockSpec((CHUNK, w), lambda p, s: ((s + n_real) % (n_real + 1), p))
    return pl.pallas_call(
        functools.partial(_scan_kernel, pairs=pairs),
        grid=(C // w, steps),
        in_specs=[spec_f] * 6 + [spec_b] * 6,
        out_specs=[out_f, spec_b],
        out_shape=[out_sds, out_sds],
        scratch_shapes=[pltpu.VMEM((pairs, LANES, LANES), F32), pltpu.VMEM((pairs, LANES, LANES), F32)],
        compiler_params=_params(("arbitrary", "arbitrary"), 32 << 20),
        name="rwkv_scan",
    )(r, v, kk, lw_f, kd_f, bd_f, r, v, kk, lw_b, kd_b, bd_b)


def _post_kernel(yf_ref, yb_ref, bonus_ref, g_ref, pq_ref, ones_ref, o_ref):
    ones_bd = ones_ref[...]
    inv_n = 1.0 / RWKV_HEAD
    y = yf_ref[...] + yb_ref[...]
    mean = _head_sum(y, ones_bd) * inv_n
    yc = y - mean
    var = _head_sum(yc * yc, ones_bd) * inv_n
    yn = yc * lax.rsqrt(var + GN_EPS) * pq_ref[0:1, :] + pq_ref[1:2, :]
    o_ref[...] = ((yn + bonus_ref[...]) * g_ref[...]).astype(o_ref.dtype)


def _rwkv_post(y_f, y_b, bonus, g, pq, ones_bd, S):
    C = g.shape[1]
    tr = 512
    ct = _pick(C, (1024, 512))
    spec = pl.BlockSpec((tr, ct), lambda i, c: (i, c))
    return pl.pallas_call(
        _post_kernel,
        grid=(S // tr, C // ct),
        in_specs=[spec] * 4 + [pl.BlockSpec((8, ct), lambda i, c: (0, c)),
                               pl.BlockSpec((LANES, LANES), lambda i, c: (0, 0))],
        out_specs=spec,
        out_shape=jax.ShapeDtypeStruct((S, C), BF16),
        compiler_params=_params(("arbitrary", "arbitrary"),
                                _vmem_limit(4 * tr * ct * 4, tr * ct * 2, temps=16 * tr * ct * 4)),
        name="rwkv_post",
    )(y_f, y_b, bonus, g, pq, ones_bd)


def _pad_cols(w, width):
    return jnp.pad(w, ((0, 0), (0, width - w.shape[1])))


def kernel(x, meta_tokens, norm_mix_w, w_in, b_gate, mu_prev, mu_next, dec_w0, dec_w2, iclr_a0, iclr_a2,
           gate_w2, k_k, k_a, r_k, ln_x_w, ln_x_b, attn_sink, w_proj_attn, w_proj_rwkv, w_out, norm_ffn_w,
           w_ffn_gate, w_ffn_up, w_ffn_down, norm_final_w):
    B, S, D = x.shape
    assert B == 1 and norm_mix_w.shape[0] == 1, "one sequence, one layer"
    C = k_k.shape[-1]
    AW = w_proj_attn.shape[1]
    n_heads = attn_sink.shape[-1]
    DL, IL, GL = dec_w2.shape[2], iclr_a2.shape[2], gate_w2.shape[1]
    shift_w = mu_prev.shape[-1]
    KVW = (w_in.shape[-1] - 2 * D - AW - shift_w) // 2
    n_kv = KVW // HEAD_DIM
    F = w_ffn_gate.shape[-1]
    lora_w = 2 * DL + 2 * IL
    assert shift_w == 3 * C + lora_w + GL and lora_w <= LORA_PAD and GL <= LORA_PAD
    assert S % BLOCK == 0 and C % LANES == 0 and n_heads * HEAD_DIM == AW
    Lp = S + PAD_ROWS

    off_q = 2 * D
    off_r = off_q + AW + 2 * KVW
    off_lo = off_r + 3 * C
    tn_in = 1024
    n_in_pad = -(-w_in.shape[-1] // tn_in) * tn_in
    assert n_in_pad >= off_lo + 2 * LORA_PAD and off_lo % LORA_PAD == 0
    low_w = lora_w + GL
    gl_a = LORA_PAD - lora_w

    mu_p, mu_n = mu_prev[0], mu_next[0]
    zc = jnp.zeros((C,), F32)
    pc = jnp.stack([mu_p[:C], mu_n[:C], mu_p[C:2 * C], mu_n[C:2 * C], mu_p[2 * C:3 * C], mu_n[2 * C:3 * C],
                    dec_w0[0, 0], dec_w0[0, 1], iclr_a0[0, 0], iclr_a0[0, 1], k_k[0], k_a[0], r_k[0], zc, zc, zc])
    plo = jnp.zeros((8, 2 * LORA_PAD), F32).at[0, :low_w].set(mu_p[3 * C:]).at[1, :low_w].set(mu_n[3 * C:])
    wl = jnp.zeros((4, LORA_PAD, C), F32)
    wl = wl.at[0, 0:DL].set(dec_w2[0, 0]).at[1, DL:2 * DL].set(dec_w2[0, 1])
    wl = wl.at[2, 2 * DL:2 * DL + IL].set(iclr_a2[0, 0]).at[3, 2 * DL + IL:lora_w].set(iclr_a2[0, 1])
    wl = wl.astype(BF16)
    g2 = jnp.zeros((2, LORA_PAD, C), F32).at[0, lora_w:].set(gate_w2[0, :gl_a]).at[1, :GL - gl_a].set(
        gate_w2[0, gl_a:]).astype(BF16)
    pq = jnp.zeros((8, C), F32).at[0].set(ln_x_w[0]).at[1].set(ln_x_b[0])
    lane = jnp.arange(LANES)
    ones_bd = (lane[:, None] // RWKV_HEAD == lane[None, :] // RWKV_HEAD).astype(BF16)

    w_d = w_ffn_down[0].astype(BF16)

    pos = jnp.concatenate([jnp.arange(N_META, N_META + S), jnp.zeros((PAD_ROWS - N_META,), jnp.int32),
                           jnp.arange(N_META)]).astype(F32)
    inv = ROPE_THETA ** (-jnp.arange(0, ROPE_DIMS, 2, dtype=F32) / ROPE_DIMS)
    ang = pos[:, None] * inv[None, :]
    ones_rest = jnp.ones((Lp, HEAD_DIM - ROPE_DIMS), F32)
    cos_t = jnp.concatenate([jnp.cos(ang), jnp.cos(ang), ones_rest], axis=1)
    sin_t = jnp.concatenate([-jnp.sin(ang), jnp.sin(ang), 0.0 * ones_rest], axis=1)

    x2 = x[0]
    h_ext = _norm_ext(x2, meta_tokens.astype(x.dtype), norm_mix_w[0])
    z = _mm_in_s(h_ext, w_in[0], _pick(Lp, (1040, 640)), tn_in)
    o_att = _attention_res(z, cos_t, sin_t, attn_sink[0], S, n_heads, n_kv, off_q)
    r_s, v_s, kk, lw_f, lw_b, kd_f, kd_b, bd_f, bd_b, g, bonus = _rwkv_prep(
        z, off_r, C, pc, plo, wl, g2, ones_bd, 2 * DL)
    y_f, y_b = _rwkv_scan(r_s, v_s, kk, lw_f, lw_b, kd_f, kd_b, bd_f, bd_b, S)
    o_rwkv = _rwkv_post(y_f, y_b, bonus, g, pq, ones_bd, S)
    tm = _pick(S, (1024, 512))
    tn_d = _pick(D, (1024, 512))
    mixed = _mm_mix_s(o_att, w_proj_attn[0], o_rwkv, w_proj_rwkv[0], z, b_gate[0], S, D, 512, tn_d)
    h1 = _mm_out_s(mixed, w_out[0], x2, 512, tn_d)
    hn = _norm(h1, norm_ffn_w[0], BF16)
    act = _mm_glu_s(hn, w_ffn_gate[0], w_ffn_up[0], tm, 512)
    h2 = _mm_res(act, w_d, h1, tm, 512, F // 2)
    y = _norm(h2, norm_final_w, x.dtype)
    return y[None]
```

```python
import functools
import math

import jax
import jax.numpy as jnp
from jax import lax
from jax.experimental import pallas as pl
from jax.experimental.pallas import tpu as pltpu

F32 = jnp.float32
BF16 = jnp.bfloat16

N_META = 16
HEAD_DIM = 128
WINDOW = 128
BLOCK = 128
ROPE_DIMS = HEAD_DIM // 4
ROPE_THETA = 500000.0
RWKV_HEAD = 64
RMS_EPS = 1e-6
GN_EPS = 64e-5
LANES = 128
PAD_ROWS = BLOCK
CHUNK = 64
LORA_PAD = 512
VMEM_PHYSICAL = 64 * 1024 * 1024

NN = (((1,), (0,)), ((), ()))
NT = (((1,), (1,)), ((), ()))
TN = (((0,), (0,)), ((), ()))


def _vmem_limit(*block_bytes, temps=0):
    need = 2 * sum(block_bytes) + temps + (4 << 20)
    return int(min(max(need, 16 << 20), VMEM_PHYSICAL - (4 << 20)))


def _params(sem, limit):
    return pltpu.CompilerParams(dimension_semantics=sem, vmem_limit_bytes=limit)


def _pick(n, cands):
    for c in cands:
        if n % c == 0:
            return c
    raise ValueError(f"no tile in {cands} divides {n}")


def _rms(x, w):
    return x * lax.rsqrt(jnp.mean(x * x, axis=-1, keepdims=True) + RMS_EPS) * w


def _norm_ext_kernel(x_ref, meta_ref, w_ref, o_ref, *, n_steps, real_last):
    i = pl.program_id(0)
    tr = o_ref.shape[0]

    @pl.when(i < n_steps - 1)
    def _():
        o_ref[...] = _rms(x_ref[...], w_ref[...]).astype(o_ref.dtype)

    @pl.when(i == n_steps - 1)
    def _():
        if real_last:
            o_ref[0:real_last, :] = _rms(x_ref[0:real_last, :], w_ref[...]).astype(o_ref.dtype)
        o_ref[real_last:tr - N_META, :] = jnp.zeros((tr - N_META - real_last, o_ref.shape[1]), o_ref.dtype)
        o_ref[tr - N_META:, :] = _rms(meta_ref[...], w_ref[...]).astype(o_ref.dtype)


def _norm_ext(x, meta, w):
    S, D = x.shape
    Lp = S + PAD_ROWS
    tr = _pick(Lp, (640, BLOCK))
    n_steps = Lp // tr
    real_last = S - (n_steps - 1) * tr
    last_x_blk = (S - 1) // tr
    return pl.pallas_call(
        functools.partial(_norm_ext_kernel, n_steps=n_steps, real_last=real_last),
        grid=(n_steps,),
        in_specs=[pl.BlockSpec((tr, D), lambda i: (jnp.minimum(i, last_x_blk), 0)),
                  pl.BlockSpec((N_META, D), lambda i: (0, 0)),
                  pl.BlockSpec((1, D), lambda i: (0, 0))],
        out_specs=pl.BlockSpec((tr, D), lambda i: (i, 0)),
        out_shape=jax.ShapeDtypeStruct((Lp, D), BF16),
        compiler_params=_params(("arbitrary",), _vmem_limit(tr * D * 4, tr * D * 2, temps=2 * tr * D * 4)),
        name="norm_ext",
    )(x, meta, w.reshape(1, D))


def _norm_kernel(x_ref, w_ref, o_ref):
    o_ref[...] = _rms(x_ref[...], w_ref[...]).astype(o_ref.dtype)


def _norm(x, w, out_dtype, tr=512):
    S, D = x.shape
    return pl.pallas_call(
        _norm_kernel,
        grid=(S // tr,),
        in_specs=[pl.BlockSpec((tr, D), lambda i: (i, 0)), pl.BlockSpec((1, D), lambda i: (0, 0))],
        out_specs=pl.BlockSpec((tr, D), lambda i: (i, 0)),
        out_shape=jax.ShapeDtypeStruct((S, D), out_dtype),
        compiler_params=_params(("arbitrary",), _vmem_limit(tr * D * 4, tr * D * 4, temps=2 * tr * D * 4)),
        name="norm",
    )(x, w.reshape(1, D))


def _dot(a, b, dims=NN):
    return lax.dot_general(a, b, dims, preferred_element_type=F32)


def _cast_resident(src_ref, dst_ref, col0=None, n_valid=None):
    K = src_ref.shape[0]
    rows = _pick(K, (256, 128))

    def body(c, carry):
        sl = pl.ds(pl.multiple_of(c * rows, rows), rows)
        blk = src_ref[sl, :]
        if n_valid is not None:
            col = col0 + lax.broadcasted_iota(jnp.int32, blk.shape, 1)
            blk = jnp.where(col < n_valid, blk, 0.0)
        dst_ref[sl, :] = blk.astype(dst_ref.dtype)
        return carry

    lax.fori_loop(0, K // rows, body, 0)


def _mm_in_kernel(a_ref, b_ref, o_ref, bw_ref, *, n_valid):
    j, i = pl.program_id(0), pl.program_id(1)

    @pl.when(i == 0)
    def _():
        _cast_resident(b_ref, bw_ref, j * b_ref.shape[1], n_valid)

    o_ref[...] = _dot(a_ref[...], bw_ref[...])


def _mm_in(a, w, tm, tn):
    M, K = a.shape
    N = w.shape[1]
    nj = pl.cdiv(N, tn)
    return pl.pallas_call(
        functools.partial(_mm_in_kernel, n_valid=N),
        grid=(nj, M // tm),
        in_specs=[pl.BlockSpec((tm, K), lambda j, i: (i, 0)), pl.BlockSpec((K, tn), lambda j, i: (0, j))],
        out_specs=pl.BlockSpec((tm, tn), lambda j, i: (i, j)),
        out_shape=jax.ShapeDtypeStruct((M, nj * tn), F32),
        scratch_shapes=[pltpu.VMEM((K, tn), BF16)],
        compiler_params=_params(("arbitrary", "arbitrary"),
                                _vmem_limit(tm * K * 2, K * tn * 4, tm * tn * 4, temps=K * tn * 2 + tm * tn * 4)),
        name="mm_in",
    )(a, w)


def _mm_mix_kernel(a1_ref, b1_ref, a2_ref, b2_ref, z0_ref, z1_ref, bg_ref, o_ref, w1_ref, w2_ref):
    @pl.when(pl.program_id(1) == 0)
    def _():
        _cast_resident(b1_ref, w1_ref)
        _cast_resident(b2_ref, w2_ref)

    g0 = jax.nn.sigmoid(z0_ref[...] + bg_ref[0:1, :])
    g1 = jax.nn.sigmoid(z1_ref[...] + bg_ref[1:2, :])
    o_ref[...] = (g0 * _dot(a1_ref[...], w1_ref[...]) + g1 * _dot(a2_ref[...], w2_ref[...])).astype(o_ref.dtype)


def _mm_mix(o_att, w_pa, o_rwkv, w_pr, z, b_gate, S, D, tm, tn):
    K1, K2 = o_att.shape[1], o_rwkv.shape[1]
    nj = D // tn
    return pl.pallas_call(
        _mm_mix_kernel,
        grid=(nj, S // tm),
        in_specs=[pl.BlockSpec((tm, K1), lambda j, i: (i, 0)), pl.BlockSpec((K1, tn), lambda j, i: (0, j)),
                  pl.BlockSpec((tm, K2), lambda j, i: (i, 0)), pl.BlockSpec((K2, tn), lambda j, i: (0, j)),
                  pl.BlockSpec((tm, tn), lambda j, i: (i, j)), pl.BlockSpec((tm, tn), lambda j, i: (i, nj + j)),
                  pl.BlockSpec((2, tn), lambda j, i: (0, j))],
        out_specs=pl.BlockSpec((tm, tn), lambda j, i: (i, j)),
        out_shape=jax.ShapeDtypeStruct((S, D), BF16),
        scratch_shapes=[pltpu.VMEM((K1, tn), BF16), pltpu.VMEM((K2, tn), BF16)],
        compiler_params=_params(("arbitrary", "arbitrary"),
                                _vmem_limit(tm * K1 * 2, K1 * tn * 4, tm * K2 * 2, K2 * tn * 4,
                                            2 * tm * tn * 4, tm * tn * 2,
                                            temps=(K1 + K2) * tn * 2 + 3 * tm * tn * 4)),
        name="mm_mix",
    )(o_att, w_pa, o_rwkv, w_pr, z, z, b_gate)


def _mm_resw_kernel(a_ref, b_ref, r_ref, o_ref, bw_ref):
    @pl.when(pl.program_id(1) == 0)
    def _():
        _cast_resident(b_ref, bw_ref)

    o_ref[...] = r_ref[...] + _dot(a_ref[...], bw_ref[...])


def _mm_resw(a, w, res, tm, tn):
    M, K = a.shape
    N = w.shape[1]
    return pl.pallas_call(
        _mm_resw_kernel,
        grid=(N // tn, M // tm),
        in_specs=[pl.BlockSpec((tm, K), lambda j, i: (i, 0)), pl.BlockSpec((K, tn), lambda j, i: (0, j)),
                  pl.BlockSpec((tm, tn), lambda j, i: (i, j))],
        out_specs=pl.BlockSpec((tm, tn), lambda j, i: (i, j)),
        out_shape=jax.ShapeDtypeStruct((M, N), F32),
        scratch_shapes=[pltpu.VMEM((K, tn), BF16)],
        compiler_params=_params(("arbitrary", "arbitrary"),
                                _vmem_limit(tm * K * 2, K * tn * 4, 2 * tm * tn * 4, temps=K * tn * 2 + tm * tn * 4)),
        name="mm_out",
    )(a, w, res)


def _weight_copy(w_hbm, land_ref, sem, jj, tn, width):
    return pltpu.make_async_copy(w_hbm.at[:, pl.ds(pl.multiple_of(jj * tn, LANES), width)],
                                 land_ref.at[:, pl.ds(0, width)], sem)


def _weight_copy_op(op, w_hbm, land_ref, sem, jj, tn, nj, n_cols):
    rem = n_cols - (nj - 1) * tn
    if rem == tn:
        getattr(_weight_copy(w_hbm, land_ref, sem, jj, tn, tn), op)()
        return

    @pl.when(jj < nj - 1)
    def _():
        getattr(_weight_copy(w_hbm, land_ref, sem, jj, tn, tn), op)()

    @pl.when(jj == nj - 1)
    def _():
        getattr(_weight_copy(w_hbm, land_ref, sem, jj, tn, rem), op)()


def _stream_weights(weights, j, i, nj, tn, n_cols):
    @pl.when(i == 0)
    def _():
        @pl.when(j == 0)
        def _():
            for w_hbm, land_ref, _, sem in weights:
                _weight_copy_op("start", w_hbm, land_ref, sem, j, tn, nj, n_cols)

        for w_hbm, land_ref, bw_ref, sem in weights:
            _weight_copy_op("wait", w_hbm, land_ref, sem, j, tn, nj, n_cols)
            _cast_resident(land_ref, bw_ref, j * tn, n_cols if n_cols % tn else None)

        @pl.when(j + 1 < nj)
        def _():
            for w_hbm, land_ref, _, sem in weights:
                _weight_copy_op("start", w_hbm, land_ref, sem, j + 1, tn, nj, n_cols)


def _weight_scratch(ks, tn):
    return ([pltpu.VMEM((k, tn), F32) for k in ks] + [pltpu.VMEM((k, tn), BF16) for k in ks]
            + [pltpu.SemaphoreType.DMA(()) for _ in ks])


def _mm_in_s_kernel(a_ref, w_hbm, o_ref, land_ref, bw_ref, sem, *, nj, tn, n_cols):
    j, i = pl.program_id(0), pl.program_id(1)
    _stream_weights([(w_hbm, land_ref, bw_ref, sem)], j, i, nj, tn, n_cols)
    o_ref[...] = _dot(a_ref[...], bw_ref[...])


def _mm_in_s(a, w, tm, tn):
    M, K = a.shape
    N = w.shape[1]
    nj = pl.cdiv(N, tn)
    return pl.pallas_call(
        functools.partial(_mm_in_s_kernel, nj=nj, tn=tn, n_cols=N),
        grid=(nj, M // tm),
        in_specs=[pl.BlockSpec((tm, K), lambda j, i: (i, 0)), pl.BlockSpec(memory_space=pl.ANY)],
        out_specs=pl.BlockSpec((tm, tn), lambda j, i: (i, j)),
        out_shape=jax.ShapeDtypeStruct((M, nj * tn), F32),
        scratch_shapes=_weight_scratch((K,), tn),
        compiler_params=_params(("arbitrary", "arbitrary"),
                                _vmem_limit(tm * K * 2, tm * tn * 4, temps=K * tn * 6 + 2 * tm * tn * 4)),
        name="mm_in",
    )(a, w)


def _mm_mix_s_kernel(a1_ref, a2_ref, z0_ref, z1_ref, bg_ref, w1_hbm, w2_hbm, o_ref,
                     l1_ref, l2_ref, b1_ref, b2_ref, s1, s2, *, nj, tn, n_cols):
    j, i = pl.program_id(0), pl.program_id(1)
    _stream_weights([(w1_hbm, l1_ref, b1_ref, s1), (w2_hbm, l2_ref, b2_ref, s2)], j, i, nj, tn, n_cols)
    g0 = jax.nn.sigmoid(z0_ref[...] + bg_ref[0:1, :])
    g1 = jax.nn.sigmoid(z1_ref[...] + bg_ref[1:2, :])
    o_ref[...] = (g0 * _dot(a1_ref[...], b1_ref[...]) + g1 * _dot(a2_ref[...], b2_ref[...])).astype(o_ref.dtype)


def _mm_mix_s(o_att, w_pa, o_rwkv, w_pr, z, b_gate, S, D, tm, tn):
    K1, K2 = o_att.shape[1], o_rwkv.shape[1]
    assert D % tn == 0
    nj = D // tn
    return pl.pallas_call(
        functools.partial(_mm_mix_s_kernel, nj=nj, tn=tn, n_cols=D),
        grid=(nj, S // tm),
        in_specs=[pl.BlockSpec((tm, K1), lambda j, i: (i, 0)), pl.BlockSpec((tm, K2), lambda j, i: (i, 0)),
                  pl.BlockSpec((tm, tn), lambda j, i: (i, j)), pl.BlockSpec((tm, tn), lambda j, i: (i, nj + j)),
                  pl.BlockSpec((2, tn), lambda j, i: (0, j)),
                  pl.BlockSpec(memory_space=pl.ANY), pl.BlockSpec(memory_space=pl.ANY)],
        out_specs=pl.BlockSpec((tm, tn), lambda j, i: (i, j)),
        out_shape=jax.ShapeDtypeStruct((S, D), BF16),
        scratch_shapes=_weight_scratch((K1, K2), tn),
        compiler_params=_params(("arbitrary", "arbitrary"),
                                _vmem_limit(tm * (K1 + K2) * 2, 2 * tm * tn * 4, tm * tn * 2,
                                            temps=(K1 + K2) * tn * 6 + 3 * tm * tn * 4)),
        name="mm_mix",
    )(o_att, o_rwkv, z, z, b_gate, w_pa, w_pr)


def _mm_out_s_kernel(a_ref, r_ref, w_hbm, o_ref, land_ref, bw_ref, sem, *, nj, tn, n_cols):
    j, i = pl.program_id(0), pl.program_id(1)
    _stream_weights([(w_hbm, land_ref, bw_ref, sem)], j, i, nj, tn, n_cols)
    o_ref[...] = r_ref[...] + _dot(a_ref[...], bw_ref[...])


def _mm_out_s(a, w, res, tm, tn):
    M, K = a.shape
    N = w.shape[1]
    assert N % tn == 0
    nj = N // tn
    return pl.pallas_call(
        functools.partial(_mm_out_s_kernel, nj=nj, tn=tn, n_cols=N),
        grid=(nj, M // tm),
        in_specs=[pl.BlockSpec((tm, K), lambda j, i: (i, 0)), pl.BlockSpec((tm, tn), lambda j, i: (i, j)),
                  pl.BlockSpec(memory_space=pl.ANY)],
        out_specs=pl.BlockSpec((tm, tn), lambda j, i: (i, j)),
        out_shape=jax.ShapeDtypeStruct((M, N), F32),
        scratch_shapes=_weight_scratch((K,), tn),
        compiler_params=_params(("arbitrary", "arbitrary"),
                                _vmem_limit(tm * K * 2, 2 * tm * tn * 4, temps=K * tn * 6 + tm * tn * 4)),
        name="mm_out",
    )(a, res, w)


def _mm_glu_s_kernel(a_ref, wg_hbm, wu_hbm, o_ref, lg_ref, lu_ref, bg_ref, bu_ref, sg, su, *, nj, tn, n_cols):
    j, i = pl.program_id(0), pl.program_id(1)
    _stream_weights([(wg_hbm, lg_ref, bg_ref, sg), (wu_hbm, lu_ref, bu_ref, su)], j, i, nj, tn, n_cols)
    a = a_ref[...]
    gate = _dot(a, bg_ref[...])
    up = _dot(a, bu_ref[...])
    o_ref[...] = (gate * jax.nn.sigmoid(gate) * up).astype(o_ref.dtype)


def _mm_glu_s(a, wg, wu, tm, tn):
    M, K = a.shape
    N = wg.shape[1]
    nj = pl.cdiv(N, tn)
    return pl.pallas_call(
        functools.partial(_mm_glu_s_kernel, nj=nj, tn=tn, n_cols=N),
        grid=(nj, M // tm),
        in_specs=[pl.BlockSpec((tm, K), lambda j, i: (i, 0)),
                  pl.BlockSpec(memory_space=pl.ANY), pl.BlockSpec(memory_space=pl.ANY)],
        out_specs=pl.BlockSpec((tm, tn), lambda j, i: (i, j)),
        out_shape=jax.ShapeDtypeStruct((M, N), BF16),
        scratch_shapes=_weight_scratch((K, K), tn),
        compiler_params=_params(("arbitrary", "arbitrary"),
                                _vmem_limit(tm * K * 2, tm * tn * 2, temps=2 * K * tn * 6 + 3 * tm * tn * 4)),
        name="mm_glu",
    )(a, wg, wu)


def _mm_res_kernel(a_ref, b_ref, r_ref, o_ref, *, nk):
    k = pl.program_id(2)
    if nk == 1:
        o_ref[...] = r_ref[...] + _dot(a_ref[...], b_ref[...])
    else:
        @pl.when(k == 0)
        def _():
            o_ref[...] = r_ref[...] + _dot(a_ref[...], b_ref[...])

        @pl.when(k > 0)
        def _():
            o_ref[...] = o_ref[...] + _dot(a_ref[...], b_ref[...])


def _mm_res(a, b, res, tm, tn, tk):
    M, K = a.shape
    N = b.shape[1]
    nk = K // tk
    return pl.pallas_call(
        functools.partial(_mm_res_kernel, nk=nk),
        grid=(M // tm, N // tn, nk),
        in_specs=[pl.BlockSpec((tm, tk), lambda i, j, k: (i, k)), pl.BlockSpec((tk, tn), lambda i, j, k: (k, j)),
                  pl.BlockSpec((tm, tn), lambda i, j, k: (i, j))],
        out_specs=pl.BlockSpec((tm, tn), lambda i, j, k: (i, j)),
        out_shape=jax.ShapeDtypeStruct((M, N), F32),
        compiler_params=_params(("arbitrary", "arbitrary", "arbitrary"),
                                _vmem_limit(tm * tk * 2, tk * tn * 2, 2 * tm * tn * 4, temps=tm * tn * 4)),
        name="mm_down",
    )(a, b, res)


def _mm_glu_kernel(a_ref, bg_ref, bu_ref, o_ref, wg_ref, wu_ref):
    @pl.when(pl.program_id(1) == 0)
    def _():
        _cast_resident(bg_ref, wg_ref)
        _cast_resident(bu_ref, wu_ref)

    a = a_ref[...]
    gate = _dot(a, wg_ref[...])
    up = _dot(a, wu_ref[...])
    o_ref[...] = (gate * jax.nn.sigmoid(gate) * up).astype(o_ref.dtype)


def _mm_glu(a, wg, wu, tm, tn):
    M, K = a.shape
    N = wg.shape[1]
    return pl.pallas_call(
        _mm_glu_kernel,
        grid=(N // tn, M // tm),
        in_specs=[pl.BlockSpec((tm, K), lambda j, i: (i, 0)), pl.BlockSpec((K, tn), lambda j, i: (0, j)),
                  pl.BlockSpec((K, tn), lambda j, i: (0, j))],
        out_specs=pl.BlockSpec((tm, tn), lambda j, i: (i, j)),
        out_shape=jax.ShapeDtypeStruct((M, N), BF16),
        scratch_shapes=[pltpu.VMEM((K, tn), BF16), pltpu.VMEM((K, tn), BF16)],
        compiler_params=_params(("arbitrary", "arbitrary"),
                                _vmem_limit(tm * K * 2, 2 * K * tn * 4, tm * tn * 2,
                                            temps=2 * K * tn * 2 + 3 * tm * tn * 4)),
        name="mm_glu",
    )(a, wg, wu)


def _rope_kernel(z_ref, cos_ref, sin_ref, o_ref, *, n_rope_blocks, heads_per_block, k_heads_last):
    j = pl.program_id(1)
    cos = cos_ref[...]
    sin = sin_ref[...]
    lane = lax.broadcasted_iota(jnp.int32, cos.shape, 1)
    half = ROPE_DIMS // 2
    for h in range(heads_per_block):
        t = z_ref[:, h * HEAD_DIM:(h + 1) * HEAD_DIM]
        up = jnp.concatenate([t[:, half:], t[:, :half]], axis=1)
        down = jnp.concatenate([t[:, HEAD_DIM - half:], t[:, :HEAD_DIM - half]], axis=1)
        partner = jnp.where(lane < half, up, down)
        rot = t * cos + partner * sin
        if h >= k_heads_last:
            rot = jnp.where(j == n_rope_blocks - 1, t, rot)
        o_ref[:, h * HEAD_DIM:(h + 1) * HEAD_DIM] = rot.astype(o_ref.dtype)


def _rope_cast(z, cos_t, sin_t, off_q, width, kv_width):
    Lp = z.shape[0]
    tc = 2 * kv_width
    assert off_q % tc == 0 and width % tc == 0
    nblk = width // tc
    return pl.pallas_call(
        functools.partial(_rope_kernel, n_rope_blocks=nblk, heads_per_block=tc // HEAD_DIM,
                          k_heads_last=kv_width // HEAD_DIM),
        grid=(Lp // BLOCK, nblk),
        in_specs=[pl.BlockSpec((BLOCK, tc), lambda i, j: (i, off_q // tc + j)),
                  pl.BlockSpec((BLOCK, HEAD_DIM), lambda i, j: (i, 0)),
                  pl.BlockSpec((BLOCK, HEAD_DIM), lambda i, j: (i, 0))],
        out_specs=pl.BlockSpec((BLOCK, tc), lambda i, j: (i, j)),
        out_shape=jax.ShapeDtypeStruct((Lp, width), BF16),
        compiler_params=_params(("arbitrary", "arbitrary"), _vmem_limit(BLOCK * tc * 4, BLOCK * tc * 2)),
        name="rope_cast",
    )(z, cos_t, sin_t)


NEG_BIG = -1e30


def _attn_kernel(q_ref, kp_ref, kc_ref, kn_ref, km_ref, vp_ref, vc_ref, vn_ref, vm_ref, sink_ref, o_ref,
                 *, nb, group):
    i = pl.program_id(1)
    scale = HEAD_DIM ** -0.5
    zpad = jnp.zeros((BLOCK - N_META, HEAD_DIM), BF16)
    k_all = jnp.concatenate([kp_ref[...], kc_ref[...], kn_ref[...], km_ref[...], zpad], axis=0)
    v_all = jnp.concatenate([vp_ref[...], vc_ref[...], vn_ref[...], vm_ref[...], zpad], axis=0)
    q_all = jnp.concatenate([q_ref[:, g * HEAD_DIM:(g + 1) * HEAD_DIM] for g in range(group)], axis=0)
    s = _dot(q_all, k_all, NT) * scale
    rows = lax.broadcasted_iota(jnp.int32, s.shape, 0) % BLOCK
    cols = lax.broadcasted_iota(jnp.int32, s.shape, 1)
    rel = cols - BLOCK - rows
    kblk = i - 1 + cols // BLOCK
    band_ok = (cols < 3 * BLOCK) & (kblk >= 0) & (kblk < nb) & (jnp.abs(rel) <= WINDOW)
    meta_ok = (cols >= 3 * BLOCK) & (cols < 3 * BLOCK + N_META)
    s = jnp.where(band_ok | meta_ok, s, NEG_BIG)
    sink = sink_ref[:, 0:1]
    m = jnp.maximum(jnp.max(s, axis=-1, keepdims=True), sink)
    p = jnp.exp(s - m)
    denom = jnp.sum(p, axis=-1, keepdims=True) + jnp.exp(sink - m)
    o = _dot(p.astype(BF16), v_all) / denom
    for g in range(group):
        o_ref[:, g * HEAD_DIM:(g + 1) * HEAD_DIM] = o[g * BLOCK:(g + 1) * BLOCK, :].astype(o_ref.dtype)


def _attention(qkv, sink, S, n_heads, n_kv):
    group = n_heads // n_kv
    nb = S // BLOCK
    qw = group * HEAD_DIM
    kcol = n_heads
    vcol = n_heads + n_kv
    meta_blk = (S + PAD_ROWS - N_META) // N_META
    sink_rows = jnp.broadcast_to(sink.astype(F32).reshape(n_kv, group, 1, 1),
                                 (n_kv, group, BLOCK, LANES)).reshape(n_kv, group * BLOCK, LANES)

    def kv_specs(col0):
        return [pl.BlockSpec((BLOCK, HEAD_DIM), lambda h, i: (jnp.maximum(i - 1, 0), col0 + h)),
                pl.BlockSpec((BLOCK, HEAD_DIM), lambda h, i: (i, col0 + h)),
                pl.BlockSpec((BLOCK, HEAD_DIM), lambda h, i: (jnp.minimum(i + 1, nb - 1), col0 + h)),
                pl.BlockSpec((N_META, HEAD_DIM), lambda h, i: (meta_blk, col0 + h))]

    return pl.pallas_call(
        functools.partial(_attn_kernel, nb=nb, group=group),
        grid=(n_kv, nb),
        in_specs=[pl.BlockSpec((BLOCK, qw), lambda h, i: (i, h))] + kv_specs(kcol) + kv_specs(vcol)
                 + [pl.BlockSpec((None, group * BLOCK, LANES), lambda h, i: (h, 0, 0))],
        out_specs=pl.BlockSpec((BLOCK, qw), lambda h, i: (i, h)),
        out_shape=jax.ShapeDtypeStruct((S, n_heads * HEAD_DIM), BF16),
        compiler_params=_params(("arbitrary", "arbitrary"),
                                _vmem_limit(BLOCK * qw * 2 * 2, 8 * BLOCK * HEAD_DIM * 2,
                                            group * BLOCK * LANES * 4, temps=6 * group * BLOCK * 512 * 4)),
        name="attention",
    )(qkv, qkv, qkv, qkv, qkv, qkv, qkv, qkv, qkv, sink_rows)


def _rope(t, cos, sin):
    half = ROPE_DIMS // 2
    lane = lax.broadcasted_iota(jnp.int32, t.shape, 1)
    up = jnp.concatenate([t[:, half:], t[:, :half]], axis=1)
    down = jnp.concatenate([t[:, HEAD_DIM - half:], t[:, :HEAD_DIM - half]], axis=1)
    return t * cos + jnp.where(lane < half, up, down) * sin


def _attn_res_kernel(sink_ref, q_ref, k_ref, v_ref, cos_ref, sin_ref, o_ref, kb_ref, vb_ref,
                     *, nb, group, qblocks):
    h, i = pl.program_id(0), pl.program_id(1)
    lp = k_ref.shape[0]

    @pl.when(i == 0)
    def _():
        def body(c, carry):
            sl = pl.ds(pl.multiple_of(c * BLOCK, BLOCK), BLOCK)
            kb_ref[sl, :] = _rope(k_ref[sl, :], cos_ref[sl, :], sin_ref[sl, :]).astype(BF16)
            vb_ref[sl, :] = v_ref[sl, :].astype(BF16)
            return carry
        lax.fori_loop(0, lp // BLOCK, body, 0)

    zpad = jnp.zeros((BLOCK - N_META, HEAD_DIM), BF16)
    k_meta = jnp.concatenate([kb_ref[lp - N_META:lp, :], zpad], axis=0)
    v_meta = jnp.concatenate([vb_ref[lp - N_META:lp, :], zpad], axis=0)
    rows = lax.broadcasted_iota(jnp.int32, (BLOCK, 4 * BLOCK), 0)
    cols = lax.broadcasted_iota(jnp.int32, (BLOCK, 4 * BLOCK), 1)
    is_meta = (cols >= 3 * BLOCK) & (cols < 3 * BLOCK + N_META)
    scale = HEAD_DIM ** -0.5
    k_all, v_all, bias, q, sink = [], [], [], [], []
    for b in range(qblocks):
        qi = i * qblocks + b
        kb0 = jnp.clip(qi - 1, 0, nb - 3)
        win = pl.ds(pl.multiple_of(kb0 * BLOCK, BLOCK), 3 * BLOCK)
        k_b = jnp.concatenate([kb_ref[win, :], k_meta], axis=0)
        v_b = jnp.concatenate([vb_ref[win, :], v_meta], axis=0)
        rel = cols - rows + (kb0 - qi) * BLOCK
        ok = ((cols < 3 * BLOCK) & (jnp.abs(rel) <= WINDOW)) | is_meta
        bias_b = jnp.where(ok, 0.0, NEG_BIG).astype(F32)
        qrows = pl.ds(pl.multiple_of(qi * BLOCK, BLOCK), BLOCK)
        cos_q, sin_q = cos_ref[qrows, :], sin_ref[qrows, :]
        for g in range(group):
            k_all.append(k_b)
            v_all.append(v_b)
            bias.append(bias_b)
            q.append(_rope(q_ref[b * BLOCK:(b + 1) * BLOCK, g * HEAD_DIM:(g + 1) * HEAD_DIM],
                           cos_q, sin_q).astype(BF16))
            sink.append(sink_ref[h * group + g])
    n = range(len(q))
    s = [_dot(q[c], k_all[c], NT) * scale + bias[c] for c in n]
    m = [jnp.maximum(jnp.max(s[c], axis=-1, keepdims=True), sink[c]) for c in n]
    p = [jnp.exp(s[c] - m[c]) for c in n]
    denom = [jnp.sum(p[c], axis=-1, keepdims=True) + jnp.exp(sink[c] - m[c]) for c in n]
    o = [_dot(p[c].astype(BF16), v_all[c]) / denom[c] for c in n]
    for c in n:
        b, g = divmod(c, group)
        o_ref[b * BLOCK:(b + 1) * BLOCK, g * HEAD_DIM:(g + 1) * HEAD_DIM] = o[c].astype(o_ref.dtype)


def _attention_res(z, cos_t, sin_t, sink, S, n_heads, n_kv, off_q):
    Lp = z.shape[0]
    group = n_heads // n_kv
    nb = S // BLOCK
    assert nb >= 3
    qw = group * HEAD_DIM
    kcol = off_q // HEAD_DIM + n_heads
    vcol = kcol + n_kv
    col_bytes = Lp * HEAD_DIM * 4
    qblocks = _pick(nb, (4, 2, 1))
    tq = qblocks * BLOCK
    return pl.pallas_call(
        functools.partial(_attn_res_kernel, nb=nb, group=group, qblocks=qblocks),
        grid_spec=pltpu.PrefetchScalarGridSpec(
            num_scalar_prefetch=1,
            grid=(n_kv, nb // qblocks),
            in_specs=[pl.BlockSpec((tq, qw), lambda h, i, s: (i, off_q // qw + h)),
                      pl.BlockSpec((Lp, HEAD_DIM), lambda h, i, s: (0, kcol + h)),
                      pl.BlockSpec((Lp, HEAD_DIM), lambda h, i, s: (0, vcol + h)),
                      pl.BlockSpec((Lp, HEAD_DIM), lambda h, i, s: (0, 0)),
                      pl.BlockSpec((Lp, HEAD_DIM), lambda h, i, s: (0, 0))],
            out_specs=pl.BlockSpec((tq, qw), lambda h, i, s: (i, h)),
            scratch_shapes=[pltpu.VMEM((Lp, HEAD_DIM), BF16), pltpu.VMEM((Lp, HEAD_DIM), BF16)]),
        out_shape=jax.ShapeDtypeStruct((S, n_heads * HEAD_DIM), BF16),
        compiler_params=_params(("arbitrary", "arbitrary"),
                                _vmem_limit(tq * qw * 4, 4 * col_bytes, tq * qw * 2,
                                            temps=col_bytes + 8 * group * tq * 512 * 4)),
        name="attention",
    )(sink.astype(F32), z, z, z, cos_t, sin_t)


def _split3(x):
    hi = x.astype(BF16)
    r1 = x - hi.astype(F32)
    mid = r1.astype(BF16)
    lo = (r1 - mid.astype(F32)).astype(BF16)
    return hi, mid, lo


def _split2(x):
    hi = x.astype(BF16)
    return hi, (x - hi.astype(F32)).astype(BF16)


def _dot_exact_rhs(x, m_bf16, dims=NN, left=False):
    parts = _split3(x)
    if left:
        return sum(_dot(m_bf16, p, dims) for p in parts)
    return sum(_dot(p, m_bf16, dims) for p in parts)


def _dotp(a, b, passes, dims=NN):
    if passes == 1:
        return _dot(a.astype(BF16), b.astype(BF16), dims)
    a_hi, a_lo = _split2(a)
    b_hi, b_lo = _split2(b)
    return _dot(a_hi, b_hi, dims) + (_dot(a_hi, b_lo, dims) + _dot(a_lo, b_hi, dims))


def _head_sum(x, ones_bd):
    outs = []
    for s in range(x.shape[1] // LANES):
        outs.append(_dot_exact_rhs(x[:, s * LANES:(s + 1) * LANES], ones_bd))
    return outs[0] if len(outs) == 1 else jnp.concatenate(outs, axis=1)


def _prep_kernel(r_ref, rp_ref, rn_ref, k_ref, kp_ref, kn_ref, v_ref, vp_ref, vn_ref,
                 lo_ref, lop_ref, lon_ref, gd_ref, gdp_ref, gdn_ref,
                 pc_ref, pl_ref, wl_ref, g2_ref, ones_ref,
                 ro_ref, vo_ref, kko_ref, lwf_ref, lwb_ref, kdf_ref, kdb_ref, bdf_ref, bdb_ref, go_ref, bo_ref,
                 *, n_tiles, tr, lora_tanh_cols):
    i = pl.program_id(1)

    def shifted(main_ref, prev_ref, next_ref, mup, mun):
        x = main_ref[...]
        row8 = lax.broadcasted_iota(jnp.int32, (8, x.shape[1]), 0)
        down = pltpu.roll(x, 1, 0)
        up = pltpu.roll(x, tr - 1, 0)
        prev = jnp.concatenate([jnp.where(row8 == 0, prev_ref[7:8, :], down[0:8]), down[8:]], axis=0)
        nxt = jnp.concatenate([up[:tr - 8], jnp.where(row8 == 7, next_ref[0:1, :], up[tr - 8:])], axis=0)
        return x + mup * (prev - x) + mun * (nxt - x)

    r = shifted(r_ref, rp_ref, rn_ref, pc_ref[0:1, :], pc_ref[1:2, :])
    k = shifted(k_ref, kp_ref, kn_ref, pc_ref[2:3, :], pc_ref[3:4, :])
    v = shifted(v_ref, vp_ref, vn_ref, pc_ref[4:5, :], pc_ref[5:6, :])
    lo = shifted(lo_ref, lop_ref, lon_ref, pl_ref[0:1, 0:LORA_PAD], pl_ref[1:2, 0:LORA_PAD])
    gd = shifted(gd_ref, gdp_ref, gdn_ref, pl_ref[0:1, LORA_PAD:], pl_ref[1:2, LORA_PAD:])

    row = lax.broadcasted_iota(jnp.int32, (tr, 1), 0)
    valid = jnp.where((i < n_tiles - 1) | (row >= tr - N_META), 1.0, 0.0).astype(F32)

    lane = lax.broadcasted_iota(jnp.int32, lo.shape, 1)
    lo_act = jnp.where(lane < lora_tanh_cols, jnp.tanh(lo), lo).astype(BF16)
    dec_f = _dot(lo_act, wl_ref[0])
    dec_b = _dot(lo_act, wl_ref[1])
    apre_f = _dot(lo_act, wl_ref[2])
    apre_b = _dot(lo_act, wl_ref[3])
    g = _dot(jax.nn.sigmoid(lo).astype(BF16), g2_ref[0]) + _dot(jax.nn.sigmoid(gd).astype(BF16), g2_ref[1])

    def log_decay(dec, w0):
        return -math.exp(-0.5) * jax.nn.sigmoid(w0 + dec)

    a_f = jax.nn.sigmoid(pc_ref[8:9, :] + apre_f)
    a_b = jax.nn.sigmoid(pc_ref[9:10, :] + apre_b)
    kk = k * pc_ref[10:11, :]
    ss = _head_sum(kk * kk, ones_ref[...])
    kk = kk / jnp.maximum(jnp.sqrt(ss), 1e-12) * valid
    k_a = pc_ref[11:12, :]
    kv = k * valid

    rv = r * valid
    vv = v * valid
    kd_f = kv * (1.0 + (a_f - 1.0) * k_a)
    kd_b = kv * (1.0 + (a_b - 1.0) * k_a)
    ro_ref[...] = rv
    vo_ref[...] = vv
    kko_ref[...] = kk
    lwf_ref[...] = log_decay(dec_f, pc_ref[6:7, :])
    lwb_ref[...] = log_decay(dec_b, pc_ref[7:8, :])
    kdf_ref[...] = kd_f
    kdb_ref[...] = kd_b
    bdf_ref[...] = kk * a_f
    bdb_ref[...] = kk * a_b
    go_ref[...] = g
    bo_ref[...] = _head_sum(rv * (kd_f + kd_b) * pc_ref[12:13, :], ones_ref[...]) * vv


def _rwkv_prep(z, off_r, C, pc, plo, wl, g2, ones_bd, lora_tanh_cols):
    Lp = z.shape[0]
    tr = BLOCK
    ct = _pick(C, (1024, 512))
    n_tiles = Lp // tr
    n8 = Lp // 8
    off_lo = off_r + 3 * C
    off_gd = off_lo + LORA_PAD
    assert off_r % ct == 0 and off_lo % LORA_PAD == 0

    def seg_specs(off, w, with_c):
        cb = off // w

        def col(c):
            return cb + c if with_c else cb
        return [pl.BlockSpec((tr, w), lambda c, i: (i, col(c))),
                pl.BlockSpec((8, w), lambda c, i: ((i * (tr // 8) + n8 - 1) % n8, col(c))),
                pl.BlockSpec((8, w), lambda c, i: (((i + 1) * (tr // 8)) % n8, col(c)))]

    in_specs = (seg_specs(off_r, ct, True) + seg_specs(off_r + C, ct, True) + seg_specs(off_r + 2 * C, ct, True)
                + seg_specs(off_lo, LORA_PAD, False) + seg_specs(off_gd, LORA_PAD, False)
                + [pl.BlockSpec((16, ct), lambda c, i: (0, c)),
                   pl.BlockSpec((8, 2 * LORA_PAD), lambda c, i: (0, 0)),
                   pl.BlockSpec((4, LORA_PAD, ct), lambda c, i: (0, 0, c)),
                   pl.BlockSpec((2, LORA_PAD, ct), lambda c, i: (0, 0, c)),
                   pl.BlockSpec((LANES, LANES), lambda c, i: (0, 0))])
    out_spec = pl.BlockSpec((tr, ct), lambda c, i: (i, c))
    out_sds = jax.ShapeDtypeStruct((Lp, C), F32)
    return pl.pallas_call(
        functools.partial(_prep_kernel, n_tiles=n_tiles, tr=tr, lora_tanh_cols=lora_tanh_cols),
        grid=(C // ct, n_tiles),
        in_specs=in_specs,
        out_specs=[out_spec] * 11,
        out_shape=[out_sds] * 11,
        compiler_params=_params(("arbitrary", "arbitrary"),
                                _vmem_limit(3 * tr * ct * 4, 2 * tr * LORA_PAD * 4, 4 * LORA_PAD * ct * 2,
                                            2 * LORA_PAD * ct * 2, 11 * tr * ct * 4, temps=24 * tr * ct * 4)),
        name="rwkv_prep",
    )(*([z] * 15), pc, plo, wl, g2, ones_bd)


SCAN_PAIRS = 8


def _scan_consts():
    C = CHUNK
    row = lax.broadcasted_iota(jnp.int32, (C, 2 * C), 0)
    col = lax.broadcasted_iota(jnp.int32, (C, 2 * C), 1) % C
    rr = lax.broadcasted_iota(jnp.int32, (LANES, LANES), 0)
    cc = lax.broadcasted_iota(jnp.int32, (LANES, LANES), 1)
    return dict(
        row=lax.broadcasted_iota(jnp.int32, (C, LANES), 0),
        head0=lax.broadcasted_iota(jnp.int32, (C, LANES), 1) < RWKV_HEAD,
        strict_f=col < row, strict_b=col > row, incl_f=col <= row, incl_b=col >= row,
        eye_sbs=jnp.where(col == row, 1.0, 0.0).astype(F32),
        same_head=(rr // RWKV_HEAD) == (cc // RWKV_HEAD), eye=rr == cc)


def _scan_chains(chains, cst):
    C = CHUNK
    head0, same, eye = cst["head0"], cst["same_head"], cst["eye"]
    n = range(len(chains))
    rev = [c[7] for c in chains]
    r, v, al, lw, k, be, h = ([c[i] for c in chains] for i in range(7))
    strict = [cst["strict_b" if x else "strict_f"] for x in rev]
    incl = [cst["incl_b" if x else "incl_f"] for x in rev]

    def bd(x):
        xb = x.astype(BF16)
        zero = jnp.zeros_like(xb)
        return jnp.concatenate([jnp.where(head0, xb, zero), jnp.where(head0, zero, xb)], axis=0)

    def mm(a, b_bf16, dims=NN):
        return _dot(a.astype(BF16), b_bf16, dims)

    row = cst["row"]
    cl = list(lw)
    sh = 1
    while sh < C:
        cl = [cl[i] + (jnp.where(row < C - sh, pltpu.roll(cl[i], C - sh, 0), 0.0) if rev[i] else
                       jnp.where(row >= sh, pltpu.roll(cl[i], sh, 0), 0.0)) for i in n]
        sh *= 2
    total = [cl[i][0:1, :] if rev[i] else cl[i][C - 1:C, :] for i in n]
    a_t = [al[i] * jnp.exp(cl[i] - lw[i]) for i in n]
    r_t = [r[i] * jnp.exp(cl[i]) for i in n]
    w_inv = [jnp.exp(-cl[i]) for i in n]
    w_rest = [jnp.exp(total[i] - cl[i]) for i in n]
    bd_v = [bd(v[i]) for i in n]
    sc = [mm(jnp.concatenate([a_t[i], r_t[i]], axis=0),
             jnp.concatenate([bd(k[i] * w_inv[i]), bd(be[i] * w_inv[i])], axis=0), NT) for i in n]
    a_ak = [jnp.where(strict[i], sc[i][0:C, 0:2 * C], 0.0) for i in n]
    n_ab = [jnp.where(strict[i], sc[i][0:C, 2 * C:], 0.0) for i in n]
    a_rk = [jnp.where(incl[i], sc[i][C:, 0:2 * C], 0.0) for i in n]
    a_rb = [jnp.where(incl[i], sc[i][C:, 2 * C:], 0.0) for i in n]

    t_inv = [cst["eye_sbs"] - n_ab[i] for i in n]
    sq = [mm(n_ab[i], bd(n_ab[i])) for i in n]
    n_round = 2
    while (1 << n_round) < C:
        both = [mm(jnp.concatenate([t_inv[i], sq[i]], axis=0), bd(sq[i])) for i in n]
        t_inv = [t_inv[i] + both[i][0:C] for i in n]
        sq = [both[i][C:] for i in n]
        n_round += 1
    t_inv = [t_inv[i] + mm(t_inv[i], bd(sq[i])) for i in n]

    akv = [mm(jnp.concatenate([a_ak[i], a_rk[i]], axis=0), bd_v[i]) for i in n]
    tp = [mm(t_inv[i], jnp.concatenate([bd(akv[i][0:C]), bd(a_t[i])], axis=1)) for i in n]
    p0 = [tp[i][:, 0:LANES] for i in n]
    at = [tp[i][:, LANES:] for i in n]
    rb = [mm(a_rb[i], jnp.concatenate([bd(p0[i]), bd(at[i])], axis=1)) for i in n]
    y0 = [akv[i][C:] - rb[i][:, 0:LANES] for i in n]
    rh = [r_t[i] - rb[i][:, LANES:] for i in n]
    bt = [mm(be[i] * w_rest[i], jnp.concatenate([at[i], p0[i]], axis=1).astype(BF16), TN) for i in n]
    kv = [mm(k[i] * w_rest[i], v[i].astype(BF16), TN) for i in n]
    m_mat = [jnp.where(same, jnp.where(eye, jnp.broadcast_to(jnp.exp(total[i]), (LANES, LANES)), 0.0)
                       - bt[i][:, 0:LANES], 0.0) for i in n]
    g_mat = [jnp.where(same, kv[i] - bt[i][:, LANES:], 0.0) for i in n]
    yh = [mm(jnp.concatenate([rh[i], m_mat[i]], axis=0), h[i].astype(BF16)) for i in n]
    return [(yh[i][0:C] + y0[i], yh[i][C:] + g_mat[i]) for i in n]


def _scan_kernel(rf_ref, vf_ref, af_ref, lwf_ref, kf_ref, bf_ref,
                 rb_ref, vb_ref, ab_ref, lwb_ref, kb_ref, bb_ref, wd_ref,
                 yf_ref, yb_ref, wdo_ref, hf_ref, hb_ref, *, pairs):
    @pl.when(pl.program_id(1) == 0)
    def _():
        hf_ref[...] = jnp.zeros(hf_ref.shape, F32)
        hb_ref[...] = jnp.zeros(hb_ref.shape, F32)

    wdo_ref[...] = wd_ref[...].astype(wdo_ref.dtype)

    cst = _scan_consts()
    chains = []
    for g in range(pairs):
        sl = slice(g * LANES, (g + 1) * LANES)
        chains.append((rf_ref[:, sl], vf_ref[:, sl], af_ref[:, sl], lwf_ref[:, sl], kf_ref[:, sl], bf_ref[:, sl],
                       hf_ref[g], False))
        chains.append((rb_ref[:, sl], vb_ref[:, sl], ab_ref[:, sl], lwb_ref[:, sl], kb_ref[:, sl], bb_ref[:, sl],
                       hb_ref[g], True))
    res = _scan_chains(chains, cst)
    for g in range(pairs):
        sl = slice(g * LANES, (g + 1) * LANES)
        yf_ref[:, sl], hf_ref[g] = res[2 * g]
        yb_ref[:, sl], hb_ref[g] = res[2 * g + 1]


def _rwkv_scan(r, v, kk, lw_f, lw_b, kd_f, kd_b, bd_f, bd_b, w_cast, S):
    Lp, C = r.shape
    n_real = S // CHUNK
    n_chunks = Lp // CHUNK
    steps = n_real + 1
    pairs = _pick(C // LANES, (SCAN_PAIRS, 2, 1))
    w = pairs * LANES
    total_steps = (C // w) * steps
    wr, wc = w_cast.shape
    cast_rows = next(rb for rb in (128, 256, 512, 1024, wr) if wr % rb == 0 and wr // rb <= total_steps)
    n_cast = wr // cast_rows
    cast_spec = pl.BlockSpec((cast_rows, wc), lambda p, s: (jnp.minimum(p * steps + s, n_cast - 1), 0))

    def fwd(p, s):
        return ((s + n_real + 1) % n_chunks, p)

    def bwd(p, s):
        return (n_real - s, p)

    spec_f = pl.BlockSpec((CHUNK, w), fwd)
    spec_b = pl.BlockSpec((CHUNK, w), bwd)
    out_sds = jax.ShapeDtypeStruct((S + CHUNK, C), F32)
    out_f = pl.BlockSpec((CHUNK, w), lambda p, s: ((s + n_real) % (n_real + 1), p))
    return pl.pallas_call(
        functools.partial(_scan_kernel, pairs=pairs),
        grid=(C // w, steps),
        in_specs=[spec_f] * 6 + [spec_b] * 6 + [cast_spec],
        out_specs=[out_f, spec_b, cast_spec],
        out_shape=[out_sds, out_sds, jax.ShapeDtypeStruct((wr, wc), BF16)],
        scratch_shapes=[pltpu.VMEM((pairs, LANES, LANES), F32), pltpu.VMEM((pairs, LANES, LANES), F32)],
        compiler_params=_params(("arbitrary", "arbitrary"),
                                _vmem_limit(14 * CHUNK * w * 4, cast_rows * wc * 6, temps=16 << 20)),
        name="rwkv_scan",
    )(r, v, kk, lw_f, kd_f, bd_f, r, v, kk, lw_b, kd_b, bd_b, w_cast)


def _post_kernel(yf_ref, yb_ref, bonus_ref, g_ref, pq_ref, ones_ref, o_ref):
    ones_bd = ones_ref[...]
    inv_n = 1.0 / RWKV_HEAD
    y = yf_ref[...] + yb_ref[...]
    mean = _head_sum(y, ones_bd) * inv_n
    yc = y - mean
    var = _head_sum(yc * yc, ones_bd) * inv_n
    yn = yc * lax.rsqrt(var + GN_EPS) * pq_ref[0:1, :] + pq_ref[1:2, :]
    o_ref[...] = ((yn + bonus_ref[...]) * g_ref[...]).astype(o_ref.dtype)


def _rwkv_post(y_f, y_b, bonus, g, pq, ones_bd, S):
    C = g.shape[1]
    tr = 512
    ct = _pick(C, (1024, 512))
    spec = pl.BlockSpec((tr, ct), lambda i, c: (i, c))
    return pl.pallas_call(
        _post_kernel,
        grid=(S // tr, C // ct),
        in_specs=[spec] * 4 + [pl.BlockSpec((8, ct), lambda i, c: (0, c)),
                               pl.BlockSpec((LANES, LANES), lambda i, c: (0, 0))],
        out_specs=spec,
        out_shape=jax.ShapeDtypeStruct((S, C), BF16),
        compiler_params=_params(("arbitrary", "arbitrary"),
                                _vmem_limit(4 * tr * ct * 4, tr * ct * 2, temps=16 * tr * ct * 4)),
        name="rwkv_post",
    )(y_f, y_b, bonus, g, pq, ones_bd)


def _pad_cols(w, width):
    return jnp.pad(w, ((0, 0), (0, width - w.shape[1])))


def kernel(x, meta_tokens, norm_mix_w, w_in, b_gate, mu_prev, mu_next, dec_w0, dec_w2, iclr_a0, iclr_a2,
           gate_w2, k_k, k_a, r_k, ln_x_w, ln_x_b, attn_sink, w_proj_attn, w_proj_rwkv, w_out, norm_ffn_w,
           w_ffn_gate, w_ffn_up, w_ffn_down, norm_final_w):
    B, S, D = x.shape
    assert B == 1 and norm_mix_w.shape[0] == 1, "one sequence, one layer"
    C = k_k.shape[-1]
    AW = w_proj_attn.shape[1]
    n_heads = attn_sink.shape[-1]
    DL, IL, GL = dec_w2.shape[2], iclr_a2.shape[2], gate_w2.shape[1]
    shift_w = mu_prev.shape[-1]
    KVW = (w_in.shape[-1] - 2 * D - AW - shift_w) // 2
    n_kv = KVW // HEAD_DIM
    F = w_ffn_gate.shape[-1]
    lora_w = 2 * DL + 2 * IL
    assert shift_w == 3 * C + lora_w + GL and lora_w <= LORA_PAD and GL <= LORA_PAD
    assert S % BLOCK == 0 and C % LANES == 0 and n_heads * HEAD_DIM == AW
    Lp = S + PAD_ROWS

    off_q = 2 * D
    off_r = off_q + AW + 2 * KVW
    off_lo = off_r + 3 * C
    tn_in = 1024
    n_in_pad = -(-w_in.shape[-1] // tn_in) * tn_in
    assert n_in_pad >= off_lo + 2 * LORA_PAD and off_lo % LORA_PAD == 0
    low_w = lora_w + GL
    gl_a = LORA_PAD - lora_w

    mu_p, mu_n = mu_prev[0], mu_next[0]
    zc = jnp.zeros((C,), F32)
    pc = jnp.stack([mu_p[:C], mu_n[:C], mu_p[C:2 * C], mu_n[C:2 * C], mu_p[2 * C:3 * C], mu_n[2 * C:3 * C],
                    dec_w0[0, 0], dec_w0[0, 1], iclr_a0[0, 0], iclr_a0[0, 1], k_k[0], k_a[0], r_k[0], zc, zc, zc])
    plo = jnp.zeros((8, 2 * LORA_PAD), F32).at[0, :low_w].set(mu_p[3 * C:]).at[1, :low_w].set(mu_n[3 * C:])
    wl = jnp.zeros((4, LORA_PAD, C), F32)
    wl = wl.at[0, 0:DL].set(dec_w2[0, 0]).at[1, DL:2 * DL].set(dec_w2[0, 1])
    wl = wl.at[2, 2 * DL:2 * DL + IL].set(iclr_a2[0, 0]).at[3, 2 * DL + IL:lora_w].set(iclr_a2[0, 1])
    wl = wl.astype(BF16)
    g2 = jnp.zeros((2, LORA_PAD, C), F32).at[0, lora_w:].set(gate_w2[0, :gl_a]).at[1, :GL - gl_a].set(
        gate_w2[0, gl_a:]).astype(BF16)
    pq = jnp.zeros((8, C), F32).at[0].set(ln_x_w[0]).at[1].set(ln_x_b[0])
    lane = jnp.arange(LANES)
    ones_bd = (lane[:, None] // RWKV_HEAD == lane[None, :] // RWKV_HEAD).astype(BF16)


    pos = jnp.concatenate([jnp.arange(N_META, N_META + S), jnp.zeros((PAD_ROWS - N_META,), jnp.int32),
                           jnp.arange(N_META)]).astype(F32)
    inv = ROPE_THETA ** (-jnp.arange(0, ROPE_DIMS, 2, dtype=F32) / ROPE_DIMS)
    ang = pos[:, None] * inv[None, :]
    ones_rest = jnp.ones((Lp, HEAD_DIM - ROPE_DIMS), F32)
    cos_t = jnp.concatenate([jnp.cos(ang), jnp.cos(ang), ones_rest], axis=1)
    sin_t = jnp.concatenate([-jnp.sin(ang), jnp.sin(ang), 0.0 * ones_rest], axis=1)

    x2 = x[0]
    h_ext = _norm_ext(x2, meta_tokens.astype(x.dtype), norm_mix_w[0])
    z = _mm_in_s(h_ext, w_in[0], _pick(Lp, (1040, 640)), tn_in)
    o_att = _attention_res(z, cos_t, sin_t, attn_sink[0], S, n_heads, n_kv, off_q)
    r_s, v_s, kk, lw_f, lw_b, kd_f, kd_b, bd_f, bd_b, g, bonus = _rwkv_prep(
        z, off_r, C, pc, plo, wl, g2, ones_bd, 2 * DL)
    y_f, y_b, w_d = _rwkv_scan(r_s, v_s, kk, lw_f, lw_b, kd_f, kd_b, bd_f, bd_b, w_ffn_down[0], S)
    o_rwkv = _rwkv_post(y_f, y_b, bonus, g, pq, ones_bd, S)
    tm = _pick(S, (1024, 512))
    tn_d = _pick(D, (1024, 512))
    mixed = _mm_mix_s(o_att, w_proj_attn[0], o_rwkv, w_proj_rwkv[0], z, b_gate[0], S, D, 512, tn_d)
    h1 = _mm_out_s(mixed, w_out[0], x2, 512, tn_d)
    hn = _norm(h1, norm_ffn_w[0], BF16)
    act = _mm_glu_s(hn, w_ffn_gate[0], w_ffn_up[0], tm, 512)
    h2 = _mm_res(act, w_d, h1, tm, 512, F // 2)
    y = _norm(h2, norm_final_w, x.dtype)
    return y[None]
```

```python
import functools
import math

import jax
import jax.numpy as jnp
from jax import lax
from jax.experimental import pallas as pl
from jax.experimental.pallas import tpu as pltpu

F32 = jnp.float32
BF16 = jnp.bfloat16

N_META = 16
HEAD_DIM = 128
WINDOW = 128
BLOCK = 128
ROPE_DIMS = HEAD_DIM // 4
ROPE_THETA = 500000.0
RWKV_HEAD = 64
RMS_EPS = 1e-6
GN_EPS = 64e-5
LANES = 128
PAD_ROWS = BLOCK
CHUNK = 64
LORA_PAD = 512
VMEM_PHYSICAL = 64 * 1024 * 1024

NN = (((1,), (0,)), ((), ()))
NT = (((1,), (1,)), ((), ()))
TN = (((0,), (0,)), ((), ()))


def _vmem_limit(*block_bytes, temps=0):
    need = 2 * sum(block_bytes) + temps + (4 << 20)
    return int(min(max(need, 16 << 20), VMEM_PHYSICAL - (4 << 20)))


def _params(sem, limit):
    return pltpu.CompilerParams(dimension_semantics=sem, vmem_limit_bytes=limit)


def _pick(n, cands):
    for c in cands:
        if n % c == 0:
            return c
    raise ValueError(f"no tile in {cands} divides {n}")


def _rms(x, w):
    return x * lax.rsqrt(jnp.mean(x * x, axis=-1, keepdims=True) + RMS_EPS) * w


def _norm_ext_kernel(x_ref, meta_ref, w_ref, o_ref, *, n_steps, real_last):
    i = pl.program_id(0)
    tr = o_ref.shape[0]

    @pl.when(i < n_steps - 1)
    def _():
        o_ref[...] = _rms(x_ref[...], w_ref[...]).astype(o_ref.dtype)

    @pl.when(i == n_steps - 1)
    def _():
        if real_last:
            o_ref[0:real_last, :] = _rms(x_ref[0:real_last, :], w_ref[...]).astype(o_ref.dtype)
        o_ref[real_last:tr - N_META, :] = jnp.zeros((tr - N_META - real_last, o_ref.shape[1]), o_ref.dtype)
        o_ref[tr - N_META:, :] = _rms(meta_ref[...], w_ref[...]).astype(o_ref.dtype)


def _norm_ext(x, meta, w):
    S, D = x.shape
    Lp = S + PAD_ROWS
    tr = _pick(Lp, (640, BLOCK))
    n_steps = Lp // tr
    real_last = S - (n_steps - 1) * tr
    last_x_blk = (S - 1) // tr
    return pl.pallas_call(
        functools.partial(_norm_ext_kernel, n_steps=n_steps, real_last=real_last),
        grid=(n_steps,),
        in_specs=[pl.BlockSpec((tr, D), lambda i: (jnp.minimum(i, last_x_blk), 0)),
                  pl.BlockSpec((N_META, D), lambda i: (0, 0)),
                  pl.BlockSpec((1, D), lambda i: (0, 0))],
        out_specs=pl.BlockSpec((tr, D), lambda i: (i, 0)),
        out_shape=jax.ShapeDtypeStruct((Lp, D), BF16),
        compiler_params=_params(("arbitrary",), _vmem_limit(tr * D * 4, tr * D * 2, temps=2 * tr * D * 4)),
        name="norm_ext",
    )(x, meta, w.reshape(1, D))


def _norm_kernel(x_ref, w_ref, o_ref):
    o_ref[...] = _rms(x_ref[...], w_ref[...]).astype(o_ref.dtype)


def _norm(x, w, out_dtype, tr=512):
    S, D = x.shape
    return pl.pallas_call(
        _norm_kernel,
        grid=(S // tr,),
        in_specs=[pl.BlockSpec((tr, D), lambda i: (i, 0)), pl.BlockSpec((1, D), lambda i: (0, 0))],
        out_specs=pl.BlockSpec((tr, D), lambda i: (i, 0)),
        out_shape=jax.ShapeDtypeStruct((S, D), out_dtype),
        compiler_params=_params(("arbitrary",), _vmem_limit(tr * D * 4, tr * D * 4, temps=2 * tr * D * 4)),
        name="norm",
    )(x, w.reshape(1, D))


def _dot(a, b, dims=NN):
    return lax.dot_general(a, b, dims, preferred_element_type=F32)


def _cast_resident(src_ref, dst_ref, col0=None, n_valid=None):
    K = src_ref.shape[0]
    rows = _pick(K, (256, 128))

    def body(c, carry):
        sl = pl.ds(pl.multiple_of(c * rows, rows), rows)
        blk = src_ref[sl, :]
        if n_valid is not None:
            col = col0 + lax.broadcasted_iota(jnp.int32, blk.shape, 1)
            blk = jnp.where(col < n_valid, blk, 0.0)
        dst_ref[sl, :] = blk.astype(dst_ref.dtype)
        return carry

    lax.fori_loop(0, K // rows, body, 0)


def _mm_in_kernel(a_ref, b_ref, o_ref, bw_ref, *, n_valid):
    j, i = pl.program_id(0), pl.program_id(1)

    @pl.when(i == 0)
    def _():
        _cast_resident(b_ref, bw_ref, j * b_ref.shape[1], n_valid)

    o_ref[...] = _dot(a_ref[...], bw_ref[...])


def _mm_in(a, w, tm, tn):
    M, K = a.shape
    N = w.shape[1]
    nj = pl.cdiv(N, tn)
    return pl.pallas_call(
        functools.partial(_mm_in_kernel, n_valid=N),
        grid=(nj, M // tm),
        in_specs=[pl.BlockSpec((tm, K), lambda j, i: (i, 0)), pl.BlockSpec((K, tn), lambda j, i: (0, j))],
        out_specs=pl.BlockSpec((tm, tn), lambda j, i: (i, j)),
        out_shape=jax.ShapeDtypeStruct((M, nj * tn), F32),
        scratch_shapes=[pltpu.VMEM((K, tn), BF16)],
        compiler_params=_params(("arbitrary", "arbitrary"),
                                _vmem_limit(tm * K * 2, K * tn * 4, tm * tn * 4, temps=K * tn * 2 + tm * tn * 4)),
        name="mm_in",
    )(a, w)


def _mm_mix_kernel(a1_ref, b1_ref, a2_ref, b2_ref, z0_ref, z1_ref, bg_ref, o_ref, w1_ref, w2_ref):
    @pl.when(pl.program_id(1) == 0)
    def _():
        _cast_resident(b1_ref, w1_ref)
        _cast_resident(b2_ref, w2_ref)

    g0 = jax.nn.sigmoid(z0_ref[...] + bg_ref[0:1, :])
    g1 = jax.nn.sigmoid(z1_ref[...] + bg_ref[1:2, :])
    o_ref[...] = (g0 * _dot(a1_ref[...], w1_ref[...]) + g1 * _dot(a2_ref[...], w2_ref[...])).astype(o_ref.dtype)


def _mm_mix(o_att, w_pa, o_rwkv, w_pr, z, b_gate, S, D, tm, tn):
    K1, K2 = o_att.shape[1], o_rwkv.shape[1]
    nj = D // tn
    return pl.pallas_call(
        _mm_mix_kernel,
        grid=(nj, S // tm),
        in_specs=[pl.BlockSpec((tm, K1), lambda j, i: (i, 0)), pl.BlockSpec((K1, tn), lambda j, i: (0, j)),
                  pl.BlockSpec((tm, K2), lambda j, i: (i, 0)), pl.BlockSpec((K2, tn), lambda j, i: (0, j)),
                  pl.BlockSpec((tm, tn), lambda j, i: (i, j)), pl.BlockSpec((tm, tn), lambda j, i: (i, nj + j)),
                  pl.BlockSpec((2, tn), lambda j, i: (0, j))],
        out_specs=pl.BlockSpec((tm, tn), lambda j, i: (i, j)),
        out_shape=jax.ShapeDtypeStruct((S, D), BF16),
        scratch_shapes=[pltpu.VMEM((K1, tn), BF16), pltpu.VMEM((K2, tn), BF16)],
        compiler_params=_params(("arbitrary", "arbitrary"),
                                _vmem_limit(tm * K1 * 2, K1 * tn * 4, tm * K2 * 2, K2 * tn * 4,
                                            2 * tm * tn * 4, tm * tn * 2,
                                            temps=(K1 + K2) * tn * 2 + 3 * tm * tn * 4)),
        name="mm_mix",
    )(o_att, w_pa, o_rwkv, w_pr, z, z, b_gate)


def _mm_resw_kernel(a_ref, b_ref, r_ref, o_ref, bw_ref):
    @pl.when(pl.program_id(1) == 0)
    def _():
        _cast_resident(b_ref, bw_ref)

    o_ref[...] = r_ref[...] + _dot(a_ref[...], bw_ref[...])


def _mm_resw(a, w, res, tm, tn):
    M, K = a.shape
    N = w.shape[1]
    return pl.pallas_call(
        _mm_resw_kernel,
        grid=(N // tn, M // tm),
        in_specs=[pl.BlockSpec((tm, K), lambda j, i: (i, 0)), pl.BlockSpec((K, tn), lambda j, i: (0, j)),
                  pl.BlockSpec((tm, tn), lambda j, i: (i, j))],
        out_specs=pl.BlockSpec((tm, tn), lambda j, i: (i, j)),
        out_shape=jax.ShapeDtypeStruct((M, N), F32),
        scratch_shapes=[pltpu.VMEM((K, tn), BF16)],
        compiler_params=_params(("arbitrary", "arbitrary"),
                                _vmem_limit(tm * K * 2, K * tn * 4, 2 * tm * tn * 4, temps=K * tn * 2 + tm * tn * 4)),
        name="mm_out",
    )(a, w, res)


def _weight_copy(w_hbm, land_ref, sem, jj, tn, width):
    return pltpu.make_async_copy(w_hbm.at[:, pl.ds(pl.multiple_of(jj * tn, LANES), width)],
                                 land_ref.at[:, pl.ds(0, width)], sem)


def _weight_copy_op(op, w_hbm, land_ref, sem, jj, tn, nj, n_cols):
    rem = n_cols - (nj - 1) * tn
    if rem == tn:
        getattr(_weight_copy(w_hbm, land_ref, sem, jj, tn, tn), op)()
        return

    @pl.when(jj < nj - 1)
    def _():
        getattr(_weight_copy(w_hbm, land_ref, sem, jj, tn, tn), op)()

    @pl.when(jj == nj - 1)
    def _():
        getattr(_weight_copy(w_hbm, land_ref, sem, jj, tn, rem), op)()


def _stream_weights(weights, j, i, nj, tn, n_cols):
    @pl.when(i == 0)
    def _():
        @pl.when(j == 0)
        def _():
            for w_hbm, land_ref, _, sem in weights:
                _weight_copy_op("start", w_hbm, land_ref, sem, j, tn, nj, n_cols)

        for w_hbm, land_ref, bw_ref, sem in weights:
            _weight_copy_op("wait", w_hbm, land_ref, sem, j, tn, nj, n_cols)
            _cast_resident(land_ref, bw_ref, j * tn, n_cols if n_cols % tn else None)

        @pl.when(j + 1 < nj)
        def _():
            for w_hbm, land_ref, _, sem in weights:
                _weight_copy_op("start", w_hbm, land_ref, sem, j + 1, tn, nj, n_cols)


def _weight_scratch(ks, tn):
    return ([pltpu.VMEM((k, tn), F32) for k in ks] + [pltpu.VMEM((k, tn), BF16) for k in ks]
            + [pltpu.SemaphoreType.DMA(()) for _ in ks])


def _mm_in_s_kernel(a_ref, w_hbm, o_ref, land_ref, bw_ref, sem, *, nj, tn, n_cols):
    j, i = pl.program_id(0), pl.program_id(1)
    _stream_weights([(w_hbm, land_ref, bw_ref, sem)], j, i, nj, tn, n_cols)
    o_ref[...] = _dot(a_ref[...], bw_ref[...])


def _mm_in_s(a, w, tm, tn):
    M, K = a.shape
    N = w.shape[1]
    nj = pl.cdiv(N, tn)
    return pl.pallas_call(
        functools.partial(_mm_in_s_kernel, nj=nj, tn=tn, n_cols=N),
        grid=(nj, M // tm),
        in_specs=[pl.BlockSpec((tm, K), lambda j, i: (i, 0)), pl.BlockSpec(memory_space=pl.ANY)],
        out_specs=pl.BlockSpec((tm, tn), lambda j, i: (i, j)),
        out_shape=jax.ShapeDtypeStruct((M, nj * tn), F32),
        scratch_shapes=_weight_scratch((K,), tn),
        compiler_params=_params(("arbitrary", "arbitrary"),
                                _vmem_limit(tm * K * 2, tm * tn * 4, temps=K * tn * 6 + 2 * tm * tn * 4)),
        name="mm_in",
    )(a, w)


def _mm_mix_s_kernel(a1_ref, a2_ref, z0_ref, z1_ref, bg_ref, w1_hbm, w2_hbm, o_ref,
                     l1_ref, l2_ref, b1_ref, b2_ref, s1, s2, *, nj, tn, n_cols):
    j, i = pl.program_id(0), pl.program_id(1)
    _stream_weights([(w1_hbm, l1_ref, b1_ref, s1), (w2_hbm, l2_ref, b2_ref, s2)], j, i, nj, tn, n_cols)
    g0 = jax.nn.sigmoid(z0_ref[...] + bg_ref[0:1, :])
    g1 = jax.nn.sigmoid(z1_ref[...] + bg_ref[1:2, :])
    o_ref[...] = (g0 * _dot(a1_ref[...], b1_ref[...]) + g1 * _dot(a2_ref[...], b2_ref[...])).astype(o_ref.dtype)


def _mm_mix_s(o_att, w_pa, o_rwkv, w_pr, z, b_gate, S, D, tm, tn):
    K1, K2 = o_att.shape[1], o_rwkv.shape[1]
    assert D % tn == 0
    nj = D // tn
    return pl.pallas_call(
        functools.partial(_mm_mix_s_kernel, nj=nj, tn=tn, n_cols=D),
        grid=(nj, S // tm),
        in_specs=[pl.BlockSpec((tm, K1), lambda j, i: (i, 0)), pl.BlockSpec((tm, K2), lambda j, i: (i, 0)),
                  pl.BlockSpec((tm, tn), lambda j, i: (i, j)), pl.BlockSpec((tm, tn), lambda j, i: (i, nj + j)),
                  pl.BlockSpec((2, tn), lambda j, i: (0, j)),
                  pl.BlockSpec(memory_space=pl.ANY), pl.BlockSpec(memory_space=pl.ANY)],
        out_specs=pl.BlockSpec((tm, tn), lambda j, i: (i, j)),
        out_shape=jax.ShapeDtypeStruct((S, D), BF16),
        scratch_shapes=_weight_scratch((K1, K2), tn),
        compiler_params=_params(("arbitrary", "arbitrary"),
                                _vmem_limit(tm * (K1 + K2) * 2, 2 * tm * tn * 4, tm * tn * 2,
                                            temps=(K1 + K2) * tn * 6 + 3 * tm * tn * 4)),
        name="mm_mix",
    )(o_att, o_rwkv, z, z, b_gate, w_pa, w_pr)


def _mm_out_s_kernel(a_ref, r_ref, g_ref, w_hbm, o_ref, hg_ref, ss_ref, land_ref, bw_ref, sem, *, nj, tn, n_cols):
    j, i = pl.program_id(0), pl.program_id(1)
    _stream_weights([(w_hbm, land_ref, bw_ref, sem)], j, i, nj, tn, n_cols)
    h = r_ref[...] + _dot(a_ref[...], bw_ref[...])
    o_ref[...] = h
    hg_ref[...] = (h * g_ref[...]).astype(hg_ref.dtype)
    ss_ref[...] = jnp.broadcast_to(jnp.sum(h * h, axis=1, keepdims=True), ss_ref.shape)


def _mm_out_s(a, w, res, gain, tm, tn):
    M, K = a.shape
    N = w.shape[1]
    assert N % tn == 0
    nj = N // tn
    tile = pl.BlockSpec((tm, tn), lambda j, i: (i, j))
    return pl.pallas_call(
        functools.partial(_mm_out_s_kernel, nj=nj, tn=tn, n_cols=N),
        grid=(nj, M // tm),
        in_specs=[pl.BlockSpec((tm, K), lambda j, i: (i, 0)), tile, pl.BlockSpec((1, tn), lambda j, i: (0, j)),
                  pl.BlockSpec(memory_space=pl.ANY)],
        out_specs=[tile, tile, pl.BlockSpec((tm, LANES), lambda j, i: (i, j))],
        out_shape=[jax.ShapeDtypeStruct((M, N), F32), jax.ShapeDtypeStruct((M, N), BF16),
                   jax.ShapeDtypeStruct((M, nj * LANES), F32)],
        scratch_shapes=_weight_scratch((K,), tn),
        compiler_params=_params(("arbitrary", "arbitrary"),
                                _vmem_limit(tm * K * 2, 2 * tm * tn * 4, tm * tn * 2, tm * LANES * 4,
                                            temps=K * tn * 6 + 2 * tm * tn * 4)),
        name="mm_out",
    )(a, res, gain.reshape(1, N), w)


def _mm_glu_s_kernel(a_ref, ss_ref, wg_hbm, wu_hbm, o_ref, lg_ref, lu_ref, bg_ref, bu_ref, sg, su,
                     *, nj, tn, n_cols, n_model):
    j, i = pl.program_id(0), pl.program_id(1)
    _stream_weights([(wg_hbm, lg_ref, bg_ref, sg), (wu_hbm, lu_ref, bu_ref, su)], j, i, nj, tn, n_cols)
    ss = ss_ref[:, 0:LANES]
    for t in range(1, ss_ref.shape[1] // LANES):
        ss = ss + ss_ref[:, t * LANES:(t + 1) * LANES]
    scale = lax.rsqrt(ss[:, 0:1] * (1.0 / n_model) + RMS_EPS)
    a = a_ref[...]
    gate = scale * _dot(a, bg_ref[...])
    up = scale * _dot(a, bu_ref[...])
    o_ref[...] = (gate * jax.nn.sigmoid(gate) * up).astype(o_ref.dtype)


def _mm_glu_s(a, ss, wg, wu, tm, tn):
    M, K = a.shape
    N = wg.shape[1]
    nj = pl.cdiv(N, tn)
    return pl.pallas_call(
        functools.partial(_mm_glu_s_kernel, nj=nj, tn=tn, n_cols=N, n_model=K),
        grid=(nj, M // tm),
        in_specs=[pl.BlockSpec((tm, K), lambda j, i: (i, 0)), pl.BlockSpec((tm, ss.shape[1]), lambda j, i: (i, 0)),
                  pl.BlockSpec(memory_space=pl.ANY), pl.BlockSpec(memory_space=pl.ANY)],
        out_specs=pl.BlockSpec((tm, tn), lambda j, i: (i, j)),
        out_shape=jax.ShapeDtypeStruct((M, N), BF16),
        scratch_shapes=_weight_scratch((K, K), tn),
        compiler_params=_params(("arbitrary", "arbitrary"),
                                _vmem_limit(tm * K * 2, tm * ss.shape[1] * 4, tm * tn * 2,
                                            temps=2 * K * tn * 6 + 3 * tm * tn * 4)),
        name="mm_glu",
    )(a, ss, wg, wu)


def _mm_res_kernel(a_ref, b_ref, r_ref, o_ref, *, nk):
    k = pl.program_id(2)
    if nk == 1:
        o_ref[...] = r_ref[...] + _dot(a_ref[...], b_ref[...])
    else:
        @pl.when(k == 0)
        def _():
            o_ref[...] = r_ref[...] + _dot(a_ref[...], b_ref[...])

        @pl.when(k > 0)
        def _():
            o_ref[...] = o_ref[...] + _dot(a_ref[...], b_ref[...])


def _mm_res(a, b, res, tm, tn, tk):
    M, K = a.shape
    N = b.shape[1]
    nk = K // tk
    return pl.pallas_call(
        functools.partial(_mm_res_kernel, nk=nk),
        grid=(M // tm, N // tn, nk),
        in_specs=[pl.BlockSpec((tm, tk), lambda i, j, k: (i, k)), pl.BlockSpec((tk, tn), lambda i, j, k: (k, j)),
                  pl.BlockSpec((tm, tn), lambda i, j, k: (i, j))],
        out_specs=pl.BlockSpec((tm, tn), lambda i, j, k: (i, j)),
        out_shape=jax.ShapeDtypeStruct((M, N), F32),
        compiler_params=_params(("arbitrary", "arbitrary", "arbitrary"),
                                _vmem_limit(tm * tk * 2, tk * tn * 2, 2 * tm * tn * 4, temps=tm * tn * 4)),
        name="mm_down",
    )(a, b, res)


def _mm_glu_kernel(a_ref, bg_ref, bu_ref, o_ref, wg_ref, wu_ref):
    @pl.when(pl.program_id(1) == 0)
    def _():
        _cast_resident(bg_ref, wg_ref)
        _cast_resident(bu_ref, wu_ref)

    a = a_ref[...]
    gate = _dot(a, wg_ref[...])
    up = _dot(a, wu_ref[...])
    o_ref[...] = (gate * jax.nn.sigmoid(gate) * up).astype(o_ref.dtype)


def _mm_glu(a, wg, wu, tm, tn):
    M, K = a.shape
    N = wg.shape[1]
    return pl.pallas_call(
        _mm_glu_kernel,
        grid=(N // tn, M // tm),
        in_specs=[pl.BlockSpec((tm, K), lambda j, i: (i, 0)), pl.BlockSpec((K, tn), lambda j, i: (0, j)),
                  pl.BlockSpec((K, tn), lambda j, i: (0, j))],
        out_specs=pl.BlockSpec((tm, tn), lambda j, i: (i, j)),
        out_shape=jax.ShapeDtypeStruct((M, N), BF16),
        scratch_shapes=[pltpu.VMEM((K, tn), BF16), pltpu.VMEM((K, tn), BF16)],
        compiler_params=_params(("arbitrary", "arbitrary"),
                                _vmem_limit(tm * K * 2, 2 * K * tn * 4, tm * tn * 2,
                                            temps=2 * K * tn * 2 + 3 * tm * tn * 4)),
        name="mm_glu",
    )(a, wg, wu)


def _rope_kernel(z_ref, cos_ref, sin_ref, o_ref, *, n_rope_blocks, heads_per_block, k_heads_last):
    j = pl.program_id(1)
    cos = cos_ref[...]
    sin = sin_ref[...]
    lane = lax.broadcasted_iota(jnp.int32, cos.shape, 1)
    half = ROPE_DIMS // 2
    for h in range(heads_per_block):
        t = z_ref[:, h * HEAD_DIM:(h + 1) * HEAD_DIM]
        up = jnp.concatenate([t[:, half:], t[:, :half]], axis=1)
        down = jnp.concatenate([t[:, HEAD_DIM - half:], t[:, :HEAD_DIM - half]], axis=1)
        partner = jnp.where(lane < half, up, down)
        rot = t * cos + partner * sin
        if h >= k_heads_last:
            rot = jnp.where(j == n_rope_blocks - 1, t, rot)
        o_ref[:, h * HEAD_DIM:(h + 1) * HEAD_DIM] = rot.astype(o_ref.dtype)


def _rope_cast(z, cos_t, sin_t, off_q, width, kv_width):
    Lp = z.shape[0]
    tc = 2 * kv_width
    assert off_q % tc == 0 and width % tc == 0
    nblk = width // tc
    return pl.pallas_call(
        functools.partial(_rope_kernel, n_rope_blocks=nblk, heads_per_block=tc // HEAD_DIM,
                          k_heads_last=kv_width // HEAD_DIM),
        grid=(Lp // BLOCK, nblk),
        in_specs=[pl.BlockSpec((BLOCK, tc), lambda i, j: (i, off_q // tc + j)),
                  pl.BlockSpec((BLOCK, HEAD_DIM), lambda i, j: (i, 0)),
                  pl.BlockSpec((BLOCK, HEAD_DIM), lambda i, j: (i, 0))],
        out_specs=pl.BlockSpec((BLOCK, tc), lambda i, j: (i, j)),
        out_shape=jax.ShapeDtypeStruct((Lp, width), BF16),
        compiler_params=_params(("arbitrary", "arbitrary"), _vmem_limit(BLOCK * tc * 4, BLOCK * tc * 2)),
        name="rope_cast",
    )(z, cos_t, sin_t)


NEG_BIG = -1e30


def _attn_kernel(q_ref, kp_ref, kc_ref, kn_ref, km_ref, vp_ref, vc_ref, vn_ref, vm_ref, sink_ref, o_ref,
                 *, nb, group):
    i = pl.program_id(1)
    scale = HEAD_DIM ** -0.5
    zpad = jnp.zeros((BLOCK - N_META, HEAD_DIM), BF16)
    k_all = jnp.concatenate([kp_ref[...], kc_ref[...], kn_ref[...], km_ref[...], zpad], axis=0)
    v_all = jnp.concatenate([vp_ref[...], vc_ref[...], vn_ref[...], vm_ref[...], zpad], axis=0)
    q_all = jnp.concatenate([q_ref[:, g * HEAD_DIM:(g + 1) * HEAD_DIM] for g in range(group)], axis=0)
    s = _dot(q_all, k_all, NT) * scale
    rows = lax.broadcasted_iota(jnp.int32, s.shape, 0) % BLOCK
    cols = lax.broadcasted_iota(jnp.int32, s.shape, 1)
    rel = cols - BLOCK - rows
    kblk = i - 1 + cols // BLOCK
    band_ok = (cols < 3 * BLOCK) & (kblk >= 0) & (kblk < nb) & (jnp.abs(rel) <= WINDOW)
    meta_ok = (cols >= 3 * BLOCK) & (cols < 3 * BLOCK + N_META)
    s = jnp.where(band_ok | meta_ok, s, NEG_BIG)
    sink = sink_ref[:, 0:1]
    m = jnp.maximum(jnp.max(s, axis=-1, keepdims=True), sink)
    p = jnp.exp(s - m)
    denom = jnp.sum(p, axis=-1, keepdims=True) + jnp.exp(sink - m)
    o = _dot(p.astype(BF16), v_all) / denom
    for g in range(group):
        o_ref[:, g * HEAD_DIM:(g + 1) * HEAD_DIM] = o[g * BLOCK:(g + 1) * BLOCK, :].astype(o_ref.dtype)


def _attention(qkv, sink, S, n_heads, n_kv):
    group = n_heads // n_kv
    nb = S // BLOCK
    qw = group * HEAD_DIM
    kcol = n_heads
    vcol = n_heads + n_kv
    meta_blk = (S + PAD_ROWS - N_META) // N_META
    sink_rows = jnp.broadcast_to(sink.astype(F32).reshape(n_kv, group, 1, 1),
                                 (n_kv, group, BLOCK, LANES)).reshape(n_kv, group * BLOCK, LANES)

    def kv_specs(col0):
        return [pl.BlockSpec((BLOCK, HEAD_DIM), lambda h, i: (jnp.maximum(i - 1, 0), col0 + h)),
                pl.BlockSpec((BLOCK, HEAD_DIM), lambda h, i: (i, col0 + h)),
                pl.BlockSpec((BLOCK, HEAD_DIM), lambda h, i: (jnp.minimum(i + 1, nb - 1), col0 + h)),
                pl.BlockSpec((N_META, HEAD_DIM), lambda h, i: (meta_blk, col0 + h))]

    return pl.pallas_call(
        functools.partial(_attn_kernel, nb=nb, group=group),
        grid=(n_kv, nb),
        in_specs=[pl.BlockSpec((BLOCK, qw), lambda h, i: (i, h))] + kv_specs(kcol) + kv_specs(vcol)
                 + [pl.BlockSpec((None, group * BLOCK, LANES), lambda h, i: (h, 0, 0))],
        out_specs=pl.BlockSpec((BLOCK, qw), lambda h, i: (i, h)),
        out_shape=jax.ShapeDtypeStruct((S, n_heads * HEAD_DIM), BF16),
        compiler_params=_params(("arbitrary", "arbitrary"),
                                _vmem_limit(BLOCK * qw * 2 * 2, 8 * BLOCK * HEAD_DIM * 2,
                                            group * BLOCK * LANES * 4, temps=6 * group * BLOCK * 512 * 4)),
        name="attention",
    )(qkv, qkv, qkv, qkv, qkv, qkv, qkv, qkv, qkv, sink_rows)


def _rope(t, cos, sin):
    half = ROPE_DIMS // 2
    lane = lax.broadcasted_iota(jnp.int32, t.shape, 1)
    up = jnp.concatenate([t[:, half:], t[:, :half]], axis=1)
    down = jnp.concatenate([t[:, HEAD_DIM - half:], t[:, :HEAD_DIM - half]], axis=1)
    return t * cos + jnp.where(lane < half, up, down) * sin


def _attn_res_kernel(sink_ref, q_ref, k_ref, v_ref, cos_ref, sin_ref, o_ref, kb_ref, vb_ref,
                     *, nb, group, qblocks):
    h, i = pl.program_id(0), pl.program_id(1)
    lp = k_ref.shape[0]

    @pl.when(i == 0)
    def _():
        def body(c, carry):
            sl = pl.ds(pl.multiple_of(c * BLOCK, BLOCK), BLOCK)
            kb_ref[sl, :] = _rope(k_ref[sl, :], cos_ref[sl, :], sin_ref[sl, :]).astype(BF16)
            vb_ref[sl, :] = v_ref[sl, :].astype(BF16)
            return carry
        lax.fori_loop(0, lp // BLOCK, body, 0)

    zpad = jnp.zeros((BLOCK - N_META, HEAD_DIM), BF16)
    k_meta = jnp.concatenate([kb_ref[lp - N_META:lp, :], zpad], axis=0)
    v_meta = jnp.concatenate([vb_ref[lp - N_META:lp, :], zpad], axis=0)
    rows = lax.broadcasted_iota(jnp.int32, (BLOCK, 4 * BLOCK), 0)
    cols = lax.broadcasted_iota(jnp.int32, (BLOCK, 4 * BLOCK), 1)
    is_meta = (cols >= 3 * BLOCK) & (cols < 3 * BLOCK + N_META)
    scale = HEAD_DIM ** -0.5
    k_all, v_all, bias, q, sink = [], [], [], [], []
    for b in range(qblocks):
        qi = i * qblocks + b
        kb0 = jnp.clip(qi - 1, 0, nb - 3)
        win = pl.ds(pl.multiple_of(kb0 * BLOCK, BLOCK), 3 * BLOCK)
        k_b = jnp.concatenate([kb_ref[win, :], k_meta], axis=0)
        v_b = jnp.concatenate([vb_ref[win, :], v_meta], axis=0)
        rel = cols - rows + (kb0 - qi) * BLOCK
        ok = ((cols < 3 * BLOCK) & (jnp.abs(rel) <= WINDOW)) | is_meta
        bias_b = jnp.where(ok, 0.0, NEG_BIG).astype(F32)
        qrows = pl.ds(pl.multiple_of(qi * BLOCK, BLOCK), BLOCK)
        cos_q, sin_q = cos_ref[qrows, :], sin_ref[qrows, :]
        for g in range(group):
            k_all.append(k_b)
            v_all.append(v_b)
            bias.append(bias_b)
            q.append(_rope(q_ref[b * BLOCK:(b + 1) * BLOCK, g * HEAD_DIM:(g + 1) * HEAD_DIM],
                           cos_q, sin_q).astype(BF16))
            sink.append(sink_ref[h * group + g])
    n = range(len(q))
    s = [_dot(q[c], k_all[c], NT) * scale + bias[c] for c in n]
    m = [jnp.maximum(jnp.max(s[c], axis=-1, keepdims=True), sink[c]) for c in n]
    p = [jnp.exp(s[c] - m[c]) for c in n]
    denom = [jnp.sum(p[c], axis=-1, keepdims=True) + jnp.exp(sink[c] - m[c]) for c in n]
    o = [_dot(p[c].astype(BF16), v_all[c]) / denom[c] for c in n]
    for c in n:
        b, g = divmod(c, group)
        o_ref[b * BLOCK:(b + 1) * BLOCK, g * HEAD_DIM:(g + 1) * HEAD_DIM] = o[c].astype(o_ref.dtype)


def _attention_res(z, cos_t, sin_t, sink, S, n_heads, n_kv, off_q):
    Lp = z.shape[0]
    group = n_heads // n_kv
    nb = S // BLOCK
    assert nb >= 3
    qw = group * HEAD_DIM
    kcol = off_q // HEAD_DIM + n_heads
    vcol = kcol + n_kv
    col_bytes = Lp * HEAD_DIM * 4
    qblocks = _pick(nb, (4, 2, 1))
    tq = qblocks * BLOCK
    return pl.pallas_call(
        functools.partial(_attn_res_kernel, nb=nb, group=group, qblocks=qblocks),
        grid_spec=pltpu.PrefetchScalarGridSpec(
            num_scalar_prefetch=1,
            grid=(n_kv, nb // qblocks),
            in_specs=[pl.BlockSpec((tq, qw), lambda h, i, s: (i, off_q // qw + h)),
                      pl.BlockSpec((Lp, HEAD_DIM), lambda h, i, s: (0, kcol + h)),
                      pl.BlockSpec((Lp, HEAD_DIM), lambda h, i, s: (0, vcol + h)),
                      pl.BlockSpec((Lp, HEAD_DIM), lambda h, i, s: (0, 0)),
                      pl.BlockSpec((Lp, HEAD_DIM), lambda h, i, s: (0, 0))],
            out_specs=pl.BlockSpec((tq, qw), lambda h, i, s: (i, h)),
            scratch_shapes=[pltpu.VMEM((Lp, HEAD_DIM), BF16), pltpu.VMEM((Lp, HEAD_DIM), BF16)]),
        out_shape=jax.ShapeDtypeStruct((S, n_heads * HEAD_DIM), BF16),
        compiler_params=_params(("arbitrary", "arbitrary"),
                                _vmem_limit(tq * qw * 4, 4 * col_bytes, tq * qw * 2,
                                            temps=col_bytes + 8 * group * tq * 512 * 4)),
        name="attention",
    )(sink.astype(F32), z, z, z, cos_t, sin_t)


def _split3(x):
    hi = x.astype(BF16)
    r1 = x - hi.astype(F32)
    mid = r1.astype(BF16)
    lo = (r1 - mid.astype(F32)).astype(BF16)
    return hi, mid, lo


def _split2(x):
    hi = x.astype(BF16)
    return hi, (x - hi.astype(F32)).astype(BF16)


def _dot_exact_rhs(x, m_bf16, dims=NN, left=False):
    parts = _split3(x)
    if left:
        return sum(_dot(m_bf16, p, dims) for p in parts)
    return sum(_dot(p, m_bf16, dims) for p in parts)


def _dotp(a, b, passes, dims=NN):
    if passes == 1:
        return _dot(a.astype(BF16), b.astype(BF16), dims)
    a_hi, a_lo = _split2(a)
    b_hi, b_lo = _split2(b)
    return _dot(a_hi, b_hi, dims) + (_dot(a_hi, b_lo, dims) + _dot(a_lo, b_hi, dims))


def _head_sum(x, ones_bd):
    outs = []
    for s in range(x.shape[1] // LANES):
        outs.append(_dot_exact_rhs(x[:, s * LANES:(s + 1) * LANES], ones_bd))
    return outs[0] if len(outs) == 1 else jnp.concatenate(outs, axis=1)


def _prep_kernel(r_ref, rp_ref, rn_ref, k_ref, kp_ref, kn_ref, v_ref, vp_ref, vn_ref,
                 lo_ref, lop_ref, lon_ref, gd_ref, gdp_ref, gdn_ref,
                 pc_ref, pl_ref, wl_ref, g2_ref, ones_ref,
                 ro_ref, vo_ref, kko_ref, lwf_ref, lwb_ref, kdf_ref, kdb_ref, bdf_ref, bdb_ref, go_ref, bo_ref,
                 *, n_tiles, tr, lora_tanh_cols):
    i = pl.program_id(1)

    def shifted(main_ref, prev_ref, next_ref, mup, mun):
        x = main_ref[...]
        row8 = lax.broadcasted_iota(jnp.int32, (8, x.shape[1]), 0)
        down = pltpu.roll(x, 1, 0)
        up = pltpu.roll(x, tr - 1, 0)
        prev = jnp.concatenate([jnp.where(row8 == 0, prev_ref[7:8, :], down[0:8]), down[8:]], axis=0)
        nxt = jnp.concatenate([up[:tr - 8], jnp.where(row8 == 7, next_ref[0:1, :], up[tr - 8:])], axis=0)
        return x + mup * (prev - x) + mun * (nxt - x)

    r = shifted(r_ref, rp_ref, rn_ref, pc_ref[0:1, :], pc_ref[1:2, :])
    k = shifted(k_ref, kp_ref, kn_ref, pc_ref[2:3, :], pc_ref[3:4, :])
    v = shifted(v_ref, vp_ref, vn_ref, pc_ref[4:5, :], pc_ref[5:6, :])
    lo = shifted(lo_ref, lop_ref, lon_ref, pl_ref[0:1, 0:LORA_PAD], pl_ref[1:2, 0:LORA_PAD])
    gd = shifted(gd_ref, gdp_ref, gdn_ref, pl_ref[0:1, LORA_PAD:], pl_ref[1:2, LORA_PAD:])

    row = lax.broadcasted_iota(jnp.int32, (tr, 1), 0)
    valid = jnp.where((i < n_tiles - 1) | (row >= tr - N_META), 1.0, 0.0).astype(F32)

    lane = lax.broadcasted_iota(jnp.int32, lo.shape, 1)
    lo_act = jnp.where(lane < lora_tanh_cols, jnp.tanh(lo), lo).astype(BF16)
    dec_f = _dot(lo_act, wl_ref[0])
    dec_b = _dot(lo_act, wl_ref[1])
    apre_f = _dot(lo_act, wl_ref[2])
    apre_b = _dot(lo_act, wl_ref[3])
    g = _dot(jax.nn.sigmoid(lo).astype(BF16), g2_ref[0]) + _dot(jax.nn.sigmoid(gd).astype(BF16), g2_ref[1])

    def log_decay(dec, w0):
        return -math.exp(-0.5) * jax.nn.sigmoid(w0 + dec)

    a_f = jax.nn.sigmoid(pc_ref[8:9, :] + apre_f)
    a_b = jax.nn.sigmoid(pc_ref[9:10, :] + apre_b)
    kk = k * pc_ref[10:11, :]
    ss = _head_sum(kk * kk, ones_ref[...])
    kk = kk / jnp.maximum(jnp.sqrt(ss), 1e-12) * valid
    k_a = pc_ref[11:12, :]
    kv = k * valid

    rv = r * valid
    vv = v * valid
    kd_f = kv * (1.0 + (a_f - 1.0) * k_a)
    kd_b = kv * (1.0 + (a_b - 1.0) * k_a)
    ro_ref[...] = rv
    vo_ref[...] = vv
    kko_ref[...] = kk
    lwf_ref[...] = log_decay(dec_f, pc_ref[6:7, :])
    lwb_ref[...] = log_decay(dec_b, pc_ref[7:8, :])
    kdf_ref[...] = kd_f
    kdb_ref[...] = kd_b
    bdf_ref[...] = kk * a_f
    bdb_ref[...] = kk * a_b
    go_ref[...] = g
    bo_ref[...] = _head_sum(rv * (kd_f + kd_b) * pc_ref[12:13, :], ones_ref[...]) * vv


def _rwkv_prep(z, off_r, C, pc, plo, wl, g2, ones_bd, lora_tanh_cols):
    Lp = z.shape[0]
    tr = BLOCK
    ct = _pick(C, (1024, 512))
    n_tiles = Lp // tr
    n8 = Lp // 8
    off_lo = off_r + 3 * C
    off_gd = off_lo + LORA_PAD
    assert off_r % ct == 0 and off_lo % LORA_PAD == 0

    def seg_specs(off, w, with_c):
        cb = off // w

        def col(c):
            return cb + c if with_c else cb
        return [pl.BlockSpec((tr, w), lambda c, i: (i, col(c))),
                pl.BlockSpec((8, w), lambda c, i: ((i * (tr // 8) + n8 - 1) % n8, col(c))),
                pl.BlockSpec((8, w), lambda c, i: (((i + 1) * (tr // 8)) % n8, col(c)))]

    in_specs = (seg_specs(off_r, ct, True) + seg_specs(off_r + C, ct, True) + seg_specs(off_r + 2 * C, ct, True)
                + seg_specs(off_lo, LORA_PAD, False) + seg_specs(off_gd, LORA_PAD, False)
                + [pl.BlockSpec((16, ct), lambda c, i: (0, c)),
                   pl.BlockSpec((8, 2 * LORA_PAD), lambda c, i: (0, 0)),
                   pl.BlockSpec((4, LORA_PAD, ct), lambda c, i: (0, 0, c)),
                   pl.BlockSpec((2, LORA_PAD, ct), lambda c, i: (0, 0, c)),
                   pl.BlockSpec((LANES, LANES), lambda c, i: (0, 0))])
    out_spec = pl.BlockSpec((tr, ct), lambda c, i: (i, c))
    out_sds = jax.ShapeDtypeStruct((Lp, C), F32)
    return pl.pallas_call(
        functools.partial(_prep_kernel, n_tiles=n_tiles, tr=tr, lora_tanh_cols=lora_tanh_cols),
        grid=(C // ct, n_tiles),
        in_specs=in_specs,
        out_specs=[out_spec] * 11,
        out_shape=[out_sds] * 11,
        compiler_params=_params(("arbitrary", "arbitrary"),
                                _vmem_limit(3 * tr * ct * 4, 2 * tr * LORA_PAD * 4, 4 * LORA_PAD * ct * 2,
                                            2 * LORA_PAD * ct * 2, 11 * tr * ct * 4, temps=24 * tr * ct * 4)),
        name="rwkv_prep",
    )(*([z] * 15), pc, plo, wl, g2, ones_bd)


SCAN_PAIRS = 16


def _scan_consts():
    C = CHUNK
    row = lax.broadcasted_iota(jnp.int32, (C, 2 * C), 0)
    col = lax.broadcasted_iota(jnp.int32, (C, 2 * C), 1) % C
    rr = lax.broadcasted_iota(jnp.int32, (LANES, LANES), 0)
    cc = lax.broadcasted_iota(jnp.int32, (LANES, LANES), 1)
    return dict(
        row=lax.broadcasted_iota(jnp.int32, (C, LANES), 0),
        head0=lax.broadcasted_iota(jnp.int32, (C, LANES), 1) < RWKV_HEAD,
        strict_f=col < row, strict_b=col > row, incl_f=col <= row, incl_b=col >= row,
        eye_sbs=jnp.where(col == row, 1.0, 0.0).astype(F32),
        same_head=(rr // RWKV_HEAD) == (cc // RWKV_HEAD), eye=rr == cc)


def _scan_chains(chains, cst):
    C = CHUNK
    head0, same, eye = cst["head0"], cst["same_head"], cst["eye"]
    n = range(len(chains))
    rev = [c[7] for c in chains]
    r, v, al, lw, k, be, h = ([c[i] for c in chains] for i in range(7))
    strict = [cst["strict_b" if x else "strict_f"] for x in rev]
    incl = [cst["incl_b" if x else "incl_f"] for x in rev]

    def bd(x):
        xb = x.astype(BF16)
        zero = jnp.zeros_like(xb)
        return jnp.concatenate([jnp.where(head0, xb, zero), jnp.where(head0, zero, xb)], axis=0)

    def mm(a, b_bf16, dims=NN):
        return _dot(a.astype(BF16), b_bf16, dims)

    row = cst["row"]
    cl = list(lw)
    sh = 1
    while sh < C:
        cl = [cl[i] + (jnp.where(row < C - sh, pltpu.roll(cl[i], C - sh, 0), 0.0) if rev[i] else
                       jnp.where(row >= sh, pltpu.roll(cl[i], sh, 0), 0.0)) for i in n]
        sh *= 2
    total = [cl[i][0:1, :] if rev[i] else cl[i][C - 1:C, :] for i in n]
    a_t = [al[i] * jnp.exp(cl[i] - lw[i]) for i in n]
    r_t = [r[i] * jnp.exp(cl[i]) for i in n]
    w_inv = [jnp.exp(-cl[i]) for i in n]
    w_rest = [jnp.exp(total[i] - cl[i]) for i in n]
    bd_v = [bd(v[i]) for i in n]
    sc = [mm(jnp.concatenate([a_t[i], r_t[i]], axis=0),
             jnp.concatenate([bd(k[i] * w_inv[i]), bd(be[i] * w_inv[i])], axis=0), NT) for i in n]
    a_ak = [jnp.where(strict[i], sc[i][0:C, 0:2 * C], 0.0) for i in n]
    n_ab = [jnp.where(strict[i], sc[i][0:C, 2 * C:], 0.0) for i in n]
    a_rk = [jnp.where(incl[i], sc[i][C:, 0:2 * C], 0.0) for i in n]
    a_rb = [jnp.where(incl[i], sc[i][C:, 2 * C:], 0.0) for i in n]

    t_inv = [cst["eye_sbs"] - n_ab[i] for i in n]
    sq = [mm(n_ab[i], bd(n_ab[i])) for i in n]
    n_round = 2
    while (1 << n_round) < C:
        both = [mm(jnp.concatenate([t_inv[i], sq[i]], axis=0), bd(sq[i])) for i in n]
        t_inv = [t_inv[i] + both[i][0:C] for i in n]
        sq = [both[i][C:] for i in n]
        n_round += 1
    t_inv = [t_inv[i] + mm(t_inv[i], bd(sq[i])) for i in n]

    akv = [mm(jnp.concatenate([a_ak[i], a_rk[i]], axis=0), bd_v[i]) for i in n]
    tp = [mm(t_inv[i], jnp.concatenate([bd(akv[i][0:C]), bd(a_t[i])], axis=1)) for i in n]
    p0 = [tp[i][:, 0:LANES] for i in n]
    at = [tp[i][:, LANES:] for i in n]
    rb = [mm(a_rb[i], jnp.concatenate([bd(p0[i]), bd(at[i])], axis=1)) for i in n]
    y0 = [akv[i][C:] - rb[i][:, 0:LANES] for i in n]
    rh = [r_t[i] - rb[i][:, LANES:] for i in n]
    bt = [mm(be[i] * w_rest[i], jnp.concatenate([at[i], p0[i]], axis=1).astype(BF16), TN) for i in n]
    kv = [mm(k[i] * w_rest[i], v[i].astype(BF16), TN) for i in n]
    m_mat = [jnp.where(same, jnp.where(eye, jnp.broadcast_to(jnp.exp(total[i]), (LANES, LANES)), 0.0)
                       - bt[i][:, 0:LANES], 0.0) for i in n]
    g_mat = [jnp.where(same, kv[i] - bt[i][:, LANES:], 0.0) for i in n]
    yh = [mm(jnp.concatenate([rh[i], m_mat[i]], axis=0), h[i].astype(BF16)) for i in n]
    return [(yh[i][0:C] + y0[i], yh[i][C:] + g_mat[i]) for i in n]


def _scan_kernel(rf_ref, vf_ref, af_ref, lwf_ref, kf_ref, bf_ref,
                 rb_ref, vb_ref, ab_ref, lwb_ref, kb_ref, bb_ref, wd_ref,
                 yf_ref, yb_ref, wdo_ref, hf_ref, hb_ref, *, pairs):
    @pl.when(pl.program_id(1) == 0)
    def _():
        hf_ref[...] = jnp.zeros(hf_ref.shape, F32)
        hb_ref[...] = jnp.zeros(hb_ref.shape, F32)

    wdo_ref[...] = wd_ref[...].astype(wdo_ref.dtype)

    cst = _scan_consts()
    chains = []
    for g in range(pairs):
        sl = slice(g * LANES, (g + 1) * LANES)
        chains.append((rf_ref[:, sl], vf_ref[:, sl], af_ref[:, sl], lwf_ref[:, sl], kf_ref[:, sl], bf_ref[:, sl],
                       hf_ref[g], False))
        chains.append((rb_ref[:, sl], vb_ref[:, sl], ab_ref[:, sl], lwb_ref[:, sl], kb_ref[:, sl], bb_ref[:, sl],
                       hb_ref[g], True))
    res = _scan_chains(chains, cst)
    for g in range(pairs):
        sl = slice(g * LANES, (g + 1) * LANES)
        yf_ref[:, sl], hf_ref[g] = res[2 * g]
        yb_ref[:, sl], hb_ref[g] = res[2 * g + 1]


def _rwkv_scan(r, v, kk, lw_f, lw_b, kd_f, kd_b, bd_f, bd_b, w_cast, S):
    Lp, C = r.shape
    n_real = S // CHUNK
    n_chunks = Lp // CHUNK
    steps = n_real + 1
    pairs = _pick(C // LANES, (SCAN_PAIRS, 2, 1))
    w = pairs * LANES
    total_steps = (C // w) * steps
    wr, wc = w_cast.shape
    cast_rows = next(rb for rb in (128, 256, 512, 1024, wr) if wr % rb == 0 and wr // rb <= total_steps)
    n_cast = wr // cast_rows
    cast_spec = pl.BlockSpec((cast_rows, wc), lambda p, s: (jnp.minimum(p * steps + s, n_cast - 1), 0))

    def fwd(p, s):
        return ((s + n_real + 1) % n_chunks, p)

    def bwd(p, s):
        return (n_real - s, p)

    spec_f = pl.BlockSpec((CHUNK, w), fwd)
    spec_b = pl.BlockSpec((CHUNK, w), bwd)
    out_sds = jax.ShapeDtypeStruct((S + CHUNK, C), F32)
    out_f = pl.BlockSpec((CHUNK, w), lambda p, s: ((s + n_real) % (n_real + 1), p))
    return pl.pallas_call(
        functools.partial(_scan_kernel, pairs=pairs),
        grid=(C // w, steps),
        in_specs=[spec_f] * 6 + [spec_b] * 6 + [cast_spec],
        out_specs=[out_f, spec_b, cast_spec],
        out_shape=[out_sds, out_sds, jax.ShapeDtypeStruct((wr, wc), BF16)],
        scratch_shapes=[pltpu.VMEM((pairs, LANES, LANES), F32), pltpu.VMEM((pairs, LANES, LANES), F32)],
        compiler_params=_params(("arbitrary", "arbitrary"),
                                _vmem_limit(14 * CHUNK * w * 4, cast_rows * wc * 6, temps=16 << 20)),
        name="rwkv_scan",
    )(r, v, kk, lw_f, kd_f, bd_f, r, v, kk, lw_b, kd_b, bd_b, w_cast)


def _post_kernel(yf_ref, yb_ref, bonus_ref, g_ref, pq_ref, ones_ref, o_ref):
    ones_bd = ones_ref[...]
    inv_n = 1.0 / RWKV_HEAD
    y = yf_ref[...] + yb_ref[...]
    mean = _head_sum(y, ones_bd) * inv_n
    yc = y - mean
    var = _head_sum(yc * yc, ones_bd) * inv_n
    yn = yc * lax.rsqrt(var + GN_EPS) * pq_ref[0:1, :] + pq_ref[1:2, :]
    o_ref[...] = ((yn + bonus_ref[...]) * g_ref[...]).astype(o_ref.dtype)


def _rwkv_post(y_f, y_b, bonus, g, pq, ones_bd, S):
    C = g.shape[1]
    tr = 512
    ct = _pick(C, (1024, 512))
    spec = pl.BlockSpec((tr, ct), lambda i, c: (i, c))
    return pl.pallas_call(
        _post_kernel,
        grid=(S // tr, C // ct),
        in_specs=[spec] * 4 + [pl.BlockSpec((8, ct), lambda i, c: (0, c)),
                               pl.BlockSpec((LANES, LANES), lambda i, c: (0, 0))],
        out_specs=spec,
        out_shape=jax.ShapeDtypeStruct((S, C), BF16),
        compiler_params=_params(("arbitrary", "arbitrary"),
                                _vmem_limit(4 * tr * ct * 4, tr * ct * 2, temps=16 * tr * ct * 4)),
        name="rwkv_post",
    )(y_f, y_b, bonus, g, pq, ones_bd)


def _pad_cols(w, width):
    return jnp.pad(w, ((0, 0), (0, width - w.shape[1])))


def kernel(x, meta_tokens, norm_mix_w, w_in, b_gate, mu_prev, mu_next, dec_w0, dec_w2, iclr_a0, iclr_a2,
           gate_w2, k_k, k_a, r_k, ln_x_w, ln_x_b, attn_sink, w_proj_attn, w_proj_rwkv, w_out, norm_ffn_w,
           w_ffn_gate, w_ffn_up, w_ffn_down, norm_final_w):
    B, S, D = x.shape
    assert B == 1 and norm_mix_w.shape[0] == 1, "one sequence, one layer"
    C = k_k.shape[-1]
    AW = w_proj_attn.shape[1]
    n_heads = attn_sink.shape[-1]
    DL, IL, GL = dec_w2.shape[2], iclr_a2.shape[2], gate_w2.shape[1]
    shift_w = mu_prev.shape[-1]
    KVW = (w_in.shape[-1] - 2 * D - AW - shift_w) // 2
    n_kv = KVW // HEAD_DIM
    F = w_ffn_gate.shape[-1]
    lora_w = 2 * DL + 2 * IL
    assert shift_w == 3 * C + lora_w + GL and lora_w <= LORA_PAD and GL <= LORA_PAD
    assert S % BLOCK == 0 and C % LANES == 0 and n_heads * HEAD_DIM == AW
    Lp = S + PAD_ROWS

    off_q = 2 * D
    off_r = off_q + AW + 2 * KVW
    off_lo = off_r + 3 * C
    tn_in = 1024
    n_in_pad = -(-w_in.shape[-1] // tn_in) * tn_in
    assert n_in_pad >= off_lo + 2 * LORA_PAD and off_lo % LORA_PAD == 0
    low_w = lora_w + GL
    gl_a = LORA_PAD - lora_w

    mu_p, mu_n = mu_prev[0], mu_next[0]
    zc = jnp.zeros((C,), F32)
    pc = jnp.stack([mu_p[:C], mu_n[:C], mu_p[C:2 * C], mu_n[C:2 * C], mu_p[2 * C:3 * C], mu_n[2 * C:3 * C],
                    dec_w0[0, 0], dec_w0[0, 1], iclr_a0[0, 0], iclr_a0[0, 1], k_k[0], k_a[0], r_k[0], zc, zc, zc])
    plo = jnp.zeros((8, 2 * LORA_PAD), F32).at[0, :low_w].set(mu_p[3 * C:]).at[1, :low_w].set(mu_n[3 * C:])
    wl = jnp.zeros((4, LORA_PAD, C), F32)
    wl = wl.at[0, 0:DL].set(dec_w2[0, 0]).at[1, DL:2 * DL].set(dec_w2[0, 1])
    wl = wl.at[2, 2 * DL:2 * DL + IL].set(iclr_a2[0, 0]).at[3, 2 * DL + IL:lora_w].set(iclr_a2[0, 1])
    wl = wl.astype(BF16)
    g2 = jnp.zeros((2, LORA_PAD, C), F32).at[0, lora_w:].set(gate_w2[0, :gl_a]).at[1, :GL - gl_a].set(
        gate_w2[0, gl_a:]).astype(BF16)
    pq = jnp.zeros((8, C), F32).at[0].set(ln_x_w[0]).at[1].set(ln_x_b[0])
    lane = jnp.arange(LANES)
    ones_bd = (lane[:, None] // RWKV_HEAD == lane[None, :] // RWKV_HEAD).astype(BF16)


    pos = jnp.concatenate([jnp.arange(N_META, N_META + S), jnp.zeros((PAD_ROWS - N_META,), jnp.int32),
                           jnp.arange(N_META)]).astype(F32)
    inv = ROPE_THETA ** (-jnp.arange(0, ROPE_DIMS, 2, dtype=F32) / ROPE_DIMS)
    ang = pos[:, None] * inv[None, :]
    ones_rest = jnp.ones((Lp, HEAD_DIM - ROPE_DIMS), F32)
    cos_t = jnp.concatenate([jnp.cos(ang), jnp.cos(ang), ones_rest], axis=1)
    sin_t = jnp.concatenate([-jnp.sin(ang), jnp.sin(ang), 0.0 * ones_rest], axis=1)

    x2 = x[0]
    h_ext = _norm_ext(x2, meta_tokens.astype(x.dtype), norm_mix_w[0])
    z = _mm_in_s(h_ext, w_in[0], _pick(Lp, (1040, 640)), tn_in)
    o_att = _attention_res(z, cos_t, sin_t, attn_sink[0], S, n_heads, n_kv, off_q)
    r_s, v_s, kk, lw_f, lw_b, kd_f, kd_b, bd_f, bd_b, g, bonus = _rwkv_prep(
        z, off_r, C, pc, plo, wl, g2, ones_bd, 2 * DL)
    y_f, y_b, w_d = _rwkv_scan(r_s, v_s, kk, lw_f, lw_b, kd_f, kd_b, bd_f, bd_b, w_ffn_down[0], S)
    o_rwkv = _rwkv_post(y_f, y_b, bonus, g, pq, ones_bd, S)
    tm = _pick(S, (1024, 512))
    tn_d = _pick(D, (1024, 512))
    mixed = _mm_mix_s(o_att, w_proj_attn[0], o_rwkv, w_proj_rwkv[0], z, b_gate[0], S, D, 512, tn_d)
    h1, h1g, h1ss = _mm_out_s(mixed, w_out[0], x2, norm_ffn_w[0], 512, tn_d)
    act = _mm_glu_s(h1g, h1ss, w_ffn_gate[0], w_ffn_up[0], tm, 512)
    h2 = _mm_res(act, w_d, h1, tm, 512, F // 2)
    y = _norm(h2, norm_final_w, x.dtype)
    return y[None]
```

```python
import functools
import math

import jax
import jax.numpy as jnp
from jax import lax
from jax.experimental import pallas as pl
from jax.experimental.pallas import tpu as pltpu

F32 = jnp.float32
BF16 = jnp.bfloat16

N_META = 16
HEAD_DIM = 128
WINDOW = 128
BLOCK = 128
ROPE_DIMS = HEAD_DIM // 4
ROPE_THETA = 500000.0
RWKV_HEAD = 64
RMS_EPS = 1e-6
GN_EPS = 64e-5
LANES = 128
PAD_ROWS = BLOCK
CHUNK = 64
LORA_PAD = 512
VMEM_PHYSICAL = 64 * 1024 * 1024

NN = (((1,), (0,)), ((), ()))
NT = (((1,), (1,)), ((), ()))
TN = (((0,), (0,)), ((), ()))


def _vmem_limit(*block_bytes, temps=0):
    need = 2 * sum(block_bytes) + temps + (4 << 20)
    return int(min(max(need, 16 << 20), VMEM_PHYSICAL - (4 << 20)))


def _params(sem, limit):
    return pltpu.CompilerParams(dimension_semantics=sem, vmem_limit_bytes=limit)


def _pick(n, cands):
    for c in cands:
        if n % c == 0:
            return c
    raise ValueError(f"no tile in {cands} divides {n}")


def _rms(x, w):
    return x * lax.rsqrt(jnp.mean(x * x, axis=-1, keepdims=True) + RMS_EPS) * w


def _norm_ext_kernel(x_ref, meta_ref, w_ref, o_ref, *, n_steps, real_last):
    i = pl.program_id(0)
    tr = o_ref.shape[0]

    @pl.when(i < n_steps - 1)
    def _():
        o_ref[...] = _rms(x_ref[...], w_ref[...]).astype(o_ref.dtype)

    @pl.when(i == n_steps - 1)
    def _():
        if real_last:
            o_ref[0:real_last, :] = _rms(x_ref[0:real_last, :], w_ref[...]).astype(o_ref.dtype)
        o_ref[real_last:tr - N_META, :] = jnp.zeros((tr - N_META - real_last, o_ref.shape[1]), o_ref.dtype)
        o_ref[tr - N_META:, :] = _rms(meta_ref[...], w_ref[...]).astype(o_ref.dtype)


def _norm_ext(x, meta, w):
    S, D = x.shape
    Lp = S + PAD_ROWS
    tr = _pick(Lp, (640, BLOCK))
    n_steps = Lp // tr
    real_last = S - (n_steps - 1) * tr
    last_x_blk = (S - 1) // tr
    return pl.pallas_call(
        functools.partial(_norm_ext_kernel, n_steps=n_steps, real_last=real_last),
        grid=(n_steps,),
        in_specs=[pl.BlockSpec((tr, D), lambda i: (jnp.minimum(i, last_x_blk), 0)),
                  pl.BlockSpec((N_META, D), lambda i: (0, 0)),
                  pl.BlockSpec((1, D), lambda i: (0, 0))],
        out_specs=pl.BlockSpec((tr, D), lambda i: (i, 0)),
        out_shape=jax.ShapeDtypeStruct((Lp, D), BF16),
        compiler_params=_params(("arbitrary",), _vmem_limit(tr * D * 4, tr * D * 2, temps=2 * tr * D * 4)),
        name="norm_ext",
    )(x, meta, w.reshape(1, D))


def _norm_kernel(x_ref, w_ref, o_ref):
    o_ref[...] = _rms(x_ref[...], w_ref[...]).astype(o_ref.dtype)


def _norm(x, w, out_dtype, tr=512):
    S, D = x.shape
    return pl.pallas_call(
        _norm_kernel,
        grid=(S // tr,),
        in_specs=[pl.BlockSpec((tr, D), lambda i: (i, 0)), pl.BlockSpec((1, D), lambda i: (0, 0))],
        out_specs=pl.BlockSpec((tr, D), lambda i: (i, 0)),
        out_shape=jax.ShapeDtypeStruct((S, D), out_dtype),
        compiler_params=_params(("arbitrary",), _vmem_limit(tr * D * 4, tr * D * 4, temps=2 * tr * D * 4)),
        name="norm",
    )(x, w.reshape(1, D))


def _dot(a, b, dims=NN):
    return lax.dot_general(a, b, dims, preferred_element_type=F32)


def _cast_resident(src_ref, dst_ref, col0=None, n_valid=None):
    K = src_ref.shape[0]
    rows = _pick(K, (256, 128))

    def body(c, carry):
        sl = pl.ds(pl.multiple_of(c * rows, rows), rows)
        blk = src_ref[sl, :]
        if n_valid is not None:
            col = col0 + lax.broadcasted_iota(jnp.int32, blk.shape, 1)
            blk = jnp.where(col < n_valid, blk, 0.0)
        dst_ref[sl, :] = blk.astype(dst_ref.dtype)
        return carry

    lax.fori_loop(0, K // rows, body, 0)


def _mm_in_kernel(a_ref, b_ref, o_ref, bw_ref, *, n_valid):
    j, i = pl.program_id(0), pl.program_id(1)

    @pl.when(i == 0)
    def _():
        _cast_resident(b_ref, bw_ref, j * b_ref.shape[1], n_valid)

    o_ref[...] = _dot(a_ref[...], bw_ref[...])


def _mm_in(a, w, tm, tn):
    M, K = a.shape
    N = w.shape[1]
    nj = pl.cdiv(N, tn)
    return pl.pallas_call(
        functools.partial(_mm_in_kernel, n_valid=N),
        grid=(nj, M // tm),
        in_specs=[pl.BlockSpec((tm, K), lambda j, i: (i, 0)), pl.BlockSpec((K, tn), lambda j, i: (0, j))],
        out_specs=pl.BlockSpec((tm, tn), lambda j, i: (i, j)),
        out_shape=jax.ShapeDtypeStruct((M, nj * tn), F32),
        scratch_shapes=[pltpu.VMEM((K, tn), BF16)],
        compiler_params=_params(("arbitrary", "arbitrary"),
                                _vmem_limit(tm * K * 2, K * tn * 4, tm * tn * 4, temps=K * tn * 2 + tm * tn * 4)),
        name="mm_in",
    )(a, w)


def _mm_mix_kernel(a1_ref, b1_ref, a2_ref, b2_ref, z0_ref, z1_ref, bg_ref, o_ref, w1_ref, w2_ref):
    @pl.when(pl.program_id(1) == 0)
    def _():
        _cast_resident(b1_ref, w1_ref)
        _cast_resident(b2_ref, w2_ref)

    g0 = jax.nn.sigmoid(z0_ref[...] + bg_ref[0:1, :])
    g1 = jax.nn.sigmoid(z1_ref[...] + bg_ref[1:2, :])
    o_ref[...] = (g0 * _dot(a1_ref[...], w1_ref[...]) + g1 * _dot(a2_ref[...], w2_ref[...])).astype(o_ref.dtype)


def _mm_mix(o_att, w_pa, o_rwkv, w_pr, z, b_gate, S, D, tm, tn):
    K1, K2 = o_att.shape[1], o_rwkv.shape[1]
    nj = D // tn
    return pl.pallas_call(
        _mm_mix_kernel,
        grid=(nj, S // tm),
        in_specs=[pl.BlockSpec((tm, K1), lambda j, i: (i, 0)), pl.BlockSpec((K1, tn), lambda j, i: (0, j)),
                  pl.BlockSpec((tm, K2), lambda j, i: (i, 0)), pl.BlockSpec((K2, tn), lambda j, i: (0, j)),
                  pl.BlockSpec((tm, tn), lambda j, i: (i, j)), pl.BlockSpec((tm, tn), lambda j, i: (i, nj + j)),
                  pl.BlockSpec((2, tn), lambda j, i: (0, j))],
        out_specs=pl.BlockSpec((tm, tn), lambda j, i: (i, j)),
        out_shape=jax.ShapeDtypeStruct((S, D), BF16),
        scratch_shapes=[pltpu.VMEM((K1, tn), BF16), pltpu.VMEM((K2, tn), BF16)],
        compiler_params=_params(("arbitrary", "arbitrary"),
                                _vmem_limit(tm * K1 * 2, K1 * tn * 4, tm * K2 * 2, K2 * tn * 4,
                                            2 * tm * tn * 4, tm * tn * 2,
                                            temps=(K1 + K2) * tn * 2 + 3 * tm * tn * 4)),
        name="mm_mix",
    )(o_att, w_pa, o_rwkv, w_pr, z, z, b_gate)


def _mm_resw_kernel(a_ref, b_ref, r_ref, o_ref, bw_ref):
    @pl.when(pl.program_id(1) == 0)
    def _():
        _cast_resident(b_ref, bw_ref)

    o_ref[...] = r_ref[...] + _dot(a_ref[...], bw_ref[...])


def _mm_resw(a, w, res, tm, tn):
    M, K = a.shape
    N = w.shape[1]
    return pl.pallas_call(
        _mm_resw_kernel,
        grid=(N // tn, M // tm),
        in_specs=[pl.BlockSpec((tm, K), lambda j, i: (i, 0)), pl.BlockSpec((K, tn), lambda j, i: (0, j)),
                  pl.BlockSpec((tm, tn), lambda j, i: (i, j))],
        out_specs=pl.BlockSpec((tm, tn), lambda j, i: (i, j)),
        out_shape=jax.ShapeDtypeStruct((M, N), F32),
        scratch_shapes=[pltpu.VMEM((K, tn), BF16)],
        compiler_params=_params(("arbitrary", "arbitrary"),
                                _vmem_limit(tm * K * 2, K * tn * 4, 2 * tm * tn * 4, temps=K * tn * 2 + tm * tn * 4)),
        name="mm_out",
    )(a, w, res)


def _weight_copy(w_hbm, land_ref, sem, jj, tn, width):
    return pltpu.make_async_copy(w_hbm.at[:, pl.ds(pl.multiple_of(jj * tn, LANES), width)],
                                 land_ref.at[:, pl.ds(0, width)], sem)


def _weight_copy_op(op, w_hbm, land_ref, sem, jj, tn, nj, n_cols):
    rem = n_cols - (nj - 1) * tn
    if rem == tn:
        getattr(_weight_copy(w_hbm, land_ref, sem, jj, tn, tn), op)()
        return

    @pl.when(jj < nj - 1)
    def _():
        getattr(_weight_copy(w_hbm, land_ref, sem, jj, tn, tn), op)()

    @pl.when(jj == nj - 1)
    def _():
        getattr(_weight_copy(w_hbm, land_ref, sem, jj, tn, rem), op)()


def _stream_weights(weights, j, i, nj, tn, n_cols):
    @pl.when(i == 0)
    def _():
        @pl.when(j == 0)
        def _():
            for w_hbm, land_ref, _, sem in weights:
                _weight_copy_op("start", w_hbm, land_ref, sem, j, tn, nj, n_cols)

        for w_hbm, land_ref, bw_ref, sem in weights:
            _weight_copy_op("wait", w_hbm, land_ref, sem, j, tn, nj, n_cols)
            _cast_resident(land_ref, bw_ref)

        @pl.when(j + 1 < nj)
        def _():
            for w_hbm, land_ref, _, sem in weights:
                _weight_copy_op("start", w_hbm, land_ref, sem, j + 1, tn, nj, n_cols)


def _weight_scratch(ks, tn):
    return ([pltpu.VMEM((k, tn), F32) for k in ks] + [pltpu.VMEM((k, tn), BF16) for k in ks]
            + [pltpu.SemaphoreType.DMA(()) for _ in ks])


def _mm_in_s_kernel(a_ref, w_hbm, o_ref, land_ref, bw_ref, sem, *, nj, tn, n_cols):
    j, i = pl.program_id(0), pl.program_id(1)
    _stream_weights([(w_hbm, land_ref, bw_ref, sem)], j, i, nj, tn, n_cols)
    _per_tile_width(j, nj, tn, n_cols, lambda wd: _store_cols(o_ref, wd, _dot(a_ref[...], bw_ref[:, 0:wd])))


def _per_tile_width(j, nj, tn, n_cols, body):
    rem = n_cols - (nj - 1) * tn
    if rem == tn:
        body(tn)
        return
    pl.when(j < nj - 1)(lambda: body(tn))
    pl.when(j == nj - 1)(lambda: body(rem))


def _store_cols(o_ref, width, value):
    o_ref[:, 0:width] = value.astype(o_ref.dtype)
    if width < o_ref.shape[1]:
        o_ref[:, width:] = jnp.zeros((o_ref.shape[0], o_ref.shape[1] - width), o_ref.dtype)


def _mm_in_s(a, w, tm, tn):
    M, K = a.shape
    N = w.shape[1]
    nj = pl.cdiv(N, tn)
    return pl.pallas_call(
        functools.partial(_mm_in_s_kernel, nj=nj, tn=tn, n_cols=N),
        grid=(nj, M // tm),
        in_specs=[pl.BlockSpec((tm, K), lambda j, i: (i, 0)), pl.BlockSpec(memory_space=pl.ANY)],
        out_specs=pl.BlockSpec((tm, tn), lambda j, i: (i, j)),
        out_shape=jax.ShapeDtypeStruct((M, nj * tn), F32),
        scratch_shapes=_weight_scratch((K,), tn),
        compiler_params=_params(("arbitrary", "arbitrary"),
                                _vmem_limit(tm * K * 2, tm * tn * 4, temps=K * tn * 6 + 2 * tm * tn * 4)),
        name="mm_in",
    )(a, w)


def _mm_mix_s_kernel(a1_ref, a2_ref, z0_ref, z1_ref, bg_ref, w1_hbm, w2_hbm, o_ref,
                     l1_ref, l2_ref, b1_ref, b2_ref, s1, s2, *, nj, tn, n_cols):
    j, i = pl.program_id(0), pl.program_id(1)
    _stream_weights([(w1_hbm, l1_ref, b1_ref, s1), (w2_hbm, l2_ref, b2_ref, s2)], j, i, nj, tn, n_cols)
    g0 = jax.nn.sigmoid(z0_ref[...] + bg_ref[0:1, :])
    g1 = jax.nn.sigmoid(z1_ref[...] + bg_ref[1:2, :])
    o_ref[...] = (g0 * _dot(a1_ref[...], b1_ref[...]) + g1 * _dot(a2_ref[...], b2_ref[...])).astype(o_ref.dtype)


def _mm_mix_s(o_att, w_pa, o_rwkv, w_pr, z, b_gate, S, D, tm, tn):
    K1, K2 = o_att.shape[1], o_rwkv.shape[1]
    assert D % tn == 0
    nj = D // tn
    return pl.pallas_call(
        functools.partial(_mm_mix_s_kernel, nj=nj, tn=tn, n_cols=D),
        grid=(nj, S // tm),
        in_specs=[pl.BlockSpec((tm, K1), lambda j, i: (i, 0)), pl.BlockSpec((tm, K2), lambda j, i: (i, 0)),
                  pl.BlockSpec((tm, tn), lambda j, i: (i, j)), pl.BlockSpec((tm, tn), lambda j, i: (i, nj + j)),
                  pl.BlockSpec((2, tn), lambda j, i: (0, j)),
                  pl.BlockSpec(memory_space=pl.ANY), pl.BlockSpec(memory_space=pl.ANY)],
        out_specs=pl.BlockSpec((tm, tn), lambda j, i: (i, j)),
        out_shape=jax.ShapeDtypeStruct((S, D), BF16),
        scratch_shapes=_weight_scratch((K1, K2), tn),
        compiler_params=_params(("arbitrary", "arbitrary"),
                                _vmem_limit(tm * (K1 + K2) * 2, 2 * tm * tn * 4, tm * tn * 2,
                                            temps=(K1 + K2) * tn * 6 + 3 * tm * tn * 4)),
        name="mm_mix",
    )(o_att, o_rwkv, z, z, b_gate, w_pa, w_pr)


def _mm_out_s_kernel(a_ref, r_ref, g_ref, w_hbm, o_ref, hg_ref, ss_ref, land_ref, bw_ref, sem, *, nj, tn, n_cols):
    j, i = pl.program_id(0), pl.program_id(1)
    _stream_weights([(w_hbm, land_ref, bw_ref, sem)], j, i, nj, tn, n_cols)
    h = r_ref[...] + _dot(a_ref[...], bw_ref[...])
    o_ref[...] = h
    hg_ref[...] = (h * g_ref[...]).astype(hg_ref.dtype)
    ss_ref[...] = jnp.broadcast_to(jnp.sum(h * h, axis=1, keepdims=True), ss_ref.shape)


def _mm_out_s(a, w, res, gain, tm, tn):
    M, K = a.shape
    N = w.shape[1]
    assert N % tn == 0
    nj = N // tn
    tile = pl.BlockSpec((tm, tn), lambda j, i: (i, j))
    return pl.pallas_call(
        functools.partial(_mm_out_s_kernel, nj=nj, tn=tn, n_cols=N),
        grid=(nj, M // tm),
        in_specs=[pl.BlockSpec((tm, K), lambda j, i: (i, 0)), tile, pl.BlockSpec((1, tn), lambda j, i: (0, j)),
                  pl.BlockSpec(memory_space=pl.ANY)],
        out_specs=[tile, tile, pl.BlockSpec((tm, LANES), lambda j, i: (i, j))],
        out_shape=[jax.ShapeDtypeStruct((M, N), F32), jax.ShapeDtypeStruct((M, N), BF16),
                   jax.ShapeDtypeStruct((M, nj * LANES), F32)],
        scratch_shapes=_weight_scratch((K,), tn),
        compiler_params=_params(("arbitrary", "arbitrary"),
                                _vmem_limit(tm * K * 2, 2 * tm * tn * 4, tm * tn * 2, tm * LANES * 4,
                                            temps=K * tn * 6 + 2 * tm * tn * 4)),
        name="mm_out",
    )(a, res, gain.reshape(1, N), w)


def _mm_glu_s_kernel(a_ref, ss_ref, wg_hbm, wu_hbm, o_ref, lg_ref, lu_ref, bg_ref, bu_ref, sg, su,
                     *, nj, tn, n_cols, n_model):
    j, i = pl.program_id(0), pl.program_id(1)
    _stream_weights([(wg_hbm, lg_ref, bg_ref, sg), (wu_hbm, lu_ref, bu_ref, su)], j, i, nj, tn, n_cols)
    ss = ss_ref[:, 0:LANES]
    for t in range(1, ss_ref.shape[1] // LANES):
        ss = ss + ss_ref[:, t * LANES:(t + 1) * LANES]
    scale = lax.rsqrt(ss[:, 0:1] * (1.0 / n_model) + RMS_EPS)

    def body(wd):
        a = a_ref[...]
        gate = scale * _dot(a, bg_ref[:, 0:wd])
        up = scale * _dot(a, bu_ref[:, 0:wd])
        _store_cols(o_ref, wd, gate * jax.nn.sigmoid(gate) * up)

    _per_tile_width(j, nj, tn, n_cols, body)


def _mm_glu_s(a, ss, wg, wu, tm, tn):
    M, K = a.shape
    N = wg.shape[1]
    nj = pl.cdiv(N, tn)
    return pl.pallas_call(
        functools.partial(_mm_glu_s_kernel, nj=nj, tn=tn, n_cols=N, n_model=K),
        grid=(nj, M // tm),
        in_specs=[pl.BlockSpec((tm, K), lambda j, i: (i, 0)), pl.BlockSpec((tm, ss.shape[1]), lambda j, i: (i, 0)),
                  pl.BlockSpec(memory_space=pl.ANY), pl.BlockSpec(memory_space=pl.ANY)],
        out_specs=pl.BlockSpec((tm, tn), lambda j, i: (i, j)),
        out_shape=jax.ShapeDtypeStruct((M, N), BF16),
        scratch_shapes=_weight_scratch((K, K), tn),
        compiler_params=_params(("arbitrary", "arbitrary"),
                                _vmem_limit(tm * K * 2, tm * ss.shape[1] * 4, tm * tn * 2,
                                            temps=2 * K * tn * 6 + 6 * tm * tn * 4)),
        name="mm_glu",
    )(a, ss, wg, wu)


def _mm_res_kernel(a_ref, b_ref, r_ref, o_ref, *, nk):
    k = pl.program_id(2)
    if nk == 1:
        o_ref[...] = r_ref[...] + _dot(a_ref[...], b_ref[...])
    else:
        @pl.when(k == 0)
        def _():
            o_ref[...] = r_ref[...] + _dot(a_ref[...], b_ref[...])

        @pl.when(k > 0)
        def _():
            o_ref[...] = o_ref[...] + _dot(a_ref[...], b_ref[...])


def _mm_res(a, b, res, tm, tn, tk):
    M, K = a.shape
    N = b.shape[1]
    nk = K // tk
    return pl.pallas_call(
        functools.partial(_mm_res_kernel, nk=nk),
        grid=(M // tm, N // tn, nk),
        in_specs=[pl.BlockSpec((tm, tk), lambda i, j, k: (i, k)), pl.BlockSpec((tk, tn), lambda i, j, k: (k, j)),
                  pl.BlockSpec((tm, tn), lambda i, j, k: (i, j))],
        out_specs=pl.BlockSpec((tm, tn), lambda i, j, k: (i, j)),
        out_shape=jax.ShapeDtypeStruct((M, N), F32),
        compiler_params=_params(("arbitrary", "arbitrary", "arbitrary"),
                                _vmem_limit(tm * tk * 2, tk * tn * 2, 2 * tm * tn * 4, temps=tm * tn * 4)),
        name="mm_down",
    )(a, b, res)


def _mm_glu_kernel(a_ref, bg_ref, bu_ref, o_ref, wg_ref, wu_ref):
    @pl.when(pl.program_id(1) == 0)
    def _():
        _cast_resident(bg_ref, wg_ref)
        _cast_resident(bu_ref, wu_ref)

    a = a_ref[...]
    gate = _dot(a, wg_ref[...])
    up = _dot(a, wu_ref[...])
    o_ref[...] = (gate * jax.nn.sigmoid(gate) * up).astype(o_ref.dtype)


def _mm_glu(a, wg, wu, tm, tn):
    M, K = a.shape
    N = wg.shape[1]
    return pl.pallas_call(
        _mm_glu_kernel,
        grid=(N // tn, M // tm),
        in_specs=[pl.BlockSpec((tm, K), lambda j, i: (i, 0)), pl.BlockSpec((K, tn), lambda j, i: (0, j)),
                  pl.BlockSpec((K, tn), lambda j, i: (0, j))],
        out_specs=pl.BlockSpec((tm, tn), lambda j, i: (i, j)),
        out_shape=jax.ShapeDtypeStruct((M, N), BF16),
        scratch_shapes=[pltpu.VMEM((K, tn), BF16), pltpu.VMEM((K, tn), BF16)],
        compiler_params=_params(("arbitrary", "arbitrary"),
                                _vmem_limit(tm * K * 2, 2 * K * tn * 4, tm * tn * 2,
                                            temps=2 * K * tn * 2 + 3 * tm * tn * 4)),
        name="mm_glu",
    )(a, wg, wu)


def _rope_kernel(z_ref, cos_ref, sin_ref, o_ref, *, n_rope_blocks, heads_per_block, k_heads_last):
    j = pl.program_id(1)
    cos = cos_ref[...]
    sin = sin_ref[...]
    lane = lax.broadcasted_iota(jnp.int32, cos.shape, 1)
    half = ROPE_DIMS // 2
    for h in range(heads_per_block):
        t = z_ref[:, h * HEAD_DIM:(h + 1) * HEAD_DIM]
        up = jnp.concatenate([t[:, half:], t[:, :half]], axis=1)
        down = jnp.concatenate([t[:, HEAD_DIM - half:], t[:, :HEAD_DIM - half]], axis=1)
        partner = jnp.where(lane < half, up, down)
        rot = t * cos + partner * sin
        if h >= k_heads_last:
            rot = jnp.where(j == n_rope_blocks - 1, t, rot)
        o_ref[:, h * HEAD_DIM:(h + 1) * HEAD_DIM] = rot.astype(o_ref.dtype)


def _rope_cast(z, cos_t, sin_t, off_q, width, kv_width):
    Lp = z.shape[0]
    tc = 2 * kv_width
    assert off_q % tc == 0 and width % tc == 0
    nblk = width // tc
    return pl.pallas_call(
        functools.partial(_rope_kernel, n_rope_blocks=nblk, heads_per_block=tc // HEAD_DIM,
                          k_heads_last=kv_width // HEAD_DIM),
        grid=(Lp // BLOCK, nblk),
        in_specs=[pl.BlockSpec((BLOCK, tc), lambda i, j: (i, off_q // tc + j)),
                  pl.BlockSpec((BLOCK, HEAD_DIM), lambda i, j: (i, 0)),
                  pl.BlockSpec((BLOCK, HEAD_DIM), lambda i, j: (i, 0))],
        out_specs=pl.BlockSpec((BLOCK, tc), lambda i, j: (i, j)),
        out_shape=jax.ShapeDtypeStruct((Lp, width), BF16),
        compiler_params=_params(("arbitrary", "arbitrary"), _vmem_limit(BLOCK * tc * 4, BLOCK * tc * 2)),
        name="rope_cast",
    )(z, cos_t, sin_t)


NEG_BIG = -1e30


def _attn_kernel(q_ref, kp_ref, kc_ref, kn_ref, km_ref, vp_ref, vc_ref, vn_ref, vm_ref, sink_ref, o_ref,
                 *, nb, group):
    i = pl.program_id(1)
    scale = HEAD_DIM ** -0.5
    zpad = jnp.zeros((BLOCK - N_META, HEAD_DIM), BF16)
    k_all = jnp.concatenate([kp_ref[...], kc_ref[...], kn_ref[...], km_ref[...], zpad], axis=0)
    v_all = jnp.concatenate([vp_ref[...], vc_ref[...], vn_ref[...], vm_ref[...], zpad], axis=0)
    q_all = jnp.concatenate([q_ref[:, g * HEAD_DIM:(g + 1) * HEAD_DIM] for g in range(group)], axis=0)
    s = _dot(q_all, k_all, NT) * scale
    rows = lax.broadcasted_iota(jnp.int32, s.shape, 0) % BLOCK
    cols = lax.broadcasted_iota(jnp.int32, s.shape, 1)
    rel = cols - BLOCK - rows
    kblk = i - 1 + cols // BLOCK
    band_ok = (cols < 3 * BLOCK) & (kblk >= 0) & (kblk < nb) & (jnp.abs(rel) <= WINDOW)
    meta_ok = (cols >= 3 * BLOCK) & (cols < 3 * BLOCK + N_META)
    s = jnp.where(band_ok | meta_ok, s, NEG_BIG)
    sink = sink_ref[:, 0:1]
    m = jnp.maximum(jnp.max(s, axis=-1, keepdims=True), sink)
    p = jnp.exp(s - m)
    denom = jnp.sum(p, axis=-1, keepdims=True) + jnp.exp(sink - m)
    o = _dot(p.astype(BF16), v_all) / denom
    for g in range(group):
        o_ref[:, g * HEAD_DIM:(g + 1) * HEAD_DIM] = o[g * BLOCK:(g + 1) * BLOCK, :].astype(o_ref.dtype)


def _attention(qkv, sink, S, n_heads, n_kv):
    group = n_heads // n_kv
    nb = S // BLOCK
    qw = group * HEAD_DIM
    kcol = n_heads
    vcol = n_heads + n_kv
    meta_blk = (S + PAD_ROWS - N_META) // N_META
    sink_rows = jnp.broadcast_to(sink.astype(F32).reshape(n_kv, group, 1, 1),
                                 (n_kv, group, BLOCK, LANES)).reshape(n_kv, group * BLOCK, LANES)

    def kv_specs(col0):
        return [pl.BlockSpec((BLOCK, HEAD_DIM), lambda h, i: (jnp.maximum(i - 1, 0), col0 + h)),
                pl.BlockSpec((BLOCK, HEAD_DIM), lambda h, i: (i, col0 + h)),
                pl.BlockSpec((BLOCK, HEAD_DIM), lambda h, i: (jnp.minimum(i + 1, nb - 1), col0 + h)),
                pl.BlockSpec((N_META, HEAD_DIM), lambda h, i: (meta_blk, col0 + h))]

    return pl.pallas_call(
        functools.partial(_attn_kernel, nb=nb, group=group),
        grid=(n_kv, nb),
        in_specs=[pl.BlockSpec((BLOCK, qw), lambda h, i: (i, h))] + kv_specs(kcol) + kv_specs(vcol)
                 + [pl.BlockSpec((None, group * BLOCK, LANES), lambda h, i: (h, 0, 0))],
        out_specs=pl.BlockSpec((BLOCK, qw), lambda h, i: (i, h)),
        out_shape=jax.ShapeDtypeStruct((S, n_heads * HEAD_DIM), BF16),
        compiler_params=_params(("arbitrary", "arbitrary"),
                                _vmem_limit(BLOCK * qw * 2 * 2, 8 * BLOCK * HEAD_DIM * 2,
                                            group * BLOCK * LANES * 4, temps=6 * group * BLOCK * 512 * 4)),
        name="attention",
    )(qkv, qkv, qkv, qkv, qkv, qkv, qkv, qkv, qkv, sink_rows)


def _rope(t, cos, sin):
    half = ROPE_DIMS // 2
    lane = lax.broadcasted_iota(jnp.int32, t.shape, 1)
    up = jnp.concatenate([t[:, half:], t[:, :half]], axis=1)
    down = jnp.concatenate([t[:, HEAD_DIM - half:], t[:, :HEAD_DIM - half]], axis=1)
    return t * cos + jnp.where(lane < half, up, down) * sin


def _attn_res_kernel(sink_ref, q_ref, k_ref, v_ref, cos_ref, sin_ref, o_ref, kb_ref, vb_ref,
                     *, nb, group, qblocks):
    h, i = pl.program_id(0), pl.program_id(1)
    lp = k_ref.shape[0]

    @pl.when(i == 0)
    def _():
        def body(c, carry):
            sl = pl.ds(pl.multiple_of(c * BLOCK, BLOCK), BLOCK)
            kb_ref[sl, :] = _rope(k_ref[sl, :], cos_ref[sl, :], sin_ref[sl, :]).astype(BF16)
            vb_ref[sl, :] = v_ref[sl, :].astype(BF16)
            return carry
        lax.fori_loop(0, lp // BLOCK, body, 0)

    zpad = jnp.zeros((BLOCK - N_META, HEAD_DIM), BF16)
    k_meta = jnp.concatenate([kb_ref[lp - N_META:lp, :], zpad], axis=0)
    v_meta = jnp.concatenate([vb_ref[lp - N_META:lp, :], zpad], axis=0)
    rows = lax.broadcasted_iota(jnp.int32, (BLOCK, 4 * BLOCK), 0)
    cols = lax.broadcasted_iota(jnp.int32, (BLOCK, 4 * BLOCK), 1)
    is_meta = (cols >= 3 * BLOCK) & (cols < 3 * BLOCK + N_META)
    scale = HEAD_DIM ** -0.5
    k_all, v_all, bias, q, sink = [], [], [], [], []
    for b in range(qblocks):
        qi = i * qblocks + b
        kb0 = jnp.clip(qi - 1, 0, nb - 3)
        win = pl.ds(pl.multiple_of(kb0 * BLOCK, BLOCK), 3 * BLOCK)
        k_b = jnp.concatenate([kb_ref[win, :], k_meta], axis=0)
        v_b = jnp.concatenate([vb_ref[win, :], v_meta], axis=0)
        rel = cols - rows + (kb0 - qi) * BLOCK
        ok = ((cols < 3 * BLOCK) & (jnp.abs(rel) <= WINDOW)) | is_meta
        bias_b = jnp.where(ok, 0.0, NEG_BIG).astype(F32)
        qrows = pl.ds(pl.multiple_of(qi * BLOCK, BLOCK), BLOCK)
        cos_q, sin_q = cos_ref[qrows, :], sin_ref[qrows, :]
        for g in range(group):
            k_all.append(k_b)
            v_all.append(v_b)
            bias.append(bias_b)
            q.append(_rope(q_ref[b * BLOCK:(b + 1) * BLOCK, g * HEAD_DIM:(g + 1) * HEAD_DIM],
                           cos_q, sin_q).astype(BF16))
            sink.append(sink_ref[h * group + g])
    n = range(len(q))
    s = [_dot(q[c], k_all[c], NT) * scale + bias[c] for c in n]
    m = [jnp.maximum(jnp.max(s[c], axis=-1, keepdims=True), sink[c]) for c in n]
    p = [jnp.exp(s[c] - m[c]) for c in n]
    denom = [jnp.sum(p[c], axis=-1, keepdims=True) + jnp.exp(sink[c] - m[c]) for c in n]
    o = [_dot(p[c].astype(BF16), v_all[c]) / denom[c] for c in n]
    for c in n:
        b, g = divmod(c, group)
        o_ref[b * BLOCK:(b + 1) * BLOCK, g * HEAD_DIM:(g + 1) * HEAD_DIM] = o[c].astype(o_ref.dtype)


def _attention_res(z, cos_t, sin_t, sink, S, n_heads, n_kv, off_q):
    Lp = z.shape[0]
    group = n_heads // n_kv
    nb = S // BLOCK
    assert nb >= 3
    qw = group * HEAD_DIM
    kcol = off_q // HEAD_DIM + n_heads
    vcol = kcol + n_kv
    col_bytes = Lp * HEAD_DIM * 4
    qblocks = _pick(nb, (4, 2, 1))
    tq = qblocks * BLOCK
    return pl.pallas_call(
        functools.partial(_attn_res_kernel, nb=nb, group=group, qblocks=qblocks),
        grid_spec=pltpu.PrefetchScalarGridSpec(
            num_scalar_prefetch=1,
            grid=(n_kv, nb // qblocks),
            in_specs=[pl.BlockSpec((tq, qw), lambda h, i, s: (i, off_q // qw + h)),
                      pl.BlockSpec((Lp, HEAD_DIM), lambda h, i, s: (0, kcol + h)),
                      pl.BlockSpec((Lp, HEAD_DIM), lambda h, i, s: (0, vcol + h)),
                      pl.BlockSpec((Lp, HEAD_DIM), lambda h, i, s: (0, 0)),
                      pl.BlockSpec((Lp, HEAD_DIM), lambda h, i, s: (0, 0))],
            out_specs=pl.BlockSpec((tq, qw), lambda h, i, s: (i, h)),
            scratch_shapes=[pltpu.VMEM((Lp, HEAD_DIM), BF16), pltpu.VMEM((Lp, HEAD_DIM), BF16)]),
        out_shape=jax.ShapeDtypeStruct((S, n_heads * HEAD_DIM), BF16),
        compiler_params=_params(("arbitrary", "arbitrary"),
                                _vmem_limit(tq * qw * 4, 4 * col_bytes, tq * qw * 2,
                                            temps=col_bytes + 8 * group * tq * 512 * 4)),
        name="attention",
    )(sink.astype(F32), z, z, z, cos_t, sin_t)


def _split3(x):
    hi = x.astype(BF16)
    r1 = x - hi.astype(F32)
    mid = r1.astype(BF16)
    lo = (r1 - mid.astype(F32)).astype(BF16)
    return hi, mid, lo


def _split2(x):
    hi = x.astype(BF16)
    return hi, (x - hi.astype(F32)).astype(BF16)


def _dot_exact_rhs(x, m_bf16, dims=NN, left=False):
    parts = _split3(x)
    if left:
        return sum(_dot(m_bf16, p, dims) for p in parts)
    return sum(_dot(p, m_bf16, dims) for p in parts)


def _dotp(a, b, passes, dims=NN):
    if passes == 1:
        return _dot(a.astype(BF16), b.astype(BF16), dims)
    a_hi, a_lo = _split2(a)
    b_hi, b_lo = _split2(b)
    return _dot(a_hi, b_hi, dims) + (_dot(a_hi, b_lo, dims) + _dot(a_lo, b_hi, dims))


def _head_sum(x, ones_bd):
    outs = []
    for s in range(x.shape[1] // LANES):
        outs.append(_dot_exact_rhs(x[:, s * LANES:(s + 1) * LANES], ones_bd))
    return outs[0] if len(outs) == 1 else jnp.concatenate(outs, axis=1)


def _prep_kernel(r_ref, rp_ref, rn_ref, k_ref, kp_ref, kn_ref, v_ref, vp_ref, vn_ref,
                 lo_ref, lop_ref, lon_ref, gd_ref, gdp_ref, gdn_ref,
                 pc_ref, pl_ref, wl_ref, g2_ref, ones_ref,
                 ro_ref, vo_ref, kko_ref, lwf_ref, lwb_ref, kdf_ref, kdb_ref, bdf_ref, bdb_ref, go_ref, bo_ref,
                 *, n_tiles, tr, lora_tanh_cols):
    i = pl.program_id(1)

    def shifted(main_ref, prev_ref, next_ref, mup, mun):
        x = main_ref[...]
        row8 = lax.broadcasted_iota(jnp.int32, (8, x.shape[1]), 0)
        down = pltpu.roll(x, 1, 0)
        up = pltpu.roll(x, tr - 1, 0)
        prev = jnp.concatenate([jnp.where(row8 == 0, prev_ref[7:8, :], down[0:8]), down[8:]], axis=0)
        nxt = jnp.concatenate([up[:tr - 8], jnp.where(row8 == 7, next_ref[0:1, :], up[tr - 8:])], axis=0)
        return x + mup * (prev - x) + mun * (nxt - x)

    r = shifted(r_ref, rp_ref, rn_ref, pc_ref[0:1, :], pc_ref[1:2, :])
    k = shifted(k_ref, kp_ref, kn_ref, pc_ref[2:3, :], pc_ref[3:4, :])
    v = shifted(v_ref, vp_ref, vn_ref, pc_ref[4:5, :], pc_ref[5:6, :])
    lo = shifted(lo_ref, lop_ref, lon_ref, pl_ref[0:1, 0:LORA_PAD], pl_ref[1:2, 0:LORA_PAD])
    gd = shifted(gd_ref, gdp_ref, gdn_ref, pl_ref[0:1, LORA_PAD:], pl_ref[1:2, LORA_PAD:])

    row = lax.broadcasted_iota(jnp.int32, (tr, 1), 0)
    valid = jnp.where((i < n_tiles - 1) | (row >= tr - N_META), 1.0, 0.0).astype(F32)

    lane = lax.broadcasted_iota(jnp.int32, lo.shape, 1)
    lo_act = jnp.where(lane < lora_tanh_cols, jnp.tanh(lo), lo).astype(BF16)
    dec_f = _dot(lo_act, wl_ref[0])
    dec_b = _dot(lo_act, wl_ref[1])
    apre_f = _dot(lo_act, wl_ref[2])
    apre_b = _dot(lo_act, wl_ref[3])
    g = _dot(jax.nn.sigmoid(lo).astype(BF16), g2_ref[0]) + _dot(jax.nn.sigmoid(gd).astype(BF16), g2_ref[1])

    def log_decay(dec, w0):
        return -math.exp(-0.5) * jax.nn.sigmoid(w0 + dec)

    a_f = jax.nn.sigmoid(pc_ref[8:9, :] + apre_f)
    a_b = jax.nn.sigmoid(pc_ref[9:10, :] + apre_b)
    kk = k * pc_ref[10:11, :]
    ss = _head_sum(kk * kk, ones_ref[...])
    kk = kk / jnp.maximum(jnp.sqrt(ss), 1e-12) * valid
    k_a = pc_ref[11:12, :]
    kv = k * valid

    rv = r * valid
    vv = v * valid
    kd_f = kv * (1.0 + (a_f - 1.0) * k_a)
    kd_b = kv * (1.0 + (a_b - 1.0) * k_a)
    ro_ref[...] = rv
    vo_ref[...] = vv
    kko_ref[...] = kk
    lwf_ref[...] = log_decay(dec_f, pc_ref[6:7, :])
    lwb_ref[...] = log_decay(dec_b, pc_ref[7:8, :])
    kdf_ref[...] = kd_f
    kdb_ref[...] = kd_b
    bdf_ref[...] = kk * a_f
    bdb_ref[...] = kk * a_b
    go_ref[...] = g
    bo_ref[...] = _head_sum(rv * (kd_f + kd_b) * pc_ref[12:13, :], ones_ref[...]) * vv


def _rwkv_prep(z, off_r, C, pc, plo, wl, g2, ones_bd, lora_tanh_cols):
    Lp = z.shape[0]
    tr = BLOCK
    ct = _pick(C, (1024, 512))
    n_tiles = Lp // tr
    n8 = Lp // 8
    off_lo = off_r + 3 * C
    off_gd = off_lo + LORA_PAD
    assert off_r % ct == 0 and off_lo % LORA_PAD == 0

    def seg_specs(off, w, with_c):
        cb = off // w

        def col(c):
            return cb + c if with_c else cb
        return [pl.BlockSpec((tr, w), lambda c, i: (i, col(c))),
                pl.BlockSpec((8, w), lambda c, i: ((i * (tr // 8) + n8 - 1) % n8, col(c))),
                pl.BlockSpec((8, w), lambda c, i: (((i + 1) * (tr // 8)) % n8, col(c)))]

    in_specs = (seg_specs(off_r, ct, True) + seg_specs(off_r + C, ct, True) + seg_specs(off_r + 2 * C, ct, True)
                + seg_specs(off_lo, LORA_PAD, False) + seg_specs(off_gd, LORA_PAD, False)
                + [pl.BlockSpec((16, ct), lambda c, i: (0, c)),
                   pl.BlockSpec((8, 2 * LORA_PAD), lambda c, i: (0, 0)),
                   pl.BlockSpec((4, LORA_PAD, ct), lambda c, i: (0, 0, c)),
                   pl.BlockSpec((2, LORA_PAD, ct), lambda c, i: (0, 0, c)),
                   pl.BlockSpec((LANES, LANES), lambda c, i: (0, 0))])
    out_spec = pl.BlockSpec((tr, ct), lambda c, i: (i, c))
    out_sds = jax.ShapeDtypeStruct((Lp, C), F32)
    return pl.pallas_call(
        functools.partial(_prep_kernel, n_tiles=n_tiles, tr=tr, lora_tanh_cols=lora_tanh_cols),
        grid=(C // ct, n_tiles),
        in_specs=in_specs,
        out_specs=[out_spec] * 11,
        out_shape=[out_sds] * 11,
        compiler_params=_params(("arbitrary", "arbitrary"),
                                _vmem_limit(3 * tr * ct * 4, 2 * tr * LORA_PAD * 4, 4 * LORA_PAD * ct * 2,
                                            2 * LORA_PAD * ct * 2, 11 * tr * ct * 4, temps=24 * tr * ct * 4)),
        name="rwkv_prep",
    )(*([z] * 15), pc, plo, wl, g2, ones_bd)


SCAN_PAIRS = 16


def _scan_consts():
    C = CHUNK
    row = lax.broadcasted_iota(jnp.int32, (C, 2 * C), 0)
    col = lax.broadcasted_iota(jnp.int32, (C, 2 * C), 1) % C
    rr = lax.broadcasted_iota(jnp.int32, (LANES, LANES), 0)
    cc = lax.broadcasted_iota(jnp.int32, (LANES, LANES), 1)
    return dict(
        row=lax.broadcasted_iota(jnp.int32, (C, LANES), 0),
        head0=lax.broadcasted_iota(jnp.int32, (C, LANES), 1) < RWKV_HEAD,
        strict_f=col < row, strict_b=col > row, incl_f=col <= row, incl_b=col >= row,
        eye_sbs=jnp.where(col == row, 1.0, 0.0).astype(F32),
        same_head=(rr // RWKV_HEAD) == (cc // RWKV_HEAD), eye=rr == cc)


def _scan_chains(chains, cst):
    C = CHUNK
    head0, same, eye = cst["head0"], cst["same_head"], cst["eye"]
    n = range(len(chains))
    rev = [c[7] for c in chains]
    r, v, al, lw, k, be, h = ([c[i] for c in chains] for i in range(7))
    strict = [cst["strict_b" if x else "strict_f"] for x in rev]
    incl = [cst["incl_b" if x else "incl_f"] for x in rev]

    def bd(x):
        xb = x.astype(BF16)
        zero = jnp.zeros_like(xb)
        return jnp.concatenate([jnp.where(head0, xb, zero), jnp.where(head0, zero, xb)], axis=0)

    def mm(a, b_bf16, dims=NN):
        return _dot(a.astype(BF16), b_bf16, dims)

    row = cst["row"]
    cl = list(lw)
    sh = 1
    while sh < C:
        cl = [cl[i] + (jnp.where(row < C - sh, pltpu.roll(cl[i], C - sh, 0), 0.0) if rev[i] else
                       jnp.where(row >= sh, pltpu.roll(cl[i], sh, 0), 0.0)) for i in n]
        sh *= 2
    total = [cl[i][0:1, :] if rev[i] else cl[i][C - 1:C, :] for i in n]
    a_t = [al[i] * jnp.exp(cl[i] - lw[i]) for i in n]
    r_t = [r[i] * jnp.exp(cl[i]) for i in n]
    w_inv = [jnp.exp(-cl[i]) for i in n]
    w_rest = [jnp.exp(total[i] - cl[i]) for i in n]
    bd_v = [bd(v[i]) for i in n]
    sc = [mm(jnp.concatenate([a_t[i], r_t[i]], axis=0),
             jnp.concatenate([bd(k[i] * w_inv[i]), bd(be[i] * w_inv[i])], axis=0), NT) for i in n]
    a_ak = [jnp.where(strict[i], sc[i][0:C, 0:2 * C], 0.0) for i in n]
    n_ab = [jnp.where(strict[i], sc[i][0:C, 2 * C:], 0.0) for i in n]
    a_rk = [jnp.where(incl[i], sc[i][C:, 0:2 * C], 0.0) for i in n]
    a_rb = [jnp.where(incl[i], sc[i][C:, 2 * C:], 0.0) for i in n]

    t_inv = [cst["eye_sbs"] - n_ab[i] for i in n]
    sq = [mm(n_ab[i], bd(n_ab[i])) for i in n]
    n_round = 2
    while (1 << n_round) < C:
        both = [mm(jnp.concatenate([t_inv[i], sq[i]], axis=0), bd(sq[i])) for i in n]
        t_inv = [t_inv[i] + both[i][0:C] for i in n]
        sq = [both[i][C:] for i in n]
        n_round += 1
    t_inv = [t_inv[i] + mm(t_inv[i], bd(sq[i])) for i in n]

    akv = [mm(jnp.concatenate([a_ak[i], a_rk[i]], axis=0), bd_v[i]) for i in n]
    tp = [mm(t_inv[i], jnp.concatenate([bd(akv[i][0:C]), bd(a_t[i])], axis=1)) for i in n]
    p0 = [tp[i][:, 0:LANES] for i in n]
    at = [tp[i][:, LANES:] for i in n]
    rb = [mm(a_rb[i], jnp.concatenate([bd(p0[i]), bd(at[i])], axis=1)) for i in n]
    y0 = [akv[i][C:] - rb[i][:, 0:LANES] for i in n]
    rh = [r_t[i] - rb[i][:, LANES:] for i in n]
    bt = [mm(be[i] * w_rest[i], jnp.concatenate([at[i], p0[i]], axis=1).astype(BF16), TN) for i in n]
    kv = [mm(k[i] * w_rest[i], v[i].astype(BF16), TN) for i in n]
    m_mat = [jnp.where(same, jnp.where(eye, jnp.broadcast_to(jnp.exp(total[i]), (LANES, LANES)), 0.0)
                       - bt[i][:, 0:LANES], 0.0) for i in n]
    g_mat = [jnp.where(same, kv[i] - bt[i][:, LANES:], 0.0) for i in n]
    yh = [mm(jnp.concatenate([rh[i], m_mat[i]], axis=0), h[i].astype(BF16)) for i in n]
    return [(yh[i][0:C] + y0[i], yh[i][C:] + g_mat[i]) for i in n]


def _scan_kernel(rf_ref, vf_ref, af_ref, lwf_ref, kf_ref, bf_ref,
                 rb_ref, vb_ref, ab_ref, lwb_ref, kb_ref, bb_ref, wd_ref,
                 yf_ref, yb_ref, wdo_ref, hf_ref, hb_ref, *, pairs):
    @pl.when(pl.program_id(1) == 0)
    def _():
        hf_ref[...] = jnp.zeros(hf_ref.shape, F32)
        hb_ref[...] = jnp.zeros(hb_ref.shape, F32)

    wdo_ref[...] = wd_ref[...].astype(wdo_ref.dtype)

    cst = _scan_consts()
    chains = []
    for g in range(pairs):
        sl = slice(g * LANES, (g + 1) * LANES)
        chains.append((rf_ref[:, sl], vf_ref[:, sl], af_ref[:, sl], lwf_ref[:, sl], kf_ref[:, sl], bf_ref[:, sl],
                       hf_ref[g], False))
        chains.append((rb_ref[:, sl], vb_ref[:, sl], ab_ref[:, sl], lwb_ref[:, sl], kb_ref[:, sl], bb_ref[:, sl],
                       hb_ref[g], True))
    res = _scan_chains(chains, cst)
    for g in range(pairs):
        sl = slice(g * LANES, (g + 1) * LANES)
        yf_ref[:, sl], hf_ref[g] = res[2 * g]
        yb_ref[:, sl], hb_ref[g] = res[2 * g + 1]


def _rwkv_scan(r, v, kk, lw_f, lw_b, kd_f, kd_b, bd_f, bd_b, w_cast, S):
    Lp, C = r.shape
    n_real = S // CHUNK
    n_chunks = Lp // CHUNK
    steps = n_real + 1
    pairs = _pick(C // LANES, (SCAN_PAIRS, 2, 1))
    w = pairs * LANES
    total_steps = (C // w) * steps
    wr, wc = w_cast.shape
    cast_rows = next(rb for rb in (128, 256, 512, 1024, wr) if wr % rb == 0 and wr // rb <= total_steps)
    n_cast = wr // cast_rows
    cast_spec = pl.BlockSpec((cast_rows, wc), lambda p, s: (jnp.minimum(p * steps + s, n_cast - 1), 0))

    def fwd(p, s):
        return ((s + n_real + 1) % n_chunks, p)

    def bwd(p, s):
        return (n_real - s, p)

    spec_f = pl.BlockSpec((CHUNK, w), fwd)
    spec_b = pl.BlockSpec((CHUNK, w), bwd)
    out_sds = jax.ShapeDtypeStruct((S + CHUNK, C), F32)
    out_f = pl.BlockSpec((CHUNK, w), lambda p, s: ((s + n_real) % (n_real + 1), p))
    return pl.pallas_call(
        functools.partial(_scan_kernel, pairs=pairs),
        grid=(C // w, steps),
        in_specs=[spec_f] * 6 + [spec_b] * 6 + [cast_spec],
        out_specs=[out_f, spec_b, cast_spec],
        out_shape=[out_sds, out_sds, jax.ShapeDtypeStruct((wr, wc), BF16)],
        scratch_shapes=[pltpu.VMEM((pairs, LANES, LANES), F32), pltpu.VMEM((pairs, LANES, LANES), F32)],
        compiler_params=_params(("arbitrary", "arbitrary"),
                                _vmem_limit(14 * CHUNK * w * 4, cast_rows * wc * 6, temps=16 << 20)),
        name="rwkv_scan",
    )(r, v, kk, lw_f, kd_f, bd_f, r, v, kk, lw_b, kd_b, bd_b, w_cast)


def _post_kernel(yf_ref, yb_ref, bonus_ref, g_ref, pq_ref, ones_ref, o_ref):
    ones_bd = ones_ref[...]
    inv_n = 1.0 / RWKV_HEAD
    y = yf_ref[...] + yb_ref[...]
    mean = _head_sum(y, ones_bd) * inv_n
    yc = y - mean
    var = _head_sum(yc * yc, ones_bd) * inv_n
    yn = yc * lax.rsqrt(var + GN_EPS) * pq_ref[0:1, :] + pq_ref[1:2, :]
    o_ref[...] = ((yn + bonus_ref[...]) * g_ref[...]).astype(o_ref.dtype)


def _rwkv_post(y_f, y_b, bonus, g, pq, ones_bd, S):
    C = g.shape[1]
    tr = 512
    ct = _pick(C, (1024, 512))
    spec = pl.BlockSpec((tr, ct), lambda i, c: (i, c))
    return pl.pallas_call(
        _post_kernel,
        grid=(S // tr, C // ct),
        in_specs=[spec] * 4 + [pl.BlockSpec((8, ct), lambda i, c: (0, c)),
                               pl.BlockSpec((LANES, LANES), lambda i, c: (0, 0))],
        out_specs=spec,
        out_shape=jax.ShapeDtypeStruct((S, C), BF16),
        compiler_params=_params(("arbitrary", "arbitrary"),
                                _vmem_limit(4 * tr * ct * 4, tr * ct * 2, temps=16 * tr * ct * 4)),
        name="rwkv_post",
    )(y_f, y_b, bonus, g, pq, ones_bd)


def _pad_cols(w, width):
    return jnp.pad(w, ((0, 0), (0, width - w.shape[1])))


def kernel(x, meta_tokens, norm_mix_w, w_in, b_gate, mu_prev, mu_next, dec_w0, dec_w2, iclr_a0, iclr_a2,
           gate_w2, k_k, k_a, r_k, ln_x_w, ln_x_b, attn_sink, w_proj_attn, w_proj_rwkv, w_out, norm_ffn_w,
           w_ffn_gate, w_ffn_up, w_ffn_down, norm_final_w):
    B, S, D = x.shape
    assert B == 1 and norm_mix_w.shape[0] == 1, "one sequence, one layer"
    C = k_k.shape[-1]
    AW = w_proj_attn.shape[1]
    n_heads = attn_sink.shape[-1]
    DL, IL, GL = dec_w2.shape[2], iclr_a2.shape[2], gate_w2.shape[1]
    shift_w = mu_prev.shape[-1]
    KVW = (w_in.shape[-1] - 2 * D - AW - shift_w) // 2
    n_kv = KVW // HEAD_DIM
    F = w_ffn_gate.shape[-1]
    lora_w = 2 * DL + 2 * IL
    assert shift_w == 3 * C + lora_w + GL and lora_w <= LORA_PAD and GL <= LORA_PAD
    assert S % BLOCK == 0 and C % LANES == 0 and n_heads * HEAD_DIM == AW
    Lp = S + PAD_ROWS

    off_q = 2 * D
    off_r = off_q + AW + 2 * KVW
    off_lo = off_r + 3 * C
    tn_in = 1024
    n_in_pad = -(-w_in.shape[-1] // tn_in) * tn_in
    assert n_in_pad >= off_lo + 2 * LORA_PAD and off_lo % LORA_PAD == 0
    low_w = lora_w + GL
    gl_a = LORA_PAD - lora_w

    mu_p, mu_n = mu_prev[0], mu_next[0]
    zc = jnp.zeros((C,), F32)
    pc = jnp.stack([mu_p[:C], mu_n[:C], mu_p[C:2 * C], mu_n[C:2 * C], mu_p[2 * C:3 * C], mu_n[2 * C:3 * C],
                    dec_w0[0, 0], dec_w0[0, 1], iclr_a0[0, 0], iclr_a0[0, 1], k_k[0], k_a[0], r_k[0], zc, zc, zc])
    plo = jnp.zeros((8, 2 * LORA_PAD), F32).at[0, :low_w].set(mu_p[3 * C:]).at[1, :low_w].set(mu_n[3 * C:])
    wl = jnp.zeros((4, LORA_PAD, C), F32)
    wl = wl.at[0, 0:DL].set(dec_w2[0, 0]).at[1, DL:2 * DL].set(dec_w2[0, 1])
    wl = wl.at[2, 2 * DL:2 * DL + IL].set(iclr_a2[0, 0]).at[3, 2 * DL + IL:lora_w].set(iclr_a2[0, 1])
    wl = wl.astype(BF16)
    g2 = jnp.zeros((2, LORA_PAD, C), F32).at[0, lora_w:].set(gate_w2[0, :gl_a]).at[1, :GL - gl_a].set(
        gate_w2[0, gl_a:]).astype(BF16)
    pq = jnp.zeros((8, C), F32).at[0].set(ln_x_w[0]).at[1].set(ln_x_b[0])
    lane = jnp.arange(LANES)
    ones_bd = (lane[:, None] // RWKV_HEAD == lane[None, :] // RWKV_HEAD).astype(BF16)


    pos = jnp.concatenate([jnp.arange(N_META, N_META + S), jnp.zeros((PAD_ROWS - N_META,), jnp.int32),
                           jnp.arange(N_META)]).astype(F32)
    inv = ROPE_THETA ** (-jnp.arange(0, ROPE_DIMS, 2, dtype=F32) / ROPE_DIMS)
    ang = pos[:, None] * inv[None, :]
    ones_rest = jnp.ones((Lp, HEAD_DIM - ROPE_DIMS), F32)
    cos_t = jnp.concatenate([jnp.cos(ang), jnp.cos(ang), ones_rest], axis=1)
    sin_t = jnp.concatenate([-jnp.sin(ang), jnp.sin(ang), 0.0 * ones_rest], axis=1)

    x2 = x[0]
    h_ext = _norm_ext(x2, meta_tokens.astype(x.dtype), norm_mix_w[0])
    z = _mm_in_s(h_ext, w_in[0], _pick(Lp, (1040, 640)), tn_in)
    o_att = _attention_res(z, cos_t, sin_t, attn_sink[0], S, n_heads, n_kv, off_q)
    r_s, v_s, kk, lw_f, lw_b, kd_f, kd_b, bd_f, bd_b, g, bonus = _rwkv_prep(
        z, off_r, C, pc, plo, wl, g2, ones_bd, 2 * DL)
    y_f, y_b, w_d = _rwkv_scan(r_s, v_s, kk, lw_f, lw_b, kd_f, kd_b, bd_f, bd_b, w_ffn_down[0], S)
    o_rwkv = _rwkv_post(y_f, y_b, bonus, g, pq, ones_bd, S)
    tm = _pick(S, (1024, 512))
    tn_d = _pick(D, (1024, 512))
    mixed = _mm_mix_s(o_att, w_proj_attn[0], o_rwkv, w_proj_rwkv[0], z, b_gate[0], S, D, 512, tn_d)
    h1, h1g, h1ss = _mm_out_s(mixed, w_out[0], x2, norm_ffn_w[0], 512, tn_d)
    act = _mm_glu_s(h1g, h1ss, w_ffn_gate[0], w_ffn_up[0], tm, 512)
    h2 = _mm_res(act, w_d, h1, tm, 512, F // 2)
    y = _norm(h2, norm_final_w, x.dtype)
    return y[None]
```

```python
import functools
import math

import jax
import jax.numpy as jnp
from jax import lax
from jax.experimental import pallas as pl
from jax.experimental.pallas import tpu as pltpu

F32 = jnp.float32
BF16 = jnp.bfloat16

N_META = 16
HEAD_DIM = 128
WINDOW = 128
BLOCK = 128
ROPE_DIMS = HEAD_DIM // 4
ROPE_THETA = 500000.0
RWKV_HEAD = 64
RMS_EPS = 1e-6
GN_EPS = 64e-5
LANES = 128
PAD_ROWS = BLOCK
CHUNK = 64
LORA_PAD = 512
VMEM_PHYSICAL = 64 * 1024 * 1024

NN = (((1,), (0,)), ((), ()))
NT = (((1,), (1,)), ((), ()))
TN = (((0,), (0,)), ((), ()))


def _vmem_limit(*block_bytes, temps=0):
    need = 2 * sum(block_bytes) + temps + (4 << 20)
    return int(min(max(need, 16 << 20), VMEM_PHYSICAL - (4 << 20)))


def _params(sem, limit):
    return pltpu.CompilerParams(dimension_semantics=sem, vmem_limit_bytes=limit)


def _pick(n, cands):
    for c in cands:
        if n % c == 0:
            return c
    raise ValueError(f"no tile in {cands} divides {n}")


def _rms(x, w):
    return x * lax.rsqrt(jnp.mean(x * x, axis=-1, keepdims=True) + RMS_EPS) * w


def _norm_ext_kernel(x_ref, meta_ref, w_ref, o_ref, *, n_steps, real_last):
    i = pl.program_id(0)
    tr = o_ref.shape[0]

    @pl.when(i < n_steps - 1)
    def _():
        o_ref[...] = _rms(x_ref[...], w_ref[...]).astype(o_ref.dtype)

    @pl.when(i == n_steps - 1)
    def _():
        if real_last:
            o_ref[0:real_last, :] = _rms(x_ref[0:real_last, :], w_ref[...]).astype(o_ref.dtype)
        o_ref[real_last:tr - N_META, :] = jnp.zeros((tr - N_META - real_last, o_ref.shape[1]), o_ref.dtype)
        o_ref[tr - N_META:, :] = _rms(meta_ref[...], w_ref[...]).astype(o_ref.dtype)


def _norm_ext(x, meta, w):
    S, D = x.shape
    Lp = S + PAD_ROWS
    tr = _pick(Lp, (640, BLOCK))
    n_steps = Lp // tr
    real_last = S - (n_steps - 1) * tr
    last_x_blk = (S - 1) // tr
    return pl.pallas_call(
        functools.partial(_norm_ext_kernel, n_steps=n_steps, real_last=real_last),
        grid=(n_steps,),
        in_specs=[pl.BlockSpec((tr, D), lambda i: (jnp.minimum(i, last_x_blk), 0)),
                  pl.BlockSpec((N_META, D), lambda i: (0, 0)),
                  pl.BlockSpec((1, D), lambda i: (0, 0))],
        out_specs=pl.BlockSpec((tr, D), lambda i: (i, 0)),
        out_shape=jax.ShapeDtypeStruct((Lp, D), BF16),
        compiler_params=_params(("arbitrary",), _vmem_limit(tr * D * 4, tr * D * 2, temps=2 * tr * D * 4)),
        name="norm_ext",
    )(x, meta, w.reshape(1, D))


def _norm_kernel(x_ref, w_ref, o_ref):
    o_ref[...] = _rms(x_ref[...], w_ref[...]).astype(o_ref.dtype)


def _norm(x, w, out_dtype, tr=512):
    S, D = x.shape
    return pl.pallas_call(
        _norm_kernel,
        grid=(S // tr,),
        in_specs=[pl.BlockSpec((tr, D), lambda i: (i, 0)), pl.BlockSpec((1, D), lambda i: (0, 0))],
        out_specs=pl.BlockSpec((tr, D), lambda i: (i, 0)),
        out_shape=jax.ShapeDtypeStruct((S, D), out_dtype),
        compiler_params=_params(("arbitrary",), _vmem_limit(tr * D * 4, tr * D * 4, temps=2 * tr * D * 4)),
        name="norm",
    )(x, w.reshape(1, D))


def _dot(a, b, dims=NN):
    return lax.dot_general(a, b, dims, preferred_element_type=F32)


def _cast_resident(src_ref, dst_ref):
    K = src_ref.shape[0]
    rows = _pick(K, (256, 128))

    def body(c, carry):
        sl = pl.ds(pl.multiple_of(c * rows, rows), rows)
        dst_ref[sl, :] = src_ref[sl, :].astype(dst_ref.dtype)
        return carry

    lax.fori_loop(0, K // rows, body, 0)


def _weight_copy(w_hbm, land_ref, sem, jj, tn, width):
    return pltpu.make_async_copy(w_hbm.at[:, pl.ds(pl.multiple_of(jj * tn, LANES), width)],
                                 land_ref.at[:, pl.ds(0, width)], sem)


def _weight_copy_op(op, w_hbm, land_ref, sem, jj, tn, nj, n_cols):
    rem = n_cols - (nj - 1) * tn
    if rem == tn:
        getattr(_weight_copy(w_hbm, land_ref, sem, jj, tn, tn), op)()
        return

    @pl.when(jj < nj - 1)
    def _():
        getattr(_weight_copy(w_hbm, land_ref, sem, jj, tn, tn), op)()

    @pl.when(jj == nj - 1)
    def _():
        getattr(_weight_copy(w_hbm, land_ref, sem, jj, tn, rem), op)()


def _stream_weights(weights, j, i, nj, tn, n_cols):
    @pl.when(i == 0)
    def _():
        @pl.when(j == 0)
        def _():
            for w_hbm, land_ref, _, sem in weights:
                _weight_copy_op("start", w_hbm, land_ref, sem, j, tn, nj, n_cols)

        for w_hbm, land_ref, bw_ref, sem in weights:
            _weight_copy_op("wait", w_hbm, land_ref, sem, j, tn, nj, n_cols)
            _cast_resident(land_ref, bw_ref)

        @pl.when(j + 1 < nj)
        def _():
            for w_hbm, land_ref, _, sem in weights:
                _weight_copy_op("start", w_hbm, land_ref, sem, j + 1, tn, nj, n_cols)


def _weight_scratch(ks, tn):
    return ([pltpu.VMEM((k, tn), F32) for k in ks] + [pltpu.VMEM((k, tn), BF16) for k in ks]
            + [pltpu.SemaphoreType.DMA(()) for _ in ks])


def _mm_in_kernel(a_ref, w_hbm, o_ref, land_ref, bw_ref, sem, *, nj, tn, n_cols):
    j, i = pl.program_id(0), pl.program_id(1)
    _stream_weights([(w_hbm, land_ref, bw_ref, sem)], j, i, nj, tn, n_cols)
    _per_tile_width(j, nj, tn, n_cols, lambda wd: _store_cols(o_ref, wd, _dot(a_ref[...], bw_ref[:, 0:wd])))


def _per_tile_width(j, nj, tn, n_cols, body):
    rem = n_cols - (nj - 1) * tn
    if rem == tn:
        body(tn)
        return
    pl.when(j < nj - 1)(lambda: body(tn))
    pl.when(j == nj - 1)(lambda: body(rem))


def _store_cols(o_ref, width, value):
    o_ref[:, 0:width] = value.astype(o_ref.dtype)
    if width < o_ref.shape[1]:
        o_ref[:, width:] = jnp.zeros((o_ref.shape[0], o_ref.shape[1] - width), o_ref.dtype)


def _mm_in(a, w, tm, tn):
    M, K = a.shape
    N = w.shape[1]
    nj = pl.cdiv(N, tn)
    return pl.pallas_call(
        functools.partial(_mm_in_kernel, nj=nj, tn=tn, n_cols=N),
        grid=(nj, M // tm),
        in_specs=[pl.BlockSpec((tm, K), lambda j, i: (i, 0)), pl.BlockSpec(memory_space=pl.ANY)],
        out_specs=pl.BlockSpec((tm, tn), lambda j, i: (i, j)),
        out_shape=jax.ShapeDtypeStruct((M, nj * tn), F32),
        scratch_shapes=_weight_scratch((K,), tn),
        compiler_params=_params(("arbitrary", "arbitrary"),
                                _vmem_limit(tm * K * 2, tm * tn * 4, temps=K * tn * 6 + 2 * tm * tn * 4)),
        name="mm_in",
    )(a, w)


def _mm_mix_kernel(a1_ref, a2_ref, z0_ref, z1_ref, bg_ref, w1_hbm, w2_hbm, o_ref,
                     l1_ref, l2_ref, b1_ref, b2_ref, s1, s2, *, nj, tn, n_cols):
    j, i = pl.program_id(0), pl.program_id(1)
    _stream_weights([(w1_hbm, l1_ref, b1_ref, s1), (w2_hbm, l2_ref, b2_ref, s2)], j, i, nj, tn, n_cols)
    g0 = jax.nn.sigmoid(z0_ref[...] + bg_ref[0:1, :])
    g1 = jax.nn.sigmoid(z1_ref[...] + bg_ref[1:2, :])
    o_ref[...] = (g0 * _dot(a1_ref[...], b1_ref[...]) + g1 * _dot(a2_ref[...], b2_ref[...])).astype(o_ref.dtype)


def _mm_mix(o_att, w_pa, o_rwkv, w_pr, z, b_gate, S, D, tm, tn):
    K1, K2 = o_att.shape[1], o_rwkv.shape[1]
    assert D % tn == 0
    nj = D // tn
    return pl.pallas_call(
        functools.partial(_mm_mix_kernel, nj=nj, tn=tn, n_cols=D),
        grid=(nj, S // tm),
        in_specs=[pl.BlockSpec((tm, K1), lambda j, i: (i, 0)), pl.BlockSpec((tm, K2), lambda j, i: (i, 0)),
                  pl.BlockSpec((tm, tn), lambda j, i: (i, j)), pl.BlockSpec((tm, tn), lambda j, i: (i, nj + j)),
                  pl.BlockSpec((2, tn), lambda j, i: (0, j)),
                  pl.BlockSpec(memory_space=pl.ANY), pl.BlockSpec(memory_space=pl.ANY)],
        out_specs=pl.BlockSpec((tm, tn), lambda j, i: (i, j)),
        out_shape=jax.ShapeDtypeStruct((S, D), BF16),
        scratch_shapes=_weight_scratch((K1, K2), tn),
        compiler_params=_params(("arbitrary", "arbitrary"),
                                _vmem_limit(tm * (K1 + K2) * 2, 2 * tm * tn * 4, tm * tn * 2,
                                            temps=(K1 + K2) * tn * 6 + 3 * tm * tn * 4)),
        name="mm_mix",
    )(o_att, o_rwkv, z, z, b_gate, w_pa, w_pr)


def _mm_out_kernel(a_ref, r_ref, g_ref, w_hbm, o_ref, hg_ref, ss_ref, land_ref, bw_ref, sem, *, nj, tn, n_cols):
    j, i = pl.program_id(0), pl.program_id(1)
    _stream_weights([(w_hbm, land_ref, bw_ref, sem)], j, i, nj, tn, n_cols)
    h = r_ref[...] + _dot(a_ref[...], bw_ref[...])
    o_ref[...] = h
    hg_ref[...] = (h * g_ref[...]).astype(hg_ref.dtype)
    ss_ref[...] = jnp.broadcast_to(jnp.sum(h * h, axis=1, keepdims=True), ss_ref.shape)


def _mm_out(a, w, res, gain, tm, tn):
    M, K = a.shape
    N = w.shape[1]
    assert N % tn == 0
    nj = N // tn
    tile = pl.BlockSpec((tm, tn), lambda j, i: (i, j))
    return pl.pallas_call(
        functools.partial(_mm_out_kernel, nj=nj, tn=tn, n_cols=N),
        grid=(nj, M // tm),
        in_specs=[pl.BlockSpec((tm, K), lambda j, i: (i, 0)), tile, pl.BlockSpec((1, tn), lambda j, i: (0, j)),
                  pl.BlockSpec(memory_space=pl.ANY)],
        out_specs=[tile, tile, pl.BlockSpec((tm, LANES), lambda j, i: (i, j))],
        out_shape=[jax.ShapeDtypeStruct((M, N), F32), jax.ShapeDtypeStruct((M, N), BF16),
                   jax.ShapeDtypeStruct((M, nj * LANES), F32)],
        scratch_shapes=_weight_scratch((K,), tn),
        compiler_params=_params(("arbitrary", "arbitrary"),
                                _vmem_limit(tm * K * 2, 2 * tm * tn * 4, tm * tn * 2, tm * LANES * 4,
                                            temps=K * tn * 6 + 2 * tm * tn * 4)),
        name="mm_out",
    )(a, res, gain.reshape(1, N), w)


def _mm_glu_kernel(a_ref, ss_ref, wg_hbm, wu_hbm, o_ref, lg_ref, lu_ref, bg_ref, bu_ref, sg, su,
                     *, nj, tn, n_cols, n_model):
    j, i = pl.program_id(0), pl.program_id(1)
    _stream_weights([(wg_hbm, lg_ref, bg_ref, sg), (wu_hbm, lu_ref, bu_ref, su)], j, i, nj, tn, n_cols)
    ss = ss_ref[:, 0:LANES]
    for t in range(1, ss_ref.shape[1] // LANES):
        ss = ss + ss_ref[:, t * LANES:(t + 1) * LANES]
    scale = lax.rsqrt(ss[:, 0:1] * (1.0 / n_model) + RMS_EPS)

    def body(wd):
        a = a_ref[...]
        gate = scale * _dot(a, bg_ref[:, 0:wd])
        up = scale * _dot(a, bu_ref[:, 0:wd])
        _store_cols(o_ref, wd, gate * jax.nn.sigmoid(gate) * up)

    _per_tile_width(j, nj, tn, n_cols, body)


def _mm_glu(a, ss, wg, wu, tm, tn):
    M, K = a.shape
    N = wg.shape[1]
    nj = pl.cdiv(N, tn)
    return pl.pallas_call(
        functools.partial(_mm_glu_kernel, nj=nj, tn=tn, n_cols=N, n_model=K),
        grid=(nj, M // tm),
        in_specs=[pl.BlockSpec((tm, K), lambda j, i: (i, 0)), pl.BlockSpec((tm, ss.shape[1]), lambda j, i: (i, 0)),
                  pl.BlockSpec(memory_space=pl.ANY), pl.BlockSpec(memory_space=pl.ANY)],
        out_specs=pl.BlockSpec((tm, tn), lambda j, i: (i, j)),
        out_shape=jax.ShapeDtypeStruct((M, N), BF16),
        scratch_shapes=_weight_scratch((K, K), tn),
        compiler_params=_params(("arbitrary", "arbitrary"),
                                _vmem_limit(tm * K * 2, tm * ss.shape[1] * 4, tm * tn * 2,
                                            temps=2 * K * tn * 6 + 6 * tm * tn * 4)),
        name="mm_glu",
    )(a, ss, wg, wu)


def _mm_res_kernel(a_ref, b_ref, r_ref, o_ref, *, nk):
    k = pl.program_id(2)
    if nk == 1:
        o_ref[...] = r_ref[...] + _dot(a_ref[...], b_ref[...])
    else:
        @pl.when(k == 0)
        def _():
            o_ref[...] = r_ref[...] + _dot(a_ref[...], b_ref[...])

        @pl.when(k > 0)
        def _():
            o_ref[...] = o_ref[...] + _dot(a_ref[...], b_ref[...])


def _mm_res(a, b, res, tm, tn, tk):
    M, K = a.shape
    N = b.shape[1]
    nk = K // tk
    return pl.pallas_call(
        functools.partial(_mm_res_kernel, nk=nk),
        grid=(M // tm, N // tn, nk),
        in_specs=[pl.BlockSpec((tm, tk), lambda i, j, k: (i, k)), pl.BlockSpec((tk, tn), lambda i, j, k: (k, j)),
                  pl.BlockSpec((tm, tn), lambda i, j, k: (i, j))],
        out_specs=pl.BlockSpec((tm, tn), lambda i, j, k: (i, j)),
        out_shape=jax.ShapeDtypeStruct((M, N), F32),
        compiler_params=_params(("arbitrary", "arbitrary", "arbitrary"),
                                _vmem_limit(tm * tk * 2, tk * tn * 2, 2 * tm * tn * 4, temps=tm * tn * 4)),
        name="mm_down",
    )(a, b, res)


NEG_BIG = -1e30


def _rope(t, cos, sin):
    half = ROPE_DIMS // 2
    lane = lax.broadcasted_iota(jnp.int32, t.shape, 1)
    up = jnp.concatenate([t[:, half:], t[:, :half]], axis=1)
    down = jnp.concatenate([t[:, HEAD_DIM - half:], t[:, :HEAD_DIM - half]], axis=1)
    return t * cos + jnp.where(lane < half, up, down) * sin


def _attn_kernel(sink_ref, q_ref, k_ref, v_ref, cos_ref, sin_ref, o_ref, kb_ref, vb_ref,
                     *, nb, group, qblocks):
    h, i = pl.program_id(0), pl.program_id(1)
    lp = k_ref.shape[0]

    @pl.when(i == 0)
    def _():
        def body(c, carry):
            sl = pl.ds(pl.multiple_of(c * BLOCK, BLOCK), BLOCK)
            kb_ref[sl, :] = _rope(k_ref[sl, :], cos_ref[sl, :], sin_ref[sl, :]).astype(BF16)
            vb_ref[sl, :] = v_ref[sl, :].astype(BF16)
            return carry
        lax.fori_loop(0, lp // BLOCK, body, 0)

    zpad = jnp.zeros((BLOCK - N_META, HEAD_DIM), BF16)
    k_meta = jnp.concatenate([kb_ref[lp - N_META:lp, :], zpad], axis=0)
    v_meta = jnp.concatenate([vb_ref[lp - N_META:lp, :], zpad], axis=0)
    rows = lax.broadcasted_iota(jnp.int32, (BLOCK, 4 * BLOCK), 0)
    cols = lax.broadcasted_iota(jnp.int32, (BLOCK, 4 * BLOCK), 1)
    is_meta = (cols >= 3 * BLOCK) & (cols < 3 * BLOCK + N_META)
    scale = HEAD_DIM ** -0.5
    k_all, v_all, bias, q, sink = [], [], [], [], []
    for b in range(qblocks):
        qi = i * qblocks + b
        kb0 = jnp.clip(qi - 1, 0, nb - 3)
        win = pl.ds(pl.multiple_of(kb0 * BLOCK, BLOCK), 3 * BLOCK)
        k_b = jnp.concatenate([kb_ref[win, :], k_meta], axis=0)
        v_b = jnp.concatenate([vb_ref[win, :], v_meta], axis=0)
        rel = cols - rows + (kb0 - qi) * BLOCK
        ok = ((cols < 3 * BLOCK) & (jnp.abs(rel) <= WINDOW)) | is_meta
        bias_b = jnp.where(ok, 0.0, NEG_BIG).astype(F32)
        qrows = pl.ds(pl.multiple_of(qi * BLOCK, BLOCK), BLOCK)
        cos_q, sin_q = cos_ref[qrows, :], sin_ref[qrows, :]
        for g in range(group):
            k_all.append(k_b)
            v_all.append(v_b)
            bias.append(bias_b)
            q.append(_rope(q_ref[b * BLOCK:(b + 1) * BLOCK, g * HEAD_DIM:(g + 1) * HEAD_DIM],
                           cos_q, sin_q).astype(BF16))
            sink.append(sink_ref[h * group + g])
    n = range(len(q))
    s = [_dot(q[c], k_all[c], NT) * scale + bias[c] for c in n]
    m = [jnp.maximum(jnp.max(s[c], axis=-1, keepdims=True), sink[c]) for c in n]
    p = [jnp.exp(s[c] - m[c]) for c in n]
    denom = [jnp.sum(p[c], axis=-1, keepdims=True) + jnp.exp(sink[c] - m[c]) for c in n]
    o = [_dot(p[c].astype(BF16), v_all[c]) / denom[c] for c in n]
    for c in n:
        b, g = divmod(c, group)
        o_ref[b * BLOCK:(b + 1) * BLOCK, g * HEAD_DIM:(g + 1) * HEAD_DIM] = o[c].astype(o_ref.dtype)


def _attention(z, cos_t, sin_t, sink, S, n_heads, n_kv, off_q):
    Lp = z.shape[0]
    group = n_heads // n_kv
    nb = S // BLOCK
    assert nb >= 3
    qw = group * HEAD_DIM
    kcol = off_q // HEAD_DIM + n_heads
    vcol = kcol + n_kv
    col_bytes = Lp * HEAD_DIM * 4
    qblocks = _pick(nb, (4, 2, 1))
    tq = qblocks * BLOCK
    return pl.pallas_call(
        functools.partial(_attn_kernel, nb=nb, group=group, qblocks=qblocks),
        grid_spec=pltpu.PrefetchScalarGridSpec(
            num_scalar_prefetch=1,
            grid=(n_kv, nb // qblocks),
            in_specs=[pl.BlockSpec((tq, qw), lambda h, i, s: (i, off_q // qw + h)),
                      pl.BlockSpec((Lp, HEAD_DIM), lambda h, i, s: (0, kcol + h)),
                      pl.BlockSpec((Lp, HEAD_DIM), lambda h, i, s: (0, vcol + h)),
                      pl.BlockSpec((Lp, HEAD_DIM), lambda h, i, s: (0, 0)),
                      pl.BlockSpec((Lp, HEAD_DIM), lambda h, i, s: (0, 0))],
            out_specs=pl.BlockSpec((tq, qw), lambda h, i, s: (i, h)),
            scratch_shapes=[pltpu.VMEM((Lp, HEAD_DIM), BF16), pltpu.VMEM((Lp, HEAD_DIM), BF16)]),
        out_shape=jax.ShapeDtypeStruct((S, n_heads * HEAD_DIM), BF16),
        compiler_params=_params(("arbitrary", "arbitrary"),
                                _vmem_limit(tq * qw * 4, 4 * col_bytes, tq * qw * 2,
                                            temps=col_bytes + 8 * group * tq * 512 * 4)),
        name="attention",
    )(sink.astype(F32), z, z, z, cos_t, sin_t)


def _split3(x):
    hi = x.astype(BF16)
    r1 = x - hi.astype(F32)
    mid = r1.astype(BF16)
    lo = (r1 - mid.astype(F32)).astype(BF16)
    return hi, mid, lo


def _head_sum(x, ones_bd):
    outs = []
    for s in range(x.shape[1] // LANES):
        parts = _split3(x[:, s * LANES:(s + 1) * LANES])
        outs.append(sum(_dot(p, ones_bd) for p in parts))
    return outs[0] if len(outs) == 1 else jnp.concatenate(outs, axis=1)


def _prep_kernel(r_ref, rp_ref, rn_ref, k_ref, kp_ref, kn_ref, v_ref, vp_ref, vn_ref,
                 lo_ref, lop_ref, lon_ref, gd_ref, gdp_ref, gdn_ref,
                 pc_ref, pl_ref, wl_ref, g2_ref, ones_ref,
                 ro_ref, vo_ref, kko_ref, lwf_ref, lwb_ref, kdf_ref, kdb_ref, bdf_ref, bdb_ref, go_ref, bo_ref,
                 *, n_tiles, tr, lora_tanh_cols):
    i = pl.program_id(1)

    def shifted(main_ref, prev_ref, next_ref, mup, mun):
        x = main_ref[...]
        row8 = lax.broadcasted_iota(jnp.int32, (8, x.shape[1]), 0)
        down = pltpu.roll(x, 1, 0)
        up = pltpu.roll(x, tr - 1, 0)
        prev = jnp.concatenate([jnp.where(row8 == 0, prev_ref[7:8, :], down[0:8]), down[8:]], axis=0)
        nxt = jnp.concatenate([up[:tr - 8], jnp.where(row8 == 7, next_ref[0:1, :], up[tr - 8:])], axis=0)
        return x + mup * (prev - x) + mun * (nxt - x)

    r = shifted(r_ref, rp_ref, rn_ref, pc_ref[0:1, :], pc_ref[1:2, :])
    k = shifted(k_ref, kp_ref, kn_ref, pc_ref[2:3, :], pc_ref[3:4, :])
    v = shifted(v_ref, vp_ref, vn_ref, pc_ref[4:5, :], pc_ref[5:6, :])
    lo = shifted(lo_ref, lop_ref, lon_ref, pl_ref[0:1, 0:LORA_PAD], pl_ref[1:2, 0:LORA_PAD])
    gd = shifted(gd_ref, gdp_ref, gdn_ref, pl_ref[0:1, LORA_PAD:], pl_ref[1:2, LORA_PAD:])

    row = lax.broadcasted_iota(jnp.int32, (tr, 1), 0)
    valid = jnp.where((i < n_tiles - 1) | (row >= tr - N_META), 1.0, 0.0).astype(F32)

    lane = lax.broadcasted_iota(jnp.int32, lo.shape, 1)
    lo_act = jnp.where(lane < lora_tanh_cols, jnp.tanh(lo), lo).astype(BF16)
    dec_f = _dot(lo_act, wl_ref[0])
    dec_b = _dot(lo_act, wl_ref[1])
    apre_f = _dot(lo_act, wl_ref[2])
    apre_b = _dot(lo_act, wl_ref[3])
    g = _dot(jax.nn.sigmoid(lo).astype(BF16), g2_ref[0]) + _dot(jax.nn.sigmoid(gd).astype(BF16), g2_ref[1])

    def log_decay(dec, w0):
        return -math.exp(-0.5) * jax.nn.sigmoid(w0 + dec)

    a_f = jax.nn.sigmoid(pc_ref[8:9, :] + apre_f)
    a_b = jax.nn.sigmoid(pc_ref[9:10, :] + apre_b)
    kk = k * pc_ref[10:11, :]
    ss = _head_sum(kk * kk, ones_ref[...])
    kk = kk / jnp.maximum(jnp.sqrt(ss), 1e-12) * valid
    k_a = pc_ref[11:12, :]
    kv = k * valid

    rv = r * valid
    vv = v * valid
    kd_f = kv * (1.0 + (a_f - 1.0) * k_a)
    kd_b = kv * (1.0 + (a_b - 1.0) * k_a)
    ro_ref[...] = rv
    vo_ref[...] = vv
    kko_ref[...] = kk
    lwf_ref[...] = log_decay(dec_f, pc_ref[6:7, :])
    lwb_ref[...] = log_decay(dec_b, pc_ref[7:8, :])
    kdf_ref[...] = kd_f
    kdb_ref[...] = kd_b
    bdf_ref[...] = kk * a_f
    bdb_ref[...] = kk * a_b
    go_ref[...] = g
    bo_ref[...] = _head_sum(rv * (kd_f + kd_b) * pc_ref[12:13, :], ones_ref[...]) * vv


def _rwkv_prep(z, off_r, C, pc, plo, wl, g2, ones_bd, lora_tanh_cols):
    Lp = z.shape[0]
    tr = BLOCK
    ct = _pick(C, (1024, 512))
    n_tiles = Lp // tr
    n8 = Lp // 8
    off_lo = off_r + 3 * C
    off_gd = off_lo + LORA_PAD
    assert off_r % ct == 0 and off_lo % LORA_PAD == 0

    def seg_specs(off, w, with_c):
        cb = off // w

        def col(c):
            return cb + c if with_c else cb
        return [pl.BlockSpec((tr, w), lambda c, i: (i, col(c))),
                pl.BlockSpec((8, w), lambda c, i: ((i * (tr // 8) + n8 - 1) % n8, col(c))),
                pl.BlockSpec((8, w), lambda c, i: (((i + 1) * (tr // 8)) % n8, col(c)))]

    in_specs = (seg_specs(off_r, ct, True) + seg_specs(off_r + C, ct, True) + seg_specs(off_r + 2 * C, ct, True)
                + seg_specs(off_lo, LORA_PAD, False) + seg_specs(off_gd, LORA_PAD, False)
                + [pl.BlockSpec((16, ct), lambda c, i: (0, c)),
                   pl.BlockSpec((8, 2 * LORA_PAD), lambda c, i: (0, 0)),
                   pl.BlockSpec((4, LORA_PAD, ct), lambda c, i: (0, 0, c)),
                   pl.BlockSpec((2, LORA_PAD, ct), lambda c, i: (0, 0, c)),
                   pl.BlockSpec((LANES, LANES), lambda c, i: (0, 0))])
    out_spec = pl.BlockSpec((tr, ct), lambda c, i: (i, c))
    out_sds = jax.ShapeDtypeStruct((Lp, C), F32)
    return pl.pallas_call(
        functools.partial(_prep_kernel, n_tiles=n_tiles, tr=tr, lora_tanh_cols=lora_tanh_cols),
        grid=(C // ct, n_tiles),
        in_specs=in_specs,
        out_specs=[out_spec] * 11,
        out_shape=[out_sds] * 11,
        compiler_params=_params(("arbitrary", "arbitrary"),
                                _vmem_limit(3 * tr * ct * 4, 2 * tr * LORA_PAD * 4, 4 * LORA_PAD * ct * 2,
                                            2 * LORA_PAD * ct * 2, 11 * tr * ct * 4, temps=24 * tr * ct * 4)),
        name="rwkv_prep",
    )(*([z] * 15), pc, plo, wl, g2, ones_bd)


SCAN_PAIRS = 16


def _scan_consts():
    C = CHUNK
    row = lax.broadcasted_iota(jnp.int32, (C, 2 * C), 0)
    col = lax.broadcasted_iota(jnp.int32, (C, 2 * C), 1) % C
    rr = lax.broadcasted_iota(jnp.int32, (LANES, LANES), 0)
    cc = lax.broadcasted_iota(jnp.int32, (LANES, LANES), 1)
    return dict(
        row=lax.broadcasted_iota(jnp.int32, (C, LANES), 0),
        head0=lax.broadcasted_iota(jnp.int32, (C, LANES), 1) < RWKV_HEAD,
        strict_f=col < row, strict_b=col > row, incl_f=col <= row, incl_b=col >= row,
        eye_sbs=jnp.where(col == row, 1.0, 0.0).astype(F32),
        same_head=(rr // RWKV_HEAD) == (cc // RWKV_HEAD), eye=rr == cc)


def _scan_chains(chains, cst):
    C = CHUNK
    head0, same, eye = cst["head0"], cst["same_head"], cst["eye"]
    n = range(len(chains))
    rev = [c[7] for c in chains]
    r, v, al, lw, k, be, h = ([c[i] for c in chains] for i in range(7))
    strict = [cst["strict_b" if x else "strict_f"] for x in rev]
    incl = [cst["incl_b" if x else "incl_f"] for x in rev]

    def bd(x):
        xb = x.astype(BF16)
        zero = jnp.zeros_like(xb)
        return jnp.concatenate([jnp.where(head0, xb, zero), jnp.where(head0, zero, xb)], axis=0)

    def mm(a, b_bf16, dims=NN):
        return _dot(a.astype(BF16), b_bf16, dims)

    row = cst["row"]
    cl = list(lw)
    sh = 1
    while sh < C:
        cl = [cl[i] + (jnp.where(row < C - sh, pltpu.roll(cl[i], C - sh, 0), 0.0) if rev[i] else
                       jnp.where(row >= sh, pltpu.roll(cl[i], sh, 0), 0.0)) for i in n]
        sh *= 2
    total = [cl[i][0:1, :] if rev[i] else cl[i][C - 1:C, :] for i in n]
    a_t = [al[i] * jnp.exp(cl[i] - lw[i]) for i in n]
    r_t = [r[i] * jnp.exp(cl[i]) for i in n]
    w_inv = [jnp.exp(-cl[i]) for i in n]
    w_rest = [jnp.exp(total[i] - cl[i]) for i in n]
    bd_v = [bd(v[i]) for i in n]
    sc = [mm(jnp.concatenate([a_t[i], r_t[i]], axis=0),
             jnp.concatenate([bd(k[i] * w_inv[i]), bd(be[i] * w_inv[i])], axis=0), NT) for i in n]
    a_ak = [jnp.where(strict[i], sc[i][0:C, 0:2 * C], 0.0) for i in n]
    n_ab = [jnp.where(strict[i], sc[i][0:C, 2 * C:], 0.0) for i in n]
    a_rk = [jnp.where(incl[i], sc[i][C:, 0:2 * C], 0.0) for i in n]
    a_rb = [jnp.where(incl[i], sc[i][C:, 2 * C:], 0.0) for i in n]

    t_inv = [cst["eye_sbs"] - n_ab[i] for i in n]
    sq = [mm(n_ab[i], bd(n_ab[i])) for i in n]
    n_round = 2
    while (1 << n_round) < C:
        both = [mm(jnp.concatenate([t_inv[i], sq[i]], axis=0), bd(sq[i])) for i in n]
        t_inv = [t_inv[i] + both[i][0:C] for i in n]
        sq = [both[i][C:] for i in n]
        n_round += 1
    t_inv = [t_inv[i] + mm(t_inv[i], bd(sq[i])) for i in n]

    akv = [mm(jnp.concatenate([a_ak[i], a_rk[i]], axis=0), bd_v[i]) for i in n]
    tp = [mm(t_inv[i], jnp.concatenate([bd(akv[i][0:C]), bd(a_t[i])], axis=1)) for i in n]
    p0 = [tp[i][:, 0:LANES] for i in n]
    at = [tp[i][:, LANES:] for i in n]
    rb = [mm(a_rb[i], jnp.concatenate([bd(p0[i]), bd(at[i])], axis=1)) for i in n]
    y0 = [akv[i][C:] - rb[i][:, 0:LANES] for i in n]
    rh = [r_t[i] - rb[i][:, LANES:] for i in n]
    bt = [mm(be[i] * w_rest[i], jnp.concatenate([at[i], p0[i]], axis=1).astype(BF16), TN) for i in n]
    kv = [mm(k[i] * w_rest[i], v[i].astype(BF16), TN) for i in n]
    m_mat = [jnp.where(same, jnp.where(eye, jnp.broadcast_to(jnp.exp(total[i]), (LANES, LANES)), 0.0)
                       - bt[i][:, 0:LANES], 0.0) for i in n]
    g_mat = [jnp.where(same, kv[i] - bt[i][:, LANES:], 0.0) for i in n]
    yh = [mm(jnp.concatenate([rh[i], m_mat[i]], axis=0), h[i].astype(BF16)) for i in n]
    return [(yh[i][0:C] + y0[i], yh[i][C:] + g_mat[i]) for i in n]


def _scan_kernel(rf_ref, vf_ref, af_ref, lwf_ref, kf_ref, bf_ref,
                 rb_ref, vb_ref, ab_ref, lwb_ref, kb_ref, bb_ref, wd_ref,
                 yf_ref, yb_ref, wdo_ref, hf_ref, hb_ref, *, pairs):
    @pl.when(pl.program_id(1) == 0)
    def _():
        hf_ref[...] = jnp.zeros(hf_ref.shape, F32)
        hb_ref[...] = jnp.zeros(hb_ref.shape, F32)

    wdo_ref[...] = wd_ref[...].astype(wdo_ref.dtype)

    cst = _scan_consts()
    chains = []
    for g in range(pairs):
        sl = slice(g * LANES, (g + 1) * LANES)
        chains.append((rf_ref[:, sl], vf_ref[:, sl], af_ref[:, sl], lwf_ref[:, sl], kf_ref[:, sl], bf_ref[:, sl],
                       hf_ref[g], False))
        chains.append((rb_ref[:, sl], vb_ref[:, sl], ab_ref[:, sl], lwb_ref[:, sl], kb_ref[:, sl], bb_ref[:, sl],
                       hb_ref[g], True))
    res = _scan_chains(chains, cst)
    for g in range(pairs):
        sl = slice(g * LANES, (g + 1) * LANES)
        yf_ref[:, sl], hf_ref[g] = res[2 * g]
        yb_ref[:, sl], hb_ref[g] = res[2 * g + 1]


def _rwkv_scan(r, v, kk, lw_f, lw_b, kd_f, kd_b, bd_f, bd_b, w_cast, S):
    Lp, C = r.shape
    n_real = S // CHUNK
    n_chunks = Lp // CHUNK
    steps = n_real + 1
    pairs = _pick(C // LANES, (SCAN_PAIRS, 2, 1))
    w = pairs * LANES
    total_steps = (C // w) * steps
    wr, wc = w_cast.shape
    cast_rows = next(rb for rb in (128, 256, 512, 1024, wr) if wr % rb == 0 and wr // rb <= total_steps)
    n_cast = wr // cast_rows
    cast_spec = pl.BlockSpec((cast_rows, wc), lambda p, s: (jnp.minimum(p * steps + s, n_cast - 1), 0))

    def fwd(p, s):
        return ((s + n_real + 1) % n_chunks, p)

    def bwd(p, s):
        return (n_real - s, p)

    spec_f = pl.BlockSpec((CHUNK, w), fwd)
    spec_b = pl.BlockSpec((CHUNK, w), bwd)
    out_sds = jax.ShapeDtypeStruct((S + CHUNK, C), F32)
    out_f = pl.BlockSpec((CHUNK, w), lambda p, s: ((s + n_real) % (n_real + 1), p))
    return pl.pallas_call(
        functools.partial(_scan_kernel, pairs=pairs),
        grid=(C // w, steps),
        in_specs=[spec_f] * 6 + [spec_b] * 6 + [cast_spec],
        out_specs=[out_f, spec_b, cast_spec],
        out_shape=[out_sds, out_sds, jax.ShapeDtypeStruct((wr, wc), BF16)],
        scratch_shapes=[pltpu.VMEM((pairs, LANES, LANES), F32), pltpu.VMEM((pairs, LANES, LANES), F32)],
        compiler_params=_params(("arbitrary", "arbitrary"),
                                _vmem_limit(14 * CHUNK * w * 4, cast_rows * wc * 6, temps=16 << 20)),
        name="rwkv_scan",
    )(r, v, kk, lw_f, kd_f, bd_f, r, v, kk, lw_b, kd_b, bd_b, w_cast)


def _post_kernel(yf_ref, yb_ref, bonus_ref, g_ref, pq_ref, ones_ref, o_ref):
    ones_bd = ones_ref[...]
    inv_n = 1.0 / RWKV_HEAD
    y = yf_ref[...] + yb_ref[...]
    mean = _head_sum(y, ones_bd) * inv_n
    yc = y - mean
    var = _head_sum(yc * yc, ones_bd) * inv_n
    yn = yc * lax.rsqrt(var + GN_EPS) * pq_ref[0:1, :] + pq_ref[1:2, :]
    o_ref[...] = ((yn + bonus_ref[...]) * g_ref[...]).astype(o_ref.dtype)


def _rwkv_post(y_f, y_b, bonus, g, pq, ones_bd, S):
    C = g.shape[1]
    tr = 512
    ct = _pick(C, (1024, 512))
    spec = pl.BlockSpec((tr, ct), lambda i, c: (i, c))
    return pl.pallas_call(
        _post_kernel,
        grid=(S // tr, C // ct),
        in_specs=[spec] * 4 + [pl.BlockSpec((8, ct), lambda i, c: (0, c)),
                               pl.BlockSpec((LANES, LANES), lambda i, c: (0, 0))],
        out_specs=spec,
        out_shape=jax.ShapeDtypeStruct((S, C), BF16),
        compiler_params=_params(("arbitrary", "arbitrary"),
                                _vmem_limit(4 * tr * ct * 4, tr * ct * 2, temps=16 * tr * ct * 4)),
        name="rwkv_post",
    )(y_f, y_b, bonus, g, pq, ones_bd)


IN_TN = 1024


def _matmul_tiles(S, D, F, Lp):
    return dict(
        in_tm=_pick(Lp, (1040, 640)),
        proj_tm=512, proj_tn=_pick(D, (1024, 512)),
        ffn_tm=_pick(S, (1024, 512)), glu_tn=512,
        down_tn=512, down_tk=F // 2,
    )

def kernel(x, meta_tokens, norm_mix_w, w_in, b_gate, mu_prev, mu_next, dec_w0, dec_w2, iclr_a0, iclr_a2,
           gate_w2, k_k, k_a, r_k, ln_x_w, ln_x_b, attn_sink, w_proj_attn, w_proj_rwkv, w_out, norm_ffn_w,
           w_ffn_gate, w_ffn_up, w_ffn_down, norm_final_w):
    B, S, D = x.shape
    assert B == 1 and norm_mix_w.shape[0] == 1, "one sequence, one layer"
    C = k_k.shape[-1]
    AW = w_proj_attn.shape[1]
    n_heads = attn_sink.shape[-1]
    DL, IL, GL = dec_w2.shape[2], iclr_a2.shape[2], gate_w2.shape[1]
    shift_w = mu_prev.shape[-1]
    KVW = (w_in.shape[-1] - 2 * D - AW - shift_w) // 2
    n_kv = KVW // HEAD_DIM
    F = w_ffn_gate.shape[-1]
    lora_w = 2 * DL + 2 * IL
    assert shift_w == 3 * C + lora_w + GL and lora_w <= LORA_PAD and GL <= LORA_PAD
    assert S % BLOCK == 0 and C % LANES == 0 and n_heads * HEAD_DIM == AW
    Lp = S + PAD_ROWS

    off_q = 2 * D
    off_r = off_q + AW + 2 * KVW
    off_lo = off_r + 3 * C
    n_in_pad = -(-w_in.shape[-1] // IN_TN) * IN_TN
    assert n_in_pad >= off_lo + 2 * LORA_PAD and off_lo % LORA_PAD == 0
    low_w = lora_w + GL
    gl_a = LORA_PAD - lora_w

    mu_p, mu_n = mu_prev[0], mu_next[0]
    zc = jnp.zeros((C,), F32)
    pc = jnp.stack([mu_p[:C], mu_n[:C], mu_p[C:2 * C], mu_n[C:2 * C], mu_p[2 * C:3 * C], mu_n[2 * C:3 * C],
                    dec_w0[0, 0], dec_w0[0, 1], iclr_a0[0, 0], iclr_a0[0, 1], k_k[0], k_a[0], r_k[0], zc, zc, zc])
    plo = jnp.zeros((8, 2 * LORA_PAD), F32).at[0, :low_w].set(mu_p[3 * C:]).at[1, :low_w].set(mu_n[3 * C:])
    wl = jnp.zeros((4, LORA_PAD, C), F32)
    wl = wl.at[0, 0:DL].set(dec_w2[0, 0]).at[1, DL:2 * DL].set(dec_w2[0, 1])
    wl = wl.at[2, 2 * DL:2 * DL + IL].set(iclr_a2[0, 0]).at[3, 2 * DL + IL:lora_w].set(iclr_a2[0, 1])
    wl = wl.astype(BF16)
    g2 = jnp.zeros((2, LORA_PAD, C), F32).at[0, lora_w:].set(gate_w2[0, :gl_a]).at[1, :GL - gl_a].set(
        gate_w2[0, gl_a:]).astype(BF16)
    pq = jnp.zeros((8, C), F32).at[0].set(ln_x_w[0]).at[1].set(ln_x_b[0])
    lane = jnp.arange(LANES)
    ones_bd = (lane[:, None] // RWKV_HEAD == lane[None, :] // RWKV_HEAD).astype(BF16)


    pos = jnp.concatenate([jnp.arange(N_META, N_META + S), jnp.zeros((PAD_ROWS - N_META,), jnp.int32),
                           jnp.arange(N_META)]).astype(F32)
    inv = ROPE_THETA ** (-jnp.arange(0, ROPE_DIMS, 2, dtype=F32) / ROPE_DIMS)
    ang = pos[:, None] * inv[None, :]
    ones_rest = jnp.ones((Lp, HEAD_DIM - ROPE_DIMS), F32)
    cos_t = jnp.concatenate([jnp.cos(ang), jnp.cos(ang), ones_rest], axis=1)
    sin_t = jnp.concatenate([-jnp.sin(ang), jnp.sin(ang), 0.0 * ones_rest], axis=1)

    t = _matmul_tiles(S, D, F, Lp)
    x2 = x[0]
    h_ext = _norm_ext(x2, meta_tokens.astype(x.dtype), norm_mix_w[0])
    z = _mm_in(h_ext, w_in[0], t["in_tm"], IN_TN)
    o_att = _attention(z, cos_t, sin_t, attn_sink[0], S, n_heads, n_kv, off_q)
    r_s, v_s, kk, lw_f, lw_b, kd_f, kd_b, bd_f, bd_b, g, bonus = _rwkv_prep(
        z, off_r, C, pc, plo, wl, g2, ones_bd, 2 * DL)
    y_f, y_b, w_d = _rwkv_scan(r_s, v_s, kk, lw_f, lw_b, kd_f, kd_b, bd_f, bd_b, w_ffn_down[0], S)
    o_rwkv = _rwkv_post(y_f, y_b, bonus, g, pq, ones_bd, S)
    mixed = _mm_mix(o_att, w_proj_attn[0], o_rwkv, w_proj_rwkv[0], z, b_gate[0], S, D, t["proj_tm"], t["proj_tn"])
    h1, h1g, h1ss = _mm_out(mixed, w_out[0], x2, norm_ffn_w[0], t["proj_tm"], t["proj_tn"])
    act = _mm_glu(h1g, h1ss, w_ffn_gate[0], w_ffn_up[0], t["ffn_tm"], t["glu_tn"])
    h2 = _mm_res(act, w_d, h1, t["ffn_tm"], t["down_tn"], t["down_tk"])
    y = _norm(h2, norm_final_w, x.dtype)
    return y[None]
```

```python
import functools
import math

import jax
import jax.numpy as jnp
from jax import lax
from jax.experimental import pallas as pl
from jax.experimental.pallas import tpu as pltpu

F32 = jnp.float32
BF16 = jnp.bfloat16

N_META = 16
HEAD_DIM = 128
WINDOW = 128
BLOCK = 128
ROPE_DIMS = HEAD_DIM // 4
ROPE_THETA = 500000.0
RWKV_HEAD = 64
RMS_EPS = 1e-6
GN_EPS = 64e-5
LANES = 128
PAD_ROWS = BLOCK
CHUNK = 64
LORA_PAD = 512
VMEM_PHYSICAL = 64 * 1024 * 1024

NN = (((1,), (0,)), ((), ()))
NT = (((1,), (1,)), ((), ()))
TN = (((0,), (0,)), ((), ()))


def _vmem_limit(*block_bytes, temps=0):
    need = 2 * sum(block_bytes) + temps + (4 << 20)
    return int(min(max(need, 16 << 20), VMEM_PHYSICAL - (4 << 20)))


def _params(sem, limit):
    return pltpu.CompilerParams(dimension_semantics=sem, vmem_limit_bytes=limit)


def _pick(n, cands):
    for c in cands:
        if n % c == 0:
            return c
    raise ValueError(f"no tile in {cands} divides {n}")


def _rms(x, w):
    return x * lax.rsqrt(jnp.mean(x * x, axis=-1, keepdims=True) + RMS_EPS) * w


def _norm_ext_kernel(x_ref, meta_ref, w_ref, o_ref, *, n_steps, real_last):
    i = pl.program_id(0)
    tr = o_ref.shape[0]

    @pl.when(i < n_steps - 1)
    def _():
        o_ref[...] = _rms(x_ref[...], w_ref[...]).astype(o_ref.dtype)

    @pl.when(i == n_steps - 1)
    def _():
        if real_last:
            o_ref[0:real_last, :] = _rms(x_ref[0:real_last, :], w_ref[...]).astype(o_ref.dtype)
        o_ref[real_last:tr - N_META, :] = jnp.zeros((tr - N_META - real_last, o_ref.shape[1]), o_ref.dtype)
        o_ref[tr - N_META:, :] = _rms(meta_ref[...], w_ref[...]).astype(o_ref.dtype)


def _norm_ext(x, meta, w):
    S, D = x.shape
    Lp = S + PAD_ROWS
    tr = _pick(Lp, (640, BLOCK))
    n_steps = Lp // tr
    real_last = S - (n_steps - 1) * tr
    last_x_blk = (S - 1) // tr
    return pl.pallas_call(
        functools.partial(_norm_ext_kernel, n_steps=n_steps, real_last=real_last),
        grid=(n_steps,),
        in_specs=[pl.BlockSpec((tr, D), lambda i: (jnp.minimum(i, last_x_blk), 0)),
                  pl.BlockSpec((N_META, D), lambda i: (0, 0)),
                  pl.BlockSpec((1, D), lambda i: (0, 0))],
        out_specs=pl.BlockSpec((tr, D), lambda i: (i, 0)),
        out_shape=jax.ShapeDtypeStruct((Lp, D), BF16),
        compiler_params=_params(("arbitrary",), _vmem_limit(tr * D * 4, tr * D * 2, temps=2 * tr * D * 4)),
        name="norm_ext",
    )(x, meta, w.reshape(1, D))


def _norm_kernel(x_ref, w_ref, o_ref):
    o_ref[...] = _rms(x_ref[...], w_ref[...]).astype(o_ref.dtype)


def _norm(x, w, out_dtype, tr=512):
    S, D = x.shape
    return pl.pallas_call(
        _norm_kernel,
        grid=(S // tr,),
        in_specs=[pl.BlockSpec((tr, D), lambda i: (i, 0)), pl.BlockSpec((1, D), lambda i: (0, 0))],
        out_specs=pl.BlockSpec((tr, D), lambda i: (i, 0)),
        out_shape=jax.ShapeDtypeStruct((S, D), out_dtype),
        compiler_params=_params(("arbitrary",), _vmem_limit(tr * D * 4, tr * D * 4, temps=2 * tr * D * 4)),
        name="norm",
    )(x, w.reshape(1, D))


def _dot(a, b, dims=NN):
    return lax.dot_general(a, b, dims, preferred_element_type=F32)


def _cast_resident(src_ref, dst_ref):
    K = src_ref.shape[0]
    rows = _pick(K, (256, 128))

    def body(c, carry):
        sl = pl.ds(pl.multiple_of(c * rows, rows), rows)
        dst_ref[sl, :] = src_ref[sl, :].astype(dst_ref.dtype)
        return carry

    lax.fori_loop(0, K // rows, body, 0)


def _weight_copy(w_hbm, land_ref, sem, jj, tn, width):
    return pltpu.make_async_copy(w_hbm.at[:, pl.ds(pl.multiple_of(jj * tn, LANES), width)],
                                 land_ref.at[:, pl.ds(0, width)], sem)


def _weight_copy_op(op, w_hbm, land_ref, sem, jj, tn, nj, n_cols):
    rem = n_cols - (nj - 1) * tn
    if rem == tn:
        getattr(_weight_copy(w_hbm, land_ref, sem, jj, tn, tn), op)()
        return

    @pl.when(jj < nj - 1)
    def _():
        getattr(_weight_copy(w_hbm, land_ref, sem, jj, tn, tn), op)()

    @pl.when(jj == nj - 1)
    def _():
        getattr(_weight_copy(w_hbm, land_ref, sem, jj, tn, rem), op)()


def _stream_weights(weights, j, i, nj, tn, n_cols):
    @pl.when(i == 0)
    def _():
        @pl.when(j == 0)
        def _():
            for w_hbm, land_ref, _, sem in weights:
                _weight_copy_op("start", w_hbm, land_ref, sem, j, tn, nj, n_cols)

        for w_hbm, land_ref, bw_ref, sem in weights:
            _weight_copy_op("wait", w_hbm, land_ref, sem, j, tn, nj, n_cols)
            _cast_resident(land_ref, bw_ref)

        @pl.when(j + 1 < nj)
        def _():
            for w_hbm, land_ref, _, sem in weights:
                _weight_copy_op("start", w_hbm, land_ref, sem, j + 1, tn, nj, n_cols)


def _weight_scratch(ks, tn):
    return ([pltpu.VMEM((k, tn), F32) for k in ks] + [pltpu.VMEM((k, tn), BF16) for k in ks]
            + [pltpu.SemaphoreType.DMA(()) for _ in ks])


def _mm_in_kernel(a_ref, w_hbm, o_ref, land_ref, bw_ref, sem, *, nj, tn, n_cols):
    j, i = pl.program_id(0), pl.program_id(1)
    _stream_weights([(w_hbm, land_ref, bw_ref, sem)], j, i, nj, tn, n_cols)
    _per_tile_width(j, nj, tn, n_cols, lambda wd: _store_cols(o_ref, wd, _dot(a_ref[...], bw_ref[:, 0:wd])))


def _per_tile_width(j, nj, tn, n_cols, body):
    rem = n_cols - (nj - 1) * tn
    if rem == tn:
        body(tn)
        return
    pl.when(j < nj - 1)(lambda: body(tn))
    pl.when(j == nj - 1)(lambda: body(rem))


def _store_cols(o_ref, width, value):
    o_ref[:, 0:width] = value.astype(o_ref.dtype)
    if width < o_ref.shape[1]:
        o_ref[:, width:] = jnp.zeros((o_ref.shape[0], o_ref.shape[1] - width), o_ref.dtype)


def _mm_in(a, w, tm, tn):
    M, K = a.shape
    N = w.shape[1]
    nj = pl.cdiv(N, tn)
    return pl.pallas_call(
        functools.partial(_mm_in_kernel, nj=nj, tn=tn, n_cols=N),
        grid=(nj, M // tm),
        in_specs=[pl.BlockSpec((tm, K), lambda j, i: (i, 0)), pl.BlockSpec(memory_space=pl.ANY)],
        out_specs=pl.BlockSpec((tm, tn), lambda j, i: (i, j)),
        out_shape=jax.ShapeDtypeStruct((M, nj * tn), F32),
        scratch_shapes=_weight_scratch((K,), tn),
        compiler_params=_params(("arbitrary", "arbitrary"),
                                _vmem_limit(tm * K * 2, tm * tn * 4, temps=K * tn * 6 + 2 * tm * tn * 4)),
        name="mm_in",
    )(a, w)


def _mm_mix_kernel(a1_ref, a2_ref, z0_ref, z1_ref, bg_ref, w1_hbm, w2_hbm, o_ref,
                     l1_ref, l2_ref, b1_ref, b2_ref, s1, s2, *, nj, tn, n_cols):
    j, i = pl.program_id(0), pl.program_id(1)
    _stream_weights([(w1_hbm, l1_ref, b1_ref, s1), (w2_hbm, l2_ref, b2_ref, s2)], j, i, nj, tn, n_cols)
    g0 = jax.nn.sigmoid(z0_ref[...] + bg_ref[0:1, :])
    g1 = jax.nn.sigmoid(z1_ref[...] + bg_ref[1:2, :])
    o_ref[...] = (g0 * _dot(a1_ref[...], b1_ref[...]) + g1 * _dot(a2_ref[...], b2_ref[...])).astype(o_ref.dtype)


def _mm_mix(o_att, w_pa, o_rwkv, w_pr, z, b_gate, S, D, tm, tn):
    K1, K2 = o_att.shape[1], o_rwkv.shape[1]
    assert D % tn == 0
    nj = D // tn
    return pl.pallas_call(
        functools.partial(_mm_mix_kernel, nj=nj, tn=tn, n_cols=D),
        grid=(nj, S // tm),
        in_specs=[pl.BlockSpec((tm, K1), lambda j, i: (i, 0)), pl.BlockSpec((tm, K2), lambda j, i: (i, 0)),
                  pl.BlockSpec((tm, tn), lambda j, i: (i, j)), pl.BlockSpec((tm, tn), lambda j, i: (i, nj + j)),
                  pl.BlockSpec((2, tn), lambda j, i: (0, j)),
                  pl.BlockSpec(memory_space=pl.ANY), pl.BlockSpec(memory_space=pl.ANY)],
        out_specs=pl.BlockSpec((tm, tn), lambda j, i: (i, j)),
        out_shape=jax.ShapeDtypeStruct((S, D), BF16),
        scratch_shapes=_weight_scratch((K1, K2), tn),
        compiler_params=_params(("arbitrary", "arbitrary"),
                                _vmem_limit(tm * (K1 + K2) * 2, 2 * tm * tn * 4, tm * tn * 2,
                                            temps=(K1 + K2) * tn * 6 + 3 * tm * tn * 4)),
        name="mm_mix",
    )(o_att, o_rwkv, z, z, b_gate, w_pa, w_pr)


def _mm_out_kernel(a_ref, r_ref, g_ref, w_hbm, o_ref, hg_ref, ss_ref, land_ref, bw_ref, sem, *, nj, tn, n_cols):
    j, i = pl.program_id(0), pl.program_id(1)
    _stream_weights([(w_hbm, land_ref, bw_ref, sem)], j, i, nj, tn, n_cols)
    h = r_ref[...] + _dot(a_ref[...], bw_ref[...])
    o_ref[...] = h
    hg_ref[...] = (h * g_ref[...]).astype(hg_ref.dtype)
    ss_ref[...] = jnp.broadcast_to(jnp.sum(h * h, axis=1, keepdims=True), ss_ref.shape)


def _mm_out(a, w, res, gain, tm, tn):
    M, K = a.shape
    N = w.shape[1]
    assert N % tn == 0
    nj = N // tn
    tile = pl.BlockSpec((tm, tn), lambda j, i: (i, j))
    return pl.pallas_call(
        functools.partial(_mm_out_kernel, nj=nj, tn=tn, n_cols=N),
        grid=(nj, M // tm),
        in_specs=[pl.BlockSpec((tm, K), lambda j, i: (i, 0)), tile, pl.BlockSpec((1, tn), lambda j, i: (0, j)),
                  pl.BlockSpec(memory_space=pl.ANY)],
        out_specs=[tile, tile, pl.BlockSpec((tm, LANES), lambda j, i: (i, j))],
        out_shape=[jax.ShapeDtypeStruct((M, N), F32), jax.ShapeDtypeStruct((M, N), BF16),
                   jax.ShapeDtypeStruct((M, nj * LANES), F32)],
        scratch_shapes=_weight_scratch((K,), tn),
        compiler_params=_params(("arbitrary", "arbitrary"),
                                _vmem_limit(tm * K * 2, 2 * tm * tn * 4, tm * tn * 2, tm * LANES * 4,
                                            temps=K * tn * 6 + 2 * tm * tn * 4)),
        name="mm_out",
    )(a, res, gain.reshape(1, N), w)


def _mm_glu_kernel(a_ref, ss_ref, wg_hbm, wu_hbm, o_ref, lg_ref, lu_ref, bg_ref, bu_ref, sg, su,
                     *, nj, tn, n_cols, n_model):
    j, i = pl.program_id(0), pl.program_id(1)
    _stream_weights([(wg_hbm, lg_ref, bg_ref, sg), (wu_hbm, lu_ref, bu_ref, su)], j, i, nj, tn, n_cols)
    ss = ss_ref[:, 0:LANES]
    for t in range(1, ss_ref.shape[1] // LANES):
        ss = ss + ss_ref[:, t * LANES:(t + 1) * LANES]
    scale = lax.rsqrt(ss[:, 0:1] * (1.0 / n_model) + RMS_EPS)

    def body(wd):
        a = a_ref[...]
        gate = scale * _dot(a, bg_ref[:, 0:wd])
        up = scale * _dot(a, bu_ref[:, 0:wd])
        _store_cols(o_ref, wd, gate * jax.nn.sigmoid(gate) * up)

    _per_tile_width(j, nj, tn, n_cols, body)


def _mm_glu(a, ss, wg, wu, tm, tn):
    M, K = a.shape
    N = wg.shape[1]
    nj = pl.cdiv(N, tn)
    return pl.pallas_call(
        functools.partial(_mm_glu_kernel, nj=nj, tn=tn, n_cols=N, n_model=K),
        grid=(nj, M // tm),
        in_specs=[pl.BlockSpec((tm, K), lambda j, i: (i, 0)), pl.BlockSpec((tm, ss.shape[1]), lambda j, i: (i, 0)),
                  pl.BlockSpec(memory_space=pl.ANY), pl.BlockSpec(memory_space=pl.ANY)],
        out_specs=pl.BlockSpec((tm, tn), lambda j, i: (i, j)),
        out_shape=jax.ShapeDtypeStruct((M, N), BF16),
        scratch_shapes=_weight_scratch((K, K), tn),
        compiler_params=_params(("arbitrary", "arbitrary"),
                                _vmem_limit(tm * K * 2, tm * ss.shape[1] * 4, tm * tn * 2,
                                            temps=2 * K * tn * 6 + 6 * tm * tn * 4)),
        name="mm_glu",
    )(a, ss, wg, wu)


def _mm_res_kernel(a_ref, b_ref, r_ref, o_ref, *, nk):
    k = pl.program_id(2)
    if nk == 1:
        o_ref[...] = r_ref[...] + _dot(a_ref[...], b_ref[...])
    else:
        @pl.when(k == 0)
        def _():
            o_ref[...] = r_ref[...] + _dot(a_ref[...], b_ref[...])

        @pl.when(k > 0)
        def _():
            o_ref[...] = o_ref[...] + _dot(a_ref[...], b_ref[...])


def _mm_res(a, b, res, tm, tn, tk):
    M, K = a.shape
    N = b.shape[1]
    nk = K // tk
    return pl.pallas_call(
        functools.partial(_mm_res_kernel, nk=nk),
        grid=(M // tm, N // tn, nk),
        in_specs=[pl.BlockSpec((tm, tk), lambda i, j, k: (i, k)), pl.BlockSpec((tk, tn), lambda i, j, k: (k, j)),
                  pl.BlockSpec((tm, tn), lambda i, j, k: (i, j))],
        out_specs=pl.BlockSpec((tm, tn), lambda i, j, k: (i, j)),
        out_shape=jax.ShapeDtypeStruct((M, N), F32),
        compiler_params=_params(("arbitrary", "arbitrary", "arbitrary"),
                                _vmem_limit(tm * tk * 2, tk * tn * 2, 2 * tm * tn * 4, temps=tm * tn * 4)),
        name="mm_down",
    )(a, b, res)


NEG_BIG = -1e30


def _rope(t, cos, sin):
    half = ROPE_DIMS // 2
    lane = lax.broadcasted_iota(jnp.int32, t.shape, 1)
    up = jnp.concatenate([t[:, half:], t[:, :half]], axis=1)
    down = jnp.concatenate([t[:, HEAD_DIM - half:], t[:, :HEAD_DIM - half]], axis=1)
    return t * cos + jnp.where(lane < half, up, down) * sin


def _attn_kernel(sink_ref, q_ref, k_ref, v_ref, cos_ref, sin_ref, o_ref, kb_ref, vb_ref, *, nb, group, qblocks):
    h, i = pl.program_id(0), pl.program_id(1)
    lp = k_ref.shape[0]

    @pl.when(i == 0)
    def _():
        def body(c, carry):
            sl = pl.ds(pl.multiple_of(c * BLOCK, BLOCK), BLOCK)
            kb_ref[sl, :] = _rope(k_ref[sl, :], cos_ref[sl, :], sin_ref[sl, :]).astype(BF16)
            vb_ref[sl, :] = v_ref[sl, :].astype(BF16)
            return carry
        lax.fori_loop(0, lp // BLOCK, body, 0)

    zpad = jnp.zeros((BLOCK - N_META, HEAD_DIM), BF16)
    k_meta = jnp.concatenate([kb_ref[lp - N_META:lp, :], zpad], axis=0)
    v_meta = jnp.concatenate([vb_ref[lp - N_META:lp, :], zpad], axis=0)
    rows = lax.broadcasted_iota(jnp.int32, (BLOCK, 4 * BLOCK), 0)
    cols = lax.broadcasted_iota(jnp.int32, (BLOCK, 4 * BLOCK), 1)
    is_meta = (cols >= 3 * BLOCK) & (cols < 3 * BLOCK + N_META)
    scale = HEAD_DIM ** -0.5
    k_all, v_all, bias, q, sink = [], [], [], [], []
    for b in range(qblocks):
        qi = i * qblocks + b
        kb0 = jnp.clip(qi - 1, 0, nb - 3)
        win = pl.ds(pl.multiple_of(kb0 * BLOCK, BLOCK), 3 * BLOCK)
        k_b = jnp.concatenate([kb_ref[win, :], k_meta], axis=0)
        v_b = jnp.concatenate([vb_ref[win, :], v_meta], axis=0)
        rel = cols - rows + (kb0 - qi) * BLOCK
        ok = ((cols < 3 * BLOCK) & (jnp.abs(rel) <= WINDOW)) | is_meta
        bias_b = jnp.where(ok, 0.0, NEG_BIG).astype(F32)
        qrows = pl.ds(pl.multiple_of(qi * BLOCK, BLOCK), BLOCK)
        cos_q, sin_q = cos_ref[qrows, :], sin_ref[qrows, :]
        for g in range(group):
            k_all.append(k_b)
            v_all.append(v_b)
            bias.append(bias_b)
            q.append(_rope(q_ref[b * BLOCK:(b + 1) * BLOCK, g * HEAD_DIM:(g + 1) * HEAD_DIM],
                           cos_q, sin_q).astype(BF16))
            sink.append(sink_ref[h * group + g])
    n = range(len(q))
    s = [_dot(q[c], k_all[c], NT) * scale + bias[c] for c in n]
    m = [jnp.maximum(jnp.max(s[c], axis=-1, keepdims=True), sink[c]) for c in n]
    p = [jnp.exp(s[c] - m[c]) for c in n]
    denom = [jnp.sum(p[c], axis=-1, keepdims=True) + jnp.exp(sink[c] - m[c]) for c in n]
    o = [_dot(p[c].astype(BF16), v_all[c]) / denom[c] for c in n]
    for c in n:
        b, g = divmod(c, group)
        o_ref[b * BLOCK:(b + 1) * BLOCK, g * HEAD_DIM:(g + 1) * HEAD_DIM] = o[c].astype(o_ref.dtype)


def _attention(z, cos_t, sin_t, sink, S, n_heads, n_kv, off_q):
    Lp = z.shape[0]
    group = n_heads // n_kv
    nb = S // BLOCK
    assert nb >= 3
    qw = group * HEAD_DIM
    kcol = off_q // HEAD_DIM + n_heads
    vcol = kcol + n_kv
    col_bytes = Lp * HEAD_DIM * 4
    qblocks = _pick(nb, (8, 4, 2, 1))
    tq = qblocks * BLOCK
    return pl.pallas_call(
        functools.partial(_attn_kernel, nb=nb, group=group, qblocks=qblocks),
        grid_spec=pltpu.PrefetchScalarGridSpec(
            num_scalar_prefetch=1,
            grid=(n_kv, nb // qblocks),
            in_specs=[pl.BlockSpec((tq, qw), lambda h, i, s: (i, off_q // qw + h)),
                      pl.BlockSpec((Lp, HEAD_DIM), lambda h, i, s: (0, kcol + h)),
                      pl.BlockSpec((Lp, HEAD_DIM), lambda h, i, s: (0, vcol + h)),
                      pl.BlockSpec((Lp, HEAD_DIM), lambda h, i, s: (0, 0)),
                      pl.BlockSpec((Lp, HEAD_DIM), lambda h, i, s: (0, 0))],
            out_specs=pl.BlockSpec((tq, qw), lambda h, i, s: (i, h)),
            scratch_shapes=[pltpu.VMEM((Lp, HEAD_DIM), BF16), pltpu.VMEM((Lp, HEAD_DIM), BF16)]),
        out_shape=jax.ShapeDtypeStruct((S, n_heads * HEAD_DIM), BF16),
        compiler_params=_params(("arbitrary", "arbitrary"),
                                _vmem_limit(tq * qw * 4, 4 * col_bytes, tq * qw * 2,
                                            temps=col_bytes + 8 * group * tq * 512 * 4)),
        name="attention",
    )(sink.astype(F32), z, z, z, cos_t, sin_t)


def _split3(x):
    hi = x.astype(BF16)
    r1 = x - hi.astype(F32)
    mid = r1.astype(BF16)
    lo = (r1 - mid.astype(F32)).astype(BF16)
    return hi, mid, lo


def _head_sum(x, ones_bd):
    outs = []
    for s in range(x.shape[1] // LANES):
        parts = _split3(x[:, s * LANES:(s + 1) * LANES])
        outs.append(sum(_dot(p, ones_bd) for p in parts))
    return outs[0] if len(outs) == 1 else jnp.concatenate(outs, axis=1)


def _prep_kernel(r_ref, rp_ref, rn_ref, k_ref, kp_ref, kn_ref, v_ref, vp_ref, vn_ref,
                 lo_ref, lop_ref, lon_ref, gd_ref, gdp_ref, gdn_ref,
                 pc_ref, pl_ref, wl_ref, g2_ref, ones_ref,
                 ro_ref, vo_ref, kko_ref, lwf_ref, lwb_ref, kdf_ref, kdb_ref, bdf_ref, bdb_ref, go_ref, bo_ref,
                 *, n_tiles, tr, lora_tanh_cols):
    i = pl.program_id(1)

    def shifted(main_ref, prev_ref, next_ref, mup, mun):
        x = main_ref[...]
        row8 = lax.broadcasted_iota(jnp.int32, (8, x.shape[1]), 0)
        down = pltpu.roll(x, 1, 0)
        up = pltpu.roll(x, tr - 1, 0)
        prev = jnp.concatenate([jnp.where(row8 == 0, prev_ref[7:8, :], down[0:8]), down[8:]], axis=0)
        nxt = jnp.concatenate([up[:tr - 8], jnp.where(row8 == 7, next_ref[0:1, :], up[tr - 8:])], axis=0)
        return x + mup * (prev - x) + mun * (nxt - x)

    r = shifted(r_ref, rp_ref, rn_ref, pc_ref[0:1, :], pc_ref[1:2, :])
    k = shifted(k_ref, kp_ref, kn_ref, pc_ref[2:3, :], pc_ref[3:4, :])
    v = shifted(v_ref, vp_ref, vn_ref, pc_ref[4:5, :], pc_ref[5:6, :])
    lo = shifted(lo_ref, lop_ref, lon_ref, pl_ref[0:1, 0:LORA_PAD], pl_ref[1:2, 0:LORA_PAD])
    gd = shifted(gd_ref, gdp_ref, gdn_ref, pl_ref[0:1, LORA_PAD:], pl_ref[1:2, LORA_PAD:])

    row = lax.broadcasted_iota(jnp.int32, (tr, 1), 0)
    valid = jnp.where((i < n_tiles - 1) | (row >= tr - N_META), 1.0, 0.0).astype(F32)

    lane = lax.broadcasted_iota(jnp.int32, lo.shape, 1)
    lo_act = jnp.where(lane < lora_tanh_cols, jnp.tanh(lo), lo).astype(BF16)
    dec_f = _dot(lo_act, wl_ref[0])
    dec_b = _dot(lo_act, wl_ref[1])
    apre_f = _dot(lo_act, wl_ref[2])
    apre_b = _dot(lo_act, wl_ref[3])
    g = _dot(jax.nn.sigmoid(lo).astype(BF16), g2_ref[0]) + _dot(jax.nn.sigmoid(gd).astype(BF16), g2_ref[1])

    def log_decay(dec, w0):
        return -math.exp(-0.5) * jax.nn.sigmoid(w0 + dec)

    a_f = jax.nn.sigmoid(pc_ref[8:9, :] + apre_f)
    a_b = jax.nn.sigmoid(pc_ref[9:10, :] + apre_b)
    kk = k * pc_ref[10:11, :]
    ss = _head_sum(kk * kk, ones_ref[...])
    kk = kk / jnp.maximum(jnp.sqrt(ss), 1e-12) * valid
    k_a = pc_ref[11:12, :]
    kv = k * valid

    rv = r * valid
    vv = v * valid
    kd_f = kv * (1.0 + (a_f - 1.0) * k_a)
    kd_b = kv * (1.0 + (a_b - 1.0) * k_a)
    ro_ref[...] = rv
    vo_ref[...] = vv
    kko_ref[...] = kk
    lwf_ref[...] = log_decay(dec_f, pc_ref[6:7, :])
    lwb_ref[...] = log_decay(dec_b, pc_ref[7:8, :])
    kdf_ref[...] = kd_f
    kdb_ref[...] = kd_b
    bdf_ref[...] = kk * a_f
    bdb_ref[...] = kk * a_b
    go_ref[...] = g
    bo_ref[...] = _head_sum(rv * (kd_f + kd_b) * pc_ref[12:13, :], ones_ref[...]) * vv


def _rwkv_prep(z, off_r, C, pc, plo, wl, g2, ones_bd, lora_tanh_cols):
    Lp = z.shape[0]
    tr = BLOCK
    ct = _pick(C, (1024, 512))
    n_tiles = Lp // tr
    n8 = Lp // 8
    off_lo = off_r + 3 * C
    off_gd = off_lo + LORA_PAD
    assert off_r % ct == 0 and off_lo % LORA_PAD == 0

    def seg_specs(off, w, with_c):
        cb = off // w

        def col(c):
            return cb + c if with_c else cb
        return [pl.BlockSpec((tr, w), lambda c, i: (i, col(c))),
                pl.BlockSpec((8, w), lambda c, i: ((i * (tr // 8) + n8 - 1) % n8, col(c))),
                pl.BlockSpec((8, w), lambda c, i: (((i + 1) * (tr // 8)) % n8, col(c)))]

    in_specs = (seg_specs(off_r, ct, True) + seg_specs(off_r + C, ct, True) + seg_specs(off_r + 2 * C, ct, True)
                + seg_specs(off_lo, LORA_PAD, False) + seg_specs(off_gd, LORA_PAD, False)
                + [pl.BlockSpec((16, ct), lambda c, i: (0, c)),
                   pl.BlockSpec((8, 2 * LORA_PAD), lambda c, i: (0, 0)),
                   pl.BlockSpec((4, LORA_PAD, ct), lambda c, i: (0, 0, c)),
                   pl.BlockSpec((2, LORA_PAD, ct), lambda c, i: (0, 0, c)),
                   pl.BlockSpec((LANES, LANES), lambda c, i: (0, 0))])
    out_spec = pl.BlockSpec((tr, ct), lambda c, i: (i, c))
    out_sds = jax.ShapeDtypeStruct((Lp, C), F32)
    return pl.pallas_call(
        functools.partial(_prep_kernel, n_tiles=n_tiles, tr=tr, lora_tanh_cols=lora_tanh_cols),
        grid=(C // ct, n_tiles),
        in_specs=in_specs,
        out_specs=[out_spec] * 11,
        out_shape=[out_sds] * 11,
        compiler_params=_params(("arbitrary", "arbitrary"),
                                _vmem_limit(3 * tr * ct * 4, 2 * tr * LORA_PAD * 4, 4 * LORA_PAD * ct * 2,
                                            2 * LORA_PAD * ct * 2, 11 * tr * ct * 4, temps=24 * tr * ct * 4)),
        name="rwkv_prep",
    )(*([z] * 15), pc, plo, wl, g2, ones_bd)


SCAN_PAIRS = 16


def _scan_consts():
    C = CHUNK
    row = lax.broadcasted_iota(jnp.int32, (C, 2 * C), 0)
    col = lax.broadcasted_iota(jnp.int32, (C, 2 * C), 1) % C
    rr = lax.broadcasted_iota(jnp.int32, (LANES, LANES), 0)
    cc = lax.broadcasted_iota(jnp.int32, (LANES, LANES), 1)
    return dict(
        row=lax.broadcasted_iota(jnp.int32, (C, LANES), 0),
        head0=lax.broadcasted_iota(jnp.int32, (C, LANES), 1) < RWKV_HEAD,
        strict_f=col < row, strict_b=col > row, incl_f=col <= row, incl_b=col >= row,
        eye_sbs=jnp.where(col == row, 1.0, 0.0).astype(F32),
        same_head=(rr // RWKV_HEAD) == (cc // RWKV_HEAD), eye=rr == cc)


def _scan_chains(chains, cst):
    C = CHUNK
    head0, same, eye = cst["head0"], cst["same_head"], cst["eye"]
    n = range(len(chains))
    rev = [c[7] for c in chains]
    r, v, al, lw, k, be, h = ([c[i] for c in chains] for i in range(7))
    strict = [cst["strict_b" if x else "strict_f"] for x in rev]
    incl = [cst["incl_b" if x else "incl_f"] for x in rev]

    def bd(x):
        xb = x.astype(BF16)
        zero = jnp.zeros_like(xb)
        return jnp.concatenate([jnp.where(head0, xb, zero), jnp.where(head0, zero, xb)], axis=0)

    def mm(a, b_bf16, dims=NN):
        return _dot(a.astype(BF16), b_bf16, dims)

    row = cst["row"]
    cl = list(lw)
    sh = 1
    while sh < C:
        cl = [cl[i] + (jnp.where(row < C - sh, pltpu.roll(cl[i], C - sh, 0), 0.0) if rev[i] else
                       jnp.where(row >= sh, pltpu.roll(cl[i], sh, 0), 0.0)) for i in n]
        sh *= 2
    total = [cl[i][0:1, :] if rev[i] else cl[i][C - 1:C, :] for i in n]
    a_t = [al[i] * jnp.exp(cl[i] - lw[i]) for i in n]
    r_t = [r[i] * jnp.exp(cl[i]) for i in n]
    w_inv = [jnp.exp(-cl[i]) for i in n]
    w_rest = [jnp.exp(total[i] - cl[i]) for i in n]
    bd_v = [bd(v[i]) for i in n]
    sc = [mm(jnp.concatenate([a_t[i], r_t[i]], axis=0),
             jnp.concatenate([bd(k[i] * w_inv[i]), bd(be[i] * w_inv[i])], axis=0), NT) for i in n]
    a_ak = [jnp.where(strict[i], sc[i][0:C, 0:2 * C], 0.0) for i in n]
    n_ab = [jnp.where(strict[i], sc[i][0:C, 2 * C:], 0.0) for i in n]
    a_rk = [jnp.where(incl[i], sc[i][C:, 0:2 * C], 0.0) for i in n]
    a_rb = [jnp.where(incl[i], sc[i][C:, 2 * C:], 0.0) for i in n]

    t_inv = [cst["eye_sbs"] - n_ab[i] for i in n]
    sq = [mm(n_ab[i], bd(n_ab[i])) for i in n]
    n_round = 2
    while (1 << n_round) < C:
        both = [mm(jnp.concatenate([t_inv[i], sq[i]], axis=0), bd(sq[i])) for i in n]
        t_inv = [t_inv[i] + both[i][0:C] for i in n]
        sq = [both[i][C:] for i in n]
        n_round += 1
    t_inv = [t_inv[i] + mm(t_inv[i], bd(sq[i])) for i in n]

    akv = [mm(jnp.concatenate([a_ak[i], a_rk[i]], axis=0), bd_v[i]) for i in n]
    tp = [mm(t_inv[i], jnp.concatenate([bd(akv[i][0:C]), bd(a_t[i])], axis=1)) for i in n]
    p0 = [tp[i][:, 0:LANES] for i in n]
    at = [tp[i][:, LANES:] for i in n]
    rb = [mm(a_rb[i], jnp.concatenate([bd(p0[i]), bd(at[i])], axis=1)) for i in n]
    y0 = [akv[i][C:] - rb[i][:, 0:LANES] for i in n]
    rh = [r_t[i] - rb[i][:, LANES:] for i in n]
    bt = [mm(be[i] * w_rest[i], jnp.concatenate([at[i], p0[i]], axis=1).astype(BF16), TN) for i in n]
    kv = [mm(k[i] * w_rest[i], v[i].astype(BF16), TN) for i in n]
    m_mat = [jnp.where(same, jnp.where(eye, jnp.broadcast_to(jnp.exp(total[i]), (LANES, LANES)), 0.0)
                       - bt[i][:, 0:LANES], 0.0) for i in n]
    g_mat = [jnp.where(same, kv[i] - bt[i][:, LANES:], 0.0) for i in n]
    yh = [mm(jnp.concatenate([rh[i], m_mat[i]], axis=0), h[i].astype(BF16)) for i in n]
    return [(yh[i][0:C] + y0[i], yh[i][C:] + g_mat[i]) for i in n]


def _scan_kernel(rf_ref, vf_ref, af_ref, lwf_ref, kf_ref, bf_ref,
                 rb_ref, vb_ref, ab_ref, lwb_ref, kb_ref, bb_ref, wd_ref,
                 yf_ref, yb_ref, wdo_ref, hf_ref, hb_ref, *, pairs):
    @pl.when(pl.program_id(1) == 0)
    def _():
        hf_ref[...] = jnp.zeros(hf_ref.shape, F32)
        hb_ref[...] = jnp.zeros(hb_ref.shape, F32)

    wdo_ref[...] = wd_ref[...].astype(wdo_ref.dtype)

    cst = _scan_consts()
    chains = []
    for g in range(pairs):
        sl = slice(g * LANES, (g + 1) * LANES)
        chains.append((rf_ref[:, sl], vf_ref[:, sl], af_ref[:, sl], lwf_ref[:, sl], kf_ref[:, sl], bf_ref[:, sl],
                       hf_ref[g], False))
        chains.append((rb_ref[:, sl], vb_ref[:, sl], ab_ref[:, sl], lwb_ref[:, sl], kb_ref[:, sl], bb_ref[:, sl],
                       hb_ref[g], True))
    res = _scan_chains(chains, cst)
    for g in range(pairs):
        sl = slice(g * LANES, (g + 1) * LANES)
        yf_ref[:, sl], hf_ref[g] = res[2 * g]
        yb_ref[:, sl], hb_ref[g] = res[2 * g + 1]


def _rwkv_scan(r, v, kk, lw_f, lw_b, kd_f, kd_b, bd_f, bd_b, w_cast, S):
    Lp, C = r.shape
    n_real = S // CHUNK
    n_chunks = Lp // CHUNK
    steps = n_real + 1
    pairs = _pick(C // LANES, (SCAN_PAIRS, 2, 1))
    w = pairs * LANES
    total_steps = (C // w) * steps
    wr, wc = w_cast.shape
    cast_rows = next(rb for rb in (128, 256, 512, 1024, wr) if wr % rb == 0 and wr // rb <= total_steps)
    n_cast = wr // cast_rows
    cast_spec = pl.BlockSpec((cast_rows, wc), lambda p, s: (jnp.minimum(p * steps + s, n_cast - 1), 0))

    def fwd(p, s):
        return ((s + n_real + 1) % n_chunks, p)

    def bwd(p, s):
        return (n_real - s, p)

    spec_f = pl.BlockSpec((CHUNK, w), fwd)
    spec_b = pl.BlockSpec((CHUNK, w), bwd)
    out_sds = jax.ShapeDtypeStruct((S + CHUNK, C), F32)
    out_f = pl.BlockSpec((CHUNK, w), lambda p, s: ((s + n_real) % (n_real + 1), p))
    return pl.pallas_call(
        functools.partial(_scan_kernel, pairs=pairs),
        grid=(C // w, steps),
        in_specs=[spec_f] * 6 + [spec_b] * 6 + [cast_spec],
        out_specs=[out_f, spec_b, cast_spec],
        out_shape=[out_sds, out_sds, jax.ShapeDtypeStruct((wr, wc), BF16)],
        scratch_shapes=[pltpu.VMEM((pairs, LANES, LANES), F32), pltpu.VMEM((pairs, LANES, LANES), F32)],
        compiler_params=_params(("arbitrary", "arbitrary"),
                                _vmem_limit(14 * CHUNK * w * 4, cast_rows * wc * 6, temps=16 << 20)),
        name="rwkv_scan",
    )(r, v, kk, lw_f, kd_f, bd_f, r, v, kk, lw_b, kd_b, bd_b, w_cast)


def _post_kernel(yf_ref, yb_ref, bonus_ref, g_ref, pq_ref, ones_ref, o_ref):
    ones_bd = ones_ref[...]
    inv_n = 1.0 / RWKV_HEAD
    y = yf_ref[...] + yb_ref[...]
    mean = _head_sum(y, ones_bd) * inv_n
    yc = y - mean
    var = _head_sum(yc * yc, ones_bd) * inv_n
    yn = yc * lax.rsqrt(var + GN_EPS) * pq_ref[0:1, :] + pq_ref[1:2, :]
    o_ref[...] = ((yn + bonus_ref[...]) * g_ref[...]).astype(o_ref.dtype)


def _rwkv_post(y_f, y_b, bonus, g, pq, ones_bd, S):
    C = g.shape[1]
    tr = 512
    ct = _pick(C, (1024, 512))
    spec = pl.BlockSpec((tr, ct), lambda i, c: (i, c))
    return pl.pallas_call(
        _post_kernel,
        grid=(S // tr, C // ct),
        in_specs=[spec] * 4 + [pl.BlockSpec((8, ct), lambda i, c: (0, c)),
                               pl.BlockSpec((LANES, LANES), lambda i, c: (0, 0))],
        out_specs=spec,
        out_shape=jax.ShapeDtypeStruct((S, C), BF16),
        compiler_params=_params(("arbitrary", "arbitrary"),
                                _vmem_limit(4 * tr * ct * 4, tr * ct * 2, temps=16 * tr * ct * 4)),
        name="rwkv_post",
    )(y_f, y_b, bonus, g, pq, ones_bd)


IN_TN = 1024


def _matmul_tiles(S, D, F, Lp):
    return dict(
        in_tm=_pick(Lp, (1040, 640)),
        proj_tm=512, proj_tn=_pick(D, (1024, 512)),
        ffn_tm=_pick(S, (1024, 512)), glu_tn=512,
        down_tn=512, down_tk=F // 2,
    )

def kernel(x, meta_tokens, norm_mix_w, w_in, b_gate, mu_prev, mu_next, dec_w0, dec_w2, iclr_a0, iclr_a2,
           gate_w2, k_k, k_a, r_k, ln_x_w, ln_x_b, attn_sink, w_proj_attn, w_proj_rwkv, w_out, norm_ffn_w,
           w_ffn_gate, w_ffn_up, w_ffn_down, norm_final_w):
    B, S, D = x.shape
    assert B == 1 and norm_mix_w.shape[0] == 1, "one sequence, one layer"
    C = k_k.shape[-1]
    AW = w_proj_attn.shape[1]
    n_heads = attn_sink.shape[-1]
    DL, IL, GL = dec_w2.shape[2], iclr_a2.shape[2], gate_w2.shape[1]
    shift_w = mu_prev.shape[-1]
    KVW = (w_in.shape[-1] - 2 * D - AW - shift_w) // 2
    n_kv = KVW // HEAD_DIM
    F = w_ffn_gate.shape[-1]
    lora_w = 2 * DL + 2 * IL
    assert shift_w == 3 * C + lora_w + GL and lora_w <= LORA_PAD and GL <= LORA_PAD
    assert S % BLOCK == 0 and C % LANES == 0 and n_heads * HEAD_DIM == AW
    Lp = S + PAD_ROWS

    off_q = 2 * D
    off_r = off_q + AW + 2 * KVW
    off_lo = off_r + 3 * C
    n_in_pad = -(-w_in.shape[-1] // IN_TN) * IN_TN
    assert n_in_pad >= off_lo + 2 * LORA_PAD and off_lo % LORA_PAD == 0
    low_w = lora_w + GL
    gl_a = LORA_PAD - lora_w

    mu_p, mu_n = mu_prev[0], mu_next[0]
    zc = jnp.zeros((C,), F32)
    pc = jnp.stack([mu_p[:C], mu_n[:C], mu_p[C:2 * C], mu_n[C:2 * C], mu_p[2 * C:3 * C], mu_n[2 * C:3 * C],
                    dec_w0[0, 0], dec_w0[0, 1], iclr_a0[0, 0], iclr_a0[0, 1], k_k[0], k_a[0], r_k[0], zc, zc, zc])
    plo = jnp.zeros((8, 2 * LORA_PAD), F32).at[0, :low_w].set(mu_p[3 * C:]).at[1, :low_w].set(mu_n[3 * C:])
    wl = jnp.zeros((4, LORA_PAD, C), F32)
    wl = wl.at[0, 0:DL].set(dec_w2[0, 0]).at[1, DL:2 * DL].set(dec_w2[0, 1])
    wl = wl.at[2, 2 * DL:2 * DL + IL].set(iclr_a2[0, 0]).at[3, 2 * DL + IL:lora_w].set(iclr_a2[0, 1])
    wl = wl.astype(BF16)
    g2 = jnp.zeros((2, LORA_PAD, C), F32).at[0, lora_w:].set(gate_w2[0, :gl_a]).at[1, :GL - gl_a].set(
        gate_w2[0, gl_a:]).astype(BF16)
    pq = jnp.zeros((8, C), F32).at[0].set(ln_x_w[0]).at[1].set(ln_x_b[0])
    lane = jnp.arange(LANES)
    ones_bd = (lane[:, None] // RWKV_HEAD == lane[None, :] // RWKV_HEAD).astype(BF16)


    pos = jnp.concatenate([jnp.arange(N_META, N_META + S), jnp.zeros((PAD_ROWS - N_META,), jnp.int32),
                           jnp.arange(N_META)]).astype(F32)
    inv = ROPE_THETA ** (-jnp.arange(0, ROPE_DIMS, 2, dtype=F32) / ROPE_DIMS)
    ang = pos[:, None] * inv[None, :]
    ones_rest = jnp.ones((Lp, HEAD_DIM - ROPE_DIMS), F32)
    cos_t = jnp.concatenate([jnp.cos(ang), jnp.cos(ang), ones_rest], axis=1)
    sin_t = jnp.concatenate([-jnp.sin(ang), jnp.sin(ang), 0.0 * ones_rest], axis=1)

    t = _matmul_tiles(S, D, F, Lp)
    x2 = x[0]
    h_ext = _norm_ext(x2, meta_tokens.astype(x.dtype), norm_mix_w[0])
    z = _mm_in(h_ext, w_in[0], t["in_tm"], IN_TN)
    o_att = _attention(z, cos_t, sin_t, attn_sink[0], S, n_heads, n_kv, off_q)
    r_s, v_s, kk, lw_f, lw_b, kd_f, kd_b, bd_f, bd_b, g, bonus = _rwkv_prep(
        z, off_r, C, pc, plo, wl, g2, ones_bd, 2 * DL)
    y_f, y_b, w_d = _rwkv_scan(r_s, v_s, kk, lw_f, lw_b, kd_f, kd_b, bd_f, bd_b, w_ffn_down[0], S)
    o_rwkv = _rwkv_post(y_f, y_b, bonus, g, pq, ones_bd, S)
    mixed = _mm_mix(o_att, w_proj_attn[0], o_rwkv, w_proj_rwkv[0], z, b_gate[0], S, D, t["proj_tm"], t["proj_tn"])
    h1, h1g, h1ss = _mm_out(mixed, w_out[0], x2, norm_ffn_w[0], t["proj_tm"], t["proj_tn"])
    act = _mm_glu(h1g, h1ss, w_ffn_gate[0], w_ffn_up[0], t["ffn_tm"], t["glu_tn"])
    h2 = _mm_res(act, w_d, h1, t["ffn_tm"], t["down_tn"], t["down_tk"])
    y = _norm(h2, norm_final_w, x.dtype)
    return y[None]
```

```python
import functools
import math

import jax
import jax.numpy as jnp
from jax import lax
from jax.experimental import pallas as pl
from jax.experimental.pallas import tpu as pltpu

F32 = jnp.float32
BF16 = jnp.bfloat16

N_META = 16
HEAD_DIM = 128
WINDOW = 128
BLOCK = 128
ROPE_DIMS = HEAD_DIM // 4
ROPE_THETA = 500000.0
RWKV_HEAD = 64
RMS_EPS = 1e-6
GN_EPS = 64e-5
LANES = 128
PAD_ROWS = BLOCK
CHUNK = 64
LORA_PAD = 512
VMEM_PHYSICAL = 64 * 1024 * 1024

NN = (((1,), (0,)), ((), ()))
NT = (((1,), (1,)), ((), ()))
TN = (((0,), (0,)), ((), ()))


def _vmem_limit(*block_bytes, temps=0):
    need = 2 * sum(block_bytes) + temps + (4 << 20)
    return int(min(max(need, 16 << 20), VMEM_PHYSICAL - (4 << 20)))


def _params(sem, limit):
    return pltpu.CompilerParams(dimension_semantics=sem, vmem_limit_bytes=limit)


def _pick(n, cands):
    for c in cands:
        if n % c == 0:
            return c
    raise ValueError(f"no tile in {cands} divides {n}")


def _rms(x, w):
    return x * lax.rsqrt(jnp.mean(x * x, axis=-1, keepdims=True) + RMS_EPS) * w


def _norm_ext_kernel(x_ref, meta_ref, w_ref, o_ref, *, n_steps, real_last):
    i = pl.program_id(0)
    tr = o_ref.shape[0]

    @pl.when(i < n_steps - 1)
    def _():
        o_ref[...] = _rms(x_ref[...], w_ref[...]).astype(o_ref.dtype)

    @pl.when(i == n_steps - 1)
    def _():
        if real_last:
            o_ref[0:real_last, :] = _rms(x_ref[0:real_last, :], w_ref[...]).astype(o_ref.dtype)
        o_ref[real_last:tr - N_META, :] = jnp.zeros((tr - N_META - real_last, o_ref.shape[1]), o_ref.dtype)
        o_ref[tr - N_META:, :] = _rms(meta_ref[...], w_ref[...]).astype(o_ref.dtype)


def _norm_ext(x, meta, w):
    S, D = x.shape
    Lp = S + PAD_ROWS
    tr = _pick(Lp, (640, BLOCK))
    n_steps = Lp // tr
    real_last = S - (n_steps - 1) * tr
    last_x_blk = (S - 1) // tr
    return pl.pallas_call(
        functools.partial(_norm_ext_kernel, n_steps=n_steps, real_last=real_last),
        grid=(n_steps,),
        in_specs=[pl.BlockSpec((tr, D), lambda i: (jnp.minimum(i, last_x_blk), 0)),
                  pl.BlockSpec((N_META, D), lambda i: (0, 0)),
                  pl.BlockSpec((1, D), lambda i: (0, 0))],
        out_specs=pl.BlockSpec((tr, D), lambda i: (i, 0)),
        out_shape=jax.ShapeDtypeStruct((Lp, D), BF16),
        compiler_params=_params(("arbitrary",), _vmem_limit(tr * D * 4, tr * D * 2, temps=2 * tr * D * 4)),
        name="norm_ext",
    )(x, meta, w.reshape(1, D))


def _norm_kernel(x_ref, w_ref, o_ref):
    o_ref[...] = _rms(x_ref[...], w_ref[...]).astype(o_ref.dtype)


def _norm(x, w, out_dtype, tr=512):
    S, D = x.shape
    return pl.pallas_call(
        _norm_kernel,
        grid=(S // tr,),
        in_specs=[pl.BlockSpec((tr, D), lambda i: (i, 0)), pl.BlockSpec((1, D), lambda i: (0, 0))],
        out_specs=pl.BlockSpec((tr, D), lambda i: (i, 0)),
        out_shape=jax.ShapeDtypeStruct((S, D), out_dtype),
        compiler_params=_params(("arbitrary",), _vmem_limit(tr * D * 4, tr * D * 4, temps=2 * tr * D * 4)),
        name="norm",
    )(x, w.reshape(1, D))


def _dot(a, b, dims=NN):
    return lax.dot_general(a, b, dims, preferred_element_type=F32)


def _cast_resident(src_ref, dst_ref):
    K = src_ref.shape[0]
    rows = _pick(K, (256, 128))

    def body(c, carry):
        sl = pl.ds(pl.multiple_of(c * rows, rows), rows)
        dst_ref[sl, :] = src_ref[sl, :].astype(dst_ref.dtype)
        return carry

    lax.fori_loop(0, K // rows, body, 0)


def _weight_copy(w_hbm, land_ref, sem, jj, tn, width):
    return pltpu.make_async_copy(w_hbm.at[:, pl.ds(pl.multiple_of(jj * tn, LANES), width)],
                                 land_ref.at[:, pl.ds(0, width)], sem)


def _weight_copy_op(op, w_hbm, land_ref, sem, jj, tn, nj, n_cols):
    rem = n_cols - (nj - 1) * tn
    if rem == tn:
        getattr(_weight_copy(w_hbm, land_ref, sem, jj, tn, tn), op)()
        return

    @pl.when(jj < nj - 1)
    def _():
        getattr(_weight_copy(w_hbm, land_ref, sem, jj, tn, tn), op)()

    @pl.when(jj == nj - 1)
    def _():
        getattr(_weight_copy(w_hbm, land_ref, sem, jj, tn, rem), op)()


def _stream_weights(weights, j, i, nj, tn, n_cols):
    @pl.when(i == 0)
    def _():
        @pl.when(j == 0)
        def _():
            for w_hbm, land_ref, _, sem in weights:
                _weight_copy_op("start", w_hbm, land_ref, sem, j, tn, nj, n_cols)

        for w_hbm, land_ref, bw_ref, sem in weights:
            _weight_copy_op("wait", w_hbm, land_ref, sem, j, tn, nj, n_cols)
            _cast_resident(land_ref, bw_ref)

        @pl.when(j + 1 < nj)
        def _():
            for w_hbm, land_ref, _, sem in weights:
                _weight_copy_op("start", w_hbm, land_ref, sem, j + 1, tn, nj, n_cols)


def _weight_scratch(ks, tn):
    return ([pltpu.VMEM((k, tn), F32) for k in ks] + [pltpu.VMEM((k, tn), BF16) for k in ks]
            + [pltpu.SemaphoreType.DMA(()) for _ in ks])


def _mm_in_kernel(a_ref, w_hbm, o_ref, land_ref, bw_ref, sem, *, nj, tn, n_cols):
    j, i = pl.program_id(0), pl.program_id(1)
    _stream_weights([(w_hbm, land_ref, bw_ref, sem)], j, i, nj, tn, n_cols)
    _per_tile_width(j, nj, tn, n_cols, lambda wd: _store_cols(o_ref, wd, _dot(a_ref[...], bw_ref[:, 0:wd])))


def _per_tile_width(j, nj, tn, n_cols, body):
    rem = n_cols - (nj - 1) * tn
    if rem == tn:
        body(tn)
        return
    pl.when(j < nj - 1)(lambda: body(tn))
    pl.when(j == nj - 1)(lambda: body(rem))


def _store_cols(o_ref, width, value):
    o_ref[:, 0:width] = value.astype(o_ref.dtype)
    if width < o_ref.shape[1]:
        o_ref[:, width:] = jnp.zeros((o_ref.shape[0], o_ref.shape[1] - width), o_ref.dtype)


def _mm_in(a, w, tm, tn, col_tile=lambda j: j):
    M, K = a.shape
    N = w.shape[1]
    nj = pl.cdiv(N, tn)
    return pl.pallas_call(
        functools.partial(_mm_in_kernel, nj=nj, tn=tn, n_cols=N),
        grid=(nj, M // tm),
        in_specs=[pl.BlockSpec((tm, K), lambda j, i: (i, 0)), pl.BlockSpec(memory_space=pl.ANY)],
        out_specs=pl.BlockSpec((tm, tn), lambda j, i: (i, col_tile(j))),
        out_shape=jax.ShapeDtypeStruct((M, nj * tn), F32),
        scratch_shapes=_weight_scratch((K,), tn),
        compiler_params=_params(("arbitrary", "arbitrary"),
                                _vmem_limit(tm * K * 2, tm * tn * 4, temps=K * tn * 6 + 2 * tm * tn * 4)),
        name="mm_in",
    )(a, w)


def _mm_mix_kernel(a1_ref, a2_ref, z0_ref, z1_ref, bg_ref, w1_hbm, w2_hbm, o_ref,
                     l1_ref, l2_ref, b1_ref, b2_ref, s1, s2, *, nj, tn, n_cols):
    j, i = pl.program_id(0), pl.program_id(1)
    _stream_weights([(w1_hbm, l1_ref, b1_ref, s1), (w2_hbm, l2_ref, b2_ref, s2)], j, i, nj, tn, n_cols)
    g0 = jax.nn.sigmoid(z0_ref[...] + bg_ref[0:1, :])
    g1 = jax.nn.sigmoid(z1_ref[...] + bg_ref[1:2, :])
    o_ref[...] = (g0 * _dot(a1_ref[...], b1_ref[...]) + g1 * _dot(a2_ref[...], b2_ref[...])).astype(o_ref.dtype)


def _mm_mix(o_att, w_pa, o_rwkv, w_pr, z, b_gate, S, D, tm, tn):
    K1, K2 = o_att.shape[1], o_rwkv.shape[1]
    assert D % tn == 0
    nj = D // tn
    return pl.pallas_call(
        functools.partial(_mm_mix_kernel, nj=nj, tn=tn, n_cols=D),
        grid=(nj, S // tm),
        in_specs=[pl.BlockSpec((tm, K1), lambda j, i: (i, 0)), pl.BlockSpec((tm, K2), lambda j, i: (i, 0)),
                  pl.BlockSpec((tm, tn), lambda j, i: (i, j)), pl.BlockSpec((tm, tn), lambda j, i: (i, nj + j)),
                  pl.BlockSpec((2, tn), lambda j, i: (0, j)),
                  pl.BlockSpec(memory_space=pl.ANY), pl.BlockSpec(memory_space=pl.ANY)],
        out_specs=pl.BlockSpec((tm, tn), lambda j, i: (i, j)),
        out_shape=jax.ShapeDtypeStruct((S, D), BF16),
        scratch_shapes=_weight_scratch((K1, K2), tn),
        compiler_params=_params(("arbitrary", "arbitrary"),
                                _vmem_limit(tm * (K1 + K2) * 2, 2 * tm * tn * 4, tm * tn * 2,
                                            temps=(K1 + K2) * tn * 6 + 3 * tm * tn * 4)),
        name="mm_mix",
    )(o_att, o_rwkv, z, z, b_gate, w_pa, w_pr)


def _mm_out_kernel(a_ref, r_ref, g_ref, w_hbm, o_ref, hg_ref, ss_ref, land_ref, bw_ref, sem, *, nj, tn, n_cols):
    j, i = pl.program_id(0), pl.program_id(1)
    _stream_weights([(w_hbm, land_ref, bw_ref, sem)], j, i, nj, tn, n_cols)
    h = r_ref[...] + _dot(a_ref[...], bw_ref[...])
    o_ref[...] = h
    hg_ref[...] = (h * g_ref[...]).astype(hg_ref.dtype)
    ss_ref[...] = jnp.broadcast_to(jnp.sum(h * h, axis=1, keepdims=True), ss_ref.shape)


def _mm_out(a, w, res, gain, tm, tn):
    M, K = a.shape
    N = w.shape[1]
    assert N % tn == 0
    nj = N // tn
    tile = pl.BlockSpec((tm, tn), lambda j, i: (i, j))
    return pl.pallas_call(
        functools.partial(_mm_out_kernel, nj=nj, tn=tn, n_cols=N),
        grid=(nj, M // tm),
        in_specs=[pl.BlockSpec((tm, K), lambda j, i: (i, 0)), tile, pl.BlockSpec((1, tn), lambda j, i: (0, j)),
                  pl.BlockSpec(memory_space=pl.ANY)],
        out_specs=[tile, tile, pl.BlockSpec((tm, LANES), lambda j, i: (i, j))],
        out_shape=[jax.ShapeDtypeStruct((M, N), F32), jax.ShapeDtypeStruct((M, N), BF16),
                   jax.ShapeDtypeStruct((M, nj * LANES), F32)],
        scratch_shapes=_weight_scratch((K,), tn),
        compiler_params=_params(("arbitrary", "arbitrary"),
                                _vmem_limit(tm * K * 2, 2 * tm * tn * 4, tm * tn * 2, tm * LANES * 4,
                                            temps=K * tn * 6 + 2 * tm * tn * 4)),
        name="mm_out",
    )(a, res, gain.reshape(1, N), w)


def _mm_glu_kernel(a_ref, ss_ref, wg_hbm, wu_hbm, o_ref, lg_ref, lu_ref, bg_ref, bu_ref, sg, su,
                     *, nj, tn, n_cols, n_model):
    j, i = pl.program_id(0), pl.program_id(1)
    _stream_weights([(wg_hbm, lg_ref, bg_ref, sg), (wu_hbm, lu_ref, bu_ref, su)], j, i, nj, tn, n_cols)
    ss = ss_ref[:, 0:LANES]
    for t in range(1, ss_ref.shape[1] // LANES):
        ss = ss + ss_ref[:, t * LANES:(t + 1) * LANES]
    scale = lax.rsqrt(ss[:, 0:1] * (1.0 / n_model) + RMS_EPS)

    def body(wd):
        a = a_ref[...]
        gate = scale * _dot(a, bg_ref[:, 0:wd])
        up = scale * _dot(a, bu_ref[:, 0:wd])
        _store_cols(o_ref, wd, gate * jax.nn.sigmoid(gate) * up)

    _per_tile_width(j, nj, tn, n_cols, body)


def _mm_glu(a, ss, wg, wu, tm, tn):
    M, K = a.shape
    N = wg.shape[1]
    nj = pl.cdiv(N, tn)
    return pl.pallas_call(
        functools.partial(_mm_glu_kernel, nj=nj, tn=tn, n_cols=N, n_model=K),
        grid=(nj, M // tm),
        in_specs=[pl.BlockSpec((tm, K), lambda j, i: (i, 0)), pl.BlockSpec((tm, ss.shape[1]), lambda j, i: (i, 0)),
                  pl.BlockSpec(memory_space=pl.ANY), pl.BlockSpec(memory_space=pl.ANY)],
        out_specs=pl.BlockSpec((tm, tn), lambda j, i: (i, j)),
        out_shape=jax.ShapeDtypeStruct((M, N), BF16),
        scratch_shapes=_weight_scratch((K, K), tn),
        compiler_params=_params(("arbitrary", "arbitrary"),
                                _vmem_limit(tm * K * 2, tm * ss.shape[1] * 4, tm * tn * 2,
                                            temps=2 * K * tn * 6 + 6 * tm * tn * 4)),
        name="mm_glu",
    )(a, ss, wg, wu)


def _mm_res_kernel(a_ref, b_ref, r_ref, o_ref, *, nk):
    k = pl.program_id(2)
    if nk == 1:
        o_ref[...] = r_ref[...] + _dot(a_ref[...], b_ref[...])
    else:
        @pl.when(k == 0)
        def _():
            o_ref[...] = r_ref[...] + _dot(a_ref[...], b_ref[...])

        @pl.when(k > 0)
        def _():
            o_ref[...] = o_ref[...] + _dot(a_ref[...], b_ref[...])


def _mm_res(a, b, res, tm, tn, tk):
    M, K = a.shape
    N = b.shape[1]
    nk = K // tk
    return pl.pallas_call(
        functools.partial(_mm_res_kernel, nk=nk),
        grid=(M // tm, N // tn, nk),
        in_specs=[pl.BlockSpec((tm, tk), lambda i, j, k: (i, k)), pl.BlockSpec((tk, tn), lambda i, j, k: (k, j)),
                  pl.BlockSpec((tm, tn), lambda i, j, k: (i, j))],
        out_specs=pl.BlockSpec((tm, tn), lambda i, j, k: (i, j)),
        out_shape=jax.ShapeDtypeStruct((M, N), F32),
        compiler_params=_params(("arbitrary", "arbitrary", "arbitrary"),
                                _vmem_limit(tm * tk * 2, tk * tn * 2, 2 * tm * tn * 4, temps=tm * tn * 4)),
        name="mm_down",
    )(a, b, res)


NEG_BIG = -1e30


def _rope(t, cos, sin):
    half = ROPE_DIMS // 2
    lane = lax.broadcasted_iota(jnp.int32, t.shape, 1)
    up = jnp.concatenate([t[:, half:], t[:, :half]], axis=1)
    down = jnp.concatenate([t[:, HEAD_DIM - half:], t[:, :HEAD_DIM - half]], axis=1)
    return t * cos + jnp.where(lane < half, up, down) * sin


def _attn_kernel(sink_ref, q_ref, k_ref, v_ref, cos_ref, sin_ref, o_ref, kb_ref, vb_ref, *, nb, group, qblocks):
    h, i = pl.program_id(0), pl.program_id(1)
    lp = k_ref.shape[0]

    @pl.when(i == 0)
    def _():
        def body(c, carry):
            sl = pl.ds(pl.multiple_of(c * BLOCK, BLOCK), BLOCK)
            kb_ref[sl, :] = _rope(k_ref[sl, :], cos_ref[sl, :], sin_ref[sl, :]).astype(BF16)
            vb_ref[sl, :] = v_ref[sl, :].astype(BF16)
            return carry
        lax.fori_loop(0, lp // BLOCK, body, 0)

    zpad = jnp.zeros((BLOCK - N_META, HEAD_DIM), BF16)
    k_meta = jnp.concatenate([kb_ref[lp - N_META:lp, :], zpad], axis=0)
    v_meta = jnp.concatenate([vb_ref[lp - N_META:lp, :], zpad], axis=0)
    rows = lax.broadcasted_iota(jnp.int32, (BLOCK, 4 * BLOCK), 0)
    cols = lax.broadcasted_iota(jnp.int32, (BLOCK, 4 * BLOCK), 1)
    is_meta = (cols >= 3 * BLOCK) & (cols < 3 * BLOCK + N_META)
    scale = HEAD_DIM ** -0.5
    k_all, v_all, bias, q, sink = [], [], [], [], []
    for b in range(qblocks):
        qi = i * qblocks + b
        kb0 = jnp.clip(qi - 1, 0, nb - 3)
        win = pl.ds(pl.multiple_of(kb0 * BLOCK, BLOCK), 3 * BLOCK)
        k_b = jnp.concatenate([kb_ref[win, :], k_meta], axis=0)
        v_b = jnp.concatenate([vb_ref[win, :], v_meta], axis=0)
        rel = cols - rows + (kb0 - qi) * BLOCK
        ok = ((cols < 3 * BLOCK) & (jnp.abs(rel) <= WINDOW)) | is_meta
        bias_b = jnp.where(ok, 0.0, NEG_BIG).astype(F32)
        qrows = pl.ds(pl.multiple_of(qi * BLOCK, BLOCK), BLOCK)
        cos_q, sin_q = cos_ref[qrows, :], sin_ref[qrows, :]
        for g in range(group):
            k_all.append(k_b)
            v_all.append(v_b)
            bias.append(bias_b)
            q.append(_rope(q_ref[b * BLOCK:(b + 1) * BLOCK, g * HEAD_DIM:(g + 1) * HEAD_DIM],
                           cos_q, sin_q).astype(BF16))
            sink.append(sink_ref[h * group + g])
    n = range(len(q))
    s = [_dot(q[c], k_all[c], NT) * scale + bias[c] for c in n]
    m = [jnp.maximum(jnp.max(s[c], axis=-1, keepdims=True), sink[c]) for c in n]
    p = [jnp.exp(s[c] - m[c]) for c in n]
    denom = [jnp.sum(p[c], axis=-1, keepdims=True) + jnp.exp(sink[c] - m[c]) for c in n]
    o = [_dot(p[c].astype(BF16), v_all[c]) / denom[c] for c in n]
    for c in n:
        b, g = divmod(c, group)
        o_ref[b * BLOCK:(b + 1) * BLOCK, g * HEAD_DIM:(g + 1) * HEAD_DIM] = o[c].astype(o_ref.dtype)


def _attention(z, cos_t, sin_t, sink, S, n_heads, n_kv, off_q):
    Lp = z.shape[0]
    group = n_heads // n_kv
    nb = S // BLOCK
    assert nb >= 3
    qw = group * HEAD_DIM
    kcol = off_q // HEAD_DIM + n_heads
    vcol = kcol + n_kv
    col_bytes = Lp * HEAD_DIM * 4
    qblocks = _pick(nb, (8, 4, 2, 1))
    tq = qblocks * BLOCK
    return pl.pallas_call(
        functools.partial(_attn_kernel, nb=nb, group=group, qblocks=qblocks),
        grid_spec=pltpu.PrefetchScalarGridSpec(
            num_scalar_prefetch=1,
            grid=(n_kv, nb // qblocks),
            in_specs=[pl.BlockSpec((tq, qw), lambda h, i, s: (i, off_q // qw + h)),
                      pl.BlockSpec((Lp, HEAD_DIM), lambda h, i, s: (0, kcol + h)),
                      pl.BlockSpec((Lp, HEAD_DIM), lambda h, i, s: (0, vcol + h)),
                      pl.BlockSpec((Lp, HEAD_DIM), lambda h, i, s: (0, 0)),
                      pl.BlockSpec((Lp, HEAD_DIM), lambda h, i, s: (0, 0))],
            out_specs=pl.BlockSpec((tq, qw), lambda h, i, s: (i, h)),
            scratch_shapes=[pltpu.VMEM((Lp, HEAD_DIM), BF16), pltpu.VMEM((Lp, HEAD_DIM), BF16)]),
        out_shape=jax.ShapeDtypeStruct((S, n_heads * HEAD_DIM), BF16),
        compiler_params=_params(("arbitrary", "arbitrary"),
                                _vmem_limit(tq * qw * 4, 4 * col_bytes, tq * qw * 2,
                                            temps=col_bytes + 8 * group * tq * 512 * 4)),
        name="attention",
    )(sink.astype(F32), z, z, z, cos_t, sin_t)


def _split3(x):
    hi = x.astype(BF16)
    r1 = x - hi.astype(F32)
    mid = r1.astype(BF16)
    lo = (r1 - mid.astype(F32)).astype(BF16)
    return hi, mid, lo


def _head_sum(x, ones_bd):
    outs = []
    for s in range(x.shape[1] // LANES):
        parts = _split3(x[:, s * LANES:(s + 1) * LANES])
        outs.append(sum(_dot(p, ones_bd) for p in parts))
    return outs[0] if len(outs) == 1 else jnp.concatenate(outs, axis=1)


def _prep_kernel(r_ref, rp_ref, rn_ref, k_ref, kp_ref, kn_ref, v_ref, vp_ref, vn_ref,
                 lo_ref, lop_ref, lon_ref, gd_ref, gdp_ref, gdn_ref,
                 pc_ref, pl_ref, wl_ref, g2_ref, ones_ref,
                 ro_ref, vo_ref, kko_ref, lwf_ref, lwb_ref, kdf_ref, kdb_ref, bdf_ref, bdb_ref, go_ref, bo_ref,
                 *, n_tiles, tr, lora_tanh_cols):
    i = pl.program_id(1)

    def shifted(main_ref, prev_ref, next_ref, mup, mun):
        x = main_ref[...]
        row8 = lax.broadcasted_iota(jnp.int32, (8, x.shape[1]), 0)
        down = pltpu.roll(x, 1, 0)
        up = pltpu.roll(x, tr - 1, 0)
        prev = jnp.concatenate([jnp.where(row8 == 0, prev_ref[7:8, :], down[0:8]), down[8:]], axis=0)
        nxt = jnp.concatenate([up[:tr - 8], jnp.where(row8 == 7, next_ref[0:1, :], up[tr - 8:])], axis=0)
        return x + mup * (prev - x) + mun * (nxt - x)

    r = shifted(r_ref, rp_ref, rn_ref, pc_ref[0:1, :], pc_ref[1:2, :])
    k = shifted(k_ref, kp_ref, kn_ref, pc_ref[2:3, :], pc_ref[3:4, :])
    v = shifted(v_ref, vp_ref, vn_ref, pc_ref[4:5, :], pc_ref[5:6, :])
    lo = shifted(lo_ref, lop_ref, lon_ref, pl_ref[0:1, 0:LORA_PAD], pl_ref[1:2, 0:LORA_PAD])
    gd = shifted(gd_ref, gdp_ref, gdn_ref, pl_ref[0:1, LORA_PAD:], pl_ref[1:2, LORA_PAD:])

    row = lax.broadcasted_iota(jnp.int32, (tr, 1), 0)
    valid = jnp.where((i < n_tiles - 1) | (row >= tr - N_META), 1.0, 0.0).astype(F32)

    lane = lax.broadcasted_iota(jnp.int32, lo.shape, 1)
    lo_act = jnp.where(lane < lora_tanh_cols, jnp.tanh(lo), lo).astype(BF16)
    dec_f = _dot(lo_act, wl_ref[0])
    dec_b = _dot(lo_act, wl_ref[1])
    apre_f = _dot(lo_act, wl_ref[2])
    apre_b = _dot(lo_act, wl_ref[3])
    g = _dot(jax.nn.sigmoid(lo).astype(BF16), g2_ref[0]) + _dot(jax.nn.sigmoid(gd).astype(BF16), g2_ref[1])

    def log_decay(dec, w0):
        return -math.exp(-0.5) * jax.nn.sigmoid(w0 + dec)

    a_f = jax.nn.sigmoid(pc_ref[8:9, :] + apre_f)
    a_b = jax.nn.sigmoid(pc_ref[9:10, :] + apre_b)
    kk = k * pc_ref[10:11, :]
    ss = _head_sum(kk * kk, ones_ref[...])
    kk = kk / jnp.maximum(jnp.sqrt(ss), 1e-12) * valid
    k_a = pc_ref[11:12, :]
    kv = k * valid

    rv = r * valid
    vv = v * valid
    kd_f = kv * (1.0 + (a_f - 1.0) * k_a)
    kd_b = kv * (1.0 + (a_b - 1.0) * k_a)
    ro_ref[...] = rv
    vo_ref[...] = vv
    kko_ref[...] = kk
    lwf_ref[...] = log_decay(dec_f, pc_ref[6:7, :])
    lwb_ref[...] = log_decay(dec_b, pc_ref[7:8, :])
    kdf_ref[...] = kd_f
    kdb_ref[...] = kd_b
    bdf_ref[...] = kk * a_f
    bdb_ref[...] = kk * a_b
    go_ref[...] = g
    bo_ref[...] = _head_sum(rv * (kd_f + kd_b) * pc_ref[12:13, :], ones_ref[...]) * vv


def _rwkv_prep(z, off_r, off_lo, C, pc, plo, wl, g2, ones_bd, lora_tanh_cols):
    Lp = z.shape[0]
    tr = BLOCK
    ct = next(c for c in (2048, 1024, 512) if C % c == 0 and off_r % c == 0)
    n_tiles = Lp // tr
    n8 = Lp // 8
    once = pl.Buffered(1) if ct == C else None
    off_gd = off_lo + LORA_PAD
    assert off_r % ct == 0 and off_lo % LORA_PAD == 0

    def seg_specs(off, w, with_c):
        cb = off // w

        def col(c):
            return cb + c if with_c else cb
        return [pl.BlockSpec((tr, w), lambda c, i: (i, col(c))),
                pl.BlockSpec((8, w), lambda c, i: ((i * (tr // 8) + n8 - 1) % n8, col(c))),
                pl.BlockSpec((8, w), lambda c, i: (((i + 1) * (tr // 8)) % n8, col(c)))]

    in_specs = (seg_specs(off_r, ct, True) + seg_specs(off_r + C, ct, True) + seg_specs(off_r + 2 * C, ct, True)
                + seg_specs(off_lo, LORA_PAD, False) + seg_specs(off_gd, LORA_PAD, False)
                + [pl.BlockSpec((16, ct), lambda c, i: (0, c)),
                   pl.BlockSpec((8, 2 * LORA_PAD), lambda c, i: (0, 0)),
                   pl.BlockSpec((4, LORA_PAD, ct), lambda c, i: (0, 0, c), pipeline_mode=once),
                   pl.BlockSpec((2, LORA_PAD, ct), lambda c, i: (0, 0, c), pipeline_mode=once),
                   pl.BlockSpec((LANES, LANES), lambda c, i: (0, 0))])
    out_spec = pl.BlockSpec((tr, ct), lambda c, i: (i, c))
    out_sds = jax.ShapeDtypeStruct((Lp, C), F32)
    return pl.pallas_call(
        functools.partial(_prep_kernel, n_tiles=n_tiles, tr=tr, lora_tanh_cols=lora_tanh_cols),
        grid=(C // ct, n_tiles),
        in_specs=in_specs,
        out_specs=[out_spec] * 11,
        out_shape=[out_sds] * 11,
        compiler_params=_params(("arbitrary", "arbitrary"),
                                _vmem_limit(3 * tr * ct * 4, 2 * tr * LORA_PAD * 4, 4 * LORA_PAD * ct * 2,
                                            2 * LORA_PAD * ct * 2, 11 * tr * ct * 4, temps=24 * tr * ct * 4)),
        name="rwkv_prep",
    )(*([z] * 15), pc, plo, wl, g2, ones_bd)


SCAN_PAIRS = 16


def _scan_consts():
    C = CHUNK
    row = lax.broadcasted_iota(jnp.int32, (C, 2 * C), 0)
    col = lax.broadcasted_iota(jnp.int32, (C, 2 * C), 1) % C
    rr = lax.broadcasted_iota(jnp.int32, (LANES, LANES), 0)
    cc = lax.broadcasted_iota(jnp.int32, (LANES, LANES), 1)
    return dict(
        row=lax.broadcasted_iota(jnp.int32, (C, LANES), 0),
        head0=lax.broadcasted_iota(jnp.int32, (C, LANES), 1) < RWKV_HEAD,
        strict_f=col < row, strict_b=col > row, incl_f=col <= row, incl_b=col >= row,
        eye_sbs=jnp.where(col == row, 1.0, 0.0).astype(F32),
        same_head=(rr // RWKV_HEAD) == (cc // RWKV_HEAD), eye=rr == cc)


def _scan_chains(chains, cst):
    C = CHUNK
    head0, same, eye = cst["head0"], cst["same_head"], cst["eye"]
    n = range(len(chains))
    rev = [c[7] for c in chains]
    r, v, al, lw, k, be, h = ([c[i] for c in chains] for i in range(7))
    strict = [cst["strict_b" if x else "strict_f"] for x in rev]
    incl = [cst["incl_b" if x else "incl_f"] for x in rev]

    def bd(x):
        xb = x.astype(BF16)
        zero = jnp.zeros_like(xb)
        return jnp.concatenate([jnp.where(head0, xb, zero), jnp.where(head0, zero, xb)], axis=0)

    def mm(a, b_bf16, dims=NN):
        return _dot(a.astype(BF16), b_bf16, dims)

    row = cst["row"]
    cl = list(lw)
    sh = 1
    while sh < C:
        cl = [cl[i] + (jnp.where(row < C - sh, pltpu.roll(cl[i], C - sh, 0), 0.0) if rev[i] else
                       jnp.where(row >= sh, pltpu.roll(cl[i], sh, 0), 0.0)) for i in n]
        sh *= 2
    total = [cl[i][0:1, :] if rev[i] else cl[i][C - 1:C, :] for i in n]
    a_t = [al[i] * jnp.exp(cl[i] - lw[i]) for i in n]
    r_t = [r[i] * jnp.exp(cl[i]) for i in n]
    w_inv = [jnp.exp(-cl[i]) for i in n]
    w_rest = [jnp.exp(total[i] - cl[i]) for i in n]
    bd_v = [bd(v[i]) for i in n]
    sc = [mm(jnp.concatenate([a_t[i], r_t[i]], axis=0),
             jnp.concatenate([bd(k[i] * w_inv[i]), bd(be[i] * w_inv[i])], axis=0), NT) for i in n]
    a_ak = [jnp.where(strict[i], sc[i][0:C, 0:2 * C], 0.0) for i in n]
    n_ab = [jnp.where(strict[i], sc[i][0:C, 2 * C:], 0.0) for i in n]
    a_rk = [jnp.where(incl[i], sc[i][C:, 0:2 * C], 0.0) for i in n]
    a_rb = [jnp.where(incl[i], sc[i][C:, 2 * C:], 0.0) for i in n]

    t_inv = [cst["eye_sbs"] - n_ab[i] for i in n]
    sq = [mm(n_ab[i], bd(n_ab[i])) for i in n]
    n_round = 2
    while (1 << n_round) < C:
        both = [mm(jnp.concatenate([t_inv[i], sq[i]], axis=0), bd(sq[i])) for i in n]
        t_inv = [t_inv[i] + both[i][0:C] for i in n]
        sq = [both[i][C:] for i in n]
        n_round += 1
    t_inv = [t_inv[i] + mm(t_inv[i], bd(sq[i])) for i in n]

    akv = [mm(jnp.concatenate([a_ak[i], a_rk[i]], axis=0), bd_v[i]) for i in n]
    tp = [mm(t_inv[i], jnp.concatenate([bd(akv[i][0:C]), bd(a_t[i])], axis=1)) for i in n]
    p0 = [tp[i][:, 0:LANES] for i in n]
    at = [tp[i][:, LANES:] for i in n]
    rb = [mm(a_rb[i], jnp.concatenate([bd(p0[i]), bd(at[i])], axis=1)) for i in n]
    y0 = [akv[i][C:] - rb[i][:, 0:LANES] for i in n]
    rh = [r_t[i] - rb[i][:, LANES:] for i in n]
    bt = [mm(be[i] * w_rest[i], jnp.concatenate([at[i], p0[i]], axis=1).astype(BF16), TN) for i in n]
    kv = [mm(k[i] * w_rest[i], v[i].astype(BF16), TN) for i in n]
    m_mat = [jnp.where(same, jnp.where(eye, jnp.broadcast_to(jnp.exp(total[i]), (LANES, LANES)), 0.0)
                       - bt[i][:, 0:LANES], 0.0) for i in n]
    g_mat = [jnp.where(same, kv[i] - bt[i][:, LANES:], 0.0) for i in n]
    yh = [mm(jnp.concatenate([rh[i], m_mat[i]], axis=0), h[i].astype(BF16)) for i in n]
    return [(yh[i][0:C] + y0[i], yh[i][C:] + g_mat[i]) for i in n]


def _scan_kernel(rf_ref, vf_ref, af_ref, lwf_ref, kf_ref, bf_ref,
                 rb_ref, vb_ref, ab_ref, lwb_ref, kb_ref, bb_ref, wd_ref,
                 yf_ref, yb_ref, wdo_ref, hf_ref, hb_ref, *, pairs):
    @pl.when(pl.program_id(1) == 0)
    def _():
        hf_ref[...] = jnp.zeros(hf_ref.shape, F32)
        hb_ref[...] = jnp.zeros(hb_ref.shape, F32)

    wdo_ref[...] = wd_ref[...].astype(wdo_ref.dtype)

    cst = _scan_consts()
    chains = []
    for g in range(pairs):
        sl = slice(g * LANES, (g + 1) * LANES)
        chains.append((rf_ref[:, sl], vf_ref[:, sl], af_ref[:, sl], lwf_ref[:, sl], kf_ref[:, sl], bf_ref[:, sl],
                       hf_ref[g], False))
        chains.append((rb_ref[:, sl], vb_ref[:, sl], ab_ref[:, sl], lwb_ref[:, sl], kb_ref[:, sl], bb_ref[:, sl],
                       hb_ref[g], True))
    res = _scan_chains(chains, cst)
    for g in range(pairs):
        sl = slice(g * LANES, (g + 1) * LANES)
        yf_ref[:, sl], hf_ref[g] = res[2 * g]
        yb_ref[:, sl], hb_ref[g] = res[2 * g + 1]


def _rwkv_scan(r, v, kk, lw_f, lw_b, kd_f, kd_b, bd_f, bd_b, w_cast, S):
    Lp, C = r.shape
    n_real = S // CHUNK
    n_chunks = Lp // CHUNK
    steps = n_real + 1
    pairs = _pick(C // LANES, (SCAN_PAIRS, 2, 1))
    w = pairs * LANES
    total_steps = (C // w) * steps
    wr, wc = w_cast.shape
    cast_rows = next(rb for rb in (128, 256, 512, 1024, wr) if wr % rb == 0 and wr // rb <= total_steps)
    n_cast = wr // cast_rows
    cast_spec = pl.BlockSpec((cast_rows, wc), lambda p, s: (jnp.minimum(p * steps + s, n_cast - 1), 0))

    def fwd(p, s):
        return ((s + n_real + 1) % n_chunks, p)

    def bwd(p, s):
        return (n_real - s, p)

    spec_f = pl.BlockSpec((CHUNK, w), fwd)
    spec_b = pl.BlockSpec((CHUNK, w), bwd)
    out_sds = jax.ShapeDtypeStruct((S + CHUNK, C), F32)
    out_f = pl.BlockSpec((CHUNK, w), lambda p, s: ((s + n_real) % (n_real + 1), p))
    return pl.pallas_call(
        functools.partial(_scan_kernel, pairs=pairs),
        grid=(C // w, steps),
        in_specs=[spec_f] * 6 + [spec_b] * 6 + [cast_spec],
        out_specs=[out_f, spec_b, cast_spec],
        out_shape=[out_sds, out_sds, jax.ShapeDtypeStruct((wr, wc), BF16)],
        scratch_shapes=[pltpu.VMEM((pairs, LANES, LANES), F32), pltpu.VMEM((pairs, LANES, LANES), F32)],
        compiler_params=_params(("arbitrary", "arbitrary"),
                                _vmem_limit(14 * CHUNK * w * 4, cast_rows * wc * 6, temps=16 << 20)),
        name="rwkv_scan",
    )(r, v, kk, lw_f, kd_f, bd_f, r, v, kk, lw_b, kd_b, bd_b, w_cast)


def _post_kernel(yf_ref, yb_ref, bonus_ref, g_ref, pq_ref, ones_ref, o_ref):
    ones_bd = ones_ref[...]
    inv_n = 1.0 / RWKV_HEAD
    y = yf_ref[...] + yb_ref[...]
    mean = _head_sum(y, ones_bd) * inv_n
    yc = y - mean
    var = _head_sum(yc * yc, ones_bd) * inv_n
    yn = yc * lax.rsqrt(var + GN_EPS) * pq_ref[0:1, :] + pq_ref[1:2, :]
    o_ref[...] = ((yn + bonus_ref[...]) * g_ref[...]).astype(o_ref.dtype)


def _rwkv_post(y_f, y_b, bonus, g, pq, ones_bd, S):
    C = g.shape[1]
    tr = 512
    ct = _pick(C, (1024, 512))
    spec = pl.BlockSpec((tr, ct), lambda i, c: (i, c))
    return pl.pallas_call(
        _post_kernel,
        grid=(S // tr, C // ct),
        in_specs=[spec] * 4 + [pl.BlockSpec((8, ct), lambda i, c: (0, c)),
                               pl.BlockSpec((LANES, LANES), lambda i, c: (0, 0))],
        out_specs=spec,
        out_shape=jax.ShapeDtypeStruct((S, C), BF16),
        compiler_params=_params(("arbitrary", "arbitrary"),
                                _vmem_limit(4 * tr * ct * 4, tr * ct * 2, temps=16 * tr * ct * 4)),
        name="rwkv_post",
    )(y_f, y_b, bonus, g, pq, ones_bd)


IN_TN = 1024


def _matmul_tiles(S, D, F, Lp):
    return dict(
        in_tm=_pick(Lp, (1040, 640)),
        proj_tm=512, proj_tn=_pick(D, (1024, 512)),
        ffn_tm=_pick(S, (1024, 512)), glu_tn=512,
        down_tn=512, down_tk=F // 2,
    )

def kernel(x, meta_tokens, norm_mix_w, w_in, b_gate, mu_prev, mu_next, dec_w0, dec_w2, iclr_a0, iclr_a2,
           gate_w2, k_k, k_a, r_k, ln_x_w, ln_x_b, attn_sink, w_proj_attn, w_proj_rwkv, w_out, norm_ffn_w,
           w_ffn_gate, w_ffn_up, w_ffn_down, norm_final_w):
    B, S, D = x.shape
    assert B == 1 and norm_mix_w.shape[0] == 1, "one sequence, one layer"
    C = k_k.shape[-1]
    AW = w_proj_attn.shape[1]
    n_heads = attn_sink.shape[-1]
    DL, IL, GL = dec_w2.shape[2], iclr_a2.shape[2], gate_w2.shape[1]
    shift_w = mu_prev.shape[-1]
    KVW = (w_in.shape[-1] - 2 * D - AW - shift_w) // 2
    n_kv = KVW // HEAD_DIM
    F = w_ffn_gate.shape[-1]
    lora_w = 2 * DL + 2 * IL
    assert shift_w == 3 * C + lora_w + GL and lora_w <= LORA_PAD and GL <= LORA_PAD
    assert S % BLOCK == 0 and C % LANES == 0 and n_heads * HEAD_DIM == AW
    Lp = S + PAD_ROWS

    off_q = 2 * D
    off_r = off_q + AW + 2 * KVW
    off_lo = off_r + 3 * C
    n_in_pad = -(-w_in.shape[-1] // IN_TN) * IN_TN
    assert n_in_pad >= off_lo + 2 * LORA_PAD and off_lo % LORA_PAD == 0
    t_r, t_lo = off_r // IN_TN, off_lo // IN_TN
    if off_r % IN_TN == 0 and off_lo % IN_TN == 0 and n_in_pad == off_lo + IN_TN:
        z_off_lo, z_off_r = off_r, off_r + IN_TN

        def z_col_tile(j):
            return jnp.where(j == t_lo, t_r, jnp.where(j >= t_r, j + 1, j))
    else:
        z_off_lo, z_off_r = off_lo, off_r

        def z_col_tile(j):
            return j
    low_w = lora_w + GL
    gl_a = LORA_PAD - lora_w

    mu_p, mu_n = mu_prev[0], mu_next[0]
    zc = jnp.zeros((C,), F32)
    pc = jnp.stack([mu_p[:C], mu_n[:C], mu_p[C:2 * C], mu_n[C:2 * C], mu_p[2 * C:3 * C], mu_n[2 * C:3 * C],
                    dec_w0[0, 0], dec_w0[0, 1], iclr_a0[0, 0], iclr_a0[0, 1], k_k[0], k_a[0], r_k[0], zc, zc, zc])
    plo = jnp.zeros((8, 2 * LORA_PAD), F32).at[0, :low_w].set(mu_p[3 * C:]).at[1, :low_w].set(mu_n[3 * C:])
    wl = jnp.zeros((4, LORA_PAD, C), F32)
    wl = wl.at[0, 0:DL].set(dec_w2[0, 0]).at[1, DL:2 * DL].set(dec_w2[0, 1])
    wl = wl.at[2, 2 * DL:2 * DL + IL].set(iclr_a2[0, 0]).at[3, 2 * DL + IL:lora_w].set(iclr_a2[0, 1])
    wl = wl.astype(BF16)
    g2 = jnp.zeros((2, LORA_PAD, C), F32).at[0, lora_w:].set(gate_w2[0, :gl_a]).at[1, :GL - gl_a].set(
        gate_w2[0, gl_a:]).astype(BF16)
    pq = jnp.zeros((8, C), F32).at[0].set(ln_x_w[0]).at[1].set(ln_x_b[0])
    lane = jnp.arange(LANES)
    ones_bd = (lane[:, None] // RWKV_HEAD == lane[None, :] // RWKV_HEAD).astype(BF16)


    pos = jnp.concatenate([jnp.arange(N_META, N_META + S), jnp.zeros((PAD_ROWS - N_META,), jnp.int32),
                           jnp.arange(N_META)]).astype(F32)
    inv = ROPE_THETA ** (-jnp.arange(0, ROPE_DIMS, 2, dtype=F32) / ROPE_DIMS)
    ang = pos[:, None] * inv[None, :]
    ones_rest = jnp.ones((Lp, HEAD_DIM - ROPE_DIMS), F32)
    cos_t = jnp.concatenate([jnp.cos(ang), jnp.cos(ang), ones_rest], axis=1)
    sin_t = jnp.concatenate([-jnp.sin(ang), jnp.sin(ang), 0.0 * ones_rest], axis=1)

    t = _matmul_tiles(S, D, F, Lp)
    x2 = x[0]
    h_ext = _norm_ext(x2, meta_tokens.astype(x.dtype), norm_mix_w[0])
    z = _mm_in(h_ext, w_in[0], t["in_tm"], IN_TN, z_col_tile)
    o_att = _attention(z, cos_t, sin_t, attn_sink[0], S, n_heads, n_kv, off_q)
    r_s, v_s, kk, lw_f, lw_b, kd_f, kd_b, bd_f, bd_b, g, bonus = _rwkv_prep(
        z, z_off_r, z_off_lo, C, pc, plo, wl, g2, ones_bd, 2 * DL)
    y_f, y_b, w_d = _rwkv_scan(r_s, v_s, kk, lw_f, lw_b, kd_f, kd_b, bd_f, bd_b, w_ffn_down[0], S)
    o_rwkv = _rwkv_post(y_f, y_b, bonus, g, pq, ones_bd, S)
    mixed = _mm_mix(o_att, w_proj_attn[0], o_rwkv, w_proj_rwkv[0], z, b_gate[0], S, D, t["proj_tm"], t["proj_tn"])
    h1, h1g, h1ss = _mm_out(mixed, w_out[0], x2, norm_ffn_w[0], t["proj_tm"], t["proj_tn"])
    act = _mm_glu(h1g, h1ss, w_ffn_gate[0], w_ffn_up[0], t["ffn_tm"], t["glu_tn"])
    h2 = _mm_res(act, w_d, h1, t["ffn_tm"], t["down_tn"], t["down_tk"])
    y = _norm(h2, norm_final_w, x.dtype)
    return y[None]
```

```python
import functools
import math

import jax
import jax.numpy as jnp
from jax import lax
from jax.experimental import pallas as pl
from jax.experimental.pallas import tpu as pltpu

F32 = jnp.float32
BF16 = jnp.bfloat16

N_META = 16
HEAD_DIM = 128
WINDOW = 128
BLOCK = 128
ROPE_DIMS = HEAD_DIM // 4
ROPE_THETA = 500000.0
RWKV_HEAD = 64
RMS_EPS = 1e-6
GN_EPS = 64e-5
LANES = 128
PAD_ROWS = BLOCK
CHUNK = 64
LORA_PAD = 512
VMEM_PHYSICAL = 64 * 1024 * 1024

NN = (((1,), (0,)), ((), ()))
NT = (((1,), (1,)), ((), ()))
TN = (((0,), (0,)), ((), ()))


def _vmem_limit(*block_bytes, temps=0):
    need = 2 * sum(block_bytes) + temps + (4 << 20)
    return int(min(max(need, 16 << 20), VMEM_PHYSICAL - (4 << 20)))


def _params(sem, limit):
    return pltpu.CompilerParams(dimension_semantics=sem, vmem_limit_bytes=limit)


def _pick(n, cands):
    for c in cands:
        if n % c == 0:
            return c
    raise ValueError(f"no tile in {cands} divides {n}")


def _rms(x, w):
    return x * lax.rsqrt(jnp.mean(x * x, axis=-1, keepdims=True) + RMS_EPS) * w


def _norm_ext_kernel(x_ref, meta_ref, w_ref, o_ref, *, n_steps, real_last):
    i = pl.program_id(0)
    tr = o_ref.shape[0]

    @pl.when(i < n_steps - 1)
    def _():
        o_ref[...] = _rms(x_ref[...], w_ref[...]).astype(o_ref.dtype)

    @pl.when(i == n_steps - 1)
    def _():
        if real_last:
            o_ref[0:real_last, :] = _rms(x_ref[0:real_last, :], w_ref[...]).astype(o_ref.dtype)
        o_ref[real_last:tr - N_META, :] = jnp.zeros((tr - N_META - real_last, o_ref.shape[1]), o_ref.dtype)
        o_ref[tr - N_META:, :] = _rms(meta_ref[...], w_ref[...]).astype(o_ref.dtype)


def _norm_ext(x, meta, w):
    S, D = x.shape
    Lp = S + PAD_ROWS
    tr = _pick(Lp, (640, BLOCK))
    n_steps = Lp // tr
    real_last = S - (n_steps - 1) * tr
    last_x_blk = (S - 1) // tr
    return pl.pallas_call(
        functools.partial(_norm_ext_kernel, n_steps=n_steps, real_last=real_last),
        grid=(n_steps,),
        in_specs=[pl.BlockSpec((tr, D), lambda i: (jnp.minimum(i, last_x_blk), 0)),
                  pl.BlockSpec((N_META, D), lambda i: (0, 0)),
                  pl.BlockSpec((1, D), lambda i: (0, 0))],
        out_specs=pl.BlockSpec((tr, D), lambda i: (i, 0)),
        out_shape=jax.ShapeDtypeStruct((Lp, D), BF16),
        compiler_params=_params(("arbitrary",), _vmem_limit(tr * D * 4, tr * D * 2, temps=2 * tr * D * 4)),
        name="norm_ext",
    )(x, meta, w.reshape(1, D))


def _norm_kernel(x_ref, w_ref, o_ref):
    o_ref[...] = _rms(x_ref[...], w_ref[...]).astype(o_ref.dtype)


def _norm(x, w, out_dtype, tr=512):
    S, D = x.shape
    return pl.pallas_call(
        _norm_kernel,
        grid=(S // tr,),
        in_specs=[pl.BlockSpec((tr, D), lambda i: (i, 0)), pl.BlockSpec((1, D), lambda i: (0, 0))],
        out_specs=pl.BlockSpec((tr, D), lambda i: (i, 0)),
        out_shape=jax.ShapeDtypeStruct((S, D), out_dtype),
        compiler_params=_params(("arbitrary",), _vmem_limit(tr * D * 4, tr * D * 4, temps=2 * tr * D * 4)),
        name="norm",
    )(x, w.reshape(1, D))


def _dot(a, b, dims=NN):
    return lax.dot_general(a, b, dims, preferred_element_type=F32)


def _cast_resident(src_ref, dst_ref):
    K = src_ref.shape[0]
    rows = _pick(K, (256, 128))

    def body(c, carry):
        sl = pl.ds(pl.multiple_of(c * rows, rows), rows)
        dst_ref[sl, :] = src_ref[sl, :].astype(dst_ref.dtype)
        return carry

    lax.fori_loop(0, K // rows, body, 0)


def _weight_copy(w_hbm, land_ref, sem, jj, tn, width):
    return pltpu.make_async_copy(w_hbm.at[:, pl.ds(pl.multiple_of(jj * tn, LANES), width)],
                                 land_ref.at[:, pl.ds(0, width)], sem)


def _weight_copy_op(op, w_hbm, land_ref, sem, jj, tn, nj, n_cols):
    rem = n_cols - (nj - 1) * tn
    if rem == tn:
        getattr(_weight_copy(w_hbm, land_ref, sem, jj, tn, tn), op)()
        return

    @pl.when(jj < nj - 1)
    def _():
        getattr(_weight_copy(w_hbm, land_ref, sem, jj, tn, tn), op)()

    @pl.when(jj == nj - 1)
    def _():
        getattr(_weight_copy(w_hbm, land_ref, sem, jj, tn, rem), op)()


def _stream_weights(weights, j, i, nj, tn, n_cols):
    @pl.when(i == 0)
    def _():
        @pl.when(j == 0)
        def _():
            for w_hbm, land_ref, _, sem in weights:
                _weight_copy_op("start", w_hbm, land_ref, sem, j, tn, nj, n_cols)

        for w_hbm, land_ref, bw_ref, sem in weights:
            _weight_copy_op("wait", w_hbm, land_ref, sem, j, tn, nj, n_cols)
            _cast_resident(land_ref, bw_ref)

        @pl.when(j + 1 < nj)
        def _():
            for w_hbm, land_ref, _, sem in weights:
                _weight_copy_op("start", w_hbm, land_ref, sem, j + 1, tn, nj, n_cols)


def _weight_scratch(ks, tn):
    return ([pltpu.VMEM((k, tn), F32) for k in ks] + [pltpu.VMEM((k, tn), BF16) for k in ks]
            + [pltpu.SemaphoreType.DMA(()) for _ in ks])


def _mm_in_kernel(a_ref, w_hbm, o_ref, land_ref, bw_ref, sem, *, nj, tn, n_cols):
    j, i = pl.program_id(0), pl.program_id(1)
    _stream_weights([(w_hbm, land_ref, bw_ref, sem)], j, i, nj, tn, n_cols)
    _per_tile_width(j, nj, tn, n_cols, lambda wd: _store_cols(o_ref, wd, _dot(a_ref[...], bw_ref[:, 0:wd])))


def _per_tile_width(j, nj, tn, n_cols, body):
    rem = n_cols - (nj - 1) * tn
    if rem == tn:
        body(tn)
        return
    pl.when(j < nj - 1)(lambda: body(tn))
    pl.when(j == nj - 1)(lambda: body(rem))


def _store_cols(o_ref, width, value):
    o_ref[:, 0:width] = value.astype(o_ref.dtype)
    if width < o_ref.shape[1]:
        o_ref[:, width:] = jnp.zeros((o_ref.shape[0], o_ref.shape[1] - width), o_ref.dtype)


def _mm_in(a, w, tm, tn, col_tile=lambda j: j):
    M, K = a.shape
    N = w.shape[1]
    nj = pl.cdiv(N, tn)
    return pl.pallas_call(
        functools.partial(_mm_in_kernel, nj=nj, tn=tn, n_cols=N),
        grid=(nj, M // tm),
        in_specs=[pl.BlockSpec((tm, K), lambda j, i: (i, 0)), pl.BlockSpec(memory_space=pl.ANY)],
        out_specs=pl.BlockSpec((tm, tn), lambda j, i: (i, col_tile(j))),
        out_shape=jax.ShapeDtypeStruct((M, nj * tn), F32),
        scratch_shapes=_weight_scratch((K,), tn),
        compiler_params=_params(("arbitrary", "arbitrary"),
                                _vmem_limit(tm * K * 2, tm * tn * 4, temps=K * tn * 6 + 2 * tm * tn * 4)),
        name="mm_in",
    )(a, w)


def _mm_mix_kernel(a1_ref, a2_ref, z0_ref, z1_ref, bg_ref, w1_hbm, w2_hbm, o_ref,
                     l1_ref, l2_ref, b1_ref, b2_ref, s1, s2, *, nj, tn, n_cols):
    j, i = pl.program_id(0), pl.program_id(1)
    _stream_weights([(w1_hbm, l1_ref, b1_ref, s1), (w2_hbm, l2_ref, b2_ref, s2)], j, i, nj, tn, n_cols)
    g0 = jax.nn.sigmoid(z0_ref[...] + bg_ref[0:1, :])
    g1 = jax.nn.sigmoid(z1_ref[...] + bg_ref[1:2, :])
    o_ref[...] = (g0 * _dot(a1_ref[...], b1_ref[...]) + g1 * _dot(a2_ref[...], b2_ref[...])).astype(o_ref.dtype)


def _mm_mix(o_att, w_pa, o_rwkv, w_pr, z, b_gate, S, D, tm, tn):
    K1, K2 = o_att.shape[1], o_rwkv.shape[1]
    assert D % tn == 0
    nj = D // tn
    return pl.pallas_call(
        functools.partial(_mm_mix_kernel, nj=nj, tn=tn, n_cols=D),
        grid=(nj, S // tm),
        in_specs=[pl.BlockSpec((tm, K1), lambda j, i: (i, 0)), pl.BlockSpec((tm, K2), lambda j, i: (i, 0)),
                  pl.BlockSpec((tm, tn), lambda j, i: (i, j)), pl.BlockSpec((tm, tn), lambda j, i: (i, nj + j)),
                  pl.BlockSpec((2, tn), lambda j, i: (0, j)),
                  pl.BlockSpec(memory_space=pl.ANY), pl.BlockSpec(memory_space=pl.ANY)],
        out_specs=pl.BlockSpec((tm, tn), lambda j, i: (i, j)),
        out_shape=jax.ShapeDtypeStruct((S, D), BF16),
        scratch_shapes=_weight_scratch((K1, K2), tn),
        compiler_params=_params(("arbitrary", "arbitrary"),
                                _vmem_limit(tm * (K1 + K2) * 2, 2 * tm * tn * 4, tm * tn * 2,
                                            temps=(K1 + K2) * tn * 6 + 3 * tm * tn * 4)),
        name="mm_mix",
    )(o_att, o_rwkv, z, z, b_gate, w_pa, w_pr)


def _mm_out_kernel(a_ref, r_ref, g_ref, w_hbm, o_ref, hg_ref, ss_ref, land_ref, bw_ref, sem, *, nj, tn, n_cols):
    j, i = pl.program_id(0), pl.program_id(1)
    _stream_weights([(w_hbm, land_ref, bw_ref, sem)], j, i, nj, tn, n_cols)
    h = r_ref[...] + _dot(a_ref[...], bw_ref[...])
    o_ref[...] = h
    hg_ref[...] = (h * g_ref[...]).astype(hg_ref.dtype)
    ss_ref[...] = jnp.broadcast_to(jnp.sum(h * h, axis=1, keepdims=True), ss_ref.shape)


def _mm_out(a, w, res, gain, tm, tn):
    M, K = a.shape
    N = w.shape[1]
    assert N % tn == 0
    nj = N // tn
    tile = pl.BlockSpec((tm, tn), lambda j, i: (i, j))
    return pl.pallas_call(
        functools.partial(_mm_out_kernel, nj=nj, tn=tn, n_cols=N),
        grid=(nj, M // tm),
        in_specs=[pl.BlockSpec((tm, K), lambda j, i: (i, 0)), tile, pl.BlockSpec((1, tn), lambda j, i: (0, j)),
                  pl.BlockSpec(memory_space=pl.ANY)],
        out_specs=[tile, tile, pl.BlockSpec((tm, LANES), lambda j, i: (i, j))],
        out_shape=[jax.ShapeDtypeStruct((M, N), F32), jax.ShapeDtypeStruct((M, N), BF16),
                   jax.ShapeDtypeStruct((M, nj * LANES), F32)],
        scratch_shapes=_weight_scratch((K,), tn),
        compiler_params=_params(("arbitrary", "arbitrary"),
                                _vmem_limit(tm * K * 2, 2 * tm * tn * 4, tm * tn * 2, tm * LANES * 4,
                                            temps=K * tn * 6 + 2 * tm * tn * 4)),
        name="mm_out",
    )(a, res, gain.reshape(1, N), w)


def _mm_glu_kernel(a_ref, ss_ref, wg_hbm, wu_hbm, o_ref, lg_ref, lu_ref, bg_ref, bu_ref, sg, su,
                     *, nj, tn, n_cols, n_model):
    j, i = pl.program_id(0), pl.program_id(1)
    _stream_weights([(wg_hbm, lg_ref, bg_ref, sg), (wu_hbm, lu_ref, bu_ref, su)], j, i, nj, tn, n_cols)
    ss = ss_ref[:, 0:LANES]
    for t in range(1, ss_ref.shape[1] // LANES):
        ss = ss + ss_ref[:, t * LANES:(t + 1) * LANES]
    scale = lax.rsqrt(ss[:, 0:1] * (1.0 / n_model) + RMS_EPS)

    def body(wd):
        a = a_ref[...]
        gate = scale * _dot(a, bg_ref[:, 0:wd])
        up = scale * _dot(a, bu_ref[:, 0:wd])
        _store_cols(o_ref, wd, gate * jax.nn.sigmoid(gate) * up)

    _per_tile_width(j, nj, tn, n_cols, body)


def _mm_glu(a, ss, wg, wu, tm, tn):
    M, K = a.shape
    N = wg.shape[1]
    nj = pl.cdiv(N, tn)
    return pl.pallas_call(
        functools.partial(_mm_glu_kernel, nj=nj, tn=tn, n_cols=N, n_model=K),
        grid=(nj, M // tm),
        in_specs=[pl.BlockSpec((tm, K), lambda j, i: (i, 0)), pl.BlockSpec((tm, ss.shape[1]), lambda j, i: (i, 0)),
                  pl.BlockSpec(memory_space=pl.ANY), pl.BlockSpec(memory_space=pl.ANY)],
        out_specs=pl.BlockSpec((tm, tn), lambda j, i: (i, j)),
        out_shape=jax.ShapeDtypeStruct((M, nj * tn), BF16),
        scratch_shapes=_weight_scratch((K, K), tn),
        compiler_params=_params(("arbitrary", "arbitrary"),
                                _vmem_limit(tm * K * 2, tm * ss.shape[1] * 4, tm * tn * 2,
                                            temps=2 * K * tn * 6 + 6 * tm * tn * 4)),
        name="mm_glu",
    )(a, ss, wg, wu)


def _mm_res_kernel(a_ref, b_ref, r_ref, o_ref, *, nk):
    k = pl.program_id(2)
    if nk == 1:
        o_ref[...] = r_ref[...] + _dot(a_ref[...], b_ref[...])
    else:
        @pl.when(k == 0)
        def _():
            o_ref[...] = r_ref[...] + _dot(a_ref[...], b_ref[...])

        @pl.when(k > 0)
        def _():
            o_ref[...] = o_ref[...] + _dot(a_ref[...], b_ref[...])


def _mm_res(a, b, res, tm, tn, tk):
    M, K = a.shape
    N = b.shape[1]
    nk = K // tk
    return pl.pallas_call(
        functools.partial(_mm_res_kernel, nk=nk),
        grid=(M // tm, N // tn, nk),
        in_specs=[pl.BlockSpec((tm, tk), lambda i, j, k: (i, k)), pl.BlockSpec((tk, tn), lambda i, j, k: (k, j)),
                  pl.BlockSpec((tm, tn), lambda i, j, k: (i, j))],
        out_specs=pl.BlockSpec((tm, tn), lambda i, j, k: (i, j)),
        out_shape=jax.ShapeDtypeStruct((M, N), F32),
        compiler_params=_params(("arbitrary", "arbitrary", "arbitrary"),
                                _vmem_limit(tm * tk * 2, tk * tn * 2, 2 * tm * tn * 4, temps=tm * tn * 4)),
        name="mm_down",
    )(a, b, res)


NEG_BIG = -1e30


def _rope(t, cos, sin):
    half = ROPE_DIMS // 2
    lane = lax.broadcasted_iota(jnp.int32, t.shape, 1)
    up = jnp.concatenate([t[:, half:], t[:, :half]], axis=1)
    down = jnp.concatenate([t[:, HEAD_DIM - half:], t[:, :HEAD_DIM - half]], axis=1)
    return t * cos + jnp.where(lane < half, up, down) * sin


def _attn_kernel(sink_ref, q_ref, k_ref, v_ref, cos_ref, sin_ref, o_ref, kb_ref, vb_ref, *, nb, group, qblocks):
    h, i = pl.program_id(0), pl.program_id(1)
    lp = k_ref.shape[0]

    @pl.when(i == 0)
    def _():
        def body(c, carry):
            sl = pl.ds(pl.multiple_of(c * BLOCK, BLOCK), BLOCK)
            kb_ref[sl, :] = _rope(k_ref[sl, :], cos_ref[sl, :], sin_ref[sl, :]).astype(BF16)
            vb_ref[sl, :] = v_ref[sl, :].astype(BF16)
            return carry
        lax.fori_loop(0, lp // BLOCK, body, 0)

    zpad = jnp.zeros((BLOCK - N_META, HEAD_DIM), BF16)
    k_meta = jnp.concatenate([kb_ref[lp - N_META:lp, :], zpad], axis=0)
    v_meta = jnp.concatenate([vb_ref[lp - N_META:lp, :], zpad], axis=0)
    rows = lax.broadcasted_iota(jnp.int32, (BLOCK, 4 * BLOCK), 0)
    cols = lax.broadcasted_iota(jnp.int32, (BLOCK, 4 * BLOCK), 1)
    is_meta = (cols >= 3 * BLOCK) & (cols < 3 * BLOCK + N_META)
    scale = HEAD_DIM ** -0.5
    k_all, v_all, bias, q, sink = [], [], [], [], []
    for b in range(qblocks):
        qi = i * qblocks + b
        kb0 = jnp.clip(qi - 1, 0, nb - 3)
        win = pl.ds(pl.multiple_of(kb0 * BLOCK, BLOCK), 3 * BLOCK)
        k_b = jnp.concatenate([kb_ref[win, :], k_meta], axis=0)
        v_b = jnp.concatenate([vb_ref[win, :], v_meta], axis=0)
        rel = cols - rows + (kb0 - qi) * BLOCK
        ok = ((cols < 3 * BLOCK) & (jnp.abs(rel) <= WINDOW)) | is_meta
        bias_b = jnp.where(ok, 0.0, NEG_BIG).astype(F32)
        qrows = pl.ds(pl.multiple_of(qi * BLOCK, BLOCK), BLOCK)
        cos_q, sin_q = cos_ref[qrows, :], sin_ref[qrows, :]
        for g in range(group):
            k_all.append(k_b)
            v_all.append(v_b)
            bias.append(bias_b)
            q.append(_rope(q_ref[b * BLOCK:(b + 1) * BLOCK, g * HEAD_DIM:(g + 1) * HEAD_DIM],
                           cos_q, sin_q).astype(BF16))
            sink.append(sink_ref[h * group + g])
    n = range(len(q))
    s = [_dot(q[c], k_all[c], NT) * scale + bias[c] for c in n]
    m = [jnp.maximum(jnp.max(s[c], axis=-1, keepdims=True), sink[c]) for c in n]
    p = [jnp.exp(s[c] - m[c]) for c in n]
    denom = [jnp.sum(p[c], axis=-1, keepdims=True) + jnp.exp(sink[c] - m[c]) for c in n]
    o = [_dot(p[c].astype(BF16), v_all[c]) / denom[c] for c in n]
    for c in n:
        b, g = divmod(c, group)
        o_ref[b * BLOCK:(b + 1) * BLOCK, g * HEAD_DIM:(g + 1) * HEAD_DIM] = o[c].astype(o_ref.dtype)


def _attention(z, cos_t, sin_t, sink, S, n_heads, n_kv, off_q):
    Lp = z.shape[0]
    group = n_heads // n_kv
    nb = S // BLOCK
    assert nb >= 3
    qw = group * HEAD_DIM
    kcol = off_q // HEAD_DIM + n_heads
    vcol = kcol + n_kv
    col_bytes = Lp * HEAD_DIM * 4
    qblocks = _pick(nb, (8, 4, 2, 1))
    tq = qblocks * BLOCK
    return pl.pallas_call(
        functools.partial(_attn_kernel, nb=nb, group=group, qblocks=qblocks),
        grid_spec=pltpu.PrefetchScalarGridSpec(
            num_scalar_prefetch=1,
            grid=(n_kv, nb // qblocks),
            in_specs=[pl.BlockSpec((tq, qw), lambda h, i, s: (i, off_q // qw + h)),
                      pl.BlockSpec((Lp, HEAD_DIM), lambda h, i, s: (0, kcol + h)),
                      pl.BlockSpec((Lp, HEAD_DIM), lambda h, i, s: (0, vcol + h)),
                      pl.BlockSpec((Lp, HEAD_DIM), lambda h, i, s: (0, 0)),
                      pl.BlockSpec((Lp, HEAD_DIM), lambda h, i, s: (0, 0))],
            out_specs=pl.BlockSpec((tq, qw), lambda h, i, s: (i, h)),
            scratch_shapes=[pltpu.VMEM((Lp, HEAD_DIM), BF16), pltpu.VMEM((Lp, HEAD_DIM), BF16)]),
        out_shape=jax.ShapeDtypeStruct((S, n_heads * HEAD_DIM), BF16),
        compiler_params=_params(("arbitrary", "arbitrary"),
                                _vmem_limit(tq * qw * 4, 4 * col_bytes, tq * qw * 2,
                                            temps=col_bytes + 8 * group * tq * 512 * 4)),
        name="attention",
    )(sink.astype(F32), z, z, z, cos_t, sin_t)


def _split3(x):
    hi = x.astype(BF16)
    r1 = x - hi.astype(F32)
    mid = r1.astype(BF16)
    lo = (r1 - mid.astype(F32)).astype(BF16)
    return hi, mid, lo


def _head_sum(x, ones_bd):
    outs = []
    for s in range(x.shape[1] // LANES):
        parts = _split3(x[:, s * LANES:(s + 1) * LANES])
        outs.append(sum(_dot(p, ones_bd) for p in parts))
    return outs[0] if len(outs) == 1 else jnp.concatenate(outs, axis=1)


def _prep_kernel(r_ref, rp_ref, rn_ref, k_ref, kp_ref, kn_ref, v_ref, vp_ref, vn_ref,
                 lo_ref, lop_ref, lon_ref, gd_ref, gdp_ref, gdn_ref,
                 pc_ref, pl_ref, wl_ref, g2_ref, ones_ref,
                 ro_ref, vo_ref, kko_ref, lwf_ref, lwb_ref, kdf_ref, kdb_ref, bdf_ref, bdb_ref, go_ref, bo_ref,
                 *, n_tiles, tr, lora_tanh_cols):
    i = pl.program_id(1)

    def shifted(main_ref, prev_ref, next_ref, mup, mun):
        x = main_ref[...]
        row8 = lax.broadcasted_iota(jnp.int32, (8, x.shape[1]), 0)
        down = pltpu.roll(x, 1, 0)
        up = pltpu.roll(x, tr - 1, 0)
        prev = jnp.concatenate([jnp.where(row8 == 0, prev_ref[7:8, :], down[0:8]), down[8:]], axis=0)
        nxt = jnp.concatenate([up[:tr - 8], jnp.where(row8 == 7, next_ref[0:1, :], up[tr - 8:])], axis=0)
        return x + mup * (prev - x) + mun * (nxt - x)

    r = shifted(r_ref, rp_ref, rn_ref, pc_ref[0:1, :], pc_ref[1:2, :])
    k = shifted(k_ref, kp_ref, kn_ref, pc_ref[2:3, :], pc_ref[3:4, :])
    v = shifted(v_ref, vp_ref, vn_ref, pc_ref[4:5, :], pc_ref[5:6, :])
    lo = shifted(lo_ref, lop_ref, lon_ref, pl_ref[0:1, 0:LORA_PAD], pl_ref[1:2, 0:LORA_PAD])
    gd = shifted(gd_ref, gdp_ref, gdn_ref, pl_ref[0:1, LORA_PAD:], pl_ref[1:2, LORA_PAD:])

    row = lax.broadcasted_iota(jnp.int32, (tr, 1), 0)
    valid = jnp.where((i < n_tiles - 1) | (row >= tr - N_META), 1.0, 0.0).astype(F32)

    lane = lax.broadcasted_iota(jnp.int32, lo.shape, 1)
    lo_act = jnp.where(lane < lora_tanh_cols, jnp.tanh(lo), lo).astype(BF16)
    dec_f = _dot(lo_act, wl_ref[0])
    dec_b = _dot(lo_act, wl_ref[1])
    apre_f = _dot(lo_act, wl_ref[2])
    apre_b = _dot(lo_act, wl_ref[3])
    g = _dot(jax.nn.sigmoid(lo).astype(BF16), g2_ref[0]) + _dot(jax.nn.sigmoid(gd).astype(BF16), g2_ref[1])

    def log_decay(dec, w0):
        return -math.exp(-0.5) * jax.nn.sigmoid(w0 + dec)

    a_f = jax.nn.sigmoid(pc_ref[8:9, :] + apre_f)
    a_b = jax.nn.sigmoid(pc_ref[9:10, :] + apre_b)
    kk = k * pc_ref[10:11, :]
    ss = _head_sum(kk * kk, ones_ref[...])
    kk = kk / jnp.maximum(jnp.sqrt(ss), 1e-12) * valid
    k_a = pc_ref[11:12, :]
    kv = k * valid

    rv = r * valid
    vv = v * valid
    kd_f = kv * (1.0 + (a_f - 1.0) * k_a)
    kd_b = kv * (1.0 + (a_b - 1.0) * k_a)
    ro_ref[...] = rv
    vo_ref[...] = vv
    kko_ref[...] = kk
    lwf_ref[...] = log_decay(dec_f, pc_ref[6:7, :])
    lwb_ref[...] = log_decay(dec_b, pc_ref[7:8, :])
    kdf_ref[...] = kd_f
    kdb_ref[...] = kd_b
    bdf_ref[...] = kk * a_f
    bdb_ref[...] = kk * a_b
    go_ref[...] = g
    bo_ref[...] = _head_sum(rv * (kd_f + kd_b) * pc_ref[12:13, :], ones_ref[...]) * vv


def _rwkv_prep(z, off_r, off_lo, C, pc, plo, wl, g2, ones_bd, lora_tanh_cols):
    Lp = z.shape[0]
    tr = BLOCK
    ct = next(c for c in (2048, 1024, 512) if C % c == 0 and off_r % c == 0)
    n_tiles = Lp // tr
    n8 = Lp // 8
    once = pl.Buffered(1) if ct == C else None
    off_gd = off_lo + LORA_PAD
    assert off_r % ct == 0 and off_lo % LORA_PAD == 0

    def seg_specs(off, w, with_c):
        cb = off // w

        def col(c):
            return cb + c if with_c else cb
        return [pl.BlockSpec((tr, w), lambda c, i: (i, col(c))),
                pl.BlockSpec((8, w), lambda c, i: ((i * (tr // 8) + n8 - 1) % n8, col(c))),
                pl.BlockSpec((8, w), lambda c, i: (((i + 1) * (tr // 8)) % n8, col(c)))]

    in_specs = (seg_specs(off_r, ct, True) + seg_specs(off_r + C, ct, True) + seg_specs(off_r + 2 * C, ct, True)
                + seg_specs(off_lo, LORA_PAD, False) + seg_specs(off_gd, LORA_PAD, False)
                + [pl.BlockSpec((16, ct), lambda c, i: (0, c)),
                   pl.BlockSpec((8, 2 * LORA_PAD), lambda c, i: (0, 0)),
                   pl.BlockSpec((4, LORA_PAD, ct), lambda c, i: (0, 0, c), pipeline_mode=once),
                   pl.BlockSpec((2, LORA_PAD, ct), lambda c, i: (0, 0, c), pipeline_mode=once),
                   pl.BlockSpec((LANES, LANES), lambda c, i: (0, 0))])
    out_spec = pl.BlockSpec((tr, ct), lambda c, i: (i, c))
    out_sds = jax.ShapeDtypeStruct((Lp, C), F32)
    return pl.pallas_call(
        functools.partial(_prep_kernel, n_tiles=n_tiles, tr=tr, lora_tanh_cols=lora_tanh_cols),
        grid=(C // ct, n_tiles),
        in_specs=in_specs,
        out_specs=[out_spec] * 11,
        out_shape=[out_sds] * 11,
        compiler_params=_params(("arbitrary", "arbitrary"),
                                _vmem_limit(3 * tr * ct * 4, 2 * tr * LORA_PAD * 4, 4 * LORA_PAD * ct * 2,
                                            2 * LORA_PAD * ct * 2, 11 * tr * ct * 4, temps=24 * tr * ct * 4)),
        name="rwkv_prep",
    )(*([z] * 15), pc, plo, wl, g2, ones_bd)


SCAN_PAIRS = 16


def _scan_consts():
    C = CHUNK
    row = lax.broadcasted_iota(jnp.int32, (C, 2 * C), 0)
    col = lax.broadcasted_iota(jnp.int32, (C, 2 * C), 1) % C
    rr = lax.broadcasted_iota(jnp.int32, (LANES, LANES), 0)
    cc = lax.broadcasted_iota(jnp.int32, (LANES, LANES), 1)
    return dict(
        row=lax.broadcasted_iota(jnp.int32, (C, LANES), 0),
        head0=lax.broadcasted_iota(jnp.int32, (C, LANES), 1) < RWKV_HEAD,
        strict_f=col < row, strict_b=col > row, incl_f=col <= row, incl_b=col >= row,
        eye_sbs=jnp.where(col == row, 1.0, 0.0).astype(F32),
        same_head=(rr // RWKV_HEAD) == (cc // RWKV_HEAD), eye=rr == cc)


def _scan_chains(chains, cst):
    C = CHUNK
    head0, same, eye = cst["head0"], cst["same_head"], cst["eye"]
    n = range(len(chains))
    rev = [c[7] for c in chains]
    r, v, al, lw, k, be, h = ([c[i] for c in chains] for i in range(7))
    strict = [cst["strict_b" if x else "strict_f"] for x in rev]
    incl = [cst["incl_b" if x else "incl_f"] for x in rev]

    def bd(x):
        xb = x.astype(BF16)
        zero = jnp.zeros_like(xb)
        return jnp.concatenate([jnp.where(head0, xb, zero), jnp.where(head0, zero, xb)], axis=0)

    def mm(a, b_bf16, dims=NN):
        return _dot(a.astype(BF16), b_bf16, dims)

    row = cst["row"]
    cl = list(lw)
    sh = 1
    while sh < C:
        cl = [cl[i] + (jnp.where(row < C - sh, pltpu.roll(cl[i], C - sh, 0), 0.0) if rev[i] else
                       jnp.where(row >= sh, pltpu.roll(cl[i], sh, 0), 0.0)) for i in n]
        sh *= 2
    total = [cl[i][0:1, :] if rev[i] else cl[i][C - 1:C, :] for i in n]
    a_t = [al[i] * jnp.exp(cl[i] - lw[i]) for i in n]
    r_t = [r[i] * jnp.exp(cl[i]) for i in n]
    w_inv = [jnp.exp(-cl[i]) for i in n]
    w_rest = [jnp.exp(total[i] - cl[i]) for i in n]
    bd_v = [bd(v[i]) for i in n]
    sc = [mm(jnp.concatenate([a_t[i], r_t[i]], axis=0),
             jnp.concatenate([bd(k[i] * w_inv[i]), bd(be[i] * w_inv[i])], axis=0), NT) for i in n]
    a_ak = [jnp.where(strict[i], sc[i][0:C, 0:2 * C], 0.0) for i in n]
    n_ab = [jnp.where(strict[i], sc[i][0:C, 2 * C:], 0.0) for i in n]
    a_rk = [jnp.where(incl[i], sc[i][C:, 0:2 * C], 0.0) for i in n]
    a_rb = [jnp.where(incl[i], sc[i][C:, 2 * C:], 0.0) for i in n]

    t_inv = [cst["eye_sbs"] - n_ab[i] for i in n]
    sq = [mm(n_ab[i], bd(n_ab[i])) for i in n]
    n_round = 2
    while (1 << n_round) < C:
        both = [mm(jnp.concatenate([t_inv[i], sq[i]], axis=0), bd(sq[i])) for i in n]
        t_inv = [t_inv[i] + both[i][0:C] for i in n]
        sq = [both[i][C:] for i in n]
        n_round += 1
    t_inv = [t_inv[i] + mm(t_inv[i], bd(sq[i])) for i in n]

    akv = [mm(jnp.concatenate([a_ak[i], a_rk[i]], axis=0), bd_v[i]) for i in n]
    tp = [mm(t_inv[i], jnp.concatenate([bd(akv[i][0:C]), bd(a_t[i])], axis=1)) for i in n]
    p0 = [tp[i][:, 0:LANES] for i in n]
    at = [tp[i][:, LANES:] for i in n]
    rb = [mm(a_rb[i], jnp.concatenate([bd(p0[i]), bd(at[i])], axis=1)) for i in n]
    y0 = [akv[i][C:] - rb[i][:, 0:LANES] for i in n]
    rh = [r_t[i] - rb[i][:, LANES:] for i in n]
    bt = [mm(be[i] * w_rest[i], jnp.concatenate([at[i], p0[i]], axis=1).astype(BF16), TN) for i in n]
    kv = [mm(k[i] * w_rest[i], v[i].astype(BF16), TN) for i in n]
    m_mat = [jnp.where(same, jnp.where(eye, jnp.broadcast_to(jnp.exp(total[i]), (LANES, LANES)), 0.0)
                       - bt[i][:, 0:LANES], 0.0) for i in n]
    g_mat = [jnp.where(same, kv[i] - bt[i][:, LANES:], 0.0) for i in n]
    yh = [mm(jnp.concatenate([rh[i], m_mat[i]], axis=0), h[i].astype(BF16)) for i in n]
    return [(yh[i][0:C] + y0[i], yh[i][C:] + g_mat[i]) for i in n]


def _scan_kernel(rf_ref, vf_ref, af_ref, lwf_ref, kf_ref, bf_ref,
                 rb_ref, vb_ref, ab_ref, lwb_ref, kb_ref, bb_ref, wd_ref,
                 yf_ref, yb_ref, wdo_ref, hf_ref, hb_ref, *, pairs, n_cast_in, cast_pads):
    @pl.when(pl.program_id(1) == 0)
    def _():
        hf_ref[...] = jnp.zeros(hf_ref.shape, F32)
        hb_ref[...] = jnp.zeros(hb_ref.shape, F32)

    w_blk = wd_ref[...].astype(wdo_ref.dtype)
    if cast_pads:
        step = pl.program_id(0) * pl.num_programs(1) + pl.program_id(1)
        w_blk = jnp.where(step < n_cast_in, w_blk, jnp.zeros_like(w_blk))
    wdo_ref[...] = w_blk

    cst = _scan_consts()
    chains = []
    for g in range(pairs):
        sl = slice(g * LANES, (g + 1) * LANES)
        chains.append((rf_ref[:, sl], vf_ref[:, sl], af_ref[:, sl], lwf_ref[:, sl], kf_ref[:, sl], bf_ref[:, sl],
                       hf_ref[g], False))
        chains.append((rb_ref[:, sl], vb_ref[:, sl], ab_ref[:, sl], lwb_ref[:, sl], kb_ref[:, sl], bb_ref[:, sl],
                       hb_ref[g], True))
    res = _scan_chains(chains, cst)
    for g in range(pairs):
        sl = slice(g * LANES, (g + 1) * LANES)
        yf_ref[:, sl], hf_ref[g] = res[2 * g]
        yb_ref[:, sl], hb_ref[g] = res[2 * g + 1]


def _rwkv_scan(r, v, kk, lw_f, lw_b, kd_f, kd_b, bd_f, bd_b, w_cast, w_cast_rows, S):
    Lp, C = r.shape
    n_real = S // CHUNK
    n_chunks = Lp // CHUNK
    steps = n_real + 1
    pairs = _pick(C // LANES, (SCAN_PAIRS, 2, 1))
    w = pairs * LANES
    total_steps = (C // w) * steps
    wr, wc = w_cast.shape
    cast_rows = next(rb for rb in (128, 256, 512, 1024)
                     if wr % rb == 0 and w_cast_rows % rb == 0 and w_cast_rows // rb <= total_steps)
    n_cast_in, n_cast_out = wr // cast_rows, w_cast_rows // cast_rows
    cast_in = pl.BlockSpec((cast_rows, wc), lambda p, s: (jnp.minimum(p * steps + s, n_cast_in - 1), 0))
    cast_out = pl.BlockSpec((cast_rows, wc), lambda p, s: (jnp.minimum(p * steps + s, n_cast_out - 1), 0))

    def fwd(p, s):
        return ((s + n_real + 1) % n_chunks, p)

    def bwd(p, s):
        return (n_real - s, p)

    spec_f = pl.BlockSpec((CHUNK, w), fwd)
    spec_b = pl.BlockSpec((CHUNK, w), bwd)
    out_sds = jax.ShapeDtypeStruct((S + CHUNK, C), F32)
    out_f = pl.BlockSpec((CHUNK, w), lambda p, s: ((s + n_real) % (n_real + 1), p))
    return pl.pallas_call(
        functools.partial(_scan_kernel, pairs=pairs, n_cast_in=n_cast_in, cast_pads=n_cast_out > n_cast_in),
        grid=(C // w, steps),
        in_specs=[spec_f] * 6 + [spec_b] * 6 + [cast_in],
        out_specs=[out_f, spec_b, cast_out],
        out_shape=[out_sds, out_sds, jax.ShapeDtypeStruct((w_cast_rows, wc), BF16)],
        scratch_shapes=[pltpu.VMEM((pairs, LANES, LANES), F32), pltpu.VMEM((pairs, LANES, LANES), F32)],
        compiler_params=_params(("arbitrary", "arbitrary"),
                                _vmem_limit(14 * CHUNK * w * 4, cast_rows * wc * 6, temps=16 << 20)),
        name="rwkv_scan",
    )(r, v, kk, lw_f, kd_f, bd_f, r, v, kk, lw_b, kd_b, bd_b, w_cast)


def _post_kernel(yf_ref, yb_ref, bonus_ref, g_ref, pq_ref, ones_ref, o_ref):
    ones_bd = ones_ref[...]
    inv_n = 1.0 / RWKV_HEAD
    y = yf_ref[...] + yb_ref[...]
    mean = _head_sum(y, ones_bd) * inv_n
    yc = y - mean
    var = _head_sum(yc * yc, ones_bd) * inv_n
    yn = yc * lax.rsqrt(var + GN_EPS) * pq_ref[0:1, :] + pq_ref[1:2, :]
    o_ref[...] = ((yn + bonus_ref[...]) * g_ref[...]).astype(o_ref.dtype)


def _rwkv_post(y_f, y_b, bonus, g, pq, ones_bd, S):
    C = g.shape[1]
    tr = 512
    ct = _pick(C, (1024, 512))
    spec = pl.BlockSpec((tr, ct), lambda i, c: (i, c))
    return pl.pallas_call(
        _post_kernel,
        grid=(S // tr, C // ct),
        in_specs=[spec] * 4 + [pl.BlockSpec((8, ct), lambda i, c: (0, c)),
                               pl.BlockSpec((LANES, LANES), lambda i, c: (0, 0))],
        out_specs=spec,
        out_shape=jax.ShapeDtypeStruct((S, C), BF16),
        compiler_params=_params(("arbitrary", "arbitrary"),
                                _vmem_limit(4 * tr * ct * 4, tr * ct * 2, temps=16 * tr * ct * 4)),
        name="rwkv_post",
    )(y_f, y_b, bonus, g, pq, ones_bd)


IN_TN = 1024


def _matmul_tiles(S, D, F, Lp):
    glu_tn = 512
    f_pad = -(-F // glu_tn) * glu_tn
    return dict(
        in_tm=_pick(Lp, (1040, 640)),
        proj_tm=512, proj_tn=_pick(D, (1024, 512)),
        glu_tm=_pick(S, (1024, 512)), glu_tn=glu_tn, f_pad=f_pad,
        down_tm=_pick(S, (1024, 512)), down_tn=_pick(D, (1024, 512)), down_tk=_pick(f_pad, (2816, 1024, 768, 512)),
    )

def kernel(x, meta_tokens, norm_mix_w, w_in, b_gate, mu_prev, mu_next, dec_w0, dec_w2, iclr_a0, iclr_a2,
           gate_w2, k_k, k_a, r_k, ln_x_w, ln_x_b, attn_sink, w_proj_attn, w_proj_rwkv, w_out, norm_ffn_w,
           w_ffn_gate, w_ffn_up, w_ffn_down, norm_final_w):
    B, S, D = x.shape
    assert B == 1 and norm_mix_w.shape[0] == 1, "one sequence, one layer"
    C = k_k.shape[-1]
    AW = w_proj_attn.shape[1]
    n_heads = attn_sink.shape[-1]
    DL, IL, GL = dec_w2.shape[2], iclr_a2.shape[2], gate_w2.shape[1]
    shift_w = mu_prev.shape[-1]
    KVW = (w_in.shape[-1] - 2 * D - AW - shift_w) // 2
    n_kv = KVW // HEAD_DIM
    F = w_ffn_gate.shape[-1]
    lora_w = 2 * DL + 2 * IL
    assert shift_w == 3 * C + lora_w + GL and lora_w <= LORA_PAD and GL <= LORA_PAD
    assert S % BLOCK == 0 and C % LANES == 0 and n_heads * HEAD_DIM == AW
    Lp = S + PAD_ROWS

    off_q = 2 * D
    off_r = off_q + AW + 2 * KVW
    off_lo = off_r + 3 * C
    n_in_pad = -(-w_in.shape[-1] // IN_TN) * IN_TN
    assert n_in_pad >= off_lo + 2 * LORA_PAD and off_lo % LORA_PAD == 0
    t_r, t_lo = off_r // IN_TN, off_lo // IN_TN
    if off_r % IN_TN == 0 and off_lo % IN_TN == 0 and n_in_pad == off_lo + IN_TN:
        z_off_lo, z_off_r = off_r, off_r + IN_TN

        def z_col_tile(j):
            return jnp.where(j == t_lo, t_r, jnp.where(j >= t_r, j + 1, j))
    else:
        z_off_lo, z_off_r = off_lo, off_r

        def z_col_tile(j):
            return j
    low_w = lora_w + GL
    gl_a = LORA_PAD - lora_w

    mu_p, mu_n = mu_prev[0], mu_next[0]
    zc = jnp.zeros((C,), F32)
    pc = jnp.stack([mu_p[:C], mu_n[:C], mu_p[C:2 * C], mu_n[C:2 * C], mu_p[2 * C:3 * C], mu_n[2 * C:3 * C],
                    dec_w0[0, 0], dec_w0[0, 1], iclr_a0[0, 0], iclr_a0[0, 1], k_k[0], k_a[0], r_k[0], zc, zc, zc])
    plo = jnp.zeros((8, 2 * LORA_PAD), F32).at[0, :low_w].set(mu_p[3 * C:]).at[1, :low_w].set(mu_n[3 * C:])
    wl = jnp.zeros((4, LORA_PAD, C), F32)
    wl = wl.at[0, 0:DL].set(dec_w2[0, 0]).at[1, DL:2 * DL].set(dec_w2[0, 1])
    wl = wl.at[2, 2 * DL:2 * DL + IL].set(iclr_a2[0, 0]).at[3, 2 * DL + IL:lora_w].set(iclr_a2[0, 1])
    wl = wl.astype(BF16)
    g2 = jnp.zeros((2, LORA_PAD, C), F32).at[0, lora_w:].set(gate_w2[0, :gl_a]).at[1, :GL - gl_a].set(
        gate_w2[0, gl_a:]).astype(BF16)
    pq = jnp.zeros((8, C), F32).at[0].set(ln_x_w[0]).at[1].set(ln_x_b[0])
    lane = jnp.arange(LANES)
    ones_bd = (lane[:, None] // RWKV_HEAD == lane[None, :] // RWKV_HEAD).astype(BF16)


    pos = jnp.concatenate([jnp.arange(N_META, N_META + S), jnp.zeros((PAD_ROWS - N_META,), jnp.int32),
                           jnp.arange(N_META)]).astype(F32)
    inv = ROPE_THETA ** (-jnp.arange(0, ROPE_DIMS, 2, dtype=F32) / ROPE_DIMS)
    ang = pos[:, None] * inv[None, :]
    ones_rest = jnp.ones((Lp, HEAD_DIM - ROPE_DIMS), F32)
    cos_t = jnp.concatenate([jnp.cos(ang), jnp.cos(ang), ones_rest], axis=1)
    sin_t = jnp.concatenate([-jnp.sin(ang), jnp.sin(ang), 0.0 * ones_rest], axis=1)

    t = _matmul_tiles(S, D, F, Lp)
    x2 = x[0]
    h_ext = _norm_ext(x2, meta_tokens.astype(x.dtype), norm_mix_w[0])
    z = _mm_in(h_ext, w_in[0], t["in_tm"], IN_TN, z_col_tile)
    o_att = _attention(z, cos_t, sin_t, attn_sink[0], S, n_heads, n_kv, off_q)
    r_s, v_s, kk, lw_f, lw_b, kd_f, kd_b, bd_f, bd_b, g, bonus = _rwkv_prep(
        z, z_off_r, z_off_lo, C, pc, plo, wl, g2, ones_bd, 2 * DL)
    y_f, y_b, w_d = _rwkv_scan(r_s, v_s, kk, lw_f, lw_b, kd_f, kd_b, bd_f, bd_b, w_ffn_down[0], t["f_pad"], S)
    o_rwkv = _rwkv_post(y_f, y_b, bonus, g, pq, ones_bd, S)
    mixed = _mm_mix(o_att, w_proj_attn[0], o_rwkv, w_proj_rwkv[0], z, b_gate[0], S, D, t["proj_tm"], t["proj_tn"])
    h1, h1g, h1ss = _mm_out(mixed, w_out[0], x2, norm_ffn_w[0], t["proj_tm"], t["proj_tn"])
    act = _mm_glu(h1g, h1ss, w_ffn_gate[0], w_ffn_up[0], t["glu_tm"], t["glu_tn"])
    h2 = _mm_res(act, w_d, h1, t["down_tm"], t["down_tn"], t["down_tk"])
    y = _norm(h2, norm_final_w, x.dtype)
    return y[None]
```

```python
import functools
import math

import jax
import jax.numpy as jnp
from jax import lax
from jax.experimental import pallas as pl
from jax.experimental.pallas import tpu as pltpu

F32 = jnp.float32
BF16 = jnp.bfloat16

N_META = 16
HEAD_DIM = 128
WINDOW = 128
BLOCK = 128
ROPE_DIMS = HEAD_DIM // 4
ROPE_THETA = 500000.0
RWKV_HEAD = 64
RMS_EPS = 1e-6
GN_EPS = 64e-5
LANES = 128
PAD_ROWS = BLOCK
CHUNK = 64
LORA_PAD = 512
VMEM_PHYSICAL = 64 * 1024 * 1024

NN = (((1,), (0,)), ((), ()))
NT = (((1,), (1,)), ((), ()))
TN = (((0,), (0,)), ((), ()))


def _vmem_limit(*block_bytes, temps=0):
    need = 2 * sum(block_bytes) + temps + (4 << 20)
    return int(min(max(need, 16 << 20), VMEM_PHYSICAL - (4 << 20)))


def _params(sem, limit):
    return pltpu.CompilerParams(dimension_semantics=sem, vmem_limit_bytes=limit)


def _pick(n, cands):
    for c in cands:
        if n % c == 0:
            return c
    raise ValueError(f"no tile in {cands} divides {n}")


def _rms(x, w):
    return x * lax.rsqrt(jnp.mean(x * x, axis=-1, keepdims=True) + RMS_EPS) * w


def _norm_ext_kernel(x_ref, meta_ref, w_ref, o_ref, *, n_steps, real_last):
    i = pl.program_id(0)
    tr = o_ref.shape[0]

    @pl.when(i < n_steps - 1)
    def _():
        o_ref[...] = _rms(x_ref[...], w_ref[...]).astype(o_ref.dtype)

    @pl.when(i == n_steps - 1)
    def _():
        if real_last:
            o_ref[0:real_last, :] = _rms(x_ref[0:real_last, :], w_ref[...]).astype(o_ref.dtype)
        o_ref[real_last:tr - N_META, :] = jnp.zeros((tr - N_META - real_last, o_ref.shape[1]), o_ref.dtype)
        o_ref[tr - N_META:, :] = _rms(meta_ref[...], w_ref[...]).astype(o_ref.dtype)


def _norm_ext(x, meta, w):
    S, D = x.shape
    Lp = S + PAD_ROWS
    tr = _pick(Lp, (640, BLOCK))
    n_steps = Lp // tr
    real_last = S - (n_steps - 1) * tr
    last_x_blk = (S - 1) // tr
    return pl.pallas_call(
        functools.partial(_norm_ext_kernel, n_steps=n_steps, real_last=real_last),
        grid=(n_steps,),
        in_specs=[pl.BlockSpec((tr, D), lambda i: (jnp.minimum(i, last_x_blk), 0)),
                  pl.BlockSpec((N_META, D), lambda i: (0, 0)),
                  pl.BlockSpec((1, D), lambda i: (0, 0))],
        out_specs=pl.BlockSpec((tr, D), lambda i: (i, 0)),
        out_shape=jax.ShapeDtypeStruct((Lp, D), BF16),
        compiler_params=_params(("arbitrary",), _vmem_limit(tr * D * 4, tr * D * 2, temps=2 * tr * D * 4)),
        name="norm_ext",
    )(x, meta, w.reshape(1, D))


def _norm_kernel(x_ref, w_ref, o_ref):
    o_ref[...] = _rms(x_ref[...], w_ref[...]).astype(o_ref.dtype)


def _norm(x, w, out_dtype, tr=512):
    S, D = x.shape
    return pl.pallas_call(
        _norm_kernel,
        grid=(S // tr,),
        in_specs=[pl.BlockSpec((tr, D), lambda i: (i, 0)), pl.BlockSpec((1, D), lambda i: (0, 0))],
        out_specs=pl.BlockSpec((tr, D), lambda i: (i, 0)),
        out_shape=jax.ShapeDtypeStruct((S, D), out_dtype),
        compiler_params=_params(("arbitrary",), _vmem_limit(tr * D * 4, tr * D * 4, temps=2 * tr * D * 4)),
        name="norm",
    )(x, w.reshape(1, D))


def _dot(a, b, dims=NN):
    return lax.dot_general(a, b, dims, preferred_element_type=F32)


def _cast_resident(src_ref, dst_ref):
    K = src_ref.shape[0]
    rows = _pick(K, (256, 128))

    def body(c, carry):
        sl = pl.ds(pl.multiple_of(c * rows, rows), rows)
        dst_ref[sl, :] = src_ref[sl, :].astype(dst_ref.dtype)
        return carry

    lax.fori_loop(0, K // rows, body, 0)


def _weight_copy(w_hbm, land_ref, sem, jj, tn, width):
    return pltpu.make_async_copy(w_hbm.at[:, pl.ds(pl.multiple_of(jj * tn, LANES), width)],
                                 land_ref.at[:, pl.ds(0, width)], sem)


def _weight_copy_op(op, w_hbm, land_ref, sem, jj, tn, nj, n_cols):
    rem = n_cols - (nj - 1) * tn
    if rem == tn:
        getattr(_weight_copy(w_hbm, land_ref, sem, jj, tn, tn), op)()
        return

    @pl.when(jj < nj - 1)
    def _():
        getattr(_weight_copy(w_hbm, land_ref, sem, jj, tn, tn), op)()

    @pl.when(jj == nj - 1)
    def _():
        getattr(_weight_copy(w_hbm, land_ref, sem, jj, tn, rem), op)()


def _stream_weights(weights, j, i, nj, tn, n_cols):
    @pl.when(i == 0)
    def _():
        @pl.when(j == 0)
        def _():
            for w_hbm, land_ref, _, sem in weights:
                _weight_copy_op("start", w_hbm, land_ref, sem, j, tn, nj, n_cols)

        for w_hbm, land_ref, bw_ref, sem in weights:
            _weight_copy_op("wait", w_hbm, land_ref, sem, j, tn, nj, n_cols)
            _cast_resident(land_ref, bw_ref)

        @pl.when(j + 1 < nj)
        def _():
            for w_hbm, land_ref, _, sem in weights:
                _weight_copy_op("start", w_hbm, land_ref, sem, j + 1, tn, nj, n_cols)


def _weight_scratch(ks, tn):
    return ([pltpu.VMEM((k, tn), F32) for k in ks] + [pltpu.VMEM((k, tn), BF16) for k in ks]
            + [pltpu.SemaphoreType.DMA(()) for _ in ks])


def _mm_in_kernel(a_ref, w_hbm, o_ref, land_ref, bw_ref, sem, *, nj, tn, n_cols):
    j, i = pl.program_id(0), pl.program_id(1)
    _stream_weights([(w_hbm, land_ref, bw_ref, sem)], j, i, nj, tn, n_cols)
    _per_tile_width(j, nj, tn, n_cols, lambda wd: _store_cols(o_ref, wd, _dot(a_ref[...], bw_ref[:, 0:wd])))


def _per_tile_width(j, nj, tn, n_cols, body):
    rem = n_cols - (nj - 1) * tn
    if rem == tn:
        body(tn)
        return
    pl.when(j < nj - 1)(lambda: body(tn))
    pl.when(j == nj - 1)(lambda: body(rem))


def _store_cols(o_ref, width, value):
    o_ref[:, 0:width] = value.astype(o_ref.dtype)
    if width < o_ref.shape[1]:
        o_ref[:, width:] = jnp.zeros((o_ref.shape[0], o_ref.shape[1] - width), o_ref.dtype)


def _mm_in(a, w, tm, tn, col_tile=lambda j: j):
    M, K = a.shape
    N = w.shape[1]
    nj = pl.cdiv(N, tn)
    return pl.pallas_call(
        functools.partial(_mm_in_kernel, nj=nj, tn=tn, n_cols=N),
        grid=(nj, M // tm),
        in_specs=[pl.BlockSpec((tm, K), lambda j, i: (i, 0)), pl.BlockSpec(memory_space=pl.ANY)],
        out_specs=pl.BlockSpec((tm, tn), lambda j, i: (i, col_tile(j))),
        out_shape=jax.ShapeDtypeStruct((M, nj * tn), F32),
        scratch_shapes=_weight_scratch((K,), tn),
        compiler_params=_params(("arbitrary", "arbitrary"),
                                _vmem_limit(tm * K * 2, tm * tn * 4, temps=K * tn * 6 + 2 * tm * tn * 4)),
        name="mm_in",
    )(a, w)


def _mm_mix_kernel(a1_ref, a2_ref, z0_ref, z1_ref, bg_ref, w1_hbm, w2_hbm, o_ref,
                     l1_ref, l2_ref, b1_ref, b2_ref, s1, s2, *, nj, tn, n_cols):
    j, i = pl.program_id(0), pl.program_id(1)
    _stream_weights([(w1_hbm, l1_ref, b1_ref, s1), (w2_hbm, l2_ref, b2_ref, s2)], j, i, nj, tn, n_cols)
    g0 = jax.nn.sigmoid(z0_ref[...] + bg_ref[0:1, :])
    g1 = jax.nn.sigmoid(z1_ref[...] + bg_ref[1:2, :])
    o_ref[...] = (g0 * _dot(a1_ref[...], b1_ref[...]) + g1 * _dot(a2_ref[...], b2_ref[...])).astype(o_ref.dtype)


def _mm_mix(o_att, w_pa, o_rwkv, w_pr, z, b_gate, S, D, tm, tn):
    K1, K2 = o_att.shape[1], o_rwkv.shape[1]
    assert D % tn == 0
    nj = D // tn
    return pl.pallas_call(
        functools.partial(_mm_mix_kernel, nj=nj, tn=tn, n_cols=D),
        grid=(nj, S // tm),
        in_specs=[pl.BlockSpec((tm, K1), lambda j, i: (i, 0)), pl.BlockSpec((tm, K2), lambda j, i: (i, 0)),
                  pl.BlockSpec((tm, tn), lambda j, i: (i, j)), pl.BlockSpec((tm, tn), lambda j, i: (i, nj + j)),
                  pl.BlockSpec((2, tn), lambda j, i: (0, j)),
                  pl.BlockSpec(memory_space=pl.ANY), pl.BlockSpec(memory_space=pl.ANY)],
        out_specs=pl.BlockSpec((tm, tn), lambda j, i: (i, j)),
        out_shape=jax.ShapeDtypeStruct((S, D), BF16),
        scratch_shapes=_weight_scratch((K1, K2), tn),
        compiler_params=_params(("arbitrary", "arbitrary"),
                                _vmem_limit(tm * (K1 + K2) * 2, 2 * tm * tn * 4, tm * tn * 2,
                                            temps=(K1 + K2) * tn * 6 + 3 * tm * tn * 4)),
        name="mm_mix",
    )(o_att, o_rwkv, z, z, b_gate, w_pa, w_pr)


def _mm_out_kernel(a_ref, r_ref, g_ref, w_hbm, o_ref, hg_ref, ss_ref, land_ref, bw_ref, sem, *, nj, tn, n_cols):
    j, i = pl.program_id(0), pl.program_id(1)
    _stream_weights([(w_hbm, land_ref, bw_ref, sem)], j, i, nj, tn, n_cols)
    h = r_ref[...] + _dot(a_ref[...], bw_ref[...])
    o_ref[...] = h
    hg_ref[...] = (h * g_ref[...]).astype(hg_ref.dtype)
    ss_ref[...] = jnp.broadcast_to(jnp.sum(h * h, axis=1, keepdims=True), ss_ref.shape)


def _mm_out(a, w, res, gain, tm, tn):
    M, K = a.shape
    N = w.shape[1]
    assert N % tn == 0
    nj = N // tn
    tile = pl.BlockSpec((tm, tn), lambda j, i: (i, j))
    return pl.pallas_call(
        functools.partial(_mm_out_kernel, nj=nj, tn=tn, n_cols=N),
        grid=(nj, M // tm),
        in_specs=[pl.BlockSpec((tm, K), lambda j, i: (i, 0)), tile, pl.BlockSpec((1, tn), lambda j, i: (0, j)),
                  pl.BlockSpec(memory_space=pl.ANY)],
        out_specs=[tile, tile, pl.BlockSpec((tm, LANES), lambda j, i: (i, j))],
        out_shape=[jax.ShapeDtypeStruct((M, N), F32), jax.ShapeDtypeStruct((M, N), BF16),
                   jax.ShapeDtypeStruct((M, nj * LANES), F32)],
        scratch_shapes=_weight_scratch((K,), tn),
        compiler_params=_params(("arbitrary", "arbitrary"),
                                _vmem_limit(tm * K * 2, 2 * tm * tn * 4, tm * tn * 2, tm * LANES * 4,
                                            temps=K * tn * 6 + 2 * tm * tn * 4)),
        name="mm_out",
    )(a, res, gain.reshape(1, N), w)


def _mm_glu_kernel(a_ref, ss_ref, wg_hbm, wu_hbm, o_ref, lg_ref, lu_ref, bg_ref, bu_ref, sg, su,
                     *, nj, tn, n_cols, n_model):
    j, i = pl.program_id(0), pl.program_id(1)
    _stream_weights([(wg_hbm, lg_ref, bg_ref, sg), (wu_hbm, lu_ref, bu_ref, su)], j, i, nj, tn, n_cols)
    ss = ss_ref[:, 0:LANES]
    for t in range(1, ss_ref.shape[1] // LANES):
        ss = ss + ss_ref[:, t * LANES:(t + 1) * LANES]
    scale = lax.rsqrt(ss[:, 0:1] * (1.0 / n_model) + RMS_EPS)

    def body(wd):
        a = a_ref[...]
        gate = scale * _dot(a, bg_ref[:, 0:wd])
        up = scale * _dot(a, bu_ref[:, 0:wd])
        _store_cols(o_ref, wd, gate * jax.nn.sigmoid(gate) * up)

    _per_tile_width(j, nj, tn, n_cols, body)


def _mm_glu(a, ss, wg, wu, tm, tn):
    M, K = a.shape
    N = wg.shape[1]
    nj = pl.cdiv(N, tn)
    return pl.pallas_call(
        functools.partial(_mm_glu_kernel, nj=nj, tn=tn, n_cols=N, n_model=K),
        grid=(nj, M // tm),
        in_specs=[pl.BlockSpec((tm, K), lambda j, i: (i, 0)), pl.BlockSpec((tm, ss.shape[1]), lambda j, i: (i, 0)),
                  pl.BlockSpec(memory_space=pl.ANY), pl.BlockSpec(memory_space=pl.ANY)],
        out_specs=pl.BlockSpec((tm, tn), lambda j, i: (i, j)),
        out_shape=jax.ShapeDtypeStruct((M, nj * tn), BF16),
        scratch_shapes=_weight_scratch((K, K), tn),
        compiler_params=_params(("arbitrary", "arbitrary"),
                                _vmem_limit(tm * K * 2, tm * ss.shape[1] * 4, tm * tn * 2,
                                            temps=2 * K * tn * 6 + 6 * tm * tn * 4)),
        name="mm_glu",
    )(a, ss, wg, wu)


def _mm_res_kernel(a_ref, b_ref, r_ref, o_ref, *, nk):
    k = pl.program_id(2)
    if nk == 1:
        o_ref[...] = r_ref[...] + _dot(a_ref[...], b_ref[...])
    else:
        @pl.when(k == 0)
        def _():
            o_ref[...] = r_ref[...] + _dot(a_ref[...], b_ref[...])

        @pl.when(k > 0)
        def _():
            o_ref[...] = o_ref[...] + _dot(a_ref[...], b_ref[...])


def _mm_res(a, b, res, tm, tn, tk):
    M, K = a.shape
    N = b.shape[1]
    nk = K // tk
    return pl.pallas_call(
        functools.partial(_mm_res_kernel, nk=nk),
        grid=(M // tm, N // tn, nk),
        in_specs=[pl.BlockSpec((tm, tk), lambda i, j, k: (i, k)), pl.BlockSpec((tk, tn), lambda i, j, k: (k, j)),
                  pl.BlockSpec((tm, tn), lambda i, j, k: (i, j))],
        out_specs=pl.BlockSpec((tm, tn), lambda i, j, k: (i, j)),
        out_shape=jax.ShapeDtypeStruct((M, N), F32),
        compiler_params=_params(("arbitrary", "arbitrary", "arbitrary"),
                                _vmem_limit(tm * tk * 2, tk * tn * 2, 2 * tm * tn * 4, temps=tm * tn * 4)),
        name="mm_down",
    )(a, b, res)


NEG_BIG = -1e30


def _rope(t, cos, sin):
    half = ROPE_DIMS // 2
    lane = lax.broadcasted_iota(jnp.int32, t.shape, 1)
    up = jnp.concatenate([t[:, half:], t[:, :half]], axis=1)
    down = jnp.concatenate([t[:, HEAD_DIM - half:], t[:, :HEAD_DIM - half]], axis=1)
    return t * cos + jnp.where(lane < half, up, down) * sin


def _attn_kernel(sink_ref, q_ref, k_ref, v_ref, cos_ref, sin_ref, o_ref, kb_ref, vb_ref, *, nb, group, qblocks):
    h, i = pl.program_id(0), pl.program_id(1)
    lp = k_ref.shape[0]

    @pl.when(i == 0)
    def _():
        def body(c, carry):
            sl = pl.ds(pl.multiple_of(c * BLOCK, BLOCK), BLOCK)
            kb_ref[sl, :] = _rope(k_ref[sl, :], cos_ref[sl, :], sin_ref[sl, :]).astype(BF16)
            vb_ref[sl, :] = v_ref[sl, :].astype(BF16)
            return carry
        lax.fori_loop(0, lp // BLOCK, body, 0)

    zpad = jnp.zeros((BLOCK - N_META, HEAD_DIM), BF16)
    k_meta = jnp.concatenate([kb_ref[lp - N_META:lp, :], zpad], axis=0)
    v_meta = jnp.concatenate([vb_ref[lp - N_META:lp, :], zpad], axis=0)
    rows = lax.broadcasted_iota(jnp.int32, (BLOCK, 4 * BLOCK), 0)
    cols = lax.broadcasted_iota(jnp.int32, (BLOCK, 4 * BLOCK), 1)
    is_meta = (cols >= 3 * BLOCK) & (cols < 3 * BLOCK + N_META)
    scale = HEAD_DIM ** -0.5
    k_all, v_all, bias, q, sink = [], [], [], [], []
    for b in range(qblocks):
        qi = i * qblocks + b
        kb0 = jnp.clip(qi - 1, 0, nb - 3)
        win = pl.ds(pl.multiple_of(kb0 * BLOCK, BLOCK), 3 * BLOCK)
        k_b = jnp.concatenate([kb_ref[win, :], k_meta], axis=0)
        v_b = jnp.concatenate([vb_ref[win, :], v_meta], axis=0)
        rel = cols - rows + (kb0 - qi) * BLOCK
        ok = ((cols < 3 * BLOCK) & (jnp.abs(rel) <= WINDOW)) | is_meta
        bias_b = jnp.where(ok, 0.0, NEG_BIG).astype(F32)
        qrows = pl.ds(pl.multiple_of(qi * BLOCK, BLOCK), BLOCK)
        cos_q, sin_q = cos_ref[qrows, :], sin_ref[qrows, :]
        for g in range(group):
            k_all.append(k_b)
            v_all.append(v_b)
            bias.append(bias_b)
            q.append(_rope(q_ref[b * BLOCK:(b + 1) * BLOCK, g * HEAD_DIM:(g + 1) * HEAD_DIM],
                           cos_q, sin_q).astype(BF16))
            sink.append(sink_ref[h * group + g])
    n = range(len(q))
    s = [_dot(q[c], k_all[c], NT) * scale + bias[c] for c in n]
    m = [jnp.maximum(jnp.max(s[c], axis=-1, keepdims=True), sink[c]) for c in n]
    p = [jnp.exp(s[c] - m[c]) for c in n]
    denom = [jnp.sum(p[c], axis=-1, keepdims=True) + jnp.exp(sink[c] - m[c]) for c in n]
    o = [_dot(p[c].astype(BF16), v_all[c]) / denom[c] for c in n]
    for c in n:
        b, g = divmod(c, group)
        o_ref[b * BLOCK:(b + 1) * BLOCK, g * HEAD_DIM:(g + 1) * HEAD_DIM] = o[c].astype(o_ref.dtype)


def _attention(z, cos_t, sin_t, sink, S, n_heads, n_kv, off_q):
    Lp = z.shape[0]
    group = n_heads // n_kv
    nb = S // BLOCK
    assert nb >= 3
    qw = group * HEAD_DIM
    kcol = off_q // HEAD_DIM + n_heads
    vcol = kcol + n_kv
    col_bytes = Lp * HEAD_DIM * 4
    qblocks = _pick(nb, (8, 4, 2, 1))
    tq = qblocks * BLOCK
    return pl.pallas_call(
        functools.partial(_attn_kernel, nb=nb, group=group, qblocks=qblocks),
        grid_spec=pltpu.PrefetchScalarGridSpec(
            num_scalar_prefetch=1,
            grid=(n_kv, nb // qblocks),
            in_specs=[pl.BlockSpec((tq, qw), lambda h, i, s: (i, off_q // qw + h)),
                      pl.BlockSpec((Lp, HEAD_DIM), lambda h, i, s: (0, kcol + h)),
                      pl.BlockSpec((Lp, HEAD_DIM), lambda h, i, s: (0, vcol + h)),
                      pl.BlockSpec((Lp, HEAD_DIM), lambda h, i, s: (0, 0)),
                      pl.BlockSpec((Lp, HEAD_DIM), lambda h, i, s: (0, 0))],
            out_specs=pl.BlockSpec((tq, qw), lambda h, i, s: (i, h)),
            scratch_shapes=[pltpu.VMEM((Lp, HEAD_DIM), BF16), pltpu.VMEM((Lp, HEAD_DIM), BF16)]),
        out_shape=jax.ShapeDtypeStruct((S, n_heads * HEAD_DIM), BF16),
        compiler_params=_params(("arbitrary", "arbitrary"),
                                _vmem_limit(tq * qw * 4, 4 * col_bytes, tq * qw * 2,
                                            temps=col_bytes + 8 * group * tq * 512 * 4)),
        name="attention",
    )(sink.astype(F32), z, z, z, cos_t, sin_t)


def _split3(x):
    hi = x.astype(BF16)
    r1 = x - hi.astype(F32)
    mid = r1.astype(BF16)
    lo = (r1 - mid.astype(F32)).astype(BF16)
    return hi, mid, lo


def _head_sum(x, ones_bd):
    outs = []
    for s in range(x.shape[1] // LANES):
        parts = _split3(x[:, s * LANES:(s + 1) * LANES])
        outs.append(sum(_dot(p, ones_bd) for p in parts))
    return outs[0] if len(outs) == 1 else jnp.concatenate(outs, axis=1)


def _prep_kernel(r_ref, rp_ref, rn_ref, k_ref, kp_ref, kn_ref, v_ref, vp_ref, vn_ref,
                 lo_ref, lop_ref, lon_ref, gd_ref, gdp_ref, gdn_ref,
                 pc_ref, pl_ref, wl_ref, g2_ref, ones_ref,
                 ro_ref, vo_ref, kko_ref, lwf_ref, lwb_ref, kdf_ref, kdb_ref, bdf_ref, bdb_ref, go_ref, bo_ref,
                 *, n_tiles, tr, lora_tanh_cols):
    i = pl.program_id(1)

    def shifted(main_ref, prev_ref, next_ref, mup, mun):
        x = main_ref[...]
        row8 = lax.broadcasted_iota(jnp.int32, (8, x.shape[1]), 0)
        down = pltpu.roll(x, 1, 0)
        up = pltpu.roll(x, tr - 1, 0)
        prev = jnp.concatenate([jnp.where(row8 == 0, prev_ref[7:8, :], down[0:8]), down[8:]], axis=0)
        nxt = jnp.concatenate([up[:tr - 8], jnp.where(row8 == 7, next_ref[0:1, :], up[tr - 8:])], axis=0)
        return x + mup * (prev - x) + mun * (nxt - x)

    r = shifted(r_ref, rp_ref, rn_ref, pc_ref[0:1, :], pc_ref[1:2, :])
    k = shifted(k_ref, kp_ref, kn_ref, pc_ref[2:3, :], pc_ref[3:4, :])
    v = shifted(v_ref, vp_ref, vn_ref, pc_ref[4:5, :], pc_ref[5:6, :])
    lo = shifted(lo_ref, lop_ref, lon_ref, pl_ref[0:1, 0:LORA_PAD], pl_ref[1:2, 0:LORA_PAD])
    gd = shifted(gd_ref, gdp_ref, gdn_ref, pl_ref[0:1, LORA_PAD:], pl_ref[1:2, LORA_PAD:])

    row = lax.broadcasted_iota(jnp.int32, (tr, 1), 0)
    valid = jnp.where((i < n_tiles - 1) | (row >= tr - N_META), 1.0, 0.0).astype(F32)

    lane = lax.broadcasted_iota(jnp.int32, lo.shape, 1)
    lo_act = jnp.where(lane < lora_tanh_cols, jnp.tanh(lo), lo).astype(BF16)
    dec_f = _dot(lo_act, wl_ref[0])
    dec_b = _dot(lo_act, wl_ref[1])
    apre_f = _dot(lo_act, wl_ref[2])
    apre_b = _dot(lo_act, wl_ref[3])
    g = _dot(jax.nn.sigmoid(lo).astype(BF16), g2_ref[0]) + _dot(jax.nn.sigmoid(gd).astype(BF16), g2_ref[1])

    def log_decay(dec, w0):
        return -math.exp(-0.5) * jax.nn.sigmoid(w0 + dec)

    a_f = jax.nn.sigmoid(pc_ref[8:9, :] + apre_f)
    a_b = jax.nn.sigmoid(pc_ref[9:10, :] + apre_b)
    kk = k * pc_ref[10:11, :]
    ss = _head_sum(kk * kk, ones_ref[...])
    kk = kk / jnp.maximum(jnp.sqrt(ss), 1e-12) * valid
    k_a = pc_ref[11:12, :]
    kv = k * valid

    rv = r * valid
    vv = v * valid
    kd_f = kv * (1.0 + (a_f - 1.0) * k_a)
    kd_b = kv * (1.0 + (a_b - 1.0) * k_a)
    ro_ref[...] = rv
    vo_ref[...] = vv
    kko_ref[...] = kk
    lwf_ref[...] = log_decay(dec_f, pc_ref[6:7, :])
    lwb_ref[...] = log_decay(dec_b, pc_ref[7:8, :])
    kdf_ref[...] = kd_f
    kdb_ref[...] = kd_b
    bdf_ref[...] = kk * a_f
    bdb_ref[...] = kk * a_b
    go_ref[...] = g
    bo_ref[...] = _head_sum(rv * (kd_f + kd_b) * pc_ref[12:13, :], ones_ref[...]) * vv


def _rwkv_prep(z, off_r, off_lo, C, pc, plo, wl, g2, ones_bd, lora_tanh_cols):
    Lp = z.shape[0]
    tr = BLOCK
    ct = next(c for c in (2048, 1024, 512) if C % c == 0 and off_r % c == 0)
    n_tiles = Lp // tr
    n8 = Lp // 8
    once = pl.Buffered(1) if ct == C else None
    off_gd = off_lo + LORA_PAD
    assert off_r % ct == 0 and off_lo % LORA_PAD == 0

    def seg_specs(off, w, with_c):
        cb = off // w

        def col(c):
            return cb + c if with_c else cb
        return [pl.BlockSpec((tr, w), lambda c, i: (i, col(c))),
                pl.BlockSpec((8, w), lambda c, i: ((i * (tr // 8) + n8 - 1) % n8, col(c))),
                pl.BlockSpec((8, w), lambda c, i: (((i + 1) * (tr // 8)) % n8, col(c)))]

    in_specs = (seg_specs(off_r, ct, True) + seg_specs(off_r + C, ct, True) + seg_specs(off_r + 2 * C, ct, True)
                + seg_specs(off_lo, LORA_PAD, False) + seg_specs(off_gd, LORA_PAD, False)
                + [pl.BlockSpec((16, ct), lambda c, i: (0, c)),
                   pl.BlockSpec((8, 2 * LORA_PAD), lambda c, i: (0, 0)),
                   pl.BlockSpec((4, LORA_PAD, ct), lambda c, i: (0, 0, c), pipeline_mode=once),
                   pl.BlockSpec((2, LORA_PAD, ct), lambda c, i: (0, 0, c), pipeline_mode=once),
                   pl.BlockSpec((LANES, LANES), lambda c, i: (0, 0))])
    out_spec = pl.BlockSpec((tr, ct), lambda c, i: (i, c))
    out_sds = jax.ShapeDtypeStruct((Lp, C), F32)
    return pl.pallas_call(
        functools.partial(_prep_kernel, n_tiles=n_tiles, tr=tr, lora_tanh_cols=lora_tanh_cols),
        grid=(C // ct, n_tiles),
        in_specs=in_specs,
        out_specs=[out_spec] * 11,
        out_shape=[out_sds] * 11,
        compiler_params=_params(("arbitrary", "arbitrary"),
                                _vmem_limit(3 * tr * ct * 4, 2 * tr * LORA_PAD * 4, 4 * LORA_PAD * ct * 2,
                                            2 * LORA_PAD * ct * 2, 11 * tr * ct * 4, temps=24 * tr * ct * 4)),
        name="rwkv_prep",
    )(*([z] * 15), pc, plo, wl, g2, ones_bd)


SCAN_PAIRS = 16


def _scan_consts():
    C = CHUNK
    row = lax.broadcasted_iota(jnp.int32, (C, 2 * C), 0)
    col = lax.broadcasted_iota(jnp.int32, (C, 2 * C), 1) % C
    rr = lax.broadcasted_iota(jnp.int32, (LANES, LANES), 0)
    cc = lax.broadcasted_iota(jnp.int32, (LANES, LANES), 1)
    return dict(
        row=lax.broadcasted_iota(jnp.int32, (C, LANES), 0),
        head0=lax.broadcasted_iota(jnp.int32, (C, LANES), 1) < RWKV_HEAD,
        strict_f=col < row, strict_b=col > row, incl_f=col <= row, incl_b=col >= row,
        eye_sbs=jnp.where(col == row, 1.0, 0.0).astype(F32),
        same_head=(rr // RWKV_HEAD) == (cc // RWKV_HEAD), eye=rr == cc)


def _scan_chains(chains, cst):
    C = CHUNK
    head0, same, eye = cst["head0"], cst["same_head"], cst["eye"]
    n = range(len(chains))
    rev = [c[7] for c in chains]
    r, v, al, lw, k, be, h = ([c[i] for c in chains] for i in range(7))
    strict = [cst["strict_b" if x else "strict_f"] for x in rev]
    incl = [cst["incl_b" if x else "incl_f"] for x in rev]

    def bd(x):
        xb = x.astype(BF16)
        zero = jnp.zeros_like(xb)
        return jnp.concatenate([jnp.where(head0, xb, zero), jnp.where(head0, zero, xb)], axis=0)

    def mm(a, b_bf16, dims=NN):
        return _dot(a.astype(BF16), b_bf16, dims)

    row = cst["row"]
    cl = list(lw)
    sh = 1
    while sh < C:
        cl = [cl[i] + (jnp.where(row < C - sh, pltpu.roll(cl[i], C - sh, 0), 0.0) if rev[i] else
                       jnp.where(row >= sh, pltpu.roll(cl[i], sh, 0), 0.0)) for i in n]
        sh *= 2
    total = [cl[i][0:1, :] if rev[i] else cl[i][C - 1:C, :] for i in n]
    a_t = [al[i] * jnp.exp(cl[i] - lw[i]) for i in n]
    r_t = [r[i] * jnp.exp(cl[i]) for i in n]
    w_inv = [jnp.exp(-cl[i]) for i in n]
    w_rest = [jnp.exp(total[i] - cl[i]) for i in n]
    bd_v = [bd(v[i]) for i in n]
    sc = [mm(jnp.concatenate([a_t[i], r_t[i]], axis=0),
             jnp.concatenate([bd(k[i] * w_inv[i]), bd(be[i] * w_inv[i])], axis=0), NT) for i in n]
    a_ak = [jnp.where(strict[i], sc[i][0:C, 0:2 * C], 0.0) for i in n]
    n_ab = [jnp.where(strict[i], sc[i][0:C, 2 * C:], 0.0) for i in n]
    a_rk = [jnp.where(incl[i], sc[i][C:, 0:2 * C], 0.0) for i in n]
    a_rb = [jnp.where(incl[i], sc[i][C:, 2 * C:], 0.0) for i in n]

    t_inv = [cst["eye_sbs"] - n_ab[i] for i in n]
    sq = [mm(n_ab[i], bd(n_ab[i])) for i in n]
    n_round = 2
    while (1 << n_round) < C:
        both = [mm(jnp.concatenate([t_inv[i], sq[i]], axis=0), bd(sq[i])) for i in n]
        t_inv = [t_inv[i] + both[i][0:C] for i in n]
        sq = [both[i][C:] for i in n]
        n_round += 1
    t_inv = [t_inv[i] + mm(t_inv[i], bd(sq[i])) for i in n]

    akv = [mm(jnp.concatenate([a_ak[i], a_rk[i]], axis=0), bd_v[i]) for i in n]
    tp = [mm(t_inv[i], jnp.concatenate([bd(akv[i][0:C]), bd(a_t[i])], axis=1)) for i in n]
    p0 = [tp[i][:, 0:LANES] for i in n]
    at = [tp[i][:, LANES:] for i in n]
    rb = [mm(a_rb[i], jnp.concatenate([bd(p0[i]), bd(at[i])], axis=1)) for i in n]
    y0 = [akv[i][C:] - rb[i][:, 0:LANES] for i in n]
    rh = [r_t[i] - rb[i][:, LANES:] for i in n]
    bt = [mm(be[i] * w_rest[i], jnp.concatenate([at[i], p0[i]], axis=1).astype(BF16), TN) for i in n]
    kv = [mm(k[i] * w_rest[i], v[i].astype(BF16), TN) for i in n]
    m_mat = [jnp.where(same, jnp.where(eye, jnp.broadcast_to(jnp.exp(total[i]), (LANES, LANES)), 0.0)
                       - bt[i][:, 0:LANES], 0.0) for i in n]
    g_mat = [jnp.where(same, kv[i] - bt[i][:, LANES:], 0.0) for i in n]
    yh = [mm(jnp.concatenate([rh[i], m_mat[i]], axis=0), h[i].astype(BF16)) for i in n]
    return [(yh[i][0:C] + y0[i], yh[i][C:] + g_mat[i]) for i in n]


def _scan_kernel(rf_ref, vf_ref, af_ref, lwf_ref, kf_ref, bf_ref,
                 rb_ref, vb_ref, ab_ref, lwb_ref, kb_ref, bb_ref, wd_ref,
                 yf_ref, yb_ref, wdo_ref, hf_ref, hb_ref, *, pairs, n_cast_in, cast_pads):
    @pl.when(pl.program_id(1) == 0)
    def _():
        hf_ref[...] = jnp.zeros(hf_ref.shape, F32)
        hb_ref[...] = jnp.zeros(hb_ref.shape, F32)

    if cast_pads:
        step = pl.program_id(0) * pl.num_programs(1) + pl.program_id(1)

        @pl.when(step < n_cast_in)
        def _():
            wdo_ref[...] = wd_ref[...].astype(wdo_ref.dtype)

        @pl.when(step >= n_cast_in)
        def _():
            wdo_ref[...] = jnp.zeros(wdo_ref.shape, wdo_ref.dtype)
    else:
        wdo_ref[...] = wd_ref[...].astype(wdo_ref.dtype)

    cst = _scan_consts()
    chains = []
    for g in range(pairs):
        sl = slice(g * LANES, (g + 1) * LANES)
        chains.append((rf_ref[:, sl], vf_ref[:, sl], af_ref[:, sl], lwf_ref[:, sl], kf_ref[:, sl], bf_ref[:, sl],
                       hf_ref[g], False))
        chains.append((rb_ref[:, sl], vb_ref[:, sl], ab_ref[:, sl], lwb_ref[:, sl], kb_ref[:, sl], bb_ref[:, sl],
                       hb_ref[g], True))
    res = _scan_chains(chains, cst)
    for g in range(pairs):
        sl = slice(g * LANES, (g + 1) * LANES)
        yf_ref[:, sl], hf_ref[g] = res[2 * g]
        yb_ref[:, sl], hb_ref[g] = res[2 * g + 1]


def _rwkv_scan(r, v, kk, lw_f, lw_b, kd_f, kd_b, bd_f, bd_b, w_cast, w_cast_rows, S):
    Lp, C = r.shape
    n_real = S // CHUNK
    n_chunks = Lp // CHUNK
    steps = n_real + 1
    pairs = _pick(C // LANES, (SCAN_PAIRS, 2, 1))
    w = pairs * LANES
    total_steps = (C // w) * steps
    wr, wc = w_cast.shape
    cast_rows = next(rb for rb in (128, 256, 512, 1024)
                     if wr % rb == 0 and w_cast_rows % rb == 0 and w_cast_rows // rb <= total_steps)
    n_cast_in, n_cast_out = wr // cast_rows, w_cast_rows // cast_rows
    cast_in = pl.BlockSpec((cast_rows, wc), lambda p, s: (jnp.minimum(p * steps + s, n_cast_in - 1), 0))
    cast_out = pl.BlockSpec((cast_rows, wc), lambda p, s: (jnp.minimum(p * steps + s, n_cast_out - 1), 0))

    def fwd(p, s):
        return ((s + n_real + 1) % n_chunks, p)

    def bwd(p, s):
        return (n_real - s, p)

    spec_f = pl.BlockSpec((CHUNK, w), fwd)
    spec_b = pl.BlockSpec((CHUNK, w), bwd)
    out_sds = jax.ShapeDtypeStruct((S + CHUNK, C), F32)
    out_f = pl.BlockSpec((CHUNK, w), lambda p, s: ((s + n_real) % (n_real + 1), p))
    return pl.pallas_call(
        functools.partial(_scan_kernel, pairs=pairs, n_cast_in=n_cast_in, cast_pads=n_cast_out > n_cast_in),
        grid=(C // w, steps),
        in_specs=[spec_f] * 6 + [spec_b] * 6 + [cast_in],
        out_specs=[out_f, spec_b, cast_out],
        out_shape=[out_sds, out_sds, jax.ShapeDtypeStruct((w_cast_rows, wc), BF16)],
        scratch_shapes=[pltpu.VMEM((pairs, LANES, LANES), F32), pltpu.VMEM((pairs, LANES, LANES), F32)],
        compiler_params=_params(("arbitrary", "arbitrary"),
                                _vmem_limit(14 * CHUNK * w * 4, cast_rows * wc * 6, temps=16 << 20)),
        name="rwkv_scan",
    )(r, v, kk, lw_f, kd_f, bd_f, r, v, kk, lw_b, kd_b, bd_b, w_cast)


def _post_kernel(yf_ref, yb_ref, bonus_ref, g_ref, pq_ref, ones_ref, o_ref):
    ones_bd = ones_ref[...]
    inv_n = 1.0 / RWKV_HEAD
    y = yf_ref[...] + yb_ref[...]
    mean = _head_sum(y, ones_bd) * inv_n
    yc = y - mean
    var = _head_sum(yc * yc, ones_bd) * inv_n
    yn = yc * lax.rsqrt(var + GN_EPS) * pq_ref[0:1, :] + pq_ref[1:2, :]
    o_ref[...] = ((yn + bonus_ref[...]) * g_ref[...]).astype(o_ref.dtype)


def _rwkv_post(y_f, y_b, bonus, g, pq, ones_bd, S):
    C = g.shape[1]
    tr = 512
    ct = _pick(C, (1024, 512))
    spec = pl.BlockSpec((tr, ct), lambda i, c: (i, c))
    return pl.pallas_call(
        _post_kernel,
        grid=(S // tr, C // ct),
        in_specs=[spec] * 4 + [pl.BlockSpec((8, ct), lambda i, c: (0, c)),
                               pl.BlockSpec((LANES, LANES), lambda i, c: (0, 0))],
        out_specs=spec,
        out_shape=jax.ShapeDtypeStruct((S, C), BF16),
        compiler_params=_params(("arbitrary", "arbitrary"),
                                _vmem_limit(4 * tr * ct * 4, tr * ct * 2, temps=16 * tr * ct * 4)),
        name="rwkv_post",
    )(y_f, y_b, bonus, g, pq, ones_bd)


IN_TN = 1024


def _matmul_tiles(S, D, F, Lp):
    glu_tn = 512
    f_pad = -(-F // glu_tn) * glu_tn
    return dict(
        in_tm=_pick(Lp, (1040, 640)),
        proj_tm=512, proj_tn=_pick(D, (1024, 512)),
        glu_tm=_pick(S, (1024, 512)), glu_tn=glu_tn, f_pad=f_pad,
        down_tm=_pick(S, (1024, 512)), down_tn=_pick(D, (1024, 512)), down_tk=_pick(f_pad, (2816, 1024, 768, 512)),
    )

def kernel(x, meta_tokens, norm_mix_w, w_in, b_gate, mu_prev, mu_next, dec_w0, dec_w2, iclr_a0, iclr_a2,
           gate_w2, k_k, k_a, r_k, ln_x_w, ln_x_b, attn_sink, w_proj_attn, w_proj_rwkv, w_out, norm_ffn_w,
           w_ffn_gate, w_ffn_up, w_ffn_down, norm_final_w):
    B, S, D = x.shape
    assert B == 1 and norm_mix_w.shape[0] == 1, "one sequence, one layer"
    C = k_k.shape[-1]
    AW = w_proj_attn.shape[1]
    n_heads = attn_sink.shape[-1]
    DL, IL, GL = dec_w2.shape[2], iclr_a2.shape[2], gate_w2.shape[1]
    shift_w = mu_prev.shape[-1]
    KVW = (w_in.shape[-1] - 2 * D - AW - shift_w) // 2
    n_kv = KVW // HEAD_DIM
    F = w_ffn_gate.shape[-1]
    lora_w = 2 * DL + 2 * IL
    assert shift_w == 3 * C + lora_w + GL and lora_w <= LORA_PAD and GL <= LORA_PAD
    assert S % BLOCK == 0 and C % LANES == 0 and n_heads * HEAD_DIM == AW
    Lp = S + PAD_ROWS

    off_q = 2 * D
    off_r = off_q + AW + 2 * KVW
    off_lo = off_r + 3 * C
    n_in_pad = -(-w_in.shape[-1] // IN_TN) * IN_TN
    assert n_in_pad >= off_lo + 2 * LORA_PAD and off_lo % LORA_PAD == 0
    t_r, t_lo = off_r // IN_TN, off_lo // IN_TN
    if off_r % IN_TN == 0 and off_lo % IN_TN == 0 and n_in_pad == off_lo + IN_TN:
        z_off_lo, z_off_r = off_r, off_r + IN_TN

        def z_col_tile(j):
            return jnp.where(j == t_lo, t_r, jnp.where(j >= t_r, j + 1, j))
    else:
        z_off_lo, z_off_r = off_lo, off_r

        def z_col_tile(j):
            return j
    low_w = lora_w + GL
    gl_a = LORA_PAD - lora_w

    mu_p, mu_n = mu_prev[0], mu_next[0]
    zc = jnp.zeros((C,), F32)
    pc = jnp.stack([mu_p[:C], mu_n[:C], mu_p[C:2 * C], mu_n[C:2 * C], mu_p[2 * C:3 * C], mu_n[2 * C:3 * C],
                    dec_w0[0, 0], dec_w0[0, 1], iclr_a0[0, 0], iclr_a0[0, 1], k_k[0], k_a[0], r_k[0], zc, zc, zc])
    plo = jnp.zeros((8, 2 * LORA_PAD), F32).at[0, :low_w].set(mu_p[3 * C:]).at[1, :low_w].set(mu_n[3 * C:])
    wl = jnp.zeros((4, LORA_PAD, C), F32)
    wl = wl.at[0, 0:DL].set(dec_w2[0, 0]).at[1, DL:2 * DL].set(dec_w2[0, 1])
    wl = wl.at[2, 2 * DL:2 * DL + IL].set(iclr_a2[0, 0]).at[3, 2 * DL + IL:lora_w].set(iclr_a2[0, 1])
    wl = wl.astype(BF16)
    g2 = jnp.zeros((2, LORA_PAD, C), F32).at[0, lora_w:].set(gate_w2[0, :gl_a]).at[1, :GL - gl_a].set(
        gate_w2[0, gl_a:]).astype(BF16)
    pq = jnp.zeros((8, C), F32).at[0].set(ln_x_w[0]).at[1].set(ln_x_b[0])
    lane = jnp.arange(LANES)
    ones_bd = (lane[:, None] // RWKV_HEAD == lane[None, :] // RWKV_HEAD).astype(BF16)


    pos = jnp.concatenate([jnp.arange(N_META, N_META + S), jnp.zeros((PAD_ROWS - N_META,), jnp.int32),
                           jnp.arange(N_META)]).astype(F32)
    inv = ROPE_THETA ** (-jnp.arange(0, ROPE_DIMS, 2, dtype=F32) / ROPE_DIMS)
    ang = pos[:, None] * inv[None, :]
    ones_rest = jnp.ones((Lp, HEAD_DIM - ROPE_DIMS), F32)
    cos_t = jnp.concatenate([jnp.cos(ang), jnp.cos(ang), ones_rest], axis=1)
    sin_t = jnp.concatenate([-jnp.sin(ang), jnp.sin(ang), 0.0 * ones_rest], axis=1)

    t = _matmul_tiles(S, D, F, Lp)
    x2 = x[0]
    h_ext = _norm_ext(x2, meta_tokens.astype(x.dtype), norm_mix_w[0])
    z = _mm_in(h_ext, w_in[0], t["in_tm"], IN_TN, z_col_tile)
    o_att = _attention(z, cos_t, sin_t, attn_sink[0], S, n_heads, n_kv, off_q)
    r_s, v_s, kk, lw_f, lw_b, kd_f, kd_b, bd_f, bd_b, g, bonus = _rwkv_prep(
        z, z_off_r, z_off_lo, C, pc, plo, wl, g2, ones_bd, 2 * DL)
    y_f, y_b, w_d = _rwkv_scan(r_s, v_s, kk, lw_f, lw_b, kd_f, kd_b, bd_f, bd_b, w_ffn_down[0], t["f_pad"], S)
    o_rwkv = _rwkv_post(y_f, y_b, bonus, g, pq, ones_bd, S)
    mixed = _mm_mix(o_att, w_proj_attn[0], o_rwkv, w_proj_rwkv[0], z, b_gate[0], S, D, t["proj_tm"], t["proj_tn"])
    h1, h1g, h1ss = _mm_out(mixed, w_out[0], x2, norm_ffn_w[0], t["proj_tm"], t["proj_tn"])
    act = _mm_glu(h1g, h1ss, w_ffn_gate[0], w_ffn_up[0], t["glu_tm"], t["glu_tn"])
    h2 = _mm_res(act, w_d, h1, t["down_tm"], t["down_tn"], t["down_tk"])
    y = _norm(h2, norm_final_w, x.dtype)
    return y[None]
```
